```python
import jax, jax.numpy as jnp
from jax import lax
import numpy as np

D_MODEL = 1024
BATCH = 8
SEQ = 2048
DEPTH = 1

HEAD_DIM = 64
N_Q_HEADS = 8
N_KV_HEADS = 2
ATTN_WIDTH = N_Q_HEADS * HEAD_DIM
KV_WIDTH = N_KV_HEADS * HEAD_DIM
CONV_WIDTH = D_MODEL - ATTN_WIDTH
MIX_WIDTH = ATTN_WIDTH + CONV_WIDTH
CONV_KERNEL = 31
WINDOW = 128
BLOCK = 128
ROPE_THETA = 10000.0
EPS = 1e-6
IN_WIDTH = ATTN_WIDTH + 2 * KV_WIDTH + ATTN_WIDTH + 2 * CONV_WIDTH + CONV_WIDTH

kernel_name = "hybrid_swa_sink_conformer_adaln"


def _rms_norm(t, w):
    tf = t.astype(jnp.float32)
    y = tf * lax.rsqrt(jnp.mean(tf * tf, axis=-1, keepdims=True) + EPS)
    return (y * w.astype(jnp.float32)).astype(t.dtype)


def _layer_norm(t, w, b):
    tf = t.astype(jnp.float32)
    mu = jnp.mean(tf, axis=-1, keepdims=True)
    var = jnp.mean(jnp.square(tf - mu), axis=-1, keepdims=True)
    y = (tf - mu) * lax.rsqrt(var + EPS)
    return (y * w.astype(jnp.float32) + b.astype(jnp.float32)).astype(t.dtype)


def _rope_tables(seq_len):
    inv = ROPE_THETA ** (-jnp.arange(0, HEAD_DIM, 2, dtype=jnp.float32) / HEAD_DIM)
    ang = jnp.arange(seq_len, dtype=jnp.float32)[:, None] * inv[None, :]
    return jnp.cos(ang), jnp.sin(ang)


def _apply_rope(t, cos, sin):
    tf = t.astype(jnp.float32)
    t1, t2 = jnp.split(tf, 2, axis=-1)
    c_, s_ = cos[None, :, None, :], sin[None, :, None, :]
    return jnp.concatenate([t1 * c_ - t2 * s_, t2 * c_ + t1 * s_], axis=-1).astype(t.dtype)


def _sliding_window_sink_attention(q, k, v, sinks):
    B, S, _, dh = q.shape
    nb = S // BLOCK
    G = N_Q_HEADS // N_KV_HEADS
    qb = q.reshape(B, nb, BLOCK, N_KV_HEADS, G, dh)

    def band(t):
        tb = t.reshape(B, nb, BLOCK, N_KV_HEADS, dh)
        prev = jnp.concatenate([jnp.zeros_like(tb[:, :1]), tb[:, :-1]], axis=1)
        return jnp.concatenate([prev, tb], axis=2)

    kb, vb = band(k), band(v)
    s = jnp.einsum('bnqhgd,bnkhd->bnhgqk', qb, kb,
                   preferred_element_type=jnp.float32) * (dh ** -0.5)
    qi = jnp.arange(BLOCK)[:, None]
    kj = jnp.arange(2 * BLOCK)[None, :]
    dist = qi + BLOCK - kj
    local = (dist >= 0) & (dist < WINDOW)
    exists = (jnp.arange(nb)[:, None, None] > 0) | (kj[None] >= BLOCK)
    mask = local[None] & exists
    s = jnp.where(mask[None, :, None, None], s, -jnp.inf)
    sink = sinks.astype(jnp.float32).reshape(1, 1, N_KV_HEADS, G, 1, 1)
    m = jnp.maximum(jnp.max(s, axis=-1, keepdims=True), sink)
    p = jnp.exp(s - m)
    p = p / (jnp.sum(p, axis=-1, keepdims=True) + jnp.exp(sink - m))
    o = jnp.einsum('bnhgqk,bnkhd->bnqhgd', p.astype(v.dtype), vb)
    return o.reshape(B, S, N_Q_HEADS * dh)


def _conformer_conv(u, conv_w, conv_b, ln_w, ln_b):
    a, g = jnp.split(u, 2, axis=-1)
    z = a * jax.nn.sigmoid(g)
    z = lax.conv_general_dilated(
        z, conv_w[:, None, :], window_strides=(1,),
        padding=[(CONV_KERNEL - 1, 0)],
        dimension_numbers=('NWC', 'WIO', 'NWC'),
        feature_group_count=CONV_WIDTH) + conv_b
    z = _layer_norm(z, ln_w, ln_b)
    return jax.nn.silu(z)


def _fwd_setup_inputs(seed: int = 0) -> dict:
    key = jax.random.key(seed)
    ks = jax.random.split(key, 16)
    f32 = jnp.float32
    n = lambda k, shape, s: jax.random.normal(k, shape, f32) * s
    return {
        "x": n(ks[0], (BATCH, SEQ, D_MODEL), 1.0),
        "c": n(ks[1], (BATCH, D_MODEL), 1.0),
        "w_ada": n(ks[2], (DEPTH, D_MODEL, 3 * D_MODEL), 0.5 * D_MODEL ** -0.5),
        "b_ada": n(ks[3], (DEPTH, 3 * D_MODEL), 0.02),
        "norm_w": 1.0 + n(ks[4], (DEPTH, D_MODEL), 0.02),
        "w_in": n(ks[5], (DEPTH, D_MODEL, IN_WIDTH), D_MODEL ** -0.5),
        "q_norm_w": 1.0 + n(ks[6], (DEPTH, HEAD_DIM), 0.02),
        "k_norm_w": 1.0 + n(ks[7], (DEPTH, HEAD_DIM), 0.02),
        "sinks": n(ks[8], (DEPTH, N_Q_HEADS), 0.5),
        "conv_w": n(ks[9], (DEPTH, CONV_KERNEL, CONV_WIDTH), CONV_KERNEL ** -0.5),
        "conv_b": n(ks[10], (DEPTH, CONV_WIDTH), 0.02),
        "ln_w": 1.0 + n(ks[11], (DEPTH, CONV_WIDTH), 0.02),
        "ln_b": n(ks[12], (DEPTH, CONV_WIDTH), 0.02),
        "w_out": n(ks[13], (DEPTH, MIX_WIDTH, D_MODEL), MIX_WIDTH ** -0.5),
    }


def _fwd_reference(x, c, w_ada, b_ada, norm_w, w_in, q_norm_w, k_norm_w, sinks,
              conv_w, conv_b, ln_w, ln_b, w_out):
    B, S, _ = x.shape
    cos, sin = _rope_tables(S)
    c_act = jax.nn.silu(c)
    splits = (ATTN_WIDTH,
              ATTN_WIDTH + KV_WIDTH,
              ATTN_WIDTH + 2 * KV_WIDTH,
              2 * ATTN_WIDTH + 2 * KV_WIDTH,
              2 * ATTN_WIDTH + 2 * KV_WIDTH + 2 * CONV_WIDTH)
    for l in range(DEPTH):
        shift, scale, gate = jnp.split(c_act @ w_ada[l] + b_ada[l], 3, axis=-1)
        h = _rms_norm(x, norm_w[l]) * (1.0 + scale[:, None, :]) + shift[:, None, :]
        proj = h @ w_in[l]
        q, k, v, gate_a, u, gate_b = jnp.split(proj, splits, axis=-1)
        q = q.reshape(B, S, N_Q_HEADS, HEAD_DIM)
        k = k.reshape(B, S, N_KV_HEADS, HEAD_DIM)
        v = v.reshape(B, S, N_KV_HEADS, HEAD_DIM)
        q = _apply_rope(_rms_norm(q, q_norm_w[l]), cos, sin)
        k = _apply_rope(_rms_norm(k, k_norm_w[l]), cos, sin)
        y_a = _sliding_window_sink_attention(q, k, v, sinks[l]) * jax.nn.silu(gate_a)
        y_b = _conformer_conv(u, conv_w[l], conv_b[l], ln_w[l], ln_b[l]) * jax.nn.silu(gate_b)
        y = jnp.concatenate([y_a, y_b], axis=-1) @ w_out[l]
        x = x + gate[:, None, :] * y
    return x


import jax as _jax
import jax.numpy as _jnp

TWIN_FORMAT = 'train_step'
FWD_PARAMS = ['x', 'c', 'w_ada', 'b_ada', 'norm_w', 'w_in', 'q_norm_w', 'k_norm_w', 'sinks', 'conv_w', 'conv_b', 'ln_w', 'ln_b', 'w_out']
TWIN_WEIGHTS = ['w_ada', 'b_ada', 'norm_w', 'w_in', 'q_norm_w', 'k_norm_w', 'sinks', 'conv_w', 'conv_b', 'ln_w', 'ln_b', 'w_out']
TWIN_DIFF_INPUT = 'x'
TWIN_INPUTS = ['x', 'c', 'w_ada', 'b_ada', 'norm_w', 'w_in', 'q_norm_w', 'k_norm_w', 'sinks', 'conv_w', 'conv_b', 'ln_w', 'ln_b', 'w_out', 'loss_target', 'm_w_ada', 'm_b_ada', 'm_norm_w', 'm_w_in', 'm_q_norm_w', 'm_k_norm_w', 'm_sinks', 'm_conv_w', 'm_conv_b', 'm_ln_w', 'm_ln_b', 'm_w_out', 'v_w_ada', 'v_b_ada', 'v_norm_w', 'v_w_in', 'v_q_norm_w', 'v_k_norm_w', 'v_sinks', 'v_conv_w', 'v_conv_b', 'v_ln_w', 'v_ln_b', 'v_w_out']
TWIN_OUTPUTS = ['loss', 'grad_x', 'grad_w_ada', 'grad_b_ada', 'grad_norm_w', 'grad_w_in', 'grad_q_norm_w', 'grad_k_norm_w', 'grad_sinks', 'grad_conv_w', 'grad_conv_b', 'grad_ln_w', 'grad_ln_b', 'grad_w_out', 'delta_w_ada', 'delta_b_ada', 'delta_norm_w', 'delta_w_in', 'delta_q_norm_w', 'delta_k_norm_w', 'delta_sinks', 'delta_conv_w', 'delta_conv_b', 'delta_ln_w', 'delta_ln_b', 'delta_w_out', 'new_m_w_ada', 'new_m_b_ada', 'new_m_norm_w', 'new_m_w_in', 'new_m_q_norm_w', 'new_m_k_norm_w', 'new_m_sinks', 'new_m_conv_w', 'new_m_conv_b', 'new_m_ln_w', 'new_m_ln_b', 'new_m_w_out', 'new_v_w_ada', 'new_v_b_ada', 'new_v_norm_w', 'new_v_w_in', 'new_v_q_norm_w', 'new_v_k_norm_w', 'new_v_sinks', 'new_v_conv_w', 'new_v_conv_b', 'new_v_ln_w', 'new_v_ln_b', 'new_v_w_out']
TWIN_LEAF_KINDS = {'loss': 'loss', 'grad_x': 'grad_x', 'grad_w_ada': 'grad_w', 'grad_b_ada': 'grad_w', 'grad_norm_w': 'grad_w', 'grad_w_in': 'grad_w', 'grad_q_norm_w': 'grad_w', 'grad_k_norm_w': 'grad_w', 'grad_sinks': 'grad_w', 'grad_conv_w': 'grad_w', 'grad_conv_b': 'grad_w', 'grad_ln_w': 'grad_w', 'grad_ln_b': 'grad_w', 'grad_w_out': 'grad_w', 'delta_w_ada': 'delta_w', 'delta_b_ada': 'delta_w', 'delta_norm_w': 'delta_w', 'delta_w_in': 'delta_w', 'delta_q_norm_w': 'delta_w', 'delta_k_norm_w': 'delta_w', 'delta_sinks': 'delta_w', 'delta_conv_w': 'delta_w', 'delta_conv_b': 'delta_w', 'delta_ln_w': 'delta_w', 'delta_ln_b': 'delta_w', 'delta_w_out': 'delta_w', 'new_m_w_ada': 'new_m', 'new_m_b_ada': 'new_m', 'new_m_norm_w': 'new_m', 'new_m_w_in': 'new_m', 'new_m_q_norm_w': 'new_m', 'new_m_k_norm_w': 'new_m', 'new_m_sinks': 'new_m', 'new_m_conv_w': 'new_m', 'new_m_conv_b': 'new_m', 'new_m_ln_w': 'new_m', 'new_m_ln_b': 'new_m', 'new_m_w_out': 'new_m', 'new_v_w_ada': 'new_v', 'new_v_b_ada': 'new_v', 'new_v_norm_w': 'new_v', 'new_v_w_in': 'new_v', 'new_v_q_norm_w': 'new_v', 'new_v_k_norm_w': 'new_v', 'new_v_sinks': 'new_v', 'new_v_conv_w': 'new_v', 'new_v_conv_b': 'new_v', 'new_v_ln_w': 'new_v', 'new_v_ln_b': 'new_v', 'new_v_w_out': 'new_v'}


def _forward(args):
    return _fwd_reference(*[args[k] for k in FWD_PARAMS])


def _output_shape():
    out = _jax.eval_shape(lambda: _forward(_fwd_setup_inputs(0)))
    return out.shape, out.dtype

N_MICROBATCH = 1
ADAM_LR = 0.001
ADAM_B1 = 0.9
ADAM_B2 = 0.999
ADAM_EPS = 1e-08
ADAM_WD = 0.01
ADAM_STEP = 10
PER_EXAMPLE_BATCH_AXIS = {'x': 0, 'c': 0, 'loss_target': 0}
SHARED_INPUTS = []
_WEIGHT_DTYPES = {'w_ada': _jnp.float32, 'b_ada': _jnp.float32, 'norm_w': _jnp.float32, 'w_in': _jnp.float32, 'q_norm_w': _jnp.float32, 'k_norm_w': _jnp.float32, 'sinks': _jnp.float32, 'conv_w': _jnp.float32, 'conv_b': _jnp.float32, 'ln_w': _jnp.float32, 'ln_b': _jnp.float32, 'w_out': _jnp.float32}
MOMENT_SCALE = {'w_ada': 1.179445e-01, 'b_ada': 2.371860e-01, 'norm_w': 1.725853e-01, 'w_in': 3.463971e-02, 'q_norm_w': 6.211507e-02, 'k_norm_w': 6.089331e-02, 'sinks': 3.499647e-02, 'conv_w': 3.023461e-02, 'conv_b': 9.670767e-02, 'ln_w': 3.047978e-01, 'ln_b': 1.888638e-01, 'w_out': 2.162508e-02}


def _to_microbatches(a, axis):
    t = _jnp.moveaxis(a, axis, 0)
    t = t.reshape((N_MICROBATCH, t.shape[0] // N_MICROBATCH) + t.shape[1:])
    return _jnp.moveaxis(t, 1, axis + 1)


def setup_inputs(seed: int = 0) -> dict:
    inp = _fwd_setup_inputs(seed)
    key = _jax.random.fold_in(_jax.random.key(seed), 7919)
    shape, _ = _output_shape()
    out = dict(inp)
    out["loss_target"] = _jax.random.normal(_jax.random.fold_in(key, 0), shape, _jnp.float32)
    for i, name in enumerate(TWIN_WEIGHTS):
        w = inp[name].astype(_jnp.float32)
        if MOMENT_SCALE is None:
            s = _jnp.sqrt(_jnp.mean(_jnp.square(w)) + 1e-30)
        else:
            s = MOMENT_SCALE[name]
        km, kv = _jax.random.split(_jax.random.fold_in(key, i + 1))
        out[name] = w
        out["m_" + name] = s * _jax.random.normal(km, w.shape, _jnp.float32)
        out["v_" + name] = (s * s) * _jax.random.uniform(kv, w.shape, _jnp.float32, 0.5, 1.5)
    if N_MICROBATCH > 1:
        for name, axis in PER_EXAMPLE_BATCH_AXIS.items():
            out[name] = _to_microbatches(out[name], axis)
    return {'x': out['x'], 'c': out['c'], 'w_ada': out['w_ada'], 'b_ada': out['b_ada'], 'norm_w': out['norm_w'], 'w_in': out['w_in'], 'q_norm_w': out['q_norm_w'], 'k_norm_w': out['k_norm_w'], 'sinks': out['sinks'], 'conv_w': out['conv_w'], 'conv_b': out['conv_b'], 'ln_w': out['ln_w'], 'ln_b': out['ln_b'], 'w_out': out['w_out'], 'loss_target': out['loss_target'], 'm_w_ada': out['m_w_ada'], 'm_b_ada': out['m_b_ada'], 'm_norm_w': out['m_norm_w'], 'm_w_in': out['m_w_in'], 'm_q_norm_w': out['m_q_norm_w'], 'm_k_norm_w': out['m_k_norm_w'], 'm_sinks': out['m_sinks'], 'm_conv_w': out['m_conv_w'], 'm_conv_b': out['m_conv_b'], 'm_ln_w': out['m_ln_w'], 'm_ln_b': out['m_ln_b'], 'm_w_out': out['m_w_out'], 'v_w_ada': out['v_w_ada'], 'v_b_ada': out['v_b_ada'], 'v_norm_w': out['v_norm_w'], 'v_w_in': out['v_w_in'], 'v_q_norm_w': out['v_q_norm_w'], 'v_k_norm_w': out['v_k_norm_w'], 'v_sinks': out['v_sinks'], 'v_conv_w': out['v_conv_w'], 'v_conv_b': out['v_conv_b'], 'v_ln_w': out['v_ln_w'], 'v_ln_b': out['v_ln_b'], 'v_w_out': out['v_w_out']}


def _loss(weights, diff, rest, loss_target):
    with _jax.named_scope("forward"):
        args = {**rest, TWIN_DIFF_INPUT: diff, **{k: w.astype(_WEIGHT_DTYPES[k]) for k, w in weights.items()}}
        y = _forward(args)
    with _jax.named_scope("loss_head"):
        err = _jnp.square(y.astype(_jnp.float32) - loss_target)
        return 0.5 * _jnp.sum(_jnp.mean(err, axis=-1)) if err.ndim else 0.5 * err


def _adamw(w, g, m, v):
    m = ADAM_B1 * m + (1.0 - ADAM_B1) * g
    v = ADAM_B2 * v + (1.0 - ADAM_B2) * _jnp.square(g)
    m_hat = m / (1.0 - ADAM_B1 ** ADAM_STEP)
    v_hat = v / (1.0 - ADAM_B2 ** ADAM_STEP)
    delta = -ADAM_LR * (m_hat / (_jnp.sqrt(v_hat) + ADAM_EPS) + ADAM_WD * w)
    return delta, m, v


def reference(x, c, w_ada, b_ada, norm_w, w_in, q_norm_w, k_norm_w, sinks, conv_w, conv_b, ln_w, ln_b, w_out, loss_target, m_w_ada, m_b_ada, m_norm_w, m_w_in, m_q_norm_w, m_k_norm_w, m_sinks, m_conv_w, m_conv_b, m_ln_w, m_ln_b, m_w_out, v_w_ada, v_b_ada, v_norm_w, v_w_in, v_q_norm_w, v_k_norm_w, v_sinks, v_conv_w, v_conv_b, v_ln_w, v_ln_b, v_w_out):
    given = dict(x=x, c=c, w_ada=w_ada, b_ada=b_ada, norm_w=norm_w, w_in=w_in, q_norm_w=q_norm_w, k_norm_w=k_norm_w, sinks=sinks, conv_w=conv_w, conv_b=conv_b, ln_w=ln_w, ln_b=ln_b, w_out=w_out, loss_target=loss_target, m_w_ada=m_w_ada, m_b_ada=m_b_ada, m_norm_w=m_norm_w, m_w_in=m_w_in, m_q_norm_w=m_q_norm_w, m_k_norm_w=m_k_norm_w, m_sinks=m_sinks, m_conv_w=m_conv_w, m_conv_b=m_conv_b, m_ln_w=m_ln_w, m_ln_b=m_ln_b, m_w_out=m_w_out, v_w_ada=v_w_ada, v_b_ada=v_b_ada, v_norm_w=v_norm_w, v_w_in=v_w_in, v_q_norm_w=v_q_norm_w, v_k_norm_w=v_k_norm_w, v_sinks=v_sinks, v_conv_w=v_conv_w, v_conv_b=v_conv_b, v_ln_w=v_ln_w, v_ln_b=v_ln_b, v_w_out=v_w_out)
    weights = {n: given[n] for n in TWIN_WEIGHTS}
    shared = {n: given[n] for n in SHARED_INPUTS}
    per_example = {n: given[n] for n in ['x', 'c']}
    grad_fn = _jax.value_and_grad(_loss, argnums=(0, 1))

    def one_microbatch(ex, loss_target):
        ex = dict(ex)
        diff = ex.pop(TWIN_DIFF_INPUT)
        return grad_fn(weights, diff, {**shared, **ex}, loss_target)

    if N_MICROBATCH == 1:
        loss, (grad_w, grad_x) = one_microbatch(per_example, given["loss_target"])
    else:
        def body(carry, xs):
            loss_sum, grad_sum = carry
            l_k, (gw_k, gx_k) = one_microbatch(xs[0], xs[1])
            with _jax.named_scope("update"):
                return (loss_sum + l_k, _jax.tree.map(_jnp.add, grad_sum, gw_k)), gx_k

        init = (_jnp.zeros((), _jnp.float32), _jax.tree.map(_jnp.zeros_like, weights))
        (loss, grad_w), grad_x = _jax.lax.scan(body, init, (per_example, given["loss_target"]))
    with _jax.named_scope("update"):
        delta_w, new_m, new_v = {}, {}, {}
        for n in TWIN_WEIGHTS:
            delta_w[n], new_m[n], new_v[n] = _adamw(weights[n], grad_w[n], given["m_" + n], given["v_" + n])
    return (loss, grad_x, *[grad_w[n] for n in TWIN_WEIGHTS], *[delta_w[n] for n in TWIN_WEIGHTS],
            *[new_m[n] for n in TWIN_WEIGHTS], *[new_v[n] for n in TWIN_WEIGHTS])
```

```python
import functools

import jax
import jax.numpy as jnp
from jax import lax
from jax.experimental import pallas as pl
from jax.experimental.pallas import tpu as pltpu

S = 2048
D = 1024
NDEV = 8
HD = 64
NQ = 8
NKV = 2
AW = 512
KVW = 128
CW = 512
INW = 2816
IN_SHARD = INW // NDEV
ADA_SHARD = 3 * D // NDEV
OUT_SHARD = D // NDEV
CONV_SHARD = CW // NDEV
CK = 31
CKP = 32
BLK = 128
TS = 256
NT = S // TS
NB = S // BLK
EPS = 1e-6
ROPE_THETA = 10000.0
NEG = -1e30
BF = jnp.bfloat16
F32 = jnp.float32

ADAM_LR = 0.001
ADAM_B1 = 0.9
ADAM_B2 = 0.999
ADAM_EPS = 1e-08
ADAM_WD = 0.01
ADAM_STEP = 10

VMEM_LIMIT = 56 * 1024 * 1024
MESH = pl.DeviceIdType.MESH

_VMEM = pl.BlockSpec(memory_space=pltpu.VMEM)
_SMEM = pl.BlockSpec(memory_space=pltpu.SMEM)


def _params(grid=False):
    if grid:
        return pltpu.CompilerParams(dimension_semantics=("arbitrary",), vmem_limit_bytes=VMEM_LIMIT)
    return pltpu.CompilerParams(vmem_limit_bytes=VMEM_LIMIT)


def _row(i):
    return (i, 0)


def _const(i):
    return (0, 0)


def _sigmoid(t):
    return 1.0 / (1.0 + jnp.exp(-t))


def _silu_and_grad(t):
    sg = _sigmoid(t)
    return t * sg, sg * (1.0 + t * (1.0 - sg))


def _group_mean(t, b_ref):
    hi = t.astype(BF)
    lo = (t - hi.astype(F32)).astype(BF)
    b = b_ref[...]
    return jnp.dot(hi, b, preferred_element_type=F32) + jnp.dot(lo, b, preferred_element_type=F32)


def _partner(t):
    w = t.shape[-1]
    lane = lax.broadcasted_iota(jnp.int32, t.shape, 1)
    first = (lane & 32) == 0
    return jnp.where(first, pltpu.roll(t, w - 32, 1), pltpu.roll(t, 32, 1))


def _norm_rope_fwd(t, w_t, cos, sin, b_ref):
    r = lax.rsqrt(_group_mean(t * t, b_ref) + EPS)
    tn = t * r * w_t
    return tn * cos + _partner(tn) * sin


def _norm_rope_bwd(d_out, t, w_t, cos, sin, b_ref):
    d_tn = d_out * cos + _partner(d_out * sin)
    r = lax.rsqrt(_group_mean(t * t, b_ref) + EPS)
    th = t * r
    g_w = jnp.sum(d_tn * th, axis=0, keepdims=True)
    d_th = d_tn * w_t
    d_t = r * (d_th - th * _group_mean(d_th * th, b_ref))
    return d_t, g_w


def _mesh_pos():
    return lax.axis_index("x"), lax.axis_index("y"), lax.axis_index("c")


def _ag_copy(chan, k, block, to):
    blk, send_sems, recv_sems = chan
    ref = blk(*block)
    return pltpu.make_async_remote_copy(src_ref=ref, dst_ref=ref, send_sem=send_sems.at[k],
                                        recv_sem=recv_sems.at[k], device_id=to, device_id_type=MESH)


def _ag_start(chan, pos):
    x, y, c = pos
    me = (x, y, c)
    chips = [(1 - x, y), (x, 1 - y), (1 - x, 1 - y)]
    first = [_ag_copy(chan, 0, me, (x, y, 1 - c))]
    first += [_ag_copy(chan, 1 + j, me, (*chip, c)) for j, chip in enumerate(chips)]
    for cp in first:
        cp.start()
    return first


def _ag_finish(chan, pos, first):
    x, y, c = pos
    me = (x, y, c)
    sibling = (x, y, 1 - c)
    chips = [(1 - x, y), (x, 1 - y), (1 - x, 1 - y)]
    passed = [_ag_copy(chan, 4 + j, (*chip, c), sibling) for j, chip in enumerate(chips)]
    for j, chip in enumerate(chips):
        _ag_copy(chan, 1 + j, (*chip, c), me).wait_recv()
        passed[j].start()
    _ag_copy(chan, 0, sibling, me).wait_recv()
    for j, chip in enumerate(chips):
        _ag_copy(chan, 4 + j, (*chip, 1 - c), me).wait_recv()
    for cp in first + passed:
        cp.wait_send()


def _slab(buf):
    return lambda px, py, pc: buf.at[4 * px + 2 * py + pc]


def _xor_peer(pos, k):
    x, y, c = pos
    kx, ky, kc = (k >> 2) & 1, (k >> 1) & 1, k & 1
    return (1 - x if kx else x, 1 - y if ky else y, 1 - c if kc else c)


def _gather_call(w_in_s, w_out_s, conv_w_s, c, w_ada_s, b_ada):
    def body(win_ref, wout_ref, cw_ref, c_ref, wada_ref, bada_ref,
             winf_ref, woutf_ref, cwf_ref, mod_ref, cact_ref,
             win_buf, wout_buf, cw_buf, ca_buf, mp_buf,
             s0, r0, s1, r1, s2, r2, s3, r3, s4, r4):
        pos = _mesh_pos()
        x, y, cc = pos
        me = 4 * x + 2 * y + cc
        ch_win = (_slab(win_buf), s0, r0)
        ch_wout = (_slab(wout_buf), s1, r1)
        ch_cw = (_slab(cw_buf), s2, r2)
        ch_ca = (_slab(ca_buf), s3, r3)
        ch_mp = (_slab(mp_buf), s4, r4)

        cv = c_ref[...]
        c_act = cv * _sigmoid(cv)
        ca_buf[me] = jnp.broadcast_to(c_act, (8, D))
        f_ca = _ag_start(ch_ca, pos)
        win_buf[me] = win_ref[...].astype(BF)
        f_win = _ag_start(ch_win, pos)
        wout_buf[me] = wout_ref[...].astype(BF)
        f_wout = _ag_start(ch_wout, pos)
        cw_buf[me] = cw_ref[...]
        f_cw = _ag_start(ch_cw, pos)

        _ag_finish(ch_ca, pos, f_ca)
        cact_all = jnp.concatenate([ca_buf[d, 0:1, :] for d in range(NDEV)], axis=0)
        cact_ref[...] = cact_all
        col0 = pl.multiple_of(me * ADA_SHARD, 128)
        mp = jnp.dot(cact_all, wada_ref[...], preferred_element_type=F32,
                     precision=lax.Precision.HIGHEST) + bada_ref[:, pl.ds(col0, ADA_SHARD)]
        mp_buf[me] = mp
        f_mp = _ag_start(ch_mp, pos)

        _ag_finish(ch_win, pos, f_win)
        _ag_finish(ch_wout, pos, f_wout)
        _ag_finish(ch_cw, pos, f_cw)
        _ag_finish(ch_mp, pos, f_mp)

        for d in range(NDEV):
            winf_ref[:, IN_SHARD * d:IN_SHARD * (d + 1)] = win_buf[d]
            woutf_ref[OUT_SHARD * d:OUT_SHARD * (d + 1), :] = wout_buf[d]
            cwf_ref[0:CK, CONV_SHARD * d:CONV_SHARD * (d + 1)] = cw_buf[d]
        cwf_ref[CK:CKP, :] = jnp.zeros((CKP - CK, CW), F32)
        mod_ref[...] = jnp.concatenate([mp_buf[d, pl.ds(me, 1), :] for d in range(NDEV)], axis=1)

    sem = pltpu.SemaphoreType.DMA((7,))
    return pl.pallas_call(
        body, name="gather",
        out_shape=[jax.ShapeDtypeStruct((D, INW), BF), jax.ShapeDtypeStruct((D, D), BF),
                   jax.ShapeDtypeStruct((CKP, CW), F32), jax.ShapeDtypeStruct((1, 3 * D), F32),
                   jax.ShapeDtypeStruct((NDEV, D), F32)],
        in_specs=[_VMEM] * 6, out_specs=[_VMEM] * 5,
        scratch_shapes=[pltpu.VMEM((NDEV, D, IN_SHARD), BF), pltpu.VMEM((NDEV, OUT_SHARD, D), BF),
                        pltpu.VMEM((NDEV, CK, CONV_SHARD), F32), pltpu.VMEM((NDEV, 8, D), F32),
                        pltpu.VMEM((NDEV, 8, ADA_SHARD), F32)] + [sem] * 10,
        compiler_params=_params(),
    )(w_in_s, w_out_s, conv_w_s, c, w_ada_s, b_ada)


def _fwd_in_call(x2, mod, norm_w, qw_t, kw_t, cos_t, sin_t, bq, bk, w_in_full):
    def body(x_ref, mod_ref, nw_ref, qw_ref, kw_ref, cos_ref, sin_ref, bq_ref, bk_ref, w_ref,
             h_ref, qraw_ref, kraw_ref, ga_ref, a_ref, g_ref, gb_ref, qr_ref, kr_ref, vb_ref, z_ref):
        xv = x_ref[...]
        shift = mod_ref[:, 0:D]
        scale = mod_ref[:, D:2 * D]
        r = lax.rsqrt(jnp.mean(xv * xv, axis=-1, keepdims=True) + EPS)
        h = (xv * r * nw_ref[...]) * (1.0 + scale) + shift
        hb = h.astype(BF)
        h_ref[...] = hb

        def proj(lo, hi):
            return jnp.dot(hb, w_ref[:, lo:hi], preferred_element_type=F32)

        cos = cos_ref[...]
        sin = sin_ref[...]
        q = proj(0, 512)
        qraw_ref[...] = q
        qr_ref[...] = _norm_rope_fwd(q, qw_ref[...], jnp.tile(cos, (1, 4)), jnp.tile(sin, (1, 4)), bq_ref).astype(BF)
        k = proj(512, 640)
        kraw_ref[...] = k
        kr_ref[...] = _norm_rope_fwd(k, kw_ref[...], cos, sin, bk_ref).astype(BF)
        vb_ref[...] = proj(640, 768).astype(BF)
        ga_ref[...] = proj(768, 1280)
        a = proj(1280, 1792)
        g = proj(1792, 2304)
        a_ref[...] = a
        g_ref[...] = g
        z_ref[...] = a * _sigmoid(g)
        gb_ref[...] = proj(2304, 2816)

    t512 = pl.BlockSpec((TS, 512), _row)
    t128 = pl.BlockSpec((TS, 128), _row)
    return pl.pallas_call(
        body, name="fwd_in", grid=(NT,),
        out_shape=[jax.ShapeDtypeStruct((S, D), BF), jax.ShapeDtypeStruct((S, AW), F32),
                   jax.ShapeDtypeStruct((S, KVW), F32), jax.ShapeDtypeStruct((S, AW), F32),
                   jax.ShapeDtypeStruct((S, CW), F32), jax.ShapeDtypeStruct((S, CW), F32),
                   jax.ShapeDtypeStruct((S, CW), F32), jax.ShapeDtypeStruct((S, AW), BF),
                   jax.ShapeDtypeStruct((S, KVW), BF), jax.ShapeDtypeStruct((S, KVW), BF),
                   jax.ShapeDtypeStruct((S, CW), F32)],
        in_specs=[pl.BlockSpec((TS, D), _row), pl.BlockSpec((1, 3 * D), _const), pl.BlockSpec((1, D), _const),
                  pl.BlockSpec((1, AW), _const), pl.BlockSpec((1, KVW), _const), t128, t128,
                  pl.BlockSpec((AW, AW), _const), pl.BlockSpec((KVW, KVW), _const),
                  pl.BlockSpec((D, INW), _const, pipeline_mode=pl.Buffered(1))],
        out_specs=[pl.BlockSpec((TS, D), _row), t512, t128, t512, t512, t512, t512, t512, t128, t128, t512],
        compiler_params=_params(True),
    )(x2, mod, norm_w, qw_t, kw_t, cos_t, sin_t, bq, bk, w_in_full)


def _band_mask(i):
    qi = lax.broadcasted_iota(jnp.int32, (4 * BLK, 2 * BLK), 0) & (BLK - 1)
    kj = lax.broadcasted_iota(jnp.int32, (4 * BLK, 2 * BLK), 1)
    dist = qi + BLK - kj
    return (dist >= 0) & (dist < BLK) & ((kj >= BLK) | (i > 0))


def _sink_rows(sink_ref, g):
    row = lax.broadcasted_iota(jnp.int32, (4 * BLK, 1), 0)
    return jnp.where(row < BLK, sink_ref[0, 4 * g],
                     jnp.where(row < 2 * BLK, sink_ref[0, 4 * g + 1],
                               jnp.where(row < 3 * BLK, sink_ref[0, 4 * g + 2], sink_ref[0, 4 * g + 3])))


def _stack_heads(t, g):
    return jnp.concatenate([t[:, HD * (4 * g + h):HD * (4 * g + h + 1)] for h in range(4)], axis=0)


def _band(prev, cur, g):
    return jnp.concatenate([prev[:, HD * g:HD * (g + 1)], cur[:, HD * g:HD * (g + 1)]], axis=0)


def _softmax_band(qs, kb, mask, sink):
    s = lax.dot_general(qs, kb, (((1,), (1,)), ((), ())), preferred_element_type=F32) * (HD ** -0.5)
    s = jnp.where(mask, s, NEG)
    m = jnp.maximum(jnp.max(s, axis=-1, keepdims=True), sink)
    e = jnp.exp(s - m)
    es = jnp.exp(sink - m)
    inv = 1.0 / (jnp.sum(e, axis=-1, keepdims=True) + es)
    return e * inv, es * inv


def _attn_fwd_call(sinks, qr, kr, vb):
    def body(sink_ref, q_ref, kp_ref, kc_ref, vp_ref, vc_ref, o_ref):
        i = pl.program_id(0)
        mask = _band_mask(i)
        q = q_ref[...]
        kp, kc, vp, vc = kp_ref[...], kc_ref[...], vp_ref[...], vc_ref[...]
        for g in range(NKV):
            p, _ = _softmax_band(_stack_heads(q, g), _band(kp, kc, g), mask, _sink_rows(sink_ref, g))
            o = jnp.dot(p.astype(BF), _band(vp, vc, g), preferred_element_type=F32)
            for h in range(4):
                o_ref[:, HD * (4 * g + h):HD * (4 * g + h + 1)] = o[BLK * h:BLK * (h + 1), :]

    prev = lambda i: (jnp.maximum(i - 1, 0), 0)
    return pl.pallas_call(
        body, name="attn_fwd", grid=(NB,),
        out_shape=jax.ShapeDtypeStruct((S, AW), F32),
        in_specs=[_SMEM, pl.BlockSpec((BLK, AW), _row), pl.BlockSpec((BLK, KVW), prev), pl.BlockSpec((BLK, KVW), _row),
                  pl.BlockSpec((BLK, KVW), prev), pl.BlockSpec((BLK, KVW), _row)],
        out_specs=pl.BlockSpec((BLK, AW), _row),
        compiler_params=_params(True),
    )(sinks, qr, kr, kr, vb, vb)


def _attn_bwd_call(sinks, qr, kr, vb, d_ya, ga, o, qraw, qw_t, cos_t, sin_t, bq):
    def body(sink_ref, q_ref, kp_ref, kc_ref, vp_ref, vc_ref, dya_ref, ga_ref, o_ref, qraw_ref, qw_ref,
             cos_ref, sin_ref, bq_ref,
             dqraw_ref, dga_ref, dk_ref, dv_ref, gqw_ref, gsink_ref):
        i = pl.program_id(0)

        @pl.when(i == 0)
        def _():
            dk_ref[...] = jnp.zeros((S, KVW), F32)
            dv_ref[...] = jnp.zeros((S, KVW), F32)
            gqw_ref[...] = jnp.zeros((1, AW), F32)
            gsink_ref[...] = jnp.zeros((1, 128), F32)

        mask = _band_mask(i)
        q = q_ref[...]
        kp, kc, vp, vc = kp_ref[...], kc_ref[...], vp_ref[...], vc_ref[...]
        d_ya = dya_ref[...]
        act, dact = _silu_and_grad(ga_ref[...])
        dga_ref[...] = d_ya * o_ref[...] * dact
        d_o = (d_ya * act).astype(BF)
        lane = lax.broadcasted_iota(jnp.int32, (1, 128), 1)
        row = lax.broadcasted_iota(jnp.int32, (4 * BLK, 1), 0)
        gsink = jnp.zeros((1, 128), F32)
        dq_parts, dk_parts, dv_parts = [], [], []
        for g in range(NKV):
            qs = _stack_heads(q, g)
            kb = _band(kp, kc, g)
            vbd = _band(vp, vc, g)
            p, ps = _softmax_band(qs, kb, mask, _sink_rows(sink_ref, g))
            dos = _stack_heads(d_o, g)
            dp = lax.dot_general(dos, vbd, (((1,), (1,)), ((), ())), preferred_element_type=F32)
            dr = jnp.sum(p * dp, axis=-1, keepdims=True)
            ds = (p * (dp - dr) * (HD ** -0.5)).astype(BF)
            sink_term = ps * dr
            for h in range(4):
                part = jnp.sum(jnp.where((row >= BLK * h) & (row < BLK * (h + 1)), sink_term, 0.0),
                               axis=0, keepdims=True)
                gsink = gsink - jnp.where(lane == 4 * g + h, part, 0.0)
            dv_parts.append(lax.dot_general(p.astype(BF), dos, (((0,), (0,)), ((), ())), preferred_element_type=F32))
            dq_parts.append(jnp.dot(ds, kb, preferred_element_type=F32))
            dk_parts.append(lax.dot_general(ds, qs, (((0,), (0,)), ((), ())), preferred_element_type=F32))
        gsink_ref[...] += gsink
        dkb = jnp.concatenate(dk_parts, axis=1)
        dvb = jnp.concatenate(dv_parts, axis=1)
        r_prev = pl.multiple_of(jnp.maximum(i - 1, 0) * BLK, BLK)
        r_cur = pl.multiple_of(i * BLK, BLK)
        dk_ref[pl.ds(r_prev, BLK), :] += dkb[0:BLK]
        dv_ref[pl.ds(r_prev, BLK), :] += dvb[0:BLK]
        dk_ref[pl.ds(r_cur, BLK), :] += dkb[BLK:2 * BLK]
        dv_ref[pl.ds(r_cur, BLK), :] += dvb[BLK:2 * BLK]
        dq = jnp.concatenate([dq_parts[g][BLK * h:BLK * (h + 1), :] for g in range(NKV) for h in range(4)], axis=1)
        dq_raw, g_qw = _norm_rope_bwd(dq, qraw_ref[...], qw_ref[...], jnp.tile(cos_ref[...], (1, 4)),
                                      jnp.tile(sin_ref[...], (1, 4)), bq_ref)
        dqraw_ref[...] = dq_raw
        gqw_ref[...] += g_qw

    prev = lambda i: (jnp.maximum(i - 1, 0), 0)
    b512 = pl.BlockSpec((BLK, AW), _row)
    b128 = pl.BlockSpec((BLK, 128), _row)
    return pl.pallas_call(
        body, name="attn_bwd", grid=(NB,),
        out_shape=[jax.ShapeDtypeStruct((S, AW), F32), jax.ShapeDtypeStruct((S, AW), F32),
                   jax.ShapeDtypeStruct((S, KVW), F32), jax.ShapeDtypeStruct((S, KVW), F32),
                   jax.ShapeDtypeStruct((1, AW), F32), jax.ShapeDtypeStruct((1, 128), F32)],
        in_specs=[_SMEM, b512, pl.BlockSpec((BLK, KVW), prev), b128, pl.BlockSpec((BLK, KVW), prev), b128,
                  b512, b512, b512, b512, pl.BlockSpec((1, AW), _const), b128, b128, pl.BlockSpec((AW, AW), _const)],
        out_specs=[b512, b512, _VMEM, _VMEM, _VMEM, _VMEM],
        compiler_params=_params(True),
    )(sinks, qr, kr, kr, vb, vb, d_ya, ga, o, qraw, qw_t, cos_t, sin_t, bq)


HALO = 32


def _taps(ext_ref, base, weight_row, acc_rows):
    acc = None
    for b in range(8):
        amax = (CK - 1 - b) // 8
        win = ext_ref[pl.ds(base + b, acc_rows + 8 * amax), :]
        for a in range(amax + 1):
            term = win[8 * a:8 * a + acc_rows] * weight_row(8 * a + b)
            acc = term if acc is None else acc + term
    return acc


def _conv_fwd_call(z, gb, cwf, conv_b, ln_w, ln_b):
    def body(zc_in_ref, zp_ref, gb_ref, cw_ref, cb_ref, lw_ref, lb_ref, zc_ref, yb_ref, zext):
        i = pl.program_id(0)
        zext[0:HALO, :] = jnp.where(i > 0, zp_ref[TS - HALO:TS, :], 0.0)
        zext[HALO:HALO + TS, :] = zc_in_ref[...]
        zc = _taps(zext, HALO - (CK - 1), lambda k: cw_ref[k:k + 1, :], TS) + cb_ref[...]
        zc_ref[...] = zc
        mu = jnp.mean(zc, axis=-1, keepdims=True)
        dz = zc - mu
        rstd = lax.rsqrt(jnp.mean(dz * dz, axis=-1, keepdims=True) + EPS)
        zn = dz * rstd * lw_ref[...] + lb_ref[...]
        gbv = gb_ref[...]
        yb_ref[...] = (zn * _sigmoid(zn)) * (gbv * _sigmoid(gbv))

    t512 = pl.BlockSpec((TS, CW), _row)
    prev = pl.BlockSpec((TS, CW), lambda i: (jnp.maximum(i - 1, 0), 0))
    c512 = pl.BlockSpec((1, CW), _const)
    return pl.pallas_call(
        body, name="conv_fwd", grid=(NT,),
        out_shape=[jax.ShapeDtypeStruct((S, CW), F32), jax.ShapeDtypeStruct((S, CW), F32)],
        in_specs=[t512, prev, t512, pl.BlockSpec((CKP, CW), _const), c512, c512, c512],
        out_specs=[t512, t512],
        scratch_shapes=[pltpu.VMEM((TS + HALO, CW), F32)],
        compiler_params=_params(True),
    )(z, z, gb, cwf, conv_b, ln_w, ln_b)


def _conv_bwd_ln_call(d_yb, zc, gb, ln_w, ln_b):
    def body(dyb_ref, zc_ref, gb_ref, lw_ref, lb_ref, dzc_ref, dgb_ref, glw_ref, glb_ref, gcb_ref):
        i = pl.program_id(0)

        @pl.when(i == 0)
        def _():
            glw_ref[...] = jnp.zeros((1, CW), F32)
            glb_ref[...] = jnp.zeros((1, CW), F32)
            gcb_ref[...] = jnp.zeros((1, CW), F32)

        zc = zc_ref[...]
        mu = jnp.mean(zc, axis=-1, keepdims=True)
        dz = zc - mu
        rstd = lax.rsqrt(jnp.mean(dz * dz, axis=-1, keepdims=True) + EPS)
        zh = dz * rstd
        lw = lw_ref[...]
        zn = zh * lw + lb_ref[...]
        d_yb = dyb_ref[...]
        act_n, dact_n = _silu_and_grad(zn)
        act_g, dact_g = _silu_and_grad(gb_ref[...])
        dgb_ref[...] = d_yb * act_n * dact_g
        d_zn = d_yb * act_g * dact_n
        glw_ref[...] += jnp.sum(d_zn * zh, axis=0, keepdims=True)
        glb_ref[...] += jnp.sum(d_zn, axis=0, keepdims=True)
        dzh = d_zn * lw
        d_zc = rstd * (dzh - jnp.mean(dzh, axis=-1, keepdims=True) - zh * jnp.mean(dzh * zh, axis=-1, keepdims=True))
        dzc_ref[...] = d_zc
        gcb_ref[...] += jnp.sum(d_zc, axis=0, keepdims=True)

    t512 = pl.BlockSpec((TS, CW), _row)
    c512 = pl.BlockSpec((1, CW), _const)
    vec = jax.ShapeDtypeStruct((1, CW), F32)
    return pl.pallas_call(
        body, name="conv_bwd_ln", grid=(NT,),
        out_shape=[jax.ShapeDtypeStruct((S, CW), F32), jax.ShapeDtypeStruct((S, CW), F32), vec, vec, vec],
        in_specs=[t512, t512, t512, c512, c512],
        out_specs=[t512, t512, _VMEM, _VMEM, _VMEM],
        compiler_params=_params(True),
    )(d_yb, zc, gb, ln_w, ln_b)


def _conv_bwd_taps_call(d_zc, z, a, g, cwf):
    def body(dc_ref, dn_ref, zc_ref, zp_ref, a_ref, g_ref, cw_ref, da_ref, dg_ref, gcw_ref, dext, zext):
        i = pl.program_id(0)

        @pl.when(i == 0)
        def _():
            gcw_ref[...] = jnp.zeros((CKP, CW), F32)

        d_cur = dc_ref[...]
        dext[0:TS, :] = d_cur
        dext[TS:TS + HALO, :] = jnp.where(i < NT - 1, dn_ref[0:HALO, :], 0.0)
        zext[0:HALO, :] = jnp.where(i > 0, zp_ref[TS - HALO:TS, :], 0.0)
        zext[HALO:HALO + TS, :] = zc_ref[...]
        d_z = _taps(dext, 0, lambda j: cw_ref[CK - 1 - j:CK - j, :], TS)
        for b in range(8):
            amax = (CK - 1 - b) // 8
            win = zext[pl.ds(HALO - (CK - 1) + b, TS + 8 * amax), :]
            for aa in range(amax + 1):
                k = 8 * aa + b
                gcw_ref[k:k + 1, :] += jnp.sum(d_cur * win[8 * aa:8 * aa + TS], axis=0, keepdims=True)
        sg = _sigmoid(g_ref[...])
        da_ref[...] = d_z * sg
        dg_ref[...] = d_z * a_ref[...] * sg * (1.0 - sg)

    t512 = pl.BlockSpec((TS, CW), _row)
    prev = pl.BlockSpec((TS, CW), lambda i: (jnp.maximum(i - 1, 0), 0))
    nxt = pl.BlockSpec((TS, CW), lambda i: (jnp.minimum(i + 1, NT - 1), 0))
    return pl.pallas_call(
        body, name="conv_bwd_taps", grid=(NT,),
        out_shape=[jax.ShapeDtypeStruct((S, CW), F32), jax.ShapeDtypeStruct((S, CW), F32),
                   jax.ShapeDtypeStruct((CKP, CW), F32)],
        in_specs=[t512, nxt, t512, prev, t512, t512, pl.BlockSpec((CKP, CW), _const)],
        out_specs=[t512, t512, _VMEM],
        scratch_shapes=[pltpu.VMEM((TS + HALO, CW), F32), pltpu.VMEM((TS + HALO, CW), F32)],
        compiler_params=_params(True),
    )(d_zc, d_zc, z, z, a, g, cwf)


def _out_loss_call(o, ga, yb, x2, tgt, mod, w_out_full):
    def body(o_ref, ga_ref, yb_ref, x_ref, t_ref, mod_ref, w_ref,
             dout_ref, dya_ref, dyb_ref, gw_ref, loss_ref, dgate_ref):
        i = pl.program_id(0)

        @pl.when(i == 0)
        def _():
            gw_ref[...] = jnp.zeros((D, D), F32)
            loss_ref[...] = jnp.zeros((1, 128), F32)
            dgate_ref[...] = jnp.zeros((1, D), F32)

        gav = ga_ref[...]
        ya = o_ref[...] * (gav * _sigmoid(gav))
        ycat = jnp.concatenate([ya, yb_ref[...]], axis=1).astype(BF)
        w = w_ref[...]
        y = jnp.dot(ycat, w, preferred_element_type=F32)
        gate = mod_ref[:, 2 * D:3 * D]
        diff = x_ref[...] + gate * y - t_ref[...]
        sq = jnp.sum(jnp.sum(diff * diff, axis=1, keepdims=True), axis=0, keepdims=True)
        loss_ref[...] += jnp.broadcast_to(sq, (1, 128))
        d_out = diff * (1.0 / D)
        dout_ref[...] = d_out
        dgate_ref[...] += jnp.sum(d_out * y, axis=0, keepdims=True)
        dy = (d_out * gate).astype(BF)
        d_ycat = lax.dot_general(dy, w, (((1,), (1,)), ((), ())), preferred_element_type=F32)
        dya_ref[...] = d_ycat[:, 0:AW]
        dyb_ref[...] = d_ycat[:, AW:D]
        gw_ref[...] += lax.dot_general(ycat, dy, (((0,), (0,)), ((), ())), preferred_element_type=F32)

    t512 = pl.BlockSpec((TS, 512), _row)
    t1024 = pl.BlockSpec((TS, D), _row)
    return pl.pallas_call(
        body, name="out_loss", grid=(NT,),
        out_shape=[jax.ShapeDtypeStruct((S, D), F32), jax.ShapeDtypeStruct((S, AW), F32),
                   jax.ShapeDtypeStruct((S, CW), F32), jax.ShapeDtypeStruct((D, D), F32),
                   jax.ShapeDtypeStruct((1, 128), F32), jax.ShapeDtypeStruct((1, D), F32)],
        in_specs=[t512, t512, t512, t1024, t1024, pl.BlockSpec((1, 3 * D), _const),
                  pl.BlockSpec((D, D), _const, pipeline_mode=pl.Buffered(1))],
        out_specs=[t1024, t512, t512, _VMEM, _VMEM, _VMEM],
        compiler_params=_params(True),
    )(o, ga, yb, x2, tgt, mod, w_out_full)


def _bwd_in_call(dqraw, dk, dv, dga, da, dg, dgb, kraw, kw_t, cos_t, sin_t, bk, h, w_in_full, x2, d_out, mod, norm_w):
    def body(dq_ref, dk_ref, dv_ref, dga_ref, da_ref, dg_ref, dgb_ref, kraw_ref, kw_ref, cos_ref, sin_ref, bk_ref,
             h_ref, w_ref, x_ref, dout_ref, mod_ref, nw_ref,
             gx_ref, gw_ref, dshift_ref, dscale_ref, gnw_ref, gkw_ref):
        i = pl.program_id(0)

        @pl.when(i == 0)
        def _():
            gw_ref[...] = jnp.zeros((D, INW), F32)
            dshift_ref[...] = jnp.zeros((1, D), F32)
            dscale_ref[...] = jnp.zeros((1, D), F32)
            gnw_ref[...] = jnp.zeros((1, D), F32)
            gkw_ref[...] = jnp.zeros((1, KVW), F32)

        dk_raw, g_kw = _norm_rope_bwd(dk_ref[...], kraw_ref[...], kw_ref[...], cos_ref[...], sin_ref[...], bk_ref)
        gkw_ref[...] += g_kw
        dproj = jnp.concatenate([dq_ref[...], dk_raw, dv_ref[...], dga_ref[...], da_ref[...], dg_ref[...],
                                 dgb_ref[...]], axis=1).astype(BF)
        d_h = lax.dot_general(dproj, w_ref[...], (((1,), (1,)), ((), ())), preferred_element_type=F32)
        gw_ref[...] += lax.dot_general(h_ref[...], dproj, (((0,), (0,)), ((), ())), preferred_element_type=F32)

        xv = x_ref[...]
        scale = mod_ref[:, D:2 * D]
        nw = nw_ref[...]
        r = lax.rsqrt(jnp.mean(xv * xv, axis=-1, keepdims=True) + EPS)
        xn = xv * r
        dshift_ref[...] += jnp.sum(d_h, axis=0, keepdims=True)
        dscale_ref[...] += jnp.sum(d_h * (xn * nw), axis=0, keepdims=True)
        d_u = d_h * (1.0 + scale)
        gnw_ref[...] += jnp.sum(d_u * xn, axis=0, keepdims=True)
        d_xn = d_u * nw
        gx_ref[...] = dout_ref[...] + r * (d_xn - xn * jnp.mean(d_xn * xn, axis=-1, keepdims=True))

    t512 = pl.BlockSpec((TS, 512), _row)
    t128 = pl.BlockSpec((TS, 128), _row)
    t1024 = pl.BlockSpec((TS, D), _row)
    c1024 = pl.BlockSpec((1, D), _const)
    vec = jax.ShapeDtypeStruct((1, D), F32)
    return pl.pallas_call(
        body, name="bwd_in", grid=(NT,),
        out_shape=[jax.ShapeDtypeStruct((S, D), F32), jax.ShapeDtypeStruct((D, INW), F32), vec, vec, vec,
                   jax.ShapeDtypeStruct((1, KVW), F32)],
        in_specs=[t512, t128, t128, t512, t512, t512, t512, t128, pl.BlockSpec((1, KVW), _const), t128, t128,
                  pl.BlockSpec((KVW, KVW), _const), t1024,
                  pl.BlockSpec((D, INW), _const, pipeline_mode=pl.Buffered(1)), t1024, t1024,
                  pl.BlockSpec((1, 3 * D), _const), c1024],
        out_specs=[t1024, _VMEM, _VMEM, _VMEM, _VMEM, _VMEM],
        compiler_params=_params(True),
    )(dqraw, dk, dv, dga, da, dg, dgb, kraw, kw_t, cos_t, sin_t, bk, h, w_in_full, x2, d_out, mod, norm_w)


SM_ROWS = 8


def _scatter_call(gw_in, gw_out, gcw, dshift, dscale, dgate, gnw, glw, glb, gcb, gqw, gkw, gsink, loss, cact_all):
    def body(gwin_ref, gwout_ref, gcw_ref, dshift_ref, dscale_ref, dgate_ref, gnw_ref, glw_ref, glb_ref, gcb_ref,
             gqw_ref, gkw_ref, gsink_ref, loss_ref, cact_ref,
             o_gwin, o_gwout, o_gwada, o_gbada, o_gnw, o_gqw, o_gkw, o_gsink, o_gcw, o_gcb, o_glw, o_glb, o_loss,
             win_send, win_recv, wout_send, wout_recv, sm_buf, cw_buf, dmod_all,
             ws_s, ws_r, wo_s, wo_r, sm_s, sm_r, cw_s, cw_r):
        pos = _mesh_pos()
        x, y, cc = pos
        me = 4 * x + 2 * y + cc

        ch_sm = (_slab(sm_buf), sm_s, sm_r)
        ch_cw = (_slab(cw_buf), cw_s, cw_r)
        z128 = jnp.zeros((1, 128), F32)
        row4 = jnp.concatenate([glw_ref[...], glb_ref[...]], axis=1)
        row5 = jnp.concatenate([gcb_ref[...], gqw_ref[...]], axis=1)
        row6 = jnp.concatenate([gkw_ref[...], gsink_ref[...], loss_ref[...]] + [z128] * 5, axis=1)
        sm_buf[me] = jnp.concatenate([dshift_ref[...], dscale_ref[...], dgate_ref[...], gnw_ref[...], row4, row5, row6,
                                      jnp.zeros((1, D), F32)], axis=0)
        f_sm = _ag_start(ch_sm, pos)
        cw_buf[me] = gcw_ref[...]
        f_cw = _ag_start(ch_cw, pos)

        for d in range(NDEV):
            win_send[d] = gwin_ref[:, IN_SHARD * d:IN_SHARD * (d + 1)].astype(BF)
            wout_send[d] = gwout_ref[OUT_SHARD * d:OUT_SHARD * (d + 1), :].astype(BF)
        sends = []
        for k in range(1, NDEV):
            px, py, pc = _xor_peer(pos, k)
            dst = 4 * px + 2 * py + pc
            sends.append(pltpu.make_async_remote_copy(
                src_ref=win_send.at[dst], dst_ref=win_recv.at[k], send_sem=ws_s.at[k - 1], recv_sem=ws_r.at[k - 1],
                device_id=(px, py, pc), device_id_type=MESH))
            sends.append(pltpu.make_async_remote_copy(
                src_ref=wout_send.at[dst], dst_ref=wout_recv.at[k], send_sem=wo_s.at[k - 1], recv_sem=wo_r.at[k - 1],
                device_id=(px, py, pc), device_id_type=MESH))
        for cp in sends:
            cp.start()

        _ag_finish(ch_sm, pos, f_sm)
        _ag_finish(ch_cw, pos, f_cw)
        tot = sm_buf[0]
        cw_tot = cw_buf[0]
        for d in range(1, NDEV):
            tot = tot + sm_buf[d]
            cw_tot = cw_tot + cw_buf[d]
        o_gbada[...] = jnp.concatenate([tot[0:1, :], tot[1:2, :], tot[2:3, :]], axis=1)
        o_gnw[...] = tot[3:4, :]
        o_glw[...] = tot[4:5, 0:CW]
        o_glb[...] = tot[4:5, CW:D]
        o_gcb[...] = tot[5:6, 0:CW]
        gq = tot[5:6, CW:CW + HD]
        for hh in range(1, NQ):
            gq = gq + tot[5:6, CW + HD * hh:CW + HD * (hh + 1)]
        o_gqw[...] = gq
        o_gkw[...] = tot[6:7, 0:HD] + tot[6:7, HD:2 * HD]
        o_gsink[...] = tot[6:7, 128:128 + NQ]
        o_loss[...] = tot[6:7, 256:384] * (0.5 / D)
        mine = jnp.zeros((CK, CONV_SHARD), F32)
        for d in range(NDEV):
            mine = mine + jnp.where(me == d, cw_tot[0:CK, CONV_SHARD * d:CONV_SHARD * (d + 1)], 0.0)
        o_gcw[...] = mine
        for d in range(NDEV):
            dmod_all[d:d + 1, :] = jnp.concatenate([sm_buf[d, 0:1, :], sm_buf[d, 1:2, :], sm_buf[d, 2:3, :]], axis=1)
        col0 = pl.multiple_of(me * ADA_SHARD, 128)
        o_gwada[...] = lax.dot_general(cact_ref[...], dmod_all[:, pl.ds(col0, ADA_SHARD)], (((0,), (0,)), ((), ())),
                                       preferred_element_type=F32, precision=lax.Precision.HIGHEST)

        for k in range(1, NDEV):
            pltpu.make_async_remote_copy(src_ref=win_send.at[0], dst_ref=win_recv.at[k], send_sem=ws_s.at[k - 1],
                                         recv_sem=ws_r.at[k - 1], device_id=pos, device_id_type=MESH).wait_recv()
            pltpu.make_async_remote_copy(src_ref=wout_send.at[0], dst_ref=wout_recv.at[k], send_sem=wo_s.at[k - 1],
                                         recv_sem=wo_r.at[k - 1], device_id=pos, device_id_type=MESH).wait_recv()
        acc_in = jnp.zeros((D, IN_SHARD), F32)
        acc_out = jnp.zeros((OUT_SHARD, D), F32)
        for d in range(NDEV):
            acc_in = acc_in + jnp.where(me == d, gwin_ref[:, IN_SHARD * d:IN_SHARD * (d + 1)], 0.0)
            acc_out = acc_out + jnp.where(me == d, gwout_ref[OUT_SHARD * d:OUT_SHARD * (d + 1), :], 0.0)
        for k in range(1, NDEV):
            acc_in = acc_in + win_recv[k].astype(F32)
            acc_out = acc_out + wout_recv[k].astype(F32)
        o_gwin[...] = acc_in
        o_gwout[...] = acc_out
        for cp in sends:
            cp.wait_send()

    sem7 = pltpu.SemaphoreType.DMA((7,))
    sds = jax.ShapeDtypeStruct
    return pl.pallas_call(
        body, name="scatter",
        out_shape=[sds((D, IN_SHARD), F32), sds((OUT_SHARD, D), F32), sds((D, ADA_SHARD), F32), sds((1, 3 * D), F32),
                   sds((1, D), F32), sds((1, HD), F32), sds((1, HD), F32), sds((1, NQ), F32),
                   sds((CK, CONV_SHARD), F32), sds((1, CW), F32), sds((1, CW), F32), sds((1, CW), F32),
                   sds((1, 128), F32)],
        in_specs=[_VMEM] * 15, out_specs=[_VMEM] * 13,
        scratch_shapes=[pltpu.VMEM((NDEV, D, IN_SHARD), BF), pltpu.VMEM((NDEV, D, IN_SHARD), BF),
                        pltpu.VMEM((NDEV, OUT_SHARD, D), BF), pltpu.VMEM((NDEV, OUT_SHARD, D), BF),
                        pltpu.VMEM((NDEV, SM_ROWS, D), F32), pltpu.VMEM((NDEV, CKP, CW), F32),
                        pltpu.VMEM((NDEV, 3 * D), F32)] + [sem7] * 8,
        compiler_params=_params(),
    )(gw_in, gw_out, gcw, dshift, dscale, dgate, gnw, glw, glb, gcb, gqw, gkw, gsink, loss, cact_all)


def _adam_call(ws, gs, ms, vs):
    n = len(ws)
    bc1 = 1.0 - ADAM_B1 ** ADAM_STEP
    bc2 = 1.0 - ADAM_B2 ** ADAM_STEP

    def body(*refs):
        ins, outs = refs[:4 * n], refs[4 * n:]
        for j in range(n):
            w, g, m, v = (ins[j][...], ins[n + j][...], ins[2 * n + j][...], ins[3 * n + j][...])
            m_new = ADAM_B1 * m + (1.0 - ADAM_B1) * g
            v_new = ADAM_B2 * v + (1.0 - ADAM_B2) * (g * g)
            m_hat = m_new / bc1
            v_hat = v_new / bc2
            outs[j][...] = -ADAM_LR * (m_hat / (jnp.sqrt(v_hat) + ADAM_EPS) + ADAM_WD * w)
            outs[n + j][...] = m_new
            outs[2 * n + j][...] = v_new

    shapes = [jax.ShapeDtypeStruct(w.shape, F32) for w in ws]
    return pl.pallas_call(
        body, name="adam",
        out_shape=shapes * 3, in_specs=[_VMEM] * (4 * n), out_specs=[_VMEM] * (3 * n),
        compiler_params=_params(),
    )(*ws, *gs, *ms, *vs)


def _rope_tables():
    inv = ROPE_THETA ** (-jnp.arange(0, HD, 2, dtype=F32) / HD)
    ang = jnp.arange(S, dtype=F32)[:, None] * inv[None, :]
    cos, sin = jnp.cos(ang), jnp.sin(ang)
    cos64 = jnp.concatenate([cos, cos], axis=-1)
    sin64 = jnp.concatenate([-sin, sin], axis=-1)
    return jnp.tile(cos64, (1, 2)), jnp.tile(sin64, (1, 2))


def _group_matrix(width):
    idx = jnp.arange(width) // HD
    return jnp.where(idx[:, None] == idx[None, :], 1.0 / HD, 0.0).astype(BF)


def kernel(x, c, w_ada, b_ada, norm_w, w_in, q_norm_w, k_norm_w, sinks, conv_w, conv_b, ln_w, ln_b, w_out, loss_target, m_w_ada, m_b_ada, m_norm_w, m_w_in, m_q_norm_w, m_k_norm_w, m_sinks, m_conv_w, m_conv_b, m_ln_w, m_ln_b, m_w_out, v_w_ada, v_b_ada, v_norm_w, v_w_in, v_q_norm_w, v_k_norm_w, v_sinks, v_conv_w, v_conv_b, v_ln_w, v_ln_b, v_w_out):
    x2 = x[0]
    tgt = loss_target[0]
    cos_t, sin_t = _rope_tables()
    bq = _group_matrix(AW)
    bk = _group_matrix(KVW)
    qw_t = jnp.tile(q_norm_w, (1, NQ))
    kw_t = jnp.tile(k_norm_w, (1, NKV))

    w_in_full, w_out_full, cwf, mod, cact_all = _gather_call(w_in[0], w_out[0], conv_w[0], c, w_ada[0], b_ada)

    h, qraw, kraw, ga, a, g, gb, qr, kr, vb, z = _fwd_in_call(x2, mod, norm_w, qw_t, kw_t, cos_t, sin_t, bq, bk, w_in_full)
    o = _attn_fwd_call(sinks, qr, kr, vb)
    zc, yb = _conv_fwd_call(z, gb, cwf, conv_b, ln_w, ln_b)
    d_out, d_ya, d_yb, gw_out, loss_p, dgate = _out_loss_call(o, ga, yb, x2, tgt, mod, w_out_full)

    d_zc, dgb, glw, glb, gcb = _conv_bwd_ln_call(d_yb, zc, gb, ln_w, ln_b)
    da, dg, gcw = _conv_bwd_taps_call(d_zc, z, a, g, cwf)
    dqraw, dga, dk, dv, gqw, gsink = _attn_bwd_call(sinks, qr, kr, vb, d_ya, ga, o, qraw, qw_t, cos_t, sin_t, bq)
    grad_x, gw_in, dshift, dscale, gnw, gkw = _bwd_in_call(dqraw, dk, dv, dga, da, dg, dgb, kraw, kw_t, cos_t, sin_t,
                                                           bk, h, w_in_full, x2, d_out, mod, norm_w)

    (g_w_in, g_w_out, g_w_ada, g_b_ada, g_norm_w, g_qw, g_kw, g_sinks, g_conv_w, g_conv_b, g_ln_w, g_ln_b,
     loss_v) = _scatter_call(gw_in, gw_out, gcw, dshift, dscale, dgate, gnw, glw, glb, gcb, gqw, gkw, gsink, loss_p,
                             cact_all)

    ws = [w_ada[0], b_ada, norm_w, w_in[0], q_norm_w, k_norm_w, sinks, conv_w[0], conv_b, ln_w, ln_b, w_out[0]]
    gs = [g_w_ada, g_b_ada, g_norm_w, g_w_in, g_qw, g_kw, g_sinks, g_conv_w, g_conv_b, g_ln_w, g_ln_b, g_w_out]
    ms = [m_w_ada[0], m_b_ada, m_norm_w, m_w_in[0], m_q_norm_w, m_k_norm_w, m_sinks, m_conv_w[0], m_conv_b, m_ln_w,
          m_ln_b, m_w_out[0]]
    vs = [v_w_ada[0], v_b_ada, v_norm_w, v_w_in[0], v_q_norm_w, v_k_norm_w, v_sinks, v_conv_w[0], v_conv_b, v_ln_w,
          v_ln_b, v_w_out[0]]
    upd = _adam_call(ws, gs, ms, vs)
    n = len(ws)
    shaped = [w_ada, b_ada, norm_w, w_in, q_norm_w, k_norm_w, sinks, conv_w, conv_b, ln_w, ln_b, w_out]

    def like(vals):
        return [v.reshape(s.shape) for v, s in zip(vals, shaped)]

    return (loss_v[0, 0], grad_x[None], *like(gs), *like(upd[0:n]), *like(upd[n:2 * n]), *like(upd[2 * n:3 * n]))
```

```python
import functools

import jax
import jax.numpy as jnp
from jax import lax
from jax.experimental import pallas as pl
from jax.experimental.pallas import tpu as pltpu

S = 2048
D = 1024
NDEV = 8
HD = 64
NQ = 8
NKV = 2
AW = 512
KVW = 128
CW = 512
INW = 2816
IN_SHARD = INW // NDEV
ADA_SHARD = 3 * D // NDEV
OUT_SHARD = D // NDEV
CONV_SHARD = CW // NDEV
CK = 31
CKP = 32
BLK = 128
TS = 256
NT = S // TS
NB = S // BLK
EPS = 1e-6
ROPE_THETA = 10000.0
NEG = -1e30
BF = jnp.bfloat16
F32 = jnp.float32

ADAM_LR = 0.001
ADAM_B1 = 0.9
ADAM_B2 = 0.999
ADAM_EPS = 1e-08
ADAM_WD = 0.01
ADAM_STEP = 10

VMEM_LIMIT = 56 * 1024 * 1024
BIG_VMEM_LIMIT = 62 * 1024 * 1024
MESH = pl.DeviceIdType.MESH

_VMEM = pl.BlockSpec(memory_space=pltpu.VMEM)
_SMEM = pl.BlockSpec(memory_space=pltpu.SMEM)


def _params(grid=False):
    if grid:
        return pltpu.CompilerParams(dimension_semantics=("arbitrary",), vmem_limit_bytes=VMEM_LIMIT)
    return pltpu.CompilerParams(vmem_limit_bytes=VMEM_LIMIT)


def _row(i):
    return (i, 0)


def _const(i):
    return (0, 0)


def _sigmoid(t):
    return 1.0 / (1.0 + jnp.exp(-t))


def _silu_and_grad(t):
    sg = _sigmoid(t)
    return t * sg, sg * (1.0 + t * (1.0 - sg))


def _group_mean(t, b_ref):
    hi = t.astype(BF)
    lo = (t - hi.astype(F32)).astype(BF)
    b = b_ref[...]
    return jnp.dot(hi, b, preferred_element_type=F32) + jnp.dot(lo, b, preferred_element_type=F32)


def _partner(t):
    w = t.shape[-1]
    lane = lax.broadcasted_iota(jnp.int32, t.shape, 1)
    first = (lane & 32) == 0
    return jnp.where(first, pltpu.roll(t, w - 32, 1), pltpu.roll(t, 32, 1))


def _norm_rope_fwd(t, w_t, cos, sin, b_ref):
    r = lax.rsqrt(_group_mean(t * t, b_ref) + EPS)
    tn = t * r * w_t
    return tn * cos + _partner(tn) * sin


def _norm_rope_bwd(d_out, t, w_t, cos, sin, b_ref):
    d_tn = d_out * cos + _partner(d_out * sin)
    r = lax.rsqrt(_group_mean(t * t, b_ref) + EPS)
    th = t * r
    g_w = jnp.sum(d_tn * th, axis=0, keepdims=True)
    d_th = d_tn * w_t
    d_t = r * (d_th - th * _group_mean(d_th * th, b_ref))
    return d_t, g_w


def _mesh_pos():
    return lax.axis_index("x"), lax.axis_index("y"), lax.axis_index("c")


def _ag_copy(chan, k, block, to):
    blk, send_sems, recv_sems = chan
    ref = blk(*block)
    return pltpu.make_async_remote_copy(src_ref=ref, dst_ref=ref, send_sem=send_sems.at[k],
                                        recv_sem=recv_sems.at[k], device_id=to, device_id_type=MESH)


def _ag_start(chan, pos):
    x, y, c = pos
    me = (x, y, c)
    chips = [(1 - x, y), (x, 1 - y), (1 - x, 1 - y)]
    first = [_ag_copy(chan, 0, me, (x, y, 1 - c))]
    first += [_ag_copy(chan, 1 + j, me, (*chip, c)) for j, chip in enumerate(chips)]
    for cp in first:
        cp.start()
    return first


def _ag_finish(chan, pos, first):
    x, y, c = pos
    me = (x, y, c)
    sibling = (x, y, 1 - c)
    chips = [(1 - x, y), (x, 1 - y), (1 - x, 1 - y)]
    passed = [_ag_copy(chan, 4 + j, (*chip, c), sibling) for j, chip in enumerate(chips)]
    for j, chip in enumerate(chips):
        _ag_copy(chan, 1 + j, (*chip, c), me).wait_recv()
        passed[j].start()
    _ag_copy(chan, 0, sibling, me).wait_recv()
    for j, chip in enumerate(chips):
        _ag_copy(chan, 4 + j, (*chip, 1 - c), me).wait_recv()
    for cp in first + passed:
        cp.wait_send()


def _slab(buf):
    return lambda px, py, pc: buf.at[4 * px + 2 * py + pc]


def _row_block(buf, rows, align):
    return lambda px, py, pc: buf.at[pl.ds(pl.multiple_of((4 * px + 2 * py + pc) * rows, align), rows), :]


def _xor_peer(pos, k):
    x, y, c = pos
    kx, ky, kc = (k >> 2) & 1, (k >> 1) & 1, k & 1
    return (1 - x if kx else x, 1 - y if ky else y, 1 - c if kc else c)


def _gather_call(w_in_t, w_out_s, conv_w_s, c, w_ada_s, b_ada):
    def body(win_ref, wout_ref, cw_ref, c_ref, wada_ref, bada_ref,
             wtf_ref, woutf_ref, cwf_ref, mod_ref, cact_ref,
             cw_buf, ca_buf, mp_buf,
             s0, r0, s1, r1, s2, r2, s3, r3, s4, r4):
        pos = _mesh_pos()
        x, y, cc = pos
        me = 4 * x + 2 * y + cc
        ch_win = (_row_block(wtf_ref, IN_SHARD, 16), s0, r0)
        ch_wout = (_row_block(woutf_ref, OUT_SHARD, 16), s1, r1)
        ch_cw = (_slab(cw_buf), s2, r2)
        ch_ca = (_slab(ca_buf), s3, r3)
        ch_mp = (_slab(mp_buf), s4, r4)

        cv = c_ref[...]
        c_act = cv * _sigmoid(cv)
        ca_buf[me] = jnp.broadcast_to(c_act, (8, D))
        f_ca = _ag_start(ch_ca, pos)
        wtf_ref[pl.ds(pl.multiple_of(me * IN_SHARD, 16), IN_SHARD), :] = win_ref[...].astype(BF)
        f_win = _ag_start(ch_win, pos)
        woutf_ref[pl.ds(pl.multiple_of(me * OUT_SHARD, 16), OUT_SHARD), :] = wout_ref[...].astype(BF)
        f_wout = _ag_start(ch_wout, pos)
        cw_buf[me] = cw_ref[...]
        f_cw = _ag_start(ch_cw, pos)

        _ag_finish(ch_ca, pos, f_ca)
        cact_all = jnp.concatenate([ca_buf[d, 0:1, :] for d in range(NDEV)], axis=0)
        cact_ref[...] = cact_all
        col0 = pl.multiple_of(me * ADA_SHARD, 128)
        mp = jnp.dot(cact_all, wada_ref[...], preferred_element_type=F32,
                     precision=lax.Precision.HIGHEST) + bada_ref[:, pl.ds(col0, ADA_SHARD)]
        mp_buf[me] = mp
        f_mp = _ag_start(ch_mp, pos)

        _ag_finish(ch_cw, pos, f_cw)
        _ag_finish(ch_mp, pos, f_mp)
        for d in range(NDEV):
            cwf_ref[0:CK, CONV_SHARD * d:CONV_SHARD * (d + 1)] = cw_buf[d]
        cwf_ref[CK:CKP, :] = jnp.zeros((CKP - CK, CW), F32)
        mod_ref[...] = jnp.concatenate([mp_buf[d, pl.ds(me, 1), :] for d in range(NDEV)], axis=1)
        _ag_finish(ch_win, pos, f_win)
        _ag_finish(ch_wout, pos, f_wout)

    sem = pltpu.SemaphoreType.DMA((7,))
    return pl.pallas_call(
        body, name="gather",
        out_shape=[jax.ShapeDtypeStruct((INW, D), BF), jax.ShapeDtypeStruct((D, D), BF),
                   jax.ShapeDtypeStruct((CKP, CW), F32), jax.ShapeDtypeStruct((1, 3 * D), F32),
                   jax.ShapeDtypeStruct((NDEV, D), F32)],
        in_specs=[_VMEM] * 6, out_specs=[_VMEM] * 5,
        scratch_shapes=[pltpu.VMEM((NDEV, CK, CONV_SHARD), F32), pltpu.VMEM((NDEV, 8, D), F32),
                        pltpu.VMEM((NDEV, 8, ADA_SHARD), F32)] + [sem] * 10,
        compiler_params=_params(),
    )(w_in_t, w_out_s, conv_w_s, c, w_ada_s, b_ada)


def _fwd_in_call(x2, mod, norm_w, qw_t, kw_t, cos_t, sin_t, bq, bk, wt_full):
    def body(x_ref, mod_ref, nw_ref, qw_ref, kw_ref, cos_ref, sin_ref, bq_ref, bk_ref, w_ref,
             h_ref, qraw_ref, kraw_ref, ga_ref, a_ref, g_ref, gb_ref, qr_ref, kr_ref, vb_ref, z_ref):
        xv = x_ref[...]
        shift = mod_ref[:, 0:D]
        scale = mod_ref[:, D:2 * D]
        r = lax.rsqrt(jnp.mean(xv * xv, axis=-1, keepdims=True) + EPS)
        h = (xv * r * nw_ref[...]) * (1.0 + scale) + shift
        hb = h.astype(BF)
        h_ref[...] = hb

        def proj(lo, hi):
            return lax.dot_general(hb, w_ref[lo:hi, :], (((1,), (1,)), ((), ())), preferred_element_type=F32)

        cos = cos_ref[...]
        sin = sin_ref[...]
        q = proj(0, 512)
        qraw_ref[...] = q
        qr_ref[...] = _norm_rope_fwd(q, qw_ref[...], jnp.tile(cos, (1, 4)), jnp.tile(sin, (1, 4)), bq_ref).astype(BF)
        k = proj(512, 640)
        kraw_ref[...] = k
        kr_ref[...] = _norm_rope_fwd(k, kw_ref[...], cos, sin, bk_ref).astype(BF)
        vb_ref[...] = proj(640, 768).astype(BF)
        ga_ref[...] = proj(768, 1280)
        a = proj(1280, 1792)
        g = proj(1792, 2304)
        a_ref[...] = a
        g_ref[...] = g
        z_ref[...] = a * _sigmoid(g)
        gb_ref[...] = proj(2304, 2816)

    t512 = pl.BlockSpec((TS, 512), _row)
    t128 = pl.BlockSpec((TS, 128), _row)
    return pl.pallas_call(
        body, name="fwd_in", grid=(NT,),
        out_shape=[jax.ShapeDtypeStruct((S, D), BF), jax.ShapeDtypeStruct((S, AW), F32),
                   jax.ShapeDtypeStruct((S, KVW), F32), jax.ShapeDtypeStruct((S, AW), F32),
                   jax.ShapeDtypeStruct((S, CW), F32), jax.ShapeDtypeStruct((S, CW), F32),
                   jax.ShapeDtypeStruct((S, CW), F32), jax.ShapeDtypeStruct((S, AW), BF),
                   jax.ShapeDtypeStruct((S, KVW), BF), jax.ShapeDtypeStruct((S, KVW), BF),
                   jax.ShapeDtypeStruct((S, CW), F32)],
        in_specs=[pl.BlockSpec((TS, D), _row), pl.BlockSpec((1, 3 * D), _const), pl.BlockSpec((1, D), _const),
                  pl.BlockSpec((1, AW), _const), pl.BlockSpec((1, KVW), _const), t128, t128,
                  pl.BlockSpec((AW, AW), _const), pl.BlockSpec((KVW, KVW), _const),
                  pl.BlockSpec((INW, D), _const, pipeline_mode=pl.Buffered(1))],
        out_specs=[pl.BlockSpec((TS, D), _row), t512, t128, t512, t512, t512, t512, t512, t128, t128, t512],
        compiler_params=_params(True),
    )(x2, mod, norm_w, qw_t, kw_t, cos_t, sin_t, bq, bk, wt_full)


def _band_mask(i):
    qi = lax.broadcasted_iota(jnp.int32, (4 * BLK, 2 * BLK), 0) & (BLK - 1)
    kj = lax.broadcasted_iota(jnp.int32, (4 * BLK, 2 * BLK), 1)
    dist = qi + BLK - kj
    return (dist >= 0) & (dist < BLK) & ((kj >= BLK) | (i > 0))


def _sink_rows(sink_ref, g):
    row = lax.broadcasted_iota(jnp.int32, (4 * BLK, 1), 0)
    return jnp.where(row < BLK, sink_ref[0, 4 * g],
                     jnp.where(row < 2 * BLK, sink_ref[0, 4 * g + 1],
                               jnp.where(row < 3 * BLK, sink_ref[0, 4 * g + 2], sink_ref[0, 4 * g + 3])))


def _stack_heads(t, g):
    return jnp.concatenate([t[:, HD * (4 * g + h):HD * (4 * g + h + 1)] for h in range(4)], axis=0)


def _band(prev, cur, g):
    return jnp.concatenate([prev[:, HD * g:HD * (g + 1)], cur[:, HD * g:HD * (g + 1)]], axis=0)


def _softmax_band(qs, kb, mask, sink):
    s = lax.dot_general(qs, kb, (((1,), (1,)), ((), ())), preferred_element_type=F32) * (HD ** -0.5)
    s = jnp.where(mask, s, NEG)
    m = jnp.maximum(jnp.max(s, axis=-1, keepdims=True), sink)
    e = jnp.exp(s - m)
    es = jnp.exp(sink - m)
    inv = 1.0 / (jnp.sum(e, axis=-1, keepdims=True) + es)
    return e * inv, es * inv


def _attn_fwd_call(sinks, qr, kr, vb):
    def body(sink_ref, q_ref, kp_ref, kc_ref, vp_ref, vc_ref, o_ref):
        i = pl.program_id(0)
        mask = _band_mask(i)
        q = q_ref[...]
        kp, kc, vp, vc = kp_ref[...], kc_ref[...], vp_ref[...], vc_ref[...]
        for g in range(NKV):
            p, _ = _softmax_band(_stack_heads(q, g), _band(kp, kc, g), mask, _sink_rows(sink_ref, g))
            o = jnp.dot(p.astype(BF), _band(vp, vc, g), preferred_element_type=F32)
            for h in range(4):
                o_ref[:, HD * (4 * g + h):HD * (4 * g + h + 1)] = o[BLK * h:BLK * (h + 1), :]

    prev = lambda i: (jnp.maximum(i - 1, 0), 0)
    return pl.pallas_call(
        body, name="attn_fwd", grid=(NB,),
        out_shape=jax.ShapeDtypeStruct((S, AW), F32),
        in_specs=[_SMEM, pl.BlockSpec((BLK, AW), _row), pl.BlockSpec((BLK, KVW), prev), pl.BlockSpec((BLK, KVW), _row),
                  pl.BlockSpec((BLK, KVW), prev), pl.BlockSpec((BLK, KVW), _row)],
        out_specs=pl.BlockSpec((BLK, AW), _row),
        compiler_params=_params(True),
    )(sinks, qr, kr, kr, vb, vb)


def _attn_bwd_call(sinks, qr, kr, vb, d_ya, ga, o, qraw, qw_t, cos_t, sin_t, bq):
    def body(sink_ref, q_ref, kp_ref, kc_ref, vp_ref, vc_ref, dya_ref, ga_ref, o_ref, qraw_ref, qw_ref,
             cos_ref, sin_ref, bq_ref,
             dqraw_ref, dga_ref, dk_ref, dv_ref, gqw_ref, gsink_ref):
        i = pl.program_id(0)

        @pl.when(i == 0)
        def _():
            dk_ref[...] = jnp.zeros((S, KVW), F32)
            dv_ref[...] = jnp.zeros((S, KVW), F32)
            gqw_ref[...] = jnp.zeros((1, AW), F32)
            gsink_ref[...] = jnp.zeros((1, 128), F32)

        mask = _band_mask(i)
        q = q_ref[...]
        kp, kc, vp, vc = kp_ref[...], kc_ref[...], vp_ref[...], vc_ref[...]
        d_ya = dya_ref[...]
        act, dact = _silu_and_grad(ga_ref[...])
        dga_ref[...] = (d_ya * o_ref[...] * dact).astype(BF)
        d_o = (d_ya * act).astype(BF)
        lane = lax.broadcasted_iota(jnp.int32, (1, 128), 1)
        row = lax.broadcasted_iota(jnp.int32, (4 * BLK, 1), 0)
        gsink = jnp.zeros((1, 128), F32)
        dq_parts, dk_parts, dv_parts = [], [], []
        for g in range(NKV):
            qs = _stack_heads(q, g)
            kb = _band(kp, kc, g)
            vbd = _band(vp, vc, g)
            p, ps = _softmax_band(qs, kb, mask, _sink_rows(sink_ref, g))
            dos = _stack_heads(d_o, g)
            dp = lax.dot_general(dos, vbd, (((1,), (1,)), ((), ())), preferred_element_type=F32)
            dr = jnp.sum(p * dp, axis=-1, keepdims=True)
            ds = (p * (dp - dr) * (HD ** -0.5)).astype(BF)
            sink_term = ps * dr
            for h in range(4):
                part = jnp.sum(jnp.where((row >= BLK * h) & (row < BLK * (h + 1)), sink_term, 0.0),
                               axis=0, keepdims=True)
                gsink = gsink - jnp.where(lane == 4 * g + h, part, 0.0)
            dv_parts.append(lax.dot_general(p.astype(BF), dos, (((0,), (0,)), ((), ())), preferred_element_type=F32))
            dq_parts.append(jnp.dot(ds, kb, preferred_element_type=F32))
            dk_parts.append(lax.dot_general(ds, qs, (((0,), (0,)), ((), ())), preferred_element_type=F32))
        gsink_ref[...] += gsink
        dkb = jnp.concatenate(dk_parts, axis=1)
        dvb = jnp.concatenate(dv_parts, axis=1)
        r_prev = pl.multiple_of(jnp.maximum(i - 1, 0) * BLK, BLK)
        r_cur = pl.multiple_of(i * BLK, BLK)
        dk_ref[pl.ds(r_prev, BLK), :] += dkb[0:BLK]
        dv_ref[pl.ds(r_prev, BLK), :] += dvb[0:BLK]
        dk_ref[pl.ds(r_cur, BLK), :] += dkb[BLK:2 * BLK]
        dv_ref[pl.ds(r_cur, BLK), :] += dvb[BLK:2 * BLK]
        dq = jnp.concatenate([dq_parts[g][BLK * h:BLK * (h + 1), :] for g in range(NKV) for h in range(4)], axis=1)
        dq_raw, g_qw = _norm_rope_bwd(dq, qraw_ref[...], qw_ref[...], jnp.tile(cos_ref[...], (1, 4)),
                                      jnp.tile(sin_ref[...], (1, 4)), bq_ref)
        dqraw_ref[...] = dq_raw.astype(BF)
        gqw_ref[...] += g_qw

    prev = lambda i: (jnp.maximum(i - 1, 0), 0)
    b512 = pl.BlockSpec((BLK, AW), _row)
    b128 = pl.BlockSpec((BLK, 128), _row)
    return pl.pallas_call(
        body, name="attn_bwd", grid=(NB,),
        out_shape=[jax.ShapeDtypeStruct((S, AW), BF), jax.ShapeDtypeStruct((S, AW), BF),
                   jax.ShapeDtypeStruct((S, KVW), F32), jax.ShapeDtypeStruct((S, KVW), F32),
                   jax.ShapeDtypeStruct((1, AW), F32), jax.ShapeDtypeStruct((1, 128), F32)],
        in_specs=[_SMEM, b512, pl.BlockSpec((BLK, KVW), prev), b128, pl.BlockSpec((BLK, KVW), prev), b128,
                  b512, b512, b512, b512, pl.BlockSpec((1, AW), _const), b128, b128, pl.BlockSpec((AW, AW), _const)],
        out_specs=[b512, b512, _VMEM, _VMEM, _VMEM, _VMEM],
        compiler_params=_params(True),
    )(sinks, qr, kr, kr, vb, vb, d_ya, ga, o, qraw, qw_t, cos_t, sin_t, bq)


HALO = 32


def _taps(ext_ref, base, weight_row, acc_rows):
    acc = None
    for b in range(8):
        amax = (CK - 1 - b) // 8
        win = ext_ref[pl.ds(base + b, acc_rows + 8 * amax), :]
        for a in range(amax + 1):
            term = win[8 * a:8 * a + acc_rows] * weight_row(8 * a + b)
            acc = term if acc is None else acc + term
    return acc


def _conv_fwd_call(z, gb, cwf, conv_b, ln_w, ln_b):
    def body(zc_in_ref, zp_ref, gb_ref, cw_ref, cb_ref, lw_ref, lb_ref, zc_ref, yb_ref, zext):
        i = pl.program_id(0)
        zext[0:HALO, :] = jnp.where(i > 0, zp_ref[TS - HALO:TS, :], 0.0)
        zext[HALO:HALO + TS, :] = zc_in_ref[...]
        zc = _taps(zext, HALO - (CK - 1), lambda k: cw_ref[k:k + 1, :], TS) + cb_ref[...]
        zc_ref[...] = zc
        mu = jnp.mean(zc, axis=-1, keepdims=True)
        dz = zc - mu
        rstd = lax.rsqrt(jnp.mean(dz * dz, axis=-1, keepdims=True) + EPS)
        zn = dz * rstd * lw_ref[...] + lb_ref[...]
        gbv = gb_ref[...]
        yb_ref[...] = (zn * _sigmoid(zn)) * (gbv * _sigmoid(gbv))

    t512 = pl.BlockSpec((TS, CW), _row)
    prev = pl.BlockSpec((TS, CW), lambda i: (jnp.maximum(i - 1, 0), 0))
    c512 = pl.BlockSpec((1, CW), _const)
    return pl.pallas_call(
        body, name="conv_fwd", grid=(NT,),
        out_shape=[jax.ShapeDtypeStruct((S, CW), F32), jax.ShapeDtypeStruct((S, CW), F32)],
        in_specs=[t512, prev, t512, pl.BlockSpec((CKP, CW), _const), c512, c512, c512],
        out_specs=[t512, t512],
        scratch_shapes=[pltpu.VMEM((TS + HALO, CW), F32)],
        compiler_params=_params(True),
    )(z, z, gb, cwf, conv_b, ln_w, ln_b)


def _conv_bwd_ln_call(d_yb, zc, gb, ln_w, ln_b):
    def body(dyb_ref, zc_ref, gb_ref, lw_ref, lb_ref, dzc_ref, dgb_ref, glw_ref, glb_ref, gcb_ref):
        i = pl.program_id(0)

        @pl.when(i == 0)
        def _():
            glw_ref[...] = jnp.zeros((1, CW), F32)
            glb_ref[...] = jnp.zeros((1, CW), F32)
            gcb_ref[...] = jnp.zeros((1, CW), F32)

        zc = zc_ref[...]
        mu = jnp.mean(zc, axis=-1, keepdims=True)
        dz = zc - mu
        rstd = lax.rsqrt(jnp.mean(dz * dz, axis=-1, keepdims=True) + EPS)
        zh = dz * rstd
        lw = lw_ref[...]
        zn = zh * lw + lb_ref[...]
        d_yb = dyb_ref[...]
        act_n, dact_n = _silu_and_grad(zn)
        act_g, dact_g = _silu_and_grad(gb_ref[...])
        dgb_ref[...] = (d_yb * act_n * dact_g).astype(BF)
        d_zn = d_yb * act_g * dact_n
        glw_ref[...] += jnp.sum(d_zn * zh, axis=0, keepdims=True)
        glb_ref[...] += jnp.sum(d_zn, axis=0, keepdims=True)
        dzh = d_zn * lw
        d_zc = rstd * (dzh - jnp.mean(dzh, axis=-1, keepdims=True) - zh * jnp.mean(dzh * zh, axis=-1, keepdims=True))
        dzc_ref[...] = d_zc
        gcb_ref[...] += jnp.sum(d_zc, axis=0, keepdims=True)

    t512 = pl.BlockSpec((TS, CW), _row)
    c512 = pl.BlockSpec((1, CW), _const)
    vec = jax.ShapeDtypeStruct((1, CW), F32)
    return pl.pallas_call(
        body, name="conv_bwd_ln", grid=(NT,),
        out_shape=[jax.ShapeDtypeStruct((S, CW), F32), jax.ShapeDtypeStruct((S, CW), BF), vec, vec, vec],
        in_specs=[t512, t512, t512, c512, c512],
        out_specs=[t512, t512, _VMEM, _VMEM, _VMEM],
        compiler_params=_params(True),
    )(d_yb, zc, gb, ln_w, ln_b)


def _conv_bwd_taps_call(d_zc, z, a, g, cwf):
    def body(dc_ref, dn_ref, zc_ref, zp_ref, a_ref, g_ref, cw_ref, da_ref, dg_ref, gcw_ref, dext, zext):
        i = pl.program_id(0)

        @pl.when(i == 0)
        def _():
            gcw_ref[...] = jnp.zeros((CKP, CW), F32)

        d_cur = dc_ref[...]
        dext[0:TS, :] = d_cur
        dext[TS:TS + HALO, :] = jnp.where(i < NT - 1, dn_ref[0:HALO, :], 0.0)
        zext[0:HALO, :] = jnp.where(i > 0, zp_ref[TS - HALO:TS, :], 0.0)
        zext[HALO:HALO + TS, :] = zc_ref[...]
        d_z = _taps(dext, 0, lambda j: cw_ref[CK - 1 - j:CK - j, :], TS)
        for b in range(8):
            amax = (CK - 1 - b) // 8
            win = zext[pl.ds(HALO - (CK - 1) + b, TS + 8 * amax), :]
            for aa in range(amax + 1):
                k = 8 * aa + b
                gcw_ref[k:k + 1, :] += jnp.sum(d_cur * win[8 * aa:8 * aa + TS], axis=0, keepdims=True)
        sg = _sigmoid(g_ref[...])
        da_ref[...] = (d_z * sg).astype(BF)
        dg_ref[...] = (d_z * a_ref[...] * sg * (1.0 - sg)).astype(BF)

    t512 = pl.BlockSpec((TS, CW), _row)
    prev = pl.BlockSpec((TS, CW), lambda i: (jnp.maximum(i - 1, 0), 0))
    nxt = pl.BlockSpec((TS, CW), lambda i: (jnp.minimum(i + 1, NT - 1), 0))
    return pl.pallas_call(
        body, name="conv_bwd_taps", grid=(NT,),
        out_shape=[jax.ShapeDtypeStruct((S, CW), BF), jax.ShapeDtypeStruct((S, CW), BF),
                   jax.ShapeDtypeStruct((CKP, CW), F32)],
        in_specs=[t512, nxt, t512, prev, t512, t512, pl.BlockSpec((CKP, CW), _const)],
        out_specs=[t512, t512, _VMEM],
        scratch_shapes=[pltpu.VMEM((TS + HALO, CW), F32), pltpu.VMEM((TS + HALO, CW), F32)],
        compiler_params=_params(True),
    )(d_zc, d_zc, z, z, a, g, cwf)


def _out_loss_call(o, ga, yb, x2, tgt, mod, w_out_full):
    def body(o_ref, ga_ref, yb_ref, x_ref, t_ref, mod_ref, w_ref,
             dout_ref, dya_ref, dyb_ref, gw_ref, loss_ref, dgate_ref):
        i = pl.program_id(0)

        @pl.when(i == 0)
        def _():
            gw_ref[...] = jnp.zeros((D, D), F32)
            loss_ref[...] = jnp.zeros((1, 128), F32)
            dgate_ref[...] = jnp.zeros((1, D), F32)

        gav = ga_ref[...]
        ya = o_ref[...] * (gav * _sigmoid(gav))
        ycat = jnp.concatenate([ya, yb_ref[...]], axis=1).astype(BF)
        w = w_ref[...]
        y = jnp.dot(ycat, w, preferred_element_type=F32)
        gate = mod_ref[:, 2 * D:3 * D]
        diff = x_ref[...] + gate * y - t_ref[...]
        sq = jnp.sum(jnp.sum(diff * diff, axis=1, keepdims=True), axis=0, keepdims=True)
        loss_ref[...] += jnp.broadcast_to(sq, (1, 128))
        d_out = diff * (1.0 / D)
        dout_ref[...] = d_out
        dgate_ref[...] += jnp.sum(d_out * y, axis=0, keepdims=True)
        dy = (d_out * gate).astype(BF)
        d_ycat = lax.dot_general(dy, w, (((1,), (1,)), ((), ())), preferred_element_type=F32)
        dya_ref[...] = d_ycat[:, 0:AW]
        dyb_ref[...] = d_ycat[:, AW:D]
        gw_ref[...] += lax.dot_general(ycat, dy, (((0,), (0,)), ((), ())), preferred_element_type=F32)

    t512 = pl.BlockSpec((TS, 512), _row)
    t1024 = pl.BlockSpec((TS, D), _row)
    return pl.pallas_call(
        body, name="out_loss", grid=(NT,),
        out_shape=[jax.ShapeDtypeStruct((S, D), F32), jax.ShapeDtypeStruct((S, AW), F32),
                   jax.ShapeDtypeStruct((S, CW), F32), jax.ShapeDtypeStruct((D, D), F32),
                   jax.ShapeDtypeStruct((1, 128), F32), jax.ShapeDtypeStruct((1, D), F32)],
        in_specs=[t512, t512, t512, t1024, t1024, pl.BlockSpec((1, 3 * D), _const),
                  pl.BlockSpec((D, D), _const, pipeline_mode=pl.Buffered(1))],
        out_specs=[t1024, t512, t512, _VMEM, _VMEM, _VMEM],
        compiler_params=_params(True),
    )(o, ga, yb, x2, tgt, mod, w_out_full)


SM_ROWS = 8
PIECES = ((0, 512), (512, 640), (640, 768), (768, 1280), (1280, 1792), (1792, 2304), (2304, 2816))


def _bwd_in_call(dqraw, dk, dv, dga, da, dg, dgb, kraw, kw_t, cos_t, sin_t, bk, h, wt_full, x2, d_out, mod, norm_w,
                 gw_out, gcw, glw, glb, gcb, gqw, gsink, dgate, loss_p, cact_all):
    def body(dq_ref, dk_ref, dv_ref, dga_ref, da_ref, dg_ref, dgb_ref, kraw_ref, kw_ref, cos_ref, sin_ref, bk_ref,
             h_ref, wt_ref, x_ref, dout_ref, mod_ref, nw_ref, gwout_ref, gcw_ref, glw_ref, glb_ref, gcb_ref, gqw_ref,
             gsink_ref, dgate_ref, loss_ref, cact_ref,
             gx_ref, o_gwin, o_gwout, o_gwada, o_gbada, o_gnw, o_gqw, o_gkw, o_gsink, o_gcw, o_gcb, o_glw, o_glb,
             o_loss,
             acc, win_send, win_recv, wout_send, wout_recv, sm_buf, cw_buf, dmod_all, vec_acc, gkw_acc,
             ws_s, ws_r, wo_s, wo_r, sm_s, sm_r, cw_s, cw_r):
        i = pl.program_id(0)
        pos = _mesh_pos()
        x, y, cc = pos
        me = 4 * x + 2 * y + cc

        def rs_copy(k, which):
            px, py, pc = _xor_peer(pos, k)
            dst = 4 * px + 2 * py + pc
            if which == "in":
                src = win_send.at[pl.ds(pl.multiple_of(dst * IN_SHARD, 16), IN_SHARD), :]
                return pltpu.make_async_remote_copy(src_ref=src, dst_ref=win_recv.at[k - 1], send_sem=ws_s.at[k - 1],
                                                    recv_sem=ws_r.at[k - 1], device_id=(px, py, pc), device_id_type=MESH)
            src = wout_send.at[pl.ds(pl.multiple_of(dst * OUT_SHARD, 16), OUT_SHARD), :]
            return pltpu.make_async_remote_copy(src_ref=src, dst_ref=wout_recv.at[k - 1], send_sem=wo_s.at[k - 1],
                                                recv_sem=wo_r.at[k - 1], device_id=(px, py, pc), device_id_type=MESH)

        def dproj_pieces():
            dk_raw, g_kw = _norm_rope_bwd(dk_ref[...], kraw_ref[...], kw_ref[...], cos_ref[...], sin_ref[...], bk_ref)
            return [dq_ref[...], dk_raw.astype(BF), dv_ref[...].astype(BF), dga_ref[...], da_ref[...], dg_ref[...],
                    dgb_ref[...]], g_kw

        @pl.when(i == 0)
        def _():
            acc[...] = jnp.zeros((INW, D), F32)
            vec_acc[...] = jnp.zeros((8, D), F32)
            gkw_acc[...] = jnp.zeros((1, KVW), F32)
            wout_send[...] = gwout_ref[...].astype(BF)
            for k in range(1, NDEV):
                rs_copy(k, "out").start()

        @pl.when(i < NT)
        def _():
            pieces, g_kw = dproj_pieces()
            gkw_acc[...] += g_kw
            hv = h_ref[...]
            for (lo, hi), piece in zip(PIECES, pieces):
                acc[lo:hi, :] += lax.dot_general(piece, hv, (((0,), (0,)), ((), ())), preferred_element_type=F32)

        @pl.when(i == NT - 1)
        def _():
            for lo, hi in PIECES:
                win_send[lo:hi, :] = acc[lo:hi, :].astype(BF)
            for k in range(1, NDEV):
                rs_copy(k, "in").start()

        @pl.when(i >= NT)
        def _():
            pieces, _ = dproj_pieces()
            dproj = jnp.concatenate(pieces, axis=1)
            d_h = jnp.dot(dproj, wt_ref[...], preferred_element_type=F32)
            xv = x_ref[...]
            scale = mod_ref[:, D:2 * D]
            nw = nw_ref[...]
            r = lax.rsqrt(jnp.mean(xv * xv, axis=-1, keepdims=True) + EPS)
            xn = xv * r
            vec_acc[0:1, :] += jnp.sum(d_h, axis=0, keepdims=True)
            vec_acc[1:2, :] += jnp.sum(d_h * (xn * nw), axis=0, keepdims=True)
            d_u = d_h * (1.0 + scale)
            vec_acc[2:3, :] += jnp.sum(d_u * xn, axis=0, keepdims=True)
            d_xn = d_u * nw
            gx_ref[...] = dout_ref[...] + r * (d_xn - xn * jnp.mean(d_xn * xn, axis=-1, keepdims=True))

        @pl.when(i == 2 * NT - 1)
        def _():
            ch_sm = (_slab(sm_buf), sm_s, sm_r)
            ch_cw = (_slab(cw_buf), cw_s, cw_r)
            z128 = jnp.zeros((1, 128), F32)
            row4 = jnp.concatenate([glw_ref[...], glb_ref[...]], axis=1)
            row5 = jnp.concatenate([gcb_ref[...], gqw_ref[...]], axis=1)
            row6 = jnp.concatenate([gkw_acc[...], gsink_ref[...], loss_ref[...]] + [z128] * 5, axis=1)
            sm_buf[me] = jnp.concatenate([vec_acc[0:2, :], dgate_ref[...], vec_acc[2:3, :], row4, row5, row6,
                                          jnp.zeros((1, D), F32)], axis=0)
            f_sm = _ag_start(ch_sm, pos)
            cw_buf[me] = gcw_ref[...]
            f_cw = _ag_start(ch_cw, pos)
            _ag_finish(ch_sm, pos, f_sm)
            _ag_finish(ch_cw, pos, f_cw)
            tot = sm_buf[0]
            cw_tot = cw_buf[0]
            for d in range(1, NDEV):
                tot = tot + sm_buf[d]
                cw_tot = cw_tot + cw_buf[d]
            o_gbada[...] = jnp.concatenate([tot[0:1, :], tot[1:2, :], tot[2:3, :]], axis=1)
            o_gnw[...] = tot[3:4, :]
            o_glw[...] = tot[4:5, 0:CW]
            o_glb[...] = tot[4:5, CW:D]
            o_gcb[...] = tot[5:6, 0:CW]
            gq = tot[5:6, CW:CW + HD]
            for hh in range(1, NQ):
                gq = gq + tot[5:6, CW + HD * hh:CW + HD * (hh + 1)]
            o_gqw[...] = gq
            o_gkw[...] = tot[6:7, 0:HD] + tot[6:7, HD:2 * HD]
            o_gsink[...] = tot[6:7, 128:128 + NQ]
            o_loss[...] = tot[6:7, 256:384] * (0.5 / D)
            mine = jnp.zeros((CK, CONV_SHARD), F32)
            for d in range(NDEV):
                mine = mine + jnp.where(me == d, cw_tot[0:CK, CONV_SHARD * d:CONV_SHARD * (d + 1)], 0.0)
            o_gcw[...] = mine
            for d in range(NDEV):
                dmod_all[d:d + 1, :] = jnp.concatenate([sm_buf[d, 0:1, :], sm_buf[d, 1:2, :], sm_buf[d, 2:3, :]],
                                                       axis=1)
            col0 = pl.multiple_of(me * ADA_SHARD, 128)
            o_gwada[...] = lax.dot_general(cact_ref[...], dmod_all[:, pl.ds(col0, ADA_SHARD)], (((0,), (0,)), ((), ())),
                                           preferred_element_type=F32, precision=lax.Precision.HIGHEST)

            for k in range(1, NDEV):
                rs_copy(k, "in").wait_recv()
                rs_copy(k, "out").wait_recv()
            g_in = acc[pl.ds(pl.multiple_of(me * IN_SHARD, 8), IN_SHARD), :]
            g_out = gwout_ref[pl.ds(pl.multiple_of(me * OUT_SHARD, 8), OUT_SHARD), :]
            for k in range(1, NDEV):
                g_in = g_in + win_recv[k - 1].astype(F32)
                g_out = g_out + wout_recv[k - 1].astype(F32)
            o_gwin[...] = g_in
            o_gwout[...] = g_out
            for k in range(1, NDEV):
                rs_copy(k, "in").wait_send()
                rs_copy(k, "out").wait_send()

    half = lambda i: (i % NT, 0)
    late = lambda i: (jnp.maximum(i - NT, 0), 0)
    t512 = pl.BlockSpec((TS, 512), half)
    t128 = pl.BlockSpec((TS, 128), half)
    l1024 = pl.BlockSpec((TS, D), late)
    sem7 = pltpu.SemaphoreType.DMA((7,))
    sds = jax.ShapeDtypeStruct
    return pl.pallas_call(
        body, name="bwd_in", grid=(2 * NT,),
        out_shape=[sds((S, D), F32), sds((IN_SHARD, D), F32), sds((OUT_SHARD, D), F32), sds((D, ADA_SHARD), F32),
                   sds((1, 3 * D), F32), sds((1, D), F32), sds((1, HD), F32), sds((1, HD), F32), sds((1, NQ), F32),
                   sds((CK, CONV_SHARD), F32), sds((1, CW), F32), sds((1, CW), F32), sds((1, CW), F32),
                   sds((1, 128), F32)],
        in_specs=[t512, t128, t128, t512, t512, t512, t512, t128, pl.BlockSpec((1, KVW), _const), t128, t128,
                  pl.BlockSpec((KVW, KVW), _const), pl.BlockSpec((TS, D), half),
                  pl.BlockSpec((INW, D), _const, pipeline_mode=pl.Buffered(1)), l1024, l1024,
                  pl.BlockSpec((1, 3 * D), _const), pl.BlockSpec((1, D), _const)] + [_VMEM] * 10,
        out_specs=[l1024] + [_VMEM] * 13,
        scratch_shapes=[pltpu.VMEM((INW, D), F32), pltpu.VMEM((INW, D), BF), pltpu.VMEM((NDEV - 1, IN_SHARD, D), BF),
                        pltpu.VMEM((D, D), BF), pltpu.VMEM((NDEV - 1, OUT_SHARD, D), BF),
                        pltpu.VMEM((NDEV, SM_ROWS, D), F32), pltpu.VMEM((NDEV, CKP, CW), F32),
                        pltpu.VMEM((NDEV, 3 * D), F32), pltpu.VMEM((8, D), F32), pltpu.VMEM((1, KVW), F32)]
        + [sem7] * 8,
        compiler_params=pltpu.CompilerParams(dimension_semantics=("arbitrary",), vmem_limit_bytes=BIG_VMEM_LIMIT),
    )(dqraw, dk, dv, dga, da, dg, dgb, kraw, kw_t, cos_t, sin_t, bk, h, wt_full, x2, d_out, mod, norm_w,
      gw_out, gcw, glw, glb, gcb, gqw, gsink, dgate, loss_p, cact_all)


def _adam_call(ws, gs, ms, vs):
    n = len(ws)
    bc1 = 1.0 - ADAM_B1 ** ADAM_STEP
    bc2 = 1.0 - ADAM_B2 ** ADAM_STEP

    def body(*refs):
        ins, outs = refs[:4 * n], refs[4 * n:]
        for j in range(n):
            w, g, m, v = (ins[j][...], ins[n + j][...], ins[2 * n + j][...], ins[3 * n + j][...])
            m_new = ADAM_B1 * m + (1.0 - ADAM_B1) * g
            v_new = ADAM_B2 * v + (1.0 - ADAM_B2) * (g * g)
            m_hat = m_new / bc1
            v_hat = v_new / bc2
            outs[j][...] = -ADAM_LR * (m_hat / (jnp.sqrt(v_hat) + ADAM_EPS) + ADAM_WD * w)
            outs[n + j][...] = m_new
            outs[2 * n + j][...] = v_new

    shapes = [jax.ShapeDtypeStruct(w.shape, F32) for w in ws]
    return pl.pallas_call(
        body, name="adam",
        out_shape=shapes * 3, in_specs=[_VMEM] * (4 * n), out_specs=[_VMEM] * (3 * n),
        compiler_params=_params(),
    )(*ws, *gs, *ms, *vs)


def _rope_tables():
    inv = ROPE_THETA ** (-jnp.arange(0, HD, 2, dtype=F32) / HD)
    ang = jnp.arange(S, dtype=F32)[:, None] * inv[None, :]
    cos, sin = jnp.cos(ang), jnp.sin(ang)
    cos64 = jnp.concatenate([cos, cos], axis=-1)
    sin64 = jnp.concatenate([-sin, sin], axis=-1)
    return jnp.tile(cos64, (1, 2)), jnp.tile(sin64, (1, 2))


def _group_matrix(width):
    idx = jnp.arange(width) // HD
    return jnp.where(idx[:, None] == idx[None, :], 1.0 / HD, 0.0).astype(BF)


def kernel(x, c, w_ada, b_ada, norm_w, w_in, q_norm_w, k_norm_w, sinks, conv_w, conv_b, ln_w, ln_b, w_out, loss_target, m_w_ada, m_b_ada, m_norm_w, m_w_in, m_q_norm_w, m_k_norm_w, m_sinks, m_conv_w, m_conv_b, m_ln_w, m_ln_b, m_w_out, v_w_ada, v_b_ada, v_norm_w, v_w_in, v_q_norm_w, v_k_norm_w, v_sinks, v_conv_w, v_conv_b, v_ln_w, v_ln_b, v_w_out):
    x2 = x[0]
    tgt = loss_target[0]
    cos_t, sin_t = _rope_tables()
    bq = _group_matrix(AW)
    bk = _group_matrix(KVW)
    qw_t = jnp.tile(q_norm_w, (1, NQ))
    kw_t = jnp.tile(k_norm_w, (1, NKV))

    tr = lambda t: jnp.swapaxes(t[0], 0, 1)
    wt_full, w_out_full, cwf, mod, cact_all = _gather_call(tr(w_in), w_out[0], conv_w[0], c, w_ada[0], b_ada)

    h, qraw, kraw, ga, a, g, gb, qr, kr, vb, z = _fwd_in_call(x2, mod, norm_w, qw_t, kw_t, cos_t, sin_t, bq, bk, wt_full)
    o = _attn_fwd_call(sinks, qr, kr, vb)
    zc, yb = _conv_fwd_call(z, gb, cwf, conv_b, ln_w, ln_b)
    d_out, d_ya, d_yb, gw_out, loss_p, dgate = _out_loss_call(o, ga, yb, x2, tgt, mod, w_out_full)

    d_zc, dgb, glw, glb, gcb = _conv_bwd_ln_call(d_yb, zc, gb, ln_w, ln_b)
    da, dg, gcw = _conv_bwd_taps_call(d_zc, z, a, g, cwf)
    dqraw, dga, dk, dv, gqw, gsink = _attn_bwd_call(sinks, qr, kr, vb, d_ya, ga, o, qraw, qw_t, cos_t, sin_t, bq)
    (grad_x, g_w_in_t, g_w_out, g_w_ada, g_b_ada, g_norm_w, g_qw, g_kw, g_sinks, g_conv_w, g_conv_b, g_ln_w, g_ln_b,
     loss_v) = _bwd_in_call(dqraw, dk, dv, dga, da, dg, dgb, kraw, kw_t, cos_t, sin_t, bk, h, wt_full, x2, d_out, mod,
                            norm_w, gw_out, gcw, glw, glb, gcb, gqw, gsink, dgate, loss_p, cact_all)

    ws = [w_ada[0], b_ada, norm_w, tr(w_in), q_norm_w, k_norm_w, sinks, conv_w[0], conv_b, ln_w, ln_b, w_out[0]]
    gs = [g_w_ada, g_b_ada, g_norm_w, g_w_in_t, g_qw, g_kw, g_sinks, g_conv_w, g_conv_b, g_ln_w, g_ln_b, g_w_out]
    ms = [m_w_ada[0], m_b_ada, m_norm_w, tr(m_w_in), m_q_norm_w, m_k_norm_w, m_sinks, m_conv_w[0], m_conv_b, m_ln_w,
          m_ln_b, m_w_out[0]]
    vs = [v_w_ada[0], v_b_ada, v_norm_w, tr(v_w_in), v_q_norm_w, v_k_norm_w, v_sinks, v_conv_w[0], v_conv_b, v_ln_w,
          v_ln_b, v_w_out[0]]
    upd = _adam_call(ws, gs, ms, vs)
    n = len(ws)
    shaped = [w_ada, b_ada, norm_w, w_in, q_norm_w, k_norm_w, sinks, conv_w, conv_b, ln_w, ln_b, w_out]
    W_IN_POS = 3

    def like(vals):
        vals = [jnp.swapaxes(v, 0, 1) if j == W_IN_POS else v for j, v in enumerate(vals)]
        return [v.reshape(s.shape) for v, s in zip(vals, shaped)]

    return (loss_v[0, 0], grad_x[None], *like(gs), *like(upd[0:n]), *like(upd[n:2 * n]), *like(upd[2 * n:3 * n]))
```

```python
import functools

import jax
import jax.numpy as jnp
from jax import lax
from jax.experimental import pallas as pl
from jax.experimental.pallas import tpu as pltpu

S = 2048
D = 1024
NDEV = 8
HD = 64
NQ = 8
NKV = 2
AW = 512
KVW = 128
CW = 512
INW = 2816
IN_SHARD = INW // NDEV
ADA_SHARD = 3 * D // NDEV
OUT_SHARD = D // NDEV
CONV_SHARD = CW // NDEV
CK = 31
CKP = 32
BLK = 128
TS = 256
NT = S // TS
NB = S // BLK
EPS = 1e-6
ROPE_THETA = 10000.0
NEG = -1e30
BF = jnp.bfloat16
F32 = jnp.float32

ADAM_LR = 0.001
ADAM_B1 = 0.9
ADAM_B2 = 0.999
ADAM_EPS = 1e-08
ADAM_WD = 0.01
ADAM_STEP = 10

VMEM_LIMIT = 56 * 1024 * 1024
BIG_VMEM_LIMIT = 62 * 1024 * 1024
MESH = pl.DeviceIdType.MESH

_VMEM = pl.BlockSpec(memory_space=pltpu.VMEM)
_SMEM = pl.BlockSpec(memory_space=pltpu.SMEM)


def _params(grid=False):
    if grid:
        return pltpu.CompilerParams(dimension_semantics=("arbitrary",), vmem_limit_bytes=VMEM_LIMIT)
    return pltpu.CompilerParams(vmem_limit_bytes=VMEM_LIMIT)


def _row(i):
    return (i, 0)


def _const(i):
    return (0, 0)


def _sigmoid(t):
    return 1.0 / (1.0 + jnp.exp(-t))


def _silu_and_grad(t):
    sg = _sigmoid(t)
    return t * sg, sg * (1.0 + t * (1.0 - sg))


def _group_mean(t, b_ref):
    hi = t.astype(BF)
    lo = (t - hi.astype(F32)).astype(BF)
    b = b_ref[...]
    return jnp.dot(hi, b, preferred_element_type=F32) + jnp.dot(lo, b, preferred_element_type=F32)


def _partner(t):
    w = t.shape[-1]
    lane = lax.broadcasted_iota(jnp.int32, t.shape, 1)
    first = (lane & 32) == 0
    return jnp.where(first, pltpu.roll(t, w - 32, 1), pltpu.roll(t, 32, 1))


def _norm_rope_fwd(t, w_t, cos, sin, b_ref):
    r = lax.rsqrt(_group_mean(t * t, b_ref) + EPS)
    tn = t * r * w_t
    return tn * cos + _partner(tn) * sin


def _norm_rope_bwd(d_out, t, w_t, cos, sin, b_ref):
    d_tn = d_out * cos + _partner(d_out * sin)
    r = lax.rsqrt(_group_mean(t * t, b_ref) + EPS)
    th = t * r
    g_w = jnp.sum(d_tn * th, axis=0, keepdims=True)
    d_th = d_tn * w_t
    d_t = r * (d_th - th * _group_mean(d_th * th, b_ref))
    return d_t, g_w


def _mesh_pos():
    return lax.axis_index("x"), lax.axis_index("y"), lax.axis_index("c")


def _ag_copy(chan, k, block, to):
    blk, send_sems, recv_sems = chan
    ref = blk(*block)
    return pltpu.make_async_remote_copy(src_ref=ref, dst_ref=ref, send_sem=send_sems.at[k],
                                        recv_sem=recv_sems.at[k], device_id=to, device_id_type=MESH)


def _ag_start(chan, pos):
    x, y, c = pos
    me = (x, y, c)
    chips = [(1 - x, y), (x, 1 - y), (1 - x, 1 - y)]
    first = [_ag_copy(chan, 0, me, (x, y, 1 - c))]
    first += [_ag_copy(chan, 1 + j, me, (*chip, c)) for j, chip in enumerate(chips)]
    for cp in first:
        cp.start()
    return first


def _ag_finish(chan, pos, first):
    x, y, c = pos
    me = (x, y, c)
    sibling = (x, y, 1 - c)
    chips = [(1 - x, y), (x, 1 - y), (1 - x, 1 - y)]
    passed = [_ag_copy(chan, 4 + j, (*chip, c), sibling) for j, chip in enumerate(chips)]
    for j, chip in enumerate(chips):
        _ag_copy(chan, 1 + j, (*chip, c), me).wait_recv()
        passed[j].start()
    _ag_copy(chan, 0, sibling, me).wait_recv()
    for j, chip in enumerate(chips):
        _ag_copy(chan, 4 + j, (*chip, 1 - c), me).wait_recv()
    for cp in first + passed:
        cp.wait_send()


def _slab(buf):
    return lambda px, py, pc: buf.at[4 * px + 2 * py + pc]


def _row_block(buf, rows, align):
    return lambda px, py, pc: buf.at[pl.ds(pl.multiple_of((4 * px + 2 * py + pc) * rows, align), rows), :]


def _xor_peer(pos, k):
    x, y, c = pos
    kx, ky, kc = (k >> 2) & 1, (k >> 1) & 1, k & 1
    return (1 - x if kx else x, 1 - y if ky else y, 1 - c if kc else c)


def _gather_call(w_in_t, w_out_s, conv_w_s, c, w_ada_s, b_ada):
    def body(win_ref, wout_ref, cw_ref, c_ref, wada_ref, bada_ref,
             wtf_ref, woutf_ref, cwf_ref, mod_ref, cact_ref,
             cw_buf, ca_buf, mp_buf,
             s0, r0, s1, r1, s2, r2, s3, r3, s4, r4):
        pos = _mesh_pos()
        x, y, cc = pos
        me = 4 * x + 2 * y + cc
        ch_win = (_row_block(wtf_ref, IN_SHARD, 16), s0, r0)
        ch_wout = (_row_block(woutf_ref, OUT_SHARD, 16), s1, r1)
        ch_cw = (_slab(cw_buf), s2, r2)
        ch_ca = (_slab(ca_buf), s3, r3)
        ch_mp = (_slab(mp_buf), s4, r4)

        cv = c_ref[...]
        c_act = cv * _sigmoid(cv)
        ca_buf[me] = jnp.broadcast_to(c_act, (8, D))
        f_ca = _ag_start(ch_ca, pos)
        wtf_ref[pl.ds(pl.multiple_of(me * IN_SHARD, 16), IN_SHARD), :] = win_ref[...].astype(BF)
        f_win = _ag_start(ch_win, pos)
        woutf_ref[pl.ds(pl.multiple_of(me * OUT_SHARD, 16), OUT_SHARD), :] = wout_ref[...].astype(BF)
        f_wout = _ag_start(ch_wout, pos)
        cw_buf[me] = cw_ref[...]
        f_cw = _ag_start(ch_cw, pos)

        _ag_finish(ch_ca, pos, f_ca)
        cact_all = jnp.concatenate([ca_buf[d, 0:1, :] for d in range(NDEV)], axis=0)
        cact_ref[...] = cact_all
        col0 = pl.multiple_of(me * ADA_SHARD, 128)
        mp = jnp.dot(cact_all, wada_ref[...], preferred_element_type=F32,
                     precision=lax.Precision.HIGHEST) + bada_ref[:, pl.ds(col0, ADA_SHARD)]
        mp_buf[me] = mp
        f_mp = _ag_start(ch_mp, pos)

        _ag_finish(ch_cw, pos, f_cw)
        _ag_finish(ch_mp, pos, f_mp)
        for d in range(NDEV):
            cwf_ref[0:CK, CONV_SHARD * d:CONV_SHARD * (d + 1)] = cw_buf[d]
        cwf_ref[CK:CKP, :] = jnp.zeros((CKP - CK, CW), F32)
        mod_ref[...] = jnp.concatenate([mp_buf[d, pl.ds(me, 1), :] for d in range(NDEV)], axis=1)
        _ag_finish(ch_win, pos, f_win)
        _ag_finish(ch_wout, pos, f_wout)

    sem = pltpu.SemaphoreType.DMA((7,))
    return pl.pallas_call(
        body, name="gather",
        out_shape=[jax.ShapeDtypeStruct((INW, D), BF), jax.ShapeDtypeStruct((D, D), BF),
                   jax.ShapeDtypeStruct((CKP, CW), F32), jax.ShapeDtypeStruct((1, 3 * D), F32),
                   jax.ShapeDtypeStruct((NDEV, D), F32)],
        in_specs=[_VMEM] * 6, out_specs=[_VMEM] * 5,
        scratch_shapes=[pltpu.VMEM((NDEV, CK, CONV_SHARD), F32), pltpu.VMEM((NDEV, 8, D), F32),
                        pltpu.VMEM((NDEV, 8, ADA_SHARD), F32)] + [sem] * 10,
        compiler_params=_params(),
    )(w_in_t, w_out_s, conv_w_s, c, w_ada_s, b_ada)


def _fwd_in_call(x2, mod, norm_w, qw_t, kw_t, cos_t, sin_t, bq, bk, wt_full):
    def body(x_ref, mod_ref, nw_ref, qw_ref, kw_ref, cos_ref, sin_ref, bq_ref, bk_ref, w_ref,
             h_ref, qraw_ref, kraw_ref, ga_ref, a_ref, g_ref, gb_ref, qr_ref, kr_ref, vb_ref, z_ref):
        xv = x_ref[...]
        shift = mod_ref[:, 0:D]
        scale = mod_ref[:, D:2 * D]
        r = lax.rsqrt(jnp.mean(xv * xv, axis=-1, keepdims=True) + EPS)
        h = (xv * r * nw_ref[...]) * (1.0 + scale) + shift
        hb = h.astype(BF)
        h_ref[...] = hb

        def proj(lo, hi):
            return lax.dot_general(hb, w_ref[lo:hi, :], (((1,), (1,)), ((), ())), preferred_element_type=F32)

        cos = cos_ref[...]
        sin = sin_ref[...]
        q = proj(0, 512)
        qraw_ref[...] = q
        qr_ref[...] = _norm_rope_fwd(q, qw_ref[...], jnp.tile(cos, (1, 4)), jnp.tile(sin, (1, 4)), bq_ref).astype(BF)
        k = proj(512, 640)
        kraw_ref[...] = k
        kr_ref[...] = _norm_rope_fwd(k, kw_ref[...], cos, sin, bk_ref).astype(BF)
        vb_ref[...] = proj(640, 768).astype(BF)
        ga_ref[...] = proj(768, 1280)
        a = proj(1280, 1792)
        g = proj(1792, 2304)
        a_ref[...] = a
        g_ref[...] = g
        z_ref[...] = a * _sigmoid(g)
        gb_ref[...] = proj(2304, 2816)

    t512 = pl.BlockSpec((TS, 512), _row)
    t128 = pl.BlockSpec((TS, 128), _row)
    return pl.pallas_call(
        body, name="fwd_in", grid=(NT,),
        out_shape=[jax.ShapeDtypeStruct((S, D), BF), jax.ShapeDtypeStruct((S, AW), F32),
                   jax.ShapeDtypeStruct((S, KVW), F32), jax.ShapeDtypeStruct((S, AW), F32),
                   jax.ShapeDtypeStruct((S, CW), F32), jax.ShapeDtypeStruct((S, CW), F32),
                   jax.ShapeDtypeStruct((S, CW), F32), jax.ShapeDtypeStruct((S, AW), BF),
                   jax.ShapeDtypeStruct((S, KVW), BF), jax.ShapeDtypeStruct((S, KVW), BF),
                   jax.ShapeDtypeStruct((S, CW), F32)],
        in_specs=[pl.BlockSpec((TS, D), _row), pl.BlockSpec((1, 3 * D), _const), pl.BlockSpec((1, D), _const),
                  pl.BlockSpec((1, AW), _const), pl.BlockSpec((1, KVW), _const), t128, t128,
                  pl.BlockSpec((AW, AW), _const), pl.BlockSpec((KVW, KVW), _const),
                  pl.BlockSpec((INW, D), _const, pipeline_mode=pl.Buffered(1))],
        out_specs=[pl.BlockSpec((TS, D), _row), t512, t128, t512, t512, t512, t512, t512, t128, t128, t512],
        compiler_params=_params(True),
    )(x2, mod, norm_w, qw_t, kw_t, cos_t, sin_t, bq, bk, wt_full)


def _band_mask(i):
    qi = lax.broadcasted_iota(jnp.int32, (4 * BLK, 2 * BLK), 0) & (BLK - 1)
    kj = lax.broadcasted_iota(jnp.int32, (4 * BLK, 2 * BLK), 1)
    dist = qi + BLK - kj
    return (dist >= 0) & (dist < BLK) & ((kj >= BLK) | (i > 0))


def _sink_rows(sink_ref, g):
    row = lax.broadcasted_iota(jnp.int32, (4 * BLK, 1), 0)
    return jnp.where(row < BLK, sink_ref[0, 4 * g],
                     jnp.where(row < 2 * BLK, sink_ref[0, 4 * g + 1],
                               jnp.where(row < 3 * BLK, sink_ref[0, 4 * g + 2], sink_ref[0, 4 * g + 3])))


def _stack_heads(t, g):
    return jnp.concatenate([t[:, HD * (4 * g + h):HD * (4 * g + h + 1)] for h in range(4)], axis=0)


def _band(prev, cur, g):
    return jnp.concatenate([prev[:, HD * g:HD * (g + 1)], cur[:, HD * g:HD * (g + 1)]], axis=0)


def _softmax_band(qs, kb, mask, sink):
    s = lax.dot_general(qs, kb, (((1,), (1,)), ((), ())), preferred_element_type=F32) * (HD ** -0.5)
    s = jnp.where(mask, s, NEG)
    m = jnp.maximum(jnp.max(s, axis=-1, keepdims=True), sink)
    e = jnp.exp(s - m)
    es = jnp.exp(sink - m)
    inv = 1.0 / (jnp.sum(e, axis=-1, keepdims=True) + es)
    return e * inv, es * inv


def _attn_fwd_call(sinks, qr, kr, vb):
    def body(sink_ref, q_ref, kp_ref, kc_ref, vp_ref, vc_ref, o_ref):
        i = pl.program_id(0)
        mask = _band_mask(i)
        q = q_ref[...]
        kp, kc, vp, vc = kp_ref[...], kc_ref[...], vp_ref[...], vc_ref[...]
        for g in range(NKV):
            p, _ = _softmax_band(_stack_heads(q, g), _band(kp, kc, g), mask, _sink_rows(sink_ref, g))
            o = jnp.dot(p.astype(BF), _band(vp, vc, g), preferred_element_type=F32)
            for h in range(4):
                o_ref[:, HD * (4 * g + h):HD * (4 * g + h + 1)] = o[BLK * h:BLK * (h + 1), :]

    prev = lambda i: (jnp.maximum(i - 1, 0), 0)
    return pl.pallas_call(
        body, name="attn_fwd", grid=(NB,),
        out_shape=jax.ShapeDtypeStruct((S, AW), F32),
        in_specs=[_SMEM, pl.BlockSpec((BLK, AW), _row), pl.BlockSpec((BLK, KVW), prev), pl.BlockSpec((BLK, KVW), _row),
                  pl.BlockSpec((BLK, KVW), prev), pl.BlockSpec((BLK, KVW), _row)],
        out_specs=pl.BlockSpec((BLK, AW), _row),
        compiler_params=_params(True),
    )(sinks, qr, kr, kr, vb, vb)


def _attn_bwd_call(sinks, qr, kr, vb, d_ya, ga, o, qraw, qw_t, cos_t, sin_t, bq):
    def body(sink_ref, q_ref, kp_ref, kc_ref, vp_ref, vc_ref, dya_ref, ga_ref, o_ref, qraw_ref, qw_ref,
             cos_ref, sin_ref, bq_ref,
             dqraw_ref, dga_ref, dk_ref, dv_ref, gqw_ref, gsink_ref):
        i = pl.program_id(0)

        @pl.when(i == 0)
        def _():
            dk_ref[...] = jnp.zeros((S, KVW), F32)
            dv_ref[...] = jnp.zeros((S, KVW), F32)
            gqw_ref[...] = jnp.zeros((1, AW), F32)
            gsink_ref[...] = jnp.zeros((1, 128), F32)

        mask = _band_mask(i)
        q = q_ref[...]
        kp, kc, vp, vc = kp_ref[...], kc_ref[...], vp_ref[...], vc_ref[...]
        d_ya = dya_ref[...]
        act, dact = _silu_and_grad(ga_ref[...])
        dga_ref[...] = (d_ya * o_ref[...] * dact).astype(BF)
        d_o = (d_ya * act).astype(BF)
        lane = lax.broadcasted_iota(jnp.int32, (1, 128), 1)
        row = lax.broadcasted_iota(jnp.int32, (4 * BLK, 1), 0)
        gsink = jnp.zeros((1, 128), F32)
        dq_parts, dk_parts, dv_parts = [], [], []
        for g in range(NKV):
            qs = _stack_heads(q, g)
            kb = _band(kp, kc, g)
            vbd = _band(vp, vc, g)
            p, ps = _softmax_band(qs, kb, mask, _sink_rows(sink_ref, g))
            dos = _stack_heads(d_o, g)
            dp = lax.dot_general(dos, vbd, (((1,), (1,)), ((), ())), preferred_element_type=F32)
            dr = jnp.sum(p * dp, axis=-1, keepdims=True)
            ds = (p * (dp - dr) * (HD ** -0.5)).astype(BF)
            sink_term = ps * dr
            for h in range(4):
                part = jnp.sum(jnp.where((row >= BLK * h) & (row < BLK * (h + 1)), sink_term, 0.0),
                               axis=0, keepdims=True)
                gsink = gsink - jnp.where(lane == 4 * g + h, part, 0.0)
            dv_parts.append(lax.dot_general(p.astype(BF), dos, (((0,), (0,)), ((), ())), preferred_element_type=F32))
            dq_parts.append(jnp.dot(ds, kb, preferred_element_type=F32))
            dk_parts.append(lax.dot_general(ds, qs, (((0,), (0,)), ((), ())), preferred_element_type=F32))
        gsink_ref[...] += gsink
        dkb = jnp.concatenate(dk_parts, axis=1)
        dvb = jnp.concatenate(dv_parts, axis=1)
        r_prev = pl.multiple_of(jnp.maximum(i - 1, 0) * BLK, BLK)
        r_cur = pl.multiple_of(i * BLK, BLK)
        dk_ref[pl.ds(r_prev, BLK), :] += dkb[0:BLK]
        dv_ref[pl.ds(r_prev, BLK), :] += dvb[0:BLK]
        dk_ref[pl.ds(r_cur, BLK), :] += dkb[BLK:2 * BLK]
        dv_ref[pl.ds(r_cur, BLK), :] += dvb[BLK:2 * BLK]
        dq = jnp.concatenate([dq_parts[g][BLK * h:BLK * (h + 1), :] for g in range(NKV) for h in range(4)], axis=1)
        dq_raw, g_qw = _norm_rope_bwd(dq, qraw_ref[...], qw_ref[...], jnp.tile(cos_ref[...], (1, 4)),
                                      jnp.tile(sin_ref[...], (1, 4)), bq_ref)
        dqraw_ref[...] = dq_raw.astype(BF)
        gqw_ref[...] += g_qw

    prev = lambda i: (jnp.maximum(i - 1, 0), 0)
    b512 = pl.BlockSpec((BLK, AW), _row)
    b128 = pl.BlockSpec((BLK, 128), _row)
    return pl.pallas_call(
        body, name="attn_bwd", grid=(NB,),
        out_shape=[jax.ShapeDtypeStruct((S, AW), BF), jax.ShapeDtypeStruct((S, AW), BF),
                   jax.ShapeDtypeStruct((S, KVW), F32), jax.ShapeDtypeStruct((S, KVW), F32),
                   jax.ShapeDtypeStruct((1, AW), F32), jax.ShapeDtypeStruct((1, 128), F32)],
        in_specs=[_SMEM, b512, pl.BlockSpec((BLK, KVW), prev), b128, pl.BlockSpec((BLK, KVW), prev), b128,
                  b512, b512, b512, b512, pl.BlockSpec((1, AW), _const), b128, b128, pl.BlockSpec((AW, AW), _const)],
        out_specs=[b512, b512, _VMEM, _VMEM, _VMEM, _VMEM],
        compiler_params=_params(True),
    )(sinks, qr, kr, kr, vb, vb, d_ya, ga, o, qraw, qw_t, cos_t, sin_t, bq)


HALO = 32


RC = 64
LC = 128


def _windows(ext_ref, r0, l0, base):
    col = ext_ref[r0:r0 + RC + HALO, l0:l0 + LC]
    for s in range(8):
        rolled = col if s == 0 else pltpu.roll(col, RC + HALO - s, 0)
        for t in range(CK):
            if (base + t) % 8 == s:
                a8 = base + t - s
                yield t, rolled[a8:a8 + RC]


def _taps(ext_ref, r0, l0, base, weight_row):
    acc = None
    for t, win in _windows(ext_ref, r0, l0, base):
        term = win * weight_row(t)[:, l0:l0 + LC]
        acc = term if acc is None else acc + term
    return acc


def _conv_fwd_call(z, gb, cwf, conv_b, ln_w, ln_b):
    def body(zc_in_ref, zp_ref, gb_ref, cw_ref, cb_ref, lw_ref, lb_ref, zc_ref, yb_ref, zext):
        i = pl.program_id(0)
        zext[0:HALO, :] = jnp.where(i > 0, zp_ref[TS - HALO:TS, :], 0.0)
        zext[HALO:HALO + TS, :] = zc_in_ref[...]
        for r0 in range(0, TS, RC):
            for l0 in range(0, CW, LC):
                acc = _taps(zext, r0, l0, HALO - (CK - 1), lambda k: cw_ref[k:k + 1, :])
                zc_ref[r0:r0 + RC, l0:l0 + LC] = acc + cb_ref[:, l0:l0 + LC]
        zc = zc_ref[...]
        mu = jnp.mean(zc, axis=-1, keepdims=True)
        dz = zc - mu
        rstd = lax.rsqrt(jnp.mean(dz * dz, axis=-1, keepdims=True) + EPS)
        zn = dz * rstd * lw_ref[...] + lb_ref[...]
        gbv = gb_ref[...]
        yb_ref[...] = (zn * _sigmoid(zn)) * (gbv * _sigmoid(gbv))

    t512 = pl.BlockSpec((TS, CW), _row)
    prev = pl.BlockSpec((TS, CW), lambda i: (jnp.maximum(i - 1, 0), 0))
    c512 = pl.BlockSpec((1, CW), _const)
    return pl.pallas_call(
        body, name="conv_fwd", grid=(NT,),
        out_shape=[jax.ShapeDtypeStruct((S, CW), F32), jax.ShapeDtypeStruct((S, CW), F32)],
        in_specs=[t512, prev, t512, pl.BlockSpec((CKP, CW), _const), c512, c512, c512],
        out_specs=[t512, t512],
        scratch_shapes=[pltpu.VMEM((TS + HALO, CW), F32)],
        compiler_params=_params(True),
    )(z, z, gb, cwf, conv_b, ln_w, ln_b)


def _conv_bwd_ln_call(d_yb, zc, gb, ln_w, ln_b):
    def body(dyb_ref, zc_ref, gb_ref, lw_ref, lb_ref, dzc_ref, dgb_ref, glw_ref, glb_ref, gcb_ref):
        i = pl.program_id(0)

        @pl.when(i == 0)
        def _():
            glw_ref[...] = jnp.zeros((1, CW), F32)
            glb_ref[...] = jnp.zeros((1, CW), F32)
            gcb_ref[...] = jnp.zeros((1, CW), F32)

        zc = zc_ref[...]
        mu = jnp.mean(zc, axis=-1, keepdims=True)
        dz = zc - mu
        rstd = lax.rsqrt(jnp.mean(dz * dz, axis=-1, keepdims=True) + EPS)
        zh = dz * rstd
        lw = lw_ref[...]
        zn = zh * lw + lb_ref[...]
        d_yb = dyb_ref[...]
        act_n, dact_n = _silu_and_grad(zn)
        act_g, dact_g = _silu_and_grad(gb_ref[...])
        dgb_ref[...] = (d_yb * act_n * dact_g).astype(BF)
        d_zn = d_yb * act_g * dact_n
        glw_ref[...] += jnp.sum(d_zn * zh, axis=0, keepdims=True)
        glb_ref[...] += jnp.sum(d_zn, axis=0, keepdims=True)
        dzh = d_zn * lw
        d_zc = rstd * (dzh - jnp.mean(dzh, axis=-1, keepdims=True) - zh * jnp.mean(dzh * zh, axis=-1, keepdims=True))
        dzc_ref[...] = d_zc
        gcb_ref[...] += jnp.sum(d_zc, axis=0, keepdims=True)

    t512 = pl.BlockSpec((TS, CW), _row)
    c512 = pl.BlockSpec((1, CW), _const)
    vec = jax.ShapeDtypeStruct((1, CW), F32)
    return pl.pallas_call(
        body, name="conv_bwd_ln", grid=(NT,),
        out_shape=[jax.ShapeDtypeStruct((S, CW), F32), jax.ShapeDtypeStruct((S, CW), BF), vec, vec, vec],
        in_specs=[t512, t512, t512, c512, c512],
        out_specs=[t512, t512, _VMEM, _VMEM, _VMEM],
        compiler_params=_params(True),
    )(d_yb, zc, gb, ln_w, ln_b)


def _conv_bwd_taps_call(d_zc, z, a, g, cwf):
    def body(dc_ref, dn_ref, zc_ref, zp_ref, a_ref, g_ref, cw_ref, da_ref, dg_ref, gcw_ref, dext, zext, gacc):
        i = pl.program_id(0)

        @pl.when(i == 0)
        def _():
            gacc[...] = jnp.zeros((CKP * 8, CW), F32)

        dext[0:TS, :] = dc_ref[...]
        dext[TS:TS + HALO, :] = jnp.where(i < NT - 1, dn_ref[0:HALO, :], 0.0)
        zext[0:HALO, :] = jnp.where(i > 0, zp_ref[TS - HALO:TS, :], 0.0)
        zext[HALO:HALO + TS, :] = zc_ref[...]
        for r0 in range(0, TS, RC):
            for l0 in range(0, CW, LC):
                d_z = _taps(dext, r0, l0, 0, lambda j: cw_ref[CK - 1 - j:CK - j, :])
                sg = _sigmoid(g_ref[r0:r0 + RC, l0:l0 + LC])
                da_ref[r0:r0 + RC, l0:l0 + LC] = (d_z * sg).astype(BF)
                dg_ref[r0:r0 + RC, l0:l0 + LC] = (d_z * a_ref[r0:r0 + RC, l0:l0 + LC] * sg * (1.0 - sg)).astype(BF)
                d_sub = dc_ref[r0:r0 + RC, l0:l0 + LC]
                for k, win in _windows(zext, r0, l0, HALO - (CK - 1)):
                    prod = d_sub * win
                    part = prod[0:8]
                    for q in range(1, RC // 8):
                        part = part + prod[8 * q:8 * q + 8]
                    gacc[8 * k:8 * k + 8, l0:l0 + LC] += part

        @pl.when(i == NT - 1)
        def _():
            for k in range(CK):
                gcw_ref[k:k + 1, :] = jnp.sum(gacc[8 * k:8 * k + 8, :], axis=0, keepdims=True)
            gcw_ref[CK:CKP, :] = jnp.zeros((CKP - CK, CW), F32)

    t512 = pl.BlockSpec((TS, CW), _row)
    prev = pl.BlockSpec((TS, CW), lambda i: (jnp.maximum(i - 1, 0), 0))
    nxt = pl.BlockSpec((TS, CW), lambda i: (jnp.minimum(i + 1, NT - 1), 0))
    return pl.pallas_call(
        body, name="conv_bwd_taps", grid=(NT,),
        out_shape=[jax.ShapeDtypeStruct((S, CW), BF), jax.ShapeDtypeStruct((S, CW), BF),
                   jax.ShapeDtypeStruct((CKP, CW), F32)],
        in_specs=[t512, nxt, t512, prev, t512, t512, pl.BlockSpec((CKP, CW), _const)],
        out_specs=[t512, t512, _VMEM],
        scratch_shapes=[pltpu.VMEM((TS + HALO, CW), F32), pltpu.VMEM((TS + HALO, CW), F32),
                        pltpu.VMEM((CKP * 8, CW), F32)],
        compiler_params=_params(True),
    )(d_zc, d_zc, z, z, a, g, cwf)


def _out_loss_call(o, ga, yb, x2, tgt, mod, w_out_full):
    def body(o_ref, ga_ref, yb_ref, x_ref, t_ref, mod_ref, w_ref,
             dout_ref, dya_ref, dyb_ref, gw_ref, loss_ref, dgate_ref):
        i = pl.program_id(0)

        @pl.when(i == 0)
        def _():
            gw_ref[...] = jnp.zeros((D, D), F32)
            loss_ref[...] = jnp.zeros((1, 128), F32)
            dgate_ref[...] = jnp.zeros((1, D), F32)

        gav = ga_ref[...]
        ya = o_ref[...] * (gav * _sigmoid(gav))
        ycat = jnp.concatenate([ya, yb_ref[...]], axis=1).astype(BF)
        w = w_ref[...]
        y = jnp.dot(ycat, w, preferred_element_type=F32)
        gate = mod_ref[:, 2 * D:3 * D]
        diff = x_ref[...] + gate * y - t_ref[...]
        sq = jnp.sum(jnp.sum(diff * diff, axis=1, keepdims=True), axis=0, keepdims=True)
        loss_ref[...] += jnp.broadcast_to(sq, (1, 128))
        d_out = diff * (1.0 / D)
        dout_ref[...] = d_out
        dgate_ref[...] += jnp.sum(d_out * y, axis=0, keepdims=True)
        dy = (d_out * gate).astype(BF)
        d_ycat = lax.dot_general(dy, w, (((1,), (1,)), ((), ())), preferred_element_type=F32)
        dya_ref[...] = d_ycat[:, 0:AW]
        dyb_ref[...] = d_ycat[:, AW:D]
        gw_ref[...] += lax.dot_general(ycat, dy, (((0,), (0,)), ((), ())), preferred_element_type=F32)

    t512 = pl.BlockSpec((TS, 512), _row)
    t1024 = pl.BlockSpec((TS, D), _row)
    return pl.pallas_call(
        body, name="out_loss", grid=(NT,),
        out_shape=[jax.ShapeDtypeStruct((S, D), F32), jax.ShapeDtypeStruct((S, AW), F32),
                   jax.ShapeDtypeStruct((S, CW), F32), jax.ShapeDtypeStruct((D, D), F32),
                   jax.ShapeDtypeStruct((1, 128), F32), jax.ShapeDtypeStruct((1, D), F32)],
        in_specs=[t512, t512, t512, t1024, t1024, pl.BlockSpec((1, 3 * D), _const),
                  pl.BlockSpec((D, D), _const, pipeline_mode=pl.Buffered(1))],
        out_specs=[t1024, t512, t512, _VMEM, _VMEM, _VMEM],
        compiler_params=_params(True),
    )(o, ga, yb, x2, tgt, mod, w_out_full)


SM_ROWS = 8
PIECES = ((0, 512), (512, 640), (640, 768), (768, 1280), (1280, 1792), (1792, 2304), (2304, 2816))


def _bwd_in_call(dqraw, dk, dv, dga, da, dg, dgb, kraw, kw_t, cos_t, sin_t, bk, h, wt_full, x2, d_out, mod, norm_w,
                 gw_out, gcw, glw, glb, gcb, gqw, gsink, dgate, loss_p, cact_all):
    def body(dq_ref, dk_ref, dv_ref, dga_ref, da_ref, dg_ref, dgb_ref, kraw_ref, kw_ref, cos_ref, sin_ref, bk_ref,
             h_ref, wt_ref, x_ref, dout_ref, mod_ref, nw_ref, gwout_ref, gcw_ref, glw_ref, glb_ref, gcb_ref, gqw_ref,
             gsink_ref, dgate_ref, loss_ref, cact_ref,
             gx_ref, o_gwin, o_gwout, o_gwada, o_gbada, o_gnw, o_gqw, o_gkw, o_gsink, o_gcw, o_gcb, o_glw, o_glb,
             o_loss,
             acc, win_send, win_recv, wout_send, wout_recv, sm_buf, cw_buf, dmod_all, vec_acc, gkw_acc,
             ws_s, ws_r, wo_s, wo_r, sm_s, sm_r, cw_s, cw_r):
        i = pl.program_id(0)
        pos = _mesh_pos()
        x, y, cc = pos
        me = 4 * x + 2 * y + cc

        def rs_copy(k, which):
            px, py, pc = _xor_peer(pos, k)
            dst = 4 * px + 2 * py + pc
            if which == "in":
                src = win_send.at[pl.ds(pl.multiple_of(dst * IN_SHARD, 16), IN_SHARD), :]
                return pltpu.make_async_remote_copy(src_ref=src, dst_ref=win_recv.at[k - 1], send_sem=ws_s.at[k - 1],
                                                    recv_sem=ws_r.at[k - 1], device_id=(px, py, pc), device_id_type=MESH)
            src = wout_send.at[pl.ds(pl.multiple_of(dst * OUT_SHARD, 16), OUT_SHARD), :]
            return pltpu.make_async_remote_copy(src_ref=src, dst_ref=wout_recv.at[k - 1], send_sem=wo_s.at[k - 1],
                                                recv_sem=wo_r.at[k - 1], device_id=(px, py, pc), device_id_type=MESH)

        def dproj_pieces():
            dk_raw, g_kw = _norm_rope_bwd(dk_ref[...], kraw_ref[...], kw_ref[...], cos_ref[...], sin_ref[...], bk_ref)
            return [dq_ref[...], dk_raw.astype(BF), dv_ref[...].astype(BF), dga_ref[...], da_ref[...], dg_ref[...],
                    dgb_ref[...]], g_kw

        @pl.when(i == 0)
        def _():
            acc[...] = jnp.zeros((INW, D), F32)
            vec_acc[...] = jnp.zeros((8, D), F32)
            gkw_acc[...] = jnp.zeros((1, KVW), F32)
            wout_send[...] = gwout_ref[...].astype(BF)
            for k in range(1, NDEV):
                rs_copy(k, "out").start()

        @pl.when(i < NT)
        def _():
            pieces, g_kw = dproj_pieces()
            gkw_acc[...] += g_kw
            hv = h_ref[...]
            for (lo, hi), piece in zip(PIECES, pieces):
                acc[lo:hi, :] += lax.dot_general(piece, hv, (((0,), (0,)), ((), ())), preferred_element_type=F32)

        @pl.when(i == NT - 1)
        def _():
            for lo, hi in PIECES:
                win_send[lo:hi, :] = acc[lo:hi, :].astype(BF)
            for k in range(1, NDEV):
                rs_copy(k, "in").start()

        @pl.when(i >= NT)
        def _():
            pieces, _ = dproj_pieces()
            dproj = jnp.concatenate(pieces, axis=1)
            d_h = jnp.dot(dproj, wt_ref[...], preferred_element_type=F32)
            xv = x_ref[...]
            scale = mod_ref[:, D:2 * D]
            nw = nw_ref[...]
            r = lax.rsqrt(jnp.mean(xv * xv, axis=-1, keepdims=True) + EPS)
            xn = xv * r
            vec_acc[0:1, :] += jnp.sum(d_h, axis=0, keepdims=True)
            vec_acc[1:2, :] += jnp.sum(d_h * (xn * nw), axis=0, keepdims=True)
            d_u = d_h * (1.0 + scale)
            vec_acc[2:3, :] += jnp.sum(d_u * xn, axis=0, keepdims=True)
            d_xn = d_u * nw
            gx_ref[...] = dout_ref[...] + r * (d_xn - xn * jnp.mean(d_xn * xn, axis=-1, keepdims=True))

        @pl.when(i == 2 * NT - 1)
        def _():
            ch_sm = (_slab(sm_buf), sm_s, sm_r)
            ch_cw = (_slab(cw_buf), cw_s, cw_r)
            z128 = jnp.zeros((1, 128), F32)
            row4 = jnp.concatenate([glw_ref[...], glb_ref[...]], axis=1)
            row5 = jnp.concatenate([gcb_ref[...], gqw_ref[...]], axis=1)
            row6 = jnp.concatenate([gkw_acc[...], gsink_ref[...], loss_ref[...]] + [z128] * 5, axis=1)
            sm_buf[me] = jnp.concatenate([vec_acc[0:2, :], dgate_ref[...], vec_acc[2:3, :], row4, row5, row6,
                                          jnp.zeros((1, D), F32)], axis=0)
            f_sm = _ag_start(ch_sm, pos)
            cw_buf[me] = gcw_ref[...]
            f_cw = _ag_start(ch_cw, pos)
            _ag_finish(ch_sm, pos, f_sm)
            _ag_finish(ch_cw, pos, f_cw)
            tot = sm_buf[0]
            cw_tot = cw_buf[0]
            for d in range(1, NDEV):
                tot = tot + sm_buf[d]
                cw_tot = cw_tot + cw_buf[d]
            o_gbada[...] = jnp.concatenate([tot[0:1, :], tot[1:2, :], tot[2:3, :]], axis=1)
            o_gnw[...] = tot[3:4, :]
            o_glw[...] = tot[4:5, 0:CW]
            o_glb[...] = tot[4:5, CW:D]
            o_gcb[...] = tot[5:6, 0:CW]
            gq = tot[5:6, CW:CW + HD]
            for hh in range(1, NQ):
                gq = gq + tot[5:6, CW + HD * hh:CW + HD * (hh + 1)]
            o_gqw[...] = gq
            o_gkw[...] = tot[6:7, 0:HD] + tot[6:7, HD:2 * HD]
            o_gsink[...] = tot[6:7, 128:128 + NQ]
            o_loss[...] = tot[6:7, 256:384] * (0.5 / D)
            mine = jnp.zeros((CK, CONV_SHARD), F32)
            for d in range(NDEV):
                mine = mine + jnp.where(me == d, cw_tot[0:CK, CONV_SHARD * d:CONV_SHARD * (d + 1)], 0.0)
            o_gcw[...] = mine
            for d in range(NDEV):
                dmod_all[d:d + 1, :] = jnp.concatenate([sm_buf[d, 0:1, :], sm_buf[d, 1:2, :], sm_buf[d, 2:3, :]],
                                                       axis=1)
            col0 = pl.multiple_of(me * ADA_SHARD, 128)
            o_gwada[...] = lax.dot_general(cact_ref[...], dmod_all[:, pl.ds(col0, ADA_SHARD)], (((0,), (0,)), ((), ())),
                                           preferred_element_type=F32, precision=lax.Precision.HIGHEST)

            for k in range(1, NDEV):
                rs_copy(k, "in").wait_recv()
                rs_copy(k, "out").wait_recv()
            g_in = acc[pl.ds(pl.multiple_of(me * IN_SHARD, 8), IN_SHARD), :]
            g_out = gwout_ref[pl.ds(pl.multiple_of(me * OUT_SHARD, 8), OUT_SHARD), :]
            for k in range(1, NDEV):
                g_in = g_in + win_recv[k - 1].astype(F32)
                g_out = g_out + wout_recv[k - 1].astype(F32)
            o_gwin[...] = g_in
            o_gwout[...] = g_out
            for k in range(1, NDEV):
                rs_copy(k, "in").wait_send()
                rs_copy(k, "out").wait_send()

    half = lambda i: (i % NT, 0)
    late = lambda i: (jnp.maximum(i - NT, 0), 0)
    t512 = pl.BlockSpec((TS, 512), half)
    t128 = pl.BlockSpec((TS, 128), half)
    l1024 = pl.BlockSpec((TS, D), late)
    sem7 = pltpu.SemaphoreType.DMA((7,))
    sds = jax.ShapeDtypeStruct
    return pl.pallas_call(
        body, name="bwd_in", grid=(2 * NT,),
        out_shape=[sds((S, D), F32), sds((IN_SHARD, D), F32), sds((OUT_SHARD, D), F32), sds((D, ADA_SHARD), F32),
                   sds((1, 3 * D), F32), sds((1, D), F32), sds((1, HD), F32), sds((1, HD), F32), sds((1, NQ), F32),
                   sds((CK, CONV_SHARD), F32), sds((1, CW), F32), sds((1, CW), F32), sds((1, CW), F32),
                   sds((1, 128), F32)],
        in_specs=[t512, t128, t128, t512, t512, t512, t512, t128, pl.BlockSpec((1, KVW), _const), t128, t128,
                  pl.BlockSpec((KVW, KVW), _const), pl.BlockSpec((TS, D), half),
                  pl.BlockSpec((INW, D), _const, pipeline_mode=pl.Buffered(1)), l1024, l1024,
                  pl.BlockSpec((1, 3 * D), _const), pl.BlockSpec((1, D), _const)] + [_VMEM] * 10,
        out_specs=[l1024] + [_VMEM] * 13,
        scratch_shapes=[pltpu.VMEM((INW, D), F32), pltpu.VMEM((INW, D), BF), pltpu.VMEM((NDEV - 1, IN_SHARD, D), BF),
                        pltpu.VMEM((D, D), BF), pltpu.VMEM((NDEV - 1, OUT_SHARD, D), BF),
                        pltpu.VMEM((NDEV, SM_ROWS, D), F32), pltpu.VMEM((NDEV, CKP, CW), F32),
                        pltpu.VMEM((NDEV, 3 * D), F32), pltpu.VMEM((8, D), F32), pltpu.VMEM((1, KVW), F32)]
        + [sem7] * 8,
        compiler_params=pltpu.CompilerParams(dimension_semantics=("arbitrary",), vmem_limit_bytes=BIG_VMEM_LIMIT),
    )(dqraw, dk, dv, dga, da, dg, dgb, kraw, kw_t, cos_t, sin_t, bk, h, wt_full, x2, d_out, mod, norm_w,
      gw_out, gcw, glw, glb, gcb, gqw, gsink, dgate, loss_p, cact_all)


def _adam_call(ws, gs, ms, vs):
    n = len(ws)
    bc1 = 1.0 - ADAM_B1 ** ADAM_STEP
    bc2 = 1.0 - ADAM_B2 ** ADAM_STEP

    def body(*refs):
        ins, outs = refs[:4 * n], refs[4 * n:]
        for j in range(n):
            w, g, m, v = (ins[j][...], ins[n + j][...], ins[2 * n + j][...], ins[3 * n + j][...])
            m_new = ADAM_B1 * m + (1.0 - ADAM_B1) * g
            v_new = ADAM_B2 * v + (1.0 - ADAM_B2) * (g * g)
            m_hat = m_new / bc1
            v_hat = v_new / bc2
            outs[j][...] = -ADAM_LR * (m_hat / (jnp.sqrt(v_hat) + ADAM_EPS) + ADAM_WD * w)
            outs[n + j][...] = m_new
            outs[2 * n + j][...] = v_new

    shapes = [jax.ShapeDtypeStruct(w.shape, F32) for w in ws]
    return pl.pallas_call(
        body, name="adam",
        out_shape=shapes * 3, in_specs=[_VMEM] * (4 * n), out_specs=[_VMEM] * (3 * n),
        compiler_params=_params(),
    )(*ws, *gs, *ms, *vs)


def _rope_tables():
    inv = ROPE_THETA ** (-jnp.arange(0, HD, 2, dtype=F32) / HD)
    ang = jnp.arange(S, dtype=F32)[:, None] * inv[None, :]
    cos, sin = jnp.cos(ang), jnp.sin(ang)
    cos64 = jnp.concatenate([cos, cos], axis=-1)
    sin64 = jnp.concatenate([-sin, sin], axis=-1)
    return jnp.tile(cos64, (1, 2)), jnp.tile(sin64, (1, 2))


def _group_matrix(width):
    idx = jnp.arange(width) // HD
    return jnp.where(idx[:, None] == idx[None, :], 1.0 / HD, 0.0).astype(BF)


def kernel(x, c, w_ada, b_ada, norm_w, w_in, q_norm_w, k_norm_w, sinks, conv_w, conv_b, ln_w, ln_b, w_out, loss_target, m_w_ada, m_b_ada, m_norm_w, m_w_in, m_q_norm_w, m_k_norm_w, m_sinks, m_conv_w, m_conv_b, m_ln_w, m_ln_b, m_w_out, v_w_ada, v_b_ada, v_norm_w, v_w_in, v_q_norm_w, v_k_norm_w, v_sinks, v_conv_w, v_conv_b, v_ln_w, v_ln_b, v_w_out):
    x2 = x[0]
    tgt = loss_target[0]
    cos_t, sin_t = _rope_tables()
    bq = _group_matrix(AW)
    bk = _group_matrix(KVW)
    qw_t = jnp.tile(q_norm_w, (1, NQ))
    kw_t = jnp.tile(k_norm_w, (1, NKV))

    tr = lambda t: jnp.swapaxes(t[0], 0, 1)
    wt_full, w_out_full, cwf, mod, cact_all = _gather_call(tr(w_in), w_out[0], conv_w[0], c, w_ada[0], b_ada)

    h, qraw, kraw, ga, a, g, gb, qr, kr, vb, z = _fwd_in_call(x2, mod, norm_w, qw_t, kw_t, cos_t, sin_t, bq, bk, wt_full)
    o = _attn_fwd_call(sinks, qr, kr, vb)
    zc, yb = _conv_fwd_call(z, gb, cwf, conv_b, ln_w, ln_b)
    d_out, d_ya, d_yb, gw_out, loss_p, dgate = _out_loss_call(o, ga, yb, x2, tgt, mod, w_out_full)

    d_zc, dgb, glw, glb, gcb = _conv_bwd_ln_call(d_yb, zc, gb, ln_w, ln_b)
    da, dg, gcw = _conv_bwd_taps_call(d_zc, z, a, g, cwf)
    dqraw, dga, dk, dv, gqw, gsink = _attn_bwd_call(sinks, qr, kr, vb, d_ya, ga, o, qraw, qw_t, cos_t, sin_t, bq)
    (grad_x, g_w_in_t, g_w_out, g_w_ada, g_b_ada, g_norm_w, g_qw, g_kw, g_sinks, g_conv_w, g_conv_b, g_ln_w, g_ln_b,
     loss_v) = _bwd_in_call(dqraw, dk, dv, dga, da, dg, dgb, kraw, kw_t, cos_t, sin_t, bk, h, wt_full, x2, d_out, mod,
                            norm_w, gw_out, gcw, glw, glb, gcb, gqw, gsink, dgate, loss_p, cact_all)

    ws = [w_ada[0], b_ada, norm_w, tr(w_in), q_norm_w, k_norm_w, sinks, conv_w[0], conv_b, ln_w, ln_b, w_out[0]]
    gs = [g_w_ada, g_b_ada, g_norm_w, g_w_in_t, g_qw, g_kw, g_sinks, g_conv_w, g_conv_b, g_ln_w, g_ln_b, g_w_out]
    ms = [m_w_ada[0], m_b_ada, m_norm_w, tr(m_w_in), m_q_norm_w, m_k_norm_w, m_sinks, m_conv_w[0], m_conv_b, m_ln_w,
          m_ln_b, m_w_out[0]]
    vs = [v_w_ada[0], v_b_ada, v_norm_w, tr(v_w_in), v_q_norm_w, v_k_norm_w, v_sinks, v_conv_w[0], v_conv_b, v_ln_w,
          v_ln_b, v_w_out[0]]
    upd = _adam_call(ws, gs, ms, vs)
    n = len(ws)
    shaped = [w_ada, b_ada, norm_w, w_in, q_norm_w, k_norm_w, sinks, conv_w, conv_b, ln_w, ln_b, w_out]
    W_IN_POS = 3

    def like(vals):
        vals = [jnp.swapaxes(v, 0, 1) if j == W_IN_POS else v for j, v in enumerate(vals)]
        return [v.reshape(s.shape) for v, s in zip(vals, shaped)]

    return (loss_v[0, 0], grad_x[None], *like(gs), *like(upd[0:n]), *like(upd[n:2 * n]), *like(upd[2 * n:3 * n]))
```

```python
import functools

import jax
import jax.numpy as jnp
from jax import lax
from jax.experimental import pallas as pl
from jax.experimental.pallas import tpu as pltpu

S = 2048
D = 1024
NDEV = 8
HD = 64
NQ = 8
NKV = 2
AW = 512
KVW = 128
CW = 512
INW = 2816
IN_SHARD = INW // NDEV
ADA_SHARD = 3 * D // NDEV
OUT_SHARD = D // NDEV
CONV_SHARD = CW // NDEV
CK = 31
CKP = 32
BLK = 128
TS = 256
NT = S // TS
NB = S // BLK
EPS = 1e-6
ROPE_THETA = 10000.0
NEG = -1e30
BF = jnp.bfloat16
F32 = jnp.float32

ADAM_LR = 0.001
ADAM_B1 = 0.9
ADAM_B2 = 0.999
ADAM_EPS = 1e-08
ADAM_WD = 0.01
ADAM_STEP = 10

VMEM_LIMIT = 56 * 1024 * 1024
BIG_VMEM_LIMIT = 62 * 1024 * 1024
MESH = pl.DeviceIdType.MESH

_VMEM = pl.BlockSpec(memory_space=pltpu.VMEM)
_SMEM = pl.BlockSpec(memory_space=pltpu.SMEM)


def _params(grid=False):
    if grid:
        return pltpu.CompilerParams(dimension_semantics=("arbitrary",), vmem_limit_bytes=VMEM_LIMIT)
    return pltpu.CompilerParams(vmem_limit_bytes=VMEM_LIMIT)


def _row(i):
    return (i, 0)


def _const(i):
    return (0, 0)


def _sigmoid(t):
    return 1.0 / (1.0 + jnp.exp(-t))


def _silu_and_grad(t):
    sg = _sigmoid(t)
    return t * sg, sg * (1.0 + t * (1.0 - sg))


def _group_mean(t, b_ref):
    hi = t.astype(BF)
    lo = (t - hi.astype(F32)).astype(BF)
    b = b_ref[...]
    return jnp.dot(hi, b, preferred_element_type=F32) + jnp.dot(lo, b, preferred_element_type=F32)


def _partner(t):
    w = t.shape[-1]
    lane = lax.broadcasted_iota(jnp.int32, t.shape, 1)
    first = (lane & 32) == 0
    return jnp.where(first, pltpu.roll(t, w - 32, 1), pltpu.roll(t, 32, 1))


def _norm_rope_fwd(t, w_t, cos, sin, b_ref):
    r = lax.rsqrt(_group_mean(t * t, b_ref) + EPS)
    tn = t * r * w_t
    return tn * cos + _partner(tn) * sin


def _norm_rope_bwd(d_out, t, w_t, cos, sin, b_ref):
    d_tn = d_out * cos + _partner(d_out * sin)
    r = lax.rsqrt(_group_mean(t * t, b_ref) + EPS)
    th = t * r
    g_w = jnp.sum(d_tn * th, axis=0, keepdims=True)
    d_th = d_tn * w_t
    d_t = r * (d_th - th * _group_mean(d_th * th, b_ref))
    return d_t, g_w


def _mesh_pos():
    return lax.axis_index("x"), lax.axis_index("y"), lax.axis_index("c")


def _ag_copy(chan, k, block, to):
    blk, send_sems, recv_sems = chan
    ref = blk(*block)
    return pltpu.make_async_remote_copy(src_ref=ref, dst_ref=ref, send_sem=send_sems.at[k],
                                        recv_sem=recv_sems.at[k], device_id=to, device_id_type=MESH)


def _ag_start(chan, pos):
    x, y, c = pos
    me = (x, y, c)
    chips = [(1 - x, y), (x, 1 - y), (1 - x, 1 - y)]
    first = [_ag_copy(chan, 0, me, (x, y, 1 - c))]
    first += [_ag_copy(chan, 1 + j, me, (*chip, c)) for j, chip in enumerate(chips)]
    for cp in first:
        cp.start()
    return first


def _ag_finish(chan, pos, first):
    x, y, c = pos
    me = (x, y, c)
    sibling = (x, y, 1 - c)
    chips = [(1 - x, y), (x, 1 - y), (1 - x, 1 - y)]
    passed = [_ag_copy(chan, 4 + j, (*chip, c), sibling) for j, chip in enumerate(chips)]
    for j, chip in enumerate(chips):
        _ag_copy(chan, 1 + j, (*chip, c), me).wait_recv()
        passed[j].start()
    _ag_copy(chan, 0, sibling, me).wait_recv()
    for j, chip in enumerate(chips):
        _ag_copy(chan, 4 + j, (*chip, 1 - c), me).wait_recv()
    for cp in first + passed:
        cp.wait_send()


def _slab(buf):
    return lambda px, py, pc: buf.at[4 * px + 2 * py + pc]


def _row_block(buf, rows, align):
    return lambda px, py, pc: buf.at[pl.ds(pl.multiple_of((4 * px + 2 * py + pc) * rows, align), rows), :]


def _gather_call(w_in_t, w_out_s, conv_w_s, c, w_ada_s, b_ada):
    def body(win_ref, wout_ref, cw_ref, c_ref, wada_ref, bada_ref,
             wtf_ref, woutf_ref, cwf_ref, mod_ref, cact_ref,
             cw_buf, ca_buf, mp_buf,
             s0, r0, s1, r1, s2, r2, s3, r3, s4, r4):
        pos = _mesh_pos()
        x, y, cc = pos
        me = 4 * x + 2 * y + cc
        ch_win = (_row_block(wtf_ref, IN_SHARD, 16), s0, r0)
        ch_wout = (_row_block(woutf_ref, OUT_SHARD, 16), s1, r1)
        ch_cw = (_slab(cw_buf), s2, r2)
        ch_ca = (_slab(ca_buf), s3, r3)
        ch_mp = (_slab(mp_buf), s4, r4)

        cv = c_ref[...]
        c_act = cv * _sigmoid(cv)
        ca_buf[me] = jnp.broadcast_to(c_act, (8, D))
        f_ca = _ag_start(ch_ca, pos)
        wtf_ref[pl.ds(pl.multiple_of(me * IN_SHARD, 16), IN_SHARD), :] = win_ref[...].astype(BF)
        f_win = _ag_start(ch_win, pos)
        woutf_ref[pl.ds(pl.multiple_of(me * OUT_SHARD, 16), OUT_SHARD), :] = wout_ref[...].astype(BF)
        f_wout = _ag_start(ch_wout, pos)
        cw_buf[me] = cw_ref[...]
        f_cw = _ag_start(ch_cw, pos)

        _ag_finish(ch_ca, pos, f_ca)
        cact_all = jnp.concatenate([ca_buf[d, 0:1, :] for d in range(NDEV)], axis=0)
        cact_ref[...] = cact_all
        col0 = pl.multiple_of(me * ADA_SHARD, 128)
        mp = jnp.dot(cact_all, wada_ref[...], preferred_element_type=F32,
                     precision=lax.Precision.HIGHEST) + bada_ref[:, pl.ds(col0, ADA_SHARD)]
        mp_buf[me] = mp
        f_mp = _ag_start(ch_mp, pos)

        _ag_finish(ch_cw, pos, f_cw)
        _ag_finish(ch_mp, pos, f_mp)
        for d in range(NDEV):
            cwf_ref[0:CK, CONV_SHARD * d:CONV_SHARD * (d + 1)] = cw_buf[d]
        cwf_ref[CK:CKP, :] = jnp.zeros((CKP - CK, CW), F32)
        mod_ref[...] = jnp.concatenate([mp_buf[d, pl.ds(me, 1), :] for d in range(NDEV)], axis=1)
        _ag_finish(ch_win, pos, f_win)
        _ag_finish(ch_wout, pos, f_wout)

    sem = pltpu.SemaphoreType.DMA((7,))
    return pl.pallas_call(
        body, name="gather",
        out_shape=[jax.ShapeDtypeStruct((INW, D), BF), jax.ShapeDtypeStruct((D, D), BF),
                   jax.ShapeDtypeStruct((CKP, CW), F32), jax.ShapeDtypeStruct((1, 3 * D), F32),
                   jax.ShapeDtypeStruct((NDEV, D), F32)],
        in_specs=[_VMEM] * 6, out_specs=[_VMEM] * 5,
        scratch_shapes=[pltpu.VMEM((NDEV, CK, CONV_SHARD), F32), pltpu.VMEM((NDEV, 8, D), F32),
                        pltpu.VMEM((NDEV, 8, ADA_SHARD), F32)] + [sem] * 10,
        compiler_params=_params(),
    )(w_in_t, w_out_s, conv_w_s, c, w_ada_s, b_ada)


def _fwd_in_call(x2, mod, norm_w, qw_t, kw_t, cos_t, sin_t, bq, bk, wt_full):
    def body(x_ref, mod_ref, nw_ref, qw_ref, kw_ref, cos_ref, sin_ref, bq_ref, bk_ref, w_ref,
             h_ref, qraw_ref, kraw_ref, ga_ref, a_ref, g_ref, gb_ref, qr_ref, kr_ref, vb_ref, z_ref):
        xv = x_ref[...]
        shift = mod_ref[:, 0:D]
        scale = mod_ref[:, D:2 * D]
        r = lax.rsqrt(jnp.mean(xv * xv, axis=-1, keepdims=True) + EPS)
        h = (xv * r * nw_ref[...]) * (1.0 + scale) + shift
        hb = h.astype(BF)
        h_ref[...] = hb

        def proj(lo, hi):
            return lax.dot_general(hb, w_ref[lo:hi, :], (((1,), (1,)), ((), ())), preferred_element_type=F32)

        cos = cos_ref[...]
        sin = sin_ref[...]
        q = proj(0, 512)
        qraw_ref[...] = q
        qr_ref[...] = _norm_rope_fwd(q, qw_ref[...], jnp.tile(cos, (1, 4)), jnp.tile(sin, (1, 4)), bq_ref).astype(BF)
        k = proj(512, 640)
        kraw_ref[...] = k
        kr_ref[...] = _norm_rope_fwd(k, kw_ref[...], cos, sin, bk_ref).astype(BF)
        vb_ref[...] = proj(640, 768).astype(BF)
        ga_ref[...] = proj(768, 1280)
        a = proj(1280, 1792)
        g = proj(1792, 2304)
        a_ref[...] = a
        g_ref[...] = g
        z_ref[...] = a * _sigmoid(g)
        gb_ref[...] = proj(2304, 2816)

    t512 = pl.BlockSpec((TS, 512), _row)
    t128 = pl.BlockSpec((TS, 128), _row)
    return pl.pallas_call(
        body, name="fwd_in", grid=(NT,),
        out_shape=[jax.ShapeDtypeStruct((S, D), BF), jax.ShapeDtypeStruct((S, AW), F32),
                   jax.ShapeDtypeStruct((S, KVW), F32), jax.ShapeDtypeStruct((S, AW), F32),
                   jax.ShapeDtypeStruct((S, CW), F32), jax.ShapeDtypeStruct((S, CW), F32),
                   jax.ShapeDtypeStruct((S, CW), F32), jax.ShapeDtypeStruct((S, AW), BF),
                   jax.ShapeDtypeStruct((S, KVW), BF), jax.ShapeDtypeStruct((S, KVW), BF),
                   jax.ShapeDtypeStruct((S, CW), F32)],
        in_specs=[pl.BlockSpec((TS, D), _row), pl.BlockSpec((1, 3 * D), _const), pl.BlockSpec((1, D), _const),
                  pl.BlockSpec((1, AW), _const), pl.BlockSpec((1, KVW), _const), t128, t128,
                  pl.BlockSpec((AW, AW), _const), pl.BlockSpec((KVW, KVW), _const),
                  pl.BlockSpec((INW, D), _const, pipeline_mode=pl.Buffered(1))],
        out_specs=[pl.BlockSpec((TS, D), _row), t512, t128, t512, t512, t512, t512, t512, t128, t128, t512],
        compiler_params=_params(True),
    )(x2, mod, norm_w, qw_t, kw_t, cos_t, sin_t, bq, bk, wt_full)


def _band_mask(i):
    qi = lax.broadcasted_iota(jnp.int32, (4 * BLK, 2 * BLK), 0) & (BLK - 1)
    kj = lax.broadcasted_iota(jnp.int32, (4 * BLK, 2 * BLK), 1)
    dist = qi + BLK - kj
    return (dist >= 0) & (dist < BLK) & ((kj >= BLK) | (i > 0))


def _sink_rows(sink_ref, g):
    row = lax.broadcasted_iota(jnp.int32, (4 * BLK, 1), 0)
    return jnp.where(row < BLK, sink_ref[0, 4 * g],
                     jnp.where(row < 2 * BLK, sink_ref[0, 4 * g + 1],
                               jnp.where(row < 3 * BLK, sink_ref[0, 4 * g + 2], sink_ref[0, 4 * g + 3])))


def _stack_heads(t, g):
    return jnp.concatenate([t[:, HD * (4 * g + h):HD * (4 * g + h + 1)] for h in range(4)], axis=0)


def _band(prev, cur, g):
    return jnp.concatenate([prev[:, HD * g:HD * (g + 1)], cur[:, HD * g:HD * (g + 1)]], axis=0)


def _softmax_band(qs, kb, mask, sink):
    s = lax.dot_general(qs, kb, (((1,), (1,)), ((), ())), preferred_element_type=F32) * (HD ** -0.5)
    s = jnp.where(mask, s, NEG)
    m = jnp.maximum(jnp.max(s, axis=-1, keepdims=True), sink)
    e = jnp.exp(s - m)
    es = jnp.exp(sink - m)
    inv = 1.0 / (jnp.sum(e, axis=-1, keepdims=True) + es)
    return e * inv, es * inv


def _attn_fwd_call(sinks, qr, kr, vb):
    def body(sink_ref, q_ref, kp_ref, kc_ref, vp_ref, vc_ref, o_ref):
        i = pl.program_id(0)
        mask = _band_mask(i)
        q = q_ref[...]
        kp, kc, vp, vc = kp_ref[...], kc_ref[...], vp_ref[...], vc_ref[...]
        for g in range(NKV):
            p, _ = _softmax_band(_stack_heads(q, g), _band(kp, kc, g), mask, _sink_rows(sink_ref, g))
            o = jnp.dot(p.astype(BF), _band(vp, vc, g), preferred_element_type=F32)
            for h in range(4):
                o_ref[:, HD * (4 * g + h):HD * (4 * g + h + 1)] = o[BLK * h:BLK * (h + 1), :]

    prev = lambda i: (jnp.maximum(i - 1, 0), 0)
    return pl.pallas_call(
        body, name="attn_fwd", grid=(NB,),
        out_shape=jax.ShapeDtypeStruct((S, AW), F32),
        in_specs=[_SMEM, pl.BlockSpec((BLK, AW), _row), pl.BlockSpec((BLK, KVW), prev), pl.BlockSpec((BLK, KVW), _row),
                  pl.BlockSpec((BLK, KVW), prev), pl.BlockSpec((BLK, KVW), _row)],
        out_specs=pl.BlockSpec((BLK, AW), _row),
        compiler_params=_params(True),
    )(sinks, qr, kr, kr, vb, vb)


def _attn_bwd_call(sinks, qr, kr, vb, d_ya, ga, o, qraw, qw_t, cos_t, sin_t, bq):
    def body(sink_ref, q_ref, kp_ref, kc_ref, vp_ref, vc_ref, dya_ref, ga_ref, o_ref, qraw_ref, qw_ref,
             cos_ref, sin_ref, bq_ref,
             dqraw_ref, dga_ref, dk_ref, dv_ref, gqw_ref, gsink_ref):
        i = pl.program_id(0)

        @pl.when(i == 0)
        def _():
            dk_ref[...] = jnp.zeros((S, KVW), F32)
            dv_ref[...] = jnp.zeros((S, KVW), F32)
            gqw_ref[...] = jnp.zeros((1, AW), F32)
            gsink_ref[...] = jnp.zeros((1, 128), F32)

        mask = _band_mask(i)
        q = q_ref[...]
        kp, kc, vp, vc = kp_ref[...], kc_ref[...], vp_ref[...], vc_ref[...]
        d_ya = dya_ref[...]
        act, dact = _silu_and_grad(ga_ref[...])
        dga_ref[...] = (d_ya * o_ref[...] * dact).astype(BF)
        d_o = (d_ya * act).astype(BF)
        lane = lax.broadcasted_iota(jnp.int32, (1, 128), 1)
        row = lax.broadcasted_iota(jnp.int32, (4 * BLK, 1), 0)
        gsink = jnp.zeros((1, 128), F32)
        dq_parts, dk_parts, dv_parts = [], [], []
        for g in range(NKV):
            qs = _stack_heads(q, g)
            kb = _band(kp, kc, g)
            vbd = _band(vp, vc, g)
            p, ps = _softmax_band(qs, kb, mask, _sink_rows(sink_ref, g))
            dos = _stack_heads(d_o, g)
            dp = lax.dot_general(dos, vbd, (((1,), (1,)), ((), ())), preferred_element_type=F32)
            dr = jnp.sum(p * dp, axis=-1, keepdims=True)
            ds = (p * (dp - dr) * (HD ** -0.5)).astype(BF)
            sink_term = ps * dr
            for h in range(4):
                part = jnp.sum(jnp.where((row >= BLK * h) & (row < BLK * (h + 1)), sink_term, 0.0),
                               axis=0, keepdims=True)
                gsink = gsink - jnp.where(lane == 4 * g + h, part, 0.0)
            dv_parts.append(lax.dot_general(p.astype(BF), dos, (((0,), (0,)), ((), ())), preferred_element_type=F32))
            dq_parts.append(jnp.dot(ds, kb, preferred_element_type=F32))
            dk_parts.append(lax.dot_general(ds, qs, (((0,), (0,)), ((), ())), preferred_element_type=F32))
        gsink_ref[...] += gsink
        dkb = jnp.concatenate(dk_parts, axis=1)
        dvb = jnp.concatenate(dv_parts, axis=1)
        r_prev = pl.multiple_of(jnp.maximum(i - 1, 0) * BLK, BLK)
        r_cur = pl.multiple_of(i * BLK, BLK)
        dk_ref[pl.ds(r_prev, BLK), :] += dkb[0:BLK]
        dv_ref[pl.ds(r_prev, BLK), :] += dvb[0:BLK]
        dk_ref[pl.ds(r_cur, BLK), :] += dkb[BLK:2 * BLK]
        dv_ref[pl.ds(r_cur, BLK), :] += dvb[BLK:2 * BLK]
        dq = jnp.concatenate([dq_parts[g][BLK * h:BLK * (h + 1), :] for g in range(NKV) for h in range(4)], axis=1)
        dq_raw, g_qw = _norm_rope_bwd(dq, qraw_ref[...], qw_ref[...], jnp.tile(cos_ref[...], (1, 4)),
                                      jnp.tile(sin_ref[...], (1, 4)), bq_ref)
        dqraw_ref[...] = dq_raw.astype(BF)
        gqw_ref[...] += g_qw

    prev = lambda i: (jnp.maximum(i - 1, 0), 0)
    b512 = pl.BlockSpec((BLK, AW), _row)
    b128 = pl.BlockSpec((BLK, 128), _row)
    return pl.pallas_call(
        body, name="attn_bwd", grid=(NB,),
        out_shape=[jax.ShapeDtypeStruct((S, AW), BF), jax.ShapeDtypeStruct((S, AW), BF),
                   jax.ShapeDtypeStruct((S, KVW), F32), jax.ShapeDtypeStruct((S, KVW), F32),
                   jax.ShapeDtypeStruct((1, AW), F32), jax.ShapeDtypeStruct((1, 128), F32)],
        in_specs=[_SMEM, b512, pl.BlockSpec((BLK, KVW), prev), b128, pl.BlockSpec((BLK, KVW), prev), b128,
                  b512, b512, b512, b512, pl.BlockSpec((1, AW), _const), b128, b128, pl.BlockSpec((AW, AW), _const)],
        out_specs=[b512, b512, _VMEM, _VMEM, _VMEM, _VMEM],
        compiler_params=_params(True),
    )(sinks, qr, kr, kr, vb, vb, d_ya, ga, o, qraw, qw_t, cos_t, sin_t, bq)


HALO = 32


RC = 64
LC = 128


def _windows(ext_ref, r0, l0, base):
    col = ext_ref[r0:r0 + RC + HALO, l0:l0 + LC]
    for s in range(8):
        rolled = col if s == 0 else pltpu.roll(col, RC + HALO - s, 0)
        for t in range(CK):
            if (base + t) % 8 == s:
                a8 = base + t - s
                yield t, rolled[a8:a8 + RC]


def _taps(ext_ref, r0, l0, base, weight_row):
    acc = None
    for t, win in _windows(ext_ref, r0, l0, base):
        term = win * weight_row(t)[:, l0:l0 + LC]
        acc = term if acc is None else acc + term
    return acc


def _conv_fwd_call(z, gb, cwf, conv_b, ln_w, ln_b):
    def body(zc_in_ref, zp_ref, gb_ref, cw_ref, cb_ref, lw_ref, lb_ref, zc_ref, yb_ref, zext):
        i = pl.program_id(0)
        zext[0:HALO, :] = jnp.where(i > 0, zp_ref[TS - HALO:TS, :], 0.0)
        zext[HALO:HALO + TS, :] = zc_in_ref[...]
        for r0 in range(0, TS, RC):
            for l0 in range(0, CW, LC):
                acc = _taps(zext, r0, l0, HALO - (CK - 1), lambda k: cw_ref[k:k + 1, :])
                zc_ref[r0:r0 + RC, l0:l0 + LC] = acc + cb_ref[:, l0:l0 + LC]
        zc = zc_ref[...]
        mu = jnp.mean(zc, axis=-1, keepdims=True)
        dz = zc - mu
        rstd = lax.rsqrt(jnp.mean(dz * dz, axis=-1, keepdims=True) + EPS)
        zn = dz * rstd * lw_ref[...] + lb_ref[...]
        gbv = gb_ref[...]
        yb_ref[...] = (zn * _sigmoid(zn)) * (gbv * _sigmoid(gbv))

    t512 = pl.BlockSpec((TS, CW), _row)
    prev = pl.BlockSpec((TS, CW), lambda i: (jnp.maximum(i - 1, 0), 0))
    c512 = pl.BlockSpec((1, CW), _const)
    return pl.pallas_call(
        body, name="conv_fwd", grid=(NT,),
        out_shape=[jax.ShapeDtypeStruct((S, CW), F32), jax.ShapeDtypeStruct((S, CW), F32)],
        in_specs=[t512, prev, t512, pl.BlockSpec((CKP, CW), _const), c512, c512, c512],
        out_specs=[t512, t512],
        scratch_shapes=[pltpu.VMEM((TS + HALO, CW), F32)],
        compiler_params=_params(True),
    )(z, z, gb, cwf, conv_b, ln_w, ln_b)


def _conv_bwd_ln_call(d_yb, zc, gb, ln_w, ln_b):
    def body(dyb_ref, zc_ref, gb_ref, lw_ref, lb_ref, dzc_ref, dgb_ref, glw_ref, glb_ref, gcb_ref):
        i = pl.program_id(0)

        @pl.when(i == 0)
        def _():
            glw_ref[...] = jnp.zeros((1, CW), F32)
            glb_ref[...] = jnp.zeros((1, CW), F32)
            gcb_ref[...] = jnp.zeros((1, CW), F32)

        zc = zc_ref[...]
        mu = jnp.mean(zc, axis=-1, keepdims=True)
        dz = zc - mu
        rstd = lax.rsqrt(jnp.mean(dz * dz, axis=-1, keepdims=True) + EPS)
        zh = dz * rstd
        lw = lw_ref[...]
        zn = zh * lw + lb_ref[...]
        d_yb = dyb_ref[...]
        act_n, dact_n = _silu_and_grad(zn)
        act_g, dact_g = _silu_and_grad(gb_ref[...])
        dgb_ref[...] = (d_yb * act_n * dact_g).astype(BF)
        d_zn = d_yb * act_g * dact_n
        glw_ref[...] += jnp.sum(d_zn * zh, axis=0, keepdims=True)
        glb_ref[...] += jnp.sum(d_zn, axis=0, keepdims=True)
        dzh = d_zn * lw
        d_zc = rstd * (dzh - jnp.mean(dzh, axis=-1, keepdims=True) - zh * jnp.mean(dzh * zh, axis=-1, keepdims=True))
        dzc_ref[...] = d_zc
        gcb_ref[...] += jnp.sum(d_zc, axis=0, keepdims=True)

    t512 = pl.BlockSpec((TS, CW), _row)
    c512 = pl.BlockSpec((1, CW), _const)
    vec = jax.ShapeDtypeStruct((1, CW), F32)
    return pl.pallas_call(
        body, name="conv_bwd_ln", grid=(NT,),
        out_shape=[jax.ShapeDtypeStruct((S, CW), F32), jax.ShapeDtypeStruct((S, CW), BF), vec, vec, vec],
        in_specs=[t512, t512, t512, c512, c512],
        out_specs=[t512, t512, _VMEM, _VMEM, _VMEM],
        compiler_params=_params(True),
    )(d_yb, zc, gb, ln_w, ln_b)


def _conv_bwd_taps_call(d_zc, z, a, g, cwf):
    def body(dc_ref, dn_ref, zc_ref, zp_ref, a_ref, g_ref, cw_ref, da_ref, dg_ref, gcw_ref, dext, zext, gacc):
        i = pl.program_id(0)

        @pl.when(i == 0)
        def _():
            gacc[...] = jnp.zeros((CKP * 8, CW), F32)

        dext[0:TS, :] = dc_ref[...]
        dext[TS:TS + HALO, :] = jnp.where(i < NT - 1, dn_ref[0:HALO, :], 0.0)
        zext[0:HALO, :] = jnp.where(i > 0, zp_ref[TS - HALO:TS, :], 0.0)
        zext[HALO:HALO + TS, :] = zc_ref[...]
        for r0 in range(0, TS, RC):
            for l0 in range(0, CW, LC):
                d_z = _taps(dext, r0, l0, 0, lambda j: cw_ref[CK - 1 - j:CK - j, :])
                sg = _sigmoid(g_ref[r0:r0 + RC, l0:l0 + LC])
                da_ref[r0:r0 + RC, l0:l0 + LC] = (d_z * sg).astype(BF)
                dg_ref[r0:r0 + RC, l0:l0 + LC] = (d_z * a_ref[r0:r0 + RC, l0:l0 + LC] * sg * (1.0 - sg)).astype(BF)
                d_sub = dc_ref[r0:r0 + RC, l0:l0 + LC]
                for k, win in _windows(zext, r0, l0, HALO - (CK - 1)):
                    prod = d_sub * win
                    part = prod[0:8]
                    for q in range(1, RC // 8):
                        part = part + prod[8 * q:8 * q + 8]
                    gacc[8 * k:8 * k + 8, l0:l0 + LC] += part

        @pl.when(i == NT - 1)
        def _():
            for k in range(CK):
                gcw_ref[k:k + 1, :] = jnp.sum(gacc[8 * k:8 * k + 8, :], axis=0, keepdims=True)
            gcw_ref[CK:CKP, :] = jnp.zeros((CKP - CK, CW), F32)

    t512 = pl.BlockSpec((TS, CW), _row)
    prev = pl.BlockSpec((TS, CW), lambda i: (jnp.maximum(i - 1, 0), 0))
    nxt = pl.BlockSpec((TS, CW), lambda i: (jnp.minimum(i + 1, NT - 1), 0))
    return pl.pallas_call(
        body, name="conv_bwd_taps", grid=(NT,),
        out_shape=[jax.ShapeDtypeStruct((S, CW), BF), jax.ShapeDtypeStruct((S, CW), BF),
                   jax.ShapeDtypeStruct((CKP, CW), F32)],
        in_specs=[t512, nxt, t512, prev, t512, t512, pl.BlockSpec((CKP, CW), _const)],
        out_specs=[t512, t512, _VMEM],
        scratch_shapes=[pltpu.VMEM((TS + HALO, CW), F32), pltpu.VMEM((TS + HALO, CW), F32),
                        pltpu.VMEM((CKP * 8, CW), F32)],
        compiler_params=_params(True),
    )(d_zc, d_zc, z, z, a, g, cwf)


def _out_loss_call(o, ga, yb, x2, tgt, mod, w_out_full):
    def body(o_ref, ga_ref, yb_ref, x_ref, t_ref, mod_ref, w_ref,
             dout_ref, dya_ref, dyb_ref, gw_ref, loss_ref, dgate_ref):
        i = pl.program_id(0)

        @pl.when(i == 0)
        def _():
            gw_ref[...] = jnp.zeros((D, D), F32)
            loss_ref[...] = jnp.zeros((1, 128), F32)
            dgate_ref[...] = jnp.zeros((1, D), F32)

        gav = ga_ref[...]
        ya = o_ref[...] * (gav * _sigmoid(gav))
        ycat = jnp.concatenate([ya, yb_ref[...]], axis=1).astype(BF)
        w = w_ref[...]
        y = jnp.dot(ycat, w, preferred_element_type=F32)
        gate = mod_ref[:, 2 * D:3 * D]
        diff = x_ref[...] + gate * y - t_ref[...]
        sq = jnp.sum(jnp.sum(diff * diff, axis=1, keepdims=True), axis=0, keepdims=True)
        loss_ref[...] += jnp.broadcast_to(sq, (1, 128))
        d_out = diff * (1.0 / D)
        dout_ref[...] = d_out
        dgate_ref[...] += jnp.sum(d_out * y, axis=0, keepdims=True)
        dy = (d_out * gate).astype(BF)
        d_ycat = lax.dot_general(dy, w, (((1,), (1,)), ((), ())), preferred_element_type=F32)
        dya_ref[...] = d_ycat[:, 0:AW]
        dyb_ref[...] = d_ycat[:, AW:D]
        gw_ref[...] += lax.dot_general(ycat, dy, (((0,), (0,)), ((), ())), preferred_element_type=F32)

    t512 = pl.BlockSpec((TS, 512), _row)
    t1024 = pl.BlockSpec((TS, D), _row)
    return pl.pallas_call(
        body, name="out_loss", grid=(NT,),
        out_shape=[jax.ShapeDtypeStruct((S, D), F32), jax.ShapeDtypeStruct((S, AW), F32),
                   jax.ShapeDtypeStruct((S, CW), F32), jax.ShapeDtypeStruct((D, D), F32),
                   jax.ShapeDtypeStruct((1, 128), F32), jax.ShapeDtypeStruct((1, D), F32)],
        in_specs=[t512, t512, t512, t1024, t1024, pl.BlockSpec((1, 3 * D), _const),
                  pl.BlockSpec((D, D), _const, pipeline_mode=pl.Buffered(1))],
        out_specs=[t1024, t512, t512, _VMEM, _VMEM, _VMEM],
        compiler_params=_params(True),
    )(o, ga, yb, x2, tgt, mod, w_out_full)


SM_ROWS = 8
PIECES = ((0, 512), (512, 640), (640, 768), (768, 1280), (1280, 1792), (1792, 2304), (2304, 2816))


def _bwd_in_call(dqraw, dk, dv, dga, da, dg, dgb, kraw, kw_t, cos_t, sin_t, bk, h, wt_full, x2, d_out, mod, norm_w,
                 gw_out, gcw, glw, glb, gcb, gqw, gsink, dgate, loss_p, cact_all):
    def body(dq_ref, dk_ref, dv_ref, dga_ref, da_ref, dg_ref, dgb_ref, kraw_ref, kw_ref, cos_ref, sin_ref, bk_ref,
             h_ref, wt_ref, x_ref, dout_ref, mod_ref, nw_ref, gwout_ref, gcw_ref, glw_ref, glb_ref, gcb_ref, gqw_ref,
             gsink_ref, dgate_ref, loss_ref, cact_ref,
             gx_ref, o_gwin, o_gwout, o_gwada, o_gbada, o_gnw, o_gqw, o_gkw, o_gsink, o_gcw, o_gcb, o_glw, o_glb,
             o_loss,
             acc, win_send, win_sib, win_ici, wout_send, wout_sib, wout_ici, sm_buf, cw_buf, dmod_all, vec_acc, gkw_acc,
             wi_ds, wi_dr, wi_is, wi_ir, wo_ds, wo_dr, wo_is, wo_ir, sm_s, sm_r, cw_s, cw_r):
        i = pl.program_id(0)
        pos = _mesh_pos()
        x, y, cc = pos
        me = 4 * x + 2 * y + cc

        def chip(j):
            return (1 - x if j & 1 else x, 1 - y if j & 2 else y)

        def rows_of(buf, px, py, pc, rows, align):
            return buf.at[pl.ds(pl.multiple_of((4 * px + 2 * py + pc) * rows, align), rows), :]

        bufs = {"in": (win_send, win_sib, win_ici, IN_SHARD, wi_ds, wi_dr, wi_is, wi_ir),
                "out": (wout_send, wout_sib, wout_ici, OUT_SHARD, wo_ds, wo_dr, wo_is, wo_ir)}

        def d2d_copy(j, which):
            send, sib, _, rows, ds_, dr_, _, _ = bufs[which]
            px, py = chip(j)
            return pltpu.make_async_remote_copy(src_ref=rows_of(send, px, py, 1 - cc, rows, 16), dst_ref=sib.at[j],
                                                send_sem=ds_.at[j], recv_sem=dr_.at[j], device_id=(x, y, 1 - cc),
                                                device_id_type=MESH)

        def ici_copy(j, which):
            send, _, ici, rows, _, _, is_, ir_ = bufs[which]
            px, py = chip(j)
            return pltpu.make_async_remote_copy(src_ref=rows_of(send, px, py, cc, rows, 16), dst_ref=ici.at[j - 1],
                                                send_sem=is_.at[j - 1], recv_sem=ir_.at[j - 1], device_id=(px, py, cc),
                                                device_id_type=MESH)

        def level2(which, partial_ref):
            send, sib, _, rows, _, _, _, _ = bufs[which]
            for j in range(1, 4):
                d2d_copy(j, which).wait_recv()
                px, py = chip(j)
                mine = rows_of(partial_ref, px, py, cc, rows, 8)[...]
                rows_of(send, px, py, cc, rows, 16)[...] = (mine + sib[j].astype(F32)).astype(BF)
                ici_copy(j, which).start()

        def finish(which, partial_ref):
            _, sib, ici, rows, _, _, _, _ = bufs[which]
            d2d_copy(0, which).wait_recv()
            total = rows_of(partial_ref, x, y, cc, rows, 8)[...] + sib[0].astype(F32)
            for j in range(1, 4):
                ici_copy(j, which).wait_recv()
                total = total + ici[j - 1].astype(F32)
            for j in range(4):
                d2d_copy(j, which).wait_send()
            for j in range(1, 4):
                ici_copy(j, which).wait_send()
            return total

        def dproj_pieces():
            dk_raw, g_kw = _norm_rope_bwd(dk_ref[...], kraw_ref[...], kw_ref[...], cos_ref[...], sin_ref[...], bk_ref)
            return [dq_ref[...], dk_raw.astype(BF), dv_ref[...].astype(BF), dga_ref[...], da_ref[...], dg_ref[...],
                    dgb_ref[...]], g_kw

        @pl.when(i == 0)
        def _():
            acc[...] = jnp.zeros((INW, D), F32)
            vec_acc[...] = jnp.zeros((8, D), F32)
            gkw_acc[...] = jnp.zeros((1, KVW), F32)
            wout_send[...] = gwout_ref[...].astype(BF)
            for j in range(4):
                d2d_copy(j, "out").start()

        @pl.when(i == 2)
        def _():
            level2("out", gwout_ref)

        @pl.when(i < NT)
        def _():
            pieces, g_kw = dproj_pieces()
            gkw_acc[...] += g_kw
            hv = h_ref[...]
            for (lo, hi), piece in zip(PIECES, pieces):
                acc[lo:hi, :] += lax.dot_general(piece, hv, (((0,), (0,)), ((), ())), preferred_element_type=F32)

        @pl.when(i == NT - 1)
        def _():
            for lo, hi in PIECES:
                win_send[lo:hi, :] = acc[lo:hi, :].astype(BF)
            for j in range(4):
                d2d_copy(j, "in").start()

        @pl.when(i == NT + 2)
        def _():
            level2("in", acc)

        @pl.when(i >= NT)
        def _():
            pieces, _ = dproj_pieces()
            dproj = jnp.concatenate(pieces, axis=1)
            d_h = jnp.dot(dproj, wt_ref[...], preferred_element_type=F32)
            xv = x_ref[...]
            scale = mod_ref[:, D:2 * D]
            nw = nw_ref[...]
            r = lax.rsqrt(jnp.mean(xv * xv, axis=-1, keepdims=True) + EPS)
            xn = xv * r
            vec_acc[0:1, :] += jnp.sum(d_h, axis=0, keepdims=True)
            vec_acc[1:2, :] += jnp.sum(d_h * (xn * nw), axis=0, keepdims=True)
            d_u = d_h * (1.0 + scale)
            vec_acc[2:3, :] += jnp.sum(d_u * xn, axis=0, keepdims=True)
            d_xn = d_u * nw
            gx_ref[...] = dout_ref[...] + r * (d_xn - xn * jnp.mean(d_xn * xn, axis=-1, keepdims=True))

        @pl.when(i == 2 * NT - 1)
        def _():
            ch_sm = (_slab(sm_buf), sm_s, sm_r)
            ch_cw = (_slab(cw_buf), cw_s, cw_r)
            z128 = jnp.zeros((1, 128), F32)
            row4 = jnp.concatenate([glw_ref[...], glb_ref[...]], axis=1)
            row5 = jnp.concatenate([gcb_ref[...], gqw_ref[...]], axis=1)
            row6 = jnp.concatenate([gkw_acc[...], gsink_ref[...], loss_ref[...]] + [z128] * 5, axis=1)
            sm_buf[me] = jnp.concatenate([vec_acc[0:2, :], dgate_ref[...], vec_acc[2:3, :], row4, row5, row6,
                                          jnp.zeros((1, D), F32)], axis=0)
            f_sm = _ag_start(ch_sm, pos)
            cw_buf[me] = gcw_ref[...]
            f_cw = _ag_start(ch_cw, pos)
            _ag_finish(ch_sm, pos, f_sm)
            _ag_finish(ch_cw, pos, f_cw)
            tot = sm_buf[0]
            cw_tot = cw_buf[0]
            for d in range(1, NDEV):
                tot = tot + sm_buf[d]
                cw_tot = cw_tot + cw_buf[d]
            o_gbada[...] = jnp.concatenate([tot[0:1, :], tot[1:2, :], tot[2:3, :]], axis=1)
            o_gnw[...] = tot[3:4, :]
            o_glw[...] = tot[4:5, 0:CW]
            o_glb[...] = tot[4:5, CW:D]
            o_gcb[...] = tot[5:6, 0:CW]
            gq = tot[5:6, CW:CW + HD]
            for hh in range(1, NQ):
                gq = gq + tot[5:6, CW + HD * hh:CW + HD * (hh + 1)]
            o_gqw[...] = gq
            o_gkw[...] = tot[6:7, 0:HD] + tot[6:7, HD:2 * HD]
            o_gsink[...] = tot[6:7, 128:128 + NQ]
            o_loss[...] = tot[6:7, 256:384] * (0.5 / D)
            mine = jnp.zeros((CK, CONV_SHARD), F32)
            for d in range(NDEV):
                mine = mine + jnp.where(me == d, cw_tot[0:CK, CONV_SHARD * d:CONV_SHARD * (d + 1)], 0.0)
            o_gcw[...] = mine
            for d in range(NDEV):
                dmod_all[d:d + 1, :] = jnp.concatenate([sm_buf[d, 0:1, :], sm_buf[d, 1:2, :], sm_buf[d, 2:3, :]],
                                                       axis=1)
            col0 = pl.multiple_of(me * ADA_SHARD, 128)
            o_gwada[...] = lax.dot_general(cact_ref[...], dmod_all[:, pl.ds(col0, ADA_SHARD)], (((0,), (0,)), ((), ())),
                                           preferred_element_type=F32, precision=lax.Precision.HIGHEST)

            o_gwout[...] = finish("out", gwout_ref)
            o_gwin[...] = finish("in", acc)

    half = lambda i: (i % NT, 0)
    late = lambda i: (jnp.maximum(i - NT, 0), 0)
    t512 = pl.BlockSpec((TS, 512), half)
    t128 = pl.BlockSpec((TS, 128), half)
    l1024 = pl.BlockSpec((TS, D), late)
    sem7 = pltpu.SemaphoreType.DMA((7,))
    sem4 = pltpu.SemaphoreType.DMA((4,))
    sem3 = pltpu.SemaphoreType.DMA((3,))
    sds = jax.ShapeDtypeStruct
    return pl.pallas_call(
        body, name="bwd_in", grid=(2 * NT,),
        out_shape=[sds((S, D), F32), sds((IN_SHARD, D), F32), sds((OUT_SHARD, D), F32), sds((D, ADA_SHARD), F32),
                   sds((1, 3 * D), F32), sds((1, D), F32), sds((1, HD), F32), sds((1, HD), F32), sds((1, NQ), F32),
                   sds((CK, CONV_SHARD), F32), sds((1, CW), F32), sds((1, CW), F32), sds((1, CW), F32),
                   sds((1, 128), F32)],
        in_specs=[t512, t128, t128, t512, t512, t512, t512, t128, pl.BlockSpec((1, KVW), _const), t128, t128,
                  pl.BlockSpec((KVW, KVW), _const), pl.BlockSpec((TS, D), half),
                  pl.BlockSpec((INW, D), _const, pipeline_mode=pl.Buffered(1)), l1024, l1024,
                  pl.BlockSpec((1, 3 * D), _const), pl.BlockSpec((1, D), _const)] + [_VMEM] * 10,
        out_specs=[l1024] + [_VMEM] * 13,
        scratch_shapes=[pltpu.VMEM((INW, D), F32), pltpu.VMEM((INW, D), BF), pltpu.VMEM((4, IN_SHARD, D), BF),
                        pltpu.VMEM((3, IN_SHARD, D), BF), pltpu.VMEM((D, D), BF), pltpu.VMEM((4, OUT_SHARD, D), BF),
                        pltpu.VMEM((3, OUT_SHARD, D), BF),
                        pltpu.VMEM((NDEV, SM_ROWS, D), F32), pltpu.VMEM((NDEV, CKP, CW), F32),
                        pltpu.VMEM((NDEV, 3 * D), F32), pltpu.VMEM((8, D), F32), pltpu.VMEM((1, KVW), F32)]
        + [sem4, sem4, sem3, sem3] * 2 + [sem7] * 4,
        compiler_params=pltpu.CompilerParams(dimension_semantics=("arbitrary",), vmem_limit_bytes=BIG_VMEM_LIMIT),
    )(dqraw, dk, dv, dga, da, dg, dgb, kraw, kw_t, cos_t, sin_t, bk, h, wt_full, x2, d_out, mod, norm_w,
      gw_out, gcw, glw, glb, gcb, gqw, gsink, dgate, loss_p, cact_all)


def _adam_call(ws, gs, ms, vs):
    n = len(ws)
    bc1 = 1.0 - ADAM_B1 ** ADAM_STEP
    bc2 = 1.0 - ADAM_B2 ** ADAM_STEP

    def body(*refs):
        ins, outs = refs[:4 * n], refs[4 * n:]
        for j in range(n):
            w, g, m, v = (ins[j][...], ins[n + j][...], ins[2 * n + j][...], ins[3 * n + j][...])
            m_new = ADAM_B1 * m + (1.0 - ADAM_B1) * g
            v_new = ADAM_B2 * v + (1.0 - ADAM_B2) * (g * g)
            m_hat = m_new / bc1
            v_hat = v_new / bc2
            outs[j][...] = -ADAM_LR * (m_hat / (jnp.sqrt(v_hat) + ADAM_EPS) + ADAM_WD * w)
            outs[n + j][...] = m_new
            outs[2 * n + j][...] = v_new

    shapes = [jax.ShapeDtypeStruct(w.shape, F32) for w in ws]
    return pl.pallas_call(
        body, name="adam",
        out_shape=shapes * 3, in_specs=[_VMEM] * (4 * n), out_specs=[_VMEM] * (3 * n),
        compiler_params=_params(),
    )(*ws, *gs, *ms, *vs)


def _rope_tables():
    inv = ROPE_THETA ** (-jnp.arange(0, HD, 2, dtype=F32) / HD)
    ang = jnp.arange(S, dtype=F32)[:, None] * inv[None, :]
    cos, sin = jnp.cos(ang), jnp.sin(ang)
    cos64 = jnp.concatenate([cos, cos], axis=-1)
    sin64 = jnp.concatenate([-sin, sin], axis=-1)
    return jnp.tile(cos64, (1, 2)), jnp.tile(sin64, (1, 2))


def _group_matrix(width):
    idx = jnp.arange(width) // HD
    return jnp.where(idx[:, None] == idx[None, :], 1.0 / HD, 0.0).astype(BF)


def kernel(x, c, w_ada, b_ada, norm_w, w_in, q_norm_w, k_norm_w, sinks, conv_w, conv_b, ln_w, ln_b, w_out, loss_target, m_w_ada, m_b_ada, m_norm_w, m_w_in, m_q_norm_w, m_k_norm_w, m_sinks, m_conv_w, m_conv_b, m_ln_w, m_ln_b, m_w_out, v_w_ada, v_b_ada, v_norm_w, v_w_in, v_q_norm_w, v_k_norm_w, v_sinks, v_conv_w, v_conv_b, v_ln_w, v_ln_b, v_w_out):
    x2 = x[0]
    tgt = loss_target[0]
    cos_t, sin_t = _rope_tables()
    bq = _group_matrix(AW)
    bk = _group_matrix(KVW)
    qw_t = jnp.tile(q_norm_w, (1, NQ))
    kw_t = jnp.tile(k_norm_w, (1, NKV))

    tr = lambda t: jnp.swapaxes(t[0], 0, 1)
    wt_full, w_out_full, cwf, mod, cact_all = _gather_call(tr(w_in), w_out[0], conv_w[0], c, w_ada[0], b_ada)

    h, qraw, kraw, ga, a, g, gb, qr, kr, vb, z = _fwd_in_call(x2, mod, norm_w, qw_t, kw_t, cos_t, sin_t, bq, bk, wt_full)
    o = _attn_fwd_call(sinks, qr, kr, vb)
    zc, yb = _conv_fwd_call(z, gb, cwf, conv_b, ln_w, ln_b)
    d_out, d_ya, d_yb, gw_out, loss_p, dgate = _out_loss_call(o, ga, yb, x2, tgt, mod, w_out_full)

    d_zc, dgb, glw, glb, gcb = _conv_bwd_ln_call(d_yb, zc, gb, ln_w, ln_b)
    da, dg, gcw = _conv_bwd_taps_call(d_zc, z, a, g, cwf)
    dqraw, dga, dk, dv, gqw, gsink = _attn_bwd_call(sinks, qr, kr, vb, d_ya, ga, o, qraw, qw_t, cos_t, sin_t, bq)
    (grad_x, g_w_in_t, g_w_out, g_w_ada, g_b_ada, g_norm_w, g_qw, g_kw, g_sinks, g_conv_w, g_conv_b, g_ln_w, g_ln_b,
     loss_v) = _bwd_in_call(dqraw, dk, dv, dga, da, dg, dgb, kraw, kw_t, cos_t, sin_t, bk, h, wt_full, x2, d_out, mod,
                            norm_w, gw_out, gcw, glw, glb, gcb, gqw, gsink, dgate, loss_p, cact_all)

    ws = [w_ada[0], b_ada, norm_w, tr(w_in), q_norm_w, k_norm_w, sinks, conv_w[0], conv_b, ln_w, ln_b, w_out[0]]
    gs = [g_w_ada, g_b_ada, g_norm_w, g_w_in_t, g_qw, g_kw, g_sinks, g_conv_w, g_conv_b, g_ln_w, g_ln_b, g_w_out]
    ms = [m_w_ada[0], m_b_ada, m_norm_w, tr(m_w_in), m_q_norm_w, m_k_norm_w, m_sinks, m_conv_w[0], m_conv_b, m_ln_w,
          m_ln_b, m_w_out[0]]
    vs = [v_w_ada[0], v_b_ada, v_norm_w, tr(v_w_in), v_q_norm_w, v_k_norm_w, v_sinks, v_conv_w[0], v_conv_b, v_ln_w,
          v_ln_b, v_w_out[0]]
    upd = _adam_call(ws, gs, ms, vs)
    n = len(ws)
    shaped = [w_ada, b_ada, norm_w, w_in, q_norm_w, k_norm_w, sinks, conv_w, conv_b, ln_w, ln_b, w_out]
    W_IN_POS = 3

    def like(vals):
        vals = [jnp.swapaxes(v, 0, 1) if j == W_IN_POS else v for j, v in enumerate(vals)]
        return [v.reshape(s.shape) for v, s in zip(vals, shaped)]

    return (loss_v[0, 0], grad_x[None], *like(gs), *like(upd[0:n]), *like(upd[n:2 * n]), *like(upd[2 * n:3 * n]))
```

```python
import functools

import jax
import jax.numpy as jnp
from jax import lax
from jax.experimental import pallas as pl
from jax.experimental.pallas import tpu as pltpu

S = 2048
D = 1024
NDEV = 8
HD = 64
NQ = 8
NKV = 2
AW = 512
KVW = 128
CW = 512
INW = 2816
IN_SHARD = INW // NDEV
ADA_SHARD = 3 * D // NDEV
OUT_SHARD = D // NDEV
CONV_SHARD = CW // NDEV
CK = 31
CKP = 32
BLK = 128
TS = 256
NT = S // TS
NB = S // BLK
EPS = 1e-6
ROPE_THETA = 10000.0
NEG = -1e30
BF = jnp.bfloat16
F32 = jnp.float32

ADAM_LR = 0.001
ADAM_B1 = 0.9
ADAM_B2 = 0.999
ADAM_EPS = 1e-08
ADAM_WD = 0.01
ADAM_STEP = 10

VMEM_LIMIT = 56 * 1024 * 1024
BIG_VMEM_LIMIT = 62 * 1024 * 1024
MESH = pl.DeviceIdType.MESH

_VMEM = pl.BlockSpec(memory_space=pltpu.VMEM)
_SMEM = pl.BlockSpec(memory_space=pltpu.SMEM)
_ANY = pl.BlockSpec(memory_space=pl.ANY)


def _params(grid=False):
    if grid:
        return pltpu.CompilerParams(dimension_semantics=("arbitrary",), vmem_limit_bytes=VMEM_LIMIT)
    return pltpu.CompilerParams(vmem_limit_bytes=VMEM_LIMIT)


def _row(i):
    return (i, 0)


def _const(i):
    return (0, 0)


def _sigmoid(t):
    return 1.0 / (1.0 + jnp.exp(-t))


def _silu_and_grad(t):
    sg = _sigmoid(t)
    return t * sg, sg * (1.0 + t * (1.0 - sg))


def _group_mean(t, b_ref):
    hi = t.astype(BF)
    lo = (t - hi.astype(F32)).astype(BF)
    b = b_ref[...]
    return jnp.dot(hi, b, preferred_element_type=F32) + jnp.dot(lo, b, preferred_element_type=F32)


def _partner(t):
    w = t.shape[-1]
    lane = lax.broadcasted_iota(jnp.int32, t.shape, 1)
    first = (lane & 32) == 0
    return jnp.where(first, pltpu.roll(t, w - 32, 1), pltpu.roll(t, 32, 1))


def _norm_rope_fwd(t, w_t, cos, sin, b_ref):
    r = lax.rsqrt(_group_mean(t * t, b_ref) + EPS)
    tn = t * r * w_t
    return tn * cos + _partner(tn) * sin


def _norm_rope_bwd(d_out, t, w_t, cos, sin, b_ref):
    d_tn = d_out * cos + _partner(d_out * sin)
    r = lax.rsqrt(_group_mean(t * t, b_ref) + EPS)
    th = t * r
    g_w = jnp.sum(d_tn * th, axis=0, keepdims=True)
    d_th = d_tn * w_t
    d_t = r * (d_th - th * _group_mean(d_th * th, b_ref))
    return d_t, g_w


def _mesh_pos():
    return lax.axis_index("x"), lax.axis_index("y"), lax.axis_index("c")


def _ag_copy(chan, k, block, to):
    blk, send_sems, recv_sems = chan
    ref = blk(*block)
    return pltpu.make_async_remote_copy(src_ref=ref, dst_ref=ref, send_sem=send_sems.at[k],
                                        recv_sem=recv_sems.at[k], device_id=to, device_id_type=MESH)


def _ag_start(chan, pos):
    x, y, c = pos
    me = (x, y, c)
    chips = [(1 - x, y), (x, 1 - y), (1 - x, 1 - y)]
    first = [_ag_copy(chan, 0, me, (x, y, 1 - c))]
    first += [_ag_copy(chan, 1 + j, me, (*chip, c)) for j, chip in enumerate(chips)]
    for cp in first:
        cp.start()
    return first


def _ag_finish(chan, pos, first):
    x, y, c = pos
    me = (x, y, c)
    sibling = (x, y, 1 - c)
    chips = [(1 - x, y), (x, 1 - y), (1 - x, 1 - y)]
    passed = [_ag_copy(chan, 4 + j, (*chip, c), sibling) for j, chip in enumerate(chips)]
    for j, chip in enumerate(chips):
        _ag_copy(chan, 1 + j, (*chip, c), me).wait_recv()
        passed[j].start()
    _ag_copy(chan, 0, sibling, me).wait_recv()
    for j, chip in enumerate(chips):
        _ag_copy(chan, 4 + j, (*chip, 1 - c), me).wait_recv()
    for cp in first + passed:
        cp.wait_send()


def _slab(buf):
    return lambda px, py, pc: buf.at[4 * px + 2 * py + pc]


def _row_block(buf, rows, align):
    return lambda px, py, pc: buf.at[pl.ds(pl.multiple_of((4 * px + 2 * py + pc) * rows, align), rows), :]


HALF = INW // 2


def _gather_fwd_call(w_in_t, w_out_s, conv_w_s, c, w_ada_s, b_ada, x2, norm_w, qw_t, kw_t, cos_t, sin_t, bq, bk):
    def body(win_ref, wout_ref, cw_ref, c_ref, wada_ref, bada_ref, x_ref, nw_ref, qw_ref, kw_ref, cos_ref, sin_ref,
             bq_ref, bk_ref,
             wtf_hbm, woutf_hbm, cwf_ref, mod_ref, cact_ref,
             h_ref, qraw_ref, kraw_ref, ga_ref, a_ref, g_ref, gb_ref, qr_ref, kr_ref, vb_ref, z_ref,
             cw_buf, ca_buf, mp_buf, h_s, raw0, pt, wtf_ref, woutf_ref,
             s0, r0, s1, r1, s2, r2, s3, r3, s4, r4, out_sems):
        s = pl.program_id(0)
        pos = _mesh_pos()
        x, y, cc = pos
        me3 = (x, y, cc)
        me = 4 * x + 2 * y + cc
        sibling = (x, y, 1 - cc)
        chips = [(1 - x, y), (x, 1 - y), (1 - x, 1 - y)]
        ch_win = (_row_block(wtf_ref, IN_SHARD, 16), s0, r0)
        ch_wout = (_row_block(woutf_ref, OUT_SHARD, 16), s1, r1)
        ch_cw = (_slab(cw_buf), s2, r2)
        ch_ca = (_slab(ca_buf), s3, r3)
        ch_mp = (_slab(mp_buf), s4, r4)

        def first(chan, j):
            return _ag_copy(chan, j, me3, sibling if j == 0 else (*chips[j - 1], cc))

        def passed(chan, j):
            return _ag_copy(chan, 4 + j, (*chips[j], cc), sibling)

        def landed(chan, j):
            return _ag_copy(chan, 1 + j, (*chips[j], cc), me3)

        def relayed(chan, j):
            return _ag_copy(chan, 4 + j, (*chips[j], 1 - cc), me3)

        def from_sibling(chan):
            return _ag_copy(chan, 0, sibling, me3)

        @pl.when(s == 0)
        def _():
            cv = c_ref[...]
            ca_buf[me] = jnp.broadcast_to(cv * _sigmoid(cv), (8, D))
            f_ca = _ag_start(ch_ca, pos)
            wtf_ref[pl.ds(pl.multiple_of(me * IN_SHARD, 16), IN_SHARD), :] = win_ref[...].astype(BF)
            for j in range(3):
                first(ch_win, j).start()
            cw_buf[me] = cw_ref[...]
            f_cw = _ag_start(ch_cw, pos)

            _ag_finish(ch_ca, pos, f_ca)
            cact_all = jnp.concatenate([ca_buf[d, 0:1, :] for d in range(NDEV)], axis=0)
            cact_ref[...] = cact_all
            col0 = pl.multiple_of(me * ADA_SHARD, 128)
            mp_buf[me] = jnp.dot(cact_all, wada_ref[...], preferred_element_type=F32,
                                 precision=lax.Precision.HIGHEST) + bada_ref[:, pl.ds(col0, ADA_SHARD)]
            f_mp = _ag_start(ch_mp, pos)
            _ag_finish(ch_cw, pos, f_cw)
            _ag_finish(ch_mp, pos, f_mp)
            for d in range(NDEV):
                cwf_ref[0:CK, CONV_SHARD * d:CONV_SHARD * (d + 1)] = cw_buf[d]
            cwf_ref[CK:CKP, :] = jnp.zeros((CKP - CK, CW), F32)
            mod_ref[...] = jnp.concatenate([mp_buf[d, pl.ds(me, 1), :] for d in range(NDEV)], axis=1)

            for j in (1, 0):
                landed(ch_win, j).wait_recv()
                passed(ch_win, j).start()
            from_sibling(ch_win).wait_recv()
            relayed(ch_win, 1).wait_recv()
            first(ch_win, 1).wait_send()
            first(ch_win, 2).wait_send()
            first(ch_win, 3).start()
            wout = wout_ref[...].astype(BF)
            woutf_ref[pl.ds(pl.multiple_of(me * OUT_SHARD, 16), OUT_SHARD), :] = wout
            for j in range(4):
                first(ch_wout, j).start()

        row0 = pl.multiple_of((s % NT) * TS, TS)

        @pl.when(s < NT)
        def _():
            xv = x_ref[...]
            shift = mod_ref[:, 0:D]
            scale = mod_ref[:, D:2 * D]
            r = lax.rsqrt(jnp.mean(xv * xv, axis=-1, keepdims=True) + EPS)
            hb = ((xv * r * nw_ref[...]) * (1.0 + scale) + shift).astype(BF)
            h_s[pl.ds(row0, TS), :] = hb
            w_half = wtf_ref[pl.ds(pl.multiple_of(x * HALF, 16), HALF), :]
            raw0[pl.ds(row0, TS), :] = lax.dot_general(hb, w_half, (((1,), (1,)), ((), ())),
                                                       preferred_element_type=F32)

        @pl.when(s == NT)
        def _():
            relayed(ch_win, 0).wait_recv()
            landed(ch_win, 2).wait_recv()
            passed(ch_win, 2).start()
            relayed(ch_win, 2).wait_recv()
            pltpu.make_async_copy(wtf_ref, wtf_hbm, out_sems.at[0]).start()

        @pl.when(s >= NT)
        def _():
            hb = h_s[pl.ds(row0, TS), :]
            h_ref[...] = hb
            w_half = wtf_ref[pl.ds(pl.multiple_of((1 - x) * HALF, 16), HALF), :]
            raw1 = lax.dot_general(hb, w_half, (((1,), (1,)), ((), ())), preferred_element_type=F32)
            pt[:, pl.ds(pl.multiple_of(x * HALF, 128), HALF)] = raw0[pl.ds(row0, TS), :]
            pt[:, pl.ds(pl.multiple_of((1 - x) * HALF, 128), HALF)] = raw1
            cos = cos_ref[...]
            sin = sin_ref[...]
            q = pt[:, 0:512]
            qraw_ref[...] = q
            qr_ref[...] = _norm_rope_fwd(q, qw_ref[...], jnp.tile(cos, (1, 4)), jnp.tile(sin, (1, 4)),
                                         bq_ref).astype(BF)
            k = pt[:, 512:640]
            kraw_ref[...] = k
            kr_ref[...] = _norm_rope_fwd(k, kw_ref[...], cos, sin, bk_ref).astype(BF)
            vb_ref[...] = pt[:, 640:768].astype(BF)
            ga_ref[...] = pt[:, 768:1280]
            a = pt[:, 1280:1792]
            g = pt[:, 1792:2304]
            a_ref[...] = a
            g_ref[...] = g
            z_ref[...] = a * _sigmoid(g)
            gb_ref[...] = pt[:, 2304:2816]

        @pl.when(s == 2 * NT - 1)
        def _():
            for j in range(3):
                landed(ch_wout, j).wait_recv()
                passed(ch_wout, j).start()
            from_sibling(ch_wout).wait_recv()
            for j in range(3):
                relayed(ch_wout, j).wait_recv()
            out_copy = pltpu.make_async_copy(woutf_ref, woutf_hbm, out_sems.at[1])
            out_copy.start()
            pltpu.make_async_copy(wtf_ref, wtf_hbm, out_sems.at[0]).wait()
            out_copy.wait()
            first(ch_win, 0).wait_send()
            first(ch_win, 3).wait_send()
            for j in range(3):
                passed(ch_win, j).wait_send()
                passed(ch_wout, j).wait_send()
            for j in range(4):
                first(ch_wout, j).wait_send()

    early = lambda i: (jnp.minimum(i, NT - 1), 0)
    late = lambda i: (jnp.maximum(i - NT, 0), 0)
    t512 = pl.BlockSpec((TS, 512), late)
    t128 = pl.BlockSpec((TS, 128), late)
    sem = pltpu.SemaphoreType.DMA((7,))
    sds = jax.ShapeDtypeStruct
    return pl.pallas_call(
        body, name="gather_fwd", grid=(2 * NT,),
        out_shape=[sds((INW, D), BF), sds((D, D), BF), sds((CKP, CW), F32), sds((1, 3 * D), F32), sds((NDEV, D), F32),
                   sds((S, D), BF), sds((S, AW), F32), sds((S, KVW), F32), sds((S, AW), F32), sds((S, CW), F32),
                   sds((S, CW), F32), sds((S, CW), F32), sds((S, AW), BF), sds((S, KVW), BF), sds((S, KVW), BF),
                   sds((S, CW), F32)],
        in_specs=[_VMEM] * 6 + [pl.BlockSpec((TS, D), early), pl.BlockSpec((1, D), _const),
                                pl.BlockSpec((1, AW), _const), pl.BlockSpec((1, KVW), _const), t128, t128,
                                pl.BlockSpec((AW, AW), _const), pl.BlockSpec((KVW, KVW), _const)],
        out_specs=[_ANY, _ANY] + [_VMEM] * 3 + [pl.BlockSpec((TS, D), late), t512, t128, t512, t512, t512, t512, t512,
                                                t128, t128, t512],
        scratch_shapes=[pltpu.VMEM((NDEV, CK, CONV_SHARD), F32), pltpu.VMEM((NDEV, 8, D), F32),
                        pltpu.VMEM((NDEV, 8, ADA_SHARD), F32), pltpu.VMEM((S, D), BF), pltpu.VMEM((S, HALF), F32),
                        pltpu.VMEM((TS, INW), F32), pltpu.VMEM((INW, D), BF), pltpu.VMEM((D, D), BF)]
        + [sem] * 10 + [pltpu.SemaphoreType.DMA((2,))],
        compiler_params=_params(True),
    )(w_in_t, w_out_s, conv_w_s, c, w_ada_s, b_ada, x2, norm_w, qw_t, kw_t, cos_t, sin_t, bq, bk)


def _band_mask(i):
    qi = lax.broadcasted_iota(jnp.int32, (4 * BLK, 2 * BLK), 0) & (BLK - 1)
    kj = lax.broadcasted_iota(jnp.int32, (4 * BLK, 2 * BLK), 1)
    dist = qi + BLK - kj
    return (dist >= 0) & (dist < BLK) & ((kj >= BLK) | (i > 0))


def _sink_rows(sink_ref, g):
    row = lax.broadcasted_iota(jnp.int32, (4 * BLK, 1), 0)
    return jnp.where(row < BLK, sink_ref[0, 4 * g],
                     jnp.where(row < 2 * BLK, sink_ref[0, 4 * g + 1],
                               jnp.where(row < 3 * BLK, sink_ref[0, 4 * g + 2], sink_ref[0, 4 * g + 3])))


def _stack_heads(t, g):
    return jnp.concatenate([t[:, HD * (4 * g + h):HD * (4 * g + h + 1)] for h in range(4)], axis=0)


def _band(prev, cur, g):
    return jnp.concatenate([prev[:, HD * g:HD * (g + 1)], cur[:, HD * g:HD * (g + 1)]], axis=0)


def _softmax_band(qs, kb, mask, sink):
    s = lax.dot_general(qs, kb, (((1,), (1,)), ((), ())), preferred_element_type=F32) * (HD ** -0.5)
    s = jnp.where(mask, s, NEG)
    m = jnp.maximum(jnp.max(s, axis=-1, keepdims=True), sink)
    e = jnp.exp(s - m)
    es = jnp.exp(sink - m)
    inv = 1.0 / (jnp.sum(e, axis=-1, keepdims=True) + es)
    return e * inv, es * inv


def _attn_fwd_call(sinks, qr, kr, vb):
    def body(sink_ref, q_ref, kp_ref, kc_ref, vp_ref, vc_ref, o_ref):
        i = pl.program_id(0)
        mask = _band_mask(i)
        q = q_ref[...]
        kp, kc, vp, vc = kp_ref[...], kc_ref[...], vp_ref[...], vc_ref[...]
        for g in range(NKV):
            p, _ = _softmax_band(_stack_heads(q, g), _band(kp, kc, g), mask, _sink_rows(sink_ref, g))
            o = jnp.dot(p.astype(BF), _band(vp, vc, g), preferred_element_type=F32)
            for h in range(4):
                o_ref[:, HD * (4 * g + h):HD * (4 * g + h + 1)] = o[BLK * h:BLK * (h + 1), :]

    prev = lambda i: (jnp.maximum(i - 1, 0), 0)
    return pl.pallas_call(
        body, name="attn_fwd", grid=(NB,),
        out_shape=jax.ShapeDtypeStruct((S, AW), F32),
        in_specs=[_SMEM, pl.BlockSpec((BLK, AW), _row), pl.BlockSpec((BLK, KVW), prev), pl.BlockSpec((BLK, KVW), _row),
                  pl.BlockSpec((BLK, KVW), prev), pl.BlockSpec((BLK, KVW), _row)],
        out_specs=pl.BlockSpec((BLK, AW), _row),
        compiler_params=_params(True),
    )(sinks, qr, kr, kr, vb, vb)


def _attn_bwd_call(sinks, qr, kr, vb, d_ya, ga, o, qraw, qw_t, cos_t, sin_t, bq):
    def body(sink_ref, q_ref, kp_ref, kc_ref, vp_ref, vc_ref, dya_ref, ga_ref, o_ref, qraw_ref, qw_ref,
             cos_ref, sin_ref, bq_ref,
             dqraw_ref, dga_ref, dk_ref, dv_ref, gqw_ref, gsink_ref):
        i = pl.program_id(0)

        @pl.when(i == 0)
        def _():
            dk_ref[...] = jnp.zeros((S, KVW), F32)
            dv_ref[...] = jnp.zeros((S, KVW), F32)
            gqw_ref[...] = jnp.zeros((1, AW), F32)
            gsink_ref[...] = jnp.zeros((1, 128), F32)

        mask = _band_mask(i)
        q = q_ref[...]
        kp, kc, vp, vc = kp_ref[...], kc_ref[...], vp_ref[...], vc_ref[...]
        d_ya = dya_ref[...]
        act, dact = _silu_and_grad(ga_ref[...])
        dga_ref[...] = (d_ya * o_ref[...] * dact).astype(BF)
        d_o = (d_ya * act).astype(BF)
        lane = lax.broadcasted_iota(jnp.int32, (1, 128), 1)
        row = lax.broadcasted_iota(jnp.int32, (4 * BLK, 1), 0)
        gsink = jnp.zeros((1, 128), F32)
        dq_parts, dk_parts, dv_parts = [], [], []
        for g in range(NKV):
            qs = _stack_heads(q, g)
            kb = _band(kp, kc, g)
            vbd = _band(vp, vc, g)
            p, ps = _softmax_band(qs, kb, mask, _sink_rows(sink_ref, g))
            dos = _stack_heads(d_o, g)
            dp = lax.dot_general(dos, vbd, (((1,), (1,)), ((), ())), preferred_element_type=F32)
            dr = jnp.sum(p * dp, axis=-1, keepdims=True)
            ds = (p * (dp - dr) * (HD ** -0.5)).astype(BF)
            sink_term = ps * dr
            for h in range(4):
                part = jnp.sum(jnp.where((row >= BLK * h) & (row < BLK * (h + 1)), sink_term, 0.0),
                               axis=0, keepdims=True)
                gsink = gsink - jnp.where(lane == 4 * g + h, part, 0.0)
            dv_parts.append(lax.dot_general(p.astype(BF), dos, (((0,), (0,)), ((), ())), preferred_element_type=F32))
            dq_parts.append(jnp.dot(ds, kb, preferred_element_type=F32))
            dk_parts.append(lax.dot_general(ds, qs, (((0,), (0,)), ((), ())), preferred_element_type=F32))
        gsink_ref[...] += gsink
        dkb = jnp.concatenate(dk_parts, axis=1)
        dvb = jnp.concatenate(dv_parts, axis=1)
        r_prev = pl.multiple_of(jnp.maximum(i - 1, 0) * BLK, BLK)
        r_cur = pl.multiple_of(i * BLK, BLK)
        dk_ref[pl.ds(r_prev, BLK), :] += dkb[0:BLK]
        dv_ref[pl.ds(r_prev, BLK), :] += dvb[0:BLK]
        dk_ref[pl.ds(r_cur, BLK), :] += dkb[BLK:2 * BLK]
        dv_ref[pl.ds(r_cur, BLK), :] += dvb[BLK:2 * BLK]
        dq = jnp.concatenate([dq_parts[g][BLK * h:BLK * (h + 1), :] for g in range(NKV) for h in range(4)], axis=1)
        dq_raw, g_qw = _norm_rope_bwd(dq, qraw_ref[...], qw_ref[...], jnp.tile(cos_ref[...], (1, 4)),
                                      jnp.tile(sin_ref[...], (1, 4)), bq_ref)
        dqraw_ref[...] = dq_raw.astype(BF)
        gqw_ref[...] += g_qw

    prev = lambda i: (jnp.maximum(i - 1, 0), 0)
    b512 = pl.BlockSpec((BLK, AW), _row)
    b128 = pl.BlockSpec((BLK, 128), _row)
    return pl.pallas_call(
        body, name="attn_bwd", grid=(NB,),
        out_shape=[jax.ShapeDtypeStruct((S, AW), BF), jax.ShapeDtypeStruct((S, AW), BF),
                   jax.ShapeDtypeStruct((S, KVW), F32), jax.ShapeDtypeStruct((S, KVW), F32),
                   jax.ShapeDtypeStruct((1, AW), F32), jax.ShapeDtypeStruct((1, 128), F32)],
        in_specs=[_SMEM, b512, pl.BlockSpec((BLK, KVW), prev), b128, pl.BlockSpec((BLK, KVW), prev), b128,
                  b512, b512, b512, b512, pl.BlockSpec((1, AW), _const), b128, b128, pl.BlockSpec((AW, AW), _const)],
        out_specs=[b512, b512, _VMEM, _VMEM, _VMEM, _VMEM],
        compiler_params=_params(True),
    )(sinks, qr, kr, kr, vb, vb, d_ya, ga, o, qraw, qw_t, cos_t, sin_t, bq)


HALO = 32


RC = 64
LC = 128


def _windows(ext_ref, r0, l0, base):
    col = ext_ref[r0:r0 + RC + HALO, l0:l0 + LC]
    for s in range(8):
        rolled = col if s == 0 else pltpu.roll(col, RC + HALO - s, 0)
        for t in range(CK):
            if (base + t) % 8 == s:
                a8 = base + t - s
                yield t, rolled[a8:a8 + RC]


def _taps(ext_ref, r0, l0, base, weight_row):
    acc = None
    for t, win in _windows(ext_ref, r0, l0, base):
        term = win * weight_row(t)[:, l0:l0 + LC]
        acc = term if acc is None else acc + term
    return acc


def _conv_fwd_call(z, gb, cwf, conv_b, ln_w, ln_b):
    def body(zc_in_ref, zp_ref, gb_ref, cw_ref, cb_ref, lw_ref, lb_ref, zc_ref, yb_ref, zext):
        i = pl.program_id(0)
        zext[0:HALO, :] = jnp.where(i > 0, zp_ref[TS - HALO:TS, :], 0.0)
        zext[HALO:HALO + TS, :] = zc_in_ref[...]
        for r0 in range(0, TS, RC):
            for l0 in range(0, CW, LC):
                acc = _taps(zext, r0, l0, HALO - (CK - 1), lambda k: cw_ref[k:k + 1, :])
                zc_ref[r0:r0 + RC, l0:l0 + LC] = acc + cb_ref[:, l0:l0 + LC]
        zc = zc_ref[...]
        mu = jnp.mean(zc, axis=-1, keepdims=True)
        dz = zc - mu
        rstd = lax.rsqrt(jnp.mean(dz * dz, axis=-1, keepdims=True) + EPS)
        zn = dz * rstd * lw_ref[...] + lb_ref[...]
        gbv = gb_ref[...]
        yb_ref[...] = (zn * _sigmoid(zn)) * (gbv * _sigmoid(gbv))

    t512 = pl.BlockSpec((TS, CW), _row)
    prev = pl.BlockSpec((TS, CW), lambda i: (jnp.maximum(i - 1, 0), 0))
    c512 = pl.BlockSpec((1, CW), _const)
    return pl.pallas_call(
        body, name="conv_fwd", grid=(NT,),
        out_shape=[jax.ShapeDtypeStruct((S, CW), F32), jax.ShapeDtypeStruct((S, CW), F32)],
        in_specs=[t512, prev, t512, pl.BlockSpec((CKP, CW), _const), c512, c512, c512],
        out_specs=[t512, t512],
        scratch_shapes=[pltpu.VMEM((TS + HALO, CW), F32)],
        compiler_params=_params(True),
    )(z, z, gb, cwf, conv_b, ln_w, ln_b)


def _conv_bwd_ln_call(d_yb, zc, gb, ln_w, ln_b):
    def body(dyb_ref, zc_ref, gb_ref, lw_ref, lb_ref, dzc_ref, dgb_ref, glw_ref, glb_ref, gcb_ref):
        i = pl.program_id(0)

        @pl.when(i == 0)
        def _():
            glw_ref[...] = jnp.zeros((1, CW), F32)
            glb_ref[...] = jnp.zeros((1, CW), F32)
            gcb_ref[...] = jnp.zeros((1, CW), F32)

        zc = zc_ref[...]
        mu = jnp.mean(zc, axis=-1, keepdims=True)
        dz = zc - mu
        rstd = lax.rsqrt(jnp.mean(dz * dz, axis=-1, keepdims=True) + EPS)
        zh = dz * rstd
        lw = lw_ref[...]
        zn = zh * lw + lb_ref[...]
        d_yb = dyb_ref[...]
        act_n, dact_n = _silu_and_grad(zn)
        act_g, dact_g = _silu_and_grad(gb_ref[...])
        dgb_ref[...] = (d_yb * act_n * dact_g).astype(BF)
        d_zn = d_yb * act_g * dact_n
        glw_ref[...] += jnp.sum(d_zn * zh, axis=0, keepdims=True)
        glb_ref[...] += jnp.sum(d_zn, axis=0, keepdims=True)
        dzh = d_zn * lw
        d_zc = rstd * (dzh - jnp.mean(dzh, axis=-1, keepdims=True) - zh * jnp.mean(dzh * zh, axis=-1, keepdims=True))
        dzc_ref[...] = d_zc
        gcb_ref[...] += jnp.sum(d_zc, axis=0, keepdims=True)

    t512 = pl.BlockSpec((TS, CW), _row)
    c512 = pl.BlockSpec((1, CW), _const)
    vec = jax.ShapeDtypeStruct((1, CW), F32)
    return pl.pallas_call(
        body, name="conv_bwd_ln", grid=(NT,),
        out_shape=[jax.ShapeDtypeStruct((S, CW), F32), jax.ShapeDtypeStruct((S, CW), BF), vec, vec, vec],
        in_specs=[t512, t512, t512, c512, c512],
        out_specs=[t512, t512, _VMEM, _VMEM, _VMEM],
        compiler_params=_params(True),
    )(d_yb, zc, gb, ln_w, ln_b)


def _conv_bwd_taps_call(d_zc, z, a, g, cwf):
    def body(dc_ref, dn_ref, zc_ref, zp_ref, a_ref, g_ref, cw_ref, da_ref, dg_ref, gcw_ref, dext, zext, gacc):
        i = pl.program_id(0)

        @pl.when(i == 0)
        def _():
            gacc[...] = jnp.zeros((CKP * 8, CW), F32)

        dext[0:TS, :] = dc_ref[...]
        dext[TS:TS + HALO, :] = jnp.where(i < NT - 1, dn_ref[0:HALO, :], 0.0)
        zext[0:HALO, :] = jnp.where(i > 0, zp_ref[TS - HALO:TS, :], 0.0)
        zext[HALO:HALO + TS, :] = zc_ref[...]
        for r0 in range(0, TS, RC):
            for l0 in range(0, CW, LC):
                d_z = _taps(dext, r0, l0, 0, lambda j: cw_ref[CK - 1 - j:CK - j, :])
                sg = _sigmoid(g_ref[r0:r0 + RC, l0:l0 + LC])
                da_ref[r0:r0 + RC, l0:l0 + LC] = (d_z * sg).astype(BF)
                dg_ref[r0:r0 + RC, l0:l0 + LC] = (d_z * a_ref[r0:r0 + RC, l0:l0 + LC] * sg * (1.0 - sg)).astype(BF)
                d_sub = dc_ref[r0:r0 + RC, l0:l0 + LC]
                for k, win in _windows(zext, r0, l0, HALO - (CK - 1)):
                    prod = d_sub * win
                    part = prod[0:8]
                    for q in range(1, RC // 8):
                        part = part + prod[8 * q:8 * q + 8]
                    gacc[8 * k:8 * k + 8, l0:l0 + LC] += part

        @pl.when(i == NT - 1)
        def _():
            for k in range(CK):
                gcw_ref[k:k + 1, :] = jnp.sum(gacc[8 * k:8 * k + 8, :], axis=0, keepdims=True)
            gcw_ref[CK:CKP, :] = jnp.zeros((CKP - CK, CW), F32)

    t512 = pl.BlockSpec((TS, CW), _row)
    prev = pl.BlockSpec((TS, CW), lambda i: (jnp.maximum(i - 1, 0), 0))
    nxt = pl.BlockSpec((TS, CW), lambda i: (jnp.minimum(i + 1, NT - 1), 0))
    return pl.pallas_call(
        body, name="conv_bwd_taps", grid=(NT,),
        out_shape=[jax.ShapeDtypeStruct((S, CW), BF), jax.ShapeDtypeStruct((S, CW), BF),
                   jax.ShapeDtypeStruct((CKP, CW), F32)],
        in_specs=[t512, nxt, t512, prev, t512, t512, pl.BlockSpec((CKP, CW), _const)],
        out_specs=[t512, t512, _VMEM],
        scratch_shapes=[pltpu.VMEM((TS + HALO, CW), F32), pltpu.VMEM((TS + HALO, CW), F32),
                        pltpu.VMEM((CKP * 8, CW), F32)],
        compiler_params=_params(True),
    )(d_zc, d_zc, z, z, a, g, cwf)


def _out_loss_call(o, ga, yb, x2, tgt, mod, w_out_full):
    def body(o_ref, ga_ref, yb_ref, x_ref, t_ref, mod_ref, w_ref,
             dout_ref, dya_ref, dyb_ref, gw_ref, loss_ref, dgate_ref):
        i = pl.program_id(0)

        @pl.when(i == 0)
        def _():
            gw_ref[...] = jnp.zeros((D, D), F32)
            loss_ref[...] = jnp.zeros((1, 128), F32)
            dgate_ref[...] = jnp.zeros((1, D), F32)

        gav = ga_ref[...]
        ya = o_ref[...] * (gav * _sigmoid(gav))
        ycat = jnp.concatenate([ya, yb_ref[...]], axis=1).astype(BF)
        w = w_ref[...]
        y = jnp.dot(ycat, w, preferred_element_type=F32)
        gate = mod_ref[:, 2 * D:3 * D]
        diff = x_ref[...] + gate * y - t_ref[...]
        sq = jnp.sum(jnp.sum(diff * diff, axis=1, keepdims=True), axis=0, keepdims=True)
        loss_ref[...] += jnp.broadcast_to(sq, (1, 128))
        d_out = diff * (1.0 / D)
        dout_ref[...] = d_out
        dgate_ref[...] += jnp.sum(d_out * y, axis=0, keepdims=True)
        dy = (d_out * gate).astype(BF)
        d_ycat = lax.dot_general(dy, w, (((1,), (1,)), ((), ())), preferred_element_type=F32)
        dya_ref[...] = d_ycat[:, 0:AW]
        dyb_ref[...] = d_ycat[:, AW:D]
        gw_ref[...] += lax.dot_general(ycat, dy, (((0,), (0,)), ((), ())), preferred_element_type=F32)

    t512 = pl.BlockSpec((TS, 512), _row)
    t1024 = pl.BlockSpec((TS, D), _row)
    return pl.pallas_call(
        body, name="out_loss", grid=(NT,),
        out_shape=[jax.ShapeDtypeStruct((S, D), F32), jax.ShapeDtypeStruct((S, AW), F32),
                   jax.ShapeDtypeStruct((S, CW), F32), jax.ShapeDtypeStruct((D, D), F32),
                   jax.ShapeDtypeStruct((1, 128), F32), jax.ShapeDtypeStruct((1, D), F32)],
        in_specs=[t512, t512, t512, t1024, t1024, pl.BlockSpec((1, 3 * D), _const),
                  pl.BlockSpec((D, D), _const, pipeline_mode=pl.Buffered(1))],
        out_specs=[t1024, t512, t512, _VMEM, _VMEM, _VMEM],
        compiler_params=_params(True),
    )(o, ga, yb, x2, tgt, mod, w_out_full)


SM_ROWS = 8
PIECES = ((0, 512), (512, 640), (640, 768), (768, 1280), (1280, 1792), (1792, 2304), (2304, 2816))


def _bwd_in_call(dqraw, dk, dv, dga, da, dg, dgb, kraw, kw_t, cos_t, sin_t, bk, h, wt_full, x2, d_out, mod, norm_w,
                 gw_out, gcw, glw, glb, gcb, gqw, gsink, dgate, loss_p, cact_all):
    def body(dq_ref, dk_ref, dv_ref, dga_ref, da_ref, dg_ref, dgb_ref, kraw_ref, kw_ref, cos_ref, sin_ref, bk_ref,
             h_ref, wt_ref, x_ref, dout_ref, mod_ref, nw_ref, gwout_ref, gcw_ref, glw_ref, glb_ref, gcb_ref, gqw_ref,
             gsink_ref, dgate_ref, loss_ref, cact_ref,
             gx_ref, o_gwin, o_gwout, o_gwada, o_gbada, o_gnw, o_gqw, o_gkw, o_gsink, o_gcw, o_gcb, o_glw, o_glb,
             o_loss,
             acc, win_send, win_sib, win_ici, wout_send, wout_sib, wout_ici, sm_buf, cw_buf, dmod_all, vec_acc, gkw_acc,
             wi_ds, wi_dr, wi_is, wi_ir, wo_ds, wo_dr, wo_is, wo_ir, sm_s, sm_r, cw_s, cw_r):
        i = pl.program_id(0)
        pos = _mesh_pos()
        x, y, cc = pos
        me = 4 * x + 2 * y + cc

        def chip(j):
            return (1 - x if j & 1 else x, 1 - y if j & 2 else y)

        def rows_of(buf, px, py, pc, rows, align):
            return buf.at[pl.ds(pl.multiple_of((4 * px + 2 * py + pc) * rows, align), rows), :]

        bufs = {"in": (win_send, win_sib, win_ici, IN_SHARD, wi_ds, wi_dr, wi_is, wi_ir),
                "out": (wout_send, wout_sib, wout_ici, OUT_SHARD, wo_ds, wo_dr, wo_is, wo_ir)}

        def d2d_copy(j, which):
            send, sib, _, rows, ds_, dr_, _, _ = bufs[which]
            px, py = chip(j)
            return pltpu.make_async_remote_copy(src_ref=rows_of(send, px, py, 1 - cc, rows, 16), dst_ref=sib.at[j],
                                                send_sem=ds_.at[j], recv_sem=dr_.at[j], device_id=(x, y, 1 - cc),
                                                device_id_type=MESH)

        def ici_copy(j, which):
            send, _, ici, rows, _, _, is_, ir_ = bufs[which]
            px, py = chip(j)
            return pltpu.make_async_remote_copy(src_ref=rows_of(send, px, py, cc, rows, 16), dst_ref=ici.at[j - 1],
                                                send_sem=is_.at[j - 1], recv_sem=ir_.at[j - 1], device_id=(px, py, cc),
                                                device_id_type=MESH)

        def level2(which, partial_ref):
            send, sib, _, rows, _, _, _, _ = bufs[which]
            for j in range(1, 4):
                d2d_copy(j, which).wait_recv()
                px, py = chip(j)
                mine = rows_of(partial_ref, px, py, cc, rows, 8)[...]
                rows_of(send, px, py, cc, rows, 16)[...] = (mine + sib[j].astype(F32)).astype(BF)
                ici_copy(j, which).start()

        def finish(which, partial_ref):
            _, sib, ici, rows, _, _, _, _ = bufs[which]
            d2d_copy(0, which).wait_recv()
            total = rows_of(partial_ref, x, y, cc, rows, 8)[...] + sib[0].astype(F32)
            for j in range(1, 4):
                ici_copy(j, which).wait_recv()
                total = total + ici[j - 1].astype(F32)
            for j in range(4):
                d2d_copy(j, which).wait_send()
            for j in range(1, 4):
                ici_copy(j, which).wait_send()
            return total

        def dproj_pieces():
            dk_raw, g_kw = _norm_rope_bwd(dk_ref[...], kraw_ref[...], kw_ref[...], cos_ref[...], sin_ref[...], bk_ref)
            return [dq_ref[...], dk_raw.astype(BF), dv_ref[...].astype(BF), dga_ref[...], da_ref[...], dg_ref[...],
                    dgb_ref[...]], g_kw

        @pl.when(i == 0)
        def _():
            acc[...] = jnp.zeros((INW, D), F32)
            vec_acc[...] = jnp.zeros((8, D), F32)
            gkw_acc[...] = jnp.zeros((1, KVW), F32)
            wout_send[...] = gwout_ref[...].astype(BF)
            for j in range(4):
                d2d_copy(j, "out").start()

        @pl.when(i == 2)
        def _():
            level2("out", gwout_ref)

        @pl.when(i < NT)
        def _():
            pieces, g_kw = dproj_pieces()
            gkw_acc[...] += g_kw
            hv = h_ref[...]
            for (lo, hi), piece in zip(PIECES, pieces):
                acc[lo:hi, :] += lax.dot_general(piece, hv, (((0,), (0,)), ((), ())), preferred_element_type=F32)

        @pl.when(i == NT - 1)
        def _():
            for lo, hi in PIECES:
                win_send[lo:hi, :] = acc[lo:hi, :].astype(BF)
            for j in range(4):
                d2d_copy(j, "in").start()

        @pl.when(i == NT + 2)
        def _():
            level2("in", acc)

        @pl.when(i >= NT)
        def _():
            pieces, _ = dproj_pieces()
            dproj = jnp.concatenate(pieces, axis=1)
            d_h = jnp.dot(dproj, wt_ref[...], preferred_element_type=F32)
            xv = x_ref[...]
            scale = mod_ref[:, D:2 * D]
            nw = nw_ref[...]
            r = lax.rsqrt(jnp.mean(xv * xv, axis=-1, keepdims=True) + EPS)
            xn = xv * r
            vec_acc[0:1, :] += jnp.sum(d_h, axis=0, keepdims=True)
            vec_acc[1:2, :] += jnp.sum(d_h * (xn * nw), axis=0, keepdims=True)
            d_u = d_h * (1.0 + scale)
            vec_acc[2:3, :] += jnp.sum(d_u * xn, axis=0, keepdims=True)
            d_xn = d_u * nw
            gx_ref[...] = dout_ref[...] + r * (d_xn - xn * jnp.mean(d_xn * xn, axis=-1, keepdims=True))

        @pl.when(i == 2 * NT - 1)
        def _():
            ch_sm = (_slab(sm_buf), sm_s, sm_r)
            ch_cw = (_slab(cw_buf), cw_s, cw_r)
            z128 = jnp.zeros((1, 128), F32)
            row4 = jnp.concatenate([glw_ref[...], glb_ref[...]], axis=1)
            row5 = jnp.concatenate([gcb_ref[...], gqw_ref[...]], axis=1)
            row6 = jnp.concatenate([gkw_acc[...], gsink_ref[...], loss_ref[...]] + [z128] * 5, axis=1)
            sm_buf[me] = jnp.concatenate([vec_acc[0:2, :], dgate_ref[...], vec_acc[2:3, :], row4, row5, row6,
                                          jnp.zeros((1, D), F32)], axis=0)
            f_sm = _ag_start(ch_sm, pos)
            cw_buf[me] = gcw_ref[...]
            f_cw = _ag_start(ch_cw, pos)
            _ag_finish(ch_sm, pos, f_sm)
            _ag_finish(ch_cw, pos, f_cw)
            tot = sm_buf[0]
            cw_tot = cw_buf[0]
            for d in range(1, NDEV):
                tot = tot + sm_buf[d]
                cw_tot = cw_tot + cw_buf[d]
            o_gbada[...] = jnp.concatenate([tot[0:1, :], tot[1:2, :], tot[2:3, :]], axis=1)
            o_gnw[...] = tot[3:4, :]
            o_glw[...] = tot[4:5, 0:CW]
            o_glb[...] = tot[4:5, CW:D]
            o_gcb[...] = tot[5:6, 0:CW]
            gq = tot[5:6, CW:CW + HD]
            for hh in range(1, NQ):
                gq = gq + tot[5:6, CW + HD * hh:CW + HD * (hh + 1)]
            o_gqw[...] = gq
            o_gkw[...] = tot[6:7, 0:HD] + tot[6:7, HD:2 * HD]
            o_gsink[...] = tot[6:7, 128:128 + NQ]
            o_loss[...] = tot[6:7, 256:384] * (0.5 / D)
            mine = jnp.zeros((CK, CONV_SHARD), F32)
            for d in range(NDEV):
                mine = mine + jnp.where(me == d, cw_tot[0:CK, CONV_SHARD * d:CONV_SHARD * (d + 1)], 0.0)
            o_gcw[...] = mine
            for d in range(NDEV):
                dmod_all[d:d + 1, :] = jnp.concatenate([sm_buf[d, 0:1, :], sm_buf[d, 1:2, :], sm_buf[d, 2:3, :]],
                                                       axis=1)
            col0 = pl.multiple_of(me * ADA_SHARD, 128)
            o_gwada[...] = lax.dot_general(cact_ref[...], dmod_all[:, pl.ds(col0, ADA_SHARD)], (((0,), (0,)), ((), ())),
                                           preferred_element_type=F32, precision=lax.Precision.HIGHEST)

            o_gwout[...] = finish("out", gwout_ref)
            o_gwin[...] = finish("in", acc)

    half = lambda i: (i % NT, 0)
    late = lambda i: (jnp.maximum(i - NT, 0), 0)
    t512 = pl.BlockSpec((TS, 512), half)
    t128 = pl.BlockSpec((TS, 128), half)
    l1024 = pl.BlockSpec((TS, D), late)
    sem7 = pltpu.SemaphoreType.DMA((7,))
    sem4 = pltpu.SemaphoreType.DMA((4,))
    sem3 = pltpu.SemaphoreType.DMA((3,))
    sds = jax.ShapeDtypeStruct
    return pl.pallas_call(
        body, name="bwd_in", grid=(2 * NT,),
        out_shape=[sds((S, D), F32), sds((IN_SHARD, D), F32), sds((OUT_SHARD, D), F32), sds((D, ADA_SHARD), F32),
                   sds((1, 3 * D), F32), sds((1, D), F32), sds((1, HD), F32), sds((1, HD), F32), sds((1, NQ), F32),
                   sds((CK, CONV_SHARD), F32), sds((1, CW), F32), sds((1, CW), F32), sds((1, CW), F32),
                   sds((1, 128), F32)],
        in_specs=[t512, t128, t128, t512, t512, t512, t512, t128, pl.BlockSpec((1, KVW), _const), t128, t128,
                  pl.BlockSpec((KVW, KVW), _const), pl.BlockSpec((TS, D), half),
                  pl.BlockSpec((INW, D), _const, pipeline_mode=pl.Buffered(1)), l1024, l1024,
                  pl.BlockSpec((1, 3 * D), _const), pl.BlockSpec((1, D), _const)] + [_VMEM] * 10,
        out_specs=[l1024] + [_VMEM] * 13,
        scratch_shapes=[pltpu.VMEM((INW, D), F32), pltpu.VMEM((INW, D), BF), pltpu.VMEM((4, IN_SHARD, D), BF),
                        pltpu.VMEM((3, IN_SHARD, D), BF), pltpu.VMEM((D, D), BF), pltpu.VMEM((4, OUT_SHARD, D), BF),
                        pltpu.VMEM((3, OUT_SHARD, D), BF),
                        pltpu.VMEM((NDEV, SM_ROWS, D), F32), pltpu.VMEM((NDEV, CKP, CW), F32),
                        pltpu.VMEM((NDEV, 3 * D), F32), pltpu.VMEM((8, D), F32), pltpu.VMEM((1, KVW), F32)]
        + [sem4, sem4, sem3, sem3] * 2 + [sem7] * 4,
        compiler_params=pltpu.CompilerParams(dimension_semantics=("arbitrary",), vmem_limit_bytes=BIG_VMEM_LIMIT),
    )(dqraw, dk, dv, dga, da, dg, dgb, kraw, kw_t, cos_t, sin_t, bk, h, wt_full, x2, d_out, mod, norm_w,
      gw_out, gcw, glw, glb, gcb, gqw, gsink, dgate, loss_p, cact_all)


def _adam_call(ws, gs, ms, vs):
    n = len(ws)
    bc1 = 1.0 - ADAM_B1 ** ADAM_STEP
    bc2 = 1.0 - ADAM_B2 ** ADAM_STEP

    def body(*refs):
        ins, outs = refs[:4 * n], refs[4 * n:]
        for j in range(n):
            w, g, m, v = (ins[j][...], ins[n + j][...], ins[2 * n + j][...], ins[3 * n + j][...])
            m_new = ADAM_B1 * m + (1.0 - ADAM_B1) * g
            v_new = ADAM_B2 * v + (1.0 - ADAM_B2) * (g * g)
            m_hat = m_new / bc1
            v_hat = v_new / bc2
            outs[j][...] = -ADAM_LR * (m_hat / (jnp.sqrt(v_hat) + ADAM_EPS) + ADAM_WD * w)
            outs[n + j][...] = m_new
            outs[2 * n + j][...] = v_new

    shapes = [jax.ShapeDtypeStruct(w.shape, F32) for w in ws]
    return pl.pallas_call(
        body, name="adam",
        out_shape=shapes * 3, in_specs=[_VMEM] * (4 * n), out_specs=[_VMEM] * (3 * n),
        compiler_params=_params(),
    )(*ws, *gs, *ms, *vs)


def _rope_tables():
    inv = ROPE_THETA ** (-jnp.arange(0, HD, 2, dtype=F32) / HD)
    ang = jnp.arange(S, dtype=F32)[:, None] * inv[None, :]
    cos, sin = jnp.cos(ang), jnp.sin(ang)
    cos64 = jnp.concatenate([cos, cos], axis=-1)
    sin64 = jnp.concatenate([-sin, sin], axis=-1)
    return jnp.tile(cos64, (1, 2)), jnp.tile(sin64, (1, 2))


def _group_matrix(width):
    idx = jnp.arange(width) // HD
    return jnp.where(idx[:, None] == idx[None, :], 1.0 / HD, 0.0).astype(BF)


def kernel(x, c, w_ada, b_ada, norm_w, w_in, q_norm_w, k_norm_w, sinks, conv_w, conv_b, ln_w, ln_b, w_out, loss_target, m_w_ada, m_b_ada, m_norm_w, m_w_in, m_q_norm_w, m_k_norm_w, m_sinks, m_conv_w, m_conv_b, m_ln_w, m_ln_b, m_w_out, v_w_ada, v_b_ada, v_norm_w, v_w_in, v_q_norm_w, v_k_norm_w, v_sinks, v_conv_w, v_conv_b, v_ln_w, v_ln_b, v_w_out):
    x2 = x[0]
    tgt = loss_target[0]
    cos_t, sin_t = _rope_tables()
    bq = _group_matrix(AW)
    bk = _group_matrix(KVW)
    qw_t = jnp.tile(q_norm_w, (1, NQ))
    kw_t = jnp.tile(k_norm_w, (1, NKV))

    tr = lambda t: jnp.swapaxes(t[0], 0, 1)
    (wt_full, w_out_full, cwf, mod, cact_all, h, qraw, kraw, ga, a, g, gb, qr, kr, vb, z) = _gather_fwd_call(
        tr(w_in), w_out[0], conv_w[0], c, w_ada[0], b_ada, x2, norm_w, qw_t, kw_t, cos_t, sin_t, bq, bk)
    o = _attn_fwd_call(sinks, qr, kr, vb)
    zc, yb = _conv_fwd_call(z, gb, cwf, conv_b, ln_w, ln_b)
    d_out, d_ya, d_yb, gw_out, loss_p, dgate = _out_loss_call(o, ga, yb, x2, tgt, mod, w_out_full)

    d_zc, dgb, glw, glb, gcb = _conv_bwd_ln_call(d_yb, zc, gb, ln_w, ln_b)
    da, dg, gcw = _conv_bwd_taps_call(d_zc, z, a, g, cwf)
    dqraw, dga, dk, dv, gqw, gsink = _attn_bwd_call(sinks, qr, kr, vb, d_ya, ga, o, qraw, qw_t, cos_t, sin_t, bq)
    (grad_x, g_w_in_t, g_w_out, g_w_ada, g_b_ada, g_norm_w, g_qw, g_kw, g_sinks, g_conv_w, g_conv_b, g_ln_w, g_ln_b,
     loss_v) = _bwd_in_call(dqraw, dk, dv, dga, da, dg, dgb, kraw, kw_t, cos_t, sin_t, bk, h, wt_full, x2, d_out, mod,
                            norm_w, gw_out, gcw, glw, glb, gcb, gqw, gsink, dgate, loss_p, cact_all)

    ws = [w_ada[0], b_ada, norm_w, tr(w_in), q_norm_w, k_norm_w, sinks, conv_w[0], conv_b, ln_w, ln_b, w_out[0]]
    gs = [g_w_ada, g_b_ada, g_norm_w, g_w_in_t, g_qw, g_kw, g_sinks, g_conv_w, g_conv_b, g_ln_w, g_ln_b, g_w_out]
    ms = [m_w_ada[0], m_b_ada, m_norm_w, tr(m_w_in), m_q_norm_w, m_k_norm_w, m_sinks, m_conv_w[0], m_conv_b, m_ln_w,
          m_ln_b, m_w_out[0]]
    vs = [v_w_ada[0], v_b_ada, v_norm_w, tr(v_w_in), v_q_norm_w, v_k_norm_w, v_sinks, v_conv_w[0], v_conv_b, v_ln_w,
          v_ln_b, v_w_out[0]]
    upd = _adam_call(ws, gs, ms, vs)
    n = len(ws)
    shaped = [w_ada, b_ada, norm_w, w_in, q_norm_w, k_norm_w, sinks, conv_w, conv_b, ln_w, ln_b, w_out]
    W_IN_POS = 3

    def like(vals):
        vals = [jnp.swapaxes(v, 0, 1) if j == W_IN_POS else v for j, v in enumerate(vals)]
        return [v.reshape(s.shape) for v, s in zip(vals, shaped)]

    return (loss_v[0, 0], grad_x[None], *like(gs), *like(upd[0:n]), *like(upd[n:2 * n]), *like(upd[2 * n:3 * n]))
```

```python
import functools

import jax
import jax.numpy as jnp
from jax import lax
from jax.experimental import pallas as pl
from jax.experimental.pallas import tpu as pltpu

S = 2048
D = 1024
NDEV = 8
HD = 64
NQ = 8
NKV = 2
AW = 512
KVW = 128
CW = 512
INW = 2816
IN_SHARD = INW // NDEV
ADA_SHARD = 3 * D // NDEV
OUT_SHARD = D // NDEV
CONV_SHARD = CW // NDEV
CK = 31
CKP = 32
BLK = 128
TS = 256
NT = S // TS
NB = S // BLK
EPS = 1e-6
ROPE_THETA = 10000.0
NEG = -1e30
BF = jnp.bfloat16
F32 = jnp.float32

ADAM_LR = 0.001
ADAM_B1 = 0.9
ADAM_B2 = 0.999
ADAM_EPS = 1e-08
ADAM_WD = 0.01
ADAM_STEP = 10

VMEM_LIMIT = 56 * 1024 * 1024
BIG_VMEM_LIMIT = 62 * 1024 * 1024
MESH = pl.DeviceIdType.MESH

_VMEM = pl.BlockSpec(memory_space=pltpu.VMEM)
_SMEM = pl.BlockSpec(memory_space=pltpu.SMEM)
_ANY = pl.BlockSpec(memory_space=pl.ANY)


def _params(grid=False):
    if grid:
        return pltpu.CompilerParams(dimension_semantics=("arbitrary",), vmem_limit_bytes=VMEM_LIMIT)
    return pltpu.CompilerParams(vmem_limit_bytes=VMEM_LIMIT)


def _row(i):
    return (i, 0)


def _const(i):
    return (0, 0)


def _sigmoid(t):
    return 1.0 / (1.0 + jnp.exp(-t))


def _silu_and_grad(t):
    sg = _sigmoid(t)
    return t * sg, sg * (1.0 + t * (1.0 - sg))


def _group_mean(t, b_ref):
    hi = t.astype(BF)
    lo = (t - hi.astype(F32)).astype(BF)
    b = b_ref[...]
    return jnp.dot(hi, b, preferred_element_type=F32) + jnp.dot(lo, b, preferred_element_type=F32)


def _partner(t):
    w = t.shape[-1]
    lane = lax.broadcasted_iota(jnp.int32, t.shape, 1)
    first = (lane & 32) == 0
    return jnp.where(first, pltpu.roll(t, w - 32, 1), pltpu.roll(t, 32, 1))


def _norm_rope_fwd(t, w_t, cos, sin, b_ref):
    r = lax.rsqrt(_group_mean(t * t, b_ref) + EPS)
    tn = t * r * w_t
    return tn * cos + _partner(tn) * sin


def _norm_rope_bwd(d_out, t, w_t, cos, sin, b_ref):
    d_tn = d_out * cos + _partner(d_out * sin)
    r = lax.rsqrt(_group_mean(t * t, b_ref) + EPS)
    th = t * r
    g_w = jnp.sum(d_tn * th, axis=0, keepdims=True)
    d_th = d_tn * w_t
    d_t = r * (d_th - th * _group_mean(d_th * th, b_ref))
    return d_t, g_w


def _mesh_pos():
    return lax.axis_index("x"), lax.axis_index("y"), lax.axis_index("c")


def _ag_copy(chan, k, block, to):
    blk, send_sems, recv_sems = chan
    ref = blk(*block)
    return pltpu.make_async_remote_copy(src_ref=ref, dst_ref=ref, send_sem=send_sems.at[k],
                                        recv_sem=recv_sems.at[k], device_id=to, device_id_type=MESH)


def _ag_start(chan, pos):
    x, y, c = pos
    me = (x, y, c)
    chips = [(1 - x, y), (x, 1 - y), (1 - x, 1 - y)]
    first = [_ag_copy(chan, 0, me, (x, y, 1 - c))]
    first += [_ag_copy(chan, 1 + j, me, (*chip, c)) for j, chip in enumerate(chips)]
    for cp in first:
        cp.start()
    return first


def _ag_finish(chan, pos, first):
    x, y, c = pos
    me = (x, y, c)
    sibling = (x, y, 1 - c)
    chips = [(1 - x, y), (x, 1 - y), (1 - x, 1 - y)]
    passed = [_ag_copy(chan, 4 + j, (*chip, c), sibling) for j, chip in enumerate(chips)]
    for j, chip in enumerate(chips):
        _ag_copy(chan, 1 + j, (*chip, c), me).wait_recv()
        passed[j].start()
    _ag_copy(chan, 0, sibling, me).wait_recv()
    for j, chip in enumerate(chips):
        _ag_copy(chan, 4 + j, (*chip, 1 - c), me).wait_recv()
    for cp in first + passed:
        cp.wait_send()


def _slab(buf):
    return lambda px, py, pc: buf.at[4 * px + 2 * py + pc]


def _row_block(buf, rows, align):
    return lambda px, py, pc: buf.at[pl.ds(pl.multiple_of((4 * px + 2 * py + pc) * rows, align), rows), :]


HALF = INW // 2


def _gather_fwd_call(w_in_t, w_out_s, conv_w_s, c, w_ada_s, b_ada, x2, norm_w, qw_t, kw_t, cos_t, sin_t, bq, bk):
    def body(win_ref, wout_ref, cw_ref, c_ref, wada_ref, bada_ref, x_ref, nw_ref, qw_ref, kw_ref, cos_ref, sin_ref,
             bq_ref, bk_ref,
             wtf_hbm, woutf_hbm, cwf_ref, mod_ref, cact_ref,
             h_ref, qraw_ref, kraw_ref, ga_ref, a_ref, g_ref, gb_ref, qr_ref, kr_ref, vb_ref, z_ref,
             cw_buf, ca_buf, mp_buf, h_s, raw0, pt, wtf_ref, woutf_ref,
             s0, r0, s1, r1, s2, r2, s3, r3, s4, r4, out_sems):
        s = pl.program_id(0)
        pos = _mesh_pos()
        x, y, cc = pos
        me3 = (x, y, cc)
        me = 4 * x + 2 * y + cc
        sibling = (x, y, 1 - cc)
        chips = [(1 - x, y), (x, 1 - y), (1 - x, 1 - y)]
        ch_win = (_row_block(wtf_ref, IN_SHARD, 16), s0, r0)
        ch_wout = (_row_block(woutf_ref, OUT_SHARD, 16), s1, r1)
        ch_cw = (_slab(cw_buf), s2, r2)
        ch_ca = (_slab(ca_buf), s3, r3)
        ch_mp = (_slab(mp_buf), s4, r4)

        def first(chan, j):
            return _ag_copy(chan, j, me3, sibling if j == 0 else (*chips[j - 1], cc))

        def passed(chan, j):
            return _ag_copy(chan, 4 + j, (*chips[j], cc), sibling)

        def landed(chan, j):
            return _ag_copy(chan, 1 + j, (*chips[j], cc), me3)

        def relayed(chan, j):
            return _ag_copy(chan, 4 + j, (*chips[j], 1 - cc), me3)

        def from_sibling(chan):
            return _ag_copy(chan, 0, sibling, me3)

        @pl.when(s == 0)
        def _():
            cv = c_ref[...]
            ca_buf[me] = jnp.broadcast_to(cv * _sigmoid(cv), (8, D))
            f_ca = _ag_start(ch_ca, pos)
            wtf_ref[pl.ds(pl.multiple_of(me * IN_SHARD, 16), IN_SHARD), :] = win_ref[...].astype(BF)
            for j in range(3):
                first(ch_win, j).start()
            cw_buf[me] = cw_ref[...]
            f_cw = _ag_start(ch_cw, pos)

            _ag_finish(ch_ca, pos, f_ca)
            cact_all = jnp.concatenate([ca_buf[d, 0:1, :] for d in range(NDEV)], axis=0)
            cact_ref[...] = cact_all
            col0 = pl.multiple_of(me * ADA_SHARD, 128)
            mp_buf[me] = jnp.dot(cact_all, wada_ref[...], preferred_element_type=F32,
                                 precision=lax.Precision.HIGHEST) + bada_ref[:, pl.ds(col0, ADA_SHARD)]
            f_mp = _ag_start(ch_mp, pos)
            _ag_finish(ch_cw, pos, f_cw)
            _ag_finish(ch_mp, pos, f_mp)
            for d in range(NDEV):
                cwf_ref[0:CK, CONV_SHARD * d:CONV_SHARD * (d + 1)] = cw_buf[d]
            cwf_ref[CK:CKP, :] = jnp.zeros((CKP - CK, CW), F32)
            mod_ref[...] = jnp.concatenate([mp_buf[d, pl.ds(me, 1), :] for d in range(NDEV)], axis=1)

            for j in (1, 0):
                landed(ch_win, j).wait_recv()
                passed(ch_win, j).start()
            from_sibling(ch_win).wait_recv()
            relayed(ch_win, 1).wait_recv()
            first(ch_win, 1).wait_send()
            first(ch_win, 2).wait_send()
            first(ch_win, 3).start()
            wout = wout_ref[...].astype(BF)
            woutf_ref[pl.ds(pl.multiple_of(me * OUT_SHARD, 16), OUT_SHARD), :] = wout
            for j in range(4):
                first(ch_wout, j).start()

        row0 = pl.multiple_of((s % NT) * TS, TS)

        @pl.when(s < NT)
        def _():
            xv = x_ref[...]
            shift = mod_ref[:, 0:D]
            scale = mod_ref[:, D:2 * D]
            r = lax.rsqrt(jnp.mean(xv * xv, axis=-1, keepdims=True) + EPS)
            hb = ((xv * r * nw_ref[...]) * (1.0 + scale) + shift).astype(BF)
            h_s[pl.ds(row0, TS), :] = hb
            w_half = wtf_ref[pl.ds(pl.multiple_of(x * HALF, 16), HALF), :]
            raw0[pl.ds(row0, TS), :] = lax.dot_general(hb, w_half, (((1,), (1,)), ((), ())),
                                                       preferred_element_type=F32)

        @pl.when(s == NT)
        def _():
            relayed(ch_win, 0).wait_recv()
            landed(ch_win, 2).wait_recv()
            passed(ch_win, 2).start()
            relayed(ch_win, 2).wait_recv()
            pltpu.make_async_copy(wtf_ref, wtf_hbm, out_sems.at[0]).start()

        @pl.when(s >= NT)
        def _():
            hb = h_s[pl.ds(row0, TS), :]
            h_ref[...] = hb
            w_half = wtf_ref[pl.ds(pl.multiple_of((1 - x) * HALF, 16), HALF), :]
            raw1 = lax.dot_general(hb, w_half, (((1,), (1,)), ((), ())), preferred_element_type=F32)
            pt[:, pl.ds(pl.multiple_of(x * HALF, 128), HALF)] = raw0[pl.ds(row0, TS), :]
            pt[:, pl.ds(pl.multiple_of((1 - x) * HALF, 128), HALF)] = raw1
            cos = cos_ref[...]
            sin = sin_ref[...]
            q = pt[:, 0:512]
            qraw_ref[...] = q
            qr_ref[...] = _norm_rope_fwd(q, qw_ref[...], jnp.tile(cos, (1, 4)), jnp.tile(sin, (1, 4)),
                                         bq_ref).astype(BF)
            k = pt[:, 512:640]
            kraw_ref[...] = k
            kr_ref[...] = _norm_rope_fwd(k, kw_ref[...], cos, sin, bk_ref).astype(BF)
            vb_ref[...] = pt[:, 640:768].astype(BF)
            ga_ref[...] = pt[:, 768:1280]
            a = pt[:, 1280:1792]
            g = pt[:, 1792:2304]
            a_ref[...] = a
            g_ref[...] = g
            z_ref[...] = a * _sigmoid(g)
            gb_ref[...] = pt[:, 2304:2816]

        @pl.when(s == 2 * NT - 1)
        def _():
            for j in range(3):
                landed(ch_wout, j).wait_recv()
                passed(ch_wout, j).start()
            from_sibling(ch_wout).wait_recv()
            for j in range(3):
                relayed(ch_wout, j).wait_recv()
            out_copy = pltpu.make_async_copy(woutf_ref, woutf_hbm, out_sems.at[1])
            out_copy.start()
            pltpu.make_async_copy(wtf_ref, wtf_hbm, out_sems.at[0]).wait()
            out_copy.wait()
            first(ch_win, 0).wait_send()
            first(ch_win, 3).wait_send()
            for j in range(3):
                passed(ch_win, j).wait_send()
                passed(ch_wout, j).wait_send()
            for j in range(4):
                first(ch_wout, j).wait_send()

    early = lambda i: (jnp.minimum(i, NT - 1), 0)
    late = lambda i: (jnp.maximum(i - NT, 0), 0)
    t512 = pl.BlockSpec((TS, 512), late)
    t128 = pl.BlockSpec((TS, 128), late)
    sem = pltpu.SemaphoreType.DMA((7,))
    sds = jax.ShapeDtypeStruct
    return pl.pallas_call(
        body, name="gather_fwd", grid=(2 * NT,),
        out_shape=[sds((INW, D), BF), sds((D, D), BF), sds((CKP, CW), F32), sds((1, 3 * D), F32), sds((NDEV, D), F32),
                   sds((S, D), BF), sds((S, AW), F32), sds((S, KVW), F32), sds((S, AW), F32), sds((S, CW), F32),
                   sds((S, CW), F32), sds((S, CW), F32), sds((S, AW), BF), sds((S, KVW), BF), sds((S, KVW), BF),
                   sds((S, CW), F32)],
        in_specs=[_VMEM] * 6 + [pl.BlockSpec((TS, D), early), pl.BlockSpec((1, D), _const),
                                pl.BlockSpec((1, AW), _const), pl.BlockSpec((1, KVW), _const), t128, t128,
                                pl.BlockSpec((AW, AW), _const), pl.BlockSpec((KVW, KVW), _const)],
        out_specs=[_ANY, _ANY] + [_VMEM] * 3 + [pl.BlockSpec((TS, D), late), t512, t128, t512, t512, t512, t512, t512,
                                                t128, t128, t512],
        scratch_shapes=[pltpu.VMEM((NDEV, CK, CONV_SHARD), F32), pltpu.VMEM((NDEV, 8, D), F32),
                        pltpu.VMEM((NDEV, 8, ADA_SHARD), F32), pltpu.VMEM((S, D), BF), pltpu.VMEM((S, HALF), F32),
                        pltpu.VMEM((TS, INW), F32), pltpu.VMEM((INW, D), BF), pltpu.VMEM((D, D), BF)]
        + [sem] * 10 + [pltpu.SemaphoreType.DMA((2,))],
        compiler_params=_params(True),
    )(w_in_t, w_out_s, conv_w_s, c, w_ada_s, b_ada, x2, norm_w, qw_t, kw_t, cos_t, sin_t, bq, bk)


QB = 4
NQB = NB // QB


def _band_mask(has_prev):
    kj = lax.broadcasted_iota(jnp.int32, (2 * BLK, 4 * BLK), 0)
    qi = lax.broadcasted_iota(jnp.int32, (2 * BLK, 4 * BLK), 1) & (BLK - 1)
    dist = qi + BLK - kj
    local = (dist >= 0) & (dist < BLK)
    return local if has_prev is True else local & ((kj >= BLK) | has_prev)


def _key_blocks(sb, prev_ref, cur_ref):
    prev = prev_ref[...] if sb == 0 else cur_ref[BLK * (sb - 1):BLK * sb, :]
    return prev, cur_ref[BLK * sb:BLK * (sb + 1), :]


def _sink_lanes(sink_ref, g):
    lane = lax.broadcasted_iota(jnp.int32, (1, 4 * BLK), 1)
    return jnp.where(lane < BLK, sink_ref[0, 4 * g],
                     jnp.where(lane < 2 * BLK, sink_ref[0, 4 * g + 1],
                               jnp.where(lane < 3 * BLK, sink_ref[0, 4 * g + 2], sink_ref[0, 4 * g + 3])))


def _unstack_t(t):
    return [t[:, BLK * h:BLK * (h + 1)].T for h in range(4)]


def _stack_heads(t, g):
    return jnp.concatenate([t[:, HD * (4 * g + h):HD * (4 * g + h + 1)] for h in range(4)], axis=0)


def _band(prev, cur, g):
    return jnp.concatenate([prev[:, HD * g:HD * (g + 1)], cur[:, HD * g:HD * (g + 1)]], axis=0)


def _softmax_band(qs, kb, mask, sink):
    s = lax.dot_general(kb, qs, (((1,), (1,)), ((), ())), preferred_element_type=F32) * (HD ** -0.5)
    s = jnp.where(mask, s, NEG)
    m = jnp.maximum(jnp.max(s, axis=0, keepdims=True), sink)
    e = jnp.exp(s - m)
    es = jnp.exp(sink - m)
    inv = 1.0 / (jnp.sum(e, axis=0, keepdims=True) + es)
    return e * inv, es * inv


def _attn_fwd_call(sinks, qr, kr, vb):
    def body(sink_ref, q_ref, kp_ref, kc_ref, vp_ref, vc_ref, o_ref):
        i = pl.program_id(0)
        for sb in range(QB):
            rows = slice(BLK * sb, BLK * (sb + 1))
            mask = _band_mask(i > 0 if sb == 0 else True)
            q = q_ref[rows, :]
            kp, kc = _key_blocks(sb, kp_ref, kc_ref)
            vp, vc = _key_blocks(sb, vp_ref, vc_ref)
            for g in range(NKV):
                p, _ = _softmax_band(_stack_heads(q, g), _band(kp, kc, g), mask, _sink_lanes(sink_ref, g))
                o_t = lax.dot_general(_band(vp, vc, g), p.astype(BF), (((0,), (0,)), ((), ())),
                                      preferred_element_type=F32)
                for h, o_h in enumerate(_unstack_t(o_t)):
                    o_ref[rows, HD * (4 * g + h):HD * (4 * g + h + 1)] = o_h

    prev = lambda i: (jnp.maximum(QB * i - 1, 0), 0)
    return pl.pallas_call(
        body, name="attn_fwd", grid=(NQB,),
        out_shape=jax.ShapeDtypeStruct((S, AW), F32),
        in_specs=[_SMEM, pl.BlockSpec((QB * BLK, AW), _row), pl.BlockSpec((BLK, KVW), prev),
                  pl.BlockSpec((QB * BLK, KVW), _row), pl.BlockSpec((BLK, KVW), prev),
                  pl.BlockSpec((QB * BLK, KVW), _row)],
        out_specs=pl.BlockSpec((QB * BLK, AW), _row),
        compiler_params=_params(True),
    )(sinks, qr, kr, kr, vb, vb)


def _attn_bwd_call(sinks, qr, kr, vb, d_ya, ga, o, qraw, qw_t, cos_t, sin_t, bq):
    def body(sink_ref, q_ref, kp_ref, kc_ref, vp_ref, vc_ref, dya_ref, ga_ref, o_ref, qraw_ref, qw_ref,
             cos_ref, sin_ref, bq_ref,
             dqraw_ref, dga_ref, dk_ref, dv_ref, gqw_ref, gsink_ref):
        i = pl.program_id(0)

        @pl.when(i == 0)
        def _():
            dk_ref[...] = jnp.zeros((S, KVW), F32)
            dv_ref[...] = jnp.zeros((S, KVW), F32)
            gqw_ref[...] = jnp.zeros((1, AW), F32)
            gsink_ref[...] = jnp.zeros((1, 128), F32)

        d_ya = dya_ref[...]
        act, dact = _silu_and_grad(ga_ref[...])
        dga_ref[...] = (d_ya * o_ref[...] * dact).astype(BF)
        d_o_all = (d_ya * act).astype(BF)
        lane = lax.broadcasted_iota(jnp.int32, (1, 128), 1)
        gsink = jnp.zeros((1, 128), F32)
        dq_rows = []
        for sb in range(QB):
            rows = slice(BLK * sb, BLK * (sb + 1))
            mask = _band_mask(i > 0 if sb == 0 else True)
            q = q_ref[rows, :]
            d_o = d_o_all[rows, :]
            kp, kc = _key_blocks(sb, kp_ref, kc_ref)
            vp, vc = _key_blocks(sb, vp_ref, vc_ref)
            dq_parts, dk_parts, dv_parts = [], [], []
            for g in range(NKV):
                qs = _stack_heads(q, g)
                kb = _band(kp, kc, g)
                vbd = _band(vp, vc, g)
                p, ps = _softmax_band(qs, kb, mask, _sink_lanes(sink_ref, g))
                dos = _stack_heads(d_o, g)
                dp = lax.dot_general(vbd, dos, (((1,), (1,)), ((), ())), preferred_element_type=F32)
                dr = jnp.sum(p * dp, axis=0, keepdims=True)
                ds = (p * (dp - dr) * (HD ** -0.5)).astype(BF)
                sink_term = ps * dr
                for h in range(4):
                    part = jnp.sum(sink_term[:, BLK * h:BLK * (h + 1)], axis=1, keepdims=True)
                    gsink = gsink - jnp.where(lane == 4 * g + h, part, 0.0)
                dv_parts.append(jnp.dot(p.astype(BF), dos, preferred_element_type=F32))
                dk_parts.append(jnp.dot(ds, qs, preferred_element_type=F32))
                dq_t = lax.dot_general(kb, ds, (((0,), (0,)), ((), ())), preferred_element_type=F32)
                dq_parts.extend(_unstack_t(dq_t))
            dkb = jnp.concatenate(dk_parts, axis=1)
            dvb = jnp.concatenate(dv_parts, axis=1)
            blk = QB * i + sb
            r_prev = pl.multiple_of(jnp.maximum(blk - 1, 0) * BLK, BLK)
            r_cur = pl.multiple_of(blk * BLK, BLK)
            dk_ref[pl.ds(r_prev, BLK), :] += dkb[0:BLK]
            dv_ref[pl.ds(r_prev, BLK), :] += dvb[0:BLK]
            dk_ref[pl.ds(r_cur, BLK), :] += dkb[BLK:2 * BLK]
            dv_ref[pl.ds(r_cur, BLK), :] += dvb[BLK:2 * BLK]
            dq_rows.append(jnp.concatenate(dq_parts, axis=1))
        gsink_ref[...] += gsink
        dq = jnp.concatenate(dq_rows, axis=0)
        dq_raw, g_qw = _norm_rope_bwd(dq, qraw_ref[...], qw_ref[...], jnp.tile(cos_ref[...], (1, 4)),
                                      jnp.tile(sin_ref[...], (1, 4)), bq_ref)
        dqraw_ref[...] = dq_raw.astype(BF)
        gqw_ref[...] += g_qw

    prev = lambda i: (jnp.maximum(QB * i - 1, 0), 0)
    b512 = pl.BlockSpec((QB * BLK, AW), _row)
    b128 = pl.BlockSpec((QB * BLK, 128), _row)
    return pl.pallas_call(
        body, name="attn_bwd", grid=(NQB,),
        out_shape=[jax.ShapeDtypeStruct((S, AW), BF), jax.ShapeDtypeStruct((S, AW), BF),
                   jax.ShapeDtypeStruct((S, KVW), F32), jax.ShapeDtypeStruct((S, KVW), F32),
                   jax.ShapeDtypeStruct((1, AW), F32), jax.ShapeDtypeStruct((1, 128), F32)],
        in_specs=[_SMEM, b512, pl.BlockSpec((BLK, KVW), prev), b128, pl.BlockSpec((BLK, KVW), prev), b128,
                  b512, b512, b512, b512, pl.BlockSpec((1, AW), _const), b128, b128, pl.BlockSpec((AW, AW), _const)],
        out_specs=[b512, b512, _VMEM, _VMEM, _VMEM, _VMEM],
        compiler_params=_params(True),
    )(sinks, qr, kr, kr, vb, vb, d_ya, ga, o, qraw, qw_t, cos_t, sin_t, bq)


HALO = 32


RC = 64
LC = 128


def _windows(ext_ref, r0, l0, base):
    col = ext_ref[r0:r0 + RC + HALO, l0:l0 + LC]
    for s in range(8):
        rolled = col if s == 0 else pltpu.roll(col, RC + HALO - s, 0)
        for t in range(CK):
            if (base + t) % 8 == s:
                a8 = base + t - s
                yield t, rolled[a8:a8 + RC]


def _taps(ext_ref, r0, l0, base, weight_row):
    acc = None
    for t, win in _windows(ext_ref, r0, l0, base):
        term = win * weight_row(t)[:, l0:l0 + LC]
        acc = term if acc is None else acc + term
    return acc


def _conv_fwd_call(z, gb, cwf, conv_b, ln_w, ln_b):
    def body(zc_in_ref, zp_ref, gb_ref, cw_ref, cb_ref, lw_ref, lb_ref, zc_ref, yb_ref, zext):
        i = pl.program_id(0)
        zext[0:HALO, :] = jnp.where(i > 0, zp_ref[TS - HALO:TS, :], 0.0)
        zext[HALO:HALO + TS, :] = zc_in_ref[...]
        for r0 in range(0, TS, RC):
            for l0 in range(0, CW, LC):
                acc = _taps(zext, r0, l0, HALO - (CK - 1), lambda k: cw_ref[k:k + 1, :])
                zc_ref[r0:r0 + RC, l0:l0 + LC] = acc + cb_ref[:, l0:l0 + LC]
        zc = zc_ref[...]
        mu = jnp.mean(zc, axis=-1, keepdims=True)
        dz = zc - mu
        rstd = lax.rsqrt(jnp.mean(dz * dz, axis=-1, keepdims=True) + EPS)
        zn = dz * rstd * lw_ref[...] + lb_ref[...]
        gbv = gb_ref[...]
        yb_ref[...] = (zn * _sigmoid(zn)) * (gbv * _sigmoid(gbv))

    t512 = pl.BlockSpec((TS, CW), _row)
    prev = pl.BlockSpec((TS, CW), lambda i: (jnp.maximum(i - 1, 0), 0))
    c512 = pl.BlockSpec((1, CW), _const)
    return pl.pallas_call(
        body, name="conv_fwd", grid=(NT,),
        out_shape=[jax.ShapeDtypeStruct((S, CW), F32), jax.ShapeDtypeStruct((S, CW), F32)],
        in_specs=[t512, prev, t512, pl.BlockSpec((CKP, CW), _const), c512, c512, c512],
        out_specs=[t512, t512],
        scratch_shapes=[pltpu.VMEM((TS + HALO, CW), F32)],
        compiler_params=_params(True),
    )(z, z, gb, cwf, conv_b, ln_w, ln_b)


def _conv_bwd_ln_call(d_yb, zc, gb, ln_w, ln_b):
    def body(dyb_ref, zc_ref, gb_ref, lw_ref, lb_ref, dzc_ref, dgb_ref, glw_ref, glb_ref, gcb_ref):
        i = pl.program_id(0)

        @pl.when(i == 0)
        def _():
            glw_ref[...] = jnp.zeros((1, CW), F32)
            glb_ref[...] = jnp.zeros((1, CW), F32)
            gcb_ref[...] = jnp.zeros((1, CW), F32)

        zc = zc_ref[...]
        mu = jnp.mean(zc, axis=-1, keepdims=True)
        dz = zc - mu
        rstd = lax.rsqrt(jnp.mean(dz * dz, axis=-1, keepdims=True) + EPS)
        zh = dz * rstd
        lw = lw_ref[...]
        zn = zh * lw + lb_ref[...]
        d_yb = dyb_ref[...]
        act_n, dact_n = _silu_and_grad(zn)
        act_g, dact_g = _silu_and_grad(gb_ref[...])
        dgb_ref[...] = (d_yb * act_n * dact_g).astype(BF)
        d_zn = d_yb * act_g * dact_n
        glw_ref[...] += jnp.sum(d_zn * zh, axis=0, keepdims=True)
        glb_ref[...] += jnp.sum(d_zn, axis=0, keepdims=True)
        dzh = d_zn * lw
        d_zc = rstd * (dzh - jnp.mean(dzh, axis=-1, keepdims=True) - zh * jnp.mean(dzh * zh, axis=-1, keepdims=True))
        dzc_ref[...] = d_zc
        gcb_ref[...] += jnp.sum(d_zc, axis=0, keepdims=True)

    t512 = pl.BlockSpec((TS, CW), _row)
    c512 = pl.BlockSpec((1, CW), _const)
    vec = jax.ShapeDtypeStruct((1, CW), F32)
    return pl.pallas_call(
        body, name="conv_bwd_ln", grid=(NT,),
        out_shape=[jax.ShapeDtypeStruct((S, CW), F32), jax.ShapeDtypeStruct((S, CW), BF), vec, vec, vec],
        in_specs=[t512, t512, t512, c512, c512],
        out_specs=[t512, t512, _VMEM, _VMEM, _VMEM],
        compiler_params=_params(True),
    )(d_yb, zc, gb, ln_w, ln_b)


def _conv_bwd_taps_call(d_zc, z, a, g, cwf):
    def body(dc_ref, dn_ref, zc_ref, zp_ref, a_ref, g_ref, cw_ref, da_ref, dg_ref, gcw_ref, dext, zext, gacc):
        i = pl.program_id(0)

        @pl.when(i == 0)
        def _():
            gacc[...] = jnp.zeros((CKP * 8, CW), F32)

        dext[0:TS, :] = dc_ref[...]
        dext[TS:TS + HALO, :] = jnp.where(i < NT - 1, dn_ref[0:HALO, :], 0.0)
        zext[0:HALO, :] = jnp.where(i > 0, zp_ref[TS - HALO:TS, :], 0.0)
        zext[HALO:HALO + TS, :] = zc_ref[...]
        for r0 in range(0, TS, RC):
            for l0 in range(0, CW, LC):
                d_z = _taps(dext, r0, l0, 0, lambda j: cw_ref[CK - 1 - j:CK - j, :])
                sg = _sigmoid(g_ref[r0:r0 + RC, l0:l0 + LC])
                da_ref[r0:r0 + RC, l0:l0 + LC] = (d_z * sg).astype(BF)
                dg_ref[r0:r0 + RC, l0:l0 + LC] = (d_z * a_ref[r0:r0 + RC, l0:l0 + LC] * sg * (1.0 - sg)).astype(BF)
                d_sub = dc_ref[r0:r0 + RC, l0:l0 + LC]
                for k, win in _windows(zext, r0, l0, HALO - (CK - 1)):
                    prod = d_sub * win
                    part = prod[0:8]
                    for q in range(1, RC // 8):
                        part = part + prod[8 * q:8 * q + 8]
                    gacc[8 * k:8 * k + 8, l0:l0 + LC] += part

        @pl.when(i == NT - 1)
        def _():
            for k in range(CK):
                gcw_ref[k:k + 1, :] = jnp.sum(gacc[8 * k:8 * k + 8, :], axis=0, keepdims=True)
            gcw_ref[CK:CKP, :] = jnp.zeros((CKP - CK, CW), F32)

    t512 = pl.BlockSpec((TS, CW), _row)
    prev = pl.BlockSpec((TS, CW), lambda i: (jnp.maximum(i - 1, 0), 0))
    nxt = pl.BlockSpec((TS, CW), lambda i: (jnp.minimum(i + 1, NT - 1), 0))
    return pl.pallas_call(
        body, name="conv_bwd_taps", grid=(NT,),
        out_shape=[jax.ShapeDtypeStruct((S, CW), BF), jax.ShapeDtypeStruct((S, CW), BF),
                   jax.ShapeDtypeStruct((CKP, CW), F32)],
        in_specs=[t512, nxt, t512, prev, t512, t512, pl.BlockSpec((CKP, CW), _const)],
        out_specs=[t512, t512, _VMEM],
        scratch_shapes=[pltpu.VMEM((TS + HALO, CW), F32), pltpu.VMEM((TS + HALO, CW), F32),
                        pltpu.VMEM((CKP * 8, CW), F32)],
        compiler_params=_params(True),
    )(d_zc, d_zc, z, z, a, g, cwf)


def _out_loss_call(o, ga, yb, x2, tgt, mod, w_out_full):
    def body(o_ref, ga_ref, yb_ref, x_ref, t_ref, mod_ref, w_ref,
             dout_ref, dya_ref, dyb_ref, gw_ref, loss_ref, dgate_ref):
        i = pl.program_id(0)

        @pl.when(i == 0)
        def _():
            gw_ref[...] = jnp.zeros((D, D), F32)
            loss_ref[...] = jnp.zeros((1, 128), F32)
            dgate_ref[...] = jnp.zeros((1, D), F32)

        gav = ga_ref[...]
        ya = o_ref[...] * (gav * _sigmoid(gav))
        ycat = jnp.concatenate([ya, yb_ref[...]], axis=1).astype(BF)
        w = w_ref[...]
        y = jnp.dot(ycat, w, preferred_element_type=F32)
        gate = mod_ref[:, 2 * D:3 * D]
        diff = x_ref[...] + gate * y - t_ref[...]
        sq = jnp.sum(jnp.sum(diff * diff, axis=1, keepdims=True), axis=0, keepdims=True)
        loss_ref[...] += jnp.broadcast_to(sq, (1, 128))
        d_out = diff * (1.0 / D)
        dout_ref[...] = d_out
        dgate_ref[...] += jnp.sum(d_out * y, axis=0, keepdims=True)
        dy = (d_out * gate).astype(BF)
        d_ycat = lax.dot_general(dy, w, (((1,), (1,)), ((), ())), preferred_element_type=F32)
        dya_ref[...] = d_ycat[:, 0:AW]
        dyb_ref[...] = d_ycat[:, AW:D]
        gw_ref[...] += lax.dot_general(ycat, dy, (((0,), (0,)), ((), ())), preferred_element_type=F32)

    t512 = pl.BlockSpec((TS, 512), _row)
    t1024 = pl.BlockSpec((TS, D), _row)
    return pl.pallas_call(
        body, name="out_loss", grid=(NT,),
        out_shape=[jax.ShapeDtypeStruct((S, D), F32), jax.ShapeDtypeStruct((S, AW), F32),
                   jax.ShapeDtypeStruct((S, CW), F32), jax.ShapeDtypeStruct((D, D), F32),
                   jax.ShapeDtypeStruct((1, 128), F32), jax.ShapeDtypeStruct((1, D), F32)],
        in_specs=[t512, t512, t512, t1024, t1024, pl.BlockSpec((1, 3 * D), _const),
                  pl.BlockSpec((D, D), _const, pipeline_mode=pl.Buffered(1))],
        out_specs=[t1024, t512, t512, _VMEM, _VMEM, _VMEM],
        compiler_params=_params(True),
    )(o, ga, yb, x2, tgt, mod, w_out_full)


SM_ROWS = 8
PIECES = ((0, 512), (512, 640), (640, 768), (768, 1280), (1280, 1792), (1792, 2304), (2304, 2816))


def _bwd_in_call(dqraw, dk, dv, dga, da, dg, dgb, kraw, kw_t, cos_t, sin_t, bk, h, wt_full, x2, d_out, mod, norm_w,
                 gw_out, gcw, glw, glb, gcb, gqw, gsink, dgate, loss_p, cact_all):
    def body(dq_ref, dk_ref, dv_ref, dga_ref, da_ref, dg_ref, dgb_ref, kraw_ref, kw_ref, cos_ref, sin_ref, bk_ref,
             h_ref, wt_ref, x_ref, dout_ref, mod_ref, nw_ref, gwout_ref, gcw_ref, glw_ref, glb_ref, gcb_ref, gqw_ref,
             gsink_ref, dgate_ref, loss_ref, cact_ref,
             gx_ref, o_gwin, o_gwout, o_gwada, o_gbada, o_gnw, o_gqw, o_gkw, o_gsink, o_gcw, o_gcb, o_glw, o_glb,
             o_loss,
             acc, win_send, win_sib, win_ici, wout_send, wout_sib, wout_ici, sm_buf, cw_buf, dmod_all, vec_acc, gkw_acc,
             wi_ds, wi_dr, wi_is, wi_ir, wo_ds, wo_dr, wo_is, wo_ir, sm_s, sm_r, cw_s, cw_r):
        i = pl.program_id(0)
        pos = _mesh_pos()
        x, y, cc = pos
        me = 4 * x + 2 * y + cc

        def chip(j):
            return (1 - x if j & 1 else x, 1 - y if j & 2 else y)

        def rows_of(buf, px, py, pc, rows, align):
            return buf.at[pl.ds(pl.multiple_of((4 * px + 2 * py + pc) * rows, align), rows), :]

        bufs = {"in": (win_send, win_sib, win_ici, IN_SHARD, wi_ds, wi_dr, wi_is, wi_ir),
                "out": (wout_send, wout_sib, wout_ici, OUT_SHARD, wo_ds, wo_dr, wo_is, wo_ir)}

        def d2d_copy(j, which):
            send, sib, _, rows, ds_, dr_, _, _ = bufs[which]
            px, py = chip(j)
            return pltpu.make_async_remote_copy(src_ref=rows_of(send, px, py, 1 - cc, rows, 16), dst_ref=sib.at[j],
                                                send_sem=ds_.at[j], recv_sem=dr_.at[j], device_id=(x, y, 1 - cc),
                                                device_id_type=MESH)

        def ici_copy(j, which):
            send, _, ici, rows, _, _, is_, ir_ = bufs[which]
            px, py = chip(j)
            return pltpu.make_async_remote_copy(src_ref=rows_of(send, px, py, cc, rows, 16), dst_ref=ici.at[j - 1],
                                                send_sem=is_.at[j - 1], recv_sem=ir_.at[j - 1], device_id=(px, py, cc),
                                                device_id_type=MESH)

        def level2(which, partial_ref):
            send, sib, _, rows, _, _, _, _ = bufs[which]
            for j in range(1, 4):
                d2d_copy(j, which).wait_recv()
                px, py = chip(j)
                mine = rows_of(partial_ref, px, py, cc, rows, 8)[...]
                rows_of(send, px, py, cc, rows, 16)[...] = (mine + sib[j].astype(F32)).astype(BF)
                ici_copy(j, which).start()

        def finish(which, partial_ref):
            _, sib, ici, rows, _, _, _, _ = bufs[which]
            d2d_copy(0, which).wait_recv()
            total = rows_of(partial_ref, x, y, cc, rows, 8)[...] + sib[0].astype(F32)
            for j in range(1, 4):
                ici_copy(j, which).wait_recv()
                total = total + ici[j - 1].astype(F32)
            for j in range(4):
                d2d_copy(j, which).wait_send()
            for j in range(1, 4):
                ici_copy(j, which).wait_send()
            return total

        def dproj_pieces():
            dk_raw, g_kw = _norm_rope_bwd(dk_ref[...], kraw_ref[...], kw_ref[...], cos_ref[...], sin_ref[...], bk_ref)
            return [dq_ref[...], dk_raw.astype(BF), dv_ref[...].astype(BF), dga_ref[...], da_ref[...], dg_ref[...],
                    dgb_ref[...]], g_kw

        @pl.when(i == 0)
        def _():
            acc[...] = jnp.zeros((INW, D), F32)
            vec_acc[...] = jnp.zeros((8, D), F32)
            gkw_acc[...] = jnp.zeros((1, KVW), F32)
            wout_send[...] = gwout_ref[...].astype(BF)
            for j in range(4):
                d2d_copy(j, "out").start()

        @pl.when(i == 2)
        def _():
            level2("out", gwout_ref)

        @pl.when(i < NT)
        def _():
            pieces, g_kw = dproj_pieces()
            gkw_acc[...] += g_kw
            hv = h_ref[...]
            for (lo, hi), piece in zip(PIECES, pieces):
                acc[lo:hi, :] += lax.dot_general(piece, hv, (((0,), (0,)), ((), ())), preferred_element_type=F32)

        @pl.when(i == NT - 1)
        def _():
            for lo, hi in PIECES:
                win_send[lo:hi, :] = acc[lo:hi, :].astype(BF)
            for j in range(4):
                d2d_copy(j, "in").start()

        @pl.when(i == NT + 2)
        def _():
            level2("in", acc)

        @pl.when(i >= NT)
        def _():
            pieces, _ = dproj_pieces()
            dproj = jnp.concatenate(pieces, axis=1)
            d_h = jnp.dot(dproj, wt_ref[...], preferred_element_type=F32)
            xv = x_ref[...]
            scale = mod_ref[:, D:2 * D]
            nw = nw_ref[...]
            r = lax.rsqrt(jnp.mean(xv * xv, axis=-1, keepdims=True) + EPS)
            xn = xv * r
            vec_acc[0:1, :] += jnp.sum(d_h, axis=0, keepdims=True)
            vec_acc[1:2, :] += jnp.sum(d_h * (xn * nw), axis=0, keepdims=True)
            d_u = d_h * (1.0 + scale)
            vec_acc[2:3, :] += jnp.sum(d_u * xn, axis=0, keepdims=True)
            d_xn = d_u * nw
            gx_ref[...] = dout_ref[...] + r * (d_xn - xn * jnp.mean(d_xn * xn, axis=-1, keepdims=True))

        @pl.when(i == 2 * NT - 1)
        def _():
            ch_sm = (_slab(sm_buf), sm_s, sm_r)
            ch_cw = (_slab(cw_buf), cw_s, cw_r)
            z128 = jnp.zeros((1, 128), F32)
            row4 = jnp.concatenate([glw_ref[...], glb_ref[...]], axis=1)
            row5 = jnp.concatenate([gcb_ref[...], gqw_ref[...]], axis=1)
            row6 = jnp.concatenate([gkw_acc[...], gsink_ref[...], loss_ref[...]] + [z128] * 5, axis=1)
            sm_buf[me] = jnp.concatenate([vec_acc[0:2, :], dgate_ref[...], vec_acc[2:3, :], row4, row5, row6,
                                          jnp.zeros((1, D), F32)], axis=0)
            f_sm = _ag_start(ch_sm, pos)
            cw_buf[me] = gcw_ref[...]
            f_cw = _ag_start(ch_cw, pos)
            _ag_finish(ch_sm, pos, f_sm)
            _ag_finish(ch_cw, pos, f_cw)
            tot = sm_buf[0]
            cw_tot = cw_buf[0]
            for d in range(1, NDEV):
                tot = tot + sm_buf[d]
                cw_tot = cw_tot + cw_buf[d]
            o_gbada[...] = jnp.concatenate([tot[0:1, :], tot[1:2, :], tot[2:3, :]], axis=1)
            o_gnw[...] = tot[3:4, :]
            o_glw[...] = tot[4:5, 0:CW]
            o_glb[...] = tot[4:5, CW:D]
            o_gcb[...] = tot[5:6, 0:CW]
            gq = tot[5:6, CW:CW + HD]
            for hh in range(1, NQ):
                gq = gq + tot[5:6, CW + HD * hh:CW + HD * (hh + 1)]
            o_gqw[...] = gq
            o_gkw[...] = tot[6:7, 0:HD] + tot[6:7, HD:2 * HD]
            o_gsink[...] = tot[6:7, 128:128 + NQ]
            o_loss[...] = tot[6:7, 256:384] * (0.5 / D)
            mine = jnp.zeros((CK, CONV_SHARD), F32)
            for d in range(NDEV):
                mine = mine + jnp.where(me == d, cw_tot[0:CK, CONV_SHARD * d:CONV_SHARD * (d + 1)], 0.0)
            o_gcw[...] = mine
            for d in range(NDEV):
                dmod_all[d:d + 1, :] = jnp.concatenate([sm_buf[d, 0:1, :], sm_buf[d, 1:2, :], sm_buf[d, 2:3, :]],
                                                       axis=1)
            col0 = pl.multiple_of(me * ADA_SHARD, 128)
            o_gwada[...] = lax.dot_general(cact_ref[...], dmod_all[:, pl.ds(col0, ADA_SHARD)], (((0,), (0,)), ((), ())),
                                           preferred_element_type=F32, precision=lax.Precision.HIGHEST)

            o_gwout[...] = finish("out", gwout_ref)
            o_gwin[...] = finish("in", acc)

    half = lambda i: (i % NT, 0)
    late = lambda i: (jnp.maximum(i - NT, 0), 0)
    t512 = pl.BlockSpec((TS, 512), half)
    t128 = pl.BlockSpec((TS, 128), half)
    l1024 = pl.BlockSpec((TS, D), late)
    sem7 = pltpu.SemaphoreType.DMA((7,))
    sem4 = pltpu.SemaphoreType.DMA((4,))
    sem3 = pltpu.SemaphoreType.DMA((3,))
    sds = jax.ShapeDtypeStruct
    return pl.pallas_call(
        body, name="bwd_in", grid=(2 * NT,),
        out_shape=[sds((S, D), F32), sds((IN_SHARD, D), F32), sds((OUT_SHARD, D), F32), sds((D, ADA_SHARD), F32),
                   sds((1, 3 * D), F32), sds((1, D), F32), sds((1, HD), F32), sds((1, HD), F32), sds((1, NQ), F32),
                   sds((CK, CONV_SHARD), F32), sds((1, CW), F32), sds((1, CW), F32), sds((1, CW), F32),
                   sds((1, 128), F32)],
        in_specs=[t512, t128, t128, t512, t512, t512, t512, t128, pl.BlockSpec((1, KVW), _const), t128, t128,
                  pl.BlockSpec((KVW, KVW), _const), pl.BlockSpec((TS, D), half),
                  pl.BlockSpec((INW, D), _const, pipeline_mode=pl.Buffered(1)), l1024, l1024,
                  pl.BlockSpec((1, 3 * D), _const), pl.BlockSpec((1, D), _const)] + [_VMEM] * 10,
        out_specs=[l1024] + [_VMEM] * 13,
        scratch_shapes=[pltpu.VMEM((INW, D), F32), pltpu.VMEM((INW, D), BF), pltpu.VMEM((4, IN_SHARD, D), BF),
                        pltpu.VMEM((3, IN_SHARD, D), BF), pltpu.VMEM((D, D), BF), pltpu.VMEM((4, OUT_SHARD, D), BF),
                        pltpu.VMEM((3, OUT_SHARD, D), BF),
                        pltpu.VMEM((NDEV, SM_ROWS, D), F32), pltpu.VMEM((NDEV, CKP, CW), F32),
                        pltpu.VMEM((NDEV, 3 * D), F32), pltpu.VMEM((8, D), F32), pltpu.VMEM((1, KVW), F32)]
        + [sem4, sem4, sem3, sem3] * 2 + [sem7] * 4,
        compiler_params=pltpu.CompilerParams(dimension_semantics=("arbitrary",), vmem_limit_bytes=BIG_VMEM_LIMIT),
    )(dqraw, dk, dv, dga, da, dg, dgb, kraw, kw_t, cos_t, sin_t, bk, h, wt_full, x2, d_out, mod, norm_w,
      gw_out, gcw, glw, glb, gcb, gqw, gsink, dgate, loss_p, cact_all)


def _adam_call(ws, gs, ms, vs):
    n = len(ws)
    bc1 = 1.0 - ADAM_B1 ** ADAM_STEP
    bc2 = 1.0 - ADAM_B2 ** ADAM_STEP

    def body(*refs):
        ins, outs = refs[:4 * n], refs[4 * n:]
        for j in range(n):
            w, g, m, v = (ins[j][...], ins[n + j][...], ins[2 * n + j][...], ins[3 * n + j][...])
            m_new = ADAM_B1 * m + (1.0 - ADAM_B1) * g
            v_new = ADAM_B2 * v + (1.0 - ADAM_B2) * (g * g)
            m_hat = m_new / bc1
            v_hat = v_new / bc2
            outs[j][...] = -ADAM_LR * (m_hat / (jnp.sqrt(v_hat) + ADAM_EPS) + ADAM_WD * w)
            outs[n + j][...] = m_new
            outs[2 * n + j][...] = v_new

    shapes = [jax.ShapeDtypeStruct(w.shape, F32) for w in ws]
    return pl.pallas_call(
        body, name="adam",
        out_shape=shapes * 3, in_specs=[_VMEM] * (4 * n), out_specs=[_VMEM] * (3 * n),
        compiler_params=_params(),
    )(*ws, *gs, *ms, *vs)


def _rope_tables():
    inv = ROPE_THETA ** (-jnp.arange(0, HD, 2, dtype=F32) / HD)
    ang = jnp.arange(S, dtype=F32)[:, None] * inv[None, :]
    cos, sin = jnp.cos(ang), jnp.sin(ang)
    cos64 = jnp.concatenate([cos, cos], axis=-1)
    sin64 = jnp.concatenate([-sin, sin], axis=-1)
    return jnp.tile(cos64, (1, 2)), jnp.tile(sin64, (1, 2))


def _group_matrix(width):
    idx = jnp.arange(width) // HD
    return jnp.where(idx[:, None] == idx[None, :], 1.0 / HD, 0.0).astype(BF)


def kernel(x, c, w_ada, b_ada, norm_w, w_in, q_norm_w, k_norm_w, sinks, conv_w, conv_b, ln_w, ln_b, w_out, loss_target, m_w_ada, m_b_ada, m_norm_w, m_w_in, m_q_norm_w, m_k_norm_w, m_sinks, m_conv_w, m_conv_b, m_ln_w, m_ln_b, m_w_out, v_w_ada, v_b_ada, v_norm_w, v_w_in, v_q_norm_w, v_k_norm_w, v_sinks, v_conv_w, v_conv_b, v_ln_w, v_ln_b, v_w_out):
    x2 = x[0]
    tgt = loss_target[0]
    cos_t, sin_t = _rope_tables()
    bq = _group_matrix(AW)
    bk = _group_matrix(KVW)
    qw_t = jnp.tile(q_norm_w, (1, NQ))
    kw_t = jnp.tile(k_norm_w, (1, NKV))

    tr = lambda t: jnp.swapaxes(t[0], 0, 1)
    (wt_full, w_out_full, cwf, mod, cact_all, h, qraw, kraw, ga, a, g, gb, qr, kr, vb, z) = _gather_fwd_call(
        tr(w_in), w_out[0], conv_w[0], c, w_ada[0], b_ada, x2, norm_w, qw_t, kw_t, cos_t, sin_t, bq, bk)
    o = _attn_fwd_call(sinks, qr, kr, vb)
    zc, yb = _conv_fwd_call(z, gb, cwf, conv_b, ln_w, ln_b)
    d_out, d_ya, d_yb, gw_out, loss_p, dgate = _out_loss_call(o, ga, yb, x2, tgt, mod, w_out_full)

    d_zc, dgb, glw, glb, gcb = _conv_bwd_ln_call(d_yb, zc, gb, ln_w, ln_b)
    da, dg, gcw = _conv_bwd_taps_call(d_zc, z, a, g, cwf)
    dqraw, dga, dk, dv, gqw, gsink = _attn_bwd_call(sinks, qr, kr, vb, d_ya, ga, o, qraw, qw_t, cos_t, sin_t, bq)
    (grad_x, g_w_in_t, g_w_out, g_w_ada, g_b_ada, g_norm_w, g_qw, g_kw, g_sinks, g_conv_w, g_conv_b, g_ln_w, g_ln_b,
     loss_v) = _bwd_in_call(dqraw, dk, dv, dga, da, dg, dgb, kraw, kw_t, cos_t, sin_t, bk, h, wt_full, x2, d_out, mod,
                            norm_w, gw_out, gcw, glw, glb, gcb, gqw, gsink, dgate, loss_p, cact_all)

    ws = [w_ada[0], b_ada, norm_w, tr(w_in), q_norm_w, k_norm_w, sinks, conv_w[0], conv_b, ln_w, ln_b, w_out[0]]
    gs = [g_w_ada, g_b_ada, g_norm_w, g_w_in_t, g_qw, g_kw, g_sinks, g_conv_w, g_conv_b, g_ln_w, g_ln_b, g_w_out]
    ms = [m_w_ada[0], m_b_ada, m_norm_w, tr(m_w_in), m_q_norm_w, m_k_norm_w, m_sinks, m_conv_w[0], m_conv_b, m_ln_w,
          m_ln_b, m_w_out[0]]
    vs = [v_w_ada[0], v_b_ada, v_norm_w, tr(v_w_in), v_q_norm_w, v_k_norm_w, v_sinks, v_conv_w[0], v_conv_b, v_ln_w,
          v_ln_b, v_w_out[0]]
    upd = _adam_call(ws, gs, ms, vs)
    n = len(ws)
    shaped = [w_ada, b_ada, norm_w, w_in, q_norm_w, k_norm_w, sinks, conv_w, conv_b, ln_w, ln_b, w_out]
    W_IN_POS = 3

    def like(vals):
        vals = [jnp.swapaxes(v, 0, 1) if j == W_IN_POS else v for j, v in enumerate(vals)]
        return [v.reshape(s.shape) for v, s in zip(vals, shaped)]

    return (loss_v[0, 0], grad_x[None], *like(gs), *like(upd[0:n]), *like(upd[n:2 * n]), *like(upd[2 * n:3 * n]))
```

```python
import functools

import jax
import jax.numpy as jnp
import numpy as np
from jax import lax
from jax.experimental import pallas as pl
from jax.experimental.pallas import tpu as pltpu

S = 2048
D = 1024
NDEV = 8
HD = 64
NQ = 8
NKV = 2
AW = 512
KVW = 128
CW = 512
INW = 2816
IN_SHARD = INW // NDEV
ADA_SHARD = 3 * D // NDEV
OUT_SHARD = D // NDEV
CONV_SHARD = CW // NDEV
CK = 31
CKP = 32
BLK = 128
TS = 256
NT = S // TS
NB = S // BLK
EPS = 1e-6
ROPE_THETA = 10000.0
NEG = -1e30
BF = jnp.bfloat16
F32 = jnp.float32

ADAM_LR = 0.001
ADAM_B1 = 0.9
ADAM_B2 = 0.999
ADAM_EPS = 1e-08
ADAM_WD = 0.01
ADAM_STEP = 10

VMEM_LIMIT = 56 * 1024 * 1024
BIG_VMEM_LIMIT = 62 * 1024 * 1024
MESH = pl.DeviceIdType.MESH

_VMEM = pl.BlockSpec(memory_space=pltpu.VMEM)
_SMEM = pl.BlockSpec(memory_space=pltpu.SMEM)
_ANY = pl.BlockSpec(memory_space=pl.ANY)


def _params(grid=False):
    if grid:
        return pltpu.CompilerParams(dimension_semantics=("arbitrary",), vmem_limit_bytes=VMEM_LIMIT)
    return pltpu.CompilerParams(vmem_limit_bytes=VMEM_LIMIT)


def _row(i):
    return (i, 0)


def _const(i):
    return (0, 0)


def _sigmoid(t):
    return 1.0 / (1.0 + jnp.exp(-t))


def _silu_and_grad(t):
    sg = _sigmoid(t)
    return t * sg, sg * (1.0 + t * (1.0 - sg))


def _group_mean(t, b_ref):
    hi = t.astype(BF)
    lo = (t - hi.astype(F32)).astype(BF)
    b = b_ref[...]
    return jnp.dot(hi, b, preferred_element_type=F32) + jnp.dot(lo, b, preferred_element_type=F32)


def _partner(t):
    w = t.shape[-1]
    lane = lax.broadcasted_iota(jnp.int32, t.shape, 1)
    first = (lane & 32) == 0
    return jnp.where(first, pltpu.roll(t, w - 32, 1), pltpu.roll(t, 32, 1))


def _norm_rope_fwd(t, w_t, cos, sin, b_ref):
    r = lax.rsqrt(_group_mean(t * t, b_ref) + EPS)
    tn = t * r * w_t
    return tn * cos + _partner(tn) * sin


def _norm_rope_bwd(d_out, t, w_t, cos, sin, b_ref):
    d_tn = d_out * cos + _partner(d_out * sin)
    r = lax.rsqrt(_group_mean(t * t, b_ref) + EPS)
    th = t * r
    g_w = jnp.sum(d_tn * th, axis=0, keepdims=True)
    d_th = d_tn * w_t
    d_t = r * (d_th - th * _group_mean(d_th * th, b_ref))
    return d_t, g_w


def _mesh_pos():
    return lax.axis_index("x"), lax.axis_index("y"), lax.axis_index("c")


def _ag_copy(chan, k, block, to):
    blk, send_sems, recv_sems = chan
    ref = blk(*block)
    return pltpu.make_async_remote_copy(src_ref=ref, dst_ref=ref, send_sem=send_sems.at[k],
                                        recv_sem=recv_sems.at[k], device_id=to, device_id_type=MESH)


def _ag_start(chan, pos):
    x, y, c = pos
    me = (x, y, c)
    chips = [(1 - x, y), (x, 1 - y), (1 - x, 1 - y)]
    first = [_ag_copy(chan, 0, me, (x, y, 1 - c))]
    first += [_ag_copy(chan, 1 + j, me, (*chip, c)) for j, chip in enumerate(chips)]
    for cp in first:
        cp.start()
    return first


def _ag_finish(chan, pos, first):
    x, y, c = pos
    me = (x, y, c)
    sibling = (x, y, 1 - c)
    chips = [(1 - x, y), (x, 1 - y), (1 - x, 1 - y)]
    passed = [_ag_copy(chan, 4 + j, (*chip, c), sibling) for j, chip in enumerate(chips)]
    for j, chip in enumerate(chips):
        _ag_copy(chan, 1 + j, (*chip, c), me).wait_recv()
        passed[j].start()
    _ag_copy(chan, 0, sibling, me).wait_recv()
    for j, chip in enumerate(chips):
        _ag_copy(chan, 4 + j, (*chip, 1 - c), me).wait_recv()
    for cp in first + passed:
        cp.wait_send()


def _slab(buf):
    return lambda px, py, pc: buf.at[4 * px + 2 * py + pc]


def _row_block(buf, rows, align):
    return lambda px, py, pc: buf.at[pl.ds(pl.multiple_of((4 * px + 2 * py + pc) * rows, align), rows), :]


HALF = INW // 2


def _gather_fwd_call(w_in_t, w_out_s, conv_w_s, c, w_ada_s, b_ada, x2, norm_w, qw_t, kw_t, cos_t, sin_t, bq, bk):
    def body(win_ref, wout_ref, cw_ref, c_ref, wada_ref, bada_ref, x_ref, nw_ref, qw_ref, kw_ref, cos_ref, sin_ref,
             bq_ref, bk_ref,
             wtf_hbm, woutf_hbm, cwf_ref, mod_ref, cact_ref,
             h_ref, qraw_ref, kraw_ref, ga_ref, a_ref, g_ref, gb_ref, qr_ref, kr_ref, vb_ref, z_ref,
             cw_buf, ca_buf, mp_buf, h_s, raw0, pt, wtf_ref, woutf_ref,
             s0, r0, s1, r1, s2, r2, s3, r3, s4, r4, out_sems):
        s = pl.program_id(0)
        pos = _mesh_pos()
        x, y, cc = pos
        me3 = (x, y, cc)
        me = 4 * x + 2 * y + cc
        sibling = (x, y, 1 - cc)
        chips = [(1 - x, y), (x, 1 - y), (1 - x, 1 - y)]
        ch_win = (_row_block(wtf_ref, IN_SHARD, 16), s0, r0)
        ch_wout = (_row_block(woutf_ref, OUT_SHARD, 16), s1, r1)
        ch_cw = (_slab(cw_buf), s2, r2)
        ch_ca = (_slab(ca_buf), s3, r3)
        ch_mp = (_slab(mp_buf), s4, r4)

        def first(chan, j):
            return _ag_copy(chan, j, me3, sibling if j == 0 else (*chips[j - 1], cc))

        def passed(chan, j):
            return _ag_copy(chan, 4 + j, (*chips[j], cc), sibling)

        def landed(chan, j):
            return _ag_copy(chan, 1 + j, (*chips[j], cc), me3)

        def relayed(chan, j):
            return _ag_copy(chan, 4 + j, (*chips[j], 1 - cc), me3)

        def from_sibling(chan):
            return _ag_copy(chan, 0, sibling, me3)

        @pl.when(s == 0)
        def _():
            cv = c_ref[...]
            ca_buf[me] = jnp.broadcast_to(cv * _sigmoid(cv), (8, D))
            f_ca = _ag_start(ch_ca, pos)
            wtf_ref[pl.ds(pl.multiple_of(me * IN_SHARD, 16), IN_SHARD), :] = win_ref[...].astype(BF)
            for j in range(3):
                first(ch_win, j).start()
            cw_buf[me] = cw_ref[:, 0, :]
            f_cw = _ag_start(ch_cw, pos)

            _ag_finish(ch_ca, pos, f_ca)
            cact_all = jnp.concatenate([ca_buf[d, 0:1, :] for d in range(NDEV)], axis=0)
            cact_ref[...] = cact_all
            col0 = pl.multiple_of(me * ADA_SHARD, 128)
            mp_buf[me] = jnp.dot(cact_all, wada_ref[...], preferred_element_type=F32,
                                 precision=lax.Precision.HIGHEST) + bada_ref[:, pl.ds(col0, ADA_SHARD)]
            f_mp = _ag_start(ch_mp, pos)
            _ag_finish(ch_cw, pos, f_cw)
            _ag_finish(ch_mp, pos, f_mp)
            for d in range(NDEV):
                cwf_ref[0:CK, CONV_SHARD * d:CONV_SHARD * (d + 1)] = cw_buf[d]
            cwf_ref[CK:CKP, :] = jnp.zeros((CKP - CK, CW), F32)
            mod_ref[...] = jnp.concatenate([mp_buf[d, pl.ds(me, 1), :] for d in range(NDEV)], axis=1)

            for j in (1, 0):
                landed(ch_win, j).wait_recv()
                passed(ch_win, j).start()
            from_sibling(ch_win).wait_recv()
            relayed(ch_win, 1).wait_recv()
            first(ch_win, 1).wait_send()
            first(ch_win, 2).wait_send()
            first(ch_win, 3).start()
            wout = wout_ref[...].astype(BF)
            woutf_ref[pl.ds(pl.multiple_of(me * OUT_SHARD, 16), OUT_SHARD), :] = wout
            for j in range(4):
                first(ch_wout, j).start()

        row0 = pl.multiple_of((s % NT) * TS, TS)

        @pl.when(s < NT)
        def _():
            xv = x_ref[...]
            shift = mod_ref[:, 0:D]
            scale = mod_ref[:, D:2 * D]
            r = lax.rsqrt(jnp.mean(xv * xv, axis=-1, keepdims=True) + EPS)
            hb = ((xv * r * nw_ref[...]) * (1.0 + scale) + shift).astype(BF)
            h_s[pl.ds(row0, TS), :] = hb
            w_half = wtf_ref[pl.ds(pl.multiple_of(x * HALF, 16), HALF), :]
            raw0[pl.ds(row0, TS), :] = lax.dot_general(hb, w_half, (((1,), (1,)), ((), ())),
                                                       preferred_element_type=F32)

        @pl.when(s == NT)
        def _():
            relayed(ch_win, 0).wait_recv()
            landed(ch_win, 2).wait_recv()
            passed(ch_win, 2).start()
            relayed(ch_win, 2).wait_recv()
            pltpu.make_async_copy(wtf_ref, wtf_hbm, out_sems.at[0]).start()

        @pl.when(s >= NT)
        def _():
            hb = h_s[pl.ds(row0, TS), :]
            h_ref[...] = hb
            w_half = wtf_ref[pl.ds(pl.multiple_of((1 - x) * HALF, 16), HALF), :]
            raw1 = lax.dot_general(hb, w_half, (((1,), (1,)), ((), ())), preferred_element_type=F32)
            pt[:, pl.ds(pl.multiple_of(x * HALF, 128), HALF)] = raw0[pl.ds(row0, TS), :]
            pt[:, pl.ds(pl.multiple_of((1 - x) * HALF, 128), HALF)] = raw1
            cos = cos_ref[...]
            sin = sin_ref[...]
            q = pt[:, 0:512]
            qraw_ref[...] = q
            qr_ref[...] = _norm_rope_fwd(q, qw_ref[...], jnp.tile(cos, (1, 4)), jnp.tile(sin, (1, 4)),
                                         bq_ref).astype(BF)
            k = pt[:, 512:640]
            kraw_ref[...] = k
            kr_ref[...] = _norm_rope_fwd(k, kw_ref[...], cos, sin, bk_ref).astype(BF)
            vb_ref[...] = pt[:, 640:768].astype(BF)
            ga_ref[...] = pt[:, 768:1280]
            a = pt[:, 1280:1792]
            g = pt[:, 1792:2304]
            a_ref[...] = a
            g_ref[...] = g
            z_ref[...] = a * _sigmoid(g)
            gb_ref[...] = pt[:, 2304:2816]

        @pl.when(s == 2 * NT - 1)
        def _():
            for j in range(3):
                landed(ch_wout, j).wait_recv()
                passed(ch_wout, j).start()
            from_sibling(ch_wout).wait_recv()
            for j in range(3):
                relayed(ch_wout, j).wait_recv()
            out_copy = pltpu.make_async_copy(woutf_ref, woutf_hbm, out_sems.at[1])
            out_copy.start()
            pltpu.make_async_copy(wtf_ref, wtf_hbm, out_sems.at[0]).wait()
            out_copy.wait()
            first(ch_win, 0).wait_send()
            first(ch_win, 3).wait_send()
            for j in range(3):
                passed(ch_win, j).wait_send()
                passed(ch_wout, j).wait_send()
            for j in range(4):
                first(ch_wout, j).wait_send()

    early = lambda i: (jnp.minimum(i, NT - 1), 0)
    late = lambda i: (jnp.maximum(i - NT, 0), 0)
    t512 = pl.BlockSpec((TS, 512), late)
    t128 = pl.BlockSpec((TS, 128), late)
    sem = pltpu.SemaphoreType.DMA((7,))
    sds = jax.ShapeDtypeStruct
    return pl.pallas_call(
        body, name="gather_fwd", grid=(2 * NT,),
        out_shape=[sds((INW, D), BF), sds((D, D), BF), sds((CKP, CW), F32), sds((1, 3 * D), F32), sds((NDEV, D), F32),
                   sds((S, D), BF), sds((S, AW), F32), sds((S, KVW), F32), sds((S, AW), F32), sds((S, CW), F32),
                   sds((S, CW), F32), sds((S, CW), F32), sds((S, AW), BF), sds((S, KVW), BF), sds((S, KVW), BF),
                   sds((S, CW), F32)],
        in_specs=[_VMEM] * 6 + [pl.BlockSpec((TS, D), early), pl.BlockSpec((1, D), _const),
                                pl.BlockSpec((1, AW), _const), pl.BlockSpec((1, KVW), _const), t128, t128,
                                pl.BlockSpec((AW, AW), _const), pl.BlockSpec((KVW, KVW), _const)],
        out_specs=[_ANY, _ANY] + [_VMEM] * 3 + [pl.BlockSpec((TS, D), late), t512, t128, t512, t512, t512, t512, t512,
                                                t128, t128, t512],
        scratch_shapes=[pltpu.VMEM((NDEV, CK, CONV_SHARD), F32), pltpu.VMEM((NDEV, 8, D), F32),
                        pltpu.VMEM((NDEV, 8, ADA_SHARD), F32), pltpu.VMEM((S, D), BF), pltpu.VMEM((S, HALF), F32),
                        pltpu.VMEM((TS, INW), F32), pltpu.VMEM((INW, D), BF), pltpu.VMEM((D, D), BF)]
        + [sem] * 10 + [pltpu.SemaphoreType.DMA((2,))],
        compiler_params=_params(True),
    )(w_in_t, w_out_s, conv_w_s, c, w_ada_s, b_ada, x2, norm_w, qw_t, kw_t, cos_t, sin_t, bq, bk)


QB = 4
NQB = NB // QB


def _band_mask(has_prev):
    kj = lax.broadcasted_iota(jnp.int32, (2 * BLK, 4 * BLK), 0)
    qi = lax.broadcasted_iota(jnp.int32, (2 * BLK, 4 * BLK), 1) & (BLK - 1)
    dist = qi + BLK - kj
    local = (dist >= 0) & (dist < BLK)
    return local if has_prev is True else local & ((kj >= BLK) | has_prev)


def _key_blocks(sb, prev_ref, cur_ref):
    prev = prev_ref[...] if sb == 0 else cur_ref[BLK * (sb - 1):BLK * sb, :]
    return prev, cur_ref[BLK * sb:BLK * (sb + 1), :]


def _sink_lanes(sink_ref, g):
    lane = lax.broadcasted_iota(jnp.int32, (1, 4 * BLK), 1)
    return jnp.where(lane < BLK, sink_ref[0, 4 * g],
                     jnp.where(lane < 2 * BLK, sink_ref[0, 4 * g + 1],
                               jnp.where(lane < 3 * BLK, sink_ref[0, 4 * g + 2], sink_ref[0, 4 * g + 3])))


def _unstack_t(t):
    return [t[:, BLK * h:BLK * (h + 1)].T for h in range(4)]


def _stack_heads(t, g):
    return jnp.concatenate([t[:, HD * (4 * g + h):HD * (4 * g + h + 1)] for h in range(4)], axis=0)


def _band(prev, cur, g):
    return jnp.concatenate([prev[:, HD * g:HD * (g + 1)], cur[:, HD * g:HD * (g + 1)]], axis=0)


def _softmax_band(qs, kb, mask, sink):
    s = lax.dot_general(kb, qs, (((1,), (1,)), ((), ())), preferred_element_type=F32) * (HD ** -0.5)
    s = jnp.where(mask, s, NEG)
    m = jnp.maximum(jnp.max(s, axis=0, keepdims=True), sink)
    e = jnp.exp(s - m)
    es = jnp.exp(sink - m)
    inv = 1.0 / (jnp.sum(e, axis=0, keepdims=True) + es)
    return e * inv, es * inv


def _attn_fwd_call(sinks, qr, kr, vb):
    def body(sink_ref, q_ref, kp_ref, kc_ref, vp_ref, vc_ref, o_ref):
        i = pl.program_id(0)
        for sb in range(QB):
            rows = slice(BLK * sb, BLK * (sb + 1))
            mask = _band_mask(i > 0 if sb == 0 else True)
            q = q_ref[rows, :]
            kp, kc = _key_blocks(sb, kp_ref, kc_ref)
            vp, vc = _key_blocks(sb, vp_ref, vc_ref)
            for g in range(NKV):
                p, _ = _softmax_band(_stack_heads(q, g), _band(kp, kc, g), mask, _sink_lanes(sink_ref, g))
                o_t = lax.dot_general(_band(vp, vc, g), p.astype(BF), (((0,), (0,)), ((), ())),
                                      preferred_element_type=F32)
                for h, o_h in enumerate(_unstack_t(o_t)):
                    o_ref[rows, HD * (4 * g + h):HD * (4 * g + h + 1)] = o_h

    prev = lambda i: (jnp.maximum(QB * i - 1, 0), 0)
    return pl.pallas_call(
        body, name="attn_fwd", grid=(NQB,),
        out_shape=jax.ShapeDtypeStruct((S, AW), F32),
        in_specs=[_SMEM, pl.BlockSpec((QB * BLK, AW), _row), pl.BlockSpec((BLK, KVW), prev),
                  pl.BlockSpec((QB * BLK, KVW), _row), pl.BlockSpec((BLK, KVW), prev),
                  pl.BlockSpec((QB * BLK, KVW), _row)],
        out_specs=pl.BlockSpec((QB * BLK, AW), _row),
        compiler_params=_params(True),
    )(sinks, qr, kr, kr, vb, vb)


def _attn_bwd_call(sinks, qr, kr, vb, d_ya, ga, o, qraw, qw_t, cos_t, sin_t, bq):
    def body(sink_ref, q_ref, kp_ref, kc_ref, vp_ref, vc_ref, dya_ref, ga_ref, o_ref, qraw_ref, qw_ref,
             cos_ref, sin_ref, bq_ref,
             dqraw_ref, dga_ref, dk_ref, dv_ref, gqw_ref, gsink_ref):
        i = pl.program_id(0)

        @pl.when(i == 0)
        def _():
            dk_ref[...] = jnp.zeros((S, KVW), F32)
            dv_ref[...] = jnp.zeros((S, KVW), F32)
            gqw_ref[...] = jnp.zeros((1, AW), F32)
            gsink_ref[...] = jnp.zeros((1, 128), F32)

        d_ya = dya_ref[...]
        act, dact = _silu_and_grad(ga_ref[...])
        dga_ref[...] = (d_ya * o_ref[...] * dact).astype(BF)
        d_o_all = (d_ya * act).astype(BF)
        lane = lax.broadcasted_iota(jnp.int32, (1, 128), 1)
        gsink = jnp.zeros((1, 128), F32)
        dq_rows = []
        for sb in range(QB):
            rows = slice(BLK * sb, BLK * (sb + 1))
            mask = _band_mask(i > 0 if sb == 0 else True)
            q = q_ref[rows, :]
            d_o = d_o_all[rows, :]
            kp, kc = _key_blocks(sb, kp_ref, kc_ref)
            vp, vc = _key_blocks(sb, vp_ref, vc_ref)
            dq_parts, dk_parts, dv_parts = [], [], []
            for g in range(NKV):
                qs = _stack_heads(q, g)
                kb = _band(kp, kc, g)
                vbd = _band(vp, vc, g)
                p, ps = _softmax_band(qs, kb, mask, _sink_lanes(sink_ref, g))
                dos = _stack_heads(d_o, g)
                dp = lax.dot_general(vbd, dos, (((1,), (1,)), ((), ())), preferred_element_type=F32)
                dr = jnp.sum(p * dp, axis=0, keepdims=True)
                ds = (p * (dp - dr) * (HD ** -0.5)).astype(BF)
                sink_term = ps * dr
                for h in range(4):
                    part = jnp.sum(sink_term[:, BLK * h:BLK * (h + 1)], axis=1, keepdims=True)
                    gsink = gsink - jnp.where(lane == 4 * g + h, part, 0.0)
                dv_parts.append(jnp.dot(p.astype(BF), dos, preferred_element_type=F32))
                dk_parts.append(jnp.dot(ds, qs, preferred_element_type=F32))
                dq_t = lax.dot_general(kb, ds, (((0,), (0,)), ((), ())), preferred_element_type=F32)
                dq_parts.extend(_unstack_t(dq_t))
            dkb = jnp.concatenate(dk_parts, axis=1)
            dvb = jnp.concatenate(dv_parts, axis=1)
            blk = QB * i + sb
            r_prev = pl.multiple_of(jnp.maximum(blk - 1, 0) * BLK, BLK)
            r_cur = pl.multiple_of(blk * BLK, BLK)
            dk_ref[pl.ds(r_prev, BLK), :] += dkb[0:BLK]
            dv_ref[pl.ds(r_prev, BLK), :] += dvb[0:BLK]
            dk_ref[pl.ds(r_cur, BLK), :] += dkb[BLK:2 * BLK]
            dv_ref[pl.ds(r_cur, BLK), :] += dvb[BLK:2 * BLK]
            dq_rows.append(jnp.concatenate(dq_parts, axis=1))
        gsink_ref[...] += gsink
        dq = jnp.concatenate(dq_rows, axis=0)
        dq_raw, g_qw = _norm_rope_bwd(dq, qraw_ref[...], qw_ref[...], jnp.tile(cos_ref[...], (1, 4)),
                                      jnp.tile(sin_ref[...], (1, 4)), bq_ref)
        dqraw_ref[...] = dq_raw.astype(BF)
        gqw_ref[...] += g_qw

    prev = lambda i: (jnp.maximum(QB * i - 1, 0), 0)
    b512 = pl.BlockSpec((QB * BLK, AW), _row)
    b128 = pl.BlockSpec((QB * BLK, 128), _row)
    return pl.pallas_call(
        body, name="attn_bwd", grid=(NQB,),
        out_shape=[jax.ShapeDtypeStruct((S, AW), BF), jax.ShapeDtypeStruct((S, AW), BF),
                   jax.ShapeDtypeStruct((S, KVW), F32), jax.ShapeDtypeStruct((S, KVW), F32),
                   jax.ShapeDtypeStruct((1, AW), F32), jax.ShapeDtypeStruct((1, 128), F32)],
        in_specs=[_SMEM, b512, pl.BlockSpec((BLK, KVW), prev), b128, pl.BlockSpec((BLK, KVW), prev), b128,
                  b512, b512, b512, b512, pl.BlockSpec((1, AW), _const), b128, b128, pl.BlockSpec((AW, AW), _const)],
        out_specs=[b512, b512, _VMEM, _VMEM, _VMEM, _VMEM],
        compiler_params=_params(True),
    )(sinks, qr, kr, kr, vb, vb, d_ya, ga, o, qraw, qw_t, cos_t, sin_t, bq)


HALO = 32


RC = 64
LC = 128


def _windows(ext_ref, r0, l0, base):
    col = ext_ref[r0:r0 + RC + HALO, l0:l0 + LC]
    for s in range(8):
        rolled = col if s == 0 else pltpu.roll(col, RC + HALO - s, 0)
        for t in range(CK):
            if (base + t) % 8 == s:
                a8 = base + t - s
                yield t, rolled[a8:a8 + RC]


def _taps(ext_ref, r0, l0, base, weight_row):
    acc = None
    for t, win in _windows(ext_ref, r0, l0, base):
        term = win * weight_row(t)[:, l0:l0 + LC]
        acc = term if acc is None else acc + term
    return acc


def _conv_fwd_call(z, gb, cwf, conv_b, ln_w, ln_b):
    def body(zc_in_ref, zp_ref, gb_ref, cw_ref, cb_ref, lw_ref, lb_ref, zc_ref, yb_ref, zext):
        i = pl.program_id(0)
        zext[0:HALO, :] = jnp.where(i > 0, zp_ref[TS - HALO:TS, :], 0.0)
        zext[HALO:HALO + TS, :] = zc_in_ref[...]
        for r0 in range(0, TS, RC):
            for l0 in range(0, CW, LC):
                acc = _taps(zext, r0, l0, HALO - (CK - 1), lambda k: cw_ref[k:k + 1, :])
                zc_ref[r0:r0 + RC, l0:l0 + LC] = acc + cb_ref[:, l0:l0 + LC]
        zc = zc_ref[...]
        mu = jnp.mean(zc, axis=-1, keepdims=True)
        dz = zc - mu
        rstd = lax.rsqrt(jnp.mean(dz * dz, axis=-1, keepdims=True) + EPS)
        zn = dz * rstd * lw_ref[...] + lb_ref[...]
        gbv = gb_ref[...]
        yb_ref[...] = (zn * _sigmoid(zn)) * (gbv * _sigmoid(gbv))

    t512 = pl.BlockSpec((TS, CW), _row)
    prev = pl.BlockSpec((TS, CW), lambda i: (jnp.maximum(i - 1, 0), 0))
    c512 = pl.BlockSpec((1, CW), _const)
    return pl.pallas_call(
        body, name="conv_fwd", grid=(NT,),
        out_shape=[jax.ShapeDtypeStruct((S, CW), F32), jax.ShapeDtypeStruct((S, CW), F32)],
        in_specs=[t512, prev, t512, pl.BlockSpec((CKP, CW), _const), c512, c512, c512],
        out_specs=[t512, t512],
        scratch_shapes=[pltpu.VMEM((TS + HALO, CW), F32)],
        compiler_params=_params(True),
    )(z, z, gb, cwf, conv_b, ln_w, ln_b)


def _conv_bwd_ln_call(d_yb, zc, gb, ln_w, ln_b):
    def body(dyb_ref, zc_ref, gb_ref, lw_ref, lb_ref, dzc_ref, dgb_ref, glw_ref, glb_ref, gcb_ref):
        i = pl.program_id(0)

        @pl.when(i == 0)
        def _():
            glw_ref[...] = jnp.zeros((1, CW), F32)
            glb_ref[...] = jnp.zeros((1, CW), F32)
            gcb_ref[...] = jnp.zeros((1, CW), F32)

        zc = zc_ref[...]
        mu = jnp.mean(zc, axis=-1, keepdims=True)
        dz = zc - mu
        rstd = lax.rsqrt(jnp.mean(dz * dz, axis=-1, keepdims=True) + EPS)
        zh = dz * rstd
        lw = lw_ref[...]
        zn = zh * lw + lb_ref[...]
        d_yb = dyb_ref[...]
        act_n, dact_n = _silu_and_grad(zn)
        act_g, dact_g = _silu_and_grad(gb_ref[...])
        dgb_ref[...] = (d_yb * act_n * dact_g).astype(BF)
        d_zn = d_yb * act_g * dact_n
        glw_ref[...] += jnp.sum(d_zn * zh, axis=0, keepdims=True)
        glb_ref[...] += jnp.sum(d_zn, axis=0, keepdims=True)
        dzh = d_zn * lw
        d_zc = rstd * (dzh - jnp.mean(dzh, axis=-1, keepdims=True) - zh * jnp.mean(dzh * zh, axis=-1, keepdims=True))
        dzc_ref[...] = d_zc
        gcb_ref[...] += jnp.sum(d_zc, axis=0, keepdims=True)

    t512 = pl.BlockSpec((TS, CW), _row)
    c512 = pl.BlockSpec((1, CW), _const)
    vec = jax.ShapeDtypeStruct((1, CW), F32)
    return pl.pallas_call(
        body, name="conv_bwd_ln", grid=(NT,),
        out_shape=[jax.ShapeDtypeStruct((S, CW), F32), jax.ShapeDtypeStruct((S, CW), BF), vec, vec, vec],
        in_specs=[t512, t512, t512, c512, c512],
        out_specs=[t512, t512, _VMEM, _VMEM, _VMEM],
        compiler_params=_params(True),
    )(d_yb, zc, gb, ln_w, ln_b)


def _conv_bwd_taps_call(d_zc, z, a, g, cwf):
    def body(dc_ref, dn_ref, zc_ref, zp_ref, a_ref, g_ref, cw_ref, da_ref, dg_ref, gcw_ref, dext, zext, gacc):
        i = pl.program_id(0)

        @pl.when(i == 0)
        def _():
            gacc[...] = jnp.zeros((CKP * 8, CW), F32)

        dext[0:TS, :] = dc_ref[...]
        dext[TS:TS + HALO, :] = jnp.where(i < NT - 1, dn_ref[0:HALO, :], 0.0)
        zext[0:HALO, :] = jnp.where(i > 0, zp_ref[TS - HALO:TS, :], 0.0)
        zext[HALO:HALO + TS, :] = zc_ref[...]
        for r0 in range(0, TS, RC):
            for l0 in range(0, CW, LC):
                d_z = _taps(dext, r0, l0, 0, lambda j: cw_ref[CK - 1 - j:CK - j, :])
                sg = _sigmoid(g_ref[r0:r0 + RC, l0:l0 + LC])
                da_ref[r0:r0 + RC, l0:l0 + LC] = (d_z * sg).astype(BF)
                dg_ref[r0:r0 + RC, l0:l0 + LC] = (d_z * a_ref[r0:r0 + RC, l0:l0 + LC] * sg * (1.0 - sg)).astype(BF)
                d_sub = dc_ref[r0:r0 + RC, l0:l0 + LC]
                for k, win in _windows(zext, r0, l0, HALO - (CK - 1)):
                    prod = d_sub * win
                    part = prod[0:8]
                    for q in range(1, RC // 8):
                        part = part + prod[8 * q:8 * q + 8]
                    gacc[8 * k:8 * k + 8, l0:l0 + LC] += part

        @pl.when(i == NT - 1)
        def _():
            for k in range(CK):
                gcw_ref[k:k + 1, :] = jnp.sum(gacc[8 * k:8 * k + 8, :], axis=0, keepdims=True)
            gcw_ref[CK:CKP, :] = jnp.zeros((CKP - CK, CW), F32)

    t512 = pl.BlockSpec((TS, CW), _row)
    prev = pl.BlockSpec((TS, CW), lambda i: (jnp.maximum(i - 1, 0), 0))
    nxt = pl.BlockSpec((TS, CW), lambda i: (jnp.minimum(i + 1, NT - 1), 0))
    return pl.pallas_call(
        body, name="conv_bwd_taps", grid=(NT,),
        out_shape=[jax.ShapeDtypeStruct((S, CW), BF), jax.ShapeDtypeStruct((S, CW), BF),
                   jax.ShapeDtypeStruct((CKP, CW), F32)],
        in_specs=[t512, nxt, t512, prev, t512, t512, pl.BlockSpec((CKP, CW), _const)],
        out_specs=[t512, t512, _VMEM],
        scratch_shapes=[pltpu.VMEM((TS + HALO, CW), F32), pltpu.VMEM((TS + HALO, CW), F32),
                        pltpu.VMEM((CKP * 8, CW), F32)],
        compiler_params=_params(True),
    )(d_zc, d_zc, z, z, a, g, cwf)


def _out_loss_call(o, ga, yb, x2, tgt, mod, w_out_full):
    def body(o_ref, ga_ref, yb_ref, x_ref, t_ref, mod_ref, w_ref,
             dout_ref, dya_ref, dyb_ref, gw_ref, loss_ref, dgate_ref):
        i = pl.program_id(0)

        @pl.when(i == 0)
        def _():
            gw_ref[...] = jnp.zeros((D, D), F32)
            loss_ref[...] = jnp.zeros((1, 128), F32)
            dgate_ref[...] = jnp.zeros((1, D), F32)

        gav = ga_ref[...]
        ya = o_ref[...] * (gav * _sigmoid(gav))
        ycat = jnp.concatenate([ya, yb_ref[...]], axis=1).astype(BF)
        w = w_ref[...]
        y = jnp.dot(ycat, w, preferred_element_type=F32)
        gate = mod_ref[:, 2 * D:3 * D]
        diff = x_ref[...] + gate * y - t_ref[...]
        sq = jnp.sum(jnp.sum(diff * diff, axis=1, keepdims=True), axis=0, keepdims=True)
        loss_ref[...] += jnp.broadcast_to(sq, (1, 128))
        d_out = diff * (1.0 / D)
        dout_ref[...] = d_out
        dgate_ref[...] += jnp.sum(d_out * y, axis=0, keepdims=True)
        dy = (d_out * gate).astype(BF)
        d_ycat = lax.dot_general(dy, w, (((1,), (1,)), ((), ())), preferred_element_type=F32)
        dya_ref[...] = d_ycat[:, 0:AW]
        dyb_ref[...] = d_ycat[:, AW:D]
        gw_ref[...] += lax.dot_general(ycat, dy, (((0,), (0,)), ((), ())), preferred_element_type=F32)

    t512 = pl.BlockSpec((TS, 512), _row)
    t1024 = pl.BlockSpec((TS, D), _row)
    return pl.pallas_call(
        body, name="out_loss", grid=(NT,),
        out_shape=[jax.ShapeDtypeStruct((S, D), F32), jax.ShapeDtypeStruct((S, AW), F32),
                   jax.ShapeDtypeStruct((S, CW), F32), jax.ShapeDtypeStruct((D, D), F32),
                   jax.ShapeDtypeStruct((1, 128), F32), jax.ShapeDtypeStruct((1, D), F32)],
        in_specs=[t512, t512, t512, t1024, t1024, pl.BlockSpec((1, 3 * D), _const),
                  pl.BlockSpec((D, D), _const, pipeline_mode=pl.Buffered(1))],
        out_specs=[t1024, t512, t512, _VMEM, _VMEM, _VMEM],
        compiler_params=_params(True),
    )(o, ga, yb, x2, tgt, mod, w_out_full)


SM_ROWS = 8
PIECES = ((0, 512), (512, 640), (640, 768), (768, 1280), (1280, 1792), (1792, 2304), (2304, 2816))


def _bwd_in_call(dqraw, dk, dv, dga, da, dg, dgb, kraw, kw_t, cos_t, sin_t, bk, h, wt_full, x2, d_out, mod, norm_w,
                 gw_out, gcw, glw, glb, gcb, gqw, gsink, dgate, loss_p, cact_all):
    def body(dq_ref, dk_ref, dv_ref, dga_ref, da_ref, dg_ref, dgb_ref, kraw_ref, kw_ref, cos_ref, sin_ref, bk_ref,
             h_ref, wt_ref, x_ref, dout_ref, mod_ref, nw_ref, gwout_ref, gcw_ref, glw_ref, glb_ref, gcb_ref, gqw_ref,
             gsink_ref, dgate_ref, loss_ref, cact_ref,
             gx_ref, o_gwin, o_gwout, o_gwada, o_gbada, o_gnw, o_gqw, o_gkw, o_gsink, o_gcw, o_gcb, o_glw, o_glb,
             o_loss,
             acc, win_send, win_sib, win_ici, wout_send, wout_sib, wout_ici, sm_buf, cw_buf, dmod_all, vec_acc, gkw_acc,
             wi_ds, wi_dr, wi_is, wi_ir, wo_ds, wo_dr, wo_is, wo_ir, sm_s, sm_r, cw_s, cw_r):
        i = pl.program_id(0)
        pos = _mesh_pos()
        x, y, cc = pos
        me = 4 * x + 2 * y + cc

        def chip(j):
            return (1 - x if j & 1 else x, 1 - y if j & 2 else y)

        def rows_of(buf, px, py, pc, rows, align):
            return buf.at[pl.ds(pl.multiple_of((4 * px + 2 * py + pc) * rows, align), rows), :]

        bufs = {"in": (win_send, win_sib, win_ici, IN_SHARD, wi_ds, wi_dr, wi_is, wi_ir),
                "out": (wout_send, wout_sib, wout_ici, OUT_SHARD, wo_ds, wo_dr, wo_is, wo_ir)}

        def d2d_copy(j, which):
            send, sib, _, rows, ds_, dr_, _, _ = bufs[which]
            px, py = chip(j)
            return pltpu.make_async_remote_copy(src_ref=rows_of(send, px, py, 1 - cc, rows, 16), dst_ref=sib.at[j],
                                                send_sem=ds_.at[j], recv_sem=dr_.at[j], device_id=(x, y, 1 - cc),
                                                device_id_type=MESH)

        def ici_copy(j, which):
            send, _, ici, rows, _, _, is_, ir_ = bufs[which]
            px, py = chip(j)
            return pltpu.make_async_remote_copy(src_ref=rows_of(send, px, py, cc, rows, 16), dst_ref=ici.at[j - 1],
                                                send_sem=is_.at[j - 1], recv_sem=ir_.at[j - 1], device_id=(px, py, cc),
                                                device_id_type=MESH)

        def level2(which, partial_ref):
            send, sib, _, rows, _, _, _, _ = bufs[which]
            for j in range(1, 4):
                d2d_copy(j, which).wait_recv()
                px, py = chip(j)
                mine = rows_of(partial_ref, px, py, cc, rows, 8)[...]
                rows_of(send, px, py, cc, rows, 16)[...] = (mine + sib[j].astype(F32)).astype(BF)
                ici_copy(j, which).start()

        def finish(which, partial_ref):
            _, sib, ici, rows, _, _, _, _ = bufs[which]
            d2d_copy(0, which).wait_recv()
            total = rows_of(partial_ref, x, y, cc, rows, 8)[...] + sib[0].astype(F32)
            for j in range(1, 4):
                ici_copy(j, which).wait_recv()
                total = total + ici[j - 1].astype(F32)
            for j in range(4):
                d2d_copy(j, which).wait_send()
            for j in range(1, 4):
                ici_copy(j, which).wait_send()
            return total

        def dproj_pieces():
            dk_raw, g_kw = _norm_rope_bwd(dk_ref[...], kraw_ref[...], kw_ref[...], cos_ref[...], sin_ref[...], bk_ref)
            return [dq_ref[...], dk_raw.astype(BF), dv_ref[...].astype(BF), dga_ref[...], da_ref[...], dg_ref[...],
                    dgb_ref[...]], g_kw

        @pl.when(i == 0)
        def _():
            acc[...] = jnp.zeros((INW, D), F32)
            vec_acc[...] = jnp.zeros((8, D), F32)
            gkw_acc[...] = jnp.zeros((1, KVW), F32)
            wout_send[...] = gwout_ref[...].astype(BF)
            for j in range(4):
                d2d_copy(j, "out").start()

        @pl.when(i == 2)
        def _():
            level2("out", gwout_ref)

        @pl.when(i < NT)
        def _():
            pieces, g_kw = dproj_pieces()
            gkw_acc[...] += g_kw
            hv = h_ref[...]
            for (lo, hi), piece in zip(PIECES, pieces):
                acc[lo:hi, :] += lax.dot_general(piece, hv, (((0,), (0,)), ((), ())), preferred_element_type=F32)

        @pl.when(i == NT - 1)
        def _():
            for lo, hi in PIECES:
                win_send[lo:hi, :] = acc[lo:hi, :].astype(BF)
            for j in range(4):
                d2d_copy(j, "in").start()

        @pl.when(i == NT + 2)
        def _():
            level2("in", acc)

        @pl.when(i >= NT)
        def _():
            pieces, _ = dproj_pieces()
            dproj = jnp.concatenate(pieces, axis=1)
            d_h = jnp.dot(dproj, wt_ref[...], preferred_element_type=F32)
            xv = x_ref[...]
            scale = mod_ref[:, D:2 * D]
            nw = nw_ref[...]
            r = lax.rsqrt(jnp.mean(xv * xv, axis=-1, keepdims=True) + EPS)
            xn = xv * r
            vec_acc[0:1, :] += jnp.sum(d_h, axis=0, keepdims=True)
            vec_acc[1:2, :] += jnp.sum(d_h * (xn * nw), axis=0, keepdims=True)
            d_u = d_h * (1.0 + scale)
            vec_acc[2:3, :] += jnp.sum(d_u * xn, axis=0, keepdims=True)
            d_xn = d_u * nw
            gx_ref[...] = dout_ref[...] + r * (d_xn - xn * jnp.mean(d_xn * xn, axis=-1, keepdims=True))

        @pl.when(i == 2 * NT - 1)
        def _():
            ch_sm = (_slab(sm_buf), sm_s, sm_r)
            ch_cw = (_slab(cw_buf), cw_s, cw_r)
            z128 = jnp.zeros((1, 128), F32)
            row4 = jnp.concatenate([glw_ref[...], glb_ref[...]], axis=1)
            row5 = jnp.concatenate([gcb_ref[...], gqw_ref[...]], axis=1)
            row6 = jnp.concatenate([gkw_acc[...], gsink_ref[...], loss_ref[...]] + [z128] * 5, axis=1)
            sm_buf[me] = jnp.concatenate([vec_acc[0:2, :], dgate_ref[...], vec_acc[2:3, :], row4, row5, row6,
                                          jnp.zeros((1, D), F32)], axis=0)
            f_sm = _ag_start(ch_sm, pos)
            cw_buf[me] = gcw_ref[...]
            f_cw = _ag_start(ch_cw, pos)
            _ag_finish(ch_sm, pos, f_sm)
            _ag_finish(ch_cw, pos, f_cw)
            tot = sm_buf[0]
            cw_tot = cw_buf[0]
            for d in range(1, NDEV):
                tot = tot + sm_buf[d]
                cw_tot = cw_tot + cw_buf[d]
            o_gbada[...] = jnp.concatenate([tot[0:1, :], tot[1:2, :], tot[2:3, :]], axis=1)
            o_gnw[...] = tot[3:4, :]
            o_glw[...] = tot[4:5, 0:CW]
            o_glb[...] = tot[4:5, CW:D]
            o_gcb[...] = tot[5:6, 0:CW]
            gq = tot[5:6, CW:CW + HD]
            for hh in range(1, NQ):
                gq = gq + tot[5:6, CW + HD * hh:CW + HD * (hh + 1)]
            o_gqw[...] = gq
            o_gkw[...] = tot[6:7, 0:HD] + tot[6:7, HD:2 * HD]
            o_gsink[...] = tot[6:7, 128:128 + NQ]
            o_loss[...] = tot[6:7, 256:384] * (0.5 / D)
            mine = jnp.zeros((CK, CONV_SHARD), F32)
            for d in range(NDEV):
                mine = mine + jnp.where(me == d, cw_tot[0:CK, CONV_SHARD * d:CONV_SHARD * (d + 1)], 0.0)
            for k in range(CK):
                o_gcw[k] = mine[k:k + 1, :]
            for d in range(NDEV):
                dmod_all[d:d + 1, :] = jnp.concatenate([sm_buf[d, 0:1, :], sm_buf[d, 1:2, :], sm_buf[d, 2:3, :]],
                                                       axis=1)
            col0 = pl.multiple_of(me * ADA_SHARD, 128)
            o_gwada[...] = lax.dot_general(cact_ref[...], dmod_all[:, pl.ds(col0, ADA_SHARD)], (((0,), (0,)), ((), ())),
                                           preferred_element_type=F32, precision=lax.Precision.HIGHEST)

            o_gwout[...] = finish("out", gwout_ref)
            o_gwin[...] = finish("in", acc)

    half = lambda i: (i % NT, 0)
    late = lambda i: (jnp.maximum(i - NT, 0), 0)
    t512 = pl.BlockSpec((TS, 512), half)
    t128 = pl.BlockSpec((TS, 128), half)
    l1024 = pl.BlockSpec((TS, D), late)
    sem7 = pltpu.SemaphoreType.DMA((7,))
    sem4 = pltpu.SemaphoreType.DMA((4,))
    sem3 = pltpu.SemaphoreType.DMA((3,))
    sds = jax.ShapeDtypeStruct
    return pl.pallas_call(
        body, name="bwd_in", grid=(2 * NT,),
        out_shape=[sds((S, D), F32), sds((IN_SHARD, D), F32), sds((OUT_SHARD, D), F32), sds((D, ADA_SHARD), F32),
                   sds((1, 3 * D), F32), sds((1, D), F32), sds((1, HD), F32), sds((1, HD), F32), sds((1, NQ), F32),
                   sds((CK, 1, CONV_SHARD), F32), sds((1, CW), F32), sds((1, CW), F32), sds((1, CW), F32),
                   sds((1, 128), F32)],
        in_specs=[t512, t128, t128, t512, t512, t512, t512, t128, pl.BlockSpec((1, KVW), _const), t128, t128,
                  pl.BlockSpec((KVW, KVW), _const), pl.BlockSpec((TS, D), half),
                  pl.BlockSpec((INW, D), _const, pipeline_mode=pl.Buffered(1)), l1024, l1024,
                  pl.BlockSpec((1, 3 * D), _const), pl.BlockSpec((1, D), _const)] + [_VMEM] * 10,
        out_specs=[l1024] + [_VMEM] * 13,
        scratch_shapes=[pltpu.VMEM((INW, D), F32), pltpu.VMEM((INW, D), BF), pltpu.VMEM((4, IN_SHARD, D), BF),
                        pltpu.VMEM((3, IN_SHARD, D), BF), pltpu.VMEM((D, D), BF), pltpu.VMEM((4, OUT_SHARD, D), BF),
                        pltpu.VMEM((3, OUT_SHARD, D), BF),
                        pltpu.VMEM((NDEV, SM_ROWS, D), F32), pltpu.VMEM((NDEV, CKP, CW), F32),
                        pltpu.VMEM((NDEV, 3 * D), F32), pltpu.VMEM((8, D), F32), pltpu.VMEM((1, KVW), F32)]
        + [sem4, sem4, sem3, sem3] * 2 + [sem7] * 4,
        compiler_params=pltpu.CompilerParams(dimension_semantics=("arbitrary",), vmem_limit_bytes=BIG_VMEM_LIMIT),
    )(dqraw, dk, dv, dga, da, dg, dgb, kraw, kw_t, cos_t, sin_t, bk, h, wt_full, x2, d_out, mod, norm_w,
      gw_out, gcw, glw, glb, gcb, gqw, gsink, dgate, loss_p, cact_all)


ADAM_STEPS = 4


def _adam_call(ws, gs, ms, vs, grad_x, loss_v):
    n = len(ws)
    bc1 = 1.0 - ADAM_B1 ** ADAM_STEP
    bc2 = 1.0 - ADAM_B2 ** ADAM_STEP
    chunked = [w.ndim == 2 and w.shape[0] % (8 * ADAM_STEPS) == 0 for w in ws]

    def body(*refs):
        ins, outs = refs[:4 * n + 2], refs[4 * n + 2:]
        i = pl.program_id(0)

        def update(j):
            w, g, m, v = (ins[j][...], ins[n + j][...], ins[2 * n + j][...], ins[3 * n + j][...])
            m_new = ADAM_B1 * m + (1.0 - ADAM_B1) * g
            v_new = ADAM_B2 * v + (1.0 - ADAM_B2) * (g * g)
            m_hat = m_new / bc1
            v_hat = v_new / bc2
            outs[j][...] = g
            outs[n + j][...] = -ADAM_LR * (m_hat / (jnp.sqrt(v_hat) + ADAM_EPS) + ADAM_WD * w)
            outs[2 * n + j][...] = m_new
            outs[3 * n + j][...] = v_new

        for j in range(n):
            if chunked[j]:
                update(j)
        outs[4 * n][...] = ins[4 * n][...]

        @pl.when(i == 0)
        def _():
            for j in range(n):
                if not chunked[j]:
                    update(j)
            outs[4 * n + 1][...] = ins[4 * n + 1][...]

    def spec(arr, is_chunked):
        if is_chunked:
            return pl.BlockSpec((arr.shape[0] // ADAM_STEPS, arr.shape[1]), _row)
        zeros = (0,) * arr.ndim
        return pl.BlockSpec(arr.shape, lambda i: zeros)

    par_specs = [spec(w, ch) for w, ch in zip(ws, chunked)]
    extra = [spec(grad_x, True), spec(loss_v, False)]
    shapes = [jax.ShapeDtypeStruct(w.shape, F32) for w in ws]
    res = pl.pallas_call(
        body, name="adam", grid=(ADAM_STEPS,),
        out_shape=shapes * 4 + [jax.ShapeDtypeStruct(grad_x.shape, F32), jax.ShapeDtypeStruct(loss_v.shape, F32)],
        in_specs=par_specs * 4 + extra, out_specs=par_specs * 4 + extra,
        compiler_params=_params(True),
    )(*ws, *gs, *ms, *vs, grad_x, loss_v)
    return res[0:n], res[n:2 * n], res[2 * n:3 * n], res[3 * n:4 * n], res[4 * n], res[4 * n + 1]


def _rope_tables():
    inv = (np.float32(ROPE_THETA) ** (-np.arange(0, HD, 2, dtype=np.float32) / np.float32(HD))).astype(np.float32)
    ang = (np.arange(S, dtype=np.float32)[:, None] * inv[None, :]).astype(np.float32)
    cos, sin = np.cos(ang).astype(np.float32), np.sin(ang).astype(np.float32)
    cos64 = np.concatenate([cos, cos], axis=-1)
    sin64 = np.concatenate([-sin, sin], axis=-1)
    return jnp.asarray(np.tile(cos64, (1, 2))), jnp.asarray(np.tile(sin64, (1, 2)))


def _group_matrix(width):
    idx = np.arange(width) // HD
    return jnp.asarray(np.where(idx[:, None] == idx[None, :], 1.0 / HD, 0.0).astype(np.float32)).astype(BF)


def kernel(x, c, w_ada, b_ada, norm_w, w_in, q_norm_w, k_norm_w, sinks, conv_w, conv_b, ln_w, ln_b, w_out, loss_target, m_w_ada, m_b_ada, m_norm_w, m_w_in, m_q_norm_w, m_k_norm_w, m_sinks, m_conv_w, m_conv_b, m_ln_w, m_ln_b, m_w_out, v_w_ada, v_b_ada, v_norm_w, v_w_in, v_q_norm_w, v_k_norm_w, v_sinks, v_conv_w, v_conv_b, v_ln_w, v_ln_b, v_w_out):
    x2 = x[0]
    tgt = loss_target[0]
    cos_t, sin_t = _rope_tables()
    bq = _group_matrix(AW)
    bk = _group_matrix(KVW)
    qw_t = jnp.tile(q_norm_w, (1, NQ))
    kw_t = jnp.tile(k_norm_w, (1, NKV))

    tr = lambda t: jnp.swapaxes(t[0], 0, 1)
    tc = lambda t: jnp.swapaxes(t, 0, 1)
    (wt_full, w_out_full, cwf, mod, cact_all, h, qraw, kraw, ga, a, g, gb, qr, kr, vb, z) = _gather_fwd_call(
        tr(w_in), w_out[0], tc(conv_w), c, w_ada[0], b_ada, x2, norm_w, qw_t, kw_t, cos_t, sin_t, bq, bk)
    o = _attn_fwd_call(sinks, qr, kr, vb)
    zc, yb = _conv_fwd_call(z, gb, cwf, conv_b, ln_w, ln_b)
    d_out, d_ya, d_yb, gw_out, loss_p, dgate = _out_loss_call(o, ga, yb, x2, tgt, mod, w_out_full)

    d_zc, dgb, glw, glb, gcb = _conv_bwd_ln_call(d_yb, zc, gb, ln_w, ln_b)
    da, dg, gcw = _conv_bwd_taps_call(d_zc, z, a, g, cwf)
    dqraw, dga, dk, dv, gqw, gsink = _attn_bwd_call(sinks, qr, kr, vb, d_ya, ga, o, qraw, qw_t, cos_t, sin_t, bq)
    (grad_x, g_w_in_t, g_w_out, g_w_ada, g_b_ada, g_norm_w, g_qw, g_kw, g_sinks, g_conv_w, g_conv_b, g_ln_w, g_ln_b,
     loss_v) = _bwd_in_call(dqraw, dk, dv, dga, da, dg, dgb, kraw, kw_t, cos_t, sin_t, bk, h, wt_full, x2, d_out, mod,
                            norm_w, gw_out, gcw, glw, glb, gcb, gqw, gsink, dgate, loss_p, cact_all)

    ws = [w_ada[0], b_ada, norm_w, tr(w_in), q_norm_w, k_norm_w, sinks, tc(conv_w), conv_b, ln_w, ln_b, w_out[0]]
    gs = [g_w_ada, g_b_ada, g_norm_w, g_w_in_t, g_qw, g_kw, g_sinks, g_conv_w, g_conv_b, g_ln_w, g_ln_b, g_w_out]
    ms = [m_w_ada[0], m_b_ada, m_norm_w, tr(m_w_in), m_q_norm_w, m_k_norm_w, m_sinks, tc(m_conv_w), m_conv_b, m_ln_w,
          m_ln_b, m_w_out[0]]
    vs = [v_w_ada[0], v_b_ada, v_norm_w, tr(v_w_in), v_q_norm_w, v_k_norm_w, v_sinks, tc(v_conv_w), v_conv_b, v_ln_w,
          v_ln_b, v_w_out[0]]
    grads, deltas, new_m, new_v, grad_x, loss_v = _adam_call(ws, gs, ms, vs, grad_x, loss_v)
    shaped = [w_ada, b_ada, norm_w, w_in, q_norm_w, k_norm_w, sinks, conv_w, conv_b, ln_w, ln_b, w_out]
    W_IN_POS, CONV_W_POS = 3, 7

    def like(vals):
        vals = [jnp.swapaxes(v, 0, 1) if j in (W_IN_POS, CONV_W_POS) else v for j, v in enumerate(vals)]
        return [v.reshape(s.shape) for v, s in zip(vals, shaped)]

    return (loss_v[0, 0], grad_x[None], *like(grads), *like(deltas), *like(new_m), *like(new_v))
```

```python
import functools

import jax
import jax.numpy as jnp
import numpy as np
from jax import lax
from jax.experimental import pallas as pl
from jax.experimental.pallas import tpu as pltpu

S = 2048
D = 1024
NDEV = 8
HD = 64
NQ = 8
NKV = 2
AW = 512
KVW = 128
CW = 512
INW = 2816
IN_SHARD = INW // NDEV
ADA_SHARD = 3 * D // NDEV
OUT_SHARD = D // NDEV
CONV_SHARD = CW // NDEV
CK = 31
CKP = 32
BLK = 128
TS = 256
NT = S // TS
NB = S // BLK
EPS = 1e-6
ROPE_THETA = 10000.0
NEG = -1e30
BF = jnp.bfloat16
F32 = jnp.float32

ADAM_LR = 0.001
ADAM_B1 = 0.9
ADAM_B2 = 0.999
ADAM_EPS = 1e-08
ADAM_WD = 0.01
ADAM_STEP = 10

VMEM_LIMIT = 56 * 1024 * 1024
BIG_VMEM_LIMIT = 62 * 1024 * 1024
MESH = pl.DeviceIdType.MESH

_VMEM = pl.BlockSpec(memory_space=pltpu.VMEM)
_SMEM = pl.BlockSpec(memory_space=pltpu.SMEM)
_ANY = pl.BlockSpec(memory_space=pl.ANY)


def _params(grid=False):
    if grid:
        return pltpu.CompilerParams(dimension_semantics=("arbitrary",), vmem_limit_bytes=VMEM_LIMIT)
    return pltpu.CompilerParams(vmem_limit_bytes=VMEM_LIMIT)


def _row(i):
    return (i, 0)


def _const(i):
    return (0, 0)


def _sigmoid(t):
    return 1.0 / (1.0 + jnp.exp(-t))


def _silu_and_grad(t):
    sg = _sigmoid(t)
    return t * sg, sg * (1.0 + t * (1.0 - sg))


def _group_mean(t, b_ref):
    hi = t.astype(BF)
    lo = (t - hi.astype(F32)).astype(BF)
    b = b_ref[...]
    return jnp.dot(hi, b, preferred_element_type=F32) + jnp.dot(lo, b, preferred_element_type=F32)


def _partner(t):
    w = t.shape[-1]
    lane = lax.broadcasted_iota(jnp.int32, t.shape, 1)
    first = (lane & 32) == 0
    return jnp.where(first, pltpu.roll(t, w - 32, 1), pltpu.roll(t, 32, 1))


def _norm_rope_fwd(t, w_t, cos, sin, b_ref):
    r = lax.rsqrt(_group_mean(t * t, b_ref) + EPS)
    tn = t * r * w_t
    return tn * cos + _partner(tn) * sin


def _norm_rope_bwd(d_out, t, w_t, cos, sin, b_ref):
    d_tn = d_out * cos + _partner(d_out * sin)
    r = lax.rsqrt(_group_mean(t * t, b_ref) + EPS)
    th = t * r
    g_w = jnp.sum(d_tn * th, axis=0, keepdims=True)
    d_th = d_tn * w_t
    d_t = r * (d_th - th * _group_mean(d_th * th, b_ref))
    return d_t, g_w


def _mesh_pos():
    return lax.axis_index("x"), lax.axis_index("y"), lax.axis_index("c")


def _ag_copy(chan, k, block, to):
    blk, send_sems, recv_sems = chan
    ref = blk(*block)
    return pltpu.make_async_remote_copy(src_ref=ref, dst_ref=ref, send_sem=send_sems.at[k],
                                        recv_sem=recv_sems.at[k], device_id=to, device_id_type=MESH)


def _ag_start(chan, pos):
    x, y, c = pos
    me = (x, y, c)
    chips = [(1 - x, y), (x, 1 - y), (1 - x, 1 - y)]
    first = [_ag_copy(chan, 0, me, (x, y, 1 - c))]
    first += [_ag_copy(chan, 1 + j, me, (*chip, c)) for j, chip in enumerate(chips)]
    for cp in first:
        cp.start()
    return first


def _ag_finish(chan, pos, first):
    x, y, c = pos
    me = (x, y, c)
    sibling = (x, y, 1 - c)
    chips = [(1 - x, y), (x, 1 - y), (1 - x, 1 - y)]
    passed = [_ag_copy(chan, 4 + j, (*chip, c), sibling) for j, chip in enumerate(chips)]
    for j, chip in enumerate(chips):
        _ag_copy(chan, 1 + j, (*chip, c), me).wait_recv()
        passed[j].start()
    _ag_copy(chan, 0, sibling, me).wait_recv()
    for j, chip in enumerate(chips):
        _ag_copy(chan, 4 + j, (*chip, 1 - c), me).wait_recv()
    for cp in first + passed:
        cp.wait_send()


def _slab(buf):
    return lambda px, py, pc: buf.at[4 * px + 2 * py + pc]


def _row_block(buf, rows, align):
    return lambda px, py, pc: buf.at[pl.ds(pl.multiple_of((4 * px + 2 * py + pc) * rows, align), rows), :]


HALF = INW // 2


def _gather_fwd_call(w_in_t, w_out_s, conv_w_s, c, w_ada_s, b_ada, x2, norm_w, qw_t, kw_t, cos_t, sin_t, bq, bk):
    def body(win_ref, wout_ref, cw_ref, c_ref, wada_ref, bada_ref, x_ref, nw_ref, qw_ref, kw_ref, cos_ref, sin_ref,
             bq_ref, bk_ref,
             wtf_hbm, woutf_hbm, cwf_ref, mod_ref, cact_ref,
             h_ref, qraw_ref, kraw_ref, ga_ref, a_ref, g_ref, gb_ref, qr_ref, kr_ref, vb_ref, z_ref,
             cw_buf, ca_buf, mp_buf, h_s, raw0, pt, wtf_ref, woutf_ref,
             s0, r0, s1, r1, s2, r2, s3, r3, s4, r4, out_sems):
        s = pl.program_id(0)
        pos = _mesh_pos()
        x, y, cc = pos
        me3 = (x, y, cc)
        me = 4 * x + 2 * y + cc
        sibling = (x, y, 1 - cc)
        chips = [(1 - x, y), (x, 1 - y), (1 - x, 1 - y)]
        ch_win = (_row_block(wtf_ref, IN_SHARD, 16), s0, r0)
        ch_wout = (_row_block(woutf_ref, OUT_SHARD, 16), s1, r1)
        ch_cw = (_slab(cw_buf), s2, r2)
        ch_ca = (_slab(ca_buf), s3, r3)
        ch_mp = (_slab(mp_buf), s4, r4)

        def first(chan, j):
            return _ag_copy(chan, j, me3, sibling if j == 0 else (*chips[j - 1], cc))

        def passed(chan, j):
            return _ag_copy(chan, 4 + j, (*chips[j], cc), sibling)

        def landed(chan, j):
            return _ag_copy(chan, 1 + j, (*chips[j], cc), me3)

        def relayed(chan, j):
            return _ag_copy(chan, 4 + j, (*chips[j], 1 - cc), me3)

        def from_sibling(chan):
            return _ag_copy(chan, 0, sibling, me3)

        @pl.when(s == 0)
        def _():
            cv = c_ref[...]
            ca_buf[me] = jnp.broadcast_to(cv * _sigmoid(cv), (8, D))
            f_ca = _ag_start(ch_ca, pos)
            wtf_ref[pl.ds(pl.multiple_of(me * IN_SHARD, 16), IN_SHARD), :] = win_ref[...].astype(BF)
            for j in range(3):
                first(ch_win, j).start()
            cw_buf[me] = cw_ref[:, 0, :]
            f_cw = _ag_start(ch_cw, pos)

            _ag_finish(ch_ca, pos, f_ca)
            cact_all = jnp.concatenate([ca_buf[d, 0:1, :] for d in range(NDEV)], axis=0)
            cact_ref[...] = cact_all
            col0 = pl.multiple_of(me * ADA_SHARD, 128)
            mp_buf[me] = jnp.dot(cact_all, wada_ref[...], preferred_element_type=F32,
                                 precision=lax.Precision.HIGHEST) + bada_ref[:, pl.ds(col0, ADA_SHARD)]
            f_mp = _ag_start(ch_mp, pos)
            _ag_finish(ch_cw, pos, f_cw)
            _ag_finish(ch_mp, pos, f_mp)
            for d in range(NDEV):
                cwf_ref[0:CK, CONV_SHARD * d:CONV_SHARD * (d + 1)] = cw_buf[d]
            cwf_ref[CK:CKP, :] = jnp.zeros((CKP - CK, CW), F32)
            mod_ref[...] = jnp.concatenate([mp_buf[d, pl.ds(me, 1), :] for d in range(NDEV)], axis=1)

            for j in (1, 0):
                landed(ch_win, j).wait_recv()
                passed(ch_win, j).start()
            from_sibling(ch_win).wait_recv()
            relayed(ch_win, 1).wait_recv()
            first(ch_win, 1).wait_send()
            first(ch_win, 2).wait_send()
            first(ch_win, 3).start()
            wout = wout_ref[...].astype(BF)
            woutf_ref[pl.ds(pl.multiple_of(me * OUT_SHARD, 16), OUT_SHARD), :] = wout
            for j in range(4):
                first(ch_wout, j).start()

        row0 = pl.multiple_of((s % NT) * TS, TS)

        @pl.when(s < NT)
        def _():
            xv = x_ref[...]
            shift = mod_ref[:, 0:D]
            scale = mod_ref[:, D:2 * D]
            r = lax.rsqrt(jnp.mean(xv * xv, axis=-1, keepdims=True) + EPS)
            hb = ((xv * r * nw_ref[...]) * (1.0 + scale) + shift).astype(BF)
            h_s[pl.ds(row0, TS), :] = hb
            w_half = wtf_ref[pl.ds(pl.multiple_of(x * HALF, 16), HALF), :]
            raw0[pl.ds(row0, TS), :] = lax.dot_general(hb, w_half, (((1,), (1,)), ((), ())),
                                                       preferred_element_type=F32)

        @pl.when(s == NT)
        def _():
            relayed(ch_win, 0).wait_recv()
            landed(ch_win, 2).wait_recv()
            passed(ch_win, 2).start()
            relayed(ch_win, 2).wait_recv()
            pltpu.make_async_copy(wtf_ref, wtf_hbm, out_sems.at[0]).start()

        @pl.when(s >= NT)
        def _():
            hb = h_s[pl.ds(row0, TS), :]
            h_ref[...] = hb
            w_half = wtf_ref[pl.ds(pl.multiple_of((1 - x) * HALF, 16), HALF), :]
            raw1 = lax.dot_general(hb, w_half, (((1,), (1,)), ((), ())), preferred_element_type=F32)
            pt[:, pl.ds(pl.multiple_of(x * HALF, 128), HALF)] = raw0[pl.ds(row0, TS), :]
            pt[:, pl.ds(pl.multiple_of((1 - x) * HALF, 128), HALF)] = raw1
            cos = cos_ref[...]
            sin = sin_ref[...]
            q = pt[:, 0:512]
            qraw_ref[...] = q
            qr_ref[...] = _norm_rope_fwd(q, qw_ref[...], jnp.tile(cos, (1, 4)), jnp.tile(sin, (1, 4)),
                                         bq_ref).astype(BF)
            k = pt[:, 512:640]
            kraw_ref[...] = k
            kr_ref[...] = _norm_rope_fwd(k, kw_ref[...], cos, sin, bk_ref).astype(BF)
            vb_ref[...] = pt[:, 640:768].astype(BF)
            ga_ref[...] = pt[:, 768:1280]
            a = pt[:, 1280:1792]
            g = pt[:, 1792:2304]
            a_ref[...] = a
            g_ref[...] = g
            z_ref[...] = a * _sigmoid(g)
            gb_ref[...] = pt[:, 2304:2816]

        @pl.when(s == 2 * NT - 1)
        def _():
            for j in range(3):
                landed(ch_wout, j).wait_recv()
                passed(ch_wout, j).start()
            from_sibling(ch_wout).wait_recv()
            for j in range(3):
                relayed(ch_wout, j).wait_recv()
            out_copy = pltpu.make_async_copy(woutf_ref, woutf_hbm, out_sems.at[1])
            out_copy.start()
            pltpu.make_async_copy(wtf_ref, wtf_hbm, out_sems.at[0]).wait()
            out_copy.wait()
            first(ch_win, 0).wait_send()
            first(ch_win, 3).wait_send()
            for j in range(3):
                passed(ch_win, j).wait_send()
                passed(ch_wout, j).wait_send()
            for j in range(4):
                first(ch_wout, j).wait_send()

    early = lambda i: (jnp.minimum(i, NT - 1), 0)
    late = lambda i: (jnp.maximum(i - NT, 0), 0)
    t512 = pl.BlockSpec((TS, 512), late)
    t128 = pl.BlockSpec((TS, 128), late)
    sem = pltpu.SemaphoreType.DMA((7,))
    sds = jax.ShapeDtypeStruct
    return pl.pallas_call(
        body, name="gather_fwd", grid=(2 * NT,),
        out_shape=[sds((INW, D), BF), sds((D, D), BF), sds((CKP, CW), F32), sds((1, 3 * D), F32), sds((NDEV, D), F32),
                   sds((S, D), BF), sds((S, AW), F32), sds((S, KVW), F32), sds((S, AW), F32), sds((S, CW), F32),
                   sds((S, CW), F32), sds((S, CW), F32), sds((S, AW), BF), sds((S, KVW), BF), sds((S, KVW), BF),
                   sds((S, CW), F32)],
        in_specs=[_VMEM] * 6 + [pl.BlockSpec((TS, D), early), pl.BlockSpec((1, D), _const),
                                pl.BlockSpec((1, AW), _const), pl.BlockSpec((1, KVW), _const), t128, t128,
                                pl.BlockSpec((AW, AW), _const), pl.BlockSpec((KVW, KVW), _const)],
        out_specs=[_ANY, _ANY] + [_VMEM] * 3 + [pl.BlockSpec((TS, D), late), t512, t128, t512, t512, t512, t512, t512,
                                                t128, t128, t512],
        scratch_shapes=[pltpu.VMEM((NDEV, CK, CONV_SHARD), F32), pltpu.VMEM((NDEV, 8, D), F32),
                        pltpu.VMEM((NDEV, 8, ADA_SHARD), F32), pltpu.VMEM((S, D), BF), pltpu.VMEM((S, HALF), F32),
                        pltpu.VMEM((TS, INW), F32), pltpu.VMEM((INW, D), BF), pltpu.VMEM((D, D), BF)]
        + [sem] * 10 + [pltpu.SemaphoreType.DMA((2,))],
        compiler_params=_params(True),
    )(w_in_t, w_out_s, conv_w_s, c, w_ada_s, b_ada, x2, norm_w, qw_t, kw_t, cos_t, sin_t, bq, bk)


QB = 4
NQB = NB // QB


def _band_mask(has_prev):
    kj = lax.broadcasted_iota(jnp.int32, (2 * BLK, 4 * BLK), 0)
    qi = lax.broadcasted_iota(jnp.int32, (2 * BLK, 4 * BLK), 1) & (BLK - 1)
    dist = qi + BLK - kj
    local = (dist >= 0) & (dist < BLK)
    return local if has_prev is True else local & ((kj >= BLK) | has_prev)


def _key_blocks(sb, prev_ref, cur_ref):
    prev = prev_ref[...] if sb == 0 else cur_ref[BLK * (sb - 1):BLK * sb, :]
    return prev, cur_ref[BLK * sb:BLK * (sb + 1), :]


def _sink_lanes(sink_ref, g):
    lane = lax.broadcasted_iota(jnp.int32, (1, 4 * BLK), 1)
    return jnp.where(lane < BLK, sink_ref[0, 4 * g],
                     jnp.where(lane < 2 * BLK, sink_ref[0, 4 * g + 1],
                               jnp.where(lane < 3 * BLK, sink_ref[0, 4 * g + 2], sink_ref[0, 4 * g + 3])))


def _unstack_t(t):
    return [t[:, BLK * h:BLK * (h + 1)].T for h in range(4)]


def _stack_heads(t, g):
    return jnp.concatenate([t[:, HD * (4 * g + h):HD * (4 * g + h + 1)] for h in range(4)], axis=0)


def _band(prev, cur, g):
    return jnp.concatenate([prev[:, HD * g:HD * (g + 1)], cur[:, HD * g:HD * (g + 1)]], axis=0)


def _softmax_band(qs, kb, mask, sink):
    s = lax.dot_general(kb, qs, (((1,), (1,)), ((), ())), preferred_element_type=F32) * (HD ** -0.5)
    s = jnp.where(mask, s, NEG)
    m = jnp.maximum(jnp.max(s, axis=0, keepdims=True), sink)
    e = jnp.exp(s - m)
    es = jnp.exp(sink - m)
    inv = 1.0 / (jnp.sum(e, axis=0, keepdims=True) + es)
    return e * inv, es * inv


def _attn_fwd_call(sinks, qr, kr, vb):
    def body(sink_ref, q_ref, kp_ref, kc_ref, vp_ref, vc_ref, o_ref):
        i = pl.program_id(0)
        for sb in range(QB):
            rows = slice(BLK * sb, BLK * (sb + 1))
            mask = _band_mask(i > 0 if sb == 0 else True)
            q = q_ref[rows, :]
            kp, kc = _key_blocks(sb, kp_ref, kc_ref)
            vp, vc = _key_blocks(sb, vp_ref, vc_ref)
            for g in range(NKV):
                p, _ = _softmax_band(_stack_heads(q, g), _band(kp, kc, g), mask, _sink_lanes(sink_ref, g))
                o_t = lax.dot_general(_band(vp, vc, g), p.astype(BF), (((0,), (0,)), ((), ())),
                                      preferred_element_type=F32)
                for h, o_h in enumerate(_unstack_t(o_t)):
                    o_ref[rows, HD * (4 * g + h):HD * (4 * g + h + 1)] = o_h

    prev = lambda i: (jnp.maximum(QB * i - 1, 0), 0)
    return pl.pallas_call(
        body, name="attn_fwd", grid=(NQB,),
        out_shape=jax.ShapeDtypeStruct((S, AW), F32),
        in_specs=[_SMEM, pl.BlockSpec((QB * BLK, AW), _row), pl.BlockSpec((BLK, KVW), prev),
                  pl.BlockSpec((QB * BLK, KVW), _row), pl.BlockSpec((BLK, KVW), prev),
                  pl.BlockSpec((QB * BLK, KVW), _row)],
        out_specs=pl.BlockSpec((QB * BLK, AW), _row),
        compiler_params=_params(True),
    )(sinks, qr, kr, kr, vb, vb)


def _attn_bwd_call(sinks, qr, kr, vb, d_ya, ga, o, qraw, qw_t, cos_t, sin_t, bq):
    def body(sink_ref, q_ref, kp_ref, kc_ref, vp_ref, vc_ref, dya_ref, ga_ref, o_ref, qraw_ref, qw_ref,
             cos_ref, sin_ref, bq_ref,
             dqraw_ref, dga_ref, dk_ref, dv_ref, gqw_ref, gsink_ref):
        i = pl.program_id(0)

        @pl.when(i == 0)
        def _():
            dk_ref[...] = jnp.zeros((S, KVW), F32)
            dv_ref[...] = jnp.zeros((S, KVW), F32)
            gqw_ref[...] = jnp.zeros((1, AW), F32)
            gsink_ref[...] = jnp.zeros((1, 128), F32)

        d_ya = dya_ref[...]
        act, dact = _silu_and_grad(ga_ref[...])
        dga_ref[...] = (d_ya * o_ref[...] * dact).astype(BF)
        d_o_all = (d_ya * act).astype(BF)
        lane = lax.broadcasted_iota(jnp.int32, (1, 128), 1)
        gsink = jnp.zeros((1, 128), F32)
        dq_rows = []
        for sb in range(QB):
            rows = slice(BLK * sb, BLK * (sb + 1))
            mask = _band_mask(i > 0 if sb == 0 else True)
            q = q_ref[rows, :]
            d_o = d_o_all[rows, :]
            kp, kc = _key_blocks(sb, kp_ref, kc_ref)
            vp, vc = _key_blocks(sb, vp_ref, vc_ref)
            dq_parts, dk_parts, dv_parts = [], [], []
            for g in range(NKV):
                qs = _stack_heads(q, g)
                kb = _band(kp, kc, g)
                vbd = _band(vp, vc, g)
                p, ps = _softmax_band(qs, kb, mask, _sink_lanes(sink_ref, g))
                dos = _stack_heads(d_o, g)
                dp = lax.dot_general(vbd, dos, (((1,), (1,)), ((), ())), preferred_element_type=F32)
                dr = jnp.sum(p * dp, axis=0, keepdims=True)
                ds = (p * (dp - dr) * (HD ** -0.5)).astype(BF)
                sink_term = ps * dr
                for h in range(4):
                    part = jnp.sum(sink_term[:, BLK * h:BLK * (h + 1)], axis=1, keepdims=True)
                    gsink = gsink - jnp.where(lane == 4 * g + h, part, 0.0)
                dv_parts.append(jnp.dot(p.astype(BF), dos, preferred_element_type=F32))
                dk_parts.append(jnp.dot(ds, qs, preferred_element_type=F32))
                dq_t = lax.dot_general(kb, ds, (((0,), (0,)), ((), ())), preferred_element_type=F32)
                dq_parts.extend(_unstack_t(dq_t))
            dkb = jnp.concatenate(dk_parts, axis=1)
            dvb = jnp.concatenate(dv_parts, axis=1)
            blk = QB * i + sb
            r_prev = pl.multiple_of(jnp.maximum(blk - 1, 0) * BLK, BLK)
            r_cur = pl.multiple_of(blk * BLK, BLK)
            dk_ref[pl.ds(r_prev, BLK), :] += dkb[0:BLK]
            dv_ref[pl.ds(r_prev, BLK), :] += dvb[0:BLK]
            dk_ref[pl.ds(r_cur, BLK), :] += dkb[BLK:2 * BLK]
            dv_ref[pl.ds(r_cur, BLK), :] += dvb[BLK:2 * BLK]
            dq_rows.append(jnp.concatenate(dq_parts, axis=1))
        gsink_ref[...] += gsink
        dq = jnp.concatenate(dq_rows, axis=0)
        dq_raw, g_qw = _norm_rope_bwd(dq, qraw_ref[...], qw_ref[...], jnp.tile(cos_ref[...], (1, 4)),
                                      jnp.tile(sin_ref[...], (1, 4)), bq_ref)
        dqraw_ref[...] = dq_raw.astype(BF)
        gqw_ref[...] += g_qw

    prev = lambda i: (jnp.maximum(QB * i - 1, 0), 0)
    b512 = pl.BlockSpec((QB * BLK, AW), _row)
    b128 = pl.BlockSpec((QB * BLK, 128), _row)
    return pl.pallas_call(
        body, name="attn_bwd", grid=(NQB,),
        out_shape=[jax.ShapeDtypeStruct((S, AW), BF), jax.ShapeDtypeStruct((S, AW), BF),
                   jax.ShapeDtypeStruct((S, KVW), F32), jax.ShapeDtypeStruct((S, KVW), F32),
                   jax.ShapeDtypeStruct((1, AW), F32), jax.ShapeDtypeStruct((1, 128), F32)],
        in_specs=[_SMEM, b512, pl.BlockSpec((BLK, KVW), prev), b128, pl.BlockSpec((BLK, KVW), prev), b128,
                  b512, b512, b512, b512, pl.BlockSpec((1, AW), _const), b128, b128, pl.BlockSpec((AW, AW), _const)],
        out_specs=[b512, b512, _VMEM, _VMEM, _VMEM, _VMEM],
        compiler_params=_params(True),
    )(sinks, qr, kr, kr, vb, vb, d_ya, ga, o, qraw, qw_t, cos_t, sin_t, bq)


HALO = 32


RC = 64
LC = 128


def _windows(ext_ref, r0, l0, base):
    col = ext_ref[r0:r0 + RC + HALO, l0:l0 + LC]
    for s in range(8):
        rolled = col if s == 0 else pltpu.roll(col, RC + HALO - s, 0)
        for t in range(CK):
            if (base + t) % 8 == s:
                a8 = base + t - s
                yield t, rolled[a8:a8 + RC]


def _taps(ext_ref, r0, l0, base, weight_row):
    acc = None
    for t, win in _windows(ext_ref, r0, l0, base):
        term = win * weight_row(t)[:, l0:l0 + LC]
        acc = term if acc is None else acc + term
    return acc


def _conv_fwd_call(z, gb, cwf, conv_b, ln_w, ln_b):
    def body(zc_in_ref, zp_ref, gb_ref, cw_ref, cb_ref, lw_ref, lb_ref, zc_ref, yb_ref, zext):
        i = pl.program_id(0)
        zext[0:HALO, :] = jnp.where(i > 0, zp_ref[TS - HALO:TS, :], 0.0)
        zext[HALO:HALO + TS, :] = zc_in_ref[...]
        for r0 in range(0, TS, RC):
            for l0 in range(0, CW, LC):
                acc = _taps(zext, r0, l0, HALO - (CK - 1), lambda k: cw_ref[k:k + 1, :])
                zc_ref[r0:r0 + RC, l0:l0 + LC] = acc + cb_ref[:, l0:l0 + LC]
        zc = zc_ref[...]
        mu = jnp.mean(zc, axis=-1, keepdims=True)
        dz = zc - mu
        rstd = lax.rsqrt(jnp.mean(dz * dz, axis=-1, keepdims=True) + EPS)
        zn = dz * rstd * lw_ref[...] + lb_ref[...]
        gbv = gb_ref[...]
        yb_ref[...] = (zn * _sigmoid(zn)) * (gbv * _sigmoid(gbv))

    t512 = pl.BlockSpec((TS, CW), _row)
    prev = pl.BlockSpec((TS, CW), lambda i: (jnp.maximum(i - 1, 0), 0))
    c512 = pl.BlockSpec((1, CW), _const)
    return pl.pallas_call(
        body, name="conv_fwd", grid=(NT,),
        out_shape=[jax.ShapeDtypeStruct((S, CW), F32), jax.ShapeDtypeStruct((S, CW), F32)],
        in_specs=[t512, prev, t512, pl.BlockSpec((CKP, CW), _const), c512, c512, c512],
        out_specs=[t512, t512],
        scratch_shapes=[pltpu.VMEM((TS + HALO, CW), F32)],
        compiler_params=_params(True),
    )(z, z, gb, cwf, conv_b, ln_w, ln_b)


FR = QB * BLK


def _attn_conv_fwd_call(sinks, qr, kr, vb, z, gb, cwf, conv_b, ln_w, ln_b):
    def body(sink_ref, q_ref, kp_ref, kc_ref, vp_ref, vc_ref, z_ref, zh_ref, gb_ref, cw_ref, cb_ref, lw_ref, lb_ref,
             o_ref, zc_ref, yb_ref, zext):
        i = pl.program_id(0)
        zext[0:HALO, :] = jnp.where(i > 0, zh_ref[...], 0.0)
        zext[HALO:HALO + FR, :] = z_ref[...]
        for sb in range(QB):
            rows = slice(BLK * sb, BLK * (sb + 1))
            mask = _band_mask(i > 0 if sb == 0 else True)
            q = q_ref[rows, :]
            kp, kc = _key_blocks(sb, kp_ref, kc_ref)
            vp, vc = _key_blocks(sb, vp_ref, vc_ref)
            for g in range(NKV):
                p, _ = _softmax_band(_stack_heads(q, g), _band(kp, kc, g), mask, _sink_lanes(sink_ref, g))
                o_t = lax.dot_general(_band(vp, vc, g), p.astype(BF), (((0,), (0,)), ((), ())),
                                      preferred_element_type=F32)
                for h, o_h in enumerate(_unstack_t(o_t)):
                    o_ref[rows, HD * (4 * g + h):HD * (4 * g + h + 1)] = o_h
            for r0 in range(BLK * sb, BLK * (sb + 1), RC):
                for l0 in range(0, CW, LC):
                    acc = _taps(zext, r0, l0, HALO - (CK - 1), lambda k: cw_ref[k:k + 1, :])
                    zc_ref[r0:r0 + RC, l0:l0 + LC] = acc + cb_ref[:, l0:l0 + LC]
            zc = zc_ref[rows, :]
            mu = jnp.mean(zc, axis=-1, keepdims=True)
            dz = zc - mu
            rstd = lax.rsqrt(jnp.mean(dz * dz, axis=-1, keepdims=True) + EPS)
            zn = dz * rstd * lw_ref[...] + lb_ref[...]
            gbv = gb_ref[rows, :]
            yb_ref[rows, :] = (zn * _sigmoid(zn)) * (gbv * _sigmoid(gbv))

    prev = lambda i: (jnp.maximum(QB * i - 1, 0), 0)
    halo = lambda i: (jnp.maximum(FR // HALO * i - 1, 0), 0)
    f512 = pl.BlockSpec((FR, 512), _row)
    f128 = pl.BlockSpec((FR, KVW), _row)
    c512 = pl.BlockSpec((1, CW), _const)
    return pl.pallas_call(
        body, name="attn_conv_fwd", grid=(NQB,),
        out_shape=[jax.ShapeDtypeStruct((S, AW), F32), jax.ShapeDtypeStruct((S, CW), F32),
                   jax.ShapeDtypeStruct((S, CW), F32)],
        in_specs=[_SMEM, f512, pl.BlockSpec((BLK, KVW), prev), f128, pl.BlockSpec((BLK, KVW), prev), f128,
                  f512, pl.BlockSpec((HALO, CW), halo), f512, pl.BlockSpec((CKP, CW), _const), c512, c512, c512],
        out_specs=[f512, f512, f512],
        scratch_shapes=[pltpu.VMEM((FR + HALO, CW), F32)],
        compiler_params=_params(True),
    )(sinks, qr, kr, kr, vb, vb, z, z, gb, cwf, conv_b, ln_w, ln_b)


def _conv_bwd_ln_call(d_yb, zc, gb, ln_w, ln_b):
    def body(dyb_ref, zc_ref, gb_ref, lw_ref, lb_ref, dzc_ref, dgb_ref, glw_ref, glb_ref, gcb_ref):
        i = pl.program_id(0)

        @pl.when(i == 0)
        def _():
            glw_ref[...] = jnp.zeros((1, CW), F32)
            glb_ref[...] = jnp.zeros((1, CW), F32)
            gcb_ref[...] = jnp.zeros((1, CW), F32)

        zc = zc_ref[...]
        mu = jnp.mean(zc, axis=-1, keepdims=True)
        dz = zc - mu
        rstd = lax.rsqrt(jnp.mean(dz * dz, axis=-1, keepdims=True) + EPS)
        zh = dz * rstd
        lw = lw_ref[...]
        zn = zh * lw + lb_ref[...]
        d_yb = dyb_ref[...]
        act_n, dact_n = _silu_and_grad(zn)
        act_g, dact_g = _silu_and_grad(gb_ref[...])
        dgb_ref[...] = (d_yb * act_n * dact_g).astype(BF)
        d_zn = d_yb * act_g * dact_n
        glw_ref[...] += jnp.sum(d_zn * zh, axis=0, keepdims=True)
        glb_ref[...] += jnp.sum(d_zn, axis=0, keepdims=True)
        dzh = d_zn * lw
        d_zc = rstd * (dzh - jnp.mean(dzh, axis=-1, keepdims=True) - zh * jnp.mean(dzh * zh, axis=-1, keepdims=True))
        dzc_ref[...] = d_zc
        gcb_ref[...] += jnp.sum(d_zc, axis=0, keepdims=True)

    t512 = pl.BlockSpec((TS, CW), _row)
    c512 = pl.BlockSpec((1, CW), _const)
    vec = jax.ShapeDtypeStruct((1, CW), F32)
    return pl.pallas_call(
        body, name="conv_bwd_ln", grid=(NT,),
        out_shape=[jax.ShapeDtypeStruct((S, CW), F32), jax.ShapeDtypeStruct((S, CW), BF), vec, vec, vec],
        in_specs=[t512, t512, t512, c512, c512],
        out_specs=[t512, t512, _VMEM, _VMEM, _VMEM],
        compiler_params=_params(True),
    )(d_yb, zc, gb, ln_w, ln_b)


def _conv_bwd_taps_call(d_zc, z, a, g, cwf):
    def body(dc_ref, dn_ref, zc_ref, zp_ref, a_ref, g_ref, cw_ref, da_ref, dg_ref, gcw_ref, dext, zext, gacc):
        i = pl.program_id(0)

        @pl.when(i == 0)
        def _():
            gacc[...] = jnp.zeros((CKP * 8, CW), F32)

        dext[0:TS, :] = dc_ref[...]
        dext[TS:TS + HALO, :] = jnp.where(i < NT - 1, dn_ref[0:HALO, :], 0.0)
        zext[0:HALO, :] = jnp.where(i > 0, zp_ref[TS - HALO:TS, :], 0.0)
        zext[HALO:HALO + TS, :] = zc_ref[...]
        for r0 in range(0, TS, RC):
            for l0 in range(0, CW, LC):
                d_z = _taps(dext, r0, l0, 0, lambda j: cw_ref[CK - 1 - j:CK - j, :])
                sg = _sigmoid(g_ref[r0:r0 + RC, l0:l0 + LC])
                da_ref[r0:r0 + RC, l0:l0 + LC] = (d_z * sg).astype(BF)
                dg_ref[r0:r0 + RC, l0:l0 + LC] = (d_z * a_ref[r0:r0 + RC, l0:l0 + LC] * sg * (1.0 - sg)).astype(BF)
                d_sub = dc_ref[r0:r0 + RC, l0:l0 + LC]
                for k, win in _windows(zext, r0, l0, HALO - (CK - 1)):
                    prod = d_sub * win
                    part = prod[0:8]
                    for q in range(1, RC // 8):
                        part = part + prod[8 * q:8 * q + 8]
                    gacc[8 * k:8 * k + 8, l0:l0 + LC] += part

        @pl.when(i == NT - 1)
        def _():
            for k in range(CK):
                gcw_ref[k:k + 1, :] = jnp.sum(gacc[8 * k:8 * k + 8, :], axis=0, keepdims=True)
            gcw_ref[CK:CKP, :] = jnp.zeros((CKP - CK, CW), F32)

    t512 = pl.BlockSpec((TS, CW), _row)
    prev = pl.BlockSpec((TS, CW), lambda i: (jnp.maximum(i - 1, 0), 0))
    nxt = pl.BlockSpec((TS, CW), lambda i: (jnp.minimum(i + 1, NT - 1), 0))
    return pl.pallas_call(
        body, name="conv_bwd_taps", grid=(NT,),
        out_shape=[jax.ShapeDtypeStruct((S, CW), BF), jax.ShapeDtypeStruct((S, CW), BF),
                   jax.ShapeDtypeStruct((CKP, CW), F32)],
        in_specs=[t512, nxt, t512, prev, t512, t512, pl.BlockSpec((CKP, CW), _const)],
        out_specs=[t512, t512, _VMEM],
        scratch_shapes=[pltpu.VMEM((TS + HALO, CW), F32), pltpu.VMEM((TS + HALO, CW), F32),
                        pltpu.VMEM((CKP * 8, CW), F32)],
        compiler_params=_params(True),
    )(d_zc, d_zc, z, z, a, g, cwf)


def _ln_gate_bwd(d_yb, zc, gbv, lw, lb):
    mu = jnp.mean(zc, axis=-1, keepdims=True)
    dz = zc - mu
    rstd = lax.rsqrt(jnp.mean(dz * dz, axis=-1, keepdims=True) + EPS)
    zh = dz * rstd
    zn = zh * lw + lb
    act_n, dact_n = _silu_and_grad(zn)
    act_g, dact_g = _silu_and_grad(gbv)
    d_gb = d_yb * act_n * dact_g
    d_zn = d_yb * act_g * dact_n
    dzh = d_zn * lw
    d_zc = rstd * (dzh - jnp.mean(dzh, axis=-1, keepdims=True) - zh * jnp.mean(dzh * zh, axis=-1, keepdims=True))
    return d_zc, d_gb, d_zn, zh


def _attn_conv_bwd_call(sinks, qr, kr, vb, d_ya, ga, o, qraw, qw_t, cos_t, sin_t, bq, d_yb, zc, gb, z, a, g, cwf,
                        ln_w, ln_b):
    def body(sink_ref, q_ref, kp_ref, kc_ref, vp_ref, vc_ref, dya_ref, ga_ref, o_ref, qraw_ref, qw_ref,
             cos_ref, sin_ref, bq_ref,
             dyb_ref, dybn_ref, zc_ref, zcn_ref, gb_ref, gbn_ref, z_ref, zh_ref, a_ref, g_ref, cw_ref, lw_ref, lb_ref,
             dqraw_ref, dga_ref, dk_ref, dv_ref, gqw_ref, gsink_ref,
             dgb_ref, da_ref, dg_ref, gcw_ref, glw_ref, glb_ref, gcb_ref,
             dext, zext, gacc):
        i = pl.program_id(0)

        @pl.when(i == 0)
        def _():
            dk_ref[...] = jnp.zeros((S, KVW), F32)
            dv_ref[...] = jnp.zeros((S, KVW), F32)
            gqw_ref[...] = jnp.zeros((1, AW), F32)
            gsink_ref[...] = jnp.zeros((1, 128), F32)
            gacc[...] = jnp.zeros((CKP * 8, CW), F32)
            glw_ref[...] = jnp.zeros((1, CW), F32)
            glb_ref[...] = jnp.zeros((1, CW), F32)
            gcb_ref[...] = jnp.zeros((1, CW), F32)

        lw = lw_ref[...]
        lb = lb_ref[...]

        def ln_rows(rows):
            d_zc, d_gb, d_zn, zh = _ln_gate_bwd(dyb_ref[rows, :], zc_ref[rows, :], gb_ref[rows, :], lw, lb)
            dgb_ref[rows, :] = d_gb.astype(BF)
            glw_ref[...] += jnp.sum(d_zn * zh, axis=0, keepdims=True)
            glb_ref[...] += jnp.sum(d_zn, axis=0, keepdims=True)
            gcb_ref[...] += jnp.sum(d_zc, axis=0, keepdims=True)
            dext[rows, :] = d_zc

        ln_rows(slice(0, BLK))
        zext[0:HALO, :] = jnp.where(i > 0, zh_ref[...], 0.0)
        zext[HALO:HALO + FR, :] = z_ref[...]

        d_ya = dya_ref[...]
        act, dact = _silu_and_grad(ga_ref[...])
        dga_ref[...] = (d_ya * o_ref[...] * dact).astype(BF)
        d_o_all = (d_ya * act).astype(BF)
        lane = lax.broadcasted_iota(jnp.int32, (1, 128), 1)
        gsink = jnp.zeros((1, 128), F32)
        dq_rows = []
        for sb in range(QB):
            rows = slice(BLK * sb, BLK * (sb + 1))
            if sb + 1 < QB:
                ln_rows(slice(BLK * (sb + 1), BLK * (sb + 2)))
            else:
                d_zc_next, _, _, _ = _ln_gate_bwd(dybn_ref[...], zcn_ref[...], gbn_ref[...], lw, lb)
                dext[FR:FR + HALO, :] = jnp.where(i < NQB - 1, d_zc_next, 0.0)
            mask = _band_mask(i > 0 if sb == 0 else True)
            q = q_ref[rows, :]
            d_o = d_o_all[rows, :]
            kp, kc = _key_blocks(sb, kp_ref, kc_ref)
            vp, vc = _key_blocks(sb, vp_ref, vc_ref)
            dq_parts, dk_parts, dv_parts = [], [], []
            for gi in range(NKV):
                qs = _stack_heads(q, gi)
                kb = _band(kp, kc, gi)
                vbd = _band(vp, vc, gi)
                p, ps = _softmax_band(qs, kb, mask, _sink_lanes(sink_ref, gi))
                dos = _stack_heads(d_o, gi)
                dp = lax.dot_general(vbd, dos, (((1,), (1,)), ((), ())), preferred_element_type=F32)
                dr = jnp.sum(p * dp, axis=0, keepdims=True)
                ds = (p * (dp - dr) * (HD ** -0.5)).astype(BF)
                sink_term = ps * dr
                for h in range(4):
                    part = jnp.sum(sink_term[:, BLK * h:BLK * (h + 1)], axis=1, keepdims=True)
                    gsink = gsink - jnp.where(lane == 4 * gi + h, part, 0.0)
                dv_parts.append(jnp.dot(p.astype(BF), dos, preferred_element_type=F32))
                dk_parts.append(jnp.dot(ds, qs, preferred_element_type=F32))
                dq_t = lax.dot_general(kb, ds, (((0,), (0,)), ((), ())), preferred_element_type=F32)
                dq_parts.extend(_unstack_t(dq_t))
            dkb = jnp.concatenate(dk_parts, axis=1)
            dvb = jnp.concatenate(dv_parts, axis=1)
            blk = QB * i + sb
            r_prev = pl.multiple_of(jnp.maximum(blk - 1, 0) * BLK, BLK)
            r_cur = pl.multiple_of(blk * BLK, BLK)
            dk_ref[pl.ds(r_prev, BLK), :] += dkb[0:BLK]
            dv_ref[pl.ds(r_prev, BLK), :] += dvb[0:BLK]
            dk_ref[pl.ds(r_cur, BLK), :] += dkb[BLK:2 * BLK]
            dv_ref[pl.ds(r_cur, BLK), :] += dvb[BLK:2 * BLK]
            dq_rows.append(jnp.concatenate(dq_parts, axis=1))

            for r0 in range(BLK * sb, BLK * (sb + 1), RC):
                for l0 in range(0, CW, LC):
                    d_z = _taps(dext, r0, l0, 0, lambda j: cw_ref[CK - 1 - j:CK - j, :])
                    sg = _sigmoid(g_ref[r0:r0 + RC, l0:l0 + LC])
                    da_ref[r0:r0 + RC, l0:l0 + LC] = (d_z * sg).astype(BF)
                    dg_ref[r0:r0 + RC, l0:l0 + LC] = (d_z * a_ref[r0:r0 + RC, l0:l0 + LC] * sg
                                                      * (1.0 - sg)).astype(BF)
                    d_sub = dext[r0:r0 + RC, l0:l0 + LC]
                    for k, win in _windows(zext, r0, l0, HALO - (CK - 1)):
                        prod = d_sub * win
                        part = prod[0:8]
                        for q8 in range(1, RC // 8):
                            part = part + prod[8 * q8:8 * q8 + 8]
                        gacc[8 * k:8 * k + 8, l0:l0 + LC] += part
        gsink_ref[...] += gsink
        dq = jnp.concatenate(dq_rows, axis=0)
        dq_raw, g_qw = _norm_rope_bwd(dq, qraw_ref[...], qw_ref[...], jnp.tile(cos_ref[...], (1, 4)),
                                      jnp.tile(sin_ref[...], (1, 4)), bq_ref)
        dqraw_ref[...] = dq_raw.astype(BF)
        gqw_ref[...] += g_qw

        @pl.when(i == NQB - 1)
        def _():
            for k in range(CK):
                gcw_ref[k:k + 1, :] = jnp.sum(gacc[8 * k:8 * k + 8, :], axis=0, keepdims=True)
            gcw_ref[CK:CKP, :] = jnp.zeros((CKP - CK, CW), F32)

    prev = lambda i: (jnp.maximum(QB * i - 1, 0), 0)
    halo_prev = lambda i: (jnp.maximum(FR // HALO * i - 1, 0), 0)
    halo_next = lambda i: (jnp.minimum(FR // HALO * (i + 1), S // HALO - 1), 0)
    f512 = pl.BlockSpec((FR, 512), _row)
    f128 = pl.BlockSpec((FR, 128), _row)
    hn = pl.BlockSpec((HALO, CW), halo_next)
    c512 = pl.BlockSpec((1, CW), _const)
    sds = jax.ShapeDtypeStruct
    vec = sds((1, CW), F32)
    return pl.pallas_call(
        body, name="attn_conv_bwd", grid=(NQB,),
        out_shape=[sds((S, AW), BF), sds((S, AW), BF), sds((S, KVW), F32), sds((S, KVW), F32), sds((1, AW), F32),
                   sds((1, 128), F32),
                   sds((S, CW), BF), sds((S, CW), BF), sds((S, CW), BF), sds((CKP, CW), F32), vec, vec, vec],
        in_specs=[_SMEM, f512, pl.BlockSpec((BLK, KVW), prev), f128, pl.BlockSpec((BLK, KVW), prev), f128,
                  f512, f512, f512, f512, pl.BlockSpec((1, AW), _const), f128, f128, pl.BlockSpec((AW, AW), _const),
                  f512, hn, f512, hn, f512, hn, f512, pl.BlockSpec((HALO, CW), halo_prev), f512, f512,
                  pl.BlockSpec((CKP, CW), _const), c512, c512],
        out_specs=[f512, f512, _VMEM, _VMEM, _VMEM, _VMEM, f512, f512, f512, _VMEM, _VMEM, _VMEM, _VMEM],
        scratch_shapes=[pltpu.VMEM((FR + HALO, CW), F32), pltpu.VMEM((FR + HALO, CW), F32),
                        pltpu.VMEM((CKP * 8, CW), F32)],
        compiler_params=_params(True),
    )(sinks, qr, kr, kr, vb, vb, d_ya, ga, o, qraw, qw_t, cos_t, sin_t, bq,
      d_yb, d_yb, zc, zc, gb, gb, z, z, a, g, cwf, ln_w, ln_b)


def _out_loss_call(o, ga, yb, x2, tgt, mod, w_out_full):
    def body(o_ref, ga_ref, yb_ref, x_ref, t_ref, mod_ref, w_ref,
             dout_ref, dya_ref, dyb_ref, gw_ref, loss_ref, dgate_ref):
        i = pl.program_id(0)

        @pl.when(i == 0)
        def _():
            gw_ref[...] = jnp.zeros((D, D), F32)
            loss_ref[...] = jnp.zeros((1, 128), F32)
            dgate_ref[...] = jnp.zeros((1, D), F32)

        gav = ga_ref[...]
        ya = o_ref[...] * (gav * _sigmoid(gav))
        ycat = jnp.concatenate([ya, yb_ref[...]], axis=1).astype(BF)
        w = w_ref[...]
        y = jnp.dot(ycat, w, preferred_element_type=F32)
        gate = mod_ref[:, 2 * D:3 * D]
        diff = x_ref[...] + gate * y - t_ref[...]
        sq = jnp.sum(jnp.sum(diff * diff, axis=1, keepdims=True), axis=0, keepdims=True)
        loss_ref[...] += jnp.broadcast_to(sq, (1, 128))
        d_out = diff * (1.0 / D)
        dout_ref[...] = d_out
        dgate_ref[...] += jnp.sum(d_out * y, axis=0, keepdims=True)
        dy = (d_out * gate).astype(BF)
        d_ycat = lax.dot_general(dy, w, (((1,), (1,)), ((), ())), preferred_element_type=F32)
        dya_ref[...] = d_ycat[:, 0:AW]
        dyb_ref[...] = d_ycat[:, AW:D]
        gw_ref[...] += lax.dot_general(ycat, dy, (((0,), (0,)), ((), ())), preferred_element_type=F32)

    t512 = pl.BlockSpec((TS, 512), _row)
    t1024 = pl.BlockSpec((TS, D), _row)
    return pl.pallas_call(
        body, name="out_loss", grid=(NT,),
        out_shape=[jax.ShapeDtypeStruct((S, D), F32), jax.ShapeDtypeStruct((S, AW), F32),
                   jax.ShapeDtypeStruct((S, CW), F32), jax.ShapeDtypeStruct((D, D), F32),
                   jax.ShapeDtypeStruct((1, 128), F32), jax.ShapeDtypeStruct((1, D), F32)],
        in_specs=[t512, t512, t512, t1024, t1024, pl.BlockSpec((1, 3 * D), _const),
                  pl.BlockSpec((D, D), _const, pipeline_mode=pl.Buffered(1))],
        out_specs=[t1024, t512, t512, _VMEM, _VMEM, _VMEM],
        compiler_params=_params(True),
    )(o, ga, yb, x2, tgt, mod, w_out_full)


SM_ROWS = 8
PIECES = ((0, 512), (512, 640), (640, 768), (768, 1280), (1280, 1792), (1792, 2304), (2304, 2816))


def _bwd_in_call(dqraw, dk, dv, dga, da, dg, dgb, kraw, kw_t, cos_t, sin_t, bk, h, wt_full, x2, d_out, mod, norm_w,
                 gw_out, gcw, glw, glb, gcb, gqw, gsink, dgate, loss_p, cact_all):
    def body(dq_ref, dk_ref, dv_ref, dga_ref, da_ref, dg_ref, dgb_ref, kraw_ref, kw_ref, cos_ref, sin_ref, bk_ref,
             h_ref, wt_ref, x_ref, dout_ref, mod_ref, nw_ref, gwout_ref, gcw_ref, glw_ref, glb_ref, gcb_ref, gqw_ref,
             gsink_ref, dgate_ref, loss_ref, cact_ref,
             gx_ref, o_gwin, o_gwout, o_gwada, o_gbada, o_gnw, o_gqw, o_gkw, o_gsink, o_gcw, o_gcb, o_glw, o_glb,
             o_loss,
             acc, win_send, win_sib, win_ici, wout_send, wout_sib, wout_ici, sm_buf, cw_buf, dmod_all, vec_acc, gkw_acc,
             wi_ds, wi_dr, wi_is, wi_ir, wo_ds, wo_dr, wo_is, wo_ir, sm_s, sm_r, cw_s, cw_r):
        i = pl.program_id(0)
        pos = _mesh_pos()
        x, y, cc = pos
        me = 4 * x + 2 * y + cc

        def chip(j):
            return (1 - x if j & 1 else x, 1 - y if j & 2 else y)

        def rows_of(buf, px, py, pc, rows, align):
            return buf.at[pl.ds(pl.multiple_of((4 * px + 2 * py + pc) * rows, align), rows), :]

        bufs = {"in": (win_send, win_sib, win_ici, IN_SHARD, wi_ds, wi_dr, wi_is, wi_ir),
                "out": (wout_send, wout_sib, wout_ici, OUT_SHARD, wo_ds, wo_dr, wo_is, wo_ir)}

        def d2d_copy(j, which):
            send, sib, _, rows, ds_, dr_, _, _ = bufs[which]
            px, py = chip(j)
            return pltpu.make_async_remote_copy(src_ref=rows_of(send, px, py, 1 - cc, rows, 16), dst_ref=sib.at[j],
                                                send_sem=ds_.at[j], recv_sem=dr_.at[j], device_id=(x, y, 1 - cc),
                                                device_id_type=MESH)

        def ici_copy(j, which):
            send, _, ici, rows, _, _, is_, ir_ = bufs[which]
            px, py = chip(j)
            return pltpu.make_async_remote_copy(src_ref=rows_of(send, px, py, cc, rows, 16), dst_ref=ici.at[j - 1],
                                                send_sem=is_.at[j - 1], recv_sem=ir_.at[j - 1], device_id=(px, py, cc),
                                                device_id_type=MESH)

        def level2(which, partial_ref):
            send, sib, _, rows, _, _, _, _ = bufs[which]
            for j in range(1, 4):
                d2d_copy(j, which).wait_recv()
                px, py = chip(j)
                mine = rows_of(partial_ref, px, py, cc, rows, 8)[...]
                rows_of(send, px, py, cc, rows, 16)[...] = (mine + sib[j].astype(F32)).astype(BF)
                ici_copy(j, which).start()

        def finish(which, partial_ref):
            _, sib, ici, rows, _, _, _, _ = bufs[which]
            d2d_copy(0, which).wait_recv()
            total = rows_of(partial_ref, x, y, cc, rows, 8)[...] + sib[0].astype(F32)
            for j in range(1, 4):
                ici_copy(j, which).wait_recv()
                total = total + ici[j - 1].astype(F32)
            for j in range(4):
                d2d_copy(j, which).wait_send()
            for j in range(1, 4):
                ici_copy(j, which).wait_send()
            return total

        def dproj_pieces():
            dk_raw, g_kw = _norm_rope_bwd(dk_ref[...], kraw_ref[...], kw_ref[...], cos_ref[...], sin_ref[...], bk_ref)
            return [dq_ref[...], dk_raw.astype(BF), dv_ref[...].astype(BF), dga_ref[...], da_ref[...], dg_ref[...],
                    dgb_ref[...]], g_kw

        @pl.when(i == 0)
        def _():
            acc[...] = jnp.zeros((INW, D), F32)
            vec_acc[...] = jnp.zeros((8, D), F32)
            gkw_acc[...] = jnp.zeros((1, KVW), F32)
            wout_send[...] = gwout_ref[...].astype(BF)
            for j in range(4):
                d2d_copy(j, "out").start()

        @pl.when(i == 2)
        def _():
            level2("out", gwout_ref)

        @pl.when(i < NT)
        def _():
            pieces, g_kw = dproj_pieces()
            gkw_acc[...] += g_kw
            hv = h_ref[...]
            for (lo, hi), piece in zip(PIECES, pieces):
                acc[lo:hi, :] += lax.dot_general(piece, hv, (((0,), (0,)), ((), ())), preferred_element_type=F32)

        @pl.when(i == NT - 1)
        def _():
            for lo, hi in PIECES:
                win_send[lo:hi, :] = acc[lo:hi, :].astype(BF)
            for j in range(4):
                d2d_copy(j, "in").start()

        @pl.when(i == NT + 2)
        def _():
            level2("in", acc)

        @pl.when(i >= NT)
        def _():
            pieces, _ = dproj_pieces()
            dproj = jnp.concatenate(pieces, axis=1)
            d_h = jnp.dot(dproj, wt_ref[...], preferred_element_type=F32)
            xv = x_ref[...]
            scale = mod_ref[:, D:2 * D]
            nw = nw_ref[...]
            r = lax.rsqrt(jnp.mean(xv * xv, axis=-1, keepdims=True) + EPS)
            xn = xv * r
            vec_acc[0:1, :] += jnp.sum(d_h, axis=0, keepdims=True)
            vec_acc[1:2, :] += jnp.sum(d_h * (xn * nw), axis=0, keepdims=True)
            d_u = d_h * (1.0 + scale)
            vec_acc[2:3, :] += jnp.sum(d_u * xn, axis=0, keepdims=True)
            d_xn = d_u * nw
            gx_ref[...] = dout_ref[...] + r * (d_xn - xn * jnp.mean(d_xn * xn, axis=-1, keepdims=True))

        @pl.when(i == 2 * NT - 1)
        def _():
            ch_sm = (_slab(sm_buf), sm_s, sm_r)
            ch_cw = (_slab(cw_buf), cw_s, cw_r)
            z128 = jnp.zeros((1, 128), F32)
            row4 = jnp.concatenate([glw_ref[...], glb_ref[...]], axis=1)
            row5 = jnp.concatenate([gcb_ref[...], gqw_ref[...]], axis=1)
            row6 = jnp.concatenate([gkw_acc[...], gsink_ref[...], loss_ref[...]] + [z128] * 5, axis=1)
            sm_buf[me] = jnp.concatenate([vec_acc[0:2, :], dgate_ref[...], vec_acc[2:3, :], row4, row5, row6,
                                          jnp.zeros((1, D), F32)], axis=0)
            f_sm = _ag_start(ch_sm, pos)
            cw_buf[me] = gcw_ref[...]
            f_cw = _ag_start(ch_cw, pos)
            _ag_finish(ch_sm, pos, f_sm)
            _ag_finish(ch_cw, pos, f_cw)
            tot = sm_buf[0]
            cw_tot = cw_buf[0]
            for d in range(1, NDEV):
                tot = tot + sm_buf[d]
                cw_tot = cw_tot + cw_buf[d]
            o_gbada[...] = jnp.concatenate([tot[0:1, :], tot[1:2, :], tot[2:3, :]], axis=1)
            o_gnw[...] = tot[3:4, :]
            o_glw[...] = tot[4:5, 0:CW]
            o_glb[...] = tot[4:5, CW:D]
            o_gcb[...] = tot[5:6, 0:CW]
            gq = tot[5:6, CW:CW + HD]
            for hh in range(1, NQ):
                gq = gq + tot[5:6, CW + HD * hh:CW + HD * (hh + 1)]
            o_gqw[...] = gq
            o_gkw[...] = tot[6:7, 0:HD] + tot[6:7, HD:2 * HD]
            o_gsink[...] = tot[6:7, 128:128 + NQ]
            o_loss[...] = tot[6:7, 256:384] * (0.5 / D)
            mine = jnp.zeros((CK, CONV_SHARD), F32)
            for d in range(NDEV):
                mine = mine + jnp.where(me == d, cw_tot[0:CK, CONV_SHARD * d:CONV_SHARD * (d + 1)], 0.0)
            for k in range(CK):
                o_gcw[k] = mine[k:k + 1, :]
            for d in range(NDEV):
                dmod_all[d:d + 1, :] = jnp.concatenate([sm_buf[d, 0:1, :], sm_buf[d, 1:2, :], sm_buf[d, 2:3, :]],
                                                       axis=1)
            col0 = pl.multiple_of(me * ADA_SHARD, 128)
            o_gwada[...] = lax.dot_general(cact_ref[...], dmod_all[:, pl.ds(col0, ADA_SHARD)], (((0,), (0,)), ((), ())),
                                           preferred_element_type=F32, precision=lax.Precision.HIGHEST)

            o_gwout[...] = finish("out", gwout_ref)
            o_gwin[...] = finish("in", acc)

    half = lambda i: (i % NT, 0)
    late = lambda i: (jnp.maximum(i - NT, 0), 0)
    t512 = pl.BlockSpec((TS, 512), half)
    t128 = pl.BlockSpec((TS, 128), half)
    l1024 = pl.BlockSpec((TS, D), late)
    sem7 = pltpu.SemaphoreType.DMA((7,))
    sem4 = pltpu.SemaphoreType.DMA((4,))
    sem3 = pltpu.SemaphoreType.DMA((3,))
    sds = jax.ShapeDtypeStruct
    return pl.pallas_call(
        body, name="bwd_in", grid=(2 * NT,),
        out_shape=[sds((S, D), F32), sds((IN_SHARD, D), F32), sds((OUT_SHARD, D), F32), sds((D, ADA_SHARD), F32),
                   sds((1, 3 * D), F32), sds((1, D), F32), sds((1, HD), F32), sds((1, HD), F32), sds((1, NQ), F32),
                   sds((CK, 1, CONV_SHARD), F32), sds((1, CW), F32), sds((1, CW), F32), sds((1, CW), F32),
                   sds((1, 128), F32)],
        in_specs=[t512, t128, t128, t512, t512, t512, t512, t128, pl.BlockSpec((1, KVW), _const), t128, t128,
                  pl.BlockSpec((KVW, KVW), _const), pl.BlockSpec((TS, D), half),
                  pl.BlockSpec((INW, D), _const, pipeline_mode=pl.Buffered(1)), l1024, l1024,
                  pl.BlockSpec((1, 3 * D), _const), pl.BlockSpec((1, D), _const)] + [_VMEM] * 10,
        out_specs=[l1024] + [_VMEM] * 13,
        scratch_shapes=[pltpu.VMEM((INW, D), F32), pltpu.VMEM((INW, D), BF), pltpu.VMEM((4, IN_SHARD, D), BF),
                        pltpu.VMEM((3, IN_SHARD, D), BF), pltpu.VMEM((D, D), BF), pltpu.VMEM((4, OUT_SHARD, D), BF),
                        pltpu.VMEM((3, OUT_SHARD, D), BF),
                        pltpu.VMEM((NDEV, SM_ROWS, D), F32), pltpu.VMEM((NDEV, CKP, CW), F32),
                        pltpu.VMEM((NDEV, 3 * D), F32), pltpu.VMEM((8, D), F32), pltpu.VMEM((1, KVW), F32)]
        + [sem4, sem4, sem3, sem3] * 2 + [sem7] * 4,
        compiler_params=pltpu.CompilerParams(dimension_semantics=("arbitrary",), vmem_limit_bytes=BIG_VMEM_LIMIT),
    )(dqraw, dk, dv, dga, da, dg, dgb, kraw, kw_t, cos_t, sin_t, bk, h, wt_full, x2, d_out, mod, norm_w,
      gw_out, gcw, glw, glb, gcb, gqw, gsink, dgate, loss_p, cact_all)


ADAM_STEPS = 4


def _adam_call(ws, gs, ms, vs, grad_x, loss_v):
    n = len(ws)
    bc1 = 1.0 - ADAM_B1 ** ADAM_STEP
    bc2 = 1.0 - ADAM_B2 ** ADAM_STEP
    chunked = [w.ndim == 2 and w.shape[0] % (8 * ADAM_STEPS) == 0 for w in ws]

    def body(*refs):
        ins, outs = refs[:4 * n + 2], refs[4 * n + 2:]
        i = pl.program_id(0)

        def update(j):
            w, g, m, v = (ins[j][...], ins[n + j][...], ins[2 * n + j][...], ins[3 * n + j][...])
            m_new = ADAM_B1 * m + (1.0 - ADAM_B1) * g
            v_new = ADAM_B2 * v + (1.0 - ADAM_B2) * (g * g)
            m_hat = m_new / bc1
            v_hat = v_new / bc2
            outs[j][...] = g
            outs[n + j][...] = -ADAM_LR * (m_hat / (jnp.sqrt(v_hat) + ADAM_EPS) + ADAM_WD * w)
            outs[2 * n + j][...] = m_new
            outs[3 * n + j][...] = v_new

        for j in range(n):
            if chunked[j]:
                update(j)
        outs[4 * n][...] = ins[4 * n][...]

        @pl.when(i == 0)
        def _():
            for j in range(n):
                if not chunked[j]:
                    update(j)
            outs[4 * n + 1][...] = ins[4 * n + 1][...]

    def spec(arr, is_chunked):
        if is_chunked:
            return pl.BlockSpec((arr.shape[0] // ADAM_STEPS, arr.shape[1]), _row)
        zeros = (0,) * arr.ndim
        return pl.BlockSpec(arr.shape, lambda i: zeros)

    par_specs = [spec(w, ch) for w, ch in zip(ws, chunked)]
    extra = [spec(grad_x, True), spec(loss_v, False)]
    shapes = [jax.ShapeDtypeStruct(w.shape, F32) for w in ws]
    res = pl.pallas_call(
        body, name="adam", grid=(ADAM_STEPS,),
        out_shape=shapes * 4 + [jax.ShapeDtypeStruct(grad_x.shape, F32), jax.ShapeDtypeStruct(loss_v.shape, F32)],
        in_specs=par_specs * 4 + extra, out_specs=par_specs * 4 + extra,
        compiler_params=_params(True),
    )(*ws, *gs, *ms, *vs, grad_x, loss_v)
    return res[0:n], res[n:2 * n], res[2 * n:3 * n], res[3 * n:4 * n], res[4 * n], res[4 * n + 1]


def _rope_tables():
    inv = (np.float32(ROPE_THETA) ** (-np.arange(0, HD, 2, dtype=np.float32) / np.float32(HD))).astype(np.float32)
    ang = (np.arange(S, dtype=np.float32)[:, None] * inv[None, :]).astype(np.float32)
    cos, sin = np.cos(ang).astype(np.float32), np.sin(ang).astype(np.float32)
    cos64 = np.concatenate([cos, cos], axis=-1)
    sin64 = np.concatenate([-sin, sin], axis=-1)
    return jnp.asarray(np.tile(cos64, (1, 2))), jnp.asarray(np.tile(sin64, (1, 2)))


def _group_matrix(width):
    idx = np.arange(width) // HD
    return jnp.asarray(np.where(idx[:, None] == idx[None, :], 1.0 / HD, 0.0).astype(np.float32)).astype(BF)


def kernel(x, c, w_ada, b_ada, norm_w, w_in, q_norm_w, k_norm_w, sinks, conv_w, conv_b, ln_w, ln_b, w_out, loss_target, m_w_ada, m_b_ada, m_norm_w, m_w_in, m_q_norm_w, m_k_norm_w, m_sinks, m_conv_w, m_conv_b, m_ln_w, m_ln_b, m_w_out, v_w_ada, v_b_ada, v_norm_w, v_w_in, v_q_norm_w, v_k_norm_w, v_sinks, v_conv_w, v_conv_b, v_ln_w, v_ln_b, v_w_out):
    x2 = x[0]
    tgt = loss_target[0]
    cos_t, sin_t = _rope_tables()
    bq = _group_matrix(AW)
    bk = _group_matrix(KVW)
    qw_t = jnp.tile(q_norm_w, (1, NQ))
    kw_t = jnp.tile(k_norm_w, (1, NKV))

    tr = lambda t: jnp.swapaxes(t[0], 0, 1)
    tc = lambda t: jnp.swapaxes(t, 0, 1)
    (wt_full, w_out_full, cwf, mod, cact_all, h, qraw, kraw, ga, a, g, gb, qr, kr, vb, z) = _gather_fwd_call(
        tr(w_in), w_out[0], tc(conv_w), c, w_ada[0], b_ada, x2, norm_w, qw_t, kw_t, cos_t, sin_t, bq, bk)
    o, zc, yb = _attn_conv_fwd_call(sinks, qr, kr, vb, z, gb, cwf, conv_b, ln_w, ln_b)
    d_out, d_ya, d_yb, gw_out, loss_p, dgate = _out_loss_call(o, ga, yb, x2, tgt, mod, w_out_full)

    dqraw, dga, dk, dv, gqw, gsink, dgb, da, dg, gcw, glw, glb, gcb = _attn_conv_bwd_call(
        sinks, qr, kr, vb, d_ya, ga, o, qraw, qw_t, cos_t, sin_t, bq, d_yb, zc, gb, z, a, g, cwf, ln_w, ln_b)
    (grad_x, g_w_in_t, g_w_out, g_w_ada, g_b_ada, g_norm_w, g_qw, g_kw, g_sinks, g_conv_w, g_conv_b, g_ln_w, g_ln_b,
     loss_v) = _bwd_in_call(dqraw, dk, dv, dga, da, dg, dgb, kraw, kw_t, cos_t, sin_t, bk, h, wt_full, x2, d_out, mod,
                            norm_w, gw_out, gcw, glw, glb, gcb, gqw, gsink, dgate, loss_p, cact_all)

    ws = [w_ada[0], b_ada, norm_w, tr(w_in), q_norm_w, k_norm_w, sinks, tc(conv_w), conv_b, ln_w, ln_b, w_out[0]]
    gs = [g_w_ada, g_b_ada, g_norm_w, g_w_in_t, g_qw, g_kw, g_sinks, g_conv_w, g_conv_b, g_ln_w, g_ln_b, g_w_out]
    ms = [m_w_ada[0], m_b_ada, m_norm_w, tr(m_w_in), m_q_norm_w, m_k_norm_w, m_sinks, tc(m_conv_w), m_conv_b, m_ln_w,
          m_ln_b, m_w_out[0]]
    vs = [v_w_ada[0], v_b_ada, v_norm_w, tr(v_w_in), v_q_norm_w, v_k_norm_w, v_sinks, tc(v_conv_w), v_conv_b, v_ln_w,
          v_ln_b, v_w_out[0]]
    grads, deltas, new_m, new_v, grad_x, loss_v = _adam_call(ws, gs, ms, vs, grad_x, loss_v)
    shaped = [w_ada, b_ada, norm_w, w_in, q_norm_w, k_norm_w, sinks, conv_w, conv_b, ln_w, ln_b, w_out]
    W_IN_POS, CONV_W_POS = 3, 7

    def like(vals):
        vals = [jnp.swapaxes(v, 0, 1) if j in (W_IN_POS, CONV_W_POS) else v for j, v in enumerate(vals)]
        return [v.reshape(s.shape) for v, s in zip(vals, shaped)]

    return (loss_v[0, 0], grad_x[None], *like(grads), *like(deltas), *like(new_m), *like(new_v))
```

```python
import functools

import jax
import jax.numpy as jnp
import numpy as np
from jax import lax
from jax.experimental import pallas as pl
from jax.experimental.pallas import tpu as pltpu

S = 2048
D = 1024
NDEV = 8
HD = 64
NQ = 8
NKV = 2
AW = 512
KVW = 128
CW = 512
INW = 2816
IN_SHARD = INW // NDEV
ADA_SHARD = 3 * D // NDEV
OUT_SHARD = D // NDEV
CONV_SHARD = CW // NDEV
CK = 31
CKP = 32
BLK = 128
TS = 256
NT = S // TS
NB = S // BLK
EPS = 1e-6
ROPE_THETA = 10000.0
NEG = -1e30
BF = jnp.bfloat16
F32 = jnp.float32

ADAM_LR = 0.001
ADAM_B1 = 0.9
ADAM_B2 = 0.999
ADAM_EPS = 1e-08
ADAM_WD = 0.01
ADAM_STEP = 10

VMEM_LIMIT = 56 * 1024 * 1024
BIG_VMEM_LIMIT = 62 * 1024 * 1024
MESH = pl.DeviceIdType.MESH

_VMEM = pl.BlockSpec(memory_space=pltpu.VMEM)
_SMEM = pl.BlockSpec(memory_space=pltpu.SMEM)
_ANY = pl.BlockSpec(memory_space=pl.ANY)


def _params(grid=False):
    if grid:
        return pltpu.CompilerParams(dimension_semantics=("arbitrary",), vmem_limit_bytes=VMEM_LIMIT)
    return pltpu.CompilerParams(vmem_limit_bytes=VMEM_LIMIT)


def _row(i):
    return (i, 0)


def _const(i):
    return (0, 0)


def _sigmoid(t):
    return 1.0 / (1.0 + jnp.exp(-t))


def _silu_and_grad(t):
    sg = _sigmoid(t)
    return t * sg, sg * (1.0 + t * (1.0 - sg))


def _group_mean(t, b_ref):
    hi = t.astype(BF)
    lo = (t - hi.astype(F32)).astype(BF)
    b = b_ref[...]
    return jnp.dot(hi, b, preferred_element_type=F32) + jnp.dot(lo, b, preferred_element_type=F32)


def _partner(t):
    w = t.shape[-1]
    lane = lax.broadcasted_iota(jnp.int32, t.shape, 1)
    first = (lane & 32) == 0
    return jnp.where(first, pltpu.roll(t, w - 32, 1), pltpu.roll(t, 32, 1))


def _norm_rope_fwd(t, w_t, cos, sin, b_ref):
    r = lax.rsqrt(_group_mean(t * t, b_ref) + EPS)
    tn = t * r * w_t
    return tn * cos + _partner(tn) * sin


def _norm_rope_bwd(d_out, t, w_t, cos, sin, b_ref):
    d_tn = d_out * cos + _partner(d_out * sin)
    r = lax.rsqrt(_group_mean(t * t, b_ref) + EPS)
    th = t * r
    g_w = jnp.sum(d_tn * th, axis=0, keepdims=True)
    d_th = d_tn * w_t
    d_t = r * (d_th - th * _group_mean(d_th * th, b_ref))
    return d_t, g_w


def _mesh_pos():
    return lax.axis_index("x"), lax.axis_index("y"), lax.axis_index("c")


def _ag_copy(chan, k, block, to):
    blk, send_sems, recv_sems = chan
    ref = blk(*block)
    return pltpu.make_async_remote_copy(src_ref=ref, dst_ref=ref, send_sem=send_sems.at[k],
                                        recv_sem=recv_sems.at[k], device_id=to, device_id_type=MESH)


def _ag_start(chan, pos):
    x, y, c = pos
    me = (x, y, c)
    chips = [(1 - x, y), (x, 1 - y), (1 - x, 1 - y)]
    first = [_ag_copy(chan, 0, me, (x, y, 1 - c))]
    first += [_ag_copy(chan, 1 + j, me, (*chip, c)) for j, chip in enumerate(chips)]
    for cp in first:
        cp.start()
    return first


def _ag_finish(chan, pos, first):
    x, y, c = pos
    me = (x, y, c)
    sibling = (x, y, 1 - c)
    chips = [(1 - x, y), (x, 1 - y), (1 - x, 1 - y)]
    passed = [_ag_copy(chan, 4 + j, (*chip, c), sibling) for j, chip in enumerate(chips)]
    for j, chip in enumerate(chips):
        _ag_copy(chan, 1 + j, (*chip, c), me).wait_recv()
        passed[j].start()
    _ag_copy(chan, 0, sibling, me).wait_recv()
    for j, chip in enumerate(chips):
        _ag_copy(chan, 4 + j, (*chip, 1 - c), me).wait_recv()
    for cp in first + passed:
        cp.wait_send()


def _slab(buf):
    return lambda px, py, pc: buf.at[4 * px + 2 * py + pc]


def _row_block(buf, rows, align):
    return lambda px, py, pc: buf.at[pl.ds(pl.multiple_of((4 * px + 2 * py + pc) * rows, align), rows), :]


HALF = INW // 2


def _gather_fwd_call(w_in_t, w_out_s, conv_w_s, c, w_ada_s, b_ada, x2, norm_w, qw_t, kw_t, cos_t, sin_t, bq, bk):
    def body(win_ref, wout_ref, cw_ref, c_ref, wada_ref, bada_ref, x_ref, nw_ref, qw_ref, kw_ref, cos_ref, sin_ref,
             bq_ref, bk_ref,
             wtf_hbm, woutf_hbm, cwf_ref, mod_ref, cact_ref,
             h_ref, qraw_ref, kraw_ref, ga_ref, a_ref, g_ref, gb_ref, qr_ref, kr_ref, vb_ref, z_ref,
             cw_buf, ca_buf, mp_buf, h_s, raw0, pt, wtf_ref, woutf_ref,
             s0, r0, s1, r1, s2, r2, s3, r3, s4, r4, out_sems):
        s = pl.program_id(0)
        pos = _mesh_pos()
        x, y, cc = pos
        me3 = (x, y, cc)
        me = 4 * x + 2 * y + cc
        sibling = (x, y, 1 - cc)
        chips = [(1 - x, y), (x, 1 - y), (1 - x, 1 - y)]
        ch_win = (_row_block(wtf_ref, IN_SHARD, 16), s0, r0)
        ch_wout = (_row_block(woutf_ref, OUT_SHARD, 16), s1, r1)
        ch_cw = (_slab(cw_buf), s2, r2)
        ch_ca = (_slab(ca_buf), s3, r3)
        ch_mp = (_slab(mp_buf), s4, r4)

        def first(chan, j):
            return _ag_copy(chan, j, me3, sibling if j == 0 else (*chips[j - 1], cc))

        def passed(chan, j):
            return _ag_copy(chan, 4 + j, (*chips[j], cc), sibling)

        def landed(chan, j):
            return _ag_copy(chan, 1 + j, (*chips[j], cc), me3)

        def relayed(chan, j):
            return _ag_copy(chan, 4 + j, (*chips[j], 1 - cc), me3)

        def from_sibling(chan):
            return _ag_copy(chan, 0, sibling, me3)

        @pl.when(s == 0)
        def _():
            cv = c_ref[...]
            ca_buf[me] = jnp.broadcast_to(cv * _sigmoid(cv), (8, D))
            f_ca = _ag_start(ch_ca, pos)
            wtf_ref[pl.ds(pl.multiple_of(me * IN_SHARD, 16), IN_SHARD), :] = win_ref[...].astype(BF)
            for j in range(3):
                first(ch_win, j).start()
            cw_buf[me] = cw_ref[:, 0, :]
            f_cw = _ag_start(ch_cw, pos)

            _ag_finish(ch_ca, pos, f_ca)
            cact_all = jnp.concatenate([ca_buf[d, 0:1, :] for d in range(NDEV)], axis=0)
            cact_ref[...] = cact_all
            col0 = pl.multiple_of(me * ADA_SHARD, 128)
            mp_buf[me] = jnp.dot(cact_all, wada_ref[...], preferred_element_type=F32,
                                 precision=lax.Precision.HIGHEST) + bada_ref[:, pl.ds(col0, ADA_SHARD)]
            f_mp = _ag_start(ch_mp, pos)
            _ag_finish(ch_cw, pos, f_cw)
            _ag_finish(ch_mp, pos, f_mp)
            for d in range(NDEV):
                cwf_ref[0:CK, CONV_SHARD * d:CONV_SHARD * (d + 1)] = cw_buf[d]
            cwf_ref[CK:CKP, :] = jnp.zeros((CKP - CK, CW), F32)
            mod_ref[...] = jnp.concatenate([mp_buf[d, pl.ds(me, 1), :] for d in range(NDEV)], axis=1)

            for j in (1, 0):
                landed(ch_win, j).wait_recv()
                passed(ch_win, j).start()
            from_sibling(ch_win).wait_recv()
            relayed(ch_win, 1).wait_recv()
            first(ch_win, 1).wait_send()
            first(ch_win, 2).wait_send()
            first(ch_win, 3).start()
            wout = wout_ref[...].astype(BF)
            woutf_ref[pl.ds(pl.multiple_of(me * OUT_SHARD, 16), OUT_SHARD), :] = wout
            for j in range(4):
                first(ch_wout, j).start()

        row0 = pl.multiple_of((s % NT) * TS, TS)

        @pl.when(s < NT)
        def _():
            xv = x_ref[...]
            shift = mod_ref[:, 0:D]
            scale = mod_ref[:, D:2 * D]
            r = lax.rsqrt(jnp.mean(xv * xv, axis=-1, keepdims=True) + EPS)
            hb = ((xv * r * nw_ref[...]) * (1.0 + scale) + shift).astype(BF)
            h_s[pl.ds(row0, TS), :] = hb
            w_half = wtf_ref[pl.ds(pl.multiple_of(x * HALF, 16), HALF), :]
            raw0[pl.ds(row0, TS), :] = lax.dot_general(hb, w_half, (((1,), (1,)), ((), ())),
                                                       preferred_element_type=F32)

        @pl.when(s == NT)
        def _():
            relayed(ch_win, 0).wait_recv()
            landed(ch_win, 2).wait_recv()
            passed(ch_win, 2).start()
            relayed(ch_win, 2).wait_recv()
            pltpu.make_async_copy(wtf_ref, wtf_hbm, out_sems.at[0]).start()

        @pl.when(s >= NT)
        def _():
            hb = h_s[pl.ds(row0, TS), :]
            h_ref[...] = hb
            w_half = wtf_ref[pl.ds(pl.multiple_of((1 - x) * HALF, 16), HALF), :]
            raw1 = lax.dot_general(hb, w_half, (((1,), (1,)), ((), ())), preferred_element_type=F32)
            pt[:, pl.ds(pl.multiple_of(x * HALF, 128), HALF)] = raw0[pl.ds(row0, TS), :]
            pt[:, pl.ds(pl.multiple_of((1 - x) * HALF, 128), HALF)] = raw1
            cos = cos_ref[...]
            sin = sin_ref[...]
            q = pt[:, 0:512]
            qraw_ref[...] = q
            qr_ref[...] = _norm_rope_fwd(q, qw_ref[...], jnp.tile(cos, (1, 4)), jnp.tile(sin, (1, 4)),
                                         bq_ref).astype(BF)
            k = pt[:, 512:640]
            kraw_ref[...] = k
            kr_ref[...] = _norm_rope_fwd(k, kw_ref[...], cos, sin, bk_ref).astype(BF)
            vb_ref[...] = pt[:, 640:768].astype(BF)
            ga_ref[...] = pt[:, 768:1280]
            a = pt[:, 1280:1792]
            g = pt[:, 1792:2304]
            a_ref[...] = a
            g_ref[...] = g
            z_ref[...] = a * _sigmoid(g)
            gb_ref[...] = pt[:, 2304:2816]

        @pl.when(s == 2 * NT - 1)
        def _():
            for j in range(3):
                landed(ch_wout, j).wait_recv()
                passed(ch_wout, j).start()
            from_sibling(ch_wout).wait_recv()
            for j in range(3):
                relayed(ch_wout, j).wait_recv()
            out_copy = pltpu.make_async_copy(woutf_ref, woutf_hbm, out_sems.at[1])
            out_copy.start()
            pltpu.make_async_copy(wtf_ref, wtf_hbm, out_sems.at[0]).wait()
            out_copy.wait()
            first(ch_win, 0).wait_send()
            first(ch_win, 3).wait_send()
            for j in range(3):
                passed(ch_win, j).wait_send()
                passed(ch_wout, j).wait_send()
            for j in range(4):
                first(ch_wout, j).wait_send()

    early = lambda i: (jnp.minimum(i, NT - 1), 0)
    late = lambda i: (jnp.maximum(i - NT, 0), 0)
    t512 = pl.BlockSpec((TS, 512), late)
    t128 = pl.BlockSpec((TS, 128), late)
    sem = pltpu.SemaphoreType.DMA((7,))
    sds = jax.ShapeDtypeStruct
    return pl.pallas_call(
        body, name="gather_fwd", grid=(2 * NT,),
        out_shape=[sds((INW, D), BF), sds((D, D), BF), sds((CKP, CW), F32), sds((1, 3 * D), F32), sds((NDEV, D), F32),
                   sds((S, D), BF), sds((S, AW), F32), sds((S, KVW), F32), sds((S, AW), F32), sds((S, CW), F32),
                   sds((S, CW), F32), sds((S, CW), F32), sds((S, AW), BF), sds((S, KVW), BF), sds((S, KVW), BF),
                   sds((S, CW), F32)],
        in_specs=[_VMEM] * 6 + [pl.BlockSpec((TS, D), early), pl.BlockSpec((1, D), _const),
                                pl.BlockSpec((1, AW), _const), pl.BlockSpec((1, KVW), _const), t128, t128,
                                pl.BlockSpec((AW, AW), _const), pl.BlockSpec((KVW, KVW), _const)],
        out_specs=[_ANY, _ANY] + [_VMEM] * 3 + [pl.BlockSpec((TS, D), late), t512, t128, t512, t512, t512, t512, t512,
                                                t128, t128, t512],
        scratch_shapes=[pltpu.VMEM((NDEV, CK, CONV_SHARD), F32), pltpu.VMEM((NDEV, 8, D), F32),
                        pltpu.VMEM((NDEV, 8, ADA_SHARD), F32), pltpu.VMEM((S, D), BF), pltpu.VMEM((S, HALF), F32),
                        pltpu.VMEM((TS, INW), F32), pltpu.VMEM((INW, D), BF), pltpu.VMEM((D, D), BF)]
        + [sem] * 10 + [pltpu.SemaphoreType.DMA((2,))],
        compiler_params=_params(True),
    )(w_in_t, w_out_s, conv_w_s, c, w_ada_s, b_ada, x2, norm_w, qw_t, kw_t, cos_t, sin_t, bq, bk)


QB = 4
NQB = NB // QB


def _band_mask(has_prev):
    kj = lax.broadcasted_iota(jnp.int32, (2 * BLK, 4 * BLK), 0)
    qi = lax.broadcasted_iota(jnp.int32, (2 * BLK, 4 * BLK), 1) & (BLK - 1)
    dist = qi + BLK - kj
    local = (dist >= 0) & (dist < BLK)
    return local if has_prev is True else local & ((kj >= BLK) | has_prev)


def _key_blocks(sb, prev_ref, cur_ref):
    prev = prev_ref[...] if sb == 0 else cur_ref[BLK * (sb - 1):BLK * sb, :]
    return prev, cur_ref[BLK * sb:BLK * (sb + 1), :]


def _sink_lanes(sink_ref, g):
    lane = lax.broadcasted_iota(jnp.int32, (1, 4 * BLK), 1)
    return jnp.where(lane < BLK, sink_ref[0, 4 * g],
                     jnp.where(lane < 2 * BLK, sink_ref[0, 4 * g + 1],
                               jnp.where(lane < 3 * BLK, sink_ref[0, 4 * g + 2], sink_ref[0, 4 * g + 3])))


def _unstack_t(t):
    return [t[:, BLK * h:BLK * (h + 1)].T for h in range(4)]


def _stack_heads(t, g):
    return jnp.concatenate([t[:, HD * (4 * g + h):HD * (4 * g + h + 1)] for h in range(4)], axis=0)


def _band(prev, cur, g):
    return jnp.concatenate([prev[:, HD * g:HD * (g + 1)], cur[:, HD * g:HD * (g + 1)]], axis=0)


def _softmax_band(qs, kb, mask, sink):
    s = lax.dot_general(kb, qs, (((1,), (1,)), ((), ())), preferred_element_type=F32) * (HD ** -0.5)
    s = jnp.where(mask, s, NEG)
    m = jnp.maximum(jnp.max(s, axis=0, keepdims=True), sink)
    e = jnp.exp(s - m)
    es = jnp.exp(sink - m)
    inv = 1.0 / (jnp.sum(e, axis=0, keepdims=True) + es)
    return e * inv, es * inv


HALO = 32


RC = 64
LC = 128


def _windows(ext_ref, r0, l0, base):
    col = ext_ref[pl.ds(r0, RC + HALO), pl.ds(l0, LC)]
    for s in range(8):
        rolled = col if s == 0 else pltpu.roll(col, RC + HALO - s, 0)
        for t in range(CK):
            if (base + t) % 8 == s:
                a8 = base + t - s
                yield t, rolled[a8:a8 + RC]


def _taps(ext_ref, r0, l0, base, cw_ref, flip):
    acc = None
    for t, win in _windows(ext_ref, r0, l0, base):
        k = CK - 1 - t if flip else t
        term = win * cw_ref[k:k + 1, pl.ds(l0, LC)]
        acc = term if acc is None else acc + term
    return acc


NSUB = (BLK // RC) * (CW // LC)


def _sub_tile(t, row_base):
    r0 = pl.multiple_of(row_base + (t // (CW // LC)) * RC, RC)
    l0 = pl.multiple_of((t % (CW // LC)) * LC, LC)
    return r0, l0


FR = QB * BLK


def _attn_conv_fwd_call(sinks, qr, kr, vb, z, gb, cwf, conv_b, ln_w, ln_b):
    def body(sink_ref, q_ref, kp_ref, kc_ref, vp_ref, vc_ref, z_ref, zh_ref, gb_ref, cw_ref, cb_ref, lw_ref, lb_ref,
             o_ref, zc_ref, yb_ref, zext):
        i = pl.program_id(0)
        zext[0:HALO, :] = jnp.where(i > 0, zh_ref[...], 0.0)
        zext[HALO:HALO + FR, :] = z_ref[...]
        for sb in range(QB):
            rows = slice(BLK * sb, BLK * (sb + 1))
            mask = _band_mask(i > 0 if sb == 0 else True)
            q = q_ref[rows, :]
            kp, kc = _key_blocks(sb, kp_ref, kc_ref)
            vp, vc = _key_blocks(sb, vp_ref, vc_ref)
            for g in range(NKV):
                p, _ = _softmax_band(_stack_heads(q, g), _band(kp, kc, g), mask, _sink_lanes(sink_ref, g))
                o_t = lax.dot_general(_band(vp, vc, g), p.astype(BF), (((0,), (0,)), ((), ())),
                                      preferred_element_type=F32)
                for h, o_h in enumerate(_unstack_t(o_t)):
                    o_ref[rows, HD * (4 * g + h):HD * (4 * g + h + 1)] = o_h
            for r0 in range(BLK * sb, BLK * (sb + 1), RC):
                for l0 in range(0, CW, LC):
                    acc = _taps(zext, r0, l0, HALO - (CK - 1), cw_ref, False)
                    zc_ref[r0:r0 + RC, l0:l0 + LC] = acc + cb_ref[:, l0:l0 + LC]
            zc = zc_ref[rows, :]
            mu = jnp.mean(zc, axis=-1, keepdims=True)
            dz = zc - mu
            rstd = lax.rsqrt(jnp.mean(dz * dz, axis=-1, keepdims=True) + EPS)
            zn = dz * rstd * lw_ref[...] + lb_ref[...]
            gbv = gb_ref[rows, :]
            yb_ref[rows, :] = (zn * _sigmoid(zn)) * (gbv * _sigmoid(gbv))

    prev = lambda i: (jnp.maximum(QB * i - 1, 0), 0)
    halo = lambda i: (jnp.maximum(FR // HALO * i - 1, 0), 0)
    f512 = pl.BlockSpec((FR, 512), _row)
    f128 = pl.BlockSpec((FR, KVW), _row)
    c512 = pl.BlockSpec((1, CW), _const)
    return pl.pallas_call(
        body, name="attn_conv_fwd", grid=(NQB,),
        out_shape=[jax.ShapeDtypeStruct((S, AW), F32), jax.ShapeDtypeStruct((S, CW), F32),
                   jax.ShapeDtypeStruct((S, CW), F32)],
        in_specs=[_SMEM, f512, pl.BlockSpec((BLK, KVW), prev), f128, pl.BlockSpec((BLK, KVW), prev), f128,
                  f512, pl.BlockSpec((HALO, CW), halo), f512, pl.BlockSpec((CKP, CW), _const), c512, c512, c512],
        out_specs=[f512, f512, f512],
        scratch_shapes=[pltpu.VMEM((FR + HALO, CW), F32)],
        compiler_params=_params(True),
    )(sinks, qr, kr, kr, vb, vb, z, z, gb, cwf, conv_b, ln_w, ln_b)


def _ln_gate_bwd(d_yb, zc, gbv, lw, lb):
    mu = jnp.mean(zc, axis=-1, keepdims=True)
    dz = zc - mu
    rstd = lax.rsqrt(jnp.mean(dz * dz, axis=-1, keepdims=True) + EPS)
    zh = dz * rstd
    zn = zh * lw + lb
    act_n, dact_n = _silu_and_grad(zn)
    act_g, dact_g = _silu_and_grad(gbv)
    d_gb = d_yb * act_n * dact_g
    d_zn = d_yb * act_g * dact_n
    dzh = d_zn * lw
    d_zc = rstd * (dzh - jnp.mean(dzh, axis=-1, keepdims=True) - zh * jnp.mean(dzh * zh, axis=-1, keepdims=True))
    return d_zc, d_gb, d_zn, zh


def _attn_conv_bwd_call(sinks, qr, kr, vb, d_ya, ga, o, qraw, qw_t, cos_t, sin_t, bq, d_yb, zc, gb, z, a, g, cwf,
                        ln_w, ln_b):
    def body(sink_ref, q_ref, kp_ref, kc_ref, vp_ref, vc_ref, dya_ref, ga_ref, o_ref, qraw_ref, qw_ref,
             cos_ref, sin_ref, bq_ref,
             dyb_ref, dybn_ref, zc_ref, zcn_ref, gb_ref, gbn_ref, z_ref, zh_ref, a_ref, g_ref, cw_ref, lw_ref, lb_ref,
             dqraw_ref, dga_ref, dk_ref, dv_ref, gqw_ref, gsink_ref,
             dgb_ref, da_ref, dg_ref, gcw_ref, glw_ref, glb_ref, gcb_ref,
             dext, zext, gacc):
        i = pl.program_id(0)

        @pl.when(i == 0)
        def _():
            dk_ref[...] = jnp.zeros((S, KVW), F32)
            dv_ref[...] = jnp.zeros((S, KVW), F32)
            gqw_ref[...] = jnp.zeros((1, AW), F32)
            gsink_ref[...] = jnp.zeros((1, 128), F32)
            gacc[...] = jnp.zeros((CKP * 8, CW), F32)
            glw_ref[...] = jnp.zeros((1, CW), F32)
            glb_ref[...] = jnp.zeros((1, CW), F32)
            gcb_ref[...] = jnp.zeros((1, CW), F32)

        lw = lw_ref[...]
        lb = lb_ref[...]

        def ln_rows(rows):
            d_zc, d_gb, d_zn, zh = _ln_gate_bwd(dyb_ref[rows, :], zc_ref[rows, :], gb_ref[rows, :], lw, lb)
            dgb_ref[rows, :] = d_gb.astype(BF)
            glw_ref[...] += jnp.sum(d_zn * zh, axis=0, keepdims=True)
            glb_ref[...] += jnp.sum(d_zn, axis=0, keepdims=True)
            gcb_ref[...] += jnp.sum(d_zc, axis=0, keepdims=True)
            dext[rows, :] = d_zc

        ln_rows(slice(0, BLK))
        zext[0:HALO, :] = jnp.where(i > 0, zh_ref[...], 0.0)
        zext[HALO:HALO + FR, :] = z_ref[...]

        lane = lax.broadcasted_iota(jnp.int32, (1, 128), 1)
        gsink = jnp.zeros((1, 128), F32)
        dq_rows = []
        for sb in range(QB):
            rows = slice(BLK * sb, BLK * (sb + 1))
            if sb + 1 < QB:
                ln_rows(slice(BLK * (sb + 1), BLK * (sb + 2)))
            else:
                d_zc_next, _, _, _ = _ln_gate_bwd(dybn_ref[...], zcn_ref[...], gbn_ref[...], lw, lb)
                dext[FR:FR + HALO, :] = jnp.where(i < NQB - 1, d_zc_next, 0.0)
            mask = _band_mask(i > 0 if sb == 0 else True)
            q = q_ref[rows, :]
            d_ya = dya_ref[rows, :]
            act, dact = _silu_and_grad(ga_ref[rows, :])
            dga_ref[rows, :] = (d_ya * o_ref[rows, :] * dact).astype(BF)
            d_o = (d_ya * act).astype(BF)
            kp, kc = _key_blocks(sb, kp_ref, kc_ref)
            vp, vc = _key_blocks(sb, vp_ref, vc_ref)
            dq_parts, dk_parts, dv_parts = [], [], []
            for gi in range(NKV):
                qs = _stack_heads(q, gi)
                kb = _band(kp, kc, gi)
                vbd = _band(vp, vc, gi)
                p, ps = _softmax_band(qs, kb, mask, _sink_lanes(sink_ref, gi))
                dos = _stack_heads(d_o, gi)
                dp = lax.dot_general(vbd, dos, (((1,), (1,)), ((), ())), preferred_element_type=F32)
                dr = jnp.sum(p * dp, axis=0, keepdims=True)
                ds = (p * (dp - dr) * (HD ** -0.5)).astype(BF)
                sink_term = ps * dr
                for h in range(4):
                    part = jnp.sum(sink_term[:, BLK * h:BLK * (h + 1)], axis=1, keepdims=True)
                    gsink = gsink - jnp.where(lane == 4 * gi + h, part, 0.0)
                dv_parts.append(jnp.dot(p.astype(BF), dos, preferred_element_type=F32))
                dk_parts.append(jnp.dot(ds, qs, preferred_element_type=F32))
                dq_t = lax.dot_general(kb, ds, (((0,), (0,)), ((), ())), preferred_element_type=F32)
                dq_parts.extend(_unstack_t(dq_t))
            dkb = jnp.concatenate(dk_parts, axis=1)
            dvb = jnp.concatenate(dv_parts, axis=1)
            blk = QB * i + sb
            r_prev = pl.multiple_of(jnp.maximum(blk - 1, 0) * BLK, BLK)
            r_cur = pl.multiple_of(blk * BLK, BLK)
            dk_ref[pl.ds(r_prev, BLK), :] += dkb[0:BLK]
            dv_ref[pl.ds(r_prev, BLK), :] += dvb[0:BLK]
            dk_ref[pl.ds(r_cur, BLK), :] += dkb[BLK:2 * BLK]
            dv_ref[pl.ds(r_cur, BLK), :] += dvb[BLK:2 * BLK]
            dq_rows.append(jnp.concatenate(dq_parts, axis=1))

            def taps_sub(t, carry, sb=sb):
                r0, l0 = _sub_tile(t, BLK * sb)
                here = (pl.ds(r0, RC), pl.ds(l0, LC))
                d_z = _taps(dext, r0, l0, 0, cw_ref, True)
                sg = _sigmoid(g_ref[here])
                da_ref[here] = (d_z * sg).astype(BF)
                dg_ref[here] = (d_z * a_ref[here] * sg * (1.0 - sg)).astype(BF)
                d_sub = dext[here]
                for k, win in _windows(zext, r0, l0, HALO - (CK - 1)):
                    prod = d_sub * win
                    part = prod[0:8]
                    for q8 in range(1, RC // 8):
                        part = part + prod[8 * q8:8 * q8 + 8]
                    gacc[8 * k:8 * k + 8, pl.ds(l0, LC)] += part
                return carry

            lax.fori_loop(0, NSUB, taps_sub, 0)
        gsink_ref[...] += gsink
        dq = jnp.concatenate(dq_rows, axis=0)
        dq_raw, g_qw = _norm_rope_bwd(dq, qraw_ref[...], qw_ref[...], jnp.tile(cos_ref[...], (1, 4)),
                                      jnp.tile(sin_ref[...], (1, 4)), bq_ref)
        dqraw_ref[...] = dq_raw.astype(BF)
        gqw_ref[...] += g_qw

        @pl.when(i == NQB - 1)
        def _():
            for k in range(CK):
                gcw_ref[k:k + 1, :] = jnp.sum(gacc[8 * k:8 * k + 8, :], axis=0, keepdims=True)
            gcw_ref[CK:CKP, :] = jnp.zeros((CKP - CK, CW), F32)

    prev = lambda i: (jnp.maximum(QB * i - 1, 0), 0)
    halo_prev = lambda i: (jnp.maximum(FR // HALO * i - 1, 0), 0)
    halo_next = lambda i: (jnp.minimum(FR // HALO * (i + 1), S // HALO - 1), 0)
    f512 = pl.BlockSpec((FR, 512), _row)
    f128 = pl.BlockSpec((FR, 128), _row)
    hn = pl.BlockSpec((HALO, CW), halo_next)
    c512 = pl.BlockSpec((1, CW), _const)
    sds = jax.ShapeDtypeStruct
    vec = sds((1, CW), F32)
    return pl.pallas_call(
        body, name="attn_conv_bwd", grid=(NQB,),
        out_shape=[sds((S, AW), BF), sds((S, AW), BF), sds((S, KVW), F32), sds((S, KVW), F32), sds((1, AW), F32),
                   sds((1, 128), F32),
                   sds((S, CW), BF), sds((S, CW), BF), sds((S, CW), BF), sds((CKP, CW), F32), vec, vec, vec],
        in_specs=[_SMEM, f512, pl.BlockSpec((BLK, KVW), prev), f128, pl.BlockSpec((BLK, KVW), prev), f128,
                  f512, f512, f512, f512, pl.BlockSpec((1, AW), _const), f128, f128, pl.BlockSpec((AW, AW), _const),
                  f512, hn, f512, hn, f512, hn, f512, pl.BlockSpec((HALO, CW), halo_prev), f512, f512,
                  pl.BlockSpec((CKP, CW), _const), c512, c512],
        out_specs=[f512, f512, _VMEM, _VMEM, _VMEM, _VMEM, f512, f512, f512, _VMEM, _VMEM, _VMEM, _VMEM],
        scratch_shapes=[pltpu.VMEM((FR + HALO, CW), F32), pltpu.VMEM((FR + HALO, CW), F32),
                        pltpu.VMEM((CKP * 8, CW), F32)],
        compiler_params=_params(True),
    )(sinks, qr, kr, kr, vb, vb, d_ya, ga, o, qraw, qw_t, cos_t, sin_t, bq,
      d_yb, d_yb, zc, zc, gb, gb, z, z, a, g, cwf, ln_w, ln_b)


def _out_loss_call(o, ga, yb, x2, tgt, mod, w_out_full):
    def body(o_ref, ga_ref, yb_ref, x_ref, t_ref, mod_ref, w_ref,
             dout_ref, dya_ref, dyb_ref, gw_ref, loss_ref, dgate_ref):
        i = pl.program_id(0)

        @pl.when(i == 0)
        def _():
            gw_ref[...] = jnp.zeros((D, D), F32)
            loss_ref[...] = jnp.zeros((1, 128), F32)
            dgate_ref[...] = jnp.zeros((1, D), F32)

        gav = ga_ref[...]
        ya = o_ref[...] * (gav * _sigmoid(gav))
        ycat = jnp.concatenate([ya, yb_ref[...]], axis=1).astype(BF)
        w = w_ref[...]
        y = jnp.dot(ycat, w, preferred_element_type=F32)
        gate = mod_ref[:, 2 * D:3 * D]
        diff = x_ref[...] + gate * y - t_ref[...]
        sq = jnp.sum(jnp.sum(diff * diff, axis=1, keepdims=True), axis=0, keepdims=True)
        loss_ref[...] += jnp.broadcast_to(sq, (1, 128))
        d_out = diff * (1.0 / D)
        dout_ref[...] = d_out
        dgate_ref[...] += jnp.sum(d_out * y, axis=0, keepdims=True)
        dy = (d_out * gate).astype(BF)
        d_ycat = lax.dot_general(dy, w, (((1,), (1,)), ((), ())), preferred_element_type=F32)
        dya_ref[...] = d_ycat[:, 0:AW]
        dyb_ref[...] = d_ycat[:, AW:D]
        gw_ref[...] += lax.dot_general(ycat, dy, (((0,), (0,)), ((), ())), preferred_element_type=F32)

    t512 = pl.BlockSpec((TS, 512), _row)
    t1024 = pl.BlockSpec((TS, D), _row)
    return pl.pallas_call(
        body, name="out_loss", grid=(NT,),
        out_shape=[jax.ShapeDtypeStruct((S, D), F32), jax.ShapeDtypeStruct((S, AW), F32),
                   jax.ShapeDtypeStruct((S, CW), F32), jax.ShapeDtypeStruct((D, D), F32),
                   jax.ShapeDtypeStruct((1, 128), F32), jax.ShapeDtypeStruct((1, D), F32)],
        in_specs=[t512, t512, t512, t1024, t1024, pl.BlockSpec((1, 3 * D), _const),
                  pl.BlockSpec((D, D), _const, pipeline_mode=pl.Buffered(1))],
        out_specs=[t1024, t512, t512, _VMEM, _VMEM, _VMEM],
        compiler_params=_params(True),
    )(o, ga, yb, x2, tgt, mod, w_out_full)


SM_ROWS = 8
PIECES = ((0, 512), (512, 640), (640, 768), (768, 1280), (1280, 1792), (1792, 2304), (2304, 2816))


def _bwd_in_call(dqraw, dk, dv, dga, da, dg, dgb, kraw, kw_t, cos_t, sin_t, bk, h, wt_full, x2, d_out, mod, norm_w,
                 gw_out, gcw, glw, glb, gcb, gqw, gsink, dgate, loss_p, cact_all):
    def body(dq_ref, dk_ref, dv_ref, dga_ref, da_ref, dg_ref, dgb_ref, kraw_ref, kw_ref, cos_ref, sin_ref, bk_ref,
             h_ref, wt_ref, x_ref, dout_ref, mod_ref, nw_ref, gwout_ref, gcw_ref, glw_ref, glb_ref, gcb_ref, gqw_ref,
             gsink_ref, dgate_ref, loss_ref, cact_ref,
             gx_ref, o_gwin, o_gwout, o_gwada, o_gbada, o_gnw, o_gqw, o_gkw, o_gsink, o_gcw, o_gcb, o_glw, o_glb,
             o_loss,
             acc, win_send, win_sib, win_ici, wout_send, wout_sib, wout_ici, sm_buf, cw_buf, dmod_all, vec_acc, gkw_acc,
             wi_ds, wi_dr, wi_is, wi_ir, wo_ds, wo_dr, wo_is, wo_ir, sm_s, sm_r, cw_s, cw_r):
        i = pl.program_id(0)
        pos = _mesh_pos()
        x, y, cc = pos
        me = 4 * x + 2 * y + cc

        def chip(j):
            return (1 - x if j & 1 else x, 1 - y if j & 2 else y)

        def rows_of(buf, px, py, pc, rows, align):
            return buf.at[pl.ds(pl.multiple_of((4 * px + 2 * py + pc) * rows, align), rows), :]

        bufs = {"in": (win_send, win_sib, win_ici, IN_SHARD, wi_ds, wi_dr, wi_is, wi_ir),
                "out": (wout_send, wout_sib, wout_ici, OUT_SHARD, wo_ds, wo_dr, wo_is, wo_ir)}

        def d2d_copy(j, which):
            send, sib, _, rows, ds_, dr_, _, _ = bufs[which]
            px, py = chip(j)
            return pltpu.make_async_remote_copy(src_ref=rows_of(send, px, py, 1 - cc, rows, 16), dst_ref=sib.at[j],
                                                send_sem=ds_.at[j], recv_sem=dr_.at[j], device_id=(x, y, 1 - cc),
                                                device_id_type=MESH)

        def ici_copy(j, which):
            send, _, ici, rows, _, _, is_, ir_ = bufs[which]
            px, py = chip(j)
            return pltpu.make_async_remote_copy(src_ref=rows_of(send, px, py, cc, rows, 16), dst_ref=ici.at[j - 1],
                                                send_sem=is_.at[j - 1], recv_sem=ir_.at[j - 1], device_id=(px, py, cc),
                                                device_id_type=MESH)

        def level2(which, partial_ref):
            send, sib, _, rows, _, _, _, _ = bufs[which]
            for j in range(1, 4):
                d2d_copy(j, which).wait_recv()
                px, py = chip(j)
                mine = rows_of(partial_ref, px, py, cc, rows, 8)[...]
                rows_of(send, px, py, cc, rows, 16)[...] = (mine + sib[j].astype(F32)).astype(BF)
                ici_copy(j, which).start()

        def finish(which, partial_ref):
            _, sib, ici, rows, _, _, _, _ = bufs[which]
            d2d_copy(0, which).wait_recv()
            total = rows_of(partial_ref, x, y, cc, rows, 8)[...] + sib[0].astype(F32)
            for j in range(1, 4):
                ici_copy(j, which).wait_recv()
                total = total + ici[j - 1].astype(F32)
            for j in range(4):
                d2d_copy(j, which).wait_send()
            for j in range(1, 4):
                ici_copy(j, which).wait_send()
            return total

        def dproj_pieces():
            dk_raw, g_kw = _norm_rope_bwd(dk_ref[...], kraw_ref[...], kw_ref[...], cos_ref[...], sin_ref[...], bk_ref)
            return [dq_ref[...], dk_raw.astype(BF), dv_ref[...].astype(BF), dga_ref[...], da_ref[...], dg_ref[...],
                    dgb_ref[...]], g_kw

        @pl.when(i == 0)
        def _():
            acc[...] = jnp.zeros((INW, D), F32)
            vec_acc[...] = jnp.zeros((8, D), F32)
            gkw_acc[...] = jnp.zeros((1, KVW), F32)
            wout_send[...] = gwout_ref[...].astype(BF)
            for j in range(4):
                d2d_copy(j, "out").start()

        @pl.when(i == 2)
        def _():
            level2("out", gwout_ref)

        @pl.when(i < NT)
        def _():
            pieces, g_kw = dproj_pieces()
            gkw_acc[...] += g_kw
            hv = h_ref[...]
            for (lo, hi), piece in zip(PIECES, pieces):
                acc[lo:hi, :] += lax.dot_general(piece, hv, (((0,), (0,)), ((), ())), preferred_element_type=F32)

        @pl.when(i == NT - 1)
        def _():
            for lo, hi in PIECES:
                win_send[lo:hi, :] = acc[lo:hi, :].astype(BF)
            for j in range(4):
                d2d_copy(j, "in").start()

        @pl.when(i == NT + 2)
        def _():
            level2("in", acc)

        @pl.when(i >= NT)
        def _():
            pieces, _ = dproj_pieces()
            dproj = jnp.concatenate(pieces, axis=1)
            d_h = jnp.dot(dproj, wt_ref[...], preferred_element_type=F32)
            xv = x_ref[...]
            scale = mod_ref[:, D:2 * D]
            nw = nw_ref[...]
            r = lax.rsqrt(jnp.mean(xv * xv, axis=-1, keepdims=True) + EPS)
            xn = xv * r
            vec_acc[0:1, :] += jnp.sum(d_h, axis=0, keepdims=True)
            vec_acc[1:2, :] += jnp.sum(d_h * (xn * nw), axis=0, keepdims=True)
            d_u = d_h * (1.0 + scale)
            vec_acc[2:3, :] += jnp.sum(d_u * xn, axis=0, keepdims=True)
            d_xn = d_u * nw
            gx_ref[...] = dout_ref[...] + r * (d_xn - xn * jnp.mean(d_xn * xn, axis=-1, keepdims=True))

        @pl.when(i == 2 * NT - 1)
        def _():
            ch_sm = (_slab(sm_buf), sm_s, sm_r)
            ch_cw = (_slab(cw_buf), cw_s, cw_r)
            z128 = jnp.zeros((1, 128), F32)
            row4 = jnp.concatenate([glw_ref[...], glb_ref[...]], axis=1)
            row5 = jnp.concatenate([gcb_ref[...], gqw_ref[...]], axis=1)
            row6 = jnp.concatenate([gkw_acc[...], gsink_ref[...], loss_ref[...]] + [z128] * 5, axis=1)
            sm_buf[me] = jnp.concatenate([vec_acc[0:2, :], dgate_ref[...], vec_acc[2:3, :], row4, row5, row6,
                                          jnp.zeros((1, D), F32)], axis=0)
            f_sm = _ag_start(ch_sm, pos)
            cw_buf[me] = gcw_ref[...]
            f_cw = _ag_start(ch_cw, pos)
            _ag_finish(ch_sm, pos, f_sm)
            _ag_finish(ch_cw, pos, f_cw)
            tot = sm_buf[0]
            cw_tot = cw_buf[0]
            for d in range(1, NDEV):
                tot = tot + sm_buf[d]
                cw_tot = cw_tot + cw_buf[d]
            o_gbada[...] = jnp.concatenate([tot[0:1, :], tot[1:2, :], tot[2:3, :]], axis=1)
            o_gnw[...] = tot[3:4, :]
            o_glw[...] = tot[4:5, 0:CW]
            o_glb[...] = tot[4:5, CW:D]
            o_gcb[...] = tot[5:6, 0:CW]
            gq = tot[5:6, CW:CW + HD]
            for hh in range(1, NQ):
                gq = gq + tot[5:6, CW + HD * hh:CW + HD * (hh + 1)]
            o_gqw[...] = gq
            o_gkw[...] = tot[6:7, 0:HD] + tot[6:7, HD:2 * HD]
            o_gsink[...] = tot[6:7, 128:128 + NQ]
            o_loss[...] = tot[6:7, 256:384] * (0.5 / D)
            mine = jnp.zeros((CK, CONV_SHARD), F32)
            for d in range(NDEV):
                mine = mine + jnp.where(me == d, cw_tot[0:CK, CONV_SHARD * d:CONV_SHARD * (d + 1)], 0.0)
            for k in range(CK):
                o_gcw[k] = mine[k:k + 1, :]
            for d in range(NDEV):
                dmod_all[d:d + 1, :] = jnp.concatenate([sm_buf[d, 0:1, :], sm_buf[d, 1:2, :], sm_buf[d, 2:3, :]],
                                                       axis=1)
            col0 = pl.multiple_of(me * ADA_SHARD, 128)
            o_gwada[...] = lax.dot_general(cact_ref[...], dmod_all[:, pl.ds(col0, ADA_SHARD)], (((0,), (0,)), ((), ())),
                                           preferred_element_type=F32, precision=lax.Precision.HIGHEST)

            o_gwout[...] = finish("out", gwout_ref)
            o_gwin[...] = finish("in", acc)

    half = lambda i: (i % NT, 0)
    late = lambda i: (jnp.maximum(i - NT, 0), 0)
    t512 = pl.BlockSpec((TS, 512), half)
    t128 = pl.BlockSpec((TS, 128), half)
    l1024 = pl.BlockSpec((TS, D), late)
    sem7 = pltpu.SemaphoreType.DMA((7,))
    sem4 = pltpu.SemaphoreType.DMA((4,))
    sem3 = pltpu.SemaphoreType.DMA((3,))
    sds = jax.ShapeDtypeStruct
    return pl.pallas_call(
        body, name="bwd_in", grid=(2 * NT,),
        out_shape=[sds((S, D), F32), sds((IN_SHARD, D), F32), sds((OUT_SHARD, D), F32), sds((D, ADA_SHARD), F32),
                   sds((1, 3 * D), F32), sds((1, D), F32), sds((1, HD), F32), sds((1, HD), F32), sds((1, NQ), F32),
                   sds((CK, 1, CONV_SHARD), F32), sds((1, CW), F32), sds((1, CW), F32), sds((1, CW), F32),
                   sds((1, 128), F32)],
        in_specs=[t512, t128, t128, t512, t512, t512, t512, t128, pl.BlockSpec((1, KVW), _const), t128, t128,
                  pl.BlockSpec((KVW, KVW), _const), pl.BlockSpec((TS, D), half),
                  pl.BlockSpec((INW, D), _const, pipeline_mode=pl.Buffered(1)), l1024, l1024,
                  pl.BlockSpec((1, 3 * D), _const), pl.BlockSpec((1, D), _const)] + [_VMEM] * 10,
        out_specs=[l1024] + [_VMEM] * 13,
        scratch_shapes=[pltpu.VMEM((INW, D), F32), pltpu.VMEM((INW, D), BF), pltpu.VMEM((4, IN_SHARD, D), BF),
                        pltpu.VMEM((3, IN_SHARD, D), BF), pltpu.VMEM((D, D), BF), pltpu.VMEM((4, OUT_SHARD, D), BF),
                        pltpu.VMEM((3, OUT_SHARD, D), BF),
                        pltpu.VMEM((NDEV, SM_ROWS, D), F32), pltpu.VMEM((NDEV, CKP, CW), F32),
                        pltpu.VMEM((NDEV, 3 * D), F32), pltpu.VMEM((8, D), F32), pltpu.VMEM((1, KVW), F32)]
        + [sem4, sem4, sem3, sem3] * 2 + [sem7] * 4,
        compiler_params=pltpu.CompilerParams(dimension_semantics=("arbitrary",), vmem_limit_bytes=BIG_VMEM_LIMIT),
    )(dqraw, dk, dv, dga, da, dg, dgb, kraw, kw_t, cos_t, sin_t, bk, h, wt_full, x2, d_out, mod, norm_w,
      gw_out, gcw, glw, glb, gcb, gqw, gsink, dgate, loss_p, cact_all)


ADAM_STEPS = 4


def _adam_call(ws, gs, ms, vs, grad_x, loss_v):
    n = len(ws)
    bc1 = 1.0 - ADAM_B1 ** ADAM_STEP
    bc2 = 1.0 - ADAM_B2 ** ADAM_STEP
    chunked = [w.ndim == 2 and w.shape[0] % (8 * ADAM_STEPS) == 0 for w in ws]

    def body(*refs):
        ins, outs = refs[:4 * n + 2], refs[4 * n + 2:]
        i = pl.program_id(0)

        def update(j):
            w, g, m, v = (ins[j][...], ins[n + j][...], ins[2 * n + j][...], ins[3 * n + j][...])
            m_new = ADAM_B1 * m + (1.0 - ADAM_B1) * g
            v_new = ADAM_B2 * v + (1.0 - ADAM_B2) * (g * g)
            m_hat = m_new / bc1
            v_hat = v_new / bc2
            outs[j][...] = g
            outs[n + j][...] = -ADAM_LR * (m_hat / (jnp.sqrt(v_hat) + ADAM_EPS) + ADAM_WD * w)
            outs[2 * n + j][...] = m_new
            outs[3 * n + j][...] = v_new

        for j in range(n):
            if chunked[j]:
                update(j)
        outs[4 * n][...] = ins[4 * n][...]

        @pl.when(i == 0)
        def _():
            for j in range(n):
                if not chunked[j]:
                    update(j)
            outs[4 * n + 1][...] = ins[4 * n + 1][...]

    def spec(arr, is_chunked):
        if is_chunked:
            return pl.BlockSpec((arr.shape[0] // ADAM_STEPS, arr.shape[1]), _row)
        zeros = (0,) * arr.ndim
        return pl.BlockSpec(arr.shape, lambda i: zeros)

    par_specs = [spec(w, ch) for w, ch in zip(ws, chunked)]
    extra = [spec(grad_x, True), spec(loss_v, False)]
    shapes = [jax.ShapeDtypeStruct(w.shape, F32) for w in ws]
    res = pl.pallas_call(
        body, name="adam", grid=(ADAM_STEPS,),
        out_shape=shapes * 4 + [jax.ShapeDtypeStruct(grad_x.shape, F32), jax.ShapeDtypeStruct(loss_v.shape, F32)],
        in_specs=par_specs * 4 + extra, out_specs=par_specs * 4 + extra,
        compiler_params=_params(True),
    )(*ws, *gs, *ms, *vs, grad_x, loss_v)
    return res[0:n], res[n:2 * n], res[2 * n:3 * n], res[3 * n:4 * n], res[4 * n], res[4 * n + 1]


def _rope_tables():
    inv = (np.float32(ROPE_THETA) ** (-np.arange(0, HD, 2, dtype=np.float32) / np.float32(HD))).astype(np.float32)
    ang = (np.arange(S, dtype=np.float32)[:, None] * inv[None, :]).astype(np.float32)
    cos, sin = np.cos(ang).astype(np.float32), np.sin(ang).astype(np.float32)
    cos64 = np.concatenate([cos, cos], axis=-1)
    sin64 = np.concatenate([-sin, sin], axis=-1)
    return jnp.asarray(np.tile(cos64, (1, 2))), jnp.asarray(np.tile(sin64, (1, 2)))


def _group_matrix(width):
    idx = np.arange(width) // HD
    return jnp.asarray(np.where(idx[:, None] == idx[None, :], 1.0 / HD, 0.0).astype(np.float32)).astype(BF)


def kernel(x, c, w_ada, b_ada, norm_w, w_in, q_norm_w, k_norm_w, sinks, conv_w, conv_b, ln_w, ln_b, w_out, loss_target, m_w_ada, m_b_ada, m_norm_w, m_w_in, m_q_norm_w, m_k_norm_w, m_sinks, m_conv_w, m_conv_b, m_ln_w, m_ln_b, m_w_out, v_w_ada, v_b_ada, v_norm_w, v_w_in, v_q_norm_w, v_k_norm_w, v_sinks, v_conv_w, v_conv_b, v_ln_w, v_ln_b, v_w_out):
    x2 = x[0]
    tgt = loss_target[0]
    cos_t, sin_t = _rope_tables()
    bq = _group_matrix(AW)
    bk = _group_matrix(KVW)
    qw_t = jnp.tile(q_norm_w, (1, NQ))
    kw_t = jnp.tile(k_norm_w, (1, NKV))

    tr = lambda t: jnp.swapaxes(t[0], 0, 1)
    tc = lambda t: jnp.swapaxes(t, 0, 1)
    (wt_full, w_out_full, cwf, mod, cact_all, h, qraw, kraw, ga, a, g, gb, qr, kr, vb, z) = _gather_fwd_call(
        tr(w_in), w_out[0], tc(conv_w), c, w_ada[0], b_ada, x2, norm_w, qw_t, kw_t, cos_t, sin_t, bq, bk)
    o, zc, yb = _attn_conv_fwd_call(sinks, qr, kr, vb, z, gb, cwf, conv_b, ln_w, ln_b)
    d_out, d_ya, d_yb, gw_out, loss_p, dgate = _out_loss_call(o, ga, yb, x2, tgt, mod, w_out_full)

    dqraw, dga, dk, dv, gqw, gsink, dgb, da, dg, gcw, glw, glb, gcb = _attn_conv_bwd_call(
        sinks, qr, kr, vb, d_ya, ga, o, qraw, qw_t, cos_t, sin_t, bq, d_yb, zc, gb, z, a, g, cwf, ln_w, ln_b)
    (grad_x, g_w_in_t, g_w_out, g_w_ada, g_b_ada, g_norm_w, g_qw, g_kw, g_sinks, g_conv_w, g_conv_b, g_ln_w, g_ln_b,
     loss_v) = _bwd_in_call(dqraw, dk, dv, dga, da, dg, dgb, kraw, kw_t, cos_t, sin_t, bk, h, wt_full, x2, d_out, mod,
                            norm_w, gw_out, gcw, glw, glb, gcb, gqw, gsink, dgate, loss_p, cact_all)

    ws = [w_ada[0], b_ada, norm_w, tr(w_in), q_norm_w, k_norm_w, sinks, tc(conv_w), conv_b, ln_w, ln_b, w_out[0]]
    gs = [g_w_ada, g_b_ada, g_norm_w, g_w_in_t, g_qw, g_kw, g_sinks, g_conv_w, g_conv_b, g_ln_w, g_ln_b, g_w_out]
    ms = [m_w_ada[0], m_b_ada, m_norm_w, tr(m_w_in), m_q_norm_w, m_k_norm_w, m_sinks, tc(m_conv_w), m_conv_b, m_ln_w,
          m_ln_b, m_w_out[0]]
    vs = [v_w_ada[0], v_b_ada, v_norm_w, tr(v_w_in), v_q_norm_w, v_k_norm_w, v_sinks, tc(v_conv_w), v_conv_b, v_ln_w,
          v_ln_b, v_w_out[0]]
    grads, deltas, new_m, new_v, grad_x, loss_v = _adam_call(ws, gs, ms, vs, grad_x, loss_v)
    shaped = [w_ada, b_ada, norm_w, w_in, q_norm_w, k_norm_w, sinks, conv_w, conv_b, ln_w, ln_b, w_out]
    W_IN_POS, CONV_W_POS = 3, 7

    def like(vals):
        vals = [jnp.swapaxes(v, 0, 1) if j in (W_IN_POS, CONV_W_POS) else v for j, v in enumerate(vals)]
        return [v.reshape(s.shape) for v, s in zip(vals, shaped)]

    return (loss_v[0, 0], grad_x[None], *like(grads), *like(deltas), *like(new_m), *like(new_v))
```

```python
import functools

import jax
import jax.numpy as jnp
import numpy as np
from jax import lax
from jax.experimental import pallas as pl
from jax.experimental.pallas import tpu as pltpu

S = 2048
D = 1024
NDEV = 8
HD = 64
NQ = 8
NKV = 2
AW = 512
KVW = 128
CW = 512
INW = 2816
IN_SHARD = INW // NDEV
ADA_SHARD = 3 * D // NDEV
OUT_SHARD = D // NDEV
CONV_SHARD = CW // NDEV
CK = 31
CKP = 32
BLK = 128
TS = 256
NT = S // TS
NB = S // BLK
EPS = 1e-6
ROPE_THETA = 10000.0
NEG = -1e30
BF = jnp.bfloat16
F32 = jnp.float32

ADAM_LR = 0.001
ADAM_B1 = 0.9
ADAM_B2 = 0.999
ADAM_EPS = 1e-08
ADAM_WD = 0.01
ADAM_STEP = 10

VMEM_LIMIT = 56 * 1024 * 1024
BIG_VMEM_LIMIT = 62 * 1024 * 1024
MESH = pl.DeviceIdType.MESH

_VMEM = pl.BlockSpec(memory_space=pltpu.VMEM)
_SMEM = pl.BlockSpec(memory_space=pltpu.SMEM)
_ANY = pl.BlockSpec(memory_space=pl.ANY)


def _params(grid=False):
    if grid:
        return pltpu.CompilerParams(dimension_semantics=("arbitrary",), vmem_limit_bytes=VMEM_LIMIT)
    return pltpu.CompilerParams(vmem_limit_bytes=VMEM_LIMIT)


def _row(i):
    return (i, 0)


def _const(i):
    return (0, 0)


def _sigmoid(t):
    return 1.0 / (1.0 + jnp.exp(-t))


def _silu_and_grad(t):
    sg = _sigmoid(t)
    return t * sg, sg * (1.0 + t * (1.0 - sg))


def _group_mean(t, b_ref):
    hi = t.astype(BF)
    lo = (t - hi.astype(F32)).astype(BF)
    b = b_ref[...]
    return jnp.dot(hi, b, preferred_element_type=F32) + jnp.dot(lo, b, preferred_element_type=F32)


def _partner(t):
    w = t.shape[-1]
    lane = lax.broadcasted_iota(jnp.int32, t.shape, 1)
    first = (lane & 32) == 0
    return jnp.where(first, pltpu.roll(t, w - 32, 1), pltpu.roll(t, 32, 1))


def _norm_rope_fwd(t, w_t, cos, sin, b_ref):
    r = lax.rsqrt(_group_mean(t * t, b_ref) + EPS)
    tn = t * r * w_t
    return tn * cos + _partner(tn) * sin


def _norm_rope_bwd(d_out, t, w_t, cos, sin, b_ref):
    d_tn = d_out * cos + _partner(d_out * sin)
    r = lax.rsqrt(_group_mean(t * t, b_ref) + EPS)
    th = t * r
    g_w = jnp.sum(d_tn * th, axis=0, keepdims=True)
    d_th = d_tn * w_t
    d_t = r * (d_th - th * _group_mean(d_th * th, b_ref))
    return d_t, g_w


def _mesh_pos():
    return lax.axis_index("x"), lax.axis_index("y"), lax.axis_index("c")


def _ag_copy(chan, k, block, to):
    blk, send_sems, recv_sems = chan
    ref = blk(*block)
    return pltpu.make_async_remote_copy(src_ref=ref, dst_ref=ref, send_sem=send_sems.at[k],
                                        recv_sem=recv_sems.at[k], device_id=to, device_id_type=MESH)


def _ag_start(chan, pos):
    x, y, c = pos
    me = (x, y, c)
    chips = [(1 - x, y), (x, 1 - y), (1 - x, 1 - y)]
    first = [_ag_copy(chan, 0, me, (x, y, 1 - c))]
    first += [_ag_copy(chan, 1 + j, me, (*chip, c)) for j, chip in enumerate(chips)]
    for cp in first:
        cp.start()
    return first


def _ag_finish(chan, pos, first):
    x, y, c = pos
    me = (x, y, c)
    sibling = (x, y, 1 - c)
    chips = [(1 - x, y), (x, 1 - y), (1 - x, 1 - y)]
    passed = [_ag_copy(chan, 4 + j, (*chip, c), sibling) for j, chip in enumerate(chips)]
    for j, chip in enumerate(chips):
        _ag_copy(chan, 1 + j, (*chip, c), me).wait_recv()
        passed[j].start()
    _ag_copy(chan, 0, sibling, me).wait_recv()
    for j, chip in enumerate(chips):
        _ag_copy(chan, 4 + j, (*chip, 1 - c), me).wait_recv()
    for cp in first + passed:
        cp.wait_send()


def _slab(buf):
    return lambda px, py, pc: buf.at[4 * px + 2 * py + pc]


def _row_block(buf, rows, align):
    return lambda px, py, pc: buf.at[pl.ds(pl.multiple_of((4 * px + 2 * py + pc) * rows, align), rows), :]


HALF = INW // 2


def _gather_fwd_call(w_in_t, w_out_s, conv_w_s, c, w_ada_s, b_ada, x2, norm_w, qw_t, kw_t, cos_t, sin_t, bq, bk):
    def body(win_ref, wout_ref, cw_ref, c_ref, wada_ref, bada_ref, x_ref, nw_ref, qw_ref, kw_ref, cos_ref, sin_ref,
             bq_ref, bk_ref,
             wtf_hbm, woutf_hbm, cwf_ref, mod_ref, cact_ref,
             h_ref, qraw_ref, kraw_ref, ga_ref, a_ref, g_ref, gb_ref, qr_ref, kr_ref, vb_ref, z_ref,
             cw_buf, ca_buf, mp_buf, h_s, raw0, pt, wtf_ref, woutf_ref,
             s0, r0, s1, r1, s2, r2, s3, r3, s4, r4, out_sems):
        s = pl.program_id(0)
        pos = _mesh_pos()
        x, y, cc = pos
        me3 = (x, y, cc)
        me = 4 * x + 2 * y + cc
        sibling = (x, y, 1 - cc)
        chips = [(1 - x, y), (x, 1 - y), (1 - x, 1 - y)]
        ch_win = (_row_block(wtf_ref, IN_SHARD, 16), s0, r0)
        ch_wout = (_row_block(woutf_ref, OUT_SHARD, 16), s1, r1)
        ch_cw = (_slab(cw_buf), s2, r2)
        ch_ca = (_slab(ca_buf), s3, r3)
        ch_mp = (_slab(mp_buf), s4, r4)

        def first(chan, j):
            return _ag_copy(chan, j, me3, sibling if j == 0 else (*chips[j - 1], cc))

        def passed(chan, j):
            return _ag_copy(chan, 4 + j, (*chips[j], cc), sibling)

        def landed(chan, j):
            return _ag_copy(chan, 1 + j, (*chips[j], cc), me3)

        def relayed(chan, j):
            return _ag_copy(chan, 4 + j, (*chips[j], 1 - cc), me3)

        def from_sibling(chan):
            return _ag_copy(chan, 0, sibling, me3)

        @pl.when(s == 0)
        def _():
            cv = c_ref[...]
            ca_buf[me] = jnp.broadcast_to(cv * _sigmoid(cv), (8, D))
            f_ca = _ag_start(ch_ca, pos)
            wtf_ref[pl.ds(pl.multiple_of(me * IN_SHARD, 16), IN_SHARD), :] = win_ref[...].astype(BF)
            for j in range(3):
                first(ch_win, j).start()
            cw_buf[me] = cw_ref[:, 0, :]
            f_cw = _ag_start(ch_cw, pos)

            _ag_finish(ch_ca, pos, f_ca)
            cact_all = jnp.concatenate([ca_buf[d, 0:1, :] for d in range(NDEV)], axis=0)
            cact_ref[...] = cact_all
            col0 = pl.multiple_of(me * ADA_SHARD, 128)
            mp_buf[me] = jnp.dot(cact_all, wada_ref[...], preferred_element_type=F32,
                                 precision=lax.Precision.HIGHEST) + bada_ref[:, pl.ds(col0, ADA_SHARD)]
            f_mp = _ag_start(ch_mp, pos)
            _ag_finish(ch_cw, pos, f_cw)
            _ag_finish(ch_mp, pos, f_mp)
            for d in range(NDEV):
                cwf_ref[0:CK, CONV_SHARD * d:CONV_SHARD * (d + 1)] = cw_buf[d]
            cwf_ref[CK:CKP, :] = jnp.zeros((CKP - CK, CW), F32)
            mod_ref[...] = jnp.concatenate([mp_buf[d, pl.ds(me, 1), :] for d in range(NDEV)], axis=1)

            for j in (1, 0):
                landed(ch_win, j).wait_recv()
                passed(ch_win, j).start()
            from_sibling(ch_win).wait_recv()
            relayed(ch_win, 1).wait_recv()
            first(ch_win, 1).wait_send()
            first(ch_win, 2).wait_send()
            first(ch_win, 3).start()
            wout = wout_ref[...].astype(BF)
            woutf_ref[pl.ds(pl.multiple_of(me * OUT_SHARD, 16), OUT_SHARD), :] = wout
            for j in range(4):
                first(ch_wout, j).start()

        row0 = pl.multiple_of((s % NT) * TS, TS)

        @pl.when(s < NT)
        def _():
            xv = x_ref[...]
            shift = mod_ref[:, 0:D]
            scale = mod_ref[:, D:2 * D]
            r = lax.rsqrt(jnp.mean(xv * xv, axis=-1, keepdims=True) + EPS)
            hb = ((xv * r * nw_ref[...]) * (1.0 + scale) + shift).astype(BF)
            h_s[pl.ds(row0, TS), :] = hb
            w_half = wtf_ref[pl.ds(pl.multiple_of(x * HALF, 16), HALF), :]
            raw0[pl.ds(row0, TS), :] = lax.dot_general(hb, w_half, (((1,), (1,)), ((), ())),
                                                       preferred_element_type=F32)

        @pl.when(s == NT)
        def _():
            relayed(ch_win, 0).wait_recv()
            landed(ch_win, 2).wait_recv()
            passed(ch_win, 2).start()
            relayed(ch_win, 2).wait_recv()
            pltpu.make_async_copy(wtf_ref, wtf_hbm, out_sems.at[0]).start()

        @pl.when(s >= NT)
        def _():
            hb = h_s[pl.ds(row0, TS), :]
            h_ref[...] = hb
            w_half = wtf_ref[pl.ds(pl.multiple_of((1 - x) * HALF, 16), HALF), :]
            raw1 = lax.dot_general(hb, w_half, (((1,), (1,)), ((), ())), preferred_element_type=F32)
            pt[:, pl.ds(pl.multiple_of(x * HALF, 128), HALF)] = raw0[pl.ds(row0, TS), :]
            pt[:, pl.ds(pl.multiple_of((1 - x) * HALF, 128), HALF)] = raw1
            cos = cos_ref[...]
            sin = sin_ref[...]
            q = pt[:, 0:512]
            qraw_ref[...] = q
            qr_ref[...] = _norm_rope_fwd(q, qw_ref[...], jnp.tile(cos, (1, 4)), jnp.tile(sin, (1, 4)),
                                         bq_ref).astype(BF)
            k = pt[:, 512:640]
            kraw_ref[...] = k
            kr_ref[...] = _norm_rope_fwd(k, kw_ref[...], cos, sin, bk_ref).astype(BF)
            vb_ref[...] = pt[:, 640:768].astype(BF)
            ga_ref[...] = pt[:, 768:1280]
            a = pt[:, 1280:1792]
            g = pt[:, 1792:2304]
            a_ref[...] = a
            g_ref[...] = g
            z_ref[...] = a * _sigmoid(g)
            gb_ref[...] = pt[:, 2304:2816]

        @pl.when(s == 2 * NT - 1)
        def _():
            for j in range(3):
                landed(ch_wout, j).wait_recv()
                passed(ch_wout, j).start()
            from_sibling(ch_wout).wait_recv()
            for j in range(3):
                relayed(ch_wout, j).wait_recv()
            out_copy = pltpu.make_async_copy(woutf_ref, woutf_hbm, out_sems.at[1])
            out_copy.start()
            pltpu.make_async_copy(wtf_ref, wtf_hbm, out_sems.at[0]).wait()
            out_copy.wait()
            first(ch_win, 0).wait_send()
            first(ch_win, 3).wait_send()
            for j in range(3):
                passed(ch_win, j).wait_send()
                passed(ch_wout, j).wait_send()
            for j in range(4):
                first(ch_wout, j).wait_send()

    early = lambda i: (jnp.minimum(i, NT - 1), 0)
    late = lambda i: (jnp.maximum(i - NT, 0), 0)
    t512 = pl.BlockSpec((TS, 512), late)
    t128 = pl.BlockSpec((TS, 128), late)
    sem = pltpu.SemaphoreType.DMA((7,))
    sds = jax.ShapeDtypeStruct
    return pl.pallas_call(
        body, name="gather_fwd", grid=(2 * NT,),
        out_shape=[sds((INW, D), BF), sds((D, D), BF), sds((CKP, CW), F32), sds((1, 3 * D), F32), sds((NDEV, D), F32),
                   sds((S, D), BF), sds((S, AW), F32), sds((S, KVW), F32), sds((S, AW), F32), sds((S, CW), F32),
                   sds((S, CW), F32), sds((S, CW), F32), sds((S, AW), BF), sds((S, KVW), BF), sds((S, KVW), BF),
                   sds((S, CW), F32)],
        in_specs=[_VMEM] * 6 + [pl.BlockSpec((TS, D), early), pl.BlockSpec((1, D), _const),
                                pl.BlockSpec((1, AW), _const), pl.BlockSpec((1, KVW), _const), t128, t128,
                                pl.BlockSpec((AW, AW), _const), pl.BlockSpec((KVW, KVW), _const)],
        out_specs=[_ANY, _ANY] + [_VMEM] * 3 + [pl.BlockSpec((TS, D), late), t512, t128, t512, t512, t512, t512, t512,
                                                t128, t128, t512],
        scratch_shapes=[pltpu.VMEM((NDEV, CK, CONV_SHARD), F32), pltpu.VMEM((NDEV, 8, D), F32),
                        pltpu.VMEM((NDEV, 8, ADA_SHARD), F32), pltpu.VMEM((S, D), BF), pltpu.VMEM((S, HALF), F32),
                        pltpu.VMEM((TS, INW), F32), pltpu.VMEM((INW, D), BF), pltpu.VMEM((D, D), BF)]
        + [sem] * 10 + [pltpu.SemaphoreType.DMA((2,))],
        compiler_params=_params(True),
    )(w_in_t, w_out_s, conv_w_s, c, w_ada_s, b_ada, x2, norm_w, qw_t, kw_t, cos_t, sin_t, bq, bk)


QB = 4
NQB = NB // QB


def _band_mask(has_prev):
    kj = lax.broadcasted_iota(jnp.int32, (2 * BLK, 4 * BLK), 0)
    qi = lax.broadcasted_iota(jnp.int32, (2 * BLK, 4 * BLK), 1) & (BLK - 1)
    dist = qi + BLK - kj
    local = (dist >= 0) & (dist < BLK)
    return local if has_prev is True else local & ((kj >= BLK) | has_prev)


def _key_blocks(sb, prev_ref, cur_ref):
    prev = prev_ref[...] if sb == 0 else cur_ref[BLK * (sb - 1):BLK * sb, :]
    return prev, cur_ref[BLK * sb:BLK * (sb + 1), :]


def _sink_lanes(sink_ref, g):
    lane = lax.broadcasted_iota(jnp.int32, (1, 4 * BLK), 1)
    return jnp.where(lane < BLK, sink_ref[0, 4 * g],
                     jnp.where(lane < 2 * BLK, sink_ref[0, 4 * g + 1],
                               jnp.where(lane < 3 * BLK, sink_ref[0, 4 * g + 2], sink_ref[0, 4 * g + 3])))


def _unstack_t(t):
    return [t[:, BLK * h:BLK * (h + 1)].T for h in range(4)]


def _stack_heads(t, g):
    return jnp.concatenate([t[:, HD * (4 * g + h):HD * (4 * g + h + 1)] for h in range(4)], axis=0)


def _band(prev, cur, g):
    return jnp.concatenate([prev[:, HD * g:HD * (g + 1)], cur[:, HD * g:HD * (g + 1)]], axis=0)


def _softmax_band(qs, kb, mask, sink):
    s = lax.dot_general(kb, qs, (((1,), (1,)), ((), ())), preferred_element_type=F32) * (HD ** -0.5)
    s = jnp.where(mask, s, NEG)
    m = jnp.maximum(jnp.max(s, axis=0, keepdims=True), sink)
    e = jnp.exp(s - m)
    es = jnp.exp(sink - m)
    inv = 1.0 / (jnp.sum(e, axis=0, keepdims=True) + es)
    return e * inv, es * inv


HALO = 32


RC = 64
LC = 128


def _windows(ext_ref, r0, l0, base):
    col = ext_ref[pl.ds(r0, RC + HALO), pl.ds(l0, LC)]
    for s in range(8):
        rolled = col if s == 0 else pltpu.roll(col, RC + HALO - s, 0)
        for t in range(CK):
            if (base + t) % 8 == s:
                a8 = base + t - s
                yield t, rolled[a8:a8 + RC]


def _taps(ext_ref, r0, l0, base, cw_ref, flip):
    acc = None
    for t, win in _windows(ext_ref, r0, l0, base):
        k = CK - 1 - t if flip else t
        term = win * cw_ref[k:k + 1, pl.ds(l0, LC)]
        acc = term if acc is None else acc + term
    return acc


NSUB = (BLK // RC) * (CW // LC)


def _sub_tile(t, row_base):
    r0 = pl.multiple_of(row_base + (t // (CW // LC)) * RC, RC)
    l0 = pl.multiple_of((t % (CW // LC)) * LC, LC)
    return r0, l0


FR = QB * BLK


def _attn_conv_fwd_call(sinks, qr, kr, vb, z, gb, cwf, conv_b, ln_w, ln_b):
    def body(sink_ref, q_ref, kp_ref, kc_ref, vp_ref, vc_ref, z_ref, zh_ref, gb_ref, cw_ref, cb_ref, lw_ref, lb_ref,
             o_ref, zc_ref, yb_ref, zext):
        i = pl.program_id(0)
        zext[0:HALO, :] = jnp.where(i > 0, zh_ref[...], 0.0)
        zext[HALO:HALO + FR, :] = z_ref[...]
        for sb in range(QB):
            rows = slice(BLK * sb, BLK * (sb + 1))
            mask = _band_mask(i > 0 if sb == 0 else True)
            q = q_ref[rows, :]
            kp, kc = _key_blocks(sb, kp_ref, kc_ref)
            vp, vc = _key_blocks(sb, vp_ref, vc_ref)
            for g in range(NKV):
                p, _ = _softmax_band(_stack_heads(q, g), _band(kp, kc, g), mask, _sink_lanes(sink_ref, g))
                o_t = lax.dot_general(_band(vp, vc, g), p.astype(BF), (((0,), (0,)), ((), ())),
                                      preferred_element_type=F32)
                for h, o_h in enumerate(_unstack_t(o_t)):
                    o_ref[rows, HD * (4 * g + h):HD * (4 * g + h + 1)] = o_h
            for r0 in range(BLK * sb, BLK * (sb + 1), RC):
                for l0 in range(0, CW, LC):
                    acc = _taps(zext, r0, l0, HALO - (CK - 1), cw_ref, False)
                    zc_ref[r0:r0 + RC, l0:l0 + LC] = acc + cb_ref[:, l0:l0 + LC]
            zc = zc_ref[rows, :]
            mu = jnp.mean(zc, axis=-1, keepdims=True)
            dz = zc - mu
            rstd = lax.rsqrt(jnp.mean(dz * dz, axis=-1, keepdims=True) + EPS)
            zn = dz * rstd * lw_ref[...] + lb_ref[...]
            gbv = gb_ref[rows, :]
            yb_ref[rows, :] = ((zn * _sigmoid(zn)) * (gbv * _sigmoid(gbv))).astype(BF)

    prev = lambda i: (jnp.maximum(QB * i - 1, 0), 0)
    halo = lambda i: (jnp.maximum(FR // HALO * i - 1, 0), 0)
    f512 = pl.BlockSpec((FR, 512), _row)
    f128 = pl.BlockSpec((FR, KVW), _row)
    c512 = pl.BlockSpec((1, CW), _const)
    return pl.pallas_call(
        body, name="attn_conv_fwd", grid=(NQB,),
        out_shape=[jax.ShapeDtypeStruct((S, AW), F32), jax.ShapeDtypeStruct((S, CW), F32),
                   jax.ShapeDtypeStruct((S, CW), BF)],
        in_specs=[_SMEM, f512, pl.BlockSpec((BLK, KVW), prev), f128, pl.BlockSpec((BLK, KVW), prev), f128,
                  f512, pl.BlockSpec((HALO, CW), halo), f512, pl.BlockSpec((CKP, CW), _const), c512, c512, c512],
        out_specs=[f512, f512, f512],
        scratch_shapes=[pltpu.VMEM((FR + HALO, CW), F32)],
        compiler_params=_params(True),
    )(sinks, qr, kr, kr, vb, vb, z, z, gb, cwf, conv_b, ln_w, ln_b)


def _ln_gate_bwd(d_yb, zc, gbv, lw, lb):
    mu = jnp.mean(zc, axis=-1, keepdims=True)
    dz = zc - mu
    rstd = lax.rsqrt(jnp.mean(dz * dz, axis=-1, keepdims=True) + EPS)
    zh = dz * rstd
    zn = zh * lw + lb
    act_n, dact_n = _silu_and_grad(zn)
    act_g, dact_g = _silu_and_grad(gbv)
    d_gb = d_yb * act_n * dact_g
    d_zn = d_yb * act_g * dact_n
    dzh = d_zn * lw
    d_zc = rstd * (dzh - jnp.mean(dzh, axis=-1, keepdims=True) - zh * jnp.mean(dzh * zh, axis=-1, keepdims=True))
    return d_zc, d_gb, d_zn, zh


def _attn_conv_bwd_call(sinks, qr, kr, vb, d_ya, ga, o, qraw, qw_t, cos_t, sin_t, bq, d_yb, zc, gb, z, a, g, cwf,
                        ln_w, ln_b):
    def body(sink_ref, q_ref, kp_ref, kc_ref, vp_ref, vc_ref, dya_ref, ga_ref, o_ref, qraw_ref, qw_ref,
             cos_ref, sin_ref, bq_ref,
             dyb_ref, dybn_ref, zc_ref, zcn_ref, gb_ref, gbn_ref, z_ref, zh_ref, a_ref, g_ref, cw_ref, lw_ref, lb_ref,
             dqraw_ref, dga_ref, dk_ref, dv_ref, gqw_ref, gsink_ref,
             dgb_ref, da_ref, dg_ref, gcw_ref, glw_ref, glb_ref, gcb_ref,
             dext, zext, gacc):
        i = pl.program_id(0)

        @pl.when(i == 0)
        def _():
            dk_ref[...] = jnp.zeros((S, KVW), F32)
            dv_ref[...] = jnp.zeros((S, KVW), F32)
            gqw_ref[...] = jnp.zeros((1, AW), F32)
            gsink_ref[...] = jnp.zeros((1, 128), F32)
            gacc[...] = jnp.zeros((CKP * 8, CW), F32)
            glw_ref[...] = jnp.zeros((1, CW), F32)
            glb_ref[...] = jnp.zeros((1, CW), F32)
            gcb_ref[...] = jnp.zeros((1, CW), F32)

        lw = lw_ref[...]
        lb = lb_ref[...]

        def ln_rows(rows):
            d_zc, d_gb, d_zn, zh = _ln_gate_bwd(dyb_ref[rows, :], zc_ref[rows, :], gb_ref[rows, :], lw, lb)
            dgb_ref[rows, :] = d_gb.astype(BF)
            glw_ref[...] += jnp.sum(d_zn * zh, axis=0, keepdims=True)
            glb_ref[...] += jnp.sum(d_zn, axis=0, keepdims=True)
            gcb_ref[...] += jnp.sum(d_zc, axis=0, keepdims=True)
            dext[rows, :] = d_zc

        ln_rows(slice(0, BLK))
        zext[0:HALO, :] = jnp.where(i > 0, zh_ref[...], 0.0)
        zext[HALO:HALO + FR, :] = z_ref[...]

        lane = lax.broadcasted_iota(jnp.int32, (1, 128), 1)
        gsink = jnp.zeros((1, 128), F32)
        dq_rows = []
        for sb in range(QB):
            rows = slice(BLK * sb, BLK * (sb + 1))
            if sb + 1 < QB:
                ln_rows(slice(BLK * (sb + 1), BLK * (sb + 2)))
            else:
                d_zc_next, _, _, _ = _ln_gate_bwd(dybn_ref[...], zcn_ref[...], gbn_ref[...], lw, lb)
                dext[FR:FR + HALO, :] = jnp.where(i < NQB - 1, d_zc_next, 0.0)
            mask = _band_mask(i > 0 if sb == 0 else True)
            q = q_ref[rows, :]
            d_ya = dya_ref[rows, :]
            act, dact = _silu_and_grad(ga_ref[rows, :])
            dga_ref[rows, :] = (d_ya * o_ref[rows, :] * dact).astype(BF)
            d_o = (d_ya * act).astype(BF)
            kp, kc = _key_blocks(sb, kp_ref, kc_ref)
            vp, vc = _key_blocks(sb, vp_ref, vc_ref)
            dq_parts, dk_parts, dv_parts = [], [], []
            for gi in range(NKV):
                qs = _stack_heads(q, gi)
                kb = _band(kp, kc, gi)
                vbd = _band(vp, vc, gi)
                p, ps = _softmax_band(qs, kb, mask, _sink_lanes(sink_ref, gi))
                dos = _stack_heads(d_o, gi)
                dp = lax.dot_general(vbd, dos, (((1,), (1,)), ((), ())), preferred_element_type=F32)
                dr = jnp.sum(p * dp, axis=0, keepdims=True)
                ds = (p * (dp - dr) * (HD ** -0.5)).astype(BF)
                sink_term = ps * dr
                for h in range(4):
                    part = jnp.sum(sink_term[:, BLK * h:BLK * (h + 1)], axis=1, keepdims=True)
                    gsink = gsink - jnp.where(lane == 4 * gi + h, part, 0.0)
                dv_parts.append(jnp.dot(p.astype(BF), dos, preferred_element_type=F32))
                dk_parts.append(jnp.dot(ds, qs, preferred_element_type=F32))
                dq_t = lax.dot_general(kb, ds, (((0,), (0,)), ((), ())), preferred_element_type=F32)
                dq_parts.extend(_unstack_t(dq_t))
            dkb = jnp.concatenate(dk_parts, axis=1)
            dvb = jnp.concatenate(dv_parts, axis=1)
            blk = QB * i + sb
            r_prev = pl.multiple_of(jnp.maximum(blk - 1, 0) * BLK, BLK)
            r_cur = pl.multiple_of(blk * BLK, BLK)
            dk_ref[pl.ds(r_prev, BLK), :] += dkb[0:BLK]
            dv_ref[pl.ds(r_prev, BLK), :] += dvb[0:BLK]
            dk_ref[pl.ds(r_cur, BLK), :] += dkb[BLK:2 * BLK]
            dv_ref[pl.ds(r_cur, BLK), :] += dvb[BLK:2 * BLK]
            dq_rows.append(jnp.concatenate(dq_parts, axis=1))

            def taps_sub(t, carry, sb=sb):
                r0, l0 = _sub_tile(t, BLK * sb)
                here = (pl.ds(r0, RC), pl.ds(l0, LC))
                d_z = _taps(dext, r0, l0, 0, cw_ref, True)
                sg = _sigmoid(g_ref[here])
                da_ref[here] = (d_z * sg).astype(BF)
                dg_ref[here] = (d_z * a_ref[here] * sg * (1.0 - sg)).astype(BF)
                d_sub = dext[here]
                for k, win in _windows(zext, r0, l0, HALO - (CK - 1)):
                    prod = d_sub * win
                    part = prod[0:8]
                    for q8 in range(1, RC // 8):
                        part = part + prod[8 * q8:8 * q8 + 8]
                    gacc[8 * k:8 * k + 8, pl.ds(l0, LC)] += part
                return carry

            lax.fori_loop(0, NSUB, taps_sub, 0)
        gsink_ref[...] += gsink
        dq = jnp.concatenate(dq_rows, axis=0)
        dq_raw, g_qw = _norm_rope_bwd(dq, qraw_ref[...], qw_ref[...], jnp.tile(cos_ref[...], (1, 4)),
                                      jnp.tile(sin_ref[...], (1, 4)), bq_ref)
        dqraw_ref[...] = dq_raw.astype(BF)
        gqw_ref[...] += g_qw

        @pl.when(i == NQB - 1)
        def _():
            for k in range(CK):
                gcw_ref[k:k + 1, :] = jnp.sum(gacc[8 * k:8 * k + 8, :], axis=0, keepdims=True)
            gcw_ref[CK:CKP, :] = jnp.zeros((CKP - CK, CW), F32)

    prev = lambda i: (jnp.maximum(QB * i - 1, 0), 0)
    halo_prev = lambda i: (jnp.maximum(FR // HALO * i - 1, 0), 0)
    halo_next = lambda i: (jnp.minimum(FR // HALO * (i + 1), S // HALO - 1), 0)
    f512 = pl.BlockSpec((FR, 512), _row)
    f128 = pl.BlockSpec((FR, 128), _row)
    hn = pl.BlockSpec((HALO, CW), halo_next)
    c512 = pl.BlockSpec((1, CW), _const)
    sds = jax.ShapeDtypeStruct
    vec = sds((1, CW), F32)
    return pl.pallas_call(
        body, name="attn_conv_bwd", grid=(NQB,),
        out_shape=[sds((S, AW), BF), sds((S, AW), BF), sds((S, KVW), F32), sds((S, KVW), F32), sds((1, AW), F32),
                   sds((1, 128), F32),
                   sds((S, CW), BF), sds((S, CW), BF), sds((S, CW), BF), sds((CKP, CW), F32), vec, vec, vec],
        in_specs=[_SMEM, f512, pl.BlockSpec((BLK, KVW), prev), f128, pl.BlockSpec((BLK, KVW), prev), f128,
                  f512, f512, f512, f512, pl.BlockSpec((1, AW), _const), f128, f128, pl.BlockSpec((AW, AW), _const),
                  f512, hn, f512, hn, f512, hn, f512, pl.BlockSpec((HALO, CW), halo_prev), f512, f512,
                  pl.BlockSpec((CKP, CW), _const), c512, c512],
        out_specs=[f512, f512, _VMEM, _VMEM, _VMEM, _VMEM, f512, f512, f512, _VMEM, _VMEM, _VMEM, _VMEM],
        scratch_shapes=[pltpu.VMEM((FR + HALO, CW), F32), pltpu.VMEM((FR + HALO, CW), F32),
                        pltpu.VMEM((CKP * 8, CW), F32)],
        compiler_params=_params(True),
    )(sinks, qr, kr, kr, vb, vb, d_ya, ga, o, qraw, qw_t, cos_t, sin_t, bq,
      d_yb, d_yb, zc, zc, gb, gb, z, z, a, g, cwf, ln_w, ln_b)


def _out_loss_call(o, ga, yb, x2, tgt, mod, w_out_full):
    def body(o_ref, ga_ref, yb_ref, x_ref, t_ref, mod_ref, w_ref,
             dout_ref, dya_ref, dyb_ref, gw_ref, loss_ref, dgate_ref):
        i = pl.program_id(0)

        @pl.when(i == 0)
        def _():
            gw_ref[...] = jnp.zeros((D, D), F32)
            loss_ref[...] = jnp.zeros((1, 128), F32)
            dgate_ref[...] = jnp.zeros((1, D), F32)

        gav = ga_ref[...]
        ya = o_ref[...] * (gav * _sigmoid(gav))
        ycat = jnp.concatenate([ya.astype(BF), yb_ref[...]], axis=1)
        w = w_ref[...]
        y = jnp.dot(ycat, w, preferred_element_type=F32)
        gate = mod_ref[:, 2 * D:3 * D]
        diff = x_ref[...] + gate * y - t_ref[...]
        sq = jnp.sum(jnp.sum(diff * diff, axis=1, keepdims=True), axis=0, keepdims=True)
        loss_ref[...] += jnp.broadcast_to(sq, (1, 128))
        d_out = diff * (1.0 / D)
        dout_ref[...] = d_out
        dgate_ref[...] += jnp.sum(d_out * y, axis=0, keepdims=True)
        dy = (d_out * gate).astype(BF)
        d_ycat = lax.dot_general(dy, w, (((1,), (1,)), ((), ())), preferred_element_type=F32)
        dya_ref[...] = d_ycat[:, 0:AW]
        dyb_ref[...] = d_ycat[:, AW:D]
        gw_ref[...] += lax.dot_general(ycat, dy, (((0,), (0,)), ((), ())), preferred_element_type=F32)

    t512 = pl.BlockSpec((TS, 512), _row)
    t1024 = pl.BlockSpec((TS, D), _row)
    return pl.pallas_call(
        body, name="out_loss", grid=(NT,),
        out_shape=[jax.ShapeDtypeStruct((S, D), F32), jax.ShapeDtypeStruct((S, AW), F32),
                   jax.ShapeDtypeStruct((S, CW), F32), jax.ShapeDtypeStruct((D, D), F32),
                   jax.ShapeDtypeStruct((1, 128), F32), jax.ShapeDtypeStruct((1, D), F32)],
        in_specs=[t512, t512, t512, t1024, t1024, pl.BlockSpec((1, 3 * D), _const),
                  pl.BlockSpec((D, D), _const, pipeline_mode=pl.Buffered(1))],
        out_specs=[t1024, t512, t512, _VMEM, _VMEM, _VMEM],
        compiler_params=_params(True),
    )(o, ga, yb, x2, tgt, mod, w_out_full)


SM_ROWS = 8
PIECES = ((0, 512), (512, 640), (640, 768), (768, 1280), (1280, 1792), (1792, 2304), (2304, 2816))


def _bwd_in_call(dqraw, dk, dv, dga, da, dg, dgb, kraw, kw_t, cos_t, sin_t, bk, h, wt_full, x2, d_out, mod, norm_w,
                 gw_out, gcw, glw, glb, gcb, gqw, gsink, dgate, loss_p, cact_all):
    def body(dq_ref, dk_ref, dv_ref, dga_ref, da_ref, dg_ref, dgb_ref, kraw_ref, kw_ref, cos_ref, sin_ref, bk_ref,
             h_ref, wt_ref, x_ref, dout_ref, mod_ref, nw_ref, gwout_ref, gcw_ref, glw_ref, glb_ref, gcb_ref, gqw_ref,
             gsink_ref, dgate_ref, loss_ref, cact_ref,
             gx_ref, o_gwin, o_gwout, o_gwada, o_gbada, o_gnw, o_gqw, o_gkw, o_gsink, o_gcw, o_gcb, o_glw, o_glb,
             o_loss,
             acc, win_send, win_sib, win_ici, wout_send, wout_sib, wout_ici, sm_buf, cw_buf, dmod_all, vec_acc, gkw_acc,
             wi_ds, wi_dr, wi_is, wi_ir, wo_ds, wo_dr, wo_is, wo_ir, sm_s, sm_r, cw_s, cw_r):
        i = pl.program_id(0)
        pos = _mesh_pos()
        x, y, cc = pos
        me = 4 * x + 2 * y + cc

        def chip(j):
            return (1 - x if j & 1 else x, 1 - y if j & 2 else y)

        def rows_of(buf, px, py, pc, rows, align):
            return buf.at[pl.ds(pl.multiple_of((4 * px + 2 * py + pc) * rows, align), rows), :]

        bufs = {"in": (win_send, win_sib, win_ici, IN_SHARD, wi_ds, wi_dr, wi_is, wi_ir),
                "out": (wout_send, wout_sib, wout_ici, OUT_SHARD, wo_ds, wo_dr, wo_is, wo_ir)}

        def d2d_copy(j, which):
            send, sib, _, rows, ds_, dr_, _, _ = bufs[which]
            px, py = chip(j)
            return pltpu.make_async_remote_copy(src_ref=rows_of(send, px, py, 1 - cc, rows, 16), dst_ref=sib.at[j],
                                                send_sem=ds_.at[j], recv_sem=dr_.at[j], device_id=(x, y, 1 - cc),
                                                device_id_type=MESH)

        def ici_copy(j, which):
            send, _, ici, rows, _, _, is_, ir_ = bufs[which]
            px, py = chip(j)
            return pltpu.make_async_remote_copy(src_ref=rows_of(send, px, py, cc, rows, 16), dst_ref=ici.at[j - 1],
                                                send_sem=is_.at[j - 1], recv_sem=ir_.at[j - 1], device_id=(px, py, cc),
                                                device_id_type=MESH)

        def level2(which, partial_ref):
            send, sib, _, rows, _, _, _, _ = bufs[which]
            for j in range(1, 4):
                d2d_copy(j, which).wait_recv()
                px, py = chip(j)
                mine = rows_of(partial_ref, px, py, cc, rows, 8)[...]
                rows_of(send, px, py, cc, rows, 16)[...] = (mine + sib[j].astype(F32)).astype(BF)
                ici_copy(j, which).start()

        def finish(which, partial_ref):
            _, sib, ici, rows, _, _, _, _ = bufs[which]
            d2d_copy(0, which).wait_recv()
            total = rows_of(partial_ref, x, y, cc, rows, 8)[...] + sib[0].astype(F32)
            for j in range(1, 4):
                ici_copy(j, which).wait_recv()
                total = total + ici[j - 1].astype(F32)
            for j in range(4):
                d2d_copy(j, which).wait_send()
            for j in range(1, 4):
                ici_copy(j, which).wait_send()
            return total

        def dproj_pieces():
            dk_raw, g_kw = _norm_rope_bwd(dk_ref[...], kraw_ref[...], kw_ref[...], cos_ref[...], sin_ref[...], bk_ref)
            return [dq_ref[...], dk_raw.astype(BF), dv_ref[...].astype(BF), dga_ref[...], da_ref[...], dg_ref[...],
                    dgb_ref[...]], g_kw

        @pl.when(i == 0)
        def _():
            acc[...] = jnp.zeros((INW, D), F32)
            vec_acc[...] = jnp.zeros((8, D), F32)
            gkw_acc[...] = jnp.zeros((1, KVW), F32)
            wout_send[...] = gwout_ref[...].astype(BF)
            for j in range(4):
                d2d_copy(j, "out").start()

        @pl.when(i == 2)
        def _():
            level2("out", gwout_ref)

        @pl.when(i < NT)
        def _():
            pieces, g_kw = dproj_pieces()
            gkw_acc[...] += g_kw
            hv = h_ref[...]
            for (lo, hi), piece in zip(PIECES, pieces):
                acc[lo:hi, :] += lax.dot_general(piece, hv, (((0,), (0,)), ((), ())), preferred_element_type=F32)

        @pl.when(i == NT - 1)
        def _():
            for lo, hi in PIECES:
                win_send[lo:hi, :] = acc[lo:hi, :].astype(BF)
            for j in range(4):
                d2d_copy(j, "in").start()

        @pl.when(i == NT + 2)
        def _():
            level2("in", acc)

        @pl.when(i >= NT)
        def _():
            pieces, _ = dproj_pieces()
            dproj = jnp.concatenate(pieces, axis=1)
            d_h = jnp.dot(dproj, wt_ref[...], preferred_element_type=F32)
            xv = x_ref[...]
            scale = mod_ref[:, D:2 * D]
            nw = nw_ref[...]
            r = lax.rsqrt(jnp.mean(xv * xv, axis=-1, keepdims=True) + EPS)
            xn = xv * r
            vec_acc[0:1, :] += jnp.sum(d_h, axis=0, keepdims=True)
            vec_acc[1:2, :] += jnp.sum(d_h * (xn * nw), axis=0, keepdims=True)
            d_u = d_h * (1.0 + scale)
            vec_acc[2:3, :] += jnp.sum(d_u * xn, axis=0, keepdims=True)
            d_xn = d_u * nw
            gx_ref[...] = dout_ref[...] + r * (d_xn - xn * jnp.mean(d_xn * xn, axis=-1, keepdims=True))

        @pl.when(i == 2 * NT - 1)
        def _():
            ch_sm = (_slab(sm_buf), sm_s, sm_r)
            ch_cw = (_slab(cw_buf), cw_s, cw_r)
            z128 = jnp.zeros((1, 128), F32)
            row4 = jnp.concatenate([glw_ref[...], glb_ref[...]], axis=1)
            row5 = jnp.concatenate([gcb_ref[...], gqw_ref[...]], axis=1)
            row6 = jnp.concatenate([gkw_acc[...], gsink_ref[...], loss_ref[...]] + [z128] * 5, axis=1)
            sm_buf[me] = jnp.concatenate([vec_acc[0:2, :], dgate_ref[...], vec_acc[2:3, :], row4, row5, row6,
                                          jnp.zeros((1, D), F32)], axis=0)
            f_sm = _ag_start(ch_sm, pos)
            cw_buf[me] = gcw_ref[...]
            f_cw = _ag_start(ch_cw, pos)
            _ag_finish(ch_sm, pos, f_sm)
            _ag_finish(ch_cw, pos, f_cw)
            tot = sm_buf[0]
            cw_tot = cw_buf[0]
            for d in range(1, NDEV):
                tot = tot + sm_buf[d]
                cw_tot = cw_tot + cw_buf[d]
            o_gbada[...] = jnp.concatenate([tot[0:1, :], tot[1:2, :], tot[2:3, :]], axis=1)
            o_gnw[...] = tot[3:4, :]
            o_glw[...] = tot[4:5, 0:CW]
            o_glb[...] = tot[4:5, CW:D]
            o_gcb[...] = tot[5:6, 0:CW]
            gq = tot[5:6, CW:CW + HD]
            for hh in range(1, NQ):
                gq = gq + tot[5:6, CW + HD * hh:CW + HD * (hh + 1)]
            o_gqw[...] = gq
            o_gkw[...] = tot[6:7, 0:HD] + tot[6:7, HD:2 * HD]
            o_gsink[...] = tot[6:7, 128:128 + NQ]
            o_loss[...] = tot[6:7, 256:384] * (0.5 / D)
            mine = jnp.zeros((CK, CONV_SHARD), F32)
            for d in range(NDEV):
                mine = mine + jnp.where(me == d, cw_tot[0:CK, CONV_SHARD * d:CONV_SHARD * (d + 1)], 0.0)
            for k in range(CK):
                o_gcw[k] = mine[k:k + 1, :]
            for d in range(NDEV):
                dmod_all[d:d + 1, :] = jnp.concatenate([sm_buf[d, 0:1, :], sm_buf[d, 1:2, :], sm_buf[d, 2:3, :]],
                                                       axis=1)
            col0 = pl.multiple_of(me * ADA_SHARD, 128)
            o_gwada[...] = lax.dot_general(cact_ref[...], dmod_all[:, pl.ds(col0, ADA_SHARD)], (((0,), (0,)), ((), ())),
                                           preferred_element_type=F32, precision=lax.Precision.HIGHEST)

            o_gwout[...] = finish("out", gwout_ref)
            o_gwin[...] = finish("in", acc)

    half = lambda i: (i % NT, 0)
    late = lambda i: (jnp.maximum(i - NT, 0), 0)
    t512 = pl.BlockSpec((TS, 512), half)
    t128 = pl.BlockSpec((TS, 128), half)
    l1024 = pl.BlockSpec((TS, D), late)
    sem7 = pltpu.SemaphoreType.DMA((7,))
    sem4 = pltpu.SemaphoreType.DMA((4,))
    sem3 = pltpu.SemaphoreType.DMA((3,))
    sds = jax.ShapeDtypeStruct
    return pl.pallas_call(
        body, name="bwd_in", grid=(2 * NT,),
        out_shape=[sds((S, D), F32), sds((IN_SHARD, D), F32), sds((OUT_SHARD, D), F32), sds((D, ADA_SHARD), F32),
                   sds((1, 3 * D), F32), sds((1, D), F32), sds((1, HD), F32), sds((1, HD), F32), sds((1, NQ), F32),
                   sds((CK, 1, CONV_SHARD), F32), sds((1, CW), F32), sds((1, CW), F32), sds((1, CW), F32),
                   sds((1, 128), F32)],
        in_specs=[t512, t128, t128, t512, t512, t512, t512, t128, pl.BlockSpec((1, KVW), _const), t128, t128,
                  pl.BlockSpec((KVW, KVW), _const), pl.BlockSpec((TS, D), half),
                  pl.BlockSpec((INW, D), _const, pipeline_mode=pl.Buffered(1)), l1024, l1024,
                  pl.BlockSpec((1, 3 * D), _const), pl.BlockSpec((1, D), _const)] + [_VMEM] * 10,
        out_specs=[l1024] + [_VMEM] * 13,
        scratch_shapes=[pltpu.VMEM((INW, D), F32), pltpu.VMEM((INW, D), BF), pltpu.VMEM((4, IN_SHARD, D), BF),
                        pltpu.VMEM((3, IN_SHARD, D), BF), pltpu.VMEM((D, D), BF), pltpu.VMEM((4, OUT_SHARD, D), BF),
                        pltpu.VMEM((3, OUT_SHARD, D), BF),
                        pltpu.VMEM((NDEV, SM_ROWS, D), F32), pltpu.VMEM((NDEV, CKP, CW), F32),
                        pltpu.VMEM((NDEV, 3 * D), F32), pltpu.VMEM((8, D), F32), pltpu.VMEM((1, KVW), F32)]
        + [sem4, sem4, sem3, sem3] * 2 + [sem7] * 4,
        compiler_params=pltpu.CompilerParams(dimension_semantics=("arbitrary",), vmem_limit_bytes=BIG_VMEM_LIMIT),
    )(dqraw, dk, dv, dga, da, dg, dgb, kraw, kw_t, cos_t, sin_t, bk, h, wt_full, x2, d_out, mod, norm_w,
      gw_out, gcw, glw, glb, gcb, gqw, gsink, dgate, loss_p, cact_all)


ADAM_STEPS = 8


def _adam_chunking(arr):
    if arr.ndim == 2 and arr.shape[0] % (8 * ADAM_STEPS) == 0:
        return "rows"
    if arr.ndim == 2 and arr.shape[1] % (128 * ADAM_STEPS) == 0:
        return "cols"
    return None


def _adam_call(ws, gs, ms, vs, grad_x, loss_v):
    n = len(ws)
    bc1 = 1.0 - ADAM_B1 ** ADAM_STEP
    bc2 = 1.0 - ADAM_B2 ** ADAM_STEP
    chunked = [_adam_chunking(w) for w in ws]

    def body(*refs):
        ins, outs = refs[:4 * n + 2], refs[4 * n + 2:]
        i = pl.program_id(0)

        def update(j):
            w, g, m, v = (ins[j][...], ins[n + j][...], ins[2 * n + j][...], ins[3 * n + j][...])
            m_new = ADAM_B1 * m + (1.0 - ADAM_B1) * g
            v_new = ADAM_B2 * v + (1.0 - ADAM_B2) * (g * g)
            m_hat = m_new / bc1
            v_hat = v_new / bc2
            outs[j][...] = g
            outs[n + j][...] = -ADAM_LR * (m_hat / (jnp.sqrt(v_hat) + ADAM_EPS) + ADAM_WD * w)
            outs[2 * n + j][...] = m_new
            outs[3 * n + j][...] = v_new

        for j in range(n):
            if chunked[j]:
                update(j)
        outs[4 * n][...] = ins[4 * n][...]

        @pl.when(i == 0)
        def _():
            for j in range(n):
                if not chunked[j]:
                    update(j)
            outs[4 * n + 1][...] = ins[4 * n + 1][...]

    def spec(arr, how):
        if how == "rows":
            return pl.BlockSpec((arr.shape[0] // ADAM_STEPS, arr.shape[1]), _row)
        if how == "cols":
            return pl.BlockSpec((arr.shape[0], arr.shape[1] // ADAM_STEPS), lambda i: (0, i))
        zeros = (0,) * arr.ndim
        return pl.BlockSpec(arr.shape, lambda i: zeros)

    par_specs = [spec(w, ch) for w, ch in zip(ws, chunked)]
    extra = [spec(grad_x, "rows"), spec(loss_v, None)]
    shapes = [jax.ShapeDtypeStruct(w.shape, F32) for w in ws]
    res = pl.pallas_call(
        body, name="adam", grid=(ADAM_STEPS,),
        out_shape=shapes * 4 + [jax.ShapeDtypeStruct(grad_x.shape, F32), jax.ShapeDtypeStruct(loss_v.shape, F32)],
        in_specs=par_specs * 4 + extra, out_specs=par_specs * 4 + extra,
        compiler_params=_params(True),
    )(*ws, *gs, *ms, *vs, grad_x, loss_v)
    return res[0:n], res[n:2 * n], res[2 * n:3 * n], res[3 * n:4 * n], res[4 * n], res[4 * n + 1]


def _rope_tables():
    inv = (np.float32(ROPE_THETA) ** (-np.arange(0, HD, 2, dtype=np.float32) / np.float32(HD))).astype(np.float32)
    ang = (np.arange(S, dtype=np.float32)[:, None] * inv[None, :]).astype(np.float32)
    cos, sin = np.cos(ang).astype(np.float32), np.sin(ang).astype(np.float32)
    cos64 = np.concatenate([cos, cos], axis=-1)
    sin64 = np.concatenate([-sin, sin], axis=-1)
    return jnp.asarray(np.tile(cos64, (1, 2))), jnp.asarray(np.tile(sin64, (1, 2)))


def _group_matrix(width):
    idx = np.arange(width) // HD
    return jnp.asarray(np.where(idx[:, None] == idx[None, :], 1.0 / HD, 0.0).astype(np.float32)).astype(BF)


def kernel(x, c, w_ada, b_ada, norm_w, w_in, q_norm_w, k_norm_w, sinks, conv_w, conv_b, ln_w, ln_b, w_out, loss_target, m_w_ada, m_b_ada, m_norm_w, m_w_in, m_q_norm_w, m_k_norm_w, m_sinks, m_conv_w, m_conv_b, m_ln_w, m_ln_b, m_w_out, v_w_ada, v_b_ada, v_norm_w, v_w_in, v_q_norm_w, v_k_norm_w, v_sinks, v_conv_w, v_conv_b, v_ln_w, v_ln_b, v_w_out):
    x2 = x[0]
    tgt = loss_target[0]
    cos_t, sin_t = _rope_tables()
    bq = _group_matrix(AW)
    bk = _group_matrix(KVW)
    qw_t = jnp.tile(q_norm_w, (1, NQ))
    kw_t = jnp.tile(k_norm_w, (1, NKV))

    tr = lambda t: jnp.swapaxes(t[0], 0, 1)
    tc = lambda t: jnp.swapaxes(t, 0, 1)
    (wt_full, w_out_full, cwf, mod, cact_all, h, qraw, kraw, ga, a, g, gb, qr, kr, vb, z) = _gather_fwd_call(
        tr(w_in), w_out[0], tc(conv_w), c, w_ada[0], b_ada, x2, norm_w, qw_t, kw_t, cos_t, sin_t, bq, bk)
    o, zc, yb = _attn_conv_fwd_call(sinks, qr, kr, vb, z, gb, cwf, conv_b, ln_w, ln_b)
    d_out, d_ya, d_yb, gw_out, loss_p, dgate = _out_loss_call(o, ga, yb, x2, tgt, mod, w_out_full)

    dqraw, dga, dk, dv, gqw, gsink, dgb, da, dg, gcw, glw, glb, gcb = _attn_conv_bwd_call(
        sinks, qr, kr, vb, d_ya, ga, o, qraw, qw_t, cos_t, sin_t, bq, d_yb, zc, gb, z, a, g, cwf, ln_w, ln_b)
    (grad_x, g_w_in_t, g_w_out, g_w_ada, g_b_ada, g_norm_w, g_qw, g_kw, g_sinks, g_conv_w, g_conv_b, g_ln_w, g_ln_b,
     loss_v) = _bwd_in_call(dqraw, dk, dv, dga, da, dg, dgb, kraw, kw_t, cos_t, sin_t, bk, h, wt_full, x2, d_out, mod,
                            norm_w, gw_out, gcw, glw, glb, gcb, gqw, gsink, dgate, loss_p, cact_all)

    ws = [w_ada[0], b_ada, norm_w, tr(w_in), q_norm_w, k_norm_w, sinks, tc(conv_w), conv_b, ln_w, ln_b, w_out[0]]
    gs = [g_w_ada, g_b_ada, g_norm_w, g_w_in_t, g_qw, g_kw, g_sinks, g_conv_w, g_conv_b, g_ln_w, g_ln_b, g_w_out]
    ms = [m_w_ada[0], m_b_ada, m_norm_w, tr(m_w_in), m_q_norm_w, m_k_norm_w, m_sinks, tc(m_conv_w), m_conv_b, m_ln_w,
          m_ln_b, m_w_out[0]]
    vs = [v_w_ada[0], v_b_ada, v_norm_w, tr(v_w_in), v_q_norm_w, v_k_norm_w, v_sinks, tc(v_conv_w), v_conv_b, v_ln_w,
          v_ln_b, v_w_out[0]]
    grads, deltas, new_m, new_v, grad_x, loss_v = _adam_call(ws, gs, ms, vs, grad_x, loss_v)
    shaped = [w_ada, b_ada, norm_w, w_in, q_norm_w, k_norm_w, sinks, conv_w, conv_b, ln_w, ln_b, w_out]
    W_IN_POS, CONV_W_POS = 3, 7

    def like(vals):
        vals = [jnp.swapaxes(v, 0, 1) if j in (W_IN_POS, CONV_W_POS) else v for j, v in enumerate(vals)]
        return [v.reshape(s.shape) for v, s in zip(vals, shaped)]

    return (loss_v[0, 0], grad_x[None], *like(grads), *like(deltas), *like(new_m), *like(new_v))
```

```python
import functools

import jax
import jax.numpy as jnp
import numpy as np
from jax import lax
from jax.experimental import pallas as pl
from jax.experimental.pallas import tpu as pltpu

S = 2048
D = 1024
NDEV = 8
HD = 64
NQ = 8
NKV = 2
AW = 512
KVW = 128
CW = 512
INW = 2816
IN_SHARD = INW // NDEV
ADA_SHARD = 3 * D // NDEV
OUT_SHARD = D // NDEV
CONV_SHARD = CW // NDEV
CK = 31
CKP = 32
BLK = 128
TS = 256
NT = S // TS
NB = S // BLK
EPS = 1e-6
ROPE_THETA = 10000.0
NEG = -1e30
BF = jnp.bfloat16
F32 = jnp.float32

ADAM_LR = 0.001
ADAM_B1 = 0.9
ADAM_B2 = 0.999
ADAM_EPS = 1e-08
ADAM_WD = 0.01
ADAM_STEP = 10

VMEM_LIMIT = 56 * 1024 * 1024
BIG_VMEM_LIMIT = 62 * 1024 * 1024
MESH = pl.DeviceIdType.MESH

_VMEM = pl.BlockSpec(memory_space=pltpu.VMEM)
_SMEM = pl.BlockSpec(memory_space=pltpu.SMEM)
_ANY = pl.BlockSpec(memory_space=pl.ANY)


def _params(grid=False):
    if grid:
        return pltpu.CompilerParams(dimension_semantics=("arbitrary",), vmem_limit_bytes=VMEM_LIMIT)
    return pltpu.CompilerParams(vmem_limit_bytes=VMEM_LIMIT)


def _row(i):
    return (i, 0)


def _const(i):
    return (0, 0)


def _sigmoid(t):
    return 1.0 / (1.0 + jnp.exp(-t))


def _silu_and_grad(t):
    sg = _sigmoid(t)
    return t * sg, sg * (1.0 + t * (1.0 - sg))


def _group_mean(t, b_ref):
    hi = t.astype(BF)
    lo = (t - hi.astype(F32)).astype(BF)
    b = b_ref[...]
    return jnp.dot(hi, b, preferred_element_type=F32) + jnp.dot(lo, b, preferred_element_type=F32)


def _partner(t):
    w = t.shape[-1]
    lane = lax.broadcasted_iota(jnp.int32, t.shape, 1)
    first = (lane & 32) == 0
    return jnp.where(first, pltpu.roll(t, w - 32, 1), pltpu.roll(t, 32, 1))


def _norm_rope_fwd(t, w_t, cos, sin, b_ref):
    r = lax.rsqrt(_group_mean(t * t, b_ref) + EPS)
    tn = t * r * w_t
    return tn * cos + _partner(tn) * sin


def _norm_rope_bwd(d_out, t, w_t, cos, sin, b_ref):
    d_tn = d_out * cos + _partner(d_out * sin)
    r = lax.rsqrt(_group_mean(t * t, b_ref) + EPS)
    th = t * r
    g_w = jnp.sum(d_tn * th, axis=0, keepdims=True)
    d_th = d_tn * w_t
    d_t = r * (d_th - th * _group_mean(d_th * th, b_ref))
    return d_t, g_w


def _mesh_pos():
    return lax.axis_index("x"), lax.axis_index("y"), lax.axis_index("c")


def _ag_copy(chan, k, block, to):
    blk, send_sems, recv_sems = chan
    ref = blk(*block)
    return pltpu.make_async_remote_copy(src_ref=ref, dst_ref=ref, send_sem=send_sems.at[k],
                                        recv_sem=recv_sems.at[k], device_id=to, device_id_type=MESH)


def _ag_start(chan, pos):
    x, y, c = pos
    me = (x, y, c)
    chips = [(1 - x, y), (x, 1 - y), (1 - x, 1 - y)]
    first = [_ag_copy(chan, 0, me, (x, y, 1 - c))]
    first += [_ag_copy(chan, 1 + j, me, (*chip, c)) for j, chip in enumerate(chips)]
    for cp in first:
        cp.start()
    return first


def _ag_finish(chan, pos, first):
    x, y, c = pos
    me = (x, y, c)
    sibling = (x, y, 1 - c)
    chips = [(1 - x, y), (x, 1 - y), (1 - x, 1 - y)]
    passed = [_ag_copy(chan, 4 + j, (*chip, c), sibling) for j, chip in enumerate(chips)]
    for j, chip in enumerate(chips):
        _ag_copy(chan, 1 + j, (*chip, c), me).wait_recv()
        passed[j].start()
    _ag_copy(chan, 0, sibling, me).wait_recv()
    for j, chip in enumerate(chips):
        _ag_copy(chan, 4 + j, (*chip, 1 - c), me).wait_recv()
    for cp in first + passed:
        cp.wait_send()


def _slab(buf):
    return lambda px, py, pc: buf.at[4 * px + 2 * py + pc]


def _row_block(buf, rows, align):
    return lambda px, py, pc: buf.at[pl.ds(pl.multiple_of((4 * px + 2 * py + pc) * rows, align), rows), :]


HALF = INW // 2


def _gather_fwd_call(w_in_t, w_out_s, conv_w_s, c, w_ada_s, b_ada, x2, norm_w, qw_t, kw_t, cos_t, sin_t, bq, bk):
    def body(win_ref, wout_ref, cw_ref, c_ref, wada_ref, bada_ref, x_ref, nw_ref, qw_ref, kw_ref, cos_ref, sin_ref,
             bq_ref, bk_ref,
             wtf_hbm, woutf_hbm, cwf_ref, mod_ref, cact_ref,
             h_ref, qraw_ref, kraw_ref, ga_ref, a_ref, g_ref, gb_ref, qr_ref, kr_ref, vb_ref, z_ref,
             cw_buf, ca_buf, mp_buf, h_s, raw0, pt, wtf_ref, woutf_ref,
             s0, r0, s1, r1, s2, r2, s3, r3, s4, r4, out_sems):
        s = pl.program_id(0)
        pos = _mesh_pos()
        x, y, cc = pos
        me3 = (x, y, cc)
        me = 4 * x + 2 * y + cc
        sibling = (x, y, 1 - cc)
        chips = [(1 - x, y), (x, 1 - y), (1 - x, 1 - y)]
        ch_win = (_row_block(wtf_ref, IN_SHARD, 16), s0, r0)
        ch_wout = (_row_block(woutf_ref, OUT_SHARD, 16), s1, r1)
        ch_cw = (_slab(cw_buf), s2, r2)
        ch_ca = (_slab(ca_buf), s3, r3)
        ch_mp = (_slab(mp_buf), s4, r4)

        def first(chan, j):
            return _ag_copy(chan, j, me3, sibling if j == 0 else (*chips[j - 1], cc))

        def passed(chan, j):
            return _ag_copy(chan, 4 + j, (*chips[j], cc), sibling)

        def landed(chan, j):
            return _ag_copy(chan, 1 + j, (*chips[j], cc), me3)

        def relayed(chan, j):
            return _ag_copy(chan, 4 + j, (*chips[j], 1 - cc), me3)

        def from_sibling(chan):
            return _ag_copy(chan, 0, sibling, me3)

        @pl.when(s == 0)
        def _():
            cv = c_ref[...]
            ca_buf[me] = jnp.broadcast_to(cv * _sigmoid(cv), (8, D))
            f_ca = _ag_start(ch_ca, pos)
            wtf_ref[pl.ds(pl.multiple_of(me * IN_SHARD, 16), IN_SHARD), :] = win_ref[...].astype(BF)
            for j in range(3):
                first(ch_win, j).start()
            cw_buf[me] = cw_ref[:, 0, :]
            f_cw = _ag_start(ch_cw, pos)

            _ag_finish(ch_ca, pos, f_ca)
            cact_all = jnp.concatenate([ca_buf[d, 0:1, :] for d in range(NDEV)], axis=0)
            cact_ref[...] = cact_all
            col0 = pl.multiple_of(me * ADA_SHARD, 128)
            mp_buf[me] = jnp.dot(cact_all, wada_ref[...], preferred_element_type=F32,
                                 precision=lax.Precision.HIGHEST) + bada_ref[:, pl.ds(col0, ADA_SHARD)]
            f_mp = _ag_start(ch_mp, pos)
            _ag_finish(ch_cw, pos, f_cw)
            _ag_finish(ch_mp, pos, f_mp)
            for d in range(NDEV):
                cwf_ref[0:CK, CONV_SHARD * d:CONV_SHARD * (d + 1)] = cw_buf[d]
            cwf_ref[CK:CKP, :] = jnp.zeros((CKP - CK, CW), F32)
            mod_ref[...] = jnp.concatenate([mp_buf[d, pl.ds(me, 1), :] for d in range(NDEV)], axis=1)

            for j in (1, 0):
                landed(ch_win, j).wait_recv()
                passed(ch_win, j).start()
            from_sibling(ch_win).wait_recv()
            relayed(ch_win, 1).wait_recv()
            first(ch_win, 1).wait_send()
            first(ch_win, 2).wait_send()
            first(ch_win, 3).start()
            wout = wout_ref[...].astype(BF)
            woutf_ref[pl.ds(pl.multiple_of(me * OUT_SHARD, 16), OUT_SHARD), :] = wout
            for j in range(4):
                first(ch_wout, j).start()

        row0 = pl.multiple_of((s % NT) * TS, TS)

        @pl.when(s < NT)
        def _():
            xv = x_ref[...]
            shift = mod_ref[:, 0:D]
            scale = mod_ref[:, D:2 * D]
            r = lax.rsqrt(jnp.mean(xv * xv, axis=-1, keepdims=True) + EPS)
            hb = ((xv * r * nw_ref[...]) * (1.0 + scale) + shift).astype(BF)
            h_s[pl.ds(row0, TS), :] = hb
            w_half = wtf_ref[pl.ds(pl.multiple_of(x * HALF, 16), HALF), :]
            raw0[pl.ds(row0, TS), :] = lax.dot_general(hb, w_half, (((1,), (1,)), ((), ())),
                                                       preferred_element_type=F32)

        @pl.when(s == NT)
        def _():
            relayed(ch_win, 0).wait_recv()
            landed(ch_win, 2).wait_recv()
            passed(ch_win, 2).start()
            relayed(ch_win, 2).wait_recv()
            pltpu.make_async_copy(wtf_ref, wtf_hbm, out_sems.at[0]).start()

        @pl.when(s >= NT)
        def _():
            hb = h_s[pl.ds(row0, TS), :]
            h_ref[...] = hb
            w_half = wtf_ref[pl.ds(pl.multiple_of((1 - x) * HALF, 16), HALF), :]
            raw1 = lax.dot_general(hb, w_half, (((1,), (1,)), ((), ())), preferred_element_type=F32)
            pt[:, pl.ds(pl.multiple_of(x * HALF, 128), HALF)] = raw0[pl.ds(row0, TS), :]
            pt[:, pl.ds(pl.multiple_of((1 - x) * HALF, 128), HALF)] = raw1
            cos = cos_ref[...]
            sin = sin_ref[...]
            q = pt[:, 0:512]
            qraw_ref[...] = q
            qr_ref[...] = _norm_rope_fwd(q, qw_ref[...], jnp.tile(cos, (1, 4)), jnp.tile(sin, (1, 4)),
                                         bq_ref).astype(BF)
            k = pt[:, 512:640]
            kraw_ref[...] = k
            kr_ref[...] = _norm_rope_fwd(k, kw_ref[...], cos, sin, bk_ref).astype(BF)
            vb_ref[...] = pt[:, 640:768].astype(BF)
            ga_ref[...] = pt[:, 768:1280]
            a = pt[:, 1280:1792]
            g = pt[:, 1792:2304]
            a_ref[...] = a
            g_ref[...] = g
            z_ref[...] = a * _sigmoid(g)
            gb_ref[...] = pt[:, 2304:2816]

        @pl.when(s == 2 * NT - 1)
        def _():
            for j in range(3):
                landed(ch_wout, j).wait_recv()
                passed(ch_wout, j).start()
            from_sibling(ch_wout).wait_recv()
            for j in range(3):
                relayed(ch_wout, j).wait_recv()
            out_copy = pltpu.make_async_copy(woutf_ref, woutf_hbm, out_sems.at[1])
            out_copy.start()
            pltpu.make_async_copy(wtf_ref, wtf_hbm, out_sems.at[0]).wait()
            out_copy.wait()
            first(ch_win, 0).wait_send()
            first(ch_win, 3).wait_send()
            for j in range(3):
                passed(ch_win, j).wait_send()
                passed(ch_wout, j).wait_send()
            for j in range(4):
                first(ch_wout, j).wait_send()

    early = lambda i: (jnp.minimum(i, NT - 1), 0)
    late = lambda i: (jnp.maximum(i - NT, 0), 0)
    t512 = pl.BlockSpec((TS, 512), late)
    t128 = pl.BlockSpec((TS, 128), late)
    sem = pltpu.SemaphoreType.DMA((7,))
    sds = jax.ShapeDtypeStruct
    return pl.pallas_call(
        body, name="gather_fwd", grid=(2 * NT,),
        out_shape=[sds((INW, D), BF), sds((D, D), BF), sds((CKP, CW), F32), sds((1, 3 * D), F32), sds((NDEV, D), F32),
                   sds((S, D), BF), sds((S, AW), F32), sds((S, KVW), F32), sds((S, AW), F32), sds((S, CW), F32),
                   sds((S, CW), F32), sds((S, CW), F32), sds((S, AW), BF), sds((S, KVW), BF), sds((S, KVW), BF),
                   sds((S, CW), F32)],
        in_specs=[_VMEM] * 6 + [pl.BlockSpec((TS, D), early), pl.BlockSpec((1, D), _const),
                                pl.BlockSpec((1, AW), _const), pl.BlockSpec((1, KVW), _const), t128, t128,
                                pl.BlockSpec((AW, AW), _const), pl.BlockSpec((KVW, KVW), _const)],
        out_specs=[_ANY, _ANY] + [_VMEM] * 3 + [pl.BlockSpec((TS, D), late), t512, t128, t512, t512, t512, t512, t512,
                                                t128, t128, t512],
        scratch_shapes=[pltpu.VMEM((NDEV, CK, CONV_SHARD), F32), pltpu.VMEM((NDEV, 8, D), F32),
                        pltpu.VMEM((NDEV, 8, ADA_SHARD), F32), pltpu.VMEM((S, D), BF), pltpu.VMEM((S, HALF), F32),
                        pltpu.VMEM((TS, INW), F32), pltpu.VMEM((INW, D), BF), pltpu.VMEM((D, D), BF)]
        + [sem] * 10 + [pltpu.SemaphoreType.DMA((2,))],
        compiler_params=_params(True),
    )(w_in_t, w_out_s, conv_w_s, c, w_ada_s, b_ada, x2, norm_w, qw_t, kw_t, cos_t, sin_t, bq, bk)


QB = 4
NQB = NB // QB


def _band_mask(has_prev):
    kj = lax.broadcasted_iota(jnp.int32, (2 * BLK, 4 * BLK), 0)
    qi = lax.broadcasted_iota(jnp.int32, (2 * BLK, 4 * BLK), 1) & (BLK - 1)
    dist = qi + BLK - kj
    local = (dist >= 0) & (dist < BLK)
    return local if has_prev is True else local & ((kj >= BLK) | has_prev)


def _key_blocks(sb, prev_ref, cur_ref):
    prev = prev_ref[...] if sb == 0 else cur_ref[BLK * (sb - 1):BLK * sb, :]
    return prev, cur_ref[BLK * sb:BLK * (sb + 1), :]


def _sink_lanes(sink_ref, g):
    lane = lax.broadcasted_iota(jnp.int32, (1, 4 * BLK), 1)
    return jnp.where(lane < BLK, sink_ref[0, 4 * g],
                     jnp.where(lane < 2 * BLK, sink_ref[0, 4 * g + 1],
                               jnp.where(lane < 3 * BLK, sink_ref[0, 4 * g + 2], sink_ref[0, 4 * g + 3])))


def _unstack_t(t):
    return [t[:, BLK * h:BLK * (h + 1)].T for h in range(4)]


def _stack_heads(t, g):
    return jnp.concatenate([t[:, HD * (4 * g + h):HD * (4 * g + h + 1)] for h in range(4)], axis=0)


def _band(prev, cur, g):
    return jnp.concatenate([prev[:, HD * g:HD * (g + 1)], cur[:, HD * g:HD * (g + 1)]], axis=0)


def _softmax_band(qs, kb, mask, sink):
    s = lax.dot_general(kb, qs, (((1,), (1,)), ((), ())), preferred_element_type=F32) * (HD ** -0.5)
    s = jnp.where(mask, s, NEG)
    m = jnp.maximum(jnp.max(s, axis=0, keepdims=True), sink)
    e = jnp.exp(s - m)
    es = jnp.exp(sink - m)
    inv = 1.0 / (jnp.sum(e, axis=0, keepdims=True) + es)
    return e * inv, es * inv


HALO = 32


RC = 64
LC = 128


def _windows(ext_ref, r0, l0, base):
    col = ext_ref[pl.ds(r0, RC + HALO), pl.ds(l0, LC)]
    for s in range(8):
        rolled = col if s == 0 else pltpu.roll(col, RC + HALO - s, 0)
        for t in range(CK):
            if (base + t) % 8 == s:
                a8 = base + t - s
                yield t, rolled[a8:a8 + RC]


def _taps(ext_ref, r0, l0, base, cw_ref, flip):
    acc = None
    for t, win in _windows(ext_ref, r0, l0, base):
        k = CK - 1 - t if flip else t
        term = win * cw_ref[k:k + 1, pl.ds(l0, LC)]
        acc = term if acc is None else acc + term
    return acc


NSUB = (BLK // RC) * (CW // LC)


def _sub_tile(t, row_base):
    r0 = pl.multiple_of(row_base + (t // (CW // LC)) * RC, RC)
    l0 = pl.multiple_of((t % (CW // LC)) * LC, LC)
    return r0, l0


FR = QB * BLK


def _fwd_tail_call(sinks, qr, kr, vb, z, gb, cwf, conv_b, ln_w, ln_b, ga, x2, tgt, mod, w_out_full):
    def body(sink_ref, q_ref, kp_ref, kc_ref, vp_ref, vc_ref, z_ref, zh_ref, gb_ref, cw_ref, cb_ref, lw_ref, lb_ref,
             ga_ref, x_ref, t_ref, mod_ref, w_ref,
             o_ref, zc_ref, dout_ref, dya_ref, dyb_ref, gw_ref, loss_ref, dgate_ref,
             zext, yb_ref):
        i = pl.program_id(0)

        @pl.when(i == 0)
        def _():
            gw_ref[...] = jnp.zeros((D, D), F32)
            loss_ref[...] = jnp.zeros((1, 128), F32)
            dgate_ref[...] = jnp.zeros((1, D), F32)

        def out_proj(rows):
            gav = ga_ref[rows, :]
            ya = o_ref[rows, :] * (gav * _sigmoid(gav))
            ycat = jnp.concatenate([ya.astype(BF), yb_ref[rows, :]], axis=1)
            w = w_ref[...]
            y = jnp.dot(ycat, w, preferred_element_type=F32)
            gate = mod_ref[:, 2 * D:3 * D]
            diff = x_ref[rows, :] + gate * y - t_ref[rows, :]
            sq = jnp.sum(jnp.sum(diff * diff, axis=1, keepdims=True), axis=0, keepdims=True)
            loss_ref[...] += jnp.broadcast_to(sq, (1, 128))
            d_out = diff * (1.0 / D)
            dout_ref[rows, :] = d_out
            dgate_ref[...] += jnp.sum(d_out * y, axis=0, keepdims=True)
            dy = (d_out * gate).astype(BF)
            d_ycat = lax.dot_general(dy, w, (((1,), (1,)), ((), ())), preferred_element_type=F32)
            dya_ref[rows, :] = d_ycat[:, 0:AW]
            dyb_ref[rows, :] = d_ycat[:, AW:D]
            gw_ref[...] += lax.dot_general(ycat, dy, (((0,), (0,)), ((), ())), preferred_element_type=F32)

        zext[0:HALO, :] = jnp.where(i > 0, zh_ref[...], 0.0)
        zext[HALO:HALO + FR, :] = z_ref[...]
        for sb in range(QB):
            rows = slice(BLK * sb, BLK * (sb + 1))
            mask = _band_mask(i > 0 if sb == 0 else True)
            q = q_ref[rows, :]
            kp, kc = _key_blocks(sb, kp_ref, kc_ref)
            vp, vc = _key_blocks(sb, vp_ref, vc_ref)
            for g in range(NKV):
                p, _ = _softmax_band(_stack_heads(q, g), _band(kp, kc, g), mask, _sink_lanes(sink_ref, g))
                o_t = lax.dot_general(_band(vp, vc, g), p.astype(BF), (((0,), (0,)), ((), ())),
                                      preferred_element_type=F32)
                for h, o_h in enumerate(_unstack_t(o_t)):
                    o_ref[rows, HD * (4 * g + h):HD * (4 * g + h + 1)] = o_h
            for r0 in range(BLK * sb, BLK * (sb + 1), RC):
                for l0 in range(0, CW, LC):
                    acc = _taps(zext, r0, l0, HALO - (CK - 1), cw_ref, False)
                    zc_ref[r0:r0 + RC, l0:l0 + LC] = acc + cb_ref[:, l0:l0 + LC]
            zc = zc_ref[rows, :]
            mu = jnp.mean(zc, axis=-1, keepdims=True)
            dz = zc - mu
            rstd = lax.rsqrt(jnp.mean(dz * dz, axis=-1, keepdims=True) + EPS)
            zn = dz * rstd * lw_ref[...] + lb_ref[...]
            gbv = gb_ref[rows, :]
            yb_ref[rows, :] = ((zn * _sigmoid(zn)) * (gbv * _sigmoid(gbv))).astype(BF)
            if (sb + 1) * BLK % TS == 0:
                out_proj(slice((sb + 1) * BLK - TS, (sb + 1) * BLK))

    prev = lambda i: (jnp.maximum(QB * i - 1, 0), 0)
    halo = lambda i: (jnp.maximum(FR // HALO * i - 1, 0), 0)
    f512 = pl.BlockSpec((FR, 512), _row)
    f128 = pl.BlockSpec((FR, KVW), _row)
    f1024 = pl.BlockSpec((FR, D), _row)
    c512 = pl.BlockSpec((1, CW), _const)
    sds = jax.ShapeDtypeStruct
    return pl.pallas_call(
        body, name="fwd_tail", grid=(NQB,),
        out_shape=[sds((S, AW), F32), sds((S, CW), F32), sds((S, D), F32), sds((S, AW), F32), sds((S, CW), F32),
                   sds((D, D), F32), sds((1, 128), F32), sds((1, D), F32)],
        in_specs=[_SMEM, f512, pl.BlockSpec((BLK, KVW), prev), f128, pl.BlockSpec((BLK, KVW), prev), f128,
                  f512, pl.BlockSpec((HALO, CW), halo), f512, pl.BlockSpec((CKP, CW), _const), c512, c512, c512,
                  f512, f1024, f1024, pl.BlockSpec((1, 3 * D), _const),
                  pl.BlockSpec((D, D), _const, pipeline_mode=pl.Buffered(1))],
        out_specs=[f512, f512, f1024, f512, f512, _VMEM, _VMEM, _VMEM],
        scratch_shapes=[pltpu.VMEM((FR + HALO, CW), F32), pltpu.VMEM((FR, CW), BF)],
        compiler_params=_params(True),
    )(sinks, qr, kr, kr, vb, vb, z, z, gb, cwf, conv_b, ln_w, ln_b, ga, x2, tgt, mod, w_out_full)


def _ln_gate_bwd(d_yb, zc, gbv, lw, lb):
    mu = jnp.mean(zc, axis=-1, keepdims=True)
    dz = zc - mu
    rstd = lax.rsqrt(jnp.mean(dz * dz, axis=-1, keepdims=True) + EPS)
    zh = dz * rstd
    zn = zh * lw + lb
    act_n, dact_n = _silu_and_grad(zn)
    act_g, dact_g = _silu_and_grad(gbv)
    d_gb = d_yb * act_n * dact_g
    d_zn = d_yb * act_g * dact_n
    dzh = d_zn * lw
    d_zc = rstd * (dzh - jnp.mean(dzh, axis=-1, keepdims=True) - zh * jnp.mean(dzh * zh, axis=-1, keepdims=True))
    return d_zc, d_gb, d_zn, zh


def _attn_conv_bwd_call(sinks, qr, kr, vb, d_ya, ga, o, qraw, qw_t, cos_t, sin_t, bq, d_yb, zc, gb, z, a, g, cwf,
                        ln_w, ln_b):
    def body(sink_ref, q_ref, kp_ref, kc_ref, vp_ref, vc_ref, dya_ref, ga_ref, o_ref, qraw_ref, qw_ref,
             cos_ref, sin_ref, bq_ref,
             dyb_ref, dybn_ref, zc_ref, zcn_ref, gb_ref, gbn_ref, z_ref, zh_ref, a_ref, g_ref, cw_ref, lw_ref, lb_ref,
             dqraw_ref, dga_ref, dk_ref, dv_ref, gqw_ref, gsink_ref,
             dgb_ref, da_ref, dg_ref, gcw_ref, glw_ref, glb_ref, gcb_ref,
             dext, zext, gacc):
        i = pl.program_id(0)

        @pl.when(i == 0)
        def _():
            dk_ref[...] = jnp.zeros((S, KVW), F32)
            dv_ref[...] = jnp.zeros((S, KVW), F32)
            gqw_ref[...] = jnp.zeros((1, AW), F32)
            gsink_ref[...] = jnp.zeros((1, 128), F32)
            gacc[...] = jnp.zeros((CKP * 8, CW), F32)
            glw_ref[...] = jnp.zeros((1, CW), F32)
            glb_ref[...] = jnp.zeros((1, CW), F32)
            gcb_ref[...] = jnp.zeros((1, CW), F32)

        lw = lw_ref[...]
        lb = lb_ref[...]

        def ln_rows(rows):
            d_zc, d_gb, d_zn, zh = _ln_gate_bwd(dyb_ref[rows, :], zc_ref[rows, :], gb_ref[rows, :], lw, lb)
            dgb_ref[rows, :] = d_gb.astype(BF)
            glw_ref[...] += jnp.sum(d_zn * zh, axis=0, keepdims=True)
            glb_ref[...] += jnp.sum(d_zn, axis=0, keepdims=True)
            gcb_ref[...] += jnp.sum(d_zc, axis=0, keepdims=True)
            dext[rows, :] = d_zc

        ln_rows(slice(0, BLK))
        zext[0:HALO, :] = jnp.where(i > 0, zh_ref[...], 0.0)
        zext[HALO:HALO + FR, :] = z_ref[...]

        lane = lax.broadcasted_iota(jnp.int32, (1, 128), 1)
        gsink = jnp.zeros((1, 128), F32)
        dq_rows = []
        for sb in range(QB):
            rows = slice(BLK * sb, BLK * (sb + 1))
            if sb + 1 < QB:
                ln_rows(slice(BLK * (sb + 1), BLK * (sb + 2)))
            else:
                d_zc_next, _, _, _ = _ln_gate_bwd(dybn_ref[...], zcn_ref[...], gbn_ref[...], lw, lb)
                dext[FR:FR + HALO, :] = jnp.where(i < NQB - 1, d_zc_next, 0.0)
            mask = _band_mask(i > 0 if sb == 0 else True)
            q = q_ref[rows, :]
            d_ya = dya_ref[rows, :]
            act, dact = _silu_and_grad(ga_ref[rows, :])
            dga_ref[rows, :] = (d_ya * o_ref[rows, :] * dact).astype(BF)
            d_o = (d_ya * act).astype(BF)
            kp, kc = _key_blocks(sb, kp_ref, kc_ref)
            vp, vc = _key_blocks(sb, vp_ref, vc_ref)
            dq_parts, dk_parts, dv_parts = [], [], []
            for gi in range(NKV):
                qs = _stack_heads(q, gi)
                kb = _band(kp, kc, gi)
                vbd = _band(vp, vc, gi)
                p, ps = _softmax_band(qs, kb, mask, _sink_lanes(sink_ref, gi))
                dos = _stack_heads(d_o, gi)
                dp = lax.dot_general(vbd, dos, (((1,), (1,)), ((), ())), preferred_element_type=F32)
                dr = jnp.sum(p * dp, axis=0, keepdims=True)
                ds = (p * (dp - dr) * (HD ** -0.5)).astype(BF)
                sink_term = ps * dr
                for h in range(4):
                    part = jnp.sum(sink_term[:, BLK * h:BLK * (h + 1)], axis=1, keepdims=True)
                    gsink = gsink - jnp.where(lane == 4 * gi + h, part, 0.0)
                dv_parts.append(jnp.dot(p.astype(BF), dos, preferred_element_type=F32))
                dk_parts.append(jnp.dot(ds, qs, preferred_element_type=F32))
                dq_t = lax.dot_general(kb, ds, (((0,), (0,)), ((), ())), preferred_element_type=F32)
                dq_parts.extend(_unstack_t(dq_t))
            dkb = jnp.concatenate(dk_parts, axis=1)
            dvb = jnp.concatenate(dv_parts, axis=1)
            blk = QB * i + sb
            r_prev = pl.multiple_of(jnp.maximum(blk - 1, 0) * BLK, BLK)
            r_cur = pl.multiple_of(blk * BLK, BLK)
            dk_ref[pl.ds(r_prev, BLK), :] += dkb[0:BLK]
            dv_ref[pl.ds(r_prev, BLK), :] += dvb[0:BLK]
            dk_ref[pl.ds(r_cur, BLK), :] += dkb[BLK:2 * BLK]
            dv_ref[pl.ds(r_cur, BLK), :] += dvb[BLK:2 * BLK]
            dq_rows.append(jnp.concatenate(dq_parts, axis=1))

            def taps_sub(t, carry, sb=sb):
                r0, l0 = _sub_tile(t, BLK * sb)
                here = (pl.ds(r0, RC), pl.ds(l0, LC))
                d_z = _taps(dext, r0, l0, 0, cw_ref, True)
                sg = _sigmoid(g_ref[here])
                da_ref[here] = (d_z * sg).astype(BF)
                dg_ref[here] = (d_z * a_ref[here] * sg * (1.0 - sg)).astype(BF)
                d_sub = dext[here]
                for k, win in _windows(zext, r0, l0, HALO - (CK - 1)):
                    prod = d_sub * win
                    part = prod[0:8]
                    for q8 in range(1, RC // 8):
                        part = part + prod[8 * q8:8 * q8 + 8]
                    gacc[8 * k:8 * k + 8, pl.ds(l0, LC)] += part
                return carry

            lax.fori_loop(0, NSUB, taps_sub, 0)
        gsink_ref[...] += gsink
        dq = jnp.concatenate(dq_rows, axis=0)
        dq_raw, g_qw = _norm_rope_bwd(dq, qraw_ref[...], qw_ref[...], jnp.tile(cos_ref[...], (1, 4)),
                                      jnp.tile(sin_ref[...], (1, 4)), bq_ref)
        dqraw_ref[...] = dq_raw.astype(BF)
        gqw_ref[...] += g_qw

        @pl.when(i == NQB - 1)
        def _():
            for k in range(CK):
                gcw_ref[k:k + 1, :] = jnp.sum(gacc[8 * k:8 * k + 8, :], axis=0, keepdims=True)
            gcw_ref[CK:CKP, :] = jnp.zeros((CKP - CK, CW), F32)

    prev = lambda i: (jnp.maximum(QB * i - 1, 0), 0)
    halo_prev = lambda i: (jnp.maximum(FR // HALO * i - 1, 0), 0)
    halo_next = lambda i: (jnp.minimum(FR // HALO * (i + 1), S // HALO - 1), 0)
    f512 = pl.BlockSpec((FR, 512), _row)
    f128 = pl.BlockSpec((FR, 128), _row)
    hn = pl.BlockSpec((HALO, CW), halo_next)
    c512 = pl.BlockSpec((1, CW), _const)
    sds = jax.ShapeDtypeStruct
    vec = sds((1, CW), F32)
    return pl.pallas_call(
        body, name="attn_conv_bwd", grid=(NQB,),
        out_shape=[sds((S, AW), BF), sds((S, AW), BF), sds((S, KVW), F32), sds((S, KVW), F32), sds((1, AW), F32),
                   sds((1, 128), F32),
                   sds((S, CW), BF), sds((S, CW), BF), sds((S, CW), BF), sds((CKP, CW), F32), vec, vec, vec],
        in_specs=[_SMEM, f512, pl.BlockSpec((BLK, KVW), prev), f128, pl.BlockSpec((BLK, KVW), prev), f128,
                  f512, f512, f512, f512, pl.BlockSpec((1, AW), _const), f128, f128, pl.BlockSpec((AW, AW), _const),
                  f512, hn, f512, hn, f512, hn, f512, pl.BlockSpec((HALO, CW), halo_prev), f512, f512,
                  pl.BlockSpec((CKP, CW), _const), c512, c512],
        out_specs=[f512, f512, _VMEM, _VMEM, _VMEM, _VMEM, f512, f512, f512, _VMEM, _VMEM, _VMEM, _VMEM],
        scratch_shapes=[pltpu.VMEM((FR + HALO, CW), F32), pltpu.VMEM((FR + HALO, CW), F32),
                        pltpu.VMEM((CKP * 8, CW), F32)],
        compiler_params=_params(True),
    )(sinks, qr, kr, kr, vb, vb, d_ya, ga, o, qraw, qw_t, cos_t, sin_t, bq,
      d_yb, d_yb, zc, zc, gb, gb, z, z, a, g, cwf, ln_w, ln_b)


SM_ROWS = 8
PIECES = ((0, 512), (512, 640), (640, 768), (768, 1280), (1280, 1792), (1792, 2304), (2304, 2816))


def _bwd_in_call(dqraw, dk, dv, dga, da, dg, dgb, kraw, kw_t, cos_t, sin_t, bk, h, wt_full, x2, d_out, mod, norm_w,
                 gw_out, gcw, glw, glb, gcb, gqw, gsink, dgate, loss_p, cact_all):
    def body(dq_ref, dk_ref, dv_ref, dga_ref, da_ref, dg_ref, dgb_ref, kraw_ref, kw_ref, cos_ref, sin_ref, bk_ref,
             h_ref, wt_ref, x_ref, dout_ref, mod_ref, nw_ref, gwout_ref, gcw_ref, glw_ref, glb_ref, gcb_ref, gqw_ref,
             gsink_ref, dgate_ref, loss_ref, cact_ref,
             gx_ref, o_gwin, o_gwout, o_gwada, o_gbada, o_gnw, o_gqw, o_gkw, o_gsink, o_gcw, o_gcb, o_glw, o_glb,
             o_loss,
             acc, win_send, win_sib, win_ici, wout_send, wout_sib, wout_ici, sm_buf, cw_buf, dmod_all, vec_acc, gkw_acc,
             wi_ds, wi_dr, wi_is, wi_ir, wo_ds, wo_dr, wo_is, wo_ir, sm_s, sm_r, cw_s, cw_r):
        i = pl.program_id(0)
        pos = _mesh_pos()
        x, y, cc = pos
        me = 4 * x + 2 * y + cc

        def chip(j):
            return (1 - x if j & 1 else x, 1 - y if j & 2 else y)

        def rows_of(buf, px, py, pc, rows, align):
            return buf.at[pl.ds(pl.multiple_of((4 * px + 2 * py + pc) * rows, align), rows), :]

        bufs = {"in": (win_send, win_sib, win_ici, IN_SHARD, wi_ds, wi_dr, wi_is, wi_ir),
                "out": (wout_send, wout_sib, wout_ici, OUT_SHARD, wo_ds, wo_dr, wo_is, wo_ir)}

        def d2d_copy(j, which):
            send, sib, _, rows, ds_, dr_, _, _ = bufs[which]
            px, py = chip(j)
            return pltpu.make_async_remote_copy(src_ref=rows_of(send, px, py, 1 - cc, rows, 16), dst_ref=sib.at[j],
                                                send_sem=ds_.at[j], recv_sem=dr_.at[j], device_id=(x, y, 1 - cc),
                                                device_id_type=MESH)

        def ici_copy(j, which):
            send, _, ici, rows, _, _, is_, ir_ = bufs[which]
            px, py = chip(j)
            return pltpu.make_async_remote_copy(src_ref=rows_of(send, px, py, cc, rows, 16), dst_ref=ici.at[j - 1],
                                                send_sem=is_.at[j - 1], recv_sem=ir_.at[j - 1], device_id=(px, py, cc),
                                                device_id_type=MESH)

        def level2(which, partial_ref):
            send, sib, _, rows, _, _, _, _ = bufs[which]
            for j in range(1, 4):
                d2d_copy(j, which).wait_recv()
                px, py = chip(j)
                mine = rows_of(partial_ref, px, py, cc, rows, 8)[...]
                rows_of(send, px, py, cc, rows, 16)[...] = (mine + sib[j].astype(F32)).astype(BF)
                ici_copy(j, which).start()

        def finish(which, partial_ref):
            _, sib, ici, rows, _, _, _, _ = bufs[which]
            d2d_copy(0, which).wait_recv()
            total = rows_of(partial_ref, x, y, cc, rows, 8)[...] + sib[0].astype(F32)
            for j in range(1, 4):
                ici_copy(j, which).wait_recv()
                total = total + ici[j - 1].astype(F32)
            for j in range(4):
                d2d_copy(j, which).wait_send()
            for j in range(1, 4):
                ici_copy(j, which).wait_send()
            return total

        def dproj_pieces():
            dk_raw, g_kw = _norm_rope_bwd(dk_ref[...], kraw_ref[...], kw_ref[...], cos_ref[...], sin_ref[...], bk_ref)
            return [dq_ref[...], dk_raw.astype(BF), dv_ref[...].astype(BF), dga_ref[...], da_ref[...], dg_ref[...],
                    dgb_ref[...]], g_kw

        @pl.when(i == 0)
        def _():
            acc[...] = jnp.zeros((INW, D), F32)
            vec_acc[...] = jnp.zeros((8, D), F32)
            gkw_acc[...] = jnp.zeros((1, KVW), F32)
            wout_send[...] = gwout_ref[...].astype(BF)
            for j in range(4):
                d2d_copy(j, "out").start()

        @pl.when(i == 2)
        def _():
            level2("out", gwout_ref)

        @pl.when(i < NT)
        def _():
            pieces, g_kw = dproj_pieces()
            gkw_acc[...] += g_kw
            hv = h_ref[...]
            for (lo, hi), piece in zip(PIECES, pieces):
                acc[lo:hi, :] += lax.dot_general(piece, hv, (((0,), (0,)), ((), ())), preferred_element_type=F32)

        @pl.when(i == NT - 1)
        def _():
            for lo, hi in PIECES:
                win_send[lo:hi, :] = acc[lo:hi, :].astype(BF)
            for j in range(4):
                d2d_copy(j, "in").start()

        @pl.when(i == NT + 2)
        def _():
            level2("in", acc)

        @pl.when(i >= NT)
        def _():
            pieces, _ = dproj_pieces()
            dproj = jnp.concatenate(pieces, axis=1)
            d_h = jnp.dot(dproj, wt_ref[...], preferred_element_type=F32)
            xv = x_ref[...]
            scale = mod_ref[:, D:2 * D]
            nw = nw_ref[...]
            r = lax.rsqrt(jnp.mean(xv * xv, axis=-1, keepdims=True) + EPS)
            xn = xv * r
            vec_acc[0:1, :] += jnp.sum(d_h, axis=0, keepdims=True)
            vec_acc[1:2, :] += jnp.sum(d_h * (xn * nw), axis=0, keepdims=True)
            d_u = d_h * (1.0 + scale)
            vec_acc[2:3, :] += jnp.sum(d_u * xn, axis=0, keepdims=True)
            d_xn = d_u * nw
            gx_ref[...] = dout_ref[...] + r * (d_xn - xn * jnp.mean(d_xn * xn, axis=-1, keepdims=True))

        @pl.when(i == 2 * NT - 1)
        def _():
            ch_sm = (_slab(sm_buf), sm_s, sm_r)
            ch_cw = (_slab(cw_buf), cw_s, cw_r)
            z128 = jnp.zeros((1, 128), F32)
            row4 = jnp.concatenate([glw_ref[...], glb_ref[...]], axis=1)
            row5 = jnp.concatenate([gcb_ref[...], gqw_ref[...]], axis=1)
            row6 = jnp.concatenate([gkw_acc[...], gsink_ref[...], loss_ref[...]] + [z128] * 5, axis=1)
            sm_buf[me] = jnp.concatenate([vec_acc[0:2, :], dgate_ref[...], vec_acc[2:3, :], row4, row5, row6,
                                          jnp.zeros((1, D), F32)], axis=0)
            f_sm = _ag_start(ch_sm, pos)
            cw_buf[me] = gcw_ref[...]
            f_cw = _ag_start(ch_cw, pos)
            _ag_finish(ch_sm, pos, f_sm)
            _ag_finish(ch_cw, pos, f_cw)
            tot = sm_buf[0]
            cw_tot = cw_buf[0]
            for d in range(1, NDEV):
                tot = tot + sm_buf[d]
                cw_tot = cw_tot + cw_buf[d]
            o_gbada[...] = jnp.concatenate([tot[0:1, :], tot[1:2, :], tot[2:3, :]], axis=1)
            o_gnw[...] = tot[3:4, :]
            o_glw[...] = tot[4:5, 0:CW]
            o_glb[...] = tot[4:5, CW:D]
            o_gcb[...] = tot[5:6, 0:CW]
            gq = tot[5:6, CW:CW + HD]
            for hh in range(1, NQ):
                gq = gq + tot[5:6, CW + HD * hh:CW + HD * (hh + 1)]
            o_gqw[...] = gq
            o_gkw[...] = tot[6:7, 0:HD] + tot[6:7, HD:2 * HD]
            o_gsink[...] = tot[6:7, 128:128 + NQ]
            o_loss[...] = tot[6:7, 256:384] * (0.5 / D)
            mine = jnp.zeros((CK, CONV_SHARD), F32)
            for d in range(NDEV):
                mine = mine + jnp.where(me == d, cw_tot[0:CK, CONV_SHARD * d:CONV_SHARD * (d + 1)], 0.0)
            for k in range(CK):
                o_gcw[k] = mine[k:k + 1, :]
            for d in range(NDEV):
                dmod_all[d:d + 1, :] = jnp.concatenate([sm_buf[d, 0:1, :], sm_buf[d, 1:2, :], sm_buf[d, 2:3, :]],
                                                       axis=1)
            col0 = pl.multiple_of(me * ADA_SHARD, 128)
            o_gwada[...] = lax.dot_general(cact_ref[...], dmod_all[:, pl.ds(col0, ADA_SHARD)], (((0,), (0,)), ((), ())),
                                           preferred_element_type=F32, precision=lax.Precision.HIGHEST)

            o_gwout[...] = finish("out", gwout_ref)
            o_gwin[...] = finish("in", acc)

    half = lambda i: (i % NT, 0)
    late = lambda i: (jnp.maximum(i - NT, 0), 0)
    t512 = pl.BlockSpec((TS, 512), half)
    t128 = pl.BlockSpec((TS, 128), half)
    l1024 = pl.BlockSpec((TS, D), late)
    sem7 = pltpu.SemaphoreType.DMA((7,))
    sem4 = pltpu.SemaphoreType.DMA((4,))
    sem3 = pltpu.SemaphoreType.DMA((3,))
    sds = jax.ShapeDtypeStruct
    return pl.pallas_call(
        body, name="bwd_in", grid=(2 * NT,),
        out_shape=[sds((S, D), F32), sds((IN_SHARD, D), F32), sds((OUT_SHARD, D), F32), sds((D, ADA_SHARD), F32),
                   sds((1, 3 * D), F32), sds((1, D), F32), sds((1, HD), F32), sds((1, HD), F32), sds((1, NQ), F32),
                   sds((CK, 1, CONV_SHARD), F32), sds((1, CW), F32), sds((1, CW), F32), sds((1, CW), F32),
                   sds((1, 128), F32)],
        in_specs=[t512, t128, t128, t512, t512, t512, t512, t128, pl.BlockSpec((1, KVW), _const), t128, t128,
                  pl.BlockSpec((KVW, KVW), _const), pl.BlockSpec((TS, D), half),
                  pl.BlockSpec((INW, D), _const, pipeline_mode=pl.Buffered(1)), l1024, l1024,
                  pl.BlockSpec((1, 3 * D), _const), pl.BlockSpec((1, D), _const)] + [_VMEM] * 10,
        out_specs=[l1024] + [_VMEM] * 13,
        scratch_shapes=[pltpu.VMEM((INW, D), F32), pltpu.VMEM((INW, D), BF), pltpu.VMEM((4, IN_SHARD, D), BF),
                        pltpu.VMEM((3, IN_SHARD, D), BF), pltpu.VMEM((D, D), BF), pltpu.VMEM((4, OUT_SHARD, D), BF),
                        pltpu.VMEM((3, OUT_SHARD, D), BF),
                        pltpu.VMEM((NDEV, SM_ROWS, D), F32), pltpu.VMEM((NDEV, CKP, CW), F32),
                        pltpu.VMEM((NDEV, 3 * D), F32), pltpu.VMEM((8, D), F32), pltpu.VMEM((1, KVW), F32)]
        + [sem4, sem4, sem3, sem3] * 2 + [sem7] * 4,
        compiler_params=pltpu.CompilerParams(dimension_semantics=("arbitrary",), vmem_limit_bytes=BIG_VMEM_LIMIT),
    )(dqraw, dk, dv, dga, da, dg, dgb, kraw, kw_t, cos_t, sin_t, bk, h, wt_full, x2, d_out, mod, norm_w,
      gw_out, gcw, glw, glb, gcb, gqw, gsink, dgate, loss_p, cact_all)


ADAM_STEPS = 4


def _adam_chunking(arr):
    if arr.ndim == 2 and arr.shape[0] % (8 * ADAM_STEPS) == 0:
        return "rows"
    if arr.ndim == 2 and arr.shape[1] % (128 * ADAM_STEPS) == 0:
        return "cols"
    return None


def _adam_call(ws, gs, ms, vs, grad_x, loss_v):
    n = len(ws)
    bc1 = 1.0 - ADAM_B1 ** ADAM_STEP
    bc2 = 1.0 - ADAM_B2 ** ADAM_STEP
    chunked = [_adam_chunking(w) for w in ws]

    def body(*refs):
        ins, outs = refs[:4 * n + 2], refs[4 * n + 2:]
        i = pl.program_id(0)

        def update(j):
            w, g, m, v = (ins[j][...], ins[n + j][...], ins[2 * n + j][...], ins[3 * n + j][...])
            m_new = ADAM_B1 * m + (1.0 - ADAM_B1) * g
            v_new = ADAM_B2 * v + (1.0 - ADAM_B2) * (g * g)
            m_hat = m_new / bc1
            v_hat = v_new / bc2
            outs[j][...] = g
            outs[n + j][...] = -ADAM_LR * (m_hat / (jnp.sqrt(v_hat) + ADAM_EPS) + ADAM_WD * w)
            outs[2 * n + j][...] = m_new
            outs[3 * n + j][...] = v_new

        for j in range(n):
            if chunked[j]:
                update(j)
        outs[4 * n][...] = ins[4 * n][...]

        @pl.when(i == 0)
        def _():
            for j in range(n):
                if not chunked[j]:
                    update(j)
            outs[4 * n + 1][...] = ins[4 * n + 1][...]

    def spec(arr, how):
        if how == "rows":
            return pl.BlockSpec((arr.shape[0] // ADAM_STEPS, arr.shape[1]), _row)
        if how == "cols":
            return pl.BlockSpec((arr.shape[0], arr.shape[1] // ADAM_STEPS), lambda i: (0, i))
        zeros = (0,) * arr.ndim
        return pl.BlockSpec(arr.shape, lambda i: zeros)

    par_specs = [spec(w, ch) for w, ch in zip(ws, chunked)]
    extra = [spec(grad_x, "rows"), spec(loss_v, None)]
    shapes = [jax.ShapeDtypeStruct(w.shape, F32) for w in ws]
    res = pl.pallas_call(
        body, name="adam", grid=(ADAM_STEPS,),
        out_shape=shapes * 4 + [jax.ShapeDtypeStruct(grad_x.shape, F32), jax.ShapeDtypeStruct(loss_v.shape, F32)],
        in_specs=par_specs * 4 + extra, out_specs=par_specs * 4 + extra,
        compiler_params=_params(True),
    )(*ws, *gs, *ms, *vs, grad_x, loss_v)
    return res[0:n], res[n:2 * n], res[2 * n:3 * n], res[3 * n:4 * n], res[4 * n], res[4 * n + 1]


def _rope_tables():
    inv = (np.float32(ROPE_THETA) ** (-np.arange(0, HD, 2, dtype=np.float32) / np.float32(HD))).astype(np.float32)
    ang = (np.arange(S, dtype=np.float32)[:, None] * inv[None, :]).astype(np.float32)
    cos, sin = np.cos(ang).astype(np.float32), np.sin(ang).astype(np.float32)
    cos64 = np.concatenate([cos, cos], axis=-1)
    sin64 = np.concatenate([-sin, sin], axis=-1)
    return jnp.asarray(np.tile(cos64, (1, 2))), jnp.asarray(np.tile(sin64, (1, 2)))


def _group_matrix(width):
    idx = np.arange(width) // HD
    return jnp.asarray(np.where(idx[:, None] == idx[None, :], 1.0 / HD, 0.0).astype(np.float32)).astype(BF)


def kernel(x, c, w_ada, b_ada, norm_w, w_in, q_norm_w, k_norm_w, sinks, conv_w, conv_b, ln_w, ln_b, w_out, loss_target, m_w_ada, m_b_ada, m_norm_w, m_w_in, m_q_norm_w, m_k_norm_w, m_sinks, m_conv_w, m_conv_b, m_ln_w, m_ln_b, m_w_out, v_w_ada, v_b_ada, v_norm_w, v_w_in, v_q_norm_w, v_k_norm_w, v_sinks, v_conv_w, v_conv_b, v_ln_w, v_ln_b, v_w_out):
    x2 = x[0]
    tgt = loss_target[0]
    cos_t, sin_t = _rope_tables()
    bq = _group_matrix(AW)
    bk = _group_matrix(KVW)
    qw_t = jnp.tile(q_norm_w, (1, NQ))
    kw_t = jnp.tile(k_norm_w, (1, NKV))

    tr = lambda t: jnp.swapaxes(t[0], 0, 1)
    tc = lambda t: jnp.swapaxes(t, 0, 1)
    (wt_full, w_out_full, cwf, mod, cact_all, h, qraw, kraw, ga, a, g, gb, qr, kr, vb, z) = _gather_fwd_call(
        tr(w_in), w_out[0], tc(conv_w), c, w_ada[0], b_ada, x2, norm_w, qw_t, kw_t, cos_t, sin_t, bq, bk)
    o, zc, d_out, d_ya, d_yb, gw_out, loss_p, dgate = _fwd_tail_call(
        sinks, qr, kr, vb, z, gb, cwf, conv_b, ln_w, ln_b, ga, x2, tgt, mod, w_out_full)

    dqraw, dga, dk, dv, gqw, gsink, dgb, da, dg, gcw, glw, glb, gcb = _attn_conv_bwd_call(
        sinks, qr, kr, vb, d_ya, ga, o, qraw, qw_t, cos_t, sin_t, bq, d_yb, zc, gb, z, a, g, cwf, ln_w, ln_b)
    (grad_x, g_w_in_t, g_w_out, g_w_ada, g_b_ada, g_norm_w, g_qw, g_kw, g_sinks, g_conv_w, g_conv_b, g_ln_w, g_ln_b,
     loss_v) = _bwd_in_call(dqraw, dk, dv, dga, da, dg, dgb, kraw, kw_t, cos_t, sin_t, bk, h, wt_full, x2, d_out, mod,
                            norm_w, gw_out, gcw, glw, glb, gcb, gqw, gsink, dgate, loss_p, cact_all)

    ws = [w_ada[0], b_ada, norm_w, tr(w_in), q_norm_w, k_norm_w, sinks, tc(conv_w), conv_b, ln_w, ln_b, w_out[0]]
    gs = [g_w_ada, g_b_ada, g_norm_w, g_w_in_t, g_qw, g_kw, g_sinks, g_conv_w, g_conv_b, g_ln_w, g_ln_b, g_w_out]
    ms = [m_w_ada[0], m_b_ada, m_norm_w, tr(m_w_in), m_q_norm_w, m_k_norm_w, m_sinks, tc(m_conv_w), m_conv_b, m_ln_w,
          m_ln_b, m_w_out[0]]
    vs = [v_w_ada[0], v_b_ada, v_norm_w, tr(v_w_in), v_q_norm_w, v_k_norm_w, v_sinks, tc(v_conv_w), v_conv_b, v_ln_w,
          v_ln_b, v_w_out[0]]
    grads, deltas, new_m, new_v, grad_x, loss_v = _adam_call(ws, gs, ms, vs, grad_x, loss_v)
    shaped = [w_ada, b_ada, norm_w, w_in, q_norm_w, k_norm_w, sinks, conv_w, conv_b, ln_w, ln_b, w_out]
    W_IN_POS, CONV_W_POS = 3, 7

    def like(vals):
        vals = [jnp.swapaxes(v, 0, 1) if j in (W_IN_POS, CONV_W_POS) else v for j, v in enumerate(vals)]
        return [v.reshape(s.shape) for v, s in zip(vals, shaped)]

    return (loss_v[0, 0], grad_x[None], *like(grads), *like(deltas), *like(new_m), *like(new_v))
```

```python
import functools

import jax
import jax.numpy as jnp
import numpy as np
from jax import lax
from jax.experimental import pallas as pl
from jax.experimental.pallas import tpu as pltpu

S = 2048
D = 1024
NDEV = 8
HD = 64
NQ = 8
NKV = 2
AW = 512
KVW = 128
CW = 512
INW = 2816
IN_SHARD = INW // NDEV
ADA_SHARD = 3 * D // NDEV
OUT_SHARD = D // NDEV
CONV_SHARD = CW // NDEV
CK = 31
CKP = 32
BLK = 128
TS = 256
NT = S // TS
NB = S // BLK
EPS = 1e-6
ROPE_THETA = 10000.0
NEG = -1e30
BF = jnp.bfloat16
F32 = jnp.float32

ADAM_LR = 0.001
ADAM_B1 = 0.9
ADAM_B2 = 0.999
ADAM_EPS = 1e-08
ADAM_WD = 0.01
ADAM_STEP = 10

VMEM_LIMIT = 56 * 1024 * 1024
BIG_VMEM_LIMIT = 62 * 1024 * 1024
MESH = pl.DeviceIdType.MESH

_VMEM = pl.BlockSpec(memory_space=pltpu.VMEM)
_SMEM = pl.BlockSpec(memory_space=pltpu.SMEM)
_ANY = pl.BlockSpec(memory_space=pl.ANY)


def _params(grid=False):
    if grid:
        return pltpu.CompilerParams(dimension_semantics=("arbitrary",), vmem_limit_bytes=VMEM_LIMIT)
    return pltpu.CompilerParams(vmem_limit_bytes=VMEM_LIMIT)


def _row(i):
    return (i, 0)


def _const(i):
    return (0, 0)


def _sigmoid(t):
    return 1.0 / (1.0 + jnp.exp(-t))


def _silu_and_grad(t):
    sg = _sigmoid(t)
    return t * sg, sg * (1.0 + t * (1.0 - sg))


def _group_mean(t, b_ref):
    hi = t.astype(BF)
    lo = (t - hi.astype(F32)).astype(BF)
    b = b_ref[...]
    return jnp.dot(hi, b, preferred_element_type=F32) + jnp.dot(lo, b, preferred_element_type=F32)


def _partner(t):
    w = t.shape[-1]
    lane = lax.broadcasted_iota(jnp.int32, t.shape, 1)
    first = (lane & 32) == 0
    return jnp.where(first, pltpu.roll(t, w - 32, 1), pltpu.roll(t, 32, 1))


def _norm_rope_fwd(t, w_t, cos, sin, b_ref):
    r = lax.rsqrt(_group_mean(t * t, b_ref) + EPS)
    tn = t * r * w_t
    return tn * cos + _partner(tn) * sin


def _norm_rope_bwd(d_out, t, w_t, cos, sin, b_ref):
    d_tn = d_out * cos + _partner(d_out * sin)
    r = lax.rsqrt(_group_mean(t * t, b_ref) + EPS)
    th = t * r
    g_w = jnp.sum(d_tn * th, axis=0, keepdims=True)
    d_th = d_tn * w_t
    d_t = r * (d_th - th * _group_mean(d_th * th, b_ref))
    return d_t, g_w


def _mesh_pos():
    return lax.axis_index("x"), lax.axis_index("y"), lax.axis_index("c")


def _ag_copy(chan, k, block, to):
    blk, send_sems, recv_sems = chan
    ref = blk(*block)
    return pltpu.make_async_remote_copy(src_ref=ref, dst_ref=ref, send_sem=send_sems.at[k],
                                        recv_sem=recv_sems.at[k], device_id=to, device_id_type=MESH)


def _ag_start(chan, pos):
    x, y, c = pos
    me = (x, y, c)
    chips = [(1 - x, y), (x, 1 - y), (1 - x, 1 - y)]
    first = [_ag_copy(chan, 0, me, (x, y, 1 - c))]
    first += [_ag_copy(chan, 1 + j, me, (*chip, c)) for j, chip in enumerate(chips)]
    for cp in first:
        cp.start()
    return first


def _ag_finish(chan, pos, first):
    x, y, c = pos
    me = (x, y, c)
    sibling = (x, y, 1 - c)
    chips = [(1 - x, y), (x, 1 - y), (1 - x, 1 - y)]
    passed = [_ag_copy(chan, 4 + j, (*chip, c), sibling) for j, chip in enumerate(chips)]
    for j, chip in enumerate(chips):
        _ag_copy(chan, 1 + j, (*chip, c), me).wait_recv()
        passed[j].start()
    _ag_copy(chan, 0, sibling, me).wait_recv()
    for j, chip in enumerate(chips):
        _ag_copy(chan, 4 + j, (*chip, 1 - c), me).wait_recv()
    for cp in first + passed:
        cp.wait_send()


def _slab(buf):
    return lambda px, py, pc: buf.at[4 * px + 2 * py + pc]


def _row_block(buf, rows, align):
    return lambda px, py, pc: buf.at[pl.ds(pl.multiple_of((4 * px + 2 * py + pc) * rows, align), rows), :]


HALF = INW // 2


def _gather_fwd_call(w_in_t, w_out_s, conv_w_s, c, w_ada_s, b_ada, x2, norm_w, qw_t, kw_t, cos_t, sin_t, bq, bk):
    def body(win_ref, wout_ref, cw_ref, c_ref, wada_ref, bada_ref, x_ref, nw_ref, qw_ref, kw_ref, cos_ref, sin_ref,
             bq_ref, bk_ref,
             wtf_hbm, woutf_hbm, cwf_ref, mod_ref, cact_ref,
             h_ref, qraw_ref, kraw_ref, ga_ref, a_ref, g_ref, gb_ref, qr_ref, kr_ref, vb_ref, z_ref,
             cw_buf, ca_buf, mp_buf, h_s, raw0, pt, wtf_ref, woutf_ref,
             s0, r0, s1, r1, s2, r2, s3, r3, s4, r4, out_sems):
        s = pl.program_id(0)
        pos = _mesh_pos()
        x, y, cc = pos
        me3 = (x, y, cc)
        me = 4 * x + 2 * y + cc
        sibling = (x, y, 1 - cc)
        chips = [(1 - x, y), (x, 1 - y), (1 - x, 1 - y)]
        ch_win = (_row_block(wtf_ref, IN_SHARD, 16), s0, r0)
        ch_wout = (_row_block(woutf_ref, OUT_SHARD, 16), s1, r1)
        ch_cw = (_slab(cw_buf), s2, r2)
        ch_ca = (_slab(ca_buf), s3, r3)
        ch_mp = (_slab(mp_buf), s4, r4)

        def first(chan, j):
            return _ag_copy(chan, j, me3, sibling if j == 0 else (*chips[j - 1], cc))

        def passed(chan, j):
            return _ag_copy(chan, 4 + j, (*chips[j], cc), sibling)

        def landed(chan, j):
            return _ag_copy(chan, 1 + j, (*chips[j], cc), me3)

        def relayed(chan, j):
            return _ag_copy(chan, 4 + j, (*chips[j], 1 - cc), me3)

        def from_sibling(chan):
            return _ag_copy(chan, 0, sibling, me3)

        @pl.when(s == 0)
        def _():
            cv = c_ref[...]
            ca_buf[me] = jnp.broadcast_to(cv * _sigmoid(cv), (8, D))
            f_ca = _ag_start(ch_ca, pos)
            wtf_ref[pl.ds(pl.multiple_of(me * IN_SHARD, 16), IN_SHARD), :] = win_ref[...].astype(BF)
            for j in range(3):
                first(ch_win, j).start()
            cw_buf[me] = cw_ref[:, 0, :]
            f_cw = _ag_start(ch_cw, pos)

            _ag_finish(ch_ca, pos, f_ca)
            cact_all = jnp.concatenate([ca_buf[d, 0:1, :] for d in range(NDEV)], axis=0)
            cact_ref[...] = cact_all
            col0 = pl.multiple_of(me * ADA_SHARD, 128)
            mp_buf[me] = jnp.dot(cact_all, wada_ref[...], preferred_element_type=F32,
                                 precision=lax.Precision.HIGHEST) + bada_ref[:, pl.ds(col0, ADA_SHARD)]
            f_mp = _ag_start(ch_mp, pos)
            _ag_finish(ch_cw, pos, f_cw)
            _ag_finish(ch_mp, pos, f_mp)
            for d in range(NDEV):
                cwf_ref[0:CK, CONV_SHARD * d:CONV_SHARD * (d + 1)] = cw_buf[d]
            cwf_ref[CK:CKP, :] = jnp.zeros((CKP - CK, CW), F32)
            mod_ref[...] = jnp.concatenate([mp_buf[d, pl.ds(me, 1), :] for d in range(NDEV)], axis=1)

            for j in (1, 0):
                landed(ch_win, j).wait_recv()
                passed(ch_win, j).start()
            from_sibling(ch_win).wait_recv()
            relayed(ch_win, 1).wait_recv()
            first(ch_win, 1).wait_send()
            first(ch_win, 2).wait_send()
            first(ch_win, 3).start()
            wout = wout_ref[...].astype(BF)
            woutf_ref[pl.ds(pl.multiple_of(me * OUT_SHARD, 16), OUT_SHARD), :] = wout
            for j in range(4):
                first(ch_wout, j).start()

        row0 = pl.multiple_of((s % NT) * TS, TS)

        @pl.when(s < NT)
        def _():
            xv = x_ref[...]
            shift = mod_ref[:, 0:D]
            scale = mod_ref[:, D:2 * D]
            r = lax.rsqrt(jnp.mean(xv * xv, axis=-1, keepdims=True) + EPS)
            hb = ((xv * r * nw_ref[...]) * (1.0 + scale) + shift).astype(BF)
            h_s[pl.ds(row0, TS), :] = hb
            w_half = wtf_ref[pl.ds(pl.multiple_of(x * HALF, 16), HALF), :]
            raw0[pl.ds(row0, TS), :] = lax.dot_general(hb, w_half, (((1,), (1,)), ((), ())),
                                                       preferred_element_type=F32)

        @pl.when(s == NT)
        def _():
            relayed(ch_win, 0).wait_recv()
            landed(ch_win, 2).wait_recv()
            passed(ch_win, 2).start()
            relayed(ch_win, 2).wait_recv()
            pltpu.make_async_copy(wtf_ref, wtf_hbm, out_sems.at[0]).start()

        @pl.when(s >= NT)
        def _():
            hb = h_s[pl.ds(row0, TS), :]
            h_ref[...] = hb
            w_half = wtf_ref[pl.ds(pl.multiple_of((1 - x) * HALF, 16), HALF), :]
            raw1 = lax.dot_general(hb, w_half, (((1,), (1,)), ((), ())), preferred_element_type=F32)
            pt[:, pl.ds(pl.multiple_of(x * HALF, 128), HALF)] = raw0[pl.ds(row0, TS), :]
            pt[:, pl.ds(pl.multiple_of((1 - x) * HALF, 128), HALF)] = raw1
            cos = cos_ref[...]
            sin = sin_ref[...]
            q = pt[:, 0:512]
            qraw_ref[...] = q
            qr_ref[...] = _norm_rope_fwd(q, jnp.tile(qw_ref[...], (1, NQ)), jnp.tile(cos, (1, 4)),
                                         jnp.tile(sin, (1, 4)), bq_ref).astype(BF)
            k = pt[:, 512:640]
            kraw_ref[...] = k
            kr_ref[...] = _norm_rope_fwd(k, jnp.tile(kw_ref[...], (1, NKV)), cos, sin, bk_ref).astype(BF)
            vb_ref[...] = pt[:, 640:768].astype(BF)
            ga_ref[...] = pt[:, 768:1280]
            a = pt[:, 1280:1792]
            g = pt[:, 1792:2304]
            a_ref[...] = a
            g_ref[...] = g
            z_ref[...] = a * _sigmoid(g)
            gb_ref[...] = pt[:, 2304:2816]

        @pl.when(s == 2 * NT - 1)
        def _():
            for j in range(3):
                landed(ch_wout, j).wait_recv()
                passed(ch_wout, j).start()
            from_sibling(ch_wout).wait_recv()
            for j in range(3):
                relayed(ch_wout, j).wait_recv()
            out_copy = pltpu.make_async_copy(woutf_ref, woutf_hbm, out_sems.at[1])
            out_copy.start()
            pltpu.make_async_copy(wtf_ref, wtf_hbm, out_sems.at[0]).wait()
            out_copy.wait()
            first(ch_win, 0).wait_send()
            first(ch_win, 3).wait_send()
            for j in range(3):
                passed(ch_win, j).wait_send()
                passed(ch_wout, j).wait_send()
            for j in range(4):
                first(ch_wout, j).wait_send()

    early = lambda i: (jnp.minimum(i, NT - 1), 0)
    late = lambda i: (jnp.maximum(i - NT, 0), 0)
    t512 = pl.BlockSpec((TS, 512), late)
    t128 = pl.BlockSpec((TS, 128), late)
    sem = pltpu.SemaphoreType.DMA((7,))
    sds = jax.ShapeDtypeStruct
    return pl.pallas_call(
        body, name="gather_fwd", grid=(2 * NT,),
        out_shape=[sds((INW, D), BF), sds((D, D), BF), sds((CKP, CW), F32), sds((1, 3 * D), F32), sds((NDEV, D), F32),
                   sds((S, D), BF), sds((S, AW), F32), sds((S, KVW), F32), sds((S, AW), F32), sds((S, CW), F32),
                   sds((S, CW), F32), sds((S, CW), F32), sds((S, AW), BF), sds((S, KVW), BF), sds((S, KVW), BF),
                   sds((S, CW), F32)],
        in_specs=[_VMEM] * 6 + [pl.BlockSpec((TS, D), early), pl.BlockSpec((1, D), _const),
                                pl.BlockSpec((1, HD), _const), pl.BlockSpec((1, HD), _const), t128, t128,
                                pl.BlockSpec((AW, AW), _const), pl.BlockSpec((KVW, KVW), _const)],
        out_specs=[_ANY, _ANY] + [_VMEM] * 3 + [pl.BlockSpec((TS, D), late), t512, t128, t512, t512, t512, t512, t512,
                                                t128, t128, t512],
        scratch_shapes=[pltpu.VMEM((NDEV, CK, CONV_SHARD), F32), pltpu.VMEM((NDEV, 8, D), F32),
                        pltpu.VMEM((NDEV, 8, ADA_SHARD), F32), pltpu.VMEM((S, D), BF), pltpu.VMEM((S, HALF), F32),
                        pltpu.VMEM((TS, INW), F32), pltpu.VMEM((INW, D), BF), pltpu.VMEM((D, D), BF)]
        + [sem] * 10 + [pltpu.SemaphoreType.DMA((2,))],
        compiler_params=_params(True),
    )(w_in_t, w_out_s, conv_w_s, c, w_ada_s, b_ada, x2, norm_w, qw_t, kw_t, cos_t, sin_t, bq, bk)


QB = 4
NQB = NB // QB


def _band_masks(has_prev):
    kj = lax.broadcasted_iota(jnp.int32, (2 * BLK, 4 * BLK), 0)
    qi = lax.broadcasted_iota(jnp.int32, (2 * BLK, 4 * BLK), 1) & (BLK - 1)
    dist = qi + BLK - kj
    local = (dist >= 0) & (dist < BLK)
    return local & ((kj >= BLK) | has_prev), local


def _key_blocks(sb, prev_ref, cur_ref):
    prev = prev_ref[...] if sb == 0 else cur_ref[BLK * (sb - 1):BLK * sb, :]
    return prev, cur_ref[BLK * sb:BLK * (sb + 1), :]


def _sink_lanes(sink_ref, g):
    lane = lax.broadcasted_iota(jnp.int32, (1, 4 * BLK), 1)
    return jnp.where(lane < BLK, sink_ref[0, 4 * g],
                     jnp.where(lane < 2 * BLK, sink_ref[0, 4 * g + 1],
                               jnp.where(lane < 3 * BLK, sink_ref[0, 4 * g + 2], sink_ref[0, 4 * g + 3])))


def _unstack_t(t):
    return [t[:, BLK * h:BLK * (h + 1)].T for h in range(4)]


def _stack_heads(t, g):
    return jnp.concatenate([t[:, HD * (4 * g + h):HD * (4 * g + h + 1)] for h in range(4)], axis=0)


def _band(prev, cur, g):
    return jnp.concatenate([prev[:, HD * g:HD * (g + 1)], cur[:, HD * g:HD * (g + 1)]], axis=0)


def _softmax_band(qs, kb, mask, sink):
    s = lax.dot_general(kb, qs, (((1,), (1,)), ((), ())), preferred_element_type=F32) * (HD ** -0.5)
    s = jnp.where(mask, s, NEG)
    m = jnp.maximum(jnp.max(s, axis=0, keepdims=True), sink)
    e = jnp.exp(s - m)
    es = jnp.exp(sink - m)
    inv = 1.0 / (jnp.sum(e, axis=0, keepdims=True) + es)
    return e * inv, es * inv


HALO = 32


RC = 64
LC = 128


def _windows(ext_ref, r0, l0, base):
    col = ext_ref[pl.ds(r0, RC + HALO), pl.ds(l0, LC)]
    for s in range(8):
        rolled = col if s == 0 else pltpu.roll(col, RC + HALO - s, 0)
        for t in range(CK):
            if (base + t) % 8 == s:
                a8 = base + t - s
                yield t, rolled[a8:a8 + RC]


def _taps(ext_ref, r0, l0, base, cw_ref, flip):
    acc = None
    for t, win in _windows(ext_ref, r0, l0, base):
        k = CK - 1 - t if flip else t
        term = win * cw_ref[k:k + 1, pl.ds(l0, LC)]
        acc = term if acc is None else acc + term
    return acc


NSUB = (BLK // RC) * (CW // LC)


def _sub_tile(t, row_base):
    r0 = pl.multiple_of(row_base + (t // (CW // LC)) * RC, RC)
    l0 = pl.multiple_of((t % (CW // LC)) * LC, LC)
    return r0, l0


FR = QB * BLK


def _fwd_tail_call(sinks, qr, kr, vb, z, gb, cwf, conv_b, ln_w, ln_b, ga, x2, tgt, mod, w_out_full):
    def body(sink_ref, q_ref, kp_ref, kc_ref, vp_ref, vc_ref, z_ref, zh_ref, gb_ref, cw_ref, cb_ref, lw_ref, lb_ref,
             ga_ref, x_ref, t_ref, mod_ref, w_ref,
             o_ref, zc_ref, dout_ref, dya_ref, dyb_ref, gw_ref, loss_ref, dgate_ref,
             zext, yb_ref):
        i = pl.program_id(0)

        @pl.when(i == 0)
        def _():
            gw_ref[...] = jnp.zeros((D, D), F32)
            loss_ref[...] = jnp.zeros((1, 128), F32)
            dgate_ref[...] = jnp.zeros((1, D), F32)

        def out_proj(rows):
            gav = ga_ref[rows, :]
            ya = o_ref[rows, :] * (gav * _sigmoid(gav))
            ycat = jnp.concatenate([ya.astype(BF), yb_ref[rows, :]], axis=1)
            w = w_ref[...]
            y = jnp.dot(ycat, w, preferred_element_type=F32)
            gate = mod_ref[:, 2 * D:3 * D]
            diff = x_ref[rows, :] + gate * y - t_ref[rows, :]
            sq = jnp.sum(jnp.sum(diff * diff, axis=1, keepdims=True), axis=0, keepdims=True)
            loss_ref[...] += jnp.broadcast_to(sq, (1, 128))
            d_out = diff * (1.0 / D)
            dout_ref[rows, :] = d_out
            dgate_ref[...] += jnp.sum(d_out * y, axis=0, keepdims=True)
            dy = (d_out * gate).astype(BF)
            d_ycat = lax.dot_general(dy, w, (((1,), (1,)), ((), ())), preferred_element_type=F32)
            dya_ref[rows, :] = d_ycat[:, 0:AW]
            dyb_ref[rows, :] = d_ycat[:, AW:D]
            gw_ref[...] += lax.dot_general(ycat, dy, (((0,), (0,)), ((), ())), preferred_element_type=F32)

        zext[0:HALO, :] = jnp.where(i > 0, zh_ref[...], 0.0)
        zext[HALO:HALO + FR, :] = z_ref[...]
        masks = _band_masks(i > 0)
        for sb in range(QB):
            rows = slice(BLK * sb, BLK * (sb + 1))
            mask = masks[min(sb, 1)]
            q = q_ref[rows, :]
            kp, kc = _key_blocks(sb, kp_ref, kc_ref)
            vp, vc = _key_blocks(sb, vp_ref, vc_ref)
            for g in range(NKV):
                p, _ = _softmax_band(_stack_heads(q, g), _band(kp, kc, g), mask, _sink_lanes(sink_ref, g))
                o_t = lax.dot_general(_band(vp, vc, g), p.astype(BF), (((0,), (0,)), ((), ())),
                                      preferred_element_type=F32)
                for h, o_h in enumerate(_unstack_t(o_t)):
                    o_ref[rows, HD * (4 * g + h):HD * (4 * g + h + 1)] = o_h
            for r0 in range(BLK * sb, BLK * (sb + 1), RC):
                for l0 in range(0, CW, LC):
                    acc = _taps(zext, r0, l0, HALO - (CK - 1), cw_ref, False)
                    zc_ref[r0:r0 + RC, l0:l0 + LC] = acc + cb_ref[:, l0:l0 + LC]
            zc = zc_ref[rows, :]
            mu = jnp.mean(zc, axis=-1, keepdims=True)
            dz = zc - mu
            rstd = lax.rsqrt(jnp.mean(dz * dz, axis=-1, keepdims=True) + EPS)
            zn = dz * rstd * lw_ref[...] + lb_ref[...]
            gbv = gb_ref[rows, :]
            yb_ref[rows, :] = ((zn * _sigmoid(zn)) * (gbv * _sigmoid(gbv))).astype(BF)
            if (sb + 1) * BLK % TS == 0:
                out_proj(slice((sb + 1) * BLK - TS, (sb + 1) * BLK))

    prev = lambda i: (jnp.maximum(QB * i - 1, 0), 0)
    halo = lambda i: (jnp.maximum(FR // HALO * i - 1, 0), 0)
    f512 = pl.BlockSpec((FR, 512), _row)
    f128 = pl.BlockSpec((FR, KVW), _row)
    f1024 = pl.BlockSpec((FR, D), _row)
    c512 = pl.BlockSpec((1, CW), _const)
    sds = jax.ShapeDtypeStruct
    return pl.pallas_call(
        body, name="fwd_tail", grid=(NQB,),
        out_shape=[sds((S, AW), F32), sds((S, CW), F32), sds((S, D), F32), sds((S, AW), F32), sds((S, CW), F32),
                   sds((D, D), F32), sds((1, 128), F32), sds((1, D), F32)],
        in_specs=[_SMEM, f512, pl.BlockSpec((BLK, KVW), prev), f128, pl.BlockSpec((BLK, KVW), prev), f128,
                  f512, pl.BlockSpec((HALO, CW), halo), f512, pl.BlockSpec((CKP, CW), _const), c512, c512, c512,
                  f512, f1024, f1024, pl.BlockSpec((1, 3 * D), _const),
                  pl.BlockSpec((D, D), _const, pipeline_mode=pl.Buffered(1))],
        out_specs=[f512, f512, f1024, f512, f512, _VMEM, _VMEM, _VMEM],
        scratch_shapes=[pltpu.VMEM((FR + HALO, CW), F32), pltpu.VMEM((FR, CW), BF)],
        compiler_params=_params(True),
    )(sinks, qr, kr, kr, vb, vb, z, z, gb, cwf, conv_b, ln_w, ln_b, ga, x2, tgt, mod, w_out_full)


def _ln_gate_bwd(d_yb, zc, gbv, lw, lb):
    mu = jnp.mean(zc, axis=-1, keepdims=True)
    dz = zc - mu
    rstd = lax.rsqrt(jnp.mean(dz * dz, axis=-1, keepdims=True) + EPS)
    zh = dz * rstd
    zn = zh * lw + lb
    act_n, dact_n = _silu_and_grad(zn)
    act_g, dact_g = _silu_and_grad(gbv)
    d_gb = d_yb * act_n * dact_g
    d_zn = d_yb * act_g * dact_n
    dzh = d_zn * lw
    d_zc = rstd * (dzh - jnp.mean(dzh, axis=-1, keepdims=True) - zh * jnp.mean(dzh * zh, axis=-1, keepdims=True))
    return d_zc, d_gb, d_zn, zh


def _attn_conv_bwd_call(sinks, qr, kr, vb, d_ya, ga, o, qraw, qw_t, cos_t, sin_t, bq, d_yb, zc, gb, z, a, g, cwf,
                        ln_w, ln_b):
    def body(sink_ref, q_ref, kp_ref, kc_ref, vp_ref, vc_ref, dya_ref, ga_ref, o_ref, qraw_ref, qw_ref,
             cos_ref, sin_ref, bq_ref,
             dyb_ref, dybn_ref, zc_ref, zcn_ref, gb_ref, gbn_ref, z_ref, zh_ref, a_ref, g_ref, cw_ref, lw_ref, lb_ref,
             dqraw_ref, dga_ref, dk_ref, dv_ref, gqw_ref, gsink_ref,
             dgb_ref, da_ref, dg_ref, gcw_ref, glw_ref, glb_ref, gcb_ref,
             dext, zext, gacc):
        i = pl.program_id(0)

        @pl.when(i == 0)
        def _():
            dk_ref[...] = jnp.zeros((S, KVW), F32)
            dv_ref[...] = jnp.zeros((S, KVW), F32)
            gqw_ref[...] = jnp.zeros((1, AW), F32)
            gsink_ref[...] = jnp.zeros((1, 128), F32)
            gacc[...] = jnp.zeros((CKP * 8, CW), F32)
            glw_ref[...] = jnp.zeros((1, CW), F32)
            glb_ref[...] = jnp.zeros((1, CW), F32)
            gcb_ref[...] = jnp.zeros((1, CW), F32)

        lw = lw_ref[...]
        lb = lb_ref[...]

        def ln_rows(rows):
            d_zc, d_gb, d_zn, zh = _ln_gate_bwd(dyb_ref[rows, :], zc_ref[rows, :], gb_ref[rows, :], lw, lb)
            dgb_ref[rows, :] = d_gb.astype(BF)
            glw_ref[...] += jnp.sum(d_zn * zh, axis=0, keepdims=True)
            glb_ref[...] += jnp.sum(d_zn, axis=0, keepdims=True)
            gcb_ref[...] += jnp.sum(d_zc, axis=0, keepdims=True)
            dext[rows, :] = d_zc

        ln_rows(slice(0, BLK))
        zext[0:HALO, :] = jnp.where(i > 0, zh_ref[...], 0.0)
        zext[HALO:HALO + FR, :] = z_ref[...]

        lane = lax.broadcasted_iota(jnp.int32, (1, 128), 1)
        gsink = jnp.zeros((1, 128), F32)
        dq_rows = []
        masks = _band_masks(i > 0)
        for sb in range(QB):
            rows = slice(BLK * sb, BLK * (sb + 1))
            if sb + 1 < QB:
                ln_rows(slice(BLK * (sb + 1), BLK * (sb + 2)))
            else:
                d_zc_next, _, _, _ = _ln_gate_bwd(dybn_ref[...], zcn_ref[...], gbn_ref[...], lw, lb)
                dext[FR:FR + HALO, :] = jnp.where(i < NQB - 1, d_zc_next, 0.0)
            mask = masks[min(sb, 1)]
            q = q_ref[rows, :]
            d_ya = dya_ref[rows, :]
            act, dact = _silu_and_grad(ga_ref[rows, :])
            dga_ref[rows, :] = (d_ya * o_ref[rows, :] * dact).astype(BF)
            d_o = (d_ya * act).astype(BF)
            kp, kc = _key_blocks(sb, kp_ref, kc_ref)
            vp, vc = _key_blocks(sb, vp_ref, vc_ref)
            dq_parts, dk_parts, dv_parts = [], [], []
            for gi in range(NKV):
                qs = _stack_heads(q, gi)
                kb = _band(kp, kc, gi)
                vbd = _band(vp, vc, gi)
                p, ps = _softmax_band(qs, kb, mask, _sink_lanes(sink_ref, gi))
                dos = _stack_heads(d_o, gi)
                dp = lax.dot_general(vbd, dos, (((1,), (1,)), ((), ())), preferred_element_type=F32)
                dr = jnp.sum(p * dp, axis=0, keepdims=True)
                ds = (p * (dp - dr) * (HD ** -0.5)).astype(BF)
                sink_term = ps * dr
                for h in range(4):
                    part = jnp.sum(sink_term[:, BLK * h:BLK * (h + 1)], axis=1, keepdims=True)
                    gsink = gsink - jnp.where(lane == 4 * gi + h, part, 0.0)
                dv_parts.append(jnp.dot(p.astype(BF), dos, preferred_element_type=F32))
                dk_parts.append(jnp.dot(ds, qs, preferred_element_type=F32))
                dq_t = lax.dot_general(kb, ds, (((0,), (0,)), ((), ())), preferred_element_type=F32)
                dq_parts.extend(_unstack_t(dq_t))
            dkb = jnp.concatenate(dk_parts, axis=1)
            dvb = jnp.concatenate(dv_parts, axis=1)
            blk = QB * i + sb
            r_prev = pl.multiple_of(jnp.maximum(blk - 1, 0) * BLK, BLK)
            r_cur = pl.multiple_of(blk * BLK, BLK)
            dk_ref[pl.ds(r_prev, BLK), :] += dkb[0:BLK]
            dv_ref[pl.ds(r_prev, BLK), :] += dvb[0:BLK]
            dk_ref[pl.ds(r_cur, BLK), :] += dkb[BLK:2 * BLK]
            dv_ref[pl.ds(r_cur, BLK), :] += dvb[BLK:2 * BLK]
            dq_rows.append(jnp.concatenate(dq_parts, axis=1))

            def taps_sub(t, carry, sb=sb):
                r0, l0 = _sub_tile(t, BLK * sb)
                here = (pl.ds(r0, RC), pl.ds(l0, LC))
                d_z = _taps(dext, r0, l0, 0, cw_ref, True)
                sg = _sigmoid(g_ref[here])
                da_ref[here] = (d_z * sg).astype(BF)
                dg_ref[here] = (d_z * a_ref[here] * sg * (1.0 - sg)).astype(BF)
                d_sub = dext[here]
                for k, win in _windows(zext, r0, l0, HALO - (CK - 1)):
                    prod = d_sub * win
                    part = prod[0:8]
                    for q8 in range(1, RC // 8):
                        part = part + prod[8 * q8:8 * q8 + 8]
                    gacc[8 * k:8 * k + 8, pl.ds(l0, LC)] += part
                return carry

            lax.fori_loop(0, NSUB, taps_sub, 0)
        gsink_ref[...] += gsink
        dq = jnp.concatenate(dq_rows, axis=0)
        dq_raw, g_qw = _norm_rope_bwd(dq, qraw_ref[...], jnp.tile(qw_ref[...], (1, NQ)),
                                      jnp.tile(cos_ref[...], (1, 4)), jnp.tile(sin_ref[...], (1, 4)), bq_ref)
        dqraw_ref[...] = dq_raw.astype(BF)
        gqw_ref[...] += g_qw

        @pl.when(i == NQB - 1)
        def _():
            for k in range(CK):
                gcw_ref[k:k + 1, :] = jnp.sum(gacc[8 * k:8 * k + 8, :], axis=0, keepdims=True)
            gcw_ref[CK:CKP, :] = jnp.zeros((CKP - CK, CW), F32)

    prev = lambda i: (jnp.maximum(QB * i - 1, 0), 0)
    halo_prev = lambda i: (jnp.maximum(FR // HALO * i - 1, 0), 0)
    halo_next = lambda i: (jnp.minimum(FR // HALO * (i + 1), S // HALO - 1), 0)
    f512 = pl.BlockSpec((FR, 512), _row)
    f128 = pl.BlockSpec((FR, 128), _row)
    hn = pl.BlockSpec((HALO, CW), halo_next)
    c512 = pl.BlockSpec((1, CW), _const)
    sds = jax.ShapeDtypeStruct
    vec = sds((1, CW), F32)
    return pl.pallas_call(
        body, name="attn_conv_bwd", grid=(NQB,),
        out_shape=[sds((S, AW), BF), sds((S, AW), BF), sds((S, KVW), F32), sds((S, KVW), F32), sds((1, AW), F32),
                   sds((1, 128), F32),
                   sds((S, CW), BF), sds((S, CW), BF), sds((S, CW), BF), sds((CKP, CW), F32), vec, vec, vec],
        in_specs=[_SMEM, f512, pl.BlockSpec((BLK, KVW), prev), f128, pl.BlockSpec((BLK, KVW), prev), f128,
                  f512, f512, f512, f512, pl.BlockSpec((1, HD), _const), f128, f128, pl.BlockSpec((AW, AW), _const),
                  f512, hn, f512, hn, f512, hn, f512, pl.BlockSpec((HALO, CW), halo_prev), f512, f512,
                  pl.BlockSpec((CKP, CW), _const), c512, c512],
        out_specs=[f512, f512, _VMEM, _VMEM, _VMEM, _VMEM, f512, f512, f512, _VMEM, _VMEM, _VMEM, _VMEM],
        scratch_shapes=[pltpu.VMEM((FR + HALO, CW), F32), pltpu.VMEM((FR + HALO, CW), F32),
                        pltpu.VMEM((CKP * 8, CW), F32)],
        compiler_params=_params(True),
    )(sinks, qr, kr, kr, vb, vb, d_ya, ga, o, qraw, qw_t, cos_t, sin_t, bq,
      d_yb, d_yb, zc, zc, gb, gb, z, z, a, g, cwf, ln_w, ln_b)


SM_ROWS = 8
PIECES = ((0, 512), (512, 640), (640, 768), (768, 1280), (1280, 1792), (1792, 2304), (2304, 2816))


def _bwd_in_call(dqraw, dk, dv, dga, da, dg, dgb, kraw, kw_t, cos_t, sin_t, bk, h, wt_full, x2, d_out, mod, norm_w,
                 gw_out, gcw, glw, glb, gcb, gqw, gsink, dgate, loss_p, cact_all):
    def body(dq_ref, dk_ref, dv_ref, dga_ref, da_ref, dg_ref, dgb_ref, kraw_ref, kw_ref, cos_ref, sin_ref, bk_ref,
             h_ref, wt_ref, x_ref, dout_ref, mod_ref, nw_ref, gwout_ref, gcw_ref, glw_ref, glb_ref, gcb_ref, gqw_ref,
             gsink_ref, dgate_ref, loss_ref, cact_ref,
             gx_ref, o_gwin, o_gwout, o_gwada, o_gbada, o_gnw, o_gqw, o_gkw, o_gsink, o_gcw, o_gcb, o_glw, o_glb,
             o_loss,
             acc, win_send, win_sib, win_ici, wout_send, wout_sib, wout_ici, sm_buf, cw_buf, dmod_all, vec_acc, gkw_acc,
             wi_ds, wi_dr, wi_is, wi_ir, wo_ds, wo_dr, wo_is, wo_ir, sm_s, sm_r, cw_s, cw_r):
        i = pl.program_id(0)
        pos = _mesh_pos()
        x, y, cc = pos
        me = 4 * x + 2 * y + cc

        def chip(j):
            return (1 - x if j & 1 else x, 1 - y if j & 2 else y)

        def rows_of(buf, px, py, pc, rows, align):
            return buf.at[pl.ds(pl.multiple_of((4 * px + 2 * py + pc) * rows, align), rows), :]

        bufs = {"in": (win_send, win_sib, win_ici, IN_SHARD, wi_ds, wi_dr, wi_is, wi_ir),
                "out": (wout_send, wout_sib, wout_ici, OUT_SHARD, wo_ds, wo_dr, wo_is, wo_ir)}

        def d2d_copy(j, which):
            send, sib, _, rows, ds_, dr_, _, _ = bufs[which]
            px, py = chip(j)
            return pltpu.make_async_remote_copy(src_ref=rows_of(send, px, py, 1 - cc, rows, 16), dst_ref=sib.at[j],
                                                send_sem=ds_.at[j], recv_sem=dr_.at[j], device_id=(x, y, 1 - cc),
                                                device_id_type=MESH)

        def ici_copy(j, which):
            send, _, ici, rows, _, _, is_, ir_ = bufs[which]
            px, py = chip(j)
            return pltpu.make_async_remote_copy(src_ref=rows_of(send, px, py, cc, rows, 16), dst_ref=ici.at[j - 1],
                                                send_sem=is_.at[j - 1], recv_sem=ir_.at[j - 1], device_id=(px, py, cc),
                                                device_id_type=MESH)

        def level2(which, partial_ref):
            send, sib, _, rows, _, _, _, _ = bufs[which]
            for j in range(1, 4):
                d2d_copy(j, which).wait_recv()
                px, py = chip(j)
                mine = rows_of(partial_ref, px, py, cc, rows, 8)[...]
                rows_of(send, px, py, cc, rows, 16)[...] = (mine + sib[j].astype(F32)).astype(BF)
                ici_copy(j, which).start()

        def finish(which, partial_ref):
            _, sib, ici, rows, _, _, _, _ = bufs[which]
            d2d_copy(0, which).wait_recv()
            total = rows_of(partial_ref, x, y, cc, rows, 8)[...] + sib[0].astype(F32)
            for j in range(1, 4):
                ici_copy(j, which).wait_recv()
                total = total + ici[j - 1].astype(F32)
            for j in range(4):
                d2d_copy(j, which).wait_send()
            for j in range(1, 4):
                ici_copy(j, which).wait_send()
            return total

        def dproj_pieces():
            dk_raw, g_kw = _norm_rope_bwd(dk_ref[...], kraw_ref[...], jnp.tile(kw_ref[...], (1, NKV)), cos_ref[...],
                                          sin_ref[...], bk_ref)
            return [dq_ref[...], dk_raw.astype(BF), dv_ref[...].astype(BF), dga_ref[...], da_ref[...], dg_ref[...],
                    dgb_ref[...]], g_kw

        @pl.when(i == 0)
        def _():
            acc[...] = jnp.zeros((INW, D), F32)
            vec_acc[...] = jnp.zeros((8, D), F32)
            gkw_acc[...] = jnp.zeros((1, KVW), F32)
            wout_send[...] = gwout_ref[...].astype(BF)
            for j in range(4):
                d2d_copy(j, "out").start()

        @pl.when(i == 2)
        def _():
            level2("out", gwout_ref)

        @pl.when(i < NT)
        def _():
            pieces, g_kw = dproj_pieces()
            gkw_acc[...] += g_kw
            hv = h_ref[...]
            for (lo, hi), piece in zip(PIECES, pieces):
                acc[lo:hi, :] += lax.dot_general(piece, hv, (((0,), (0,)), ((), ())), preferred_element_type=F32)

        @pl.when(i == NT - 1)
        def _():
            for lo, hi in PIECES:
                win_send[lo:hi, :] = acc[lo:hi, :].astype(BF)
            for j in range(4):
                d2d_copy(j, "in").start()

        @pl.when(i == NT + 2)
        def _():
            level2("in", acc)

        @pl.when(i >= NT)
        def _():
            pieces, _ = dproj_pieces()
            dproj = jnp.concatenate(pieces, axis=1)
            d_h = jnp.dot(dproj, wt_ref[...], preferred_element_type=F32)
            xv = x_ref[...]
            scale = mod_ref[:, D:2 * D]
            nw = nw_ref[...]
            r = lax.rsqrt(jnp.mean(xv * xv, axis=-1, keepdims=True) + EPS)
            xn = xv * r
            vec_acc[0:1, :] += jnp.sum(d_h, axis=0, keepdims=True)
            vec_acc[1:2, :] += jnp.sum(d_h * (xn * nw), axis=0, keepdims=True)
            d_u = d_h * (1.0 + scale)
            vec_acc[2:3, :] += jnp.sum(d_u * xn, axis=0, keepdims=True)
            d_xn = d_u * nw
            gx_ref[...] = dout_ref[...] + r * (d_xn - xn * jnp.mean(d_xn * xn, axis=-1, keepdims=True))

        @pl.when(i == 2 * NT - 1)
        def _():
            ch_sm = (_slab(sm_buf), sm_s, sm_r)
            ch_cw = (_slab(cw_buf), cw_s, cw_r)
            z128 = jnp.zeros((1, 128), F32)
            row4 = jnp.concatenate([glw_ref[...], glb_ref[...]], axis=1)
            row5 = jnp.concatenate([gcb_ref[...], gqw_ref[...]], axis=1)
            row6 = jnp.concatenate([gkw_acc[...], gsink_ref[...], loss_ref[...]] + [z128] * 5, axis=1)
            sm_buf[me] = jnp.concatenate([vec_acc[0:2, :], dgate_ref[...], vec_acc[2:3, :], row4, row5, row6,
                                          jnp.zeros((1, D), F32)], axis=0)
            f_sm = _ag_start(ch_sm, pos)
            cw_buf[me] = gcw_ref[...]
            f_cw = _ag_start(ch_cw, pos)
            _ag_finish(ch_sm, pos, f_sm)
            _ag_finish(ch_cw, pos, f_cw)
            tot = sm_buf[0]
            cw_tot = cw_buf[0]
            for d in range(1, NDEV):
                tot = tot + sm_buf[d]
                cw_tot = cw_tot + cw_buf[d]
            o_gbada[...] = jnp.concatenate([tot[0:1, :], tot[1:2, :], tot[2:3, :]], axis=1)
            o_gnw[...] = tot[3:4, :]
            o_glw[...] = tot[4:5, 0:CW]
            o_glb[...] = tot[4:5, CW:D]
            o_gcb[...] = tot[5:6, 0:CW]
            gq = tot[5:6, CW:CW + HD]
            for hh in range(1, NQ):
                gq = gq + tot[5:6, CW + HD * hh:CW + HD * (hh + 1)]
            o_gqw[...] = gq
            o_gkw[...] = tot[6:7, 0:HD] + tot[6:7, HD:2 * HD]
            o_gsink[...] = tot[6:7, 128:128 + NQ]
            o_loss[...] = tot[6:7, 256:384] * (0.5 / D)
            mine = jnp.zeros((CK, CONV_SHARD), F32)
            for d in range(NDEV):
                mine = mine + jnp.where(me == d, cw_tot[0:CK, CONV_SHARD * d:CONV_SHARD * (d + 1)], 0.0)
            for k in range(CK):
                o_gcw[k] = mine[k:k + 1, :]
            for d in range(NDEV):
                dmod_all[d:d + 1, :] = jnp.concatenate([sm_buf[d, 0:1, :], sm_buf[d, 1:2, :], sm_buf[d, 2:3, :]],
                                                       axis=1)
            col0 = pl.multiple_of(me * ADA_SHARD, 128)
            o_gwada[...] = lax.dot_general(cact_ref[...], dmod_all[:, pl.ds(col0, ADA_SHARD)], (((0,), (0,)), ((), ())),
                                           preferred_element_type=F32, precision=lax.Precision.HIGHEST)

            o_gwout[...] = finish("out", gwout_ref)
            o_gwin[...] = finish("in", acc)

    half = lambda i: (i % NT, 0)
    late = lambda i: (jnp.maximum(i - NT, 0), 0)
    t512 = pl.BlockSpec((TS, 512), half)
    t128 = pl.BlockSpec((TS, 128), half)
    l1024 = pl.BlockSpec((TS, D), late)
    sem7 = pltpu.SemaphoreType.DMA((7,))
    sem4 = pltpu.SemaphoreType.DMA((4,))
    sem3 = pltpu.SemaphoreType.DMA((3,))
    sds = jax.ShapeDtypeStruct
    return pl.pallas_call(
        body, name="bwd_in", grid=(2 * NT,),
        out_shape=[sds((S, D), F32), sds((IN_SHARD, D), F32), sds((OUT_SHARD, D), F32), sds((D, ADA_SHARD), F32),
                   sds((1, 3 * D), F32), sds((1, D), F32), sds((1, HD), F32), sds((1, HD), F32), sds((1, NQ), F32),
                   sds((CK, 1, CONV_SHARD), F32), sds((1, CW), F32), sds((1, CW), F32), sds((1, CW), F32),
                   sds((1, 128), F32)],
        in_specs=[t512, t128, t128, t512, t512, t512, t512, t128, pl.BlockSpec((1, HD), _const), t128, t128,
                  pl.BlockSpec((KVW, KVW), _const), pl.BlockSpec((TS, D), half),
                  pl.BlockSpec((INW, D), _const, pipeline_mode=pl.Buffered(1)), l1024, l1024,
                  pl.BlockSpec((1, 3 * D), _const), pl.BlockSpec((1, D), _const)] + [_VMEM] * 10,
        out_specs=[l1024] + [_VMEM] * 13,
        scratch_shapes=[pltpu.VMEM((INW, D), F32), pltpu.VMEM((INW, D), BF), pltpu.VMEM((4, IN_SHARD, D), BF),
                        pltpu.VMEM((3, IN_SHARD, D), BF), pltpu.VMEM((D, D), BF), pltpu.VMEM((4, OUT_SHARD, D), BF),
                        pltpu.VMEM((3, OUT_SHARD, D), BF),
                        pltpu.VMEM((NDEV, SM_ROWS, D), F32), pltpu.VMEM((NDEV, CKP, CW), F32),
                        pltpu.VMEM((NDEV, 3 * D), F32), pltpu.VMEM((8, D), F32), pltpu.VMEM((1, KVW), F32)]
        + [sem4, sem4, sem3, sem3] * 2 + [sem7] * 4,
        compiler_params=pltpu.CompilerParams(dimension_semantics=("arbitrary",), vmem_limit_bytes=BIG_VMEM_LIMIT),
    )(dqraw, dk, dv, dga, da, dg, dgb, kraw, kw_t, cos_t, sin_t, bk, h, wt_full, x2, d_out, mod, norm_w,
      gw_out, gcw, glw, glb, gcb, gqw, gsink, dgate, loss_p, cact_all)


ADAM_STEPS = 4


def _adam_chunking(arr):
    if arr.ndim == 2 and arr.shape[0] % (8 * ADAM_STEPS) == 0:
        return "rows"
    if arr.ndim == 2 and arr.shape[1] % (128 * ADAM_STEPS) == 0:
        return "cols"
    return None


def _adam_call(ws, gs, ms, vs, grad_x, loss_v):
    n = len(ws)
    bc1 = 1.0 - ADAM_B1 ** ADAM_STEP
    bc2 = 1.0 - ADAM_B2 ** ADAM_STEP
    chunked = [_adam_chunking(w) for w in ws]

    def body(*refs):
        ins, outs = refs[:4 * n + 2], refs[4 * n + 2:]
        i = pl.program_id(0)

        def update(j):
            w, g, m, v = (ins[j][...], ins[n + j][...], ins[2 * n + j][...], ins[3 * n + j][...])
            m_new = ADAM_B1 * m + (1.0 - ADAM_B1) * g
            v_new = ADAM_B2 * v + (1.0 - ADAM_B2) * (g * g)
            m_hat = m_new / bc1
            v_hat = v_new / bc2
            outs[j][...] = g
            outs[n + j][...] = -ADAM_LR * (m_hat / (jnp.sqrt(v_hat) + ADAM_EPS) + ADAM_WD * w)
            outs[2 * n + j][...] = m_new
            outs[3 * n + j][...] = v_new

        for j in range(n):
            if chunked[j]:
                update(j)
        outs[4 * n][...] = ins[4 * n][...]

        @pl.when(i == 0)
        def _():
            for j in range(n):
                if not chunked[j]:
                    update(j)
            outs[4 * n + 1][...] = ins[4 * n + 1][...]

    def spec(arr, how):
        if how == "rows":
            return pl.BlockSpec((arr.shape[0] // ADAM_STEPS, arr.shape[1]), _row)
        if how == "cols":
            return pl.BlockSpec((arr.shape[0], arr.shape[1] // ADAM_STEPS), lambda i: (0, i))
        zeros = (0,) * arr.ndim
        return pl.BlockSpec(arr.shape, lambda i: zeros)

    par_specs = [spec(w, ch) for w, ch in zip(ws, chunked)]
    extra = [spec(grad_x, "rows"), spec(loss_v, None)]
    shapes = [jax.ShapeDtypeStruct(w.shape, F32) for w in ws]
    res = pl.pallas_call(
        body, name="adam", grid=(ADAM_STEPS,),
        out_shape=shapes * 4 + [jax.ShapeDtypeStruct(grad_x.shape, F32), jax.ShapeDtypeStruct(loss_v.shape, F32)],
        in_specs=par_specs * 4 + extra, out_specs=par_specs * 4 + extra,
        compiler_params=_params(True),
    )(*ws, *gs, *ms, *vs, grad_x, loss_v)
    return res[0:n], res[n:2 * n], res[2 * n:3 * n], res[3 * n:4 * n], res[4 * n], res[4 * n + 1]


def _rope_tables():
    inv = (np.float32(ROPE_THETA) ** (-np.arange(0, HD, 2, dtype=np.float32) / np.float32(HD))).astype(np.float32)
    ang = (np.arange(S, dtype=np.float32)[:, None] * inv[None, :]).astype(np.float32)
    cos, sin = np.cos(ang).astype(np.float32), np.sin(ang).astype(np.float32)
    cos64 = np.concatenate([cos, cos], axis=-1)
    sin64 = np.concatenate([-sin, sin], axis=-1)
    return jnp.asarray(np.tile(cos64, (1, 2))), jnp.asarray(np.tile(sin64, (1, 2)))


def _group_matrix(width):
    idx = np.arange(width) // HD
    return jnp.asarray(np.where(idx[:, None] == idx[None, :], 1.0 / HD, 0.0).astype(np.float32)).astype(BF)


def kernel(x, c, w_ada, b_ada, norm_w, w_in, q_norm_w, k_norm_w, sinks, conv_w, conv_b, ln_w, ln_b, w_out, loss_target, m_w_ada, m_b_ada, m_norm_w, m_w_in, m_q_norm_w, m_k_norm_w, m_sinks, m_conv_w, m_conv_b, m_ln_w, m_ln_b, m_w_out, v_w_ada, v_b_ada, v_norm_w, v_w_in, v_q_norm_w, v_k_norm_w, v_sinks, v_conv_w, v_conv_b, v_ln_w, v_ln_b, v_w_out):
    x2 = x[0]
    tgt = loss_target[0]
    cos_t, sin_t = _rope_tables()
    bq = _group_matrix(AW)
    bk = _group_matrix(KVW)
    qw_t, kw_t = q_norm_w, k_norm_w

    tr = lambda t: jnp.swapaxes(t[0], 0, 1)
    tc = lambda t: jnp.swapaxes(t, 0, 1)
    (wt_full, w_out_full, cwf, mod, cact_all, h, qraw, kraw, ga, a, g, gb, qr, kr, vb, z) = _gather_fwd_call(
        tr(w_in), w_out[0], tc(conv_w), c, w_ada[0], b_ada, x2, norm_w, qw_t, kw_t, cos_t, sin_t, bq, bk)
    o, zc, d_out, d_ya, d_yb, gw_out, loss_p, dgate = _fwd_tail_call(
        sinks, qr, kr, vb, z, gb, cwf, conv_b, ln_w, ln_b, ga, x2, tgt, mod, w_out_full)

    dqraw, dga, dk, dv, gqw, gsink, dgb, da, dg, gcw, glw, glb, gcb = _attn_conv_bwd_call(
        sinks, qr, kr, vb, d_ya, ga, o, qraw, qw_t, cos_t, sin_t, bq, d_yb, zc, gb, z, a, g, cwf, ln_w, ln_b)
    (grad_x, g_w_in_t, g_w_out, g_w_ada, g_b_ada, g_norm_w, g_qw, g_kw, g_sinks, g_conv_w, g_conv_b, g_ln_w, g_ln_b,
     loss_v) = _bwd_in_call(dqraw, dk, dv, dga, da, dg, dgb, kraw, kw_t, cos_t, sin_t, bk, h, wt_full, x2, d_out, mod,
                            norm_w, gw_out, gcw, glw, glb, gcb, gqw, gsink, dgate, loss_p, cact_all)

    ws = [w_ada[0], b_ada, norm_w, tr(w_in), q_norm_w, k_norm_w, sinks, tc(conv_w), conv_b, ln_w, ln_b, w_out[0]]
    gs = [g_w_ada, g_b_ada, g_norm_w, g_w_in_t, g_qw, g_kw, g_sinks, g_conv_w, g_conv_b, g_ln_w, g_ln_b, g_w_out]
    ms = [m_w_ada[0], m_b_ada, m_norm_w, tr(m_w_in), m_q_norm_w, m_k_norm_w, m_sinks, tc(m_conv_w), m_conv_b, m_ln_w,
          m_ln_b, m_w_out[0]]
    vs = [v_w_ada[0], v_b_ada, v_norm_w, tr(v_w_in), v_q_norm_w, v_k_norm_w, v_sinks, tc(v_conv_w), v_conv_b, v_ln_w,
          v_ln_b, v_w_out[0]]
    grads, deltas, new_m, new_v, grad_x, loss_v = _adam_call(ws, gs, ms, vs, grad_x, loss_v)
    shaped = [w_ada, b_ada, norm_w, w_in, q_norm_w, k_norm_w, sinks, conv_w, conv_b, ln_w, ln_b, w_out]
    W_IN_POS, CONV_W_POS = 3, 7

    def like(vals):
        vals = [jnp.swapaxes(v, 0, 1) if j in (W_IN_POS, CONV_W_POS) else v for j, v in enumerate(vals)]
        return [v.reshape(s.shape) for v, s in zip(vals, shaped)]

    return (loss_v[0, 0], grad_x[None], *like(grads), *like(deltas), *like(new_m), *like(new_v))
```

```python
import functools

import jax
import jax.numpy as jnp
import numpy as np
from jax import lax
from jax.experimental import pallas as pl
from jax.experimental.pallas import tpu as pltpu

S = 2048
D = 1024
NDEV = 8
HD = 64
NQ = 8
NKV = 2
AW = 512
KVW = 128
CW = 512
INW = 2816
IN_SHARD = INW // NDEV
ADA_SHARD = 3 * D // NDEV
OUT_SHARD = D // NDEV
CONV_SHARD = CW // NDEV
CK = 31
CKP = 32
BLK = 128
TS = 256
NT = S // TS
NB = S // BLK
EPS = 1e-6
ROPE_THETA = 10000.0
NEG = -1e30
BF = jnp.bfloat16
F32 = jnp.float32

ADAM_LR = 0.001
ADAM_B1 = 0.9
ADAM_B2 = 0.999
ADAM_EPS = 1e-08
ADAM_WD = 0.01
ADAM_STEP = 10

VMEM_LIMIT = 56 * 1024 * 1024
BIG_VMEM_LIMIT = 62 * 1024 * 1024
MESH = pl.DeviceIdType.MESH

_VMEM = pl.BlockSpec(memory_space=pltpu.VMEM)
_SMEM = pl.BlockSpec(memory_space=pltpu.SMEM)
_ANY = pl.BlockSpec(memory_space=pl.ANY)


def _params(grid=False):
    if grid:
        return pltpu.CompilerParams(dimension_semantics=("arbitrary",), vmem_limit_bytes=VMEM_LIMIT)
    return pltpu.CompilerParams(vmem_limit_bytes=VMEM_LIMIT)


def _row(i):
    return (i, 0)


def _const(i):
    return (0, 0)


def _sigmoid(t):
    return 1.0 / (1.0 + jnp.exp(-t))


def _silu_and_grad(t):
    sg = _sigmoid(t)
    return t * sg, sg * (1.0 + t * (1.0 - sg))


def _group_mean(t, b_ref):
    hi = t.astype(BF)
    lo = (t - hi.astype(F32)).astype(BF)
    b = b_ref[...]
    return jnp.dot(hi, b, preferred_element_type=F32) + jnp.dot(lo, b, preferred_element_type=F32)


def _partner(t):
    w = t.shape[-1]
    lane = lax.broadcasted_iota(jnp.int32, t.shape, 1)
    first = (lane & 32) == 0
    return jnp.where(first, pltpu.roll(t, w - 32, 1), pltpu.roll(t, 32, 1))


def _norm_rope_fwd(t, w_t, cos, sin, b_ref):
    r = lax.rsqrt(_group_mean(t * t, b_ref) + EPS)
    tn = t * r * w_t
    return tn * cos + _partner(tn) * sin


def _norm_rope_bwd(d_out, t, w_t, cos, sin, b_ref):
    d_tn = d_out * cos + _partner(d_out * sin)
    r = lax.rsqrt(_group_mean(t * t, b_ref) + EPS)
    th = t * r
    g_w = jnp.sum(d_tn * th, axis=0, keepdims=True)
    d_th = d_tn * w_t
    d_t = r * (d_th - th * _group_mean(d_th * th, b_ref))
    return d_t, g_w


def _mesh_pos():
    return lax.axis_index("x"), lax.axis_index("y"), lax.axis_index("c")


def _ag_copy(chan, k, block, to):
    blk, send_sems, recv_sems = chan
    ref = blk(*block)
    return pltpu.make_async_remote_copy(src_ref=ref, dst_ref=ref, send_sem=send_sems.at[k],
                                        recv_sem=recv_sems.at[k], device_id=to, device_id_type=MESH)


def _ag_start(chan, pos):
    x, y, c = pos
    me = (x, y, c)
    chips = [(1 - x, y), (x, 1 - y), (1 - x, 1 - y)]
    first = [_ag_copy(chan, 0, me, (x, y, 1 - c))]
    first += [_ag_copy(chan, 1 + j, me, (*chip, c)) for j, chip in enumerate(chips)]
    for cp in first:
        cp.start()
    return first


def _ag_finish(chan, pos, first):
    x, y, c = pos
    me = (x, y, c)
    sibling = (x, y, 1 - c)
    chips = [(1 - x, y), (x, 1 - y), (1 - x, 1 - y)]
    passed = [_ag_copy(chan, 4 + j, (*chip, c), sibling) for j, chip in enumerate(chips)]
    for j, chip in enumerate(chips):
        _ag_copy(chan, 1 + j, (*chip, c), me).wait_recv()
        passed[j].start()
    _ag_copy(chan, 0, sibling, me).wait_recv()
    for j, chip in enumerate(chips):
        _ag_copy(chan, 4 + j, (*chip, 1 - c), me).wait_recv()
    for cp in first + passed:
        cp.wait_send()


def _slab(buf):
    return lambda px, py, pc: buf.at[4 * px + 2 * py + pc]


def _row_block(buf, rows, align):
    return lambda px, py, pc: buf.at[pl.ds(pl.multiple_of((4 * px + 2 * py + pc) * rows, align), rows), :]


HALF = INW // 2


def _gather_fwd_call(w_in_t, w_out_s, conv_w_s, c, w_ada_s, b_ada, x2, norm_w, qw_t, kw_t, cos_t, sin_t, bq, bk):
    def body(win_ref, wout_ref, cw_ref, c_ref, wada_ref, bada_ref, x_ref, nw_ref, qw_ref, kw_ref, cos_ref, sin_ref,
             bq_ref, bk_ref,
             wtf_hbm, woutf_hbm, cwf_ref, mod_ref, cact_ref,
             h_ref, qraw_ref, kraw_ref, ga_ref, a_ref, g_ref, gb_ref, qr_ref, kr_ref, vb_ref, z_ref,
             cw_buf, ca_buf, mp_buf, h_s, raw0, pt, wtf_ref, woutf_ref,
             s0, r0, s1, r1, s2, r2, s3, r3, s4, r4, out_sems):
        s = pl.program_id(0)
        pos = _mesh_pos()
        x, y, cc = pos
        me3 = (x, y, cc)
        me = 4 * x + 2 * y + cc
        sibling = (x, y, 1 - cc)
        chips = [(1 - x, y), (x, 1 - y), (1 - x, 1 - y)]
        ch_win = (_row_block(wtf_ref, IN_SHARD, 16), s0, r0)
        ch_wout = (_row_block(woutf_ref, OUT_SHARD, 16), s1, r1)
        ch_cw = (_slab(cw_buf), s2, r2)
        ch_ca = (_slab(ca_buf), s3, r3)
        ch_mp = (_slab(mp_buf), s4, r4)

        def first(chan, j):
            return _ag_copy(chan, j, me3, sibling if j == 0 else (*chips[j - 1], cc))

        def passed(chan, j):
            return _ag_copy(chan, 4 + j, (*chips[j], cc), sibling)

        def landed(chan, j):
            return _ag_copy(chan, 1 + j, (*chips[j], cc), me3)

        def relayed(chan, j):
            return _ag_copy(chan, 4 + j, (*chips[j], 1 - cc), me3)

        def from_sibling(chan):
            return _ag_copy(chan, 0, sibling, me3)

        @pl.when(s == 0)
        def _():
            cv = c_ref[...]
            ca_buf[me] = jnp.broadcast_to(cv * _sigmoid(cv), (8, D))
            f_ca = _ag_start(ch_ca, pos)
            wtf_ref[pl.ds(pl.multiple_of(me * IN_SHARD, 16), IN_SHARD), :] = win_ref[...].astype(BF)
            for j in range(3):
                first(ch_win, j).start()
            cw_buf[me] = cw_ref[:, 0, :]
            f_cw = _ag_start(ch_cw, pos)

            _ag_finish(ch_ca, pos, f_ca)
            cact_all = jnp.concatenate([ca_buf[d, 0:1, :] for d in range(NDEV)], axis=0)
            cact_ref[...] = cact_all
            col0 = pl.multiple_of(me * ADA_SHARD, 128)
            mp_buf[me] = jnp.dot(cact_all, wada_ref[...], preferred_element_type=F32,
                                 precision=lax.Precision.HIGHEST) + bada_ref[:, pl.ds(col0, ADA_SHARD)]
            f_mp = _ag_start(ch_mp, pos)
            _ag_finish(ch_cw, pos, f_cw)
            _ag_finish(ch_mp, pos, f_mp)
            for d in range(NDEV):
                cwf_ref[0:CK, CONV_SHARD * d:CONV_SHARD * (d + 1)] = cw_buf[d]
            cwf_ref[CK:CKP, :] = jnp.zeros((CKP - CK, CW), F32)
            mod_ref[...] = jnp.concatenate([mp_buf[d, pl.ds(me, 1), :] for d in range(NDEV)], axis=1)

            for j in (1, 0):
                landed(ch_win, j).wait_recv()
                passed(ch_win, j).start()
            from_sibling(ch_win).wait_recv()
            relayed(ch_win, 1).wait_recv()
            first(ch_win, 1).wait_send()
            first(ch_win, 2).wait_send()
            first(ch_win, 3).start()
            wout = wout_ref[...].astype(BF)
            woutf_ref[pl.ds(pl.multiple_of(me * OUT_SHARD, 16), OUT_SHARD), :] = wout
            for j in range(4):
                first(ch_wout, j).start()

        row0 = pl.multiple_of((s % NT) * TS, TS)

        @pl.when(s < NT)
        def _():
            xv = x_ref[...]
            shift = mod_ref[:, 0:D]
            scale = mod_ref[:, D:2 * D]
            r = lax.rsqrt(jnp.mean(xv * xv, axis=-1, keepdims=True) + EPS)
            hb = ((xv * r * nw_ref[...]) * (1.0 + scale) + shift).astype(BF)
            h_s[pl.ds(row0, TS), :] = hb
            w_half = wtf_ref[pl.ds(pl.multiple_of(x * HALF, 16), HALF), :]
            raw0[pl.ds(row0, TS), :] = lax.dot_general(hb, w_half, (((1,), (1,)), ((), ())),
                                                       preferred_element_type=F32)

        @pl.when(s == NT)
        def _():
            relayed(ch_win, 0).wait_recv()
            landed(ch_win, 2).wait_recv()
            passed(ch_win, 2).start()
            relayed(ch_win, 2).wait_recv()
            pltpu.make_async_copy(wtf_ref, wtf_hbm, out_sems.at[0]).start()

        @pl.when(s >= NT)
        def _():
            hb = h_s[pl.ds(row0, TS), :]
            h_ref[...] = hb
            w_half = wtf_ref[pl.ds(pl.multiple_of((1 - x) * HALF, 16), HALF), :]
            raw1 = lax.dot_general(hb, w_half, (((1,), (1,)), ((), ())), preferred_element_type=F32)
            pt[:, pl.ds(pl.multiple_of(x * HALF, 128), HALF)] = raw0[pl.ds(row0, TS), :]
            pt[:, pl.ds(pl.multiple_of((1 - x) * HALF, 128), HALF)] = raw1
            cos = cos_ref[...]
            sin = sin_ref[...]
            q = pt[:, 0:512]
            qraw_ref[...] = q
            qr_ref[...] = _norm_rope_fwd(q, jnp.tile(qw_ref[...], (1, NQ)), jnp.tile(cos, (1, 4)),
                                         jnp.tile(sin, (1, 4)), bq_ref).astype(BF)
            k = pt[:, 512:640]
            kraw_ref[...] = k
            kr_ref[...] = _norm_rope_fwd(k, jnp.tile(kw_ref[...], (1, NKV)), cos, sin, bk_ref).astype(BF)
            vb_ref[...] = pt[:, 640:768].astype(BF)
            ga_ref[...] = pt[:, 768:1280]
            a = pt[:, 1280:1792]
            g = pt[:, 1792:2304]
            a_ref[...] = a
            g_ref[...] = g
            z_ref[...] = a * _sigmoid(g)
            gb_ref[...] = pt[:, 2304:2816]

        @pl.when(s == 2 * NT - 1)
        def _():
            for j in range(3):
                landed(ch_wout, j).wait_recv()
                passed(ch_wout, j).start()
            from_sibling(ch_wout).wait_recv()
            for j in range(3):
                relayed(ch_wout, j).wait_recv()
            out_copy = pltpu.make_async_copy(woutf_ref, woutf_hbm, out_sems.at[1])
            out_copy.start()
            pltpu.make_async_copy(wtf_ref, wtf_hbm, out_sems.at[0]).wait()
            out_copy.wait()
            first(ch_win, 0).wait_send()
            first(ch_win, 3).wait_send()
            for j in range(3):
                passed(ch_win, j).wait_send()
                passed(ch_wout, j).wait_send()
            for j in range(4):
                first(ch_wout, j).wait_send()

    early = lambda i: (jnp.minimum(i, NT - 1), 0)
    late = lambda i: (jnp.maximum(i - NT, 0), 0)
    t512 = pl.BlockSpec((TS, 512), late)
    t128 = pl.BlockSpec((TS, 128), late)
    sem = pltpu.SemaphoreType.DMA((7,))
    sds = jax.ShapeDtypeStruct
    return pl.pallas_call(
        body, name="gather_fwd", grid=(2 * NT,),
        out_shape=[sds((INW, D), BF), sds((D, D), BF), sds((CKP, CW), F32), sds((1, 3 * D), F32), sds((NDEV, D), F32),
                   sds((S, D), BF), sds((S, AW), F32), sds((S, KVW), F32), sds((S, AW), F32), sds((S, CW), F32),
                   sds((S, CW), F32), sds((S, CW), F32), sds((S, AW), BF), sds((S, KVW), BF), sds((S, KVW), BF),
                   sds((S, CW), F32)],
        in_specs=[_VMEM] * 6 + [pl.BlockSpec((TS, D), early), pl.BlockSpec((1, D), _const),
                                pl.BlockSpec((1, HD), _const), pl.BlockSpec((1, HD), _const), t128, t128,
                                pl.BlockSpec((AW, AW), _const), pl.BlockSpec((KVW, KVW), _const)],
        out_specs=[_ANY, _ANY] + [_VMEM] * 3 + [pl.BlockSpec((TS, D), late), t512, t128, t512, t512, t512, t512, t512,
                                                t128, t128, t512],
        scratch_shapes=[pltpu.VMEM((NDEV, CK, CONV_SHARD), F32), pltpu.VMEM((NDEV, 8, D), F32),
                        pltpu.VMEM((NDEV, 8, ADA_SHARD), F32), pltpu.VMEM((S, D), BF), pltpu.VMEM((S, HALF), F32),
                        pltpu.VMEM((TS, INW), F32), pltpu.VMEM((INW, D), BF), pltpu.VMEM((D, D), BF)]
        + [sem] * 10 + [pltpu.SemaphoreType.DMA((2,))],
        compiler_params=_params(True),
    )(w_in_t, w_out_s, conv_w_s, c, w_ada_s, b_ada, x2, norm_w, qw_t, kw_t, cos_t, sin_t, bq, bk)


QB = 2
NQB = NB // QB


def _band_masks(has_prev):
    kj = lax.broadcasted_iota(jnp.int32, (2 * BLK, 4 * BLK), 0)
    qi = lax.broadcasted_iota(jnp.int32, (2 * BLK, 4 * BLK), 1) & (BLK - 1)
    dist = qi + BLK - kj
    local = (dist >= 0) & (dist < BLK)
    return local & ((kj >= BLK) | has_prev), local


def _key_blocks(sb, prev_ref, cur_ref):
    prev = prev_ref[...] if sb == 0 else cur_ref[BLK * (sb - 1):BLK * sb, :]
    return prev, cur_ref[BLK * sb:BLK * (sb + 1), :]


def _sink_lanes(sink_ref, g):
    lane = lax.broadcasted_iota(jnp.int32, (1, 4 * BLK), 1)
    return jnp.where(lane < BLK, sink_ref[0, 4 * g],
                     jnp.where(lane < 2 * BLK, sink_ref[0, 4 * g + 1],
                               jnp.where(lane < 3 * BLK, sink_ref[0, 4 * g + 2], sink_ref[0, 4 * g + 3])))


def _unstack_t(t):
    return [t[:, BLK * h:BLK * (h + 1)].T for h in range(4)]


def _stack_heads(t, g):
    return jnp.concatenate([t[:, HD * (4 * g + h):HD * (4 * g + h + 1)] for h in range(4)], axis=0)


def _band(prev, cur, g):
    return jnp.concatenate([prev[:, HD * g:HD * (g + 1)], cur[:, HD * g:HD * (g + 1)]], axis=0)


def _softmax_band(qs, kb, mask, sink):
    s = lax.dot_general(kb, qs, (((1,), (1,)), ((), ())), preferred_element_type=F32) * (HD ** -0.5)
    s = jnp.where(mask, s, NEG)
    m = jnp.maximum(jnp.max(s, axis=0, keepdims=True), sink)
    e = jnp.exp(s - m)
    es = jnp.exp(sink - m)
    inv = 1.0 / (jnp.sum(e, axis=0, keepdims=True) + es)
    return e * inv, es * inv


HALO = 32


RC = 64
LC = 128


def _windows(ext_ref, r0, l0, base):
    col = ext_ref[pl.ds(r0, RC + HALO), pl.ds(l0, LC)]
    for s in range(8):
        rolled = col if s == 0 else pltpu.roll(col, RC + HALO - s, 0)
        for t in range(CK):
            if (base + t) % 8 == s:
                a8 = base + t - s
                yield t, rolled[a8:a8 + RC]


def _taps(ext_ref, r0, l0, base, cw_ref, flip):
    acc = None
    for t, win in _windows(ext_ref, r0, l0, base):
        k = CK - 1 - t if flip else t
        term = win * cw_ref[k:k + 1, pl.ds(l0, LC)]
        acc = term if acc is None else acc + term
    return acc


NSUB = (BLK // RC) * (CW // LC)


def _sub_tile(t, row_base):
    r0 = pl.multiple_of(row_base + (t // (CW // LC)) * RC, RC)
    l0 = pl.multiple_of((t % (CW // LC)) * LC, LC)
    return r0, l0


FR = QB * BLK


def _fwd_tail_call(sinks, qr, kr, vb, z, gb, cwf, conv_b, ln_w, ln_b, ga, x2, tgt, mod, w_out_full):
    def body(sink_ref, q_ref, kp_ref, kc_ref, vp_ref, vc_ref, z_ref, zh_ref, gb_ref, cw_ref, cb_ref, lw_ref, lb_ref,
             ga_ref, x_ref, t_ref, mod_ref, w_ref,
             o_ref, zc_ref, dout_ref, dya_ref, dyb_ref, gw_ref, loss_ref, dgate_ref,
             zext, yb_ref):
        i = pl.program_id(0)

        @pl.when(i == 0)
        def _():
            gw_ref[...] = jnp.zeros((D, D), F32)
            loss_ref[...] = jnp.zeros((1, 128), F32)
            dgate_ref[...] = jnp.zeros((1, D), F32)

        def out_proj(rows):
            gav = ga_ref[rows, :]
            ya = o_ref[rows, :] * (gav * _sigmoid(gav))
            ycat = jnp.concatenate([ya.astype(BF), yb_ref[rows, :]], axis=1)
            w = w_ref[...]
            y = jnp.dot(ycat, w, preferred_element_type=F32)
            gate = mod_ref[:, 2 * D:3 * D]
            diff = x_ref[rows, :] + gate * y - t_ref[rows, :]
            sq = jnp.sum(jnp.sum(diff * diff, axis=1, keepdims=True), axis=0, keepdims=True)
            loss_ref[...] += jnp.broadcast_to(sq, (1, 128))
            d_out = diff * (1.0 / D)
            dout_ref[rows, :] = d_out
            dgate_ref[...] += jnp.sum(d_out * y, axis=0, keepdims=True)
            dy = (d_out * gate).astype(BF)
            d_ycat = lax.dot_general(dy, w, (((1,), (1,)), ((), ())), preferred_element_type=F32)
            dya_ref[rows, :] = d_ycat[:, 0:AW]
            dyb_ref[rows, :] = d_ycat[:, AW:D]
            gw_ref[...] += lax.dot_general(ycat, dy, (((0,), (0,)), ((), ())), preferred_element_type=F32)

        zext[0:HALO, :] = jnp.where(i > 0, zh_ref[...], 0.0)
        zext[HALO:HALO + FR, :] = z_ref[...]
        masks = _band_masks(i > 0)
        for sb in range(QB):
            rows = slice(BLK * sb, BLK * (sb + 1))
            mask = masks[min(sb, 1)]
            q = q_ref[rows, :]
            kp, kc = _key_blocks(sb, kp_ref, kc_ref)
            vp, vc = _key_blocks(sb, vp_ref, vc_ref)
            for g in range(NKV):
                p, _ = _softmax_band(_stack_heads(q, g), _band(kp, kc, g), mask, _sink_lanes(sink_ref, g))
                o_t = lax.dot_general(_band(vp, vc, g), p.astype(BF), (((0,), (0,)), ((), ())),
                                      preferred_element_type=F32)
                for h, o_h in enumerate(_unstack_t(o_t)):
                    o_ref[rows, HD * (4 * g + h):HD * (4 * g + h + 1)] = o_h
            for r0 in range(BLK * sb, BLK * (sb + 1), RC):
                for l0 in range(0, CW, LC):
                    acc = _taps(zext, r0, l0, HALO - (CK - 1), cw_ref, False)
                    zc_ref[r0:r0 + RC, l0:l0 + LC] = acc + cb_ref[:, l0:l0 + LC]
            zc = zc_ref[rows, :]
            mu = jnp.mean(zc, axis=-1, keepdims=True)
            dz = zc - mu
            rstd = lax.rsqrt(jnp.mean(dz * dz, axis=-1, keepdims=True) + EPS)
            zn = dz * rstd * lw_ref[...] + lb_ref[...]
            gbv = gb_ref[rows, :]
            yb_ref[rows, :] = ((zn * _sigmoid(zn)) * (gbv * _sigmoid(gbv))).astype(BF)
            if (sb + 1) * BLK % TS == 0:
                out_proj(slice((sb + 1) * BLK - TS, (sb + 1) * BLK))

    prev = lambda i: (jnp.maximum(QB * i - 1, 0), 0)
    halo = lambda i: (jnp.maximum(FR // HALO * i - 1, 0), 0)
    f512 = pl.BlockSpec((FR, 512), _row)
    f128 = pl.BlockSpec((FR, KVW), _row)
    f1024 = pl.BlockSpec((FR, D), _row)
    c512 = pl.BlockSpec((1, CW), _const)
    sds = jax.ShapeDtypeStruct
    return pl.pallas_call(
        body, name="fwd_tail", grid=(NQB,),
        out_shape=[sds((S, AW), F32), sds((S, CW), F32), sds((S, D), F32), sds((S, AW), F32), sds((S, CW), F32),
                   sds((D, D), F32), sds((1, 128), F32), sds((1, D), F32)],
        in_specs=[_SMEM, f512, pl.BlockSpec((BLK, KVW), prev), f128, pl.BlockSpec((BLK, KVW), prev), f128,
                  f512, pl.BlockSpec((HALO, CW), halo), f512, pl.BlockSpec((CKP, CW), _const), c512, c512, c512,
                  f512, f1024, f1024, pl.BlockSpec((1, 3 * D), _const),
                  pl.BlockSpec((D, D), _const, pipeline_mode=pl.Buffered(1))],
        out_specs=[f512, f512, f1024, f512, f512, _VMEM, _VMEM, _VMEM],
        scratch_shapes=[pltpu.VMEM((FR + HALO, CW), F32), pltpu.VMEM((FR, CW), BF)],
        compiler_params=_params(True),
    )(sinks, qr, kr, kr, vb, vb, z, z, gb, cwf, conv_b, ln_w, ln_b, ga, x2, tgt, mod, w_out_full)


def _ln_gate_bwd(d_yb, zc, gbv, lw, lb):
    mu = jnp.mean(zc, axis=-1, keepdims=True)
    dz = zc - mu
    rstd = lax.rsqrt(jnp.mean(dz * dz, axis=-1, keepdims=True) + EPS)
    zh = dz * rstd
    zn = zh * lw + lb
    act_n, dact_n = _silu_and_grad(zn)
    act_g, dact_g = _silu_and_grad(gbv)
    d_gb = d_yb * act_n * dact_g
    d_zn = d_yb * act_g * dact_n
    dzh = d_zn * lw
    d_zc = rstd * (dzh - jnp.mean(dzh, axis=-1, keepdims=True) - zh * jnp.mean(dzh * zh, axis=-1, keepdims=True))
    return d_zc, d_gb, d_zn, zh


def _attn_conv_bwd_call(sinks, qr, kr, vb, d_ya, ga, o, qraw, qw_t, cos_t, sin_t, bq, d_yb, zc, gb, z, a, g, cwf,
                        ln_w, ln_b):
    def body(sink_ref, q_ref, kp_ref, kc_ref, vp_ref, vc_ref, dya_ref, ga_ref, o_ref, qraw_ref, qw_ref,
             cos_ref, sin_ref, bq_ref,
             dyb_ref, dybn_ref, zc_ref, zcn_ref, gb_ref, gbn_ref, z_ref, zh_ref, a_ref, g_ref, cw_ref, lw_ref, lb_ref,
             dqraw_ref, dga_ref, dk_ref, dv_ref, gqw_ref, gsink_ref,
             dgb_ref, da_ref, dg_ref, gcw_ref, glw_ref, glb_ref, gcb_ref,
             dext, zext, gacc):
        i = pl.program_id(0)

        @pl.when(i == 0)
        def _():
            dk_ref[...] = jnp.zeros((S, KVW), F32)
            dv_ref[...] = jnp.zeros((S, KVW), F32)
            gqw_ref[...] = jnp.zeros((1, AW), F32)
            gsink_ref[...] = jnp.zeros((1, 128), F32)
            gacc[...] = jnp.zeros((CKP * 8, CW), F32)
            glw_ref[...] = jnp.zeros((1, CW), F32)
            glb_ref[...] = jnp.zeros((1, CW), F32)
            gcb_ref[...] = jnp.zeros((1, CW), F32)

        lw = lw_ref[...]
        lb = lb_ref[...]

        def ln_rows(rows):
            d_zc, d_gb, d_zn, zh = _ln_gate_bwd(dyb_ref[rows, :], zc_ref[rows, :], gb_ref[rows, :], lw, lb)
            dgb_ref[rows, :] = d_gb.astype(BF)
            glw_ref[...] += jnp.sum(d_zn * zh, axis=0, keepdims=True)
            glb_ref[...] += jnp.sum(d_zn, axis=0, keepdims=True)
            gcb_ref[...] += jnp.sum(d_zc, axis=0, keepdims=True)
            dext[rows, :] = d_zc

        ln_rows(slice(0, BLK))
        zext[0:HALO, :] = jnp.where(i > 0, zh_ref[...], 0.0)
        zext[HALO:HALO + FR, :] = z_ref[...]

        lane = lax.broadcasted_iota(jnp.int32, (1, 128), 1)
        gsink = jnp.zeros((1, 128), F32)
        dq_rows = []
        masks = _band_masks(i > 0)
        for sb in range(QB):
            rows = slice(BLK * sb, BLK * (sb + 1))
            if sb + 1 < QB:
                ln_rows(slice(BLK * (sb + 1), BLK * (sb + 2)))
            else:
                d_zc_next, _, _, _ = _ln_gate_bwd(dybn_ref[...], zcn_ref[...], gbn_ref[...], lw, lb)
                dext[FR:FR + HALO, :] = jnp.where(i < NQB - 1, d_zc_next, 0.0)
            mask = masks[min(sb, 1)]
            q = q_ref[rows, :]
            d_ya = dya_ref[rows, :]
            act, dact = _silu_and_grad(ga_ref[rows, :])
            dga_ref[rows, :] = (d_ya * o_ref[rows, :] * dact).astype(BF)
            d_o = (d_ya * act).astype(BF)
            kp, kc = _key_blocks(sb, kp_ref, kc_ref)
            vp, vc = _key_blocks(sb, vp_ref, vc_ref)
            dq_parts, dk_parts, dv_parts = [], [], []
            for gi in range(NKV):
                qs = _stack_heads(q, gi)
                kb = _band(kp, kc, gi)
                vbd = _band(vp, vc, gi)
                p, ps = _softmax_band(qs, kb, mask, _sink_lanes(sink_ref, gi))
                dos = _stack_heads(d_o, gi)
                dp = lax.dot_general(vbd, dos, (((1,), (1,)), ((), ())), preferred_element_type=F32)
                dr = jnp.sum(p * dp, axis=0, keepdims=True)
                ds = (p * (dp - dr) * (HD ** -0.5)).astype(BF)
                sink_term = ps * dr
                for h in range(4):
                    part = jnp.sum(sink_term[:, BLK * h:BLK * (h + 1)], axis=1, keepdims=True)
                    gsink = gsink - jnp.where(lane == 4 * gi + h, part, 0.0)
                dv_parts.append(jnp.dot(p.astype(BF), dos, preferred_element_type=F32))
                dk_parts.append(jnp.dot(ds, qs, preferred_element_type=F32))
                dq_t = lax.dot_general(kb, ds, (((0,), (0,)), ((), ())), preferred_element_type=F32)
                dq_parts.extend(_unstack_t(dq_t))
            dkb = jnp.concatenate(dk_parts, axis=1)
            dvb = jnp.concatenate(dv_parts, axis=1)
            blk = QB * i + sb
            r_prev = pl.multiple_of(jnp.maximum(blk - 1, 0) * BLK, BLK)
            r_cur = pl.multiple_of(blk * BLK, BLK)
            dk_ref[pl.ds(r_prev, BLK), :] += dkb[0:BLK]
            dv_ref[pl.ds(r_prev, BLK), :] += dvb[0:BLK]
            dk_ref[pl.ds(r_cur, BLK), :] += dkb[BLK:2 * BLK]
            dv_ref[pl.ds(r_cur, BLK), :] += dvb[BLK:2 * BLK]
            dq_rows.append(jnp.concatenate(dq_parts, axis=1))

            def taps_sub(t, carry, sb=sb):
                r0, l0 = _sub_tile(t, BLK * sb)
                here = (pl.ds(r0, RC), pl.ds(l0, LC))
                d_z = _taps(dext, r0, l0, 0, cw_ref, True)
                sg = _sigmoid(g_ref[here])
                da_ref[here] = (d_z * sg).astype(BF)
                dg_ref[here] = (d_z * a_ref[here] * sg * (1.0 - sg)).astype(BF)
                d_sub = dext[here]
                for k, win in _windows(zext, r0, l0, HALO - (CK - 1)):
                    prod = d_sub * win
                    part = prod[0:8]
                    for q8 in range(1, RC // 8):
                        part = part + prod[8 * q8:8 * q8 + 8]
                    gacc[8 * k:8 * k + 8, pl.ds(l0, LC)] += part
                return carry

            lax.fori_loop(0, NSUB, taps_sub, 0)
        gsink_ref[...] += gsink
        dq = jnp.concatenate(dq_rows, axis=0)
        dq_raw, g_qw = _norm_rope_bwd(dq, qraw_ref[...], jnp.tile(qw_ref[...], (1, NQ)),
                                      jnp.tile(cos_ref[...], (1, 4)), jnp.tile(sin_ref[...], (1, 4)), bq_ref)
        dqraw_ref[...] = dq_raw.astype(BF)
        gqw_ref[...] += g_qw

        @pl.when(i == NQB - 1)
        def _():
            for k in range(CK):
                gcw_ref[k:k + 1, :] = jnp.sum(gacc[8 * k:8 * k + 8, :], axis=0, keepdims=True)
            gcw_ref[CK:CKP, :] = jnp.zeros((CKP - CK, CW), F32)

    prev = lambda i: (jnp.maximum(QB * i - 1, 0), 0)
    halo_prev = lambda i: (jnp.maximum(FR // HALO * i - 1, 0), 0)
    halo_next = lambda i: (jnp.minimum(FR // HALO * (i + 1), S // HALO - 1), 0)
    f512 = pl.BlockSpec((FR, 512), _row)
    f128 = pl.BlockSpec((FR, 128), _row)
    hn = pl.BlockSpec((HALO, CW), halo_next)
    c512 = pl.BlockSpec((1, CW), _const)
    sds = jax.ShapeDtypeStruct
    vec = sds((1, CW), F32)
    return pl.pallas_call(
        body, name="attn_conv_bwd", grid=(NQB,),
        out_shape=[sds((S, AW), BF), sds((S, AW), BF), sds((S, KVW), F32), sds((S, KVW), F32), sds((1, AW), F32),
                   sds((1, 128), F32),
                   sds((S, CW), BF), sds((S, CW), BF), sds((S, CW), BF), sds((CKP, CW), F32), vec, vec, vec],
        in_specs=[_SMEM, f512, pl.BlockSpec((BLK, KVW), prev), f128, pl.BlockSpec((BLK, KVW), prev), f128,
                  f512, f512, f512, f512, pl.BlockSpec((1, HD), _const), f128, f128, pl.BlockSpec((AW, AW), _const),
                  f512, hn, f512, hn, f512, hn, f512, pl.BlockSpec((HALO, CW), halo_prev), f512, f512,
                  pl.BlockSpec((CKP, CW), _const), c512, c512],
        out_specs=[f512, f512, _VMEM, _VMEM, _VMEM, _VMEM, f512, f512, f512, _VMEM, _VMEM, _VMEM, _VMEM],
        scratch_shapes=[pltpu.VMEM((FR + HALO, CW), F32), pltpu.VMEM((FR + HALO, CW), F32),
                        pltpu.VMEM((CKP * 8, CW), F32)],
        compiler_params=_params(True),
    )(sinks, qr, kr, kr, vb, vb, d_ya, ga, o, qraw, qw_t, cos_t, sin_t, bq,
      d_yb, d_yb, zc, zc, gb, gb, z, z, a, g, cwf, ln_w, ln_b)


SM_ROWS = 8
PIECES = ((0, 512), (512, 640), (640, 768), (768, 1280), (1280, 1792), (1792, 2304), (2304, 2816))


def _bwd_in_call(dqraw, dk, dv, dga, da, dg, dgb, kraw, kw_t, cos_t, sin_t, bk, h, wt_full, x2, d_out, mod, norm_w,
                 gw_out, gcw, glw, glb, gcb, gqw, gsink, dgate, loss_p, cact_all):
    def body(dq_ref, dk_ref, dv_ref, dga_ref, da_ref, dg_ref, dgb_ref, kraw_ref, kw_ref, cos_ref, sin_ref, bk_ref,
             h_ref, wt_ref, x_ref, dout_ref, mod_ref, nw_ref, gwout_ref, gcw_ref, glw_ref, glb_ref, gcb_ref, gqw_ref,
             gsink_ref, dgate_ref, loss_ref, cact_ref,
             gx_ref, o_gwin, o_gwout, o_gwada, o_gbada, o_gnw, o_gqw, o_gkw, o_gsink, o_gcw, o_gcb, o_glw, o_glb,
             o_loss,
             acc, win_send, win_sib, win_ici, wout_send, wout_sib, wout_ici, sm_buf, cw_buf, dmod_all, vec_acc, gkw_acc,
             wi_ds, wi_dr, wi_is, wi_ir, wo_ds, wo_dr, wo_is, wo_ir, sm_s, sm_r, cw_s, cw_r):
        i = pl.program_id(0)
        pos = _mesh_pos()
        x, y, cc = pos
        me = 4 * x + 2 * y + cc

        def chip(j):
            return (1 - x if j & 1 else x, 1 - y if j & 2 else y)

        def rows_of(buf, px, py, pc, rows, align):
            return buf.at[pl.ds(pl.multiple_of((4 * px + 2 * py + pc) * rows, align), rows), :]

        bufs = {"in": (win_send, win_sib, win_ici, IN_SHARD, wi_ds, wi_dr, wi_is, wi_ir),
                "out": (wout_send, wout_sib, wout_ici, OUT_SHARD, wo_ds, wo_dr, wo_is, wo_ir)}

        def d2d_copy(j, which):
            send, sib, _, rows, ds_, dr_, _, _ = bufs[which]
            px, py = chip(j)
            return pltpu.make_async_remote_copy(src_ref=rows_of(send, px, py, 1 - cc, rows, 16), dst_ref=sib.at[j],
                                                send_sem=ds_.at[j], recv_sem=dr_.at[j], device_id=(x, y, 1 - cc),
                                                device_id_type=MESH)

        def ici_copy(j, which):
            send, _, ici, rows, _, _, is_, ir_ = bufs[which]
            px, py = chip(j)
            return pltpu.make_async_remote_copy(src_ref=rows_of(send, px, py, cc, rows, 16), dst_ref=ici.at[j - 1],
                                                send_sem=is_.at[j - 1], recv_sem=ir_.at[j - 1], device_id=(px, py, cc),
                                                device_id_type=MESH)

        def level2(which, partial_ref):
            send, sib, _, rows, _, _, _, _ = bufs[which]
            for j in range(1, 4):
                d2d_copy(j, which).wait_recv()
                px, py = chip(j)
                mine = rows_of(partial_ref, px, py, cc, rows, 8)[...]
                rows_of(send, px, py, cc, rows, 16)[...] = (mine + sib[j].astype(F32)).astype(BF)
                ici_copy(j, which).start()

        def finish(which, partial_ref):
            _, sib, ici, rows, _, _, _, _ = bufs[which]
            d2d_copy(0, which).wait_recv()
            total = rows_of(partial_ref, x, y, cc, rows, 8)[...] + sib[0].astype(F32)
            for j in range(1, 4):
                ici_copy(j, which).wait_recv()
                total = total + ici[j - 1].astype(F32)
            for j in range(4):
                d2d_copy(j, which).wait_send()
            for j in range(1, 4):
                ici_copy(j, which).wait_send()
            return total

        def dproj_pieces():
            dk_raw, g_kw = _norm_rope_bwd(dk_ref[...], kraw_ref[...], jnp.tile(kw_ref[...], (1, NKV)), cos_ref[...],
                                          sin_ref[...], bk_ref)
            return [dq_ref[...], dk_raw.astype(BF), dv_ref[...].astype(BF), dga_ref[...], da_ref[...], dg_ref[...],
                    dgb_ref[...]], g_kw

        @pl.when(i == 0)
        def _():
            acc[...] = jnp.zeros((INW, D), F32)
            vec_acc[...] = jnp.zeros((8, D), F32)
            gkw_acc[...] = jnp.zeros((1, KVW), F32)
            wout_send[...] = gwout_ref[...].astype(BF)
            for j in range(4):
                d2d_copy(j, "out").start()

        @pl.when(i == 2)
        def _():
            level2("out", gwout_ref)

        @pl.when(i < NT)
        def _():
            pieces, g_kw = dproj_pieces()
            gkw_acc[...] += g_kw
            hv = h_ref[...]
            for (lo, hi), piece in zip(PIECES, pieces):
                acc[lo:hi, :] += lax.dot_general(piece, hv, (((0,), (0,)), ((), ())), preferred_element_type=F32)

        @pl.when(i == NT - 1)
        def _():
            for lo, hi in PIECES:
                win_send[lo:hi, :] = acc[lo:hi, :].astype(BF)
            for j in range(4):
                d2d_copy(j, "in").start()

        @pl.when(i == NT + 2)
        def _():
            level2("in", acc)

        @pl.when(i >= NT)
        def _():
            pieces, _ = dproj_pieces()
            dproj = jnp.concatenate(pieces, axis=1)
            d_h = jnp.dot(dproj, wt_ref[...], preferred_element_type=F32)
            xv = x_ref[...]
            scale = mod_ref[:, D:2 * D]
            nw = nw_ref[...]
            r = lax.rsqrt(jnp.mean(xv * xv, axis=-1, keepdims=True) + EPS)
            xn = xv * r
            vec_acc[0:1, :] += jnp.sum(d_h, axis=0, keepdims=True)
            vec_acc[1:2, :] += jnp.sum(d_h * (xn * nw), axis=0, keepdims=True)
            d_u = d_h * (1.0 + scale)
            vec_acc[2:3, :] += jnp.sum(d_u * xn, axis=0, keepdims=True)
            d_xn = d_u * nw
            gx_ref[...] = dout_ref[...] + r * (d_xn - xn * jnp.mean(d_xn * xn, axis=-1, keepdims=True))

        @pl.when(i == 2 * NT - 1)
        def _():
            ch_sm = (_slab(sm_buf), sm_s, sm_r)
            ch_cw = (_slab(cw_buf), cw_s, cw_r)
            z128 = jnp.zeros((1, 128), F32)
            row4 = jnp.concatenate([glw_ref[...], glb_ref[...]], axis=1)
            row5 = jnp.concatenate([gcb_ref[...], gqw_ref[...]], axis=1)
            row6 = jnp.concatenate([gkw_acc[...], gsink_ref[...], loss_ref[...]] + [z128] * 5, axis=1)
            sm_buf[me] = jnp.concatenate([vec_acc[0:2, :], dgate_ref[...], vec_acc[2:3, :], row4, row5, row6,
                                          jnp.zeros((1, D), F32)], axis=0)
            f_sm = _ag_start(ch_sm, pos)
            cw_buf[me] = gcw_ref[...]
            f_cw = _ag_start(ch_cw, pos)
            _ag_finish(ch_sm, pos, f_sm)
            _ag_finish(ch_cw, pos, f_cw)
            tot = sm_buf[0]
            cw_tot = cw_buf[0]
            for d in range(1, NDEV):
                tot = tot + sm_buf[d]
                cw_tot = cw_tot + cw_buf[d]
            o_gbada[...] = jnp.concatenate([tot[0:1, :], tot[1:2, :], tot[2:3, :]], axis=1)
            o_gnw[...] = tot[3:4, :]
            o_glw[...] = tot[4:5, 0:CW]
            o_glb[...] = tot[4:5, CW:D]
            o_gcb[...] = tot[5:6, 0:CW]
            gq = tot[5:6, CW:CW + HD]
            for hh in range(1, NQ):
                gq = gq + tot[5:6, CW + HD * hh:CW + HD * (hh + 1)]
            o_gqw[...] = gq
            o_gkw[...] = tot[6:7, 0:HD] + tot[6:7, HD:2 * HD]
            o_gsink[...] = tot[6:7, 128:128 + NQ]
            o_loss[...] = tot[6:7, 256:384] * (0.5 / D)
            mine = jnp.zeros((CK, CONV_SHARD), F32)
            for d in range(NDEV):
                mine = mine + jnp.where(me == d, cw_tot[0:CK, CONV_SHARD * d:CONV_SHARD * (d + 1)], 0.0)
            for k in range(CK):
                o_gcw[k] = mine[k:k + 1, :]
            for d in range(NDEV):
                dmod_all[d:d + 1, :] = jnp.concatenate([sm_buf[d, 0:1, :], sm_buf[d, 1:2, :], sm_buf[d, 2:3, :]],
                                                       axis=1)
            col0 = pl.multiple_of(me * ADA_SHARD, 128)
            o_gwada[...] = lax.dot_general(cact_ref[...], dmod_all[:, pl.ds(col0, ADA_SHARD)], (((0,), (0,)), ((), ())),
                                           preferred_element_type=F32, precision=lax.Precision.HIGHEST)

            o_gwout[...] = finish("out", gwout_ref)
            o_gwin[...] = finish("in", acc)

    half = lambda i: (i % NT, 0)
    late = lambda i: (jnp.maximum(i - NT, 0), 0)
    t512 = pl.BlockSpec((TS, 512), half)
    t128 = pl.BlockSpec((TS, 128), half)
    l1024 = pl.BlockSpec((TS, D), late)
    sem7 = pltpu.SemaphoreType.DMA((7,))
    sem4 = pltpu.SemaphoreType.DMA((4,))
    sem3 = pltpu.SemaphoreType.DMA((3,))
    sds = jax.ShapeDtypeStruct
    return pl.pallas_call(
        body, name="bwd_in", grid=(2 * NT,),
        out_shape=[sds((S, D), F32), sds((IN_SHARD, D), F32), sds((OUT_SHARD, D), F32), sds((D, ADA_SHARD), F32),
                   sds((1, 3 * D), F32), sds((1, D), F32), sds((1, HD), F32), sds((1, HD), F32), sds((1, NQ), F32),
                   sds((CK, 1, CONV_SHARD), F32), sds((1, CW), F32), sds((1, CW), F32), sds((1, CW), F32),
                   sds((1, 128), F32)],
        in_specs=[t512, t128, t128, t512, t512, t512, t512, t128, pl.BlockSpec((1, HD), _const), t128, t128,
                  pl.BlockSpec((KVW, KVW), _const), pl.BlockSpec((TS, D), half),
                  pl.BlockSpec((INW, D), _const, pipeline_mode=pl.Buffered(1)), l1024, l1024,
                  pl.BlockSpec((1, 3 * D), _const), pl.BlockSpec((1, D), _const)] + [_VMEM] * 10,
        out_specs=[l1024] + [_VMEM] * 13,
        scratch_shapes=[pltpu.VMEM((INW, D), F32), pltpu.VMEM((INW, D), BF), pltpu.VMEM((4, IN_SHARD, D), BF),
                        pltpu.VMEM((3, IN_SHARD, D), BF), pltpu.VMEM((D, D), BF), pltpu.VMEM((4, OUT_SHARD, D), BF),
                        pltpu.VMEM((3, OUT_SHARD, D), BF),
                        pltpu.VMEM((NDEV, SM_ROWS, D), F32), pltpu.VMEM((NDEV, CKP, CW), F32),
                        pltpu.VMEM((NDEV, 3 * D), F32), pltpu.VMEM((8, D), F32), pltpu.VMEM((1, KVW), F32)]
        + [sem4, sem4, sem3, sem3] * 2 + [sem7] * 4,
        compiler_params=pltpu.CompilerParams(dimension_semantics=("arbitrary",), vmem_limit_bytes=BIG_VMEM_LIMIT),
    )(dqraw, dk, dv, dga, da, dg, dgb, kraw, kw_t, cos_t, sin_t, bk, h, wt_full, x2, d_out, mod, norm_w,
      gw_out, gcw, glw, glb, gcb, gqw, gsink, dgate, loss_p, cact_all)


ADAM_STEPS = 4


def _adam_chunking(arr):
    if arr.ndim == 2 and arr.shape[0] % (8 * ADAM_STEPS) == 0:
        return "rows"
    if arr.ndim == 2 and arr.shape[1] % (128 * ADAM_STEPS) == 0:
        return "cols"
    return None


def _adam_call(ws, gs, ms, vs, grad_x, loss_v):
    n = len(ws)
    bc1 = 1.0 - ADAM_B1 ** ADAM_STEP
    bc2 = 1.0 - ADAM_B2 ** ADAM_STEP
    chunked = [_adam_chunking(w) for w in ws]

    def body(*refs):
        ins, outs = refs[:4 * n + 2], refs[4 * n + 2:]
        i = pl.program_id(0)

        def update(j):
            w, g, m, v = (ins[j][...], ins[n + j][...], ins[2 * n + j][...], ins[3 * n + j][...])
            m_new = ADAM_B1 * m + (1.0 - ADAM_B1) * g
            v_new = ADAM_B2 * v + (1.0 - ADAM_B2) * (g * g)
            m_hat = m_new / bc1
            v_hat = v_new / bc2
            outs[j][...] = g
            outs[n + j][...] = -ADAM_LR * (m_hat / (jnp.sqrt(v_hat) + ADAM_EPS) + ADAM_WD * w)
            outs[2 * n + j][...] = m_new
            outs[3 * n + j][...] = v_new

        for j in range(n):
            if chunked[j]:
                update(j)
        outs[4 * n][...] = ins[4 * n][...]

        @pl.when(i == 0)
        def _():
            for j in range(n):
                if not chunked[j]:
                    update(j)
            outs[4 * n + 1][...] = ins[4 * n + 1][...]

    def spec(arr, how):
        if how == "rows":
            return pl.BlockSpec((arr.shape[0] // ADAM_STEPS, arr.shape[1]), _row)
        if how == "cols":
            return pl.BlockSpec((arr.shape[0], arr.shape[1] // ADAM_STEPS), lambda i: (0, i))
        zeros = (0,) * arr.ndim
        return pl.BlockSpec(arr.shape, lambda i: zeros)

    par_specs = [spec(w, ch) for w, ch in zip(ws, chunked)]
    extra = [spec(grad_x, "rows"), spec(loss_v, None)]
    shapes = [jax.ShapeDtypeStruct(w.shape, F32) for w in ws]
    res = pl.pallas_call(
        body, name="adam", grid=(ADAM_STEPS,),
        out_shape=shapes * 4 + [jax.ShapeDtypeStruct(grad_x.shape, F32), jax.ShapeDtypeStruct(loss_v.shape, F32)],
        in_specs=par_specs * 4 + extra, out_specs=par_specs * 4 + extra,
        compiler_params=_params(True),
    )(*ws, *gs, *ms, *vs, grad_x, loss_v)
    return res[0:n], res[n:2 * n], res[2 * n:3 * n], res[3 * n:4 * n], res[4 * n], res[4 * n + 1]


def _rope_tables():
    inv = (np.float32(ROPE_THETA) ** (-np.arange(0, HD, 2, dtype=np.float32) / np.float32(HD))).astype(np.float32)
    ang = (np.arange(S, dtype=np.float32)[:, None] * inv[None, :]).astype(np.float32)
    cos, sin = np.cos(ang).astype(np.float32), np.sin(ang).astype(np.float32)
    cos64 = np.concatenate([cos, cos], axis=-1)
    sin64 = np.concatenate([-sin, sin], axis=-1)
    return jnp.asarray(np.tile(cos64, (1, 2))), jnp.asarray(np.tile(sin64, (1, 2)))


def _group_matrix(width):
    idx = np.arange(width) // HD
    return jnp.asarray(np.where(idx[:, None] == idx[None, :], 1.0 / HD, 0.0).astype(np.float32)).astype(BF)


def kernel(x, c, w_ada, b_ada, norm_w, w_in, q_norm_w, k_norm_w, sinks, conv_w, conv_b, ln_w, ln_b, w_out, loss_target, m_w_ada, m_b_ada, m_norm_w, m_w_in, m_q_norm_w, m_k_norm_w, m_sinks, m_conv_w, m_conv_b, m_ln_w, m_ln_b, m_w_out, v_w_ada, v_b_ada, v_norm_w, v_w_in, v_q_norm_w, v_k_norm_w, v_sinks, v_conv_w, v_conv_b, v_ln_w, v_ln_b, v_w_out):
    x2 = x[0]
    tgt = loss_target[0]
    cos_t, sin_t = _rope_tables()
    bq = _group_matrix(AW)
    bk = _group_matrix(KVW)
    qw_t, kw_t = q_norm_w, k_norm_w

    tr = lambda t: jnp.swapaxes(t[0], 0, 1)
    tc = lambda t: jnp.swapaxes(t, 0, 1)
    (wt_full, w_out_full, cwf, mod, cact_all, h, qraw, kraw, ga, a, g, gb, qr, kr, vb, z) = _gather_fwd_call(
        tr(w_in), w_out[0], tc(conv_w), c, w_ada[0], b_ada, x2, norm_w, qw_t, kw_t, cos_t, sin_t, bq, bk)
    o, zc, d_out, d_ya, d_yb, gw_out, loss_p, dgate = _fwd_tail_call(
        sinks, qr, kr, vb, z, gb, cwf, conv_b, ln_w, ln_b, ga, x2, tgt, mod, w_out_full)

    dqraw, dga, dk, dv, gqw, gsink, dgb, da, dg, gcw, glw, glb, gcb = _attn_conv_bwd_call(
        sinks, qr, kr, vb, d_ya, ga, o, qraw, qw_t, cos_t, sin_t, bq, d_yb, zc, gb, z, a, g, cwf, ln_w, ln_b)
    (grad_x, g_w_in_t, g_w_out, g_w_ada, g_b_ada, g_norm_w, g_qw, g_kw, g_sinks, g_conv_w, g_conv_b, g_ln_w, g_ln_b,
     loss_v) = _bwd_in_call(dqraw, dk, dv, dga, da, dg, dgb, kraw, kw_t, cos_t, sin_t, bk, h, wt_full, x2, d_out, mod,
                            norm_w, gw_out, gcw, glw, glb, gcb, gqw, gsink, dgate, loss_p, cact_all)

    ws = [w_ada[0], b_ada, norm_w, tr(w_in), q_norm_w, k_norm_w, sinks, tc(conv_w), conv_b, ln_w, ln_b, w_out[0]]
    gs = [g_w_ada, g_b_ada, g_norm_w, g_w_in_t, g_qw, g_kw, g_sinks, g_conv_w, g_conv_b, g_ln_w, g_ln_b, g_w_out]
    ms = [m_w_ada[0], m_b_ada, m_norm_w, tr(m_w_in), m_q_norm_w, m_k_norm_w, m_sinks, tc(m_conv_w), m_conv_b, m_ln_w,
          m_ln_b, m_w_out[0]]
    vs = [v_w_ada[0], v_b_ada, v_norm_w, tr(v_w_in), v_q_norm_w, v_k_norm_w, v_sinks, tc(v_conv_w), v_conv_b, v_ln_w,
          v_ln_b, v_w_out[0]]
    grads, deltas, new_m, new_v, grad_x, loss_v = _adam_call(ws, gs, ms, vs, grad_x, loss_v)
    shaped = [w_ada, b_ada, norm_w, w_in, q_norm_w, k_norm_w, sinks, conv_w, conv_b, ln_w, ln_b, w_out]
    W_IN_POS, CONV_W_POS = 3, 7

    def like(vals):
        vals = [jnp.swapaxes(v, 0, 1) if j in (W_IN_POS, CONV_W_POS) else v for j, v in enumerate(vals)]
        return [v.reshape(s.shape) for v, s in zip(vals, shaped)]

    return (loss_v[0, 0], grad_x[None], *like(grads), *like(deltas), *like(new_m), *like(new_v))
```

```python
import functools

import jax
import jax.numpy as jnp
import numpy as np
from jax import lax
from jax.experimental import pallas as pl
from jax.experimental.pallas import tpu as pltpu

S = 2048
D = 1024
NDEV = 8
HD = 64
NQ = 8
NKV = 2
AW = 512
KVW = 128
CW = 512
INW = 2816
IN_SHARD = INW // NDEV
ADA_SHARD = 3 * D // NDEV
OUT_SHARD = D // NDEV
CONV_SHARD = CW // NDEV
CK = 31
CKP = 32
BLK = 128
TS = 256
NT = S // TS
NB = S // BLK
EPS = 1e-6
ROPE_THETA = 10000.0
NEG = -1e30
BF = jnp.bfloat16
F32 = jnp.float32

ADAM_LR = 0.001
ADAM_B1 = 0.9
ADAM_B2 = 0.999
ADAM_EPS = 1e-08
ADAM_WD = 0.01
ADAM_STEP = 10

VMEM_LIMIT = 56 * 1024 * 1024
BIG_VMEM_LIMIT = 62 * 1024 * 1024
MESH = pl.DeviceIdType.MESH

_VMEM = pl.BlockSpec(memory_space=pltpu.VMEM)
_SMEM = pl.BlockSpec(memory_space=pltpu.SMEM)
_ANY = pl.BlockSpec(memory_space=pl.ANY)


def _params(grid=False):
    if grid:
        return pltpu.CompilerParams(dimension_semantics=("arbitrary",), vmem_limit_bytes=VMEM_LIMIT)
    return pltpu.CompilerParams(vmem_limit_bytes=VMEM_LIMIT)


def _row(i):
    return (i, 0)


def _const(i):
    return (0, 0)


def _sigmoid(t):
    return 1.0 / (1.0 + jnp.exp(-t))


def _silu_and_grad(t):
    sg = _sigmoid(t)
    return t * sg, sg * (1.0 + t * (1.0 - sg))


def _group_mean(t, b_ref):
    hi = t.astype(BF)
    lo = (t - hi.astype(F32)).astype(BF)
    b = b_ref[...]
    return jnp.dot(hi, b, preferred_element_type=F32) + jnp.dot(lo, b, preferred_element_type=F32)


def _partner(t):
    w = t.shape[-1]
    lane = lax.broadcasted_iota(jnp.int32, t.shape, 1)
    first = (lane & 32) == 0
    return jnp.where(first, pltpu.roll(t, w - 32, 1), pltpu.roll(t, 32, 1))


def _norm_rope_fwd(t, w_t, cos, sin, b_ref):
    r = lax.rsqrt(_group_mean(t * t, b_ref) + EPS)
    tn = t * r * w_t
    return tn * cos + _partner(tn) * sin


def _norm_rope_bwd(d_out, t, w_t, cos, sin, b_ref):
    d_tn = d_out * cos + _partner(d_out * sin)
    r = lax.rsqrt(_group_mean(t * t, b_ref) + EPS)
    th = t * r
    g_w = jnp.sum(d_tn * th, axis=0, keepdims=True)
    d_th = d_tn * w_t
    d_t = r * (d_th - th * _group_mean(d_th * th, b_ref))
    return d_t, g_w


def _mesh_pos():
    return lax.axis_index("x"), lax.axis_index("y"), lax.axis_index("c")


def _ag_copy(chan, k, block, to):
    blk, send_sems, recv_sems = chan
    ref = blk(*block)
    return pltpu.make_async_remote_copy(src_ref=ref, dst_ref=ref, send_sem=send_sems.at[k],
                                        recv_sem=recv_sems.at[k], device_id=to, device_id_type=MESH)


def _ag_start(chan, pos):
    x, y, c = pos
    me = (x, y, c)
    chips = [(1 - x, y), (x, 1 - y), (1 - x, 1 - y)]
    first = [_ag_copy(chan, 0, me, (x, y, 1 - c))]
    first += [_ag_copy(chan, 1 + j, me, (*chip, c)) for j, chip in enumerate(chips)]
    for cp in first:
        cp.start()
    return first


def _ag_finish(chan, pos, first):
    x, y, c = pos
    me = (x, y, c)
    sibling = (x, y, 1 - c)
    chips = [(1 - x, y), (x, 1 - y), (1 - x, 1 - y)]
    passed = [_ag_copy(chan, 4 + j, (*chip, c), sibling) for j, chip in enumerate(chips)]
    for j, chip in enumerate(chips):
        _ag_copy(chan, 1 + j, (*chip, c), me).wait_recv()
        passed[j].start()
    _ag_copy(chan, 0, sibling, me).wait_recv()
    for j, chip in enumerate(chips):
        _ag_copy(chan, 4 + j, (*chip, 1 - c), me).wait_recv()
    for cp in first + passed:
        cp.wait_send()


def _slab(buf):
    return lambda px, py, pc: buf.at[4 * px + 2 * py + pc]


def _row_block(buf, rows, align):
    return lambda px, py, pc: buf.at[pl.ds(pl.multiple_of((4 * px + 2 * py + pc) * rows, align), rows), :]


HALF = INW // 2


def _gather_fwd_call(w_in_t, w_out_s, conv_w_s, c, w_ada_s, b_ada, x2, norm_w, qw_t, kw_t, cos_t, sin_t, bq, bk):
    def body(win_ref, wout_ref, cw_ref, c_ref, wada_ref, bada_ref, x_ref, nw_ref, qw_ref, kw_ref, cos_ref, sin_ref,
             bq_ref, bk_ref,
             wtf_hbm, woutf_hbm, cwf_ref, mod_ref, cact_ref,
             h_ref, qraw_ref, kraw_ref, ga_ref, a_ref, g_ref, gb_ref, qr_ref, kr_ref, vb_ref, z_ref,
             cw_buf, ca_buf, mp_buf, h_s, raw0, pt, wtf_ref, woutf_ref,
             s0, r0, s1, r1, s2, r2, s3, r3, s4, r4, out_sems):
        s = pl.program_id(0)
        pos = _mesh_pos()
        x, y, cc = pos
        me3 = (x, y, cc)
        me = 4 * x + 2 * y + cc
        sibling = (x, y, 1 - cc)
        chips = [(1 - x, y), (x, 1 - y), (1 - x, 1 - y)]
        ch_win = (_row_block(wtf_ref, IN_SHARD, 16), s0, r0)
        ch_wout = (_row_block(woutf_ref, OUT_SHARD, 16), s1, r1)
        ch_cw = (_slab(cw_buf), s2, r2)
        ch_ca = (_slab(ca_buf), s3, r3)
        ch_mp = (_slab(mp_buf), s4, r4)

        def first(chan, j):
            return _ag_copy(chan, j, me3, sibling if j == 0 else (*chips[j - 1], cc))

        def passed(chan, j):
            return _ag_copy(chan, 4 + j, (*chips[j], cc), sibling)

        def landed(chan, j):
            return _ag_copy(chan, 1 + j, (*chips[j], cc), me3)

        def relayed(chan, j):
            return _ag_copy(chan, 4 + j, (*chips[j], 1 - cc), me3)

        def from_sibling(chan):
            return _ag_copy(chan, 0, sibling, me3)

        @pl.when(s == 0)
        def _():
            cv = c_ref[...]
            ca_buf[me] = jnp.broadcast_to(cv * _sigmoid(cv), (8, D))
            f_ca = _ag_start(ch_ca, pos)
            wtf_ref[pl.ds(pl.multiple_of(me * IN_SHARD, 16), IN_SHARD), :] = win_ref[...].astype(BF)
            for j in range(3):
                first(ch_win, j).start()
            cw_buf[me] = cw_ref[:, 0, :]
            f_cw = _ag_start(ch_cw, pos)

            _ag_finish(ch_ca, pos, f_ca)
            cact_all = jnp.concatenate([ca_buf[d, 0:1, :] for d in range(NDEV)], axis=0)
            cact_ref[...] = cact_all
            col0 = pl.multiple_of(me * ADA_SHARD, 128)
            mp_buf[me] = jnp.dot(cact_all, wada_ref[...], preferred_element_type=F32,
                                 precision=lax.Precision.HIGHEST) + bada_ref[:, pl.ds(col0, ADA_SHARD)]
            f_mp = _ag_start(ch_mp, pos)
            _ag_finish(ch_cw, pos, f_cw)
            _ag_finish(ch_mp, pos, f_mp)
            for d in range(NDEV):
                cwf_ref[0:CK, CONV_SHARD * d:CONV_SHARD * (d + 1)] = cw_buf[d]
            cwf_ref[CK:CKP, :] = jnp.zeros((CKP - CK, CW), F32)
            mod_ref[...] = jnp.concatenate([mp_buf[d, pl.ds(me, 1), :] for d in range(NDEV)], axis=1)

            for j in (1, 0):
                landed(ch_win, j).wait_recv()
                passed(ch_win, j).start()
            from_sibling(ch_win).wait_recv()
            relayed(ch_win, 1).wait_recv()
            first(ch_win, 1).wait_send()
            first(ch_win, 2).wait_send()
            first(ch_win, 3).start()
            wout = wout_ref[...].astype(BF)
            woutf_ref[pl.ds(pl.multiple_of(me * OUT_SHARD, 16), OUT_SHARD), :] = wout
            for j in range(4):
                first(ch_wout, j).start()

        row0 = pl.multiple_of((s % NT) * TS, TS)

        @pl.when(s < NT)
        def _():
            xv = x_ref[...]
            shift = mod_ref[:, 0:D]
            scale = mod_ref[:, D:2 * D]
            r = lax.rsqrt(jnp.mean(xv * xv, axis=-1, keepdims=True) + EPS)
            hb = ((xv * r * nw_ref[...]) * (1.0 + scale) + shift).astype(BF)
            h_s[pl.ds(row0, TS), :] = hb
            w_half = wtf_ref[pl.ds(pl.multiple_of(x * HALF, 16), HALF), :]
            raw0[pl.ds(row0, TS), :] = lax.dot_general(hb, w_half, (((1,), (1,)), ((), ())),
                                                       preferred_element_type=F32)

        @pl.when(s == NT)
        def _():
            relayed(ch_win, 0).wait_recv()
            landed(ch_win, 2).wait_recv()
            passed(ch_win, 2).start()
            relayed(ch_win, 2).wait_recv()
            pltpu.make_async_copy(wtf_ref, wtf_hbm, out_sems.at[0]).start()

        @pl.when(s >= NT)
        def _():
            hb = h_s[pl.ds(row0, TS), :]
            h_ref[...] = hb
            w_half = wtf_ref[pl.ds(pl.multiple_of((1 - x) * HALF, 16), HALF), :]
            raw1 = lax.dot_general(hb, w_half, (((1,), (1,)), ((), ())), preferred_element_type=F32)
            pt[:, pl.ds(pl.multiple_of(x * HALF, 128), HALF)] = raw0[pl.ds(row0, TS), :]
            pt[:, pl.ds(pl.multiple_of((1 - x) * HALF, 128), HALF)] = raw1
            cos = cos_ref[...]
            sin = sin_ref[...]
            q = pt[:, 0:512]
            qraw_ref[...] = q
            qr_ref[...] = _norm_rope_fwd(q, jnp.tile(qw_ref[...], (1, NQ)), jnp.tile(cos, (1, 4)),
                                         jnp.tile(sin, (1, 4)), bq_ref).astype(BF)
            k = pt[:, 512:640]
            kraw_ref[...] = k
            kr_ref[...] = _norm_rope_fwd(k, jnp.tile(kw_ref[...], (1, NKV)), cos, sin, bk_ref).astype(BF)
            vb_ref[...] = pt[:, 640:768].astype(BF)
            ga_ref[...] = pt[:, 768:1280]
            a = pt[:, 1280:1792]
            g = pt[:, 1792:2304]
            a_ref[...] = a
            g_ref[...] = g
            z_ref[...] = a * _sigmoid(g)
            gb_ref[...] = pt[:, 2304:2816]

        @pl.when(s == 2 * NT - 1)
        def _():
            for j in range(3):
                landed(ch_wout, j).wait_recv()
                passed(ch_wout, j).start()
            from_sibling(ch_wout).wait_recv()
            for j in range(3):
                relayed(ch_wout, j).wait_recv()
            out_copy = pltpu.make_async_copy(woutf_ref, woutf_hbm, out_sems.at[1])
            out_copy.start()
            pltpu.make_async_copy(wtf_ref, wtf_hbm, out_sems.at[0]).wait()
            out_copy.wait()
            first(ch_win, 0).wait_send()
            first(ch_win, 3).wait_send()
            for j in range(3):
                passed(ch_win, j).wait_send()
                passed(ch_wout, j).wait_send()
            for j in range(4):
                first(ch_wout, j).wait_send()

    early = lambda i: (jnp.minimum(i, NT - 1), 0)
    late = lambda i: (jnp.maximum(i - NT, 0), 0)
    t512 = pl.BlockSpec((TS, 512), late)
    t128 = pl.BlockSpec((TS, 128), late)
    sem = pltpu.SemaphoreType.DMA((7,))
    sds = jax.ShapeDtypeStruct
    return pl.pallas_call(
        body, name="gather_fwd", grid=(2 * NT,),
        out_shape=[sds((INW, D), BF), sds((D, D), BF), sds((CKP, CW), F32), sds((1, 3 * D), F32), sds((NDEV, D), F32),
                   sds((S, D), BF), sds((S, AW), F32), sds((S, KVW), F32), sds((S, AW), F32), sds((S, CW), F32),
                   sds((S, CW), F32), sds((S, CW), F32), sds((S, AW), BF), sds((S, KVW), BF), sds((S, KVW), BF),
                   sds((S, CW), F32)],
        in_specs=[_VMEM] * 6 + [pl.BlockSpec((TS, D), early), pl.BlockSpec((1, D), _const),
                                pl.BlockSpec((1, HD), _const), pl.BlockSpec((1, HD), _const), t128, t128,
                                pl.BlockSpec((AW, AW), _const), pl.BlockSpec((KVW, KVW), _const)],
        out_specs=[_ANY, _ANY] + [_VMEM] * 3 + [pl.BlockSpec((TS, D), late), t512, t128, t512, t512, t512, t512, t512,
                                                t128, t128, t512],
        scratch_shapes=[pltpu.VMEM((NDEV, CK, CONV_SHARD), F32), pltpu.VMEM((NDEV, 8, D), F32),
                        pltpu.VMEM((NDEV, 8, ADA_SHARD), F32), pltpu.VMEM((S, D), BF), pltpu.VMEM((S, HALF), F32),
                        pltpu.VMEM((TS, INW), F32), pltpu.VMEM((INW, D), BF), pltpu.VMEM((D, D), BF)]
        + [sem] * 10 + [pltpu.SemaphoreType.DMA((2,))],
        compiler_params=_params(True),
    )(w_in_t, w_out_s, conv_w_s, c, w_ada_s, b_ada, x2, norm_w, qw_t, kw_t, cos_t, sin_t, bq, bk)


QB = 2
NQB = NB // QB


def _band_masks(has_prev):
    kj = lax.broadcasted_iota(jnp.int32, (2 * BLK, 4 * BLK), 0)
    qi = lax.broadcasted_iota(jnp.int32, (2 * BLK, 4 * BLK), 1) & (BLK - 1)
    dist = qi + BLK - kj
    local = (dist >= 0) & (dist < BLK)
    return local & ((kj >= BLK) | has_prev), local


def _key_blocks(sb, prev_ref, cur_ref):
    prev = prev_ref[...] if sb == 0 else cur_ref[BLK * (sb - 1):BLK * sb, :]
    return prev, cur_ref[BLK * sb:BLK * (sb + 1), :]


def _sink_lanes(sink_ref, g):
    lane = lax.broadcasted_iota(jnp.int32, (1, 4 * BLK), 1)
    return jnp.where(lane < BLK, sink_ref[0, 4 * g],
                     jnp.where(lane < 2 * BLK, sink_ref[0, 4 * g + 1],
                               jnp.where(lane < 3 * BLK, sink_ref[0, 4 * g + 2], sink_ref[0, 4 * g + 3])))


def _unstack_t(t):
    return [t[:, BLK * h:BLK * (h + 1)].T for h in range(4)]


def _stack_heads(t, g):
    return jnp.concatenate([t[:, HD * (4 * g + h):HD * (4 * g + h + 1)] for h in range(4)], axis=0)


def _band(prev, cur, g):
    return jnp.concatenate([prev[:, HD * g:HD * (g + 1)], cur[:, HD * g:HD * (g + 1)]], axis=0)


def _softmax_band(qs, kb, mask, sink):
    s = lax.dot_general(kb, qs, (((1,), (1,)), ((), ())), preferred_element_type=F32) * (HD ** -0.5)
    s = jnp.where(mask, s, NEG)
    m = jnp.maximum(jnp.max(s, axis=0, keepdims=True), sink)
    e = jnp.exp(s - m)
    es = jnp.exp(sink - m)
    inv = 1.0 / (jnp.sum(e, axis=0, keepdims=True) + es)
    return e * inv, es * inv


HALO = 32


RC = 64
LC = 128


def _windows(ext_ref, r0, l0, base):
    col = ext_ref[pl.ds(r0, RC + HALO), pl.ds(l0, LC)]
    for s in range(8):
        rolled = col if s == 0 else pltpu.roll(col, RC + HALO - s, 0)
        for t in range(CK):
            if (base + t) % 8 == s:
                a8 = base + t - s
                yield t, rolled[a8:a8 + RC]


def _taps(ext_ref, r0, l0, base, cw_ref, flip):
    acc = None
    for t, win in _windows(ext_ref, r0, l0, base):
        k = CK - 1 - t if flip else t
        term = win * cw_ref[k:k + 1, pl.ds(l0, LC)]
        acc = term if acc is None else acc + term
    return acc


NSUB = (BLK // RC) * (CW // LC)


def _sub_tile(t, row_base):
    r0 = pl.multiple_of(row_base + (t // (CW // LC)) * RC, RC)
    l0 = pl.multiple_of((t % (CW // LC)) * LC, LC)
    return r0, l0


FR = QB * BLK


def _fwd_tail_call(sinks, qr, kr, vb, z, gb, cwf, conv_b, ln_w, ln_b, ga, x2, tgt, mod, w_out_full):
    def body(sink_ref, q_ref, kp_ref, kc_ref, vp_ref, vc_ref, z_ref, zh_ref, gb_ref, cw_ref, cb_ref, lw_ref, lb_ref,
             ga_ref, x_ref, t_ref, mod_ref, w_ref,
             o_ref, zc_ref, dout_ref, dya_ref, dyb_ref, gw_ref, loss_ref, dgate_ref,
             zext, yb_ref):
        i = pl.program_id(0)

        @pl.when(i == 0)
        def _():
            gw_ref[...] = jnp.zeros((D, D), F32)
            loss_ref[...] = jnp.zeros((1, 128), F32)
            dgate_ref[...] = jnp.zeros((1, D), F32)

        def out_proj(rows):
            gav = ga_ref[rows, :]
            ya = o_ref[rows, :] * (gav * _sigmoid(gav))
            ycat = jnp.concatenate([ya.astype(BF), yb_ref[rows, :]], axis=1)
            w = w_ref[...]
            y = jnp.dot(ycat, w, preferred_element_type=F32)
            gate = mod_ref[:, 2 * D:3 * D]
            diff = x_ref[rows, :] + gate * y - t_ref[rows, :]
            sq = jnp.sum(jnp.sum(diff * diff, axis=1, keepdims=True), axis=0, keepdims=True)
            loss_ref[...] += jnp.broadcast_to(sq, (1, 128))
            d_out = diff * (1.0 / D)
            dout_ref[rows, :] = d_out
            dgate_ref[...] += jnp.sum(d_out * y, axis=0, keepdims=True)
            dy = (d_out * gate).astype(BF)
            d_ycat = lax.dot_general(dy, w, (((1,), (1,)), ((), ())), preferred_element_type=F32)
            dya_ref[rows, :] = d_ycat[:, 0:AW]
            dyb_ref[rows, :] = d_ycat[:, AW:D]
            gw_ref[...] += lax.dot_general(ycat, dy, (((0,), (0,)), ((), ())), preferred_element_type=F32)

        zext[0:HALO, :] = jnp.where(i > 0, zh_ref[...], 0.0)
        zext[HALO:HALO + FR, :] = z_ref[...]
        masks = _band_masks(i > 0)
        for sb in range(QB):
            rows = slice(BLK * sb, BLK * (sb + 1))
            mask = masks[min(sb, 1)]
            q = q_ref[rows, :]
            kp, kc = _key_blocks(sb, kp_ref, kc_ref)
            vp, vc = _key_blocks(sb, vp_ref, vc_ref)
            for g in range(NKV):
                p, _ = _softmax_band(_stack_heads(q, g), _band(kp, kc, g), mask, _sink_lanes(sink_ref, g))
                o_t = lax.dot_general(_band(vp, vc, g), p.astype(BF), (((0,), (0,)), ((), ())),
                                      preferred_element_type=F32)
                for h, o_h in enumerate(_unstack_t(o_t)):
                    o_ref[rows, HD * (4 * g + h):HD * (4 * g + h + 1)] = o_h
            for r0 in range(BLK * sb, BLK * (sb + 1), RC):
                for l0 in range(0, CW, LC):
                    acc = _taps(zext, r0, l0, HALO - (CK - 1), cw_ref, False)
                    zc_ref[r0:r0 + RC, l0:l0 + LC] = acc + cb_ref[:, l0:l0 + LC]
            zc = zc_ref[rows, :]
            mu = jnp.mean(zc, axis=-1, keepdims=True)
            dz = zc - mu
            rstd = lax.rsqrt(jnp.mean(dz * dz, axis=-1, keepdims=True) + EPS)
            zn = dz * rstd * lw_ref[...] + lb_ref[...]
            gbv = gb_ref[rows, :]
            yb_ref[rows, :] = ((zn * _sigmoid(zn)) * (gbv * _sigmoid(gbv))).astype(BF)
            if (sb + 1) * BLK % TS == 0:
                out_proj(slice((sb + 1) * BLK - TS, (sb + 1) * BLK))

    prev = lambda i: (jnp.maximum(QB * i - 1, 0), 0)
    halo = lambda i: (jnp.maximum(FR // HALO * i - 1, 0), 0)
    f512 = pl.BlockSpec((FR, 512), _row)
    f128 = pl.BlockSpec((FR, KVW), _row)
    f1024 = pl.BlockSpec((FR, D), _row)
    c512 = pl.BlockSpec((1, CW), _const)
    sds = jax.ShapeDtypeStruct
    return pl.pallas_call(
        body, name="fwd_tail", grid=(NQB,),
        out_shape=[sds((S, AW), F32), sds((S, CW), F32), sds((S, D), F32), sds((S, AW), F32), sds((S, CW), F32),
                   sds((D, D), F32), sds((1, 128), F32), sds((1, D), F32)],
        in_specs=[_SMEM, f512, pl.BlockSpec((BLK, KVW), prev), f128, pl.BlockSpec((BLK, KVW), prev), f128,
                  f512, pl.BlockSpec((HALO, CW), halo), f512, pl.BlockSpec((CKP, CW), _const), c512, c512, c512,
                  f512, f1024, f1024, pl.BlockSpec((1, 3 * D), _const),
                  pl.BlockSpec((D, D), _const, pipeline_mode=pl.Buffered(1))],
        out_specs=[f512, f512, f1024, f512, f512, _VMEM, _VMEM, _VMEM],
        scratch_shapes=[pltpu.VMEM((FR + HALO, CW), F32), pltpu.VMEM((FR, CW), BF)],
        compiler_params=_params(True),
    )(sinks, qr, kr, kr, vb, vb, z, z, gb, cwf, conv_b, ln_w, ln_b, ga, x2, tgt, mod, w_out_full)


def _ln_gate_bwd(d_yb, zc, gbv, lw, lb):
    mu = jnp.mean(zc, axis=-1, keepdims=True)
    dz = zc - mu
    rstd = lax.rsqrt(jnp.mean(dz * dz, axis=-1, keepdims=True) + EPS)
    zh = dz * rstd
    zn = zh * lw + lb
    act_n, dact_n = _silu_and_grad(zn)
    act_g, dact_g = _silu_and_grad(gbv)
    d_gb = d_yb * act_n * dact_g
    d_zn = d_yb * act_g * dact_n
    dzh = d_zn * lw
    d_zc = rstd * (dzh - jnp.mean(dzh, axis=-1, keepdims=True) - zh * jnp.mean(dzh * zh, axis=-1, keepdims=True))
    return d_zc, d_gb, d_zn, zh


def _attn_conv_bwd_call(sinks, qr, kr, vb, d_ya, ga, o, qraw, qw_t, cos_t, sin_t, bq, d_yb, zc, gb, z, a, g, cwf,
                        ln_w, ln_b):
    def body(sink_ref, q_ref, kp_ref, kc_ref, vp_ref, vc_ref, dya_ref, ga_ref, o_ref, qraw_ref, qw_ref,
             cos_ref, sin_ref, bq_ref,
             dyb_ref, dybn_ref, zc_ref, zcn_ref, gb_ref, gbn_ref, z_ref, zh_ref, a_ref, g_ref, cw_ref, lw_ref, lb_ref,
             dqraw_ref, dga_ref, dk_ref, dv_ref, gqw_ref, gsink_ref,
             dgb_ref, da_ref, dg_ref, gcw_ref, glw_ref, glb_ref, gcb_ref,
             dext, zext, gacc):
        i = pl.program_id(0)

        @pl.when(i == 0)
        def _():
            dk_ref[...] = jnp.zeros((S, KVW), F32)
            dv_ref[...] = jnp.zeros((S, KVW), F32)
            gqw_ref[...] = jnp.zeros((1, AW), F32)
            gsink_ref[...] = jnp.zeros((1, 128), F32)
            gacc[...] = jnp.zeros((CKP * 8, CW), F32)
            glw_ref[...] = jnp.zeros((1, CW), F32)
            glb_ref[...] = jnp.zeros((1, CW), F32)
            gcb_ref[...] = jnp.zeros((1, CW), F32)

        lw = lw_ref[...]
        lb = lb_ref[...]

        def ln_rows(rows):
            d_zc, d_gb, d_zn, zh = _ln_gate_bwd(dyb_ref[rows, :], zc_ref[rows, :], gb_ref[rows, :], lw, lb)
            dgb_ref[rows, :] = d_gb.astype(BF)
            glw_ref[...] += jnp.sum(d_zn * zh, axis=0, keepdims=True)
            glb_ref[...] += jnp.sum(d_zn, axis=0, keepdims=True)
            gcb_ref[...] += jnp.sum(d_zc, axis=0, keepdims=True)
            dext[rows, :] = d_zc

        ln_rows(slice(0, BLK))
        zext[0:HALO, :] = jnp.where(i > 0, zh_ref[...], 0.0)
        zext[HALO:HALO + FR, :] = z_ref[...]

        lane = lax.broadcasted_iota(jnp.int32, (1, 128), 1)
        gsink = jnp.zeros((1, 128), F32)
        dq_rows = []
        masks = _band_masks(i > 0)
        for sb in range(QB):
            rows = slice(BLK * sb, BLK * (sb + 1))
            if sb + 1 < QB:
                ln_rows(slice(BLK * (sb + 1), BLK * (sb + 2)))
            else:
                d_zc_next, _, _, _ = _ln_gate_bwd(dybn_ref[...], zcn_ref[...], gbn_ref[...], lw, lb)
                dext[FR:FR + HALO, :] = jnp.where(i < NQB - 1, d_zc_next, 0.0)
            mask = masks[min(sb, 1)]
            q = q_ref[rows, :]
            d_ya = dya_ref[rows, :]
            act, dact = _silu_and_grad(ga_ref[rows, :])
            dga_ref[rows, :] = (d_ya * o_ref[rows, :] * dact).astype(BF)
            d_o = (d_ya * act).astype(BF)
            kp, kc = _key_blocks(sb, kp_ref, kc_ref)
            vp, vc = _key_blocks(sb, vp_ref, vc_ref)
            dq_parts, dk_parts, dv_parts = [], [], []
            for gi in range(NKV):
                qs = _stack_heads(q, gi)
                kb = _band(kp, kc, gi)
                vbd = _band(vp, vc, gi)
                p, ps = _softmax_band(qs, kb, mask, _sink_lanes(sink_ref, gi))
                dos = _stack_heads(d_o, gi)
                dp = lax.dot_general(vbd, dos, (((1,), (1,)), ((), ())), preferred_element_type=F32)
                dr = jnp.sum(p * dp, axis=0, keepdims=True)
                ds = (p * (dp - dr) * (HD ** -0.5)).astype(BF)
                sink_term = ps * dr
                for h in range(4):
                    part = jnp.sum(sink_term[:, BLK * h:BLK * (h + 1)], axis=1, keepdims=True)
                    gsink = gsink - jnp.where(lane == 4 * gi + h, part, 0.0)
                dv_parts.append(jnp.dot(p.astype(BF), dos, preferred_element_type=F32))
                dk_parts.append(jnp.dot(ds, qs, preferred_element_type=F32))
                dq_t = lax.dot_general(kb, ds, (((0,), (0,)), ((), ())), preferred_element_type=F32)
                dq_parts.extend(_unstack_t(dq_t))
            dkb = jnp.concatenate(dk_parts, axis=1)
            dvb = jnp.concatenate(dv_parts, axis=1)
            blk = QB * i + sb
            r_prev = pl.multiple_of(jnp.maximum(blk - 1, 0) * BLK, BLK)
            r_cur = pl.multiple_of(blk * BLK, BLK)
            dk_ref[pl.ds(r_prev, BLK), :] += dkb[0:BLK]
            dv_ref[pl.ds(r_prev, BLK), :] += dvb[0:BLK]
            dk_ref[pl.ds(r_cur, BLK), :] += dkb[BLK:2 * BLK]
            dv_ref[pl.ds(r_cur, BLK), :] += dvb[BLK:2 * BLK]
            dq_rows.append(jnp.concatenate(dq_parts, axis=1))

            def taps_sub(t, carry, sb=sb):
                r0, l0 = _sub_tile(t, BLK * sb)
                here = (pl.ds(r0, RC), pl.ds(l0, LC))
                d_z = _taps(dext, r0, l0, 0, cw_ref, True)
                sg = _sigmoid(g_ref[here])
                da_ref[here] = (d_z * sg).astype(BF)
                dg_ref[here] = (d_z * a_ref[here] * sg * (1.0 - sg)).astype(BF)
                d_sub = dext[here]
                for k, win in _windows(zext, r0, l0, HALO - (CK - 1)):
                    prod = d_sub * win
                    part = prod[0:8]
                    for q8 in range(1, RC // 8):
                        part = part + prod[8 * q8:8 * q8 + 8]
                    gacc[8 * k:8 * k + 8, pl.ds(l0, LC)] += part
                return carry

            lax.fori_loop(0, NSUB, taps_sub, 0)
        gsink_ref[...] += gsink
        dq = jnp.concatenate(dq_rows, axis=0)
        dq_raw, g_qw = _norm_rope_bwd(dq, qraw_ref[...], jnp.tile(qw_ref[...], (1, NQ)),
                                      jnp.tile(cos_ref[...], (1, 4)), jnp.tile(sin_ref[...], (1, 4)), bq_ref)
        dqraw_ref[...] = dq_raw.astype(BF)
        gqw_ref[...] += g_qw

        @pl.when(i == NQB - 1)
        def _():
            for k in range(CK):
                gcw_ref[k:k + 1, :] = jnp.sum(gacc[8 * k:8 * k + 8, :], axis=0, keepdims=True)
            gcw_ref[CK:CKP, :] = jnp.zeros((CKP - CK, CW), F32)

    prev = lambda i: (jnp.maximum(QB * i - 1, 0), 0)
    halo_prev = lambda i: (jnp.maximum(FR // HALO * i - 1, 0), 0)
    halo_next = lambda i: (jnp.minimum(FR // HALO * (i + 1), S // HALO - 1), 0)
    f512 = pl.BlockSpec((FR, 512), _row)
    f128 = pl.BlockSpec((FR, 128), _row)
    hn = pl.BlockSpec((HALO, CW), halo_next)
    c512 = pl.BlockSpec((1, CW), _const)
    sds = jax.ShapeDtypeStruct
    vec = sds((1, CW), F32)
    return pl.pallas_call(
        body, name="attn_conv_bwd", grid=(NQB,),
        out_shape=[sds((S, AW), BF), sds((S, AW), BF), sds((S, KVW), F32), sds((S, KVW), F32), sds((1, AW), F32),
                   sds((1, 128), F32),
                   sds((S, CW), BF), sds((S, CW), BF), sds((S, CW), BF), sds((CKP, CW), F32), vec, vec, vec],
        in_specs=[_SMEM, f512, pl.BlockSpec((BLK, KVW), prev), f128, pl.BlockSpec((BLK, KVW), prev), f128,
                  f512, f512, f512, f512, pl.BlockSpec((1, HD), _const), f128, f128, pl.BlockSpec((AW, AW), _const),
                  f512, hn, f512, hn, f512, hn, f512, pl.BlockSpec((HALO, CW), halo_prev), f512, f512,
                  pl.BlockSpec((CKP, CW), _const), c512, c512],
        out_specs=[f512, f512, _VMEM, _VMEM, _VMEM, _VMEM, f512, f512, f512, _VMEM, _VMEM, _VMEM, _VMEM],
        scratch_shapes=[pltpu.VMEM((FR + HALO, CW), F32), pltpu.VMEM((FR + HALO, CW), F32),
                        pltpu.VMEM((CKP * 8, CW), F32)],
        compiler_params=_params(True),
    )(sinks, qr, kr, kr, vb, vb, d_ya, ga, o, qraw, qw_t, cos_t, sin_t, bq,
      d_yb, d_yb, zc, zc, gb, gb, z, z, a, g, cwf, ln_w, ln_b)


SM_ROWS = 8
PIECES = ((0, 512), (512, 640), (640, 768), (768, 1280), (1280, 1792), (1792, 2304), (2304, 2816))


def _bwd_in_call(dqraw, dk, dv, dga, da, dg, dgb, kraw, kw_t, cos_t, sin_t, bk, h, wt_full, x2, d_out, mod, norm_w,
                 gw_out, gcw, glw, glb, gcb, gqw, gsink, dgate, loss_p, cact_all):
    def body(dq_ref, dk_ref, dv_ref, dga_ref, da_ref, dg_ref, dgb_ref, kraw_ref, kw_ref, cos_ref, sin_ref, bk_ref,
             h_ref, wt_ref, x_ref, dout_ref, mod_ref, nw_ref, gwout_ref, gcw_ref, glw_ref, glb_ref, gcb_ref, gqw_ref,
             gsink_ref, dgate_ref, loss_ref, cact_ref,
             gx_ref, o_gwin, o_gwout, o_gwada, o_gbada, o_gnw, o_gqw, o_gkw, o_gsink, o_gcw, o_gcb, o_glw, o_glb,
             o_loss,
             acc, win_send, win_sib, win_ici, wout_send, wout_sib, wout_ici, sm_buf, cw_buf, dmod_all, vec_acc, gkw_acc,
             wi_ds, wi_dr, wi_is, wi_ir, wo_ds, wo_dr, wo_is, wo_ir, sm_s, sm_r, cw_s, cw_r):
        i = pl.program_id(0)
        pos = _mesh_pos()
        x, y, cc = pos
        me = 4 * x + 2 * y + cc

        def chip(j):
            return (1 - x if j & 1 else x, 1 - y if j & 2 else y)

        def rows_of(buf, px, py, pc, rows, align):
            return buf.at[pl.ds(pl.multiple_of((4 * px + 2 * py + pc) * rows, align), rows), :]

        bufs = {"in": (win_send, win_sib, win_ici, IN_SHARD, wi_ds, wi_dr, wi_is, wi_ir),
                "out": (wout_send, wout_sib, wout_ici, OUT_SHARD, wo_ds, wo_dr, wo_is, wo_ir)}

        def d2d_copy(j, which):
            send, sib, _, rows, ds_, dr_, _, _ = bufs[which]
            px, py = chip(j)
            return pltpu.make_async_remote_copy(src_ref=rows_of(send, px, py, 1 - cc, rows, 16), dst_ref=sib.at[j],
                                                send_sem=ds_.at[j], recv_sem=dr_.at[j], device_id=(x, y, 1 - cc),
                                                device_id_type=MESH)

        def ici_copy(j, which):
            send, _, ici, rows, _, _, is_, ir_ = bufs[which]
            px, py = chip(j)
            return pltpu.make_async_remote_copy(src_ref=rows_of(send, px, py, cc, rows, 16), dst_ref=ici.at[j - 1],
                                                send_sem=is_.at[j - 1], recv_sem=ir_.at[j - 1], device_id=(px, py, cc),
                                                device_id_type=MESH)

        def level2(which, partial_ref):
            send, sib, _, rows, _, _, _, _ = bufs[which]
            for j in range(1, 4):
                d2d_copy(j, which).wait_recv()
                px, py = chip(j)
                mine = rows_of(partial_ref, px, py, cc, rows, 8)[...]
                rows_of(send, px, py, cc, rows, 16)[...] = (mine + sib[j].astype(F32)).astype(BF)
                ici_copy(j, which).start()

        def finish(which, partial_ref):
            _, sib, ici, rows, _, _, _, _ = bufs[which]
            d2d_copy(0, which).wait_recv()
            total = rows_of(partial_ref, x, y, cc, rows, 8)[...] + sib[0].astype(F32)
            for j in range(1, 4):
                ici_copy(j, which).wait_recv()
                total = total + ici[j - 1].astype(F32)
            for j in range(4):
                d2d_copy(j, which).wait_send()
            for j in range(1, 4):
                ici_copy(j, which).wait_send()
            return total

        def dproj_pieces():
            dk_raw, g_kw = _norm_rope_bwd(dk_ref[...], kraw_ref[...], jnp.tile(kw_ref[...], (1, NKV)), cos_ref[...],
                                          sin_ref[...], bk_ref)
            return [dq_ref[...], dk_raw.astype(BF), dv_ref[...].astype(BF), dga_ref[...], da_ref[...], dg_ref[...],
                    dgb_ref[...]], g_kw

        @pl.when(i == 0)
        def _():
            acc[...] = jnp.zeros((INW, D), F32)
            vec_acc[...] = jnp.zeros((8, D), F32)
            gkw_acc[...] = jnp.zeros((1, KVW), F32)
            wout_send[...] = gwout_ref[...].astype(BF)
            for j in range(4):
                d2d_copy(j, "out").start()
            cw_buf[me] = gcw_ref[...]
            _ag_start((_slab(cw_buf), cw_s, cw_r), pos)

        @pl.when(i == 2)
        def _():
            level2("out", gwout_ref)

        @pl.when(i < NT)
        def _():
            pieces, g_kw = dproj_pieces()
            gkw_acc[...] += g_kw
            hv = h_ref[...]
            for (lo, hi), piece in zip(PIECES, pieces):
                acc[lo:hi, :] += lax.dot_general(piece, hv, (((0,), (0,)), ((), ())), preferred_element_type=F32)

        @pl.when(i == NT - 1)
        def _():
            for lo, hi in PIECES:
                win_send[lo:hi, :] = acc[lo:hi, :].astype(BF)
            for j in range(4):
                d2d_copy(j, "in").start()

        @pl.when(i == NT + 2)
        def _():
            level2("in", acc)

        @pl.when(i >= NT)
        def _():
            pieces, _ = dproj_pieces()
            dproj = jnp.concatenate(pieces, axis=1)
            d_h = jnp.dot(dproj, wt_ref[...], preferred_element_type=F32)
            xv = x_ref[...]
            scale = mod_ref[:, D:2 * D]
            nw = nw_ref[...]
            r = lax.rsqrt(jnp.mean(xv * xv, axis=-1, keepdims=True) + EPS)
            xn = xv * r
            vec_acc[0:1, :] += jnp.sum(d_h, axis=0, keepdims=True)
            vec_acc[1:2, :] += jnp.sum(d_h * (xn * nw), axis=0, keepdims=True)
            d_u = d_h * (1.0 + scale)
            vec_acc[2:3, :] += jnp.sum(d_u * xn, axis=0, keepdims=True)
            d_xn = d_u * nw
            gx_ref[...] = dout_ref[...] + r * (d_xn - xn * jnp.mean(d_xn * xn, axis=-1, keepdims=True))

        @pl.when(i == 2 * NT - 1)
        def _():
            ch_cw = (_slab(cw_buf), cw_s, cw_r)
            z128 = jnp.zeros((1, 128), F32)
            row4 = jnp.concatenate([glw_ref[...], glb_ref[...]], axis=1)
            row5 = jnp.concatenate([gcb_ref[...], gqw_ref[...]], axis=1)
            row6 = jnp.concatenate([gkw_acc[...], gsink_ref[...], loss_ref[...]] + [z128] * 5, axis=1)
            sm_buf[me] = jnp.concatenate([vec_acc[0:2, :], dgate_ref[...], vec_acc[2:3, :], row4, row5, row6,
                                          jnp.zeros((1, D), F32)], axis=0)

            def sm_copy(k):
                peer = (1 - x if k & 4 else x, 1 - y if k & 2 else y, 1 - cc if k & 1 else cc)
                return pltpu.make_async_remote_copy(src_ref=sm_buf.at[me], dst_ref=sm_buf.at[me],
                                                    send_sem=sm_s.at[k - 1], recv_sem=sm_r.at[k - 1],
                                                    device_id=peer, device_id_type=MESH)

            for k in range(1, NDEV):
                sm_copy(k).start()
            _ag_finish(ch_cw, pos, [_ag_copy(ch_cw, 0, (x, y, cc), (x, y, 1 - cc))]
                       + [_ag_copy(ch_cw, 1 + j, (x, y, cc), (*chip(j + 1), cc)) for j in range(3)])
            for k in range(1, NDEV):
                sm_copy(k).wait_recv()
            tot = sm_buf[0]
            cw_tot = cw_buf[0]
            for d in range(1, NDEV):
                tot = tot + sm_buf[d]
                cw_tot = cw_tot + cw_buf[d]
            o_gbada[...] = jnp.concatenate([tot[0:1, :], tot[1:2, :], tot[2:3, :]], axis=1)
            o_gnw[...] = tot[3:4, :]
            o_glw[...] = tot[4:5, 0:CW]
            o_glb[...] = tot[4:5, CW:D]
            o_gcb[...] = tot[5:6, 0:CW]
            gq = tot[5:6, CW:CW + HD]
            for hh in range(1, NQ):
                gq = gq + tot[5:6, CW + HD * hh:CW + HD * (hh + 1)]
            o_gqw[...] = gq
            o_gkw[...] = tot[6:7, 0:HD] + tot[6:7, HD:2 * HD]
            o_gsink[...] = tot[6:7, 128:128 + NQ]
            o_loss[...] = tot[6:7, 256:384] * (0.5 / D)
            mine = jnp.zeros((CK, CONV_SHARD), F32)
            for d in range(NDEV):
                mine = mine + jnp.where(me == d, cw_tot[0:CK, CONV_SHARD * d:CONV_SHARD * (d + 1)], 0.0)
            for k in range(CK):
                o_gcw[k] = mine[k:k + 1, :]
            for d in range(NDEV):
                dmod_all[d:d + 1, :] = jnp.concatenate([sm_buf[d, 0:1, :], sm_buf[d, 1:2, :], sm_buf[d, 2:3, :]],
                                                       axis=1)
            col0 = pl.multiple_of(me * ADA_SHARD, 128)
            o_gwada[...] = lax.dot_general(cact_ref[...], dmod_all[:, pl.ds(col0, ADA_SHARD)], (((0,), (0,)), ((), ())),
                                           preferred_element_type=F32, precision=lax.Precision.HIGHEST)

            o_gwout[...] = finish("out", gwout_ref)
            o_gwin[...] = finish("in", acc)
            for k in range(1, NDEV):
                sm_copy(k).wait_send()

    half = lambda i: (i % NT, 0)
    late = lambda i: (jnp.maximum(i - NT, 0), 0)
    t512 = pl.BlockSpec((TS, 512), half)
    t128 = pl.BlockSpec((TS, 128), half)
    l1024 = pl.BlockSpec((TS, D), late)
    sem7 = pltpu.SemaphoreType.DMA((7,))
    sem4 = pltpu.SemaphoreType.DMA((4,))
    sem3 = pltpu.SemaphoreType.DMA((3,))
    sds = jax.ShapeDtypeStruct
    return pl.pallas_call(
        body, name="bwd_in", grid=(2 * NT,),
        out_shape=[sds((S, D), F32), sds((IN_SHARD, D), F32), sds((OUT_SHARD, D), F32), sds((D, ADA_SHARD), F32),
                   sds((1, 3 * D), F32), sds((1, D), F32), sds((1, HD), F32), sds((1, HD), F32), sds((1, NQ), F32),
                   sds((CK, 1, CONV_SHARD), F32), sds((1, CW), F32), sds((1, CW), F32), sds((1, CW), F32),
                   sds((1, 128), F32)],
        in_specs=[t512, t128, t128, t512, t512, t512, t512, t128, pl.BlockSpec((1, HD), _const), t128, t128,
                  pl.BlockSpec((KVW, KVW), _const), pl.BlockSpec((TS, D), half),
                  pl.BlockSpec((INW, D), _const, pipeline_mode=pl.Buffered(1)), l1024, l1024,
                  pl.BlockSpec((1, 3 * D), _const), pl.BlockSpec((1, D), _const)] + [_VMEM] * 10,
        out_specs=[l1024] + [_VMEM] * 13,
        scratch_shapes=[pltpu.VMEM((INW, D), F32), pltpu.VMEM((INW, D), BF), pltpu.VMEM((4, IN_SHARD, D), BF),
                        pltpu.VMEM((3, IN_SHARD, D), BF), pltpu.VMEM((D, D), BF), pltpu.VMEM((4, OUT_SHARD, D), BF),
                        pltpu.VMEM((3, OUT_SHARD, D), BF),
                        pltpu.VMEM((NDEV, SM_ROWS, D), F32), pltpu.VMEM((NDEV, CKP, CW), F32),
                        pltpu.VMEM((NDEV, 3 * D), F32), pltpu.VMEM((8, D), F32), pltpu.VMEM((1, KVW), F32)]
        + [sem4, sem4, sem3, sem3] * 2 + [sem7] * 4,
        compiler_params=pltpu.CompilerParams(dimension_semantics=("arbitrary",), vmem_limit_bytes=BIG_VMEM_LIMIT),
    )(dqraw, dk, dv, dga, da, dg, dgb, kraw, kw_t, cos_t, sin_t, bk, h, wt_full, x2, d_out, mod, norm_w,
      gw_out, gcw, glw, glb, gcb, gqw, gsink, dgate, loss_p, cact_all)


ADAM_STEPS = 4


def _adam_chunking(arr):
    if arr.ndim == 2 and arr.shape[0] % (8 * ADAM_STEPS) == 0:
        return "rows"
    if arr.ndim == 2 and arr.shape[1] % (128 * ADAM_STEPS) == 0:
        return "cols"
    return None


def _adam_call(ws, gs, ms, vs, grad_x, loss_v):
    n = len(ws)
    bc1 = 1.0 - ADAM_B1 ** ADAM_STEP
    bc2 = 1.0 - ADAM_B2 ** ADAM_STEP
    chunked = [_adam_chunking(w) for w in ws]

    def body(*refs):
        ins, outs = refs[:4 * n + 2], refs[4 * n + 2:]
        i = pl.program_id(0)

        def update(j):
            w, g, m, v = (ins[j][...], ins[n + j][...], ins[2 * n + j][...], ins[3 * n + j][...])
            m_new = ADAM_B1 * m + (1.0 - ADAM_B1) * g
            v_new = ADAM_B2 * v + (1.0 - ADAM_B2) * (g * g)
            m_hat = m_new / bc1
            v_hat = v_new / bc2
            outs[j][...] = g
            outs[n + j][...] = -ADAM_LR * (m_hat / (jnp.sqrt(v_hat) + ADAM_EPS) + ADAM_WD * w)
            outs[2 * n + j][...] = m_new
            outs[3 * n + j][...] = v_new

        for j in range(n):
            if chunked[j]:
                update(j)
        outs[4 * n][...] = ins[4 * n][...]

        @pl.when(i == 0)
        def _():
            for j in range(n):
                if not chunked[j]:
                    update(j)
            outs[4 * n + 1][...] = ins[4 * n + 1][...]

    def spec(arr, how):
        if how == "rows":
            return pl.BlockSpec((arr.shape[0] // ADAM_STEPS, arr.shape[1]), _row)
        if how == "cols":
            return pl.BlockSpec((arr.shape[0], arr.shape[1] // ADAM_STEPS), lambda i: (0, i))
        zeros = (0,) * arr.ndim
        return pl.BlockSpec(arr.shape, lambda i: zeros)

    par_specs = [spec(w, ch) for w, ch in zip(ws, chunked)]
    extra = [spec(grad_x, "rows"), spec(loss_v, None)]
    shapes = [jax.ShapeDtypeStruct(w.shape, F32) for w in ws]
    res = pl.pallas_call(
        body, name="adam", grid=(ADAM_STEPS,),
        out_shape=shapes * 4 + [jax.ShapeDtypeStruct(grad_x.shape, F32), jax.ShapeDtypeStruct(loss_v.shape, F32)],
        in_specs=par_specs * 4 + extra, out_specs=par_specs * 4 + extra,
        compiler_params=_params(True),
    )(*ws, *gs, *ms, *vs, grad_x, loss_v)
    return res[0:n], res[n:2 * n], res[2 * n:3 * n], res[3 * n:4 * n], res[4 * n], res[4 * n + 1]


def _rope_tables():
    inv = (np.float32(ROPE_THETA) ** (-np.arange(0, HD, 2, dtype=np.float32) / np.float32(HD))).astype(np.float32)
    ang = (np.arange(S, dtype=np.float32)[:, None] * inv[None, :]).astype(np.float32)
    cos, sin = np.cos(ang).astype(np.float32), np.sin(ang).astype(np.float32)
    cos64 = np.concatenate([cos, cos], axis=-1)
    sin64 = np.concatenate([-sin, sin], axis=-1)
    return jnp.asarray(np.tile(cos64, (1, 2))), jnp.asarray(np.tile(sin64, (1, 2)))


def _group_matrix(width):
    idx = np.arange(width) // HD
    return jnp.asarray(np.where(idx[:, None] == idx[None, :], 1.0 / HD, 0.0).astype(np.float32)).astype(BF)


def kernel(x, c, w_ada, b_ada, norm_w, w_in, q_norm_w, k_norm_w, sinks, conv_w, conv_b, ln_w, ln_b, w_out, loss_target, m_w_ada, m_b_ada, m_norm_w, m_w_in, m_q_norm_w, m_k_norm_w, m_sinks, m_conv_w, m_conv_b, m_ln_w, m_ln_b, m_w_out, v_w_ada, v_b_ada, v_norm_w, v_w_in, v_q_norm_w, v_k_norm_w, v_sinks, v_conv_w, v_conv_b, v_ln_w, v_ln_b, v_w_out):
    x2 = x[0]
    tgt = loss_target[0]
    cos_t, sin_t = _rope_tables()
    bq = _group_matrix(AW)
    bk = _group_matrix(KVW)
    qw_t, kw_t = q_norm_w, k_norm_w

    tr = lambda t: jnp.swapaxes(t[0], 0, 1)
    tc = lambda t: jnp.swapaxes(t, 0, 1)
    (wt_full, w_out_full, cwf, mod, cact_all, h, qraw, kraw, ga, a, g, gb, qr, kr, vb, z) = _gather_fwd_call(
        tr(w_in), w_out[0], tc(conv_w), c, w_ada[0], b_ada, x2, norm_w, qw_t, kw_t, cos_t, sin_t, bq, bk)
    o, zc, d_out, d_ya, d_yb, gw_out, loss_p, dgate = _fwd_tail_call(
        sinks, qr, kr, vb, z, gb, cwf, conv_b, ln_w, ln_b, ga, x2, tgt, mod, w_out_full)

    dqraw, dga, dk, dv, gqw, gsink, dgb, da, dg, gcw, glw, glb, gcb = _attn_conv_bwd_call(
        sinks, qr, kr, vb, d_ya, ga, o, qraw, qw_t, cos_t, sin_t, bq, d_yb, zc, gb, z, a, g, cwf, ln_w, ln_b)
    (grad_x, g_w_in_t, g_w_out, g_w_ada, g_b_ada, g_norm_w, g_qw, g_kw, g_sinks, g_conv_w, g_conv_b, g_ln_w, g_ln_b,
     loss_v) = _bwd_in_call(dqraw, dk, dv, dga, da, dg, dgb, kraw, kw_t, cos_t, sin_t, bk, h, wt_full, x2, d_out, mod,
                            norm_w, gw_out, gcw, glw, glb, gcb, gqw, gsink, dgate, loss_p, cact_all)

    ws = [w_ada[0], b_ada, norm_w, tr(w_in), q_norm_w, k_norm_w, sinks, tc(conv_w), conv_b, ln_w, ln_b, w_out[0]]
    gs = [g_w_ada, g_b_ada, g_norm_w, g_w_in_t, g_qw, g_kw, g_sinks, g_conv_w, g_conv_b, g_ln_w, g_ln_b, g_w_out]
    ms = [m_w_ada[0], m_b_ada, m_norm_w, tr(m_w_in), m_q_norm_w, m_k_norm_w, m_sinks, tc(m_conv_w), m_conv_b, m_ln_w,
          m_ln_b, m_w_out[0]]
    vs = [v_w_ada[0], v_b_ada, v_norm_w, tr(v_w_in), v_q_norm_w, v_k_norm_w, v_sinks, tc(v_conv_w), v_conv_b, v_ln_w,
          v_ln_b, v_w_out[0]]
    grads, deltas, new_m, new_v, grad_x, loss_v = _adam_call(ws, gs, ms, vs, grad_x, loss_v)
    shaped = [w_ada, b_ada, norm_w, w_in, q_norm_w, k_norm_w, sinks, conv_w, conv_b, ln_w, ln_b, w_out]
    W_IN_POS, CONV_W_POS = 3, 7

    def like(vals):
        vals = [jnp.swapaxes(v, 0, 1) if j in (W_IN_POS, CONV_W_POS) else v for j, v in enumerate(vals)]
        return [v.reshape(s.shape) for v, s in zip(vals, shaped)]

    return (loss_v[0, 0], grad_x[None], *like(grads), *like(deltas), *like(new_m), *like(new_v))
```

```python
import functools

import jax
import jax.numpy as jnp
import numpy as np
from jax import lax
from jax.experimental import pallas as pl
from jax.experimental.pallas import tpu as pltpu

S = 2048
D = 1024
NDEV = 8
HD = 64
NQ = 8
NKV = 2
AW = 512
KVW = 128
CW = 512
INW = 2816
IN_SHARD = INW // NDEV
ADA_SHARD = 3 * D // NDEV
OUT_SHARD = D // NDEV
CONV_SHARD = CW // NDEV
CK = 31
CKP = 32
BLK = 128
TS = 256
NT = S // TS
NB = S // BLK
EPS = 1e-6
ROPE_THETA = 10000.0
NEG = -1e30
BF = jnp.bfloat16
F32 = jnp.float32

ADAM_LR = 0.001
ADAM_B1 = 0.9
ADAM_B2 = 0.999
ADAM_EPS = 1e-08
ADAM_WD = 0.01
ADAM_STEP = 10

VMEM_LIMIT = 56 * 1024 * 1024
BIG_VMEM_LIMIT = 62 * 1024 * 1024
MESH = pl.DeviceIdType.MESH

_VMEM = pl.BlockSpec(memory_space=pltpu.VMEM)
_SMEM = pl.BlockSpec(memory_space=pltpu.SMEM)
_ANY = pl.BlockSpec(memory_space=pl.ANY)


def _params(grid=False):
    if grid:
        return pltpu.CompilerParams(dimension_semantics=("arbitrary",), vmem_limit_bytes=VMEM_LIMIT)
    return pltpu.CompilerParams(vmem_limit_bytes=VMEM_LIMIT)


def _row(i):
    return (i, 0)


def _const(i):
    return (0, 0)


def _sigmoid(t):
    return 1.0 / (1.0 + jnp.exp(-t))


def _silu_and_grad(t):
    sg = _sigmoid(t)
    return t * sg, sg * (1.0 + t * (1.0 - sg))


def _group_mean(t, b_ref):
    hi = t.astype(BF)
    lo = (t - hi.astype(F32)).astype(BF)
    b = b_ref[...]
    return jnp.dot(hi, b, preferred_element_type=F32) + jnp.dot(lo, b, preferred_element_type=F32)


def _partner(t):
    w = t.shape[-1]
    lane = lax.broadcasted_iota(jnp.int32, t.shape, 1)
    first = (lane & 32) == 0
    return jnp.where(first, pltpu.roll(t, w - 32, 1), pltpu.roll(t, 32, 1))


def _norm_rope_fwd(t, w_t, cos, sin, b_ref):
    r = lax.rsqrt(_group_mean(t * t, b_ref) + EPS)
    tn = t * r * w_t
    return tn * cos + _partner(tn) * sin


def _norm_rope_bwd(d_out, t, w_t, cos, sin, b_ref):
    d_tn = d_out * cos + _partner(d_out * sin)
    r = lax.rsqrt(_group_mean(t * t, b_ref) + EPS)
    th = t * r
    g_w = jnp.sum(d_tn * th, axis=0, keepdims=True)
    d_th = d_tn * w_t
    d_t = r * (d_th - th * _group_mean(d_th * th, b_ref))
    return d_t, g_w


def _mesh_pos():
    return lax.axis_index("x"), lax.axis_index("y"), lax.axis_index("c")


def _ag_copy(chan, k, block, to):
    blk, send_sems, recv_sems = chan
    ref = blk(*block)
    return pltpu.make_async_remote_copy(src_ref=ref, dst_ref=ref, send_sem=send_sems.at[k],
                                        recv_sem=recv_sems.at[k], device_id=to, device_id_type=MESH)


def _ag_start(chan, pos):
    x, y, c = pos
    me = (x, y, c)
    chips = [(1 - x, y), (x, 1 - y), (1 - x, 1 - y)]
    first = [_ag_copy(chan, 0, me, (x, y, 1 - c))]
    first += [_ag_copy(chan, 1 + j, me, (*chip, c)) for j, chip in enumerate(chips)]
    for cp in first:
        cp.start()
    return first


def _ag_finish(chan, pos, first):
    x, y, c = pos
    me = (x, y, c)
    sibling = (x, y, 1 - c)
    chips = [(1 - x, y), (x, 1 - y), (1 - x, 1 - y)]
    passed = [_ag_copy(chan, 4 + j, (*chip, c), sibling) for j, chip in enumerate(chips)]
    for j, chip in enumerate(chips):
        _ag_copy(chan, 1 + j, (*chip, c), me).wait_recv()
        passed[j].start()
    _ag_copy(chan, 0, sibling, me).wait_recv()
    for j, chip in enumerate(chips):
        _ag_copy(chan, 4 + j, (*chip, 1 - c), me).wait_recv()
    for cp in first + passed:
        cp.wait_send()


def _slab(buf):
    return lambda px, py, pc: buf.at[4 * px + 2 * py + pc]


def _row_block(buf, rows, align):
    return lambda px, py, pc: buf.at[pl.ds(pl.multiple_of((4 * px + 2 * py + pc) * rows, align), rows), :]


HALF = INW // 2


def _gather_fwd_call(w_in_t, w_out_s, conv_w_s, c, w_ada_s, b_ada, x2, norm_w, qw_t, kw_t, cos_t, sin_t, bq, bk):
    def body(win_ref, wout_ref, cw_ref, c_ref, wada_ref, bada_ref, x_ref, nw_ref, qw_ref, kw_ref, cos_ref, sin_ref,
             bq_ref, bk_ref,
             wtf_hbm, woutf_hbm, cwf_ref, mod_ref, cact_ref,
             h_ref, qraw_ref, kraw_ref, ga_ref, a_ref, g_ref, gb_ref, qr_ref, kr_ref, vb_ref, z_ref,
             cw_buf, ca_buf, mp_buf, h_s, raw0, pt, wtf_ref, woutf_ref,
             s0, r0, s1, r1, s2, r2, s3, r3, s4, r4, out_sems):
        s = pl.program_id(0)
        pos = _mesh_pos()
        x, y, cc = pos
        me3 = (x, y, cc)
        me = 4 * x + 2 * y + cc
        sibling = (x, y, 1 - cc)
        chips = [(1 - x, y), (x, 1 - y), (1 - x, 1 - y)]
        ch_win = (_row_block(wtf_ref, IN_SHARD, 16), s0, r0)
        ch_wout = (_row_block(woutf_ref, OUT_SHARD, 16), s1, r1)
        ch_cw = (_slab(cw_buf), s2, r2)
        ch_ca = (_slab(ca_buf), s3, r3)
        ch_mp = (_slab(mp_buf), s4, r4)

        def first(chan, j):
            return _ag_copy(chan, j, me3, sibling if j == 0 else (*chips[j - 1], cc))

        def passed(chan, j):
            return _ag_copy(chan, 4 + j, (*chips[j], cc), sibling)

        def landed(chan, j):
            return _ag_copy(chan, 1 + j, (*chips[j], cc), me3)

        def relayed(chan, j):
            return _ag_copy(chan, 4 + j, (*chips[j], 1 - cc), me3)

        def from_sibling(chan):
            return _ag_copy(chan, 0, sibling, me3)

        @pl.when(s == 0)
        def _():
            cv = c_ref[...]
            ca_buf[me] = jnp.broadcast_to(cv * _sigmoid(cv), (8, D))
            f_ca = _ag_start(ch_ca, pos)
            wtf_ref[pl.ds(pl.multiple_of(me * IN_SHARD, 16), IN_SHARD), :] = win_ref[...].astype(BF)
            for j in range(3):
                first(ch_win, j).start()
            cw_buf[me] = cw_ref[:, 0, :]
            f_cw = _ag_start(ch_cw, pos)

            _ag_finish(ch_ca, pos, f_ca)
            cact_all = jnp.concatenate([ca_buf[d, 0:1, :] for d in range(NDEV)], axis=0)
            cact_ref[...] = cact_all
            col0 = pl.multiple_of(me * ADA_SHARD, 128)
            mp_buf[me] = jnp.dot(cact_all, wada_ref[...], preferred_element_type=F32,
                                 precision=lax.Precision.HIGHEST) + bada_ref[:, pl.ds(col0, ADA_SHARD)]
            f_mp = _ag_start(ch_mp, pos)
            _ag_finish(ch_cw, pos, f_cw)
            _ag_finish(ch_mp, pos, f_mp)
            for d in range(NDEV):
                cwf_ref[0:CK, CONV_SHARD * d:CONV_SHARD * (d + 1)] = cw_buf[d]
            cwf_ref[CK:CKP, :] = jnp.zeros((CKP - CK, CW), F32)
            mod_ref[...] = jnp.concatenate([mp_buf[d, pl.ds(me, 1), :] for d in range(NDEV)], axis=1)

            for j in (1, 0):
                landed(ch_win, j).wait_recv()
                passed(ch_win, j).start()
            from_sibling(ch_win).wait_recv()
            relayed(ch_win, 1).wait_recv()
            first(ch_win, 1).wait_send()
            first(ch_win, 2).wait_send()
            first(ch_win, 3).start()
            wout = wout_ref[...].astype(BF)
            woutf_ref[pl.ds(pl.multiple_of(me * OUT_SHARD, 16), OUT_SHARD), :] = wout
            for j in range(4):
                first(ch_wout, j).start()

        row0 = pl.multiple_of((s % NT) * TS, TS)

        @pl.when(s < NT)
        def _():
            xv = x_ref[...]
            shift = mod_ref[:, 0:D]
            scale = mod_ref[:, D:2 * D]
            r = lax.rsqrt(jnp.mean(xv * xv, axis=-1, keepdims=True) + EPS)
            hb = ((xv * r * nw_ref[...]) * (1.0 + scale) + shift).astype(BF)
            h_s[pl.ds(row0, TS), :] = hb
            w_half = wtf_ref[pl.ds(pl.multiple_of(x * HALF, 16), HALF), :]
            raw0[pl.ds(row0, TS), :] = lax.dot_general(hb, w_half, (((1,), (1,)), ((), ())),
                                                       preferred_element_type=F32)

        @pl.when(s == NT)
        def _():
            relayed(ch_win, 0).wait_recv()
            landed(ch_win, 2).wait_recv()
            passed(ch_win, 2).start()
            relayed(ch_win, 2).wait_recv()
            pltpu.make_async_copy(wtf_ref, wtf_hbm, out_sems.at[0]).start()

        @pl.when(s >= NT)
        def _():
            hb = h_s[pl.ds(row0, TS), :]
            h_ref[...] = hb
            w_half = wtf_ref[pl.ds(pl.multiple_of((1 - x) * HALF, 16), HALF), :]
            raw1 = lax.dot_general(hb, w_half, (((1,), (1,)), ((), ())), preferred_element_type=F32)
            pt[:, pl.ds(pl.multiple_of(x * HALF, 128), HALF)] = raw0[pl.ds(row0, TS), :]
            pt[:, pl.ds(pl.multiple_of((1 - x) * HALF, 128), HALF)] = raw1
            cos = cos_ref[...]
            sin = sin_ref[...]
            q = pt[:, 0:512]
            qraw_ref[...] = q
            qr_ref[...] = _norm_rope_fwd(q, jnp.tile(qw_ref[...], (1, NQ)), jnp.tile(cos, (1, 4)),
                                         jnp.tile(sin, (1, 4)), bq_ref).astype(BF)
            k = pt[:, 512:640]
            kraw_ref[...] = k
            kr_ref[...] = _norm_rope_fwd(k, jnp.tile(kw_ref[...], (1, NKV)), cos, sin, bk_ref).astype(BF)
            vb_ref[...] = pt[:, 640:768].astype(BF)
            ga_ref[...] = pt[:, 768:1280]
            a = pt[:, 1280:1792]
            g = pt[:, 1792:2304]
            a_ref[...] = a
            g_ref[...] = g
            z_ref[...] = a * _sigmoid(g)
            gb_ref[...] = pt[:, 2304:2816]

        @pl.when(s == 2 * NT - 1)
        def _():
            for j in range(3):
                landed(ch_wout, j).wait_recv()
                passed(ch_wout, j).start()
            from_sibling(ch_wout).wait_recv()
            for j in range(3):
                relayed(ch_wout, j).wait_recv()
            out_copy = pltpu.make_async_copy(woutf_ref, woutf_hbm, out_sems.at[1])
            out_copy.start()
            pltpu.make_async_copy(wtf_ref, wtf_hbm, out_sems.at[0]).wait()
            out_copy.wait()
            first(ch_win, 0).wait_send()
            first(ch_win, 3).wait_send()
            for j in range(3):
                passed(ch_win, j).wait_send()
                passed(ch_wout, j).wait_send()
            for j in range(4):
                first(ch_wout, j).wait_send()

    early = lambda i: (jnp.minimum(i, NT - 1), 0)
    late = lambda i: (jnp.maximum(i - NT, 0), 0)
    t512 = pl.BlockSpec((TS, 512), late)
    t128 = pl.BlockSpec((TS, 128), late)
    sem = pltpu.SemaphoreType.DMA((7,))
    sds = jax.ShapeDtypeStruct
    return pl.pallas_call(
        body, name="gather_fwd", grid=(2 * NT,),
        out_shape=[sds((INW, D), BF), sds((D, D), BF), sds((CKP, CW), F32), sds((1, 3 * D), F32), sds((NDEV, D), F32),
                   sds((S, D), BF), sds((S, AW), F32), sds((S, KVW), F32), sds((S, AW), F32), sds((S, CW), F32),
                   sds((S, CW), F32), sds((S, CW), F32), sds((S, AW), BF), sds((S, KVW), BF), sds((S, KVW), BF),
                   sds((S, CW), F32)],
        in_specs=[_VMEM] * 6 + [pl.BlockSpec((TS, D), early), pl.BlockSpec((1, D), _const),
                                pl.BlockSpec((1, HD), _const), pl.BlockSpec((1, HD), _const), t128, t128,
                                pl.BlockSpec((AW, AW), _const), pl.BlockSpec((KVW, KVW), _const)],
        out_specs=[_ANY, _ANY] + [_VMEM] * 3 + [pl.BlockSpec((TS, D), late), t512, t128, t512, t512, t512, t512, t512,
                                                t128, t128, t512],
        scratch_shapes=[pltpu.VMEM((NDEV, CK, CONV_SHARD), F32), pltpu.VMEM((NDEV, 8, D), F32),
                        pltpu.VMEM((NDEV, 8, ADA_SHARD), F32), pltpu.VMEM((S, D), BF), pltpu.VMEM((S, HALF), F32),
                        pltpu.VMEM((TS, INW), F32), pltpu.VMEM((INW, D), BF), pltpu.VMEM((D, D), BF)]
        + [sem] * 10 + [pltpu.SemaphoreType.DMA((2,))],
        compiler_params=_params(True),
    )(w_in_t, w_out_s, conv_w_s, c, w_ada_s, b_ada, x2, norm_w, qw_t, kw_t, cos_t, sin_t, bq, bk)


QB = 2
NQB = NB // QB


def _band_masks(has_prev):
    kj = lax.broadcasted_iota(jnp.int32, (2 * BLK, 4 * BLK), 0)
    qi = lax.broadcasted_iota(jnp.int32, (2 * BLK, 4 * BLK), 1) & (BLK - 1)
    dist = qi + BLK - kj
    local = (dist >= 0) & (dist < BLK)
    return local & ((kj >= BLK) | has_prev), local


def _key_blocks(sb, prev_ref, cur_ref):
    prev = prev_ref[...] if sb == 0 else cur_ref[BLK * (sb - 1):BLK * sb, :]
    return prev, cur_ref[BLK * sb:BLK * (sb + 1), :]


def _sink_lanes(sink_ref, g):
    lane = lax.broadcasted_iota(jnp.int32, (1, 4 * BLK), 1)
    return jnp.where(lane < BLK, sink_ref[0, 4 * g],
                     jnp.where(lane < 2 * BLK, sink_ref[0, 4 * g + 1],
                               jnp.where(lane < 3 * BLK, sink_ref[0, 4 * g + 2], sink_ref[0, 4 * g + 3])))


def _unstack_t(t):
    return [t[:, BLK * h:BLK * (h + 1)].T for h in range(4)]


def _stack_heads(t, g):
    return jnp.concatenate([t[:, HD * (4 * g + h):HD * (4 * g + h + 1)] for h in range(4)], axis=0)


def _band(prev, cur, g):
    return jnp.concatenate([prev[:, HD * g:HD * (g + 1)], cur[:, HD * g:HD * (g + 1)]], axis=0)


def _softmax_band(qs, kb, mask, sink):
    s = lax.dot_general(kb, qs, (((1,), (1,)), ((), ())), preferred_element_type=F32) * (HD ** -0.5)
    s = jnp.where(mask, s, NEG)
    m = jnp.maximum(jnp.max(s, axis=0, keepdims=True), sink)
    e = jnp.exp(s - m)
    es = jnp.exp(sink - m)
    inv = 1.0 / (jnp.sum(e, axis=0, keepdims=True) + es)
    return e * inv, es * inv


HALO = 32


RC = 64
LC = 128


def _windows(ext_ref, r0, l0, base):
    col = ext_ref[pl.ds(r0, RC + HALO), pl.ds(l0, LC)]
    for s in range(8):
        rolled = col if s == 0 else pltpu.roll(col, RC + HALO - s, 0)
        for t in range(CK):
            if (base + t) % 8 == s:
                a8 = base + t - s
                yield t, rolled[a8:a8 + RC]


def _taps(ext_ref, r0, l0, base, cw_ref, flip):
    acc = None
    for t, win in _windows(ext_ref, r0, l0, base):
        k = CK - 1 - t if flip else t
        term = win * cw_ref[k:k + 1, pl.ds(l0, LC)]
        acc = term if acc is None else acc + term
    return acc


NSUB = (BLK // RC) * (CW // LC)


def _sub_tile(t, row_base):
    r0 = pl.multiple_of(row_base + (t // (CW // LC)) * RC, RC)
    l0 = pl.multiple_of((t % (CW // LC)) * LC, LC)
    return r0, l0


FR = QB * BLK


def _fwd_tail_call(sinks, qr, kr, vb, z, gb, cwf, conv_b, ln_w, ln_b, ga, x2, tgt, mod, w_out_full):
    def body(sink_ref, q_ref, kp_ref, kc_ref, vp_ref, vc_ref, z_ref, zh_ref, gb_ref, cw_ref, cb_ref, lw_ref, lb_ref,
             ga_ref, x_ref, t_ref, mod_ref, w_ref,
             o_ref, zc_ref, dout_ref, dya_ref, dyb_ref, gw_ref, loss_ref, dgate_ref,
             zext, yb_ref):
        i = pl.program_id(0)

        @pl.when(i == 0)
        def _():
            gw_ref[...] = jnp.zeros((D, D), F32)
            loss_ref[...] = jnp.zeros((1, 128), F32)
            dgate_ref[...] = jnp.zeros((1, D), F32)

        def out_proj(rows):
            gav = ga_ref[rows, :]
            ya = o_ref[rows, :] * (gav * _sigmoid(gav))
            ycat = jnp.concatenate([ya.astype(BF), yb_ref[rows, :]], axis=1)
            w = w_ref[...]
            y = jnp.dot(ycat, w, preferred_element_type=F32)
            gate = mod_ref[:, 2 * D:3 * D]
            diff = x_ref[rows, :] + gate * y - t_ref[rows, :]
            sq = jnp.sum(jnp.sum(diff * diff, axis=1, keepdims=True), axis=0, keepdims=True)
            loss_ref[...] += jnp.broadcast_to(sq, (1, 128))
            d_out = diff * (1.0 / D)
            dout_ref[rows, :] = d_out
            dgate_ref[...] += jnp.sum(d_out * y, axis=0, keepdims=True)
            dy = (d_out * gate).astype(BF)
            d_ycat = lax.dot_general(dy, w, (((1,), (1,)), ((), ())), preferred_element_type=F32)
            dya_ref[rows, :] = d_ycat[:, 0:AW]
            dyb_ref[rows, :] = d_ycat[:, AW:D]
            gw_ref[...] += lax.dot_general(ycat, dy, (((0,), (0,)), ((), ())), preferred_element_type=F32)

        zext[0:HALO, :] = jnp.where(i > 0, zh_ref[...], 0.0)
        zext[HALO:HALO + FR, :] = z_ref[...]
        masks = _band_masks(i > 0)
        for sb in range(QB):
            rows = slice(BLK * sb, BLK * (sb + 1))
            mask = masks[min(sb, 1)]
            q = q_ref[rows, :]
            kp, kc = _key_blocks(sb, kp_ref, kc_ref)
            vp, vc = _key_blocks(sb, vp_ref, vc_ref)
            for g in range(NKV):
                p, _ = _softmax_band(_stack_heads(q, g), _band(kp, kc, g), mask, _sink_lanes(sink_ref, g))
                o_t = lax.dot_general(_band(vp, vc, g), p.astype(BF), (((0,), (0,)), ((), ())),
                                      preferred_element_type=F32)
                for h, o_h in enumerate(_unstack_t(o_t)):
                    o_ref[rows, HD * (4 * g + h):HD * (4 * g + h + 1)] = o_h
            for r0 in range(BLK * sb, BLK * (sb + 1), RC):
                for l0 in range(0, CW, LC):
                    acc = _taps(zext, r0, l0, HALO - (CK - 1), cw_ref, False)
                    zc_ref[r0:r0 + RC, l0:l0 + LC] = acc + cb_ref[:, l0:l0 + LC]
            zc = zc_ref[rows, :]
            mu = jnp.mean(zc, axis=-1, keepdims=True)
            dz = zc - mu
            rstd = lax.rsqrt(jnp.mean(dz * dz, axis=-1, keepdims=True) + EPS)
            zn = dz * rstd * lw_ref[...] + lb_ref[...]
            gbv = gb_ref[rows, :]
            yb_ref[rows, :] = ((zn * _sigmoid(zn)) * (gbv * _sigmoid(gbv))).astype(BF)
            if (sb + 1) * BLK % TS == 0:
                out_proj(slice((sb + 1) * BLK - TS, (sb + 1) * BLK))

    prev = lambda i: (jnp.maximum(QB * i - 1, 0), 0)
    halo = lambda i: (jnp.maximum(FR // HALO * i - 1, 0), 0)
    f512 = pl.BlockSpec((FR, 512), _row)
    f128 = pl.BlockSpec((FR, KVW), _row)
    f1024 = pl.BlockSpec((FR, D), _row)
    c512 = pl.BlockSpec((1, CW), _const)
    sds = jax.ShapeDtypeStruct
    return pl.pallas_call(
        body, name="fwd_tail", grid=(NQB,),
        out_shape=[sds((S, AW), F32), sds((S, CW), F32), sds((S, D), F32), sds((S, AW), F32), sds((S, CW), F32),
                   sds((D, D), F32), sds((1, 128), F32), sds((1, D), F32)],
        in_specs=[_SMEM, f512, pl.BlockSpec((BLK, KVW), prev), f128, pl.BlockSpec((BLK, KVW), prev), f128,
                  f512, pl.BlockSpec((HALO, CW), halo), f512, pl.BlockSpec((CKP, CW), _const), c512, c512, c512,
                  f512, f1024, f1024, pl.BlockSpec((1, 3 * D), _const),
                  pl.BlockSpec((D, D), _const, pipeline_mode=pl.Buffered(1))],
        out_specs=[f512, f512, f1024, f512, f512, _VMEM, _VMEM, _VMEM],
        scratch_shapes=[pltpu.VMEM((FR + HALO, CW), F32), pltpu.VMEM((FR, CW), BF)],
        compiler_params=_params(True),
    )(sinks, qr, kr, kr, vb, vb, z, z, gb, cwf, conv_b, ln_w, ln_b, ga, x2, tgt, mod, w_out_full)


def _ln_gate_bwd(d_yb, zc, gbv, lw, lb):
    mu = jnp.mean(zc, axis=-1, keepdims=True)
    dz = zc - mu
    rstd = lax.rsqrt(jnp.mean(dz * dz, axis=-1, keepdims=True) + EPS)
    zh = dz * rstd
    zn = zh * lw + lb
    act_n, dact_n = _silu_and_grad(zn)
    act_g, dact_g = _silu_and_grad(gbv)
    d_gb = d_yb * act_n * dact_g
    d_zn = d_yb * act_g * dact_n
    dzh = d_zn * lw
    d_zc = rstd * (dzh - jnp.mean(dzh, axis=-1, keepdims=True) - zh * jnp.mean(dzh * zh, axis=-1, keepdims=True))
    return d_zc, d_gb, d_zn, zh


def _attn_conv_bwd_call(sinks, qr, kr, vb, d_ya, ga, o, qraw, qw_t, cos_t, sin_t, bq, d_yb, zc, gb, z, a, g, cwf,
                        ln_w, ln_b):
    def body(sink_ref, q_ref, kp_ref, kc_ref, vp_ref, vc_ref, dya_ref, ga_ref, o_ref, qraw_ref, qw_ref,
             cos_ref, sin_ref, bq_ref,
             dyb_ref, dybn_ref, zc_ref, zcn_ref, gb_ref, gbn_ref, z_ref, zh_ref, a_ref, g_ref, cw_ref, lw_ref, lb_ref,
             dqraw_ref, dga_ref, dk_ref, dv_ref, gqw_ref, gsink_ref,
             dgb_ref, da_ref, dg_ref, gcw_ref, glw_ref, glb_ref, gcb_ref,
             dext, zext, gacc):
        i = pl.program_id(0)

        @pl.when(i == 0)
        def _():
            dk_ref[...] = jnp.zeros((S, KVW), F32)
            dv_ref[...] = jnp.zeros((S, KVW), F32)
            gqw_ref[...] = jnp.zeros((1, AW), F32)
            gsink_ref[...] = jnp.zeros((1, 128), F32)
            gacc[...] = jnp.zeros((CKP * 8, CW), F32)
            glw_ref[...] = jnp.zeros((1, CW), F32)
            glb_ref[...] = jnp.zeros((1, CW), F32)
            gcb_ref[...] = jnp.zeros((1, CW), F32)

        lw = lw_ref[...]
        lb = lb_ref[...]

        def ln_rows(rows):
            d_zc, d_gb, d_zn, zh = _ln_gate_bwd(dyb_ref[rows, :], zc_ref[rows, :], gb_ref[rows, :], lw, lb)
            dgb_ref[rows, :] = d_gb.astype(BF)
            glw_ref[...] += jnp.sum(d_zn * zh, axis=0, keepdims=True)
            glb_ref[...] += jnp.sum(d_zn, axis=0, keepdims=True)
            gcb_ref[...] += jnp.sum(d_zc, axis=0, keepdims=True)
            dext[rows, :] = d_zc

        ln_rows(slice(0, BLK))
        zext[0:HALO, :] = jnp.where(i > 0, zh_ref[...], 0.0)
        zext[HALO:HALO + FR, :] = z_ref[...]

        lane = lax.broadcasted_iota(jnp.int32, (1, 128), 1)
        gsink = jnp.zeros((1, 128), F32)
        dq_rows = []
        masks = _band_masks(i > 0)
        for sb in range(QB):
            rows = slice(BLK * sb, BLK * (sb + 1))
            if sb + 1 < QB:
                ln_rows(slice(BLK * (sb + 1), BLK * (sb + 2)))
            else:
                d_zc_next, _, _, _ = _ln_gate_bwd(dybn_ref[...], zcn_ref[...], gbn_ref[...], lw, lb)
                dext[FR:FR + HALO, :] = jnp.where(i < NQB - 1, d_zc_next, 0.0)
            mask = masks[min(sb, 1)]
            q = q_ref[rows, :]
            d_ya = dya_ref[rows, :]
            act, dact = _silu_and_grad(ga_ref[rows, :])
            dga_ref[rows, :] = (d_ya * o_ref[rows, :] * dact).astype(BF)
            d_o = (d_ya * act).astype(BF)
            kp, kc = _key_blocks(sb, kp_ref, kc_ref)
            vp, vc = _key_blocks(sb, vp_ref, vc_ref)
            dq_parts, dk_parts, dv_parts = [], [], []
            for gi in range(NKV):
                qs = _stack_heads(q, gi)
                kb = _band(kp, kc, gi)
                vbd = _band(vp, vc, gi)
                p, ps = _softmax_band(qs, kb, mask, _sink_lanes(sink_ref, gi))
                dos = _stack_heads(d_o, gi)
                dp = lax.dot_general(vbd, dos, (((1,), (1,)), ((), ())), preferred_element_type=F32)
                dr = jnp.sum(p * dp, axis=0, keepdims=True)
                ds = (p * (dp - dr) * (HD ** -0.5)).astype(BF)
                sink_term = ps * dr
                for h in range(4):
                    part = jnp.sum(sink_term[:, BLK * h:BLK * (h + 1)], axis=1, keepdims=True)
                    gsink = gsink - jnp.where(lane == 4 * gi + h, part, 0.0)
                dv_parts.append(jnp.dot(p.astype(BF), dos, preferred_element_type=F32))
                dk_parts.append(jnp.dot(ds, qs, preferred_element_type=F32))
                dq_t = lax.dot_general(kb, ds, (((0,), (0,)), ((), ())), preferred_element_type=F32)
                dq_parts.extend(_unstack_t(dq_t))
            dkb = jnp.concatenate(dk_parts, axis=1)
            dvb = jnp.concatenate(dv_parts, axis=1)
            blk = QB * i + sb
            r_prev = pl.multiple_of(jnp.maximum(blk - 1, 0) * BLK, BLK)
            r_cur = pl.multiple_of(blk * BLK, BLK)
            dk_ref[pl.ds(r_prev, BLK), :] += dkb[0:BLK]
            dv_ref[pl.ds(r_prev, BLK), :] += dvb[0:BLK]
            dk_ref[pl.ds(r_cur, BLK), :] += dkb[BLK:2 * BLK]
            dv_ref[pl.ds(r_cur, BLK), :] += dvb[BLK:2 * BLK]
            dq_rows.append(jnp.concatenate(dq_parts, axis=1))

            def taps_sub(t, carry, sb=sb):
                r0, l0 = _sub_tile(t, BLK * sb)
                here = (pl.ds(r0, RC), pl.ds(l0, LC))
                d_z = _taps(dext, r0, l0, 0, cw_ref, True)
                sg = _sigmoid(g_ref[here])
                da_ref[here] = (d_z * sg).astype(BF)
                dg_ref[here] = (d_z * a_ref[here] * sg * (1.0 - sg)).astype(BF)
                d_sub = dext[here]
                for k, win in _windows(zext, r0, l0, HALO - (CK - 1)):
                    prod = d_sub * win
                    part = prod[0:8]
                    for q8 in range(1, RC // 8):
                        part = part + prod[8 * q8:8 * q8 + 8]
                    gacc[8 * k:8 * k + 8, pl.ds(l0, LC)] += part
                return carry

            lax.fori_loop(0, NSUB, taps_sub, 0)
        gsink_ref[...] += gsink
        dq = jnp.concatenate(dq_rows, axis=0)
        dq_raw, g_qw = _norm_rope_bwd(dq, qraw_ref[...], jnp.tile(qw_ref[...], (1, NQ)),
                                      jnp.tile(cos_ref[...], (1, 4)), jnp.tile(sin_ref[...], (1, 4)), bq_ref)
        dqraw_ref[...] = dq_raw.astype(BF)
        gqw_ref[...] += g_qw

        @pl.when(i == NQB - 1)
        def _():
            for k in range(CK):
                gcw_ref[k:k + 1, :] = jnp.sum(gacc[8 * k:8 * k + 8, :], axis=0, keepdims=True)
            gcw_ref[CK:CKP, :] = jnp.zeros((CKP - CK, CW), F32)

    prev = lambda i: (jnp.maximum(QB * i - 1, 0), 0)
    halo_prev = lambda i: (jnp.maximum(FR // HALO * i - 1, 0), 0)
    halo_next = lambda i: (jnp.minimum(FR // HALO * (i + 1), S // HALO - 1), 0)
    f512 = pl.BlockSpec((FR, 512), _row)
    f128 = pl.BlockSpec((FR, 128), _row)
    hn = pl.BlockSpec((HALO, CW), halo_next)
    c512 = pl.BlockSpec((1, CW), _const)
    sds = jax.ShapeDtypeStruct
    vec = sds((1, CW), F32)
    return pl.pallas_call(
        body, name="attn_conv_bwd", grid=(NQB,),
        out_shape=[sds((S, AW), BF), sds((S, AW), BF), sds((S, KVW), F32), sds((S, KVW), F32), sds((1, AW), F32),
                   sds((1, 128), F32),
                   sds((S, CW), BF), sds((S, CW), BF), sds((S, CW), BF), sds((CKP, CW), F32), vec, vec, vec],
        in_specs=[_SMEM, f512, pl.BlockSpec((BLK, KVW), prev), f128, pl.BlockSpec((BLK, KVW), prev), f128,
                  f512, f512, f512, f512, pl.BlockSpec((1, HD), _const), f128, f128, pl.BlockSpec((AW, AW), _const),
                  f512, hn, f512, hn, f512, hn, f512, pl.BlockSpec((HALO, CW), halo_prev), f512, f512,
                  pl.BlockSpec((CKP, CW), _const), c512, c512],
        out_specs=[f512, f512, _VMEM, _VMEM, _VMEM, _VMEM, f512, f512, f512, _VMEM, _VMEM, _VMEM, _VMEM],
        scratch_shapes=[pltpu.VMEM((FR + HALO, CW), F32), pltpu.VMEM((FR + HALO, CW), F32),
                        pltpu.VMEM((CKP * 8, CW), F32)],
        compiler_params=_params(True),
    )(sinks, qr, kr, kr, vb, vb, d_ya, ga, o, qraw, qw_t, cos_t, sin_t, bq,
      d_yb, d_yb, zc, zc, gb, gb, z, z, a, g, cwf, ln_w, ln_b)


SM_ROWS = 8
PIECES = ((0, 512), (512, 640), (640, 768), (768, 1280), (1280, 1792), (1792, 2304), (2304, 2816))


def _bwd_in_call(dqraw, dk, dv, dga, da, dg, dgb, kraw, kw_t, cos_t, sin_t, bk, h, wt_full, x2, d_out, mod, norm_w,
                 gw_out, gcw, glw, glb, gcb, gqw, gsink, dgate, loss_p, cact_all):
    def body(dq_ref, dk_ref, dv_ref, dga_ref, da_ref, dg_ref, dgb_ref, kraw_ref, kw_ref, cos_ref, sin_ref, bk_ref,
             h_ref, wt_ref, x_ref, dout_ref, mod_ref, nw_ref, gwout_ref, gcw_ref, glw_ref, glb_ref, gcb_ref, gqw_ref,
             gsink_ref, dgate_ref, loss_ref, cact_ref,
             gx_ref, o_gwin, o_gwout, o_gwada, o_gbada, o_gnw, o_gqw, o_gkw, o_gsink, o_gcw, o_gcb, o_glw, o_glb,
             o_loss,
             acc, win_send, win_sib, win_ici, wout_send, wout_sib, wout_ici, sm_buf, cw_buf, dmod_all, vec_acc, gkw_acc,
             wi_ds, wi_dr, wi_is, wi_ir, wo_ds, wo_dr, wo_is, wo_ir, sm_s, sm_r, cw_s, cw_r):
        i = pl.program_id(0)
        pos = _mesh_pos()
        x, y, cc = pos
        me = 4 * x + 2 * y + cc

        def chip(j):
            return (1 - x if j & 1 else x, 1 - y if j & 2 else y)

        def rows_of(buf, px, py, pc, rows, align):
            return buf.at[pl.ds(pl.multiple_of((4 * px + 2 * py + pc) * rows, align), rows), :]

        bufs = {"in": (win_send, win_sib, win_ici, IN_SHARD, wi_ds, wi_dr, wi_is, wi_ir),
                "out": (wout_send, wout_sib, wout_ici, OUT_SHARD, wo_ds, wo_dr, wo_is, wo_ir)}

        def d2d_copy(j, which):
            send, sib, _, rows, ds_, dr_, _, _ = bufs[which]
            px, py = chip(j)
            return pltpu.make_async_remote_copy(src_ref=rows_of(send, px, py, 1 - cc, rows, 16), dst_ref=sib.at[j],
                                                send_sem=ds_.at[j], recv_sem=dr_.at[j], device_id=(x, y, 1 - cc),
                                                device_id_type=MESH)

        def ici_copy(j, which):
            send, _, ici, rows, _, _, is_, ir_ = bufs[which]
            px, py = chip(j)
            return pltpu.make_async_remote_copy(src_ref=rows_of(send, px, py, cc, rows, 16), dst_ref=ici.at[j - 1],
                                                send_sem=is_.at[j - 1], recv_sem=ir_.at[j - 1], device_id=(px, py, cc),
                                                device_id_type=MESH)

        def level2(which, partial_ref):
            send, sib, _, rows, _, _, _, _ = bufs[which]
            for j in range(1, 4):
                d2d_copy(j, which).wait_recv()
                px, py = chip(j)
                mine = rows_of(partial_ref, px, py, cc, rows, 8)[...]
                rows_of(send, px, py, cc, rows, 16)[...] = (mine + sib[j].astype(F32)).astype(BF)
                ici_copy(j, which).start()

        def finish(which, partial_ref):
            _, sib, ici, rows, _, _, _, _ = bufs[which]
            d2d_copy(0, which).wait_recv()
            total = rows_of(partial_ref, x, y, cc, rows, 8)[...] + sib[0].astype(F32)
            for j in range(1, 4):
                ici_copy(j, which).wait_recv()
                total = total + ici[j - 1].astype(F32)
            for j in range(4):
                d2d_copy(j, which).wait_send()
            for j in range(1, 4):
                ici_copy(j, which).wait_send()
            return total

        def dproj_pieces():
            dk_raw, g_kw = _norm_rope_bwd(dk_ref[...], kraw_ref[...], jnp.tile(kw_ref[...], (1, NKV)), cos_ref[...],
                                          sin_ref[...], bk_ref)
            return [dq_ref[...], dk_raw.astype(BF), dv_ref[...].astype(BF), dga_ref[...], da_ref[...], dg_ref[...],
                    dgb_ref[...]], g_kw

        @pl.when(i == 0)
        def _():
            acc[...] = jnp.zeros((INW, D), F32)
            vec_acc[...] = jnp.zeros((8, D), F32)
            gkw_acc[...] = jnp.zeros((1, KVW), F32)
            wout_send[...] = gwout_ref[...].astype(BF)
            for j in range(4):
                d2d_copy(j, "out").start()
            cw_buf[me] = gcw_ref[...]
            _ag_start((_slab(cw_buf), cw_s, cw_r), pos)

        @pl.when(i == 2)
        def _():
            level2("out", gwout_ref)

        @pl.when(i < NT)
        def _():
            pieces, g_kw = dproj_pieces()
            gkw_acc[...] += g_kw
            hv = h_ref[...]
            for (lo, hi), piece in zip(PIECES, pieces):
                acc[lo:hi, :] += lax.dot_general(piece, hv, (((0,), (0,)), ((), ())), preferred_element_type=F32)

        @pl.when(i == NT - 1)
        def _():
            for lo, hi in PIECES:
                win_send[lo:hi, :] = acc[lo:hi, :].astype(BF)
            for j in range(4):
                d2d_copy(j, "in").start()

        @pl.when(i == NT + 1)
        def _():
            level2("in", acc)

        @pl.when(i >= NT)
        def _():
            pieces, _ = dproj_pieces()
            dproj = jnp.concatenate(pieces, axis=1)
            d_h = jnp.dot(dproj, wt_ref[...], preferred_element_type=F32)
            xv = x_ref[...]
            scale = mod_ref[:, D:2 * D]
            nw = nw_ref[...]
            r = lax.rsqrt(jnp.mean(xv * xv, axis=-1, keepdims=True) + EPS)
            xn = xv * r
            vec_acc[0:1, :] += jnp.sum(d_h, axis=0, keepdims=True)
            vec_acc[1:2, :] += jnp.sum(d_h * (xn * nw), axis=0, keepdims=True)
            d_u = d_h * (1.0 + scale)
            vec_acc[2:3, :] += jnp.sum(d_u * xn, axis=0, keepdims=True)
            d_xn = d_u * nw
            gx_ref[...] = dout_ref[...] + r * (d_xn - xn * jnp.mean(d_xn * xn, axis=-1, keepdims=True))

        @pl.when(i == 2 * NT - 1)
        def _():
            ch_cw = (_slab(cw_buf), cw_s, cw_r)
            z128 = jnp.zeros((1, 128), F32)
            row4 = jnp.concatenate([glw_ref[...], glb_ref[...]], axis=1)
            row5 = jnp.concatenate([gcb_ref[...], gqw_ref[...]], axis=1)
            row6 = jnp.concatenate([gkw_acc[...], gsink_ref[...], loss_ref[...]] + [z128] * 5, axis=1)
            sm_buf[me] = jnp.concatenate([vec_acc[0:2, :], dgate_ref[...], vec_acc[2:3, :], row4, row5, row6,
                                          jnp.zeros((1, D), F32)], axis=0)

            def sm_copy(k):
                peer = (1 - x if k & 4 else x, 1 - y if k & 2 else y, 1 - cc if k & 1 else cc)
                return pltpu.make_async_remote_copy(src_ref=sm_buf.at[me], dst_ref=sm_buf.at[me],
                                                    send_sem=sm_s.at[k - 1], recv_sem=sm_r.at[k - 1],
                                                    device_id=peer, device_id_type=MESH)

            for k in range(1, NDEV):
                sm_copy(k).start()
            _ag_finish(ch_cw, pos, [_ag_copy(ch_cw, 0, (x, y, cc), (x, y, 1 - cc))]
                       + [_ag_copy(ch_cw, 1 + j, (x, y, cc), (*chip(j + 1), cc)) for j in range(3)])
            for k in range(1, NDEV):
                sm_copy(k).wait_recv()
            tot = sm_buf[0]
            cw_tot = cw_buf[0]
            for d in range(1, NDEV):
                tot = tot + sm_buf[d]
                cw_tot = cw_tot + cw_buf[d]
            o_gbada[...] = jnp.concatenate([tot[0:1, :], tot[1:2, :], tot[2:3, :]], axis=1)
            o_gnw[...] = tot[3:4, :]
            o_glw[...] = tot[4:5, 0:CW]
            o_glb[...] = tot[4:5, CW:D]
            o_gcb[...] = tot[5:6, 0:CW]
            gq = tot[5:6, CW:CW + HD]
            for hh in range(1, NQ):
                gq = gq + tot[5:6, CW + HD * hh:CW + HD * (hh + 1)]
            o_gqw[...] = gq
            o_gkw[...] = tot[6:7, 0:HD] + tot[6:7, HD:2 * HD]
            o_gsink[...] = tot[6:7, 128:128 + NQ]
            o_loss[...] = tot[6:7, 256:384] * (0.5 / D)
            mine = jnp.zeros((CK, CONV_SHARD), F32)
            for d in range(NDEV):
                mine = mine + jnp.where(me == d, cw_tot[0:CK, CONV_SHARD * d:CONV_SHARD * (d + 1)], 0.0)
            for k in range(CK):
                o_gcw[k] = mine[k:k + 1, :]
            for d in range(NDEV):
                dmod_all[d:d + 1, :] = jnp.concatenate([sm_buf[d, 0:1, :], sm_buf[d, 1:2, :], sm_buf[d, 2:3, :]],
                                                       axis=1)
            col0 = pl.multiple_of(me * ADA_SHARD, 128)
            o_gwada[...] = lax.dot_general(cact_ref[...], dmod_all[:, pl.ds(col0, ADA_SHARD)], (((0,), (0,)), ((), ())),
                                           preferred_element_type=F32, precision=lax.Precision.HIGHEST)

            o_gwout[...] = finish("out", gwout_ref)
            o_gwin[...] = finish("in", acc)
            for k in range(1, NDEV):
                sm_copy(k).wait_send()

    half = lambda i: (i % NT, 0)
    late = lambda i: (jnp.maximum(i - NT, 0), 0)
    t512 = pl.BlockSpec((TS, 512), half)
    t128 = pl.BlockSpec((TS, 128), half)
    l1024 = pl.BlockSpec((TS, D), late)
    sem7 = pltpu.SemaphoreType.DMA((7,))
    sem4 = pltpu.SemaphoreType.DMA((4,))
    sem3 = pltpu.SemaphoreType.DMA((3,))
    sds = jax.ShapeDtypeStruct
    return pl.pallas_call(
        body, name="bwd_in", grid=(2 * NT,),
        out_shape=[sds((S, D), F32), sds((IN_SHARD, D), F32), sds((OUT_SHARD, D), F32), sds((D, ADA_SHARD), F32),
                   sds((1, 3 * D), F32), sds((1, D), F32), sds((1, HD), F32), sds((1, HD), F32), sds((1, NQ), F32),
                   sds((CK, 1, CONV_SHARD), F32), sds((1, CW), F32), sds((1, CW), F32), sds((1, CW), F32),
                   sds((1, 128), F32)],
        in_specs=[t512, t128, t128, t512, t512, t512, t512, t128, pl.BlockSpec((1, HD), _const), t128, t128,
                  pl.BlockSpec((KVW, KVW), _const), pl.BlockSpec((TS, D), half),
                  pl.BlockSpec((INW, D), _const, pipeline_mode=pl.Buffered(1)), l1024, l1024,
                  pl.BlockSpec((1, 3 * D), _const), pl.BlockSpec((1, D), _const)] + [_VMEM] * 10,
        out_specs=[l1024] + [_VMEM] * 13,
        scratch_shapes=[pltpu.VMEM((INW, D), F32), pltpu.VMEM((INW, D), BF), pltpu.VMEM((4, IN_SHARD, D), BF),
                        pltpu.VMEM((3, IN_SHARD, D), BF), pltpu.VMEM((D, D), BF), pltpu.VMEM((4, OUT_SHARD, D), BF),
                        pltpu.VMEM((3, OUT_SHARD, D), BF),
                        pltpu.VMEM((NDEV, SM_ROWS, D), F32), pltpu.VMEM((NDEV, CKP, CW), F32),
                        pltpu.VMEM((NDEV, 3 * D), F32), pltpu.VMEM((8, D), F32), pltpu.VMEM((1, KVW), F32)]
        + [sem4, sem4, sem3, sem3] * 2 + [sem7] * 4,
        compiler_params=pltpu.CompilerParams(dimension_semantics=("arbitrary",), vmem_limit_bytes=BIG_VMEM_LIMIT),
    )(dqraw, dk, dv, dga, da, dg, dgb, kraw, kw_t, cos_t, sin_t, bk, h, wt_full, x2, d_out, mod, norm_w,
      gw_out, gcw, glw, glb, gcb, gqw, gsink, dgate, loss_p, cact_all)


ADAM_STEPS = 4


def _adam_chunking(arr):
    if arr.ndim == 2 and arr.shape[0] % (8 * ADAM_STEPS) == 0:
        return "rows"
    if arr.ndim == 2 and arr.shape[1] % (128 * ADAM_STEPS) == 0:
        return "cols"
    return None


def _adam_call(ws, gs, ms, vs, grad_x, loss_v):
    n = len(ws)
    bc1 = 1.0 - ADAM_B1 ** ADAM_STEP
    bc2 = 1.0 - ADAM_B2 ** ADAM_STEP
    chunked = [_adam_chunking(w) for w in ws]

    def body(*refs):
        ins, outs = refs[:4 * n + 2], refs[4 * n + 2:]
        i = pl.program_id(0)

        def update(j):
            w, g, m, v = (ins[j][...], ins[n + j][...], ins[2 * n + j][...], ins[3 * n + j][...])
            m_new = ADAM_B1 * m + (1.0 - ADAM_B1) * g
            v_new = ADAM_B2 * v + (1.0 - ADAM_B2) * (g * g)
            m_hat = m_new / bc1
            v_hat = v_new / bc2
            outs[j][...] = g
            outs[n + j][...] = -ADAM_LR * (m_hat / (jnp.sqrt(v_hat) + ADAM_EPS) + ADAM_WD * w)
            outs[2 * n + j][...] = m_new
            outs[3 * n + j][...] = v_new

        for j in range(n):
            if chunked[j]:
                update(j)
        outs[4 * n][...] = ins[4 * n][...]

        @pl.when(i == 0)
        def _():
            for j in range(n):
                if not chunked[j]:
                    update(j)
            outs[4 * n + 1][...] = ins[4 * n + 1][...]

    def spec(arr, how):
        if how == "rows":
            return pl.BlockSpec((arr.shape[0] // ADAM_STEPS, arr.shape[1]), _row)
        if how == "cols":
            return pl.BlockSpec((arr.shape[0], arr.shape[1] // ADAM_STEPS), lambda i: (0, i))
        zeros = (0,) * arr.ndim
        return pl.BlockSpec(arr.shape, lambda i: zeros)

    par_specs = [spec(w, ch) for w, ch in zip(ws, chunked)]
    extra = [spec(grad_x, "rows"), spec(loss_v, None)]
    shapes = [jax.ShapeDtypeStruct(w.shape, F32) for w in ws]
    res = pl.pallas_call(
        body, name="adam", grid=(ADAM_STEPS,),
        out_shape=shapes * 4 + [jax.ShapeDtypeStruct(grad_x.shape, F32), jax.ShapeDtypeStruct(loss_v.shape, F32)],
        in_specs=par_specs * 4 + extra, out_specs=par_specs * 4 + extra,
        compiler_params=_params(True),
    )(*ws, *gs, *ms, *vs, grad_x, loss_v)
    return res[0:n], res[n:2 * n], res[2 * n:3 * n], res[3 * n:4 * n], res[4 * n], res[4 * n + 1]


def _rope_tables():
    inv = (np.float32(ROPE_THETA) ** (-np.arange(0, HD, 2, dtype=np.float32) / np.float32(HD))).astype(np.float32)
    ang = (np.arange(S, dtype=np.float32)[:, None] * inv[None, :]).astype(np.float32)
    cos, sin = np.cos(ang).astype(np.float32), np.sin(ang).astype(np.float32)
    cos64 = np.concatenate([cos, cos], axis=-1)
    sin64 = np.concatenate([-sin, sin], axis=-1)
    return jnp.asarray(np.tile(cos64, (1, 2))), jnp.asarray(np.tile(sin64, (1, 2)))


def _group_matrix(width):
    idx = np.arange(width) // HD
    return jnp.asarray(np.where(idx[:, None] == idx[None, :], 1.0 / HD, 0.0).astype(np.float32)).astype(BF)


def kernel(x, c, w_ada, b_ada, norm_w, w_in, q_norm_w, k_norm_w, sinks, conv_w, conv_b, ln_w, ln_b, w_out, loss_target, m_w_ada, m_b_ada, m_norm_w, m_w_in, m_q_norm_w, m_k_norm_w, m_sinks, m_conv_w, m_conv_b, m_ln_w, m_ln_b, m_w_out, v_w_ada, v_b_ada, v_norm_w, v_w_in, v_q_norm_w, v_k_norm_w, v_sinks, v_conv_w, v_conv_b, v_ln_w, v_ln_b, v_w_out):
    x2 = x[0]
    tgt = loss_target[0]
    cos_t, sin_t = _rope_tables()
    bq = _group_matrix(AW)
    bk = _group_matrix(KVW)
    qw_t, kw_t = q_norm_w, k_norm_w

    tr = lambda t: jnp.swapaxes(t[0], 0, 1)
    tc = lambda t: jnp.swapaxes(t, 0, 1)
    (wt_full, w_out_full, cwf, mod, cact_all, h, qraw, kraw, ga, a, g, gb, qr, kr, vb, z) = _gather_fwd_call(
        tr(w_in), w_out[0], tc(conv_w), c, w_ada[0], b_ada, x2, norm_w, qw_t, kw_t, cos_t, sin_t, bq, bk)
    o, zc, d_out, d_ya, d_yb, gw_out, loss_p, dgate = _fwd_tail_call(
        sinks, qr, kr, vb, z, gb, cwf, conv_b, ln_w, ln_b, ga, x2, tgt, mod, w_out_full)

    dqraw, dga, dk, dv, gqw, gsink, dgb, da, dg, gcw, glw, glb, gcb = _attn_conv_bwd_call(
        sinks, qr, kr, vb, d_ya, ga, o, qraw, qw_t, cos_t, sin_t, bq, d_yb, zc, gb, z, a, g, cwf, ln_w, ln_b)
    (grad_x, g_w_in_t, g_w_out, g_w_ada, g_b_ada, g_norm_w, g_qw, g_kw, g_sinks, g_conv_w, g_conv_b, g_ln_w, g_ln_b,
     loss_v) = _bwd_in_call(dqraw, dk, dv, dga, da, dg, dgb, kraw, kw_t, cos_t, sin_t, bk, h, wt_full, x2, d_out, mod,
                            norm_w, gw_out, gcw, glw, glb, gcb, gqw, gsink, dgate, loss_p, cact_all)

    ws = [w_ada[0], b_ada, norm_w, tr(w_in), q_norm_w, k_norm_w, sinks, tc(conv_w), conv_b, ln_w, ln_b, w_out[0]]
    gs = [g_w_ada, g_b_ada, g_norm_w, g_w_in_t, g_qw, g_kw, g_sinks, g_conv_w, g_conv_b, g_ln_w, g_ln_b, g_w_out]
    ms = [m_w_ada[0], m_b_ada, m_norm_w, tr(m_w_in), m_q_norm_w, m_k_norm_w, m_sinks, tc(m_conv_w), m_conv_b, m_ln_w,
          m_ln_b, m_w_out[0]]
    vs = [v_w_ada[0], v_b_ada, v_norm_w, tr(v_w_in), v_q_norm_w, v_k_norm_w, v_sinks, tc(v_conv_w), v_conv_b, v_ln_w,
          v_ln_b, v_w_out[0]]
    grads, deltas, new_m, new_v, grad_x, loss_v = _adam_call(ws, gs, ms, vs, grad_x, loss_v)
    shaped = [w_ada, b_ada, norm_w, w_in, q_norm_w, k_norm_w, sinks, conv_w, conv_b, ln_w, ln_b, w_out]
    W_IN_POS, CONV_W_POS = 3, 7

    def like(vals):
        vals = [jnp.swapaxes(v, 0, 1) if j in (W_IN_POS, CONV_W_POS) else v for j, v in enumerate(vals)]
        return [v.reshape(s.shape) for v, s in zip(vals, shaped)]

    return (loss_v[0, 0], grad_x[None], *like(grads), *like(deltas), *like(new_m), *like(new_v))
```

```python
import functools

import jax
import jax.numpy as jnp
import numpy as np
from jax import lax
from jax.experimental import pallas as pl
from jax.experimental.pallas import tpu as pltpu

S = 2048
D = 1024
NDEV = 8
HD = 64
NQ = 8
NKV = 2
AW = 512
KVW = 128
CW = 512
INW = 2816
IN_SHARD = INW // NDEV
ADA_SHARD = 3 * D // NDEV
OUT_SHARD = D // NDEV
CONV_SHARD = CW // NDEV
CK = 31
CKP = 32
BLK = 128
TS = 256
NT = S // TS
NB = S // BLK
EPS = 1e-6
ROPE_THETA = 10000.0
NEG = -1e30
BF = jnp.bfloat16
F32 = jnp.float32

ADAM_LR = 0.001
ADAM_B1 = 0.9
ADAM_B2 = 0.999
ADAM_EPS = 1e-08
ADAM_WD = 0.01
ADAM_STEP = 10

VMEM_LIMIT = 56 * 1024 * 1024
BIG_VMEM_LIMIT = 62 * 1024 * 1024
MESH = pl.DeviceIdType.MESH

_VMEM = pl.BlockSpec(memory_space=pltpu.VMEM)
_SMEM = pl.BlockSpec(memory_space=pltpu.SMEM)
_ANY = pl.BlockSpec(memory_space=pl.ANY)


def _params(grid=False):
    if grid:
        return pltpu.CompilerParams(dimension_semantics=("arbitrary",), vmem_limit_bytes=VMEM_LIMIT)
    return pltpu.CompilerParams(vmem_limit_bytes=VMEM_LIMIT)


def _row(i):
    return (i, 0)


def _const(i):
    return (0, 0)


def _sigmoid(t):
    return 1.0 / (1.0 + jnp.exp(-t))


def _silu_and_grad(t):
    sg = _sigmoid(t)
    return t * sg, sg * (1.0 + t * (1.0 - sg))


def _group_mean(t, b_ref):
    hi = t.astype(BF)
    lo = (t - hi.astype(F32)).astype(BF)
    b = b_ref[...]
    return jnp.dot(hi, b, preferred_element_type=F32) + jnp.dot(lo, b, preferred_element_type=F32)


def _partner(t):
    w = t.shape[-1]
    lane = lax.broadcasted_iota(jnp.int32, t.shape, 1)
    first = (lane & 32) == 0
    return jnp.where(first, pltpu.roll(t, w - 32, 1), pltpu.roll(t, 32, 1))


def _norm_rope_fwd(t, w_t, cos, sin, b_ref):
    r = lax.rsqrt(_group_mean(t * t, b_ref) + EPS)
    tn = t * r * w_t
    return tn * cos + _partner(tn) * sin


def _norm_rope_bwd(d_out, t, w_t, cos, sin, b_ref):
    d_tn = d_out * cos + _partner(d_out * sin)
    r = lax.rsqrt(_group_mean(t * t, b_ref) + EPS)
    th = t * r
    g_w = jnp.sum(d_tn * th, axis=0, keepdims=True)
    d_th = d_tn * w_t
    d_t = r * (d_th - th * _group_mean(d_th * th, b_ref))
    return d_t, g_w


def _mesh_pos():
    return lax.axis_index("x"), lax.axis_index("y"), lax.axis_index("c")


def _ag_copy(chan, k, block, to):
    blk, send_sems, recv_sems = chan
    ref = blk(*block)
    return pltpu.make_async_remote_copy(src_ref=ref, dst_ref=ref, send_sem=send_sems.at[k],
                                        recv_sem=recv_sems.at[k], device_id=to, device_id_type=MESH)


def _ag_start(chan, pos):
    x, y, c = pos
    me = (x, y, c)
    chips = [(1 - x, y), (x, 1 - y), (1 - x, 1 - y)]
    first = [_ag_copy(chan, 0, me, (x, y, 1 - c))]
    first += [_ag_copy(chan, 1 + j, me, (*chip, c)) for j, chip in enumerate(chips)]
    for cp in first:
        cp.start()
    return first


def _ag_finish(chan, pos, first):
    x, y, c = pos
    me = (x, y, c)
    sibling = (x, y, 1 - c)
    chips = [(1 - x, y), (x, 1 - y), (1 - x, 1 - y)]
    passed = [_ag_copy(chan, 4 + j, (*chip, c), sibling) for j, chip in enumerate(chips)]
    for j, chip in enumerate(chips):
        _ag_copy(chan, 1 + j, (*chip, c), me).wait_recv()
        passed[j].start()
    _ag_copy(chan, 0, sibling, me).wait_recv()
    for j, chip in enumerate(chips):
        _ag_copy(chan, 4 + j, (*chip, 1 - c), me).wait_recv()
    for cp in first + passed:
        cp.wait_send()


def _slab(buf):
    return lambda px, py, pc: buf.at[4 * px + 2 * py + pc]


def _row_block(buf, rows, align):
    return lambda px, py, pc: buf.at[pl.ds(pl.multiple_of((4 * px + 2 * py + pc) * rows, align), rows), :]


HALF = INW // 2


def _gather_fwd_call(w_in_t, w_out_s, conv_w_s, c, w_ada_s, b_ada, x2, norm_w, qw_t, kw_t, cos_t, sin_t, bq, bk):
    def body(win_ref, wout_ref, cw_ref, c_ref, wada_ref, bada_ref, x_ref, nw_ref, qw_ref, kw_ref, cos_ref, sin_ref,
             bq_ref, bk_ref,
             wtf_hbm, woutf_hbm, cwf_ref, mod_ref, cact_ref,
             h_ref, qraw_ref, kraw_ref, ga_ref, a_ref, g_ref, gb_ref, qr_ref, kr_ref, vb_ref, z_ref,
             cw_buf, ca_buf, mp_buf, h_s, raw0, pt, wtf_ref, woutf_ref,
             s0, r0, s1, r1, s2, r2, s3, r3, s4, r4, out_sems):
        s = pl.program_id(0)
        pos = _mesh_pos()
        x, y, cc = pos
        me3 = (x, y, cc)
        me = 4 * x + 2 * y + cc
        sibling = (x, y, 1 - cc)
        chips = [(1 - x, y), (x, 1 - y), (1 - x, 1 - y)]
        ch_win = (_row_block(wtf_ref, IN_SHARD, 16), s0, r0)
        ch_wout = (_row_block(woutf_ref, OUT_SHARD, 16), s1, r1)
        ch_cw = (_slab(cw_buf), s2, r2)
        ch_ca = (_slab(ca_buf), s3, r3)
        ch_mp = (_slab(mp_buf), s4, r4)

        def first(chan, j):
            return _ag_copy(chan, j, me3, sibling if j == 0 else (*chips[j - 1], cc))

        def passed(chan, j):
            return _ag_copy(chan, 4 + j, (*chips[j], cc), sibling)

        def landed(chan, j):
            return _ag_copy(chan, 1 + j, (*chips[j], cc), me3)

        def relayed(chan, j):
            return _ag_copy(chan, 4 + j, (*chips[j], 1 - cc), me3)

        def from_sibling(chan):
            return _ag_copy(chan, 0, sibling, me3)

        @pl.when(s == 0)
        def _():
            cv = c_ref[...]
            ca_buf[me] = jnp.broadcast_to(cv * _sigmoid(cv), (8, D))
            f_ca = _ag_start(ch_ca, pos)
            wtf_ref[pl.ds(pl.multiple_of(me * IN_SHARD, 16), IN_SHARD), :] = win_ref[...].astype(BF)
            for j in range(3):
                first(ch_win, j).start()
            cw_buf[me] = cw_ref[:, 0, :]
            f_cw = _ag_start(ch_cw, pos)

            _ag_finish(ch_ca, pos, f_ca)
            cact_all = jnp.concatenate([ca_buf[d, 0:1, :] for d in range(NDEV)], axis=0)
            cact_ref[...] = cact_all
            col0 = pl.multiple_of(me * ADA_SHARD, 128)
            mp_buf[me] = jnp.dot(cact_all, wada_ref[...], preferred_element_type=F32,
                                 precision=lax.Precision.HIGHEST) + bada_ref[:, pl.ds(col0, ADA_SHARD)]
            f_mp = _ag_start(ch_mp, pos)
            _ag_finish(ch_cw, pos, f_cw)
            _ag_finish(ch_mp, pos, f_mp)
            for d in range(NDEV):
                cwf_ref[0:CK, CONV_SHARD * d:CONV_SHARD * (d + 1)] = cw_buf[d]
            cwf_ref[CK:CKP, :] = jnp.zeros((CKP - CK, CW), F32)
            mod_ref[...] = jnp.concatenate([mp_buf[d, pl.ds(me, 1), :] for d in range(NDEV)], axis=1)

            for j in (1, 0):
                landed(ch_win, j).wait_recv()
                passed(ch_win, j).start()
            from_sibling(ch_win).wait_recv()
            relayed(ch_win, 1).wait_recv()
            first(ch_win, 1).wait_send()
            first(ch_win, 2).wait_send()
            first(ch_win, 3).start()
            wout = wout_ref[...].astype(BF)
            woutf_ref[pl.ds(pl.multiple_of(me * OUT_SHARD, 16), OUT_SHARD), :] = wout
            for j in range(4):
                first(ch_wout, j).start()

        row0 = pl.multiple_of((s % NT) * TS, TS)

        @pl.when(s < NT)
        def _():
            xv = x_ref[...]
            shift = mod_ref[:, 0:D]
            scale = mod_ref[:, D:2 * D]
            r = lax.rsqrt(jnp.mean(xv * xv, axis=-1, keepdims=True) + EPS)
            hb = ((xv * r * nw_ref[...]) * (1.0 + scale) + shift).astype(BF)
            h_s[pl.ds(row0, TS), :] = hb
            w_half = wtf_ref[pl.ds(pl.multiple_of(x * HALF, 16), HALF), :]
            raw0[pl.ds(row0, TS), :] = lax.dot_general(hb, w_half, (((1,), (1,)), ((), ())),
                                                       preferred_element_type=F32)

        @pl.when(s == NT)
        def _():
            relayed(ch_win, 0).wait_recv()
            landed(ch_win, 2).wait_recv()
            passed(ch_win, 2).start()
            relayed(ch_win, 2).wait_recv()
            pltpu.make_async_copy(wtf_ref, wtf_hbm, out_sems.at[0]).start()

        @pl.when(s >= NT)
        def _():
            hb = h_s[pl.ds(row0, TS), :]
            h_ref[...] = hb
            w_half = wtf_ref[pl.ds(pl.multiple_of((1 - x) * HALF, 16), HALF), :]
            raw1 = lax.dot_general(hb, w_half, (((1,), (1,)), ((), ())), preferred_element_type=F32)
            pt[:, pl.ds(pl.multiple_of(x * HALF, 128), HALF)] = raw0[pl.ds(row0, TS), :]
            pt[:, pl.ds(pl.multiple_of((1 - x) * HALF, 128), HALF)] = raw1
            cos = cos_ref[...]
            sin = sin_ref[...]
            q = pt[:, 0:512]
            qraw_ref[...] = q
            qr_ref[...] = _norm_rope_fwd(q, jnp.tile(qw_ref[...], (1, NQ)), jnp.tile(cos, (1, 4)),
                                         jnp.tile(sin, (1, 4)), bq_ref).astype(BF)
            k = pt[:, 512:640]
            kraw_ref[...] = k
            kr_ref[...] = _norm_rope_fwd(k, jnp.tile(kw_ref[...], (1, NKV)), cos, sin, bk_ref).astype(BF)
            vb_ref[...] = pt[:, 640:768].astype(BF)
            ga_ref[...] = pt[:, 768:1280]
            a = pt[:, 1280:1792]
            g = pt[:, 1792:2304]
            a_ref[...] = a
            g_ref[...] = g
            z_ref[...] = a * _sigmoid(g)
            gb_ref[...] = pt[:, 2304:2816]

        @pl.when(s == 2 * NT - 1)
        def _():
            for j in range(3):
                landed(ch_wout, j).wait_recv()
                passed(ch_wout, j).start()
            from_sibling(ch_wout).wait_recv()
            for j in range(3):
                relayed(ch_wout, j).wait_recv()
            out_copy = pltpu.make_async_copy(woutf_ref, woutf_hbm, out_sems.at[1])
            out_copy.start()
            pltpu.make_async_copy(wtf_ref, wtf_hbm, out_sems.at[0]).wait()
            out_copy.wait()
            first(ch_win, 0).wait_send()
            first(ch_win, 3).wait_send()
            for j in range(3):
                passed(ch_win, j).wait_send()
                passed(ch_wout, j).wait_send()
            for j in range(4):
                first(ch_wout, j).wait_send()

    early = lambda i: (jnp.minimum(i, NT - 1), 0)
    late = lambda i: (jnp.maximum(i - NT, 0), 0)
    t512 = pl.BlockSpec((TS, 512), late)
    t128 = pl.BlockSpec((TS, 128), late)
    sem = pltpu.SemaphoreType.DMA((7,))
    sds = jax.ShapeDtypeStruct
    return pl.pallas_call(
        body, name="gather_fwd", grid=(2 * NT,),
        out_shape=[sds((INW, D), BF), sds((D, D), BF), sds((CKP, CW), F32), sds((1, 3 * D), F32), sds((NDEV, D), F32),
                   sds((S, D), BF), sds((S, AW), F32), sds((S, KVW), F32), sds((S, AW), F32), sds((S, CW), F32),
                   sds((S, CW), F32), sds((S, CW), F32), sds((S, AW), BF), sds((S, KVW), BF), sds((S, KVW), BF),
                   sds((S, CW), F32)],
        in_specs=[_VMEM] * 6 + [pl.BlockSpec((TS, D), early), pl.BlockSpec((1, D), _const),
                                pl.BlockSpec((1, HD), _const), pl.BlockSpec((1, HD), _const), t128, t128,
                                pl.BlockSpec((AW, AW), _const), pl.BlockSpec((KVW, KVW), _const)],
        out_specs=[_ANY, _ANY] + [_VMEM] * 3 + [pl.BlockSpec((TS, D), late), t512, t128, t512, t512, t512, t512, t512,
                                                t128, t128, t512],
        scratch_shapes=[pltpu.VMEM((NDEV, CK, CONV_SHARD), F32), pltpu.VMEM((NDEV, 8, D), F32),
                        pltpu.VMEM((NDEV, 8, ADA_SHARD), F32), pltpu.VMEM((S, D), BF), pltpu.VMEM((S, HALF), F32),
                        pltpu.VMEM((TS, INW), F32), pltpu.VMEM((INW, D), BF), pltpu.VMEM((D, D), BF)]
        + [sem] * 10 + [pltpu.SemaphoreType.DMA((2,))],
        compiler_params=_params(True),
    )(w_in_t, w_out_s, conv_w_s, c, w_ada_s, b_ada, x2, norm_w, qw_t, kw_t, cos_t, sin_t, bq, bk)


QB = 2
NQB = NB // QB


def _band_masks(has_prev):
    kj = lax.broadcasted_iota(jnp.int32, (2 * BLK, 4 * BLK), 0)
    qi = lax.broadcasted_iota(jnp.int32, (2 * BLK, 4 * BLK), 1) & (BLK - 1)
    dist = qi + BLK - kj
    local = (dist >= 0) & (dist < BLK)
    return local & ((kj >= BLK) | has_prev), local


def _key_blocks(sb, prev_ref, cur_ref):
    prev = prev_ref[...] if sb == 0 else cur_ref[BLK * (sb - 1):BLK * sb, :]
    return prev, cur_ref[BLK * sb:BLK * (sb + 1), :]


def _sink_lanes(sink_ref, g):
    lane = lax.broadcasted_iota(jnp.int32, (1, 4 * BLK), 1)
    return jnp.where(lane < BLK, sink_ref[0, 4 * g],
                     jnp.where(lane < 2 * BLK, sink_ref[0, 4 * g + 1],
                               jnp.where(lane < 3 * BLK, sink_ref[0, 4 * g + 2], sink_ref[0, 4 * g + 3])))


def _unstack_t(t):
    return [t[:, BLK * h:BLK * (h + 1)].T for h in range(4)]


def _stack_heads(t, g):
    return jnp.concatenate([t[:, HD * (4 * g + h):HD * (4 * g + h + 1)] for h in range(4)], axis=0)


def _band(prev, cur, g):
    return jnp.concatenate([prev[:, HD * g:HD * (g + 1)], cur[:, HD * g:HD * (g + 1)]], axis=0)


def _softmax_band(qs, kb, mask, sink):
    s = lax.dot_general(kb, qs, (((1,), (1,)), ((), ())), preferred_element_type=F32) * (HD ** -0.5)
    s = jnp.where(mask, s, NEG)
    m = jnp.maximum(jnp.max(s, axis=0, keepdims=True), sink)
    e = jnp.exp(s - m)
    es = jnp.exp(sink - m)
    inv = 1.0 / (jnp.sum(e, axis=0, keepdims=True) + es)
    return e * inv, es * inv


HALO = 32


RC = 64
LC = 128


def _windows(ext_ref, r0, l0, base):
    col = ext_ref[pl.ds(r0, RC + HALO), pl.ds(l0, LC)]
    for s in range(8):
        rolled = col if s == 0 else pltpu.roll(col, RC + HALO - s, 0)
        for t in range(CK):
            if (base + t) % 8 == s:
                a8 = base + t - s
                yield t, rolled[a8:a8 + RC]


def _taps(ext_ref, r0, l0, base, cw_ref, flip):
    acc = None
    for t, win in _windows(ext_ref, r0, l0, base):
        k = CK - 1 - t if flip else t
        term = win * cw_ref[k:k + 1, pl.ds(l0, LC)]
        acc = term if acc is None else acc + term
    return acc


NSUB = (BLK // RC) * (CW // LC)


def _sub_tile(t, row_base):
    r0 = pl.multiple_of(row_base + (t // (CW // LC)) * RC, RC)
    l0 = pl.multiple_of((t % (CW // LC)) * LC, LC)
    return r0, l0


FR = QB * BLK


def _fwd_tail_call(sinks, qr, kr, vb, z, gb, cwf, conv_b, ln_w, ln_b, ga, x2, tgt, mod, w_out_full):
    def body(sink_ref, q_ref, kp_ref, kc_ref, vp_ref, vc_ref, z_ref, zh_ref, gb_ref, cw_ref, cb_ref, lw_ref, lb_ref,
             ga_ref, x_ref, t_ref, mod_ref, w_ref,
             o_ref, zc_ref, dout_ref, dya_ref, dyb_ref, gw_ref, loss_ref, dgate_ref,
             zext, yb_ref):
        i = pl.program_id(0)

        @pl.when(i == 0)
        def _():
            gw_ref[...] = jnp.zeros((D, D), F32)
            loss_ref[...] = jnp.zeros((1, 128), F32)
            dgate_ref[...] = jnp.zeros((1, D), F32)

        def out_proj(rows):
            gav = ga_ref[rows, :]
            ya = o_ref[rows, :] * (gav * _sigmoid(gav))
            ycat = jnp.concatenate([ya.astype(BF), yb_ref[rows, :]], axis=1)
            w = w_ref[...]
            y = jnp.dot(ycat, w, preferred_element_type=F32)
            gate = mod_ref[:, 2 * D:3 * D]
            diff = x_ref[rows, :] + gate * y - t_ref[rows, :]
            sq = jnp.sum(jnp.sum(diff * diff, axis=1, keepdims=True), axis=0, keepdims=True)
            loss_ref[...] += jnp.broadcast_to(sq, (1, 128))
            d_out = diff * (1.0 / D)
            dout_ref[rows, :] = d_out
            dgate_ref[...] += jnp.sum(d_out * y, axis=0, keepdims=True)
            dy = (d_out * gate).astype(BF)
            d_ycat = lax.dot_general(dy, w, (((1,), (1,)), ((), ())), preferred_element_type=F32)
            dya_ref[rows, :] = d_ycat[:, 0:AW]
            dyb_ref[rows, :] = d_ycat[:, AW:D]
            gw_ref[...] += lax.dot_general(ycat, dy, (((0,), (0,)), ((), ())), preferred_element_type=F32)

        zext[0:HALO, :] = jnp.where(i > 0, zh_ref[...], 0.0)
        zext[HALO:HALO + FR, :] = z_ref[...]
        masks = _band_masks(i > 0)
        for sb in range(QB):
            rows = slice(BLK * sb, BLK * (sb + 1))
            mask = masks[min(sb, 1)]
            q = q_ref[rows, :]
            kp, kc = _key_blocks(sb, kp_ref, kc_ref)
            vp, vc = _key_blocks(sb, vp_ref, vc_ref)
            for g in range(NKV):
                p, _ = _softmax_band(_stack_heads(q, g), _band(kp, kc, g), mask, _sink_lanes(sink_ref, g))
                o_t = lax.dot_general(_band(vp, vc, g), p.astype(BF), (((0,), (0,)), ((), ())),
                                      preferred_element_type=F32)
                for h, o_h in enumerate(_unstack_t(o_t)):
                    o_ref[rows, HD * (4 * g + h):HD * (4 * g + h + 1)] = o_h
            for r0 in range(BLK * sb, BLK * (sb + 1), RC):
                for l0 in range(0, CW, LC):
                    acc = _taps(zext, r0, l0, HALO - (CK - 1), cw_ref, False)
                    zc_ref[r0:r0 + RC, l0:l0 + LC] = acc + cb_ref[:, l0:l0 + LC]
            zc = zc_ref[rows, :]
            mu = jnp.mean(zc, axis=-1, keepdims=True)
            dz = zc - mu
            rstd = lax.rsqrt(jnp.mean(dz * dz, axis=-1, keepdims=True) + EPS)
            zn = dz * rstd * lw_ref[...] + lb_ref[...]
            gbv = gb_ref[rows, :]
            yb_ref[rows, :] = ((zn * _sigmoid(zn)) * (gbv * _sigmoid(gbv))).astype(BF)
            if (sb + 1) * BLK % TS == 0:
                out_proj(slice((sb + 1) * BLK - TS, (sb + 1) * BLK))

    prev = lambda i: (jnp.maximum(QB * i - 1, 0), 0)
    halo = lambda i: (jnp.maximum(FR // HALO * i - 1, 0), 0)
    f512 = pl.BlockSpec((FR, 512), _row)
    f128 = pl.BlockSpec((FR, KVW), _row)
    f1024 = pl.BlockSpec((FR, D), _row)
    c512 = pl.BlockSpec((1, CW), _const)
    sds = jax.ShapeDtypeStruct
    return pl.pallas_call(
        body, name="fwd_tail", grid=(NQB,),
        out_shape=[sds((S, AW), F32), sds((S, CW), F32), sds((S, D), F32), sds((S, AW), F32), sds((S, CW), F32),
                   sds((D, D), F32), sds((1, 128), F32), sds((1, D), F32)],
        in_specs=[_SMEM, f512, pl.BlockSpec((BLK, KVW), prev), f128, pl.BlockSpec((BLK, KVW), prev), f128,
                  f512, pl.BlockSpec((HALO, CW), halo), f512, pl.BlockSpec((CKP, CW), _const), c512, c512, c512,
                  f512, f1024, f1024, pl.BlockSpec((1, 3 * D), _const),
                  pl.BlockSpec((D, D), _const, pipeline_mode=pl.Buffered(1))],
        out_specs=[f512, f512, f1024, f512, f512, _VMEM, _VMEM, _VMEM],
        scratch_shapes=[pltpu.VMEM((FR + HALO, CW), F32), pltpu.VMEM((FR, CW), BF)],
        compiler_params=_params(True),
    )(sinks, qr, kr, kr, vb, vb, z, z, gb, cwf, conv_b, ln_w, ln_b, ga, x2, tgt, mod, w_out_full)


def _ln_gate_bwd(d_yb, zc, gbv, lw, lb):
    mu = jnp.mean(zc, axis=-1, keepdims=True)
    dz = zc - mu
    rstd = lax.rsqrt(jnp.mean(dz * dz, axis=-1, keepdims=True) + EPS)
    zh = dz * rstd
    zn = zh * lw + lb
    act_n, dact_n = _silu_and_grad(zn)
    act_g, dact_g = _silu_and_grad(gbv)
    d_gb = d_yb * act_n * dact_g
    d_zn = d_yb * act_g * dact_n
    dzh = d_zn * lw
    d_zc = rstd * (dzh - jnp.mean(dzh, axis=-1, keepdims=True) - zh * jnp.mean(dzh * zh, axis=-1, keepdims=True))
    return d_zc, d_gb, d_zn, zh


def _attn_conv_bwd_call(sinks, qr, kr, vb, d_ya, ga, o, qraw, qw_t, cos_t, sin_t, bq, d_yb, zc, gb, z, a, g, cwf,
                        ln_w, ln_b):
    def body(sink_ref, q_ref, kp_ref, kc_ref, vp_ref, vc_ref, dya_ref, ga_ref, o_ref, qraw_ref, qw_ref,
             cos_ref, sin_ref, bq_ref,
             dyb_ref, dybn_ref, zc_ref, zcn_ref, gb_ref, gbn_ref, z_ref, zh_ref, a_ref, g_ref, cw_ref, lw_ref, lb_ref,
             dqraw_ref, dga_ref, dk_ref, dv_ref, gqw_ref, gsink_ref,
             dgb_ref, da_ref, dg_ref, gcw_ref, glw_ref, glb_ref, gcb_ref,
             dext, zext, gacc):
        i = pl.program_id(0)

        @pl.when(i == 0)
        def _():
            dk_ref[...] = jnp.zeros((S, KVW), F32)
            dv_ref[...] = jnp.zeros((S, KVW), F32)
            gqw_ref[...] = jnp.zeros((1, AW), F32)
            gsink_ref[...] = jnp.zeros((1, 128), F32)
            gacc[...] = jnp.zeros((CKP * 8, CW), F32)
            glw_ref[...] = jnp.zeros((1, CW), F32)
            glb_ref[...] = jnp.zeros((1, CW), F32)
            gcb_ref[...] = jnp.zeros((1, CW), F32)

        lw = lw_ref[...]
        lb = lb_ref[...]

        def ln_rows(rows):
            d_zc, d_gb, d_zn, zh = _ln_gate_bwd(dyb_ref[rows, :], zc_ref[rows, :], gb_ref[rows, :], lw, lb)
            dgb_ref[rows, :] = d_gb.astype(BF)
            glw_ref[...] += jnp.sum(d_zn * zh, axis=0, keepdims=True)
            glb_ref[...] += jnp.sum(d_zn, axis=0, keepdims=True)
            gcb_ref[...] += jnp.sum(d_zc, axis=0, keepdims=True)
            dext[rows, :] = d_zc

        ln_rows(slice(0, BLK))
        zext[0:HALO, :] = jnp.where(i > 0, zh_ref[...], 0.0)
        zext[HALO:HALO + FR, :] = z_ref[...]

        lane = lax.broadcasted_iota(jnp.int32, (1, 128), 1)
        gsink = jnp.zeros((1, 128), F32)
        dq_rows = []
        masks = _band_masks(i > 0)
        for sb in range(QB):
            rows = slice(BLK * sb, BLK * (sb + 1))
            if sb + 1 < QB:
                ln_rows(slice(BLK * (sb + 1), BLK * (sb + 2)))
            else:
                d_zc_next, _, _, _ = _ln_gate_bwd(dybn_ref[...], zcn_ref[...], gbn_ref[...], lw, lb)
                dext[FR:FR + HALO, :] = jnp.where(i < NQB - 1, d_zc_next, 0.0)
            mask = masks[min(sb, 1)]
            q = q_ref[rows, :]
            d_ya = dya_ref[rows, :]
            act, dact = _silu_and_grad(ga_ref[rows, :])
            dga_ref[rows, :] = (d_ya * o_ref[rows, :] * dact).astype(BF)
            d_o = (d_ya * act).astype(BF)
            kp, kc = _key_blocks(sb, kp_ref, kc_ref)
            vp, vc = _key_blocks(sb, vp_ref, vc_ref)
            dq_parts, dk_parts, dv_parts = [], [], []
            for gi in range(NKV):
                qs = _stack_heads(q, gi)
                kb = _band(kp, kc, gi)
                vbd = _band(vp, vc, gi)
                p, ps = _softmax_band(qs, kb, mask, _sink_lanes(sink_ref, gi))
                dos = _stack_heads(d_o, gi)
                dp = lax.dot_general(vbd, dos, (((1,), (1,)), ((), ())), preferred_element_type=F32)
                dr = jnp.sum(p * dp, axis=0, keepdims=True)
                ds = (p * (dp - dr) * (HD ** -0.5)).astype(BF)
                sink_term = ps * dr
                for h in range(4):
                    part = jnp.sum(sink_term[:, BLK * h:BLK * (h + 1)], axis=1, keepdims=True)
                    gsink = gsink - jnp.where(lane == 4 * gi + h, part, 0.0)
                dv_parts.append(jnp.dot(p.astype(BF), dos, preferred_element_type=F32))
                dk_parts.append(jnp.dot(ds, qs, preferred_element_type=F32))
                dq_t = lax.dot_general(kb, ds, (((0,), (0,)), ((), ())), preferred_element_type=F32)
                dq_parts.extend(_unstack_t(dq_t))
            dkb = jnp.concatenate(dk_parts, axis=1)
            dvb = jnp.concatenate(dv_parts, axis=1)
            blk = QB * i + sb
            r_prev = pl.multiple_of(jnp.maximum(blk - 1, 0) * BLK, BLK)
            r_cur = pl.multiple_of(blk * BLK, BLK)
            dk_ref[pl.ds(r_prev, BLK), :] += dkb[0:BLK]
            dv_ref[pl.ds(r_prev, BLK), :] += dvb[0:BLK]
            dk_ref[pl.ds(r_cur, BLK), :] += dkb[BLK:2 * BLK]
            dv_ref[pl.ds(r_cur, BLK), :] += dvb[BLK:2 * BLK]
            dq_rows.append(jnp.concatenate(dq_parts, axis=1))

            def taps_sub(t, carry, sb=sb):
                r0, l0 = _sub_tile(t, BLK * sb)
                here = (pl.ds(r0, RC), pl.ds(l0, LC))
                d_z = _taps(dext, r0, l0, 0, cw_ref, True)
                sg = _sigmoid(g_ref[here])
                da_ref[here] = (d_z * sg).astype(BF)
                dg_ref[here] = (d_z * a_ref[here] * sg * (1.0 - sg)).astype(BF)
                d_sub = dext[here]
                for k, win in _windows(zext, r0, l0, HALO - (CK - 1)):
                    prod = d_sub * win
                    part = prod[0:8]
                    for q8 in range(1, RC // 8):
                        part = part + prod[8 * q8:8 * q8 + 8]
                    gacc[8 * k:8 * k + 8, pl.ds(l0, LC)] += part
                return carry

            lax.fori_loop(0, NSUB, taps_sub, 0)
        gsink_ref[...] += gsink
        dq = jnp.concatenate(dq_rows, axis=0)
        dq_raw, g_qw = _norm_rope_bwd(dq, qraw_ref[...], jnp.tile(qw_ref[...], (1, NQ)),
                                      jnp.tile(cos_ref[...], (1, 4)), jnp.tile(sin_ref[...], (1, 4)), bq_ref)
        dqraw_ref[...] = dq_raw.astype(BF)
        gqw_ref[...] += g_qw

        @pl.when(i == NQB - 1)
        def _():
            for k in range(CK):
                gcw_ref[k:k + 1, :] = jnp.sum(gacc[8 * k:8 * k + 8, :], axis=0, keepdims=True)
            gcw_ref[CK:CKP, :] = jnp.zeros((CKP - CK, CW), F32)

    prev = lambda i: (jnp.maximum(QB * i - 1, 0), 0)
    halo_prev = lambda i: (jnp.maximum(FR // HALO * i - 1, 0), 0)
    halo_next = lambda i: (jnp.minimum(FR // HALO * (i + 1), S // HALO - 1), 0)
    f512 = pl.BlockSpec((FR, 512), _row)
    f128 = pl.BlockSpec((FR, 128), _row)
    hn = pl.BlockSpec((HALO, CW), halo_next)
    c512 = pl.BlockSpec((1, CW), _const)
    sds = jax.ShapeDtypeStruct
    vec = sds((1, CW), F32)
    return pl.pallas_call(
        body, name="attn_conv_bwd", grid=(NQB,),
        out_shape=[sds((S, AW), BF), sds((S, AW), BF), sds((S, KVW), F32), sds((S, KVW), F32), sds((1, AW), F32),
                   sds((1, 128), F32),
                   sds((S, CW), BF), sds((S, CW), BF), sds((S, CW), BF), sds((CKP, CW), F32), vec, vec, vec],
        in_specs=[_SMEM, f512, pl.BlockSpec((BLK, KVW), prev), f128, pl.BlockSpec((BLK, KVW), prev), f128,
                  f512, f512, f512, f512, pl.BlockSpec((1, HD), _const), f128, f128, pl.BlockSpec((AW, AW), _const),
                  f512, hn, f512, hn, f512, hn, f512, pl.BlockSpec((HALO, CW), halo_prev), f512, f512,
                  pl.BlockSpec((CKP, CW), _const), c512, c512],
        out_specs=[f512, f512, _VMEM, _VMEM, _VMEM, _VMEM, f512, f512, f512, _VMEM, _VMEM, _VMEM, _VMEM],
        scratch_shapes=[pltpu.VMEM((FR + HALO, CW), F32), pltpu.VMEM((FR + HALO, CW), F32),
                        pltpu.VMEM((CKP * 8, CW), F32)],
        compiler_params=_params(True),
    )(sinks, qr, kr, kr, vb, vb, d_ya, ga, o, qraw, qw_t, cos_t, sin_t, bq,
      d_yb, d_yb, zc, zc, gb, gb, z, z, a, g, cwf, ln_w, ln_b)


SM_ROWS = 8
PIECES = ((0, 512), (512, 640), (640, 768), (768, 1280), (1280, 1792), (1792, 2304), (2304, 2816))


def _bwd_in_call(dqraw, dk, dv, dga, da, dg, dgb, kraw, kw_t, cos_t, sin_t, bk, h, wt_full, x2, d_out, mod, norm_w,
                 gw_out, gcw, glw, glb, gcb, gqw, gsink, dgate, loss_p, cact_all):
    def body(dq_ref, dk_ref, dv_ref, dga_ref, da_ref, dg_ref, dgb_ref, kraw_ref, kw_ref, cos_ref, sin_ref, bk_ref,
             h_ref, wt_ref, x_ref, dout_ref, mod_ref, nw_ref, gwout_ref, gcw_ref, glw_ref, glb_ref, gcb_ref, gqw_ref,
             gsink_ref, dgate_ref, loss_ref, cact_ref,
             gx_ref, o_gwin, o_gwout, o_gwada, o_gbada, o_gnw, o_gqw, o_gkw, o_gsink, o_gcw, o_gcb, o_glw, o_glb,
             o_loss,
             acc, win_send, win_sib, win_ici, wout_send, wout_sib, wout_ici, sm_buf, cw_buf, dmod_all, vec_acc, gkw_acc,
             wi_ds, wi_dr, wi_is, wi_ir, wo_ds, wo_dr, wo_is, wo_ir, sm_s, sm_r, cw_s, cw_r):
        i = pl.program_id(0)
        pos = _mesh_pos()
        x, y, cc = pos
        me = 4 * x + 2 * y + cc

        def chip(j):
            return (1 - x if j & 1 else x, 1 - y if j & 2 else y)

        def rows_of(buf, px, py, pc, rows, align):
            return buf.at[pl.ds(pl.multiple_of((4 * px + 2 * py + pc) * rows, align), rows), :]

        bufs = {"in": (win_send, win_sib, win_ici, IN_SHARD, wi_ds, wi_dr, wi_is, wi_ir),
                "out": (wout_send, wout_sib, wout_ici, OUT_SHARD, wo_ds, wo_dr, wo_is, wo_ir)}

        def d2d_copy(j, which):
            send, sib, _, rows, ds_, dr_, _, _ = bufs[which]
            px, py = chip(j)
            return pltpu.make_async_remote_copy(src_ref=rows_of(send, px, py, 1 - cc, rows, 16), dst_ref=sib.at[j],
                                                send_sem=ds_.at[j], recv_sem=dr_.at[j], device_id=(x, y, 1 - cc),
                                                device_id_type=MESH)

        def ici_copy(j, which):
            send, _, ici, rows, _, _, is_, ir_ = bufs[which]
            px, py = chip(j)
            return pltpu.make_async_remote_copy(src_ref=rows_of(send, px, py, cc, rows, 16), dst_ref=ici.at[j - 1],
                                                send_sem=is_.at[j - 1], recv_sem=ir_.at[j - 1], device_id=(px, py, cc),
                                                device_id_type=MESH)

        def level2(which, partial_ref):
            send, sib, _, rows, _, _, _, _ = bufs[which]
            for j in range(1, 4):
                d2d_copy(j, which).wait_recv()
                px, py = chip(j)
                mine = rows_of(partial_ref, px, py, cc, rows, 8)[...]
                rows_of(send, px, py, cc, rows, 16)[...] = (mine + sib[j].astype(F32)).astype(BF)
                ici_copy(j, which).start()

        def finish(which, partial_ref):
            _, sib, ici, rows, _, _, _, _ = bufs[which]
            d2d_copy(0, which).wait_recv()
            total = rows_of(partial_ref, x, y, cc, rows, 8)[...] + sib[0].astype(F32)
            for j in range(1, 4):
                ici_copy(j, which).wait_recv()
                total = total + ici[j - 1].astype(F32)
            for j in range(4):
                d2d_copy(j, which).wait_send()
            for j in range(1, 4):
                ici_copy(j, which).wait_send()
            return total

        def dproj_pieces():
            dk_raw, g_kw = _norm_rope_bwd(dk_ref[...], kraw_ref[...], jnp.tile(kw_ref[...], (1, NKV)), cos_ref[...],
                                          sin_ref[...], bk_ref)
            return [dq_ref[...], dk_raw.astype(BF), dv_ref[...].astype(BF), dga_ref[...], da_ref[...], dg_ref[...],
                    dgb_ref[...]], g_kw

        @pl.when(i == 0)
        def _():
            acc[...] = jnp.zeros((INW, D), F32)
            vec_acc[...] = jnp.zeros((8, D), F32)
            gkw_acc[...] = jnp.zeros((1, KVW), F32)
            wout_send[...] = gwout_ref[...].astype(BF)
            for j in range(4):
                d2d_copy(j, "out").start()
            cw_buf[me] = gcw_ref[...]
            _ag_start((_slab(cw_buf), cw_s, cw_r), pos)

        @pl.when(i == 2)
        def _():
            level2("out", gwout_ref)

        @pl.when(i < NT)
        def _():
            pieces, g_kw = dproj_pieces()
            gkw_acc[...] += g_kw
            hv = h_ref[...]
            for (lo, hi), piece in zip(PIECES, pieces):
                acc[lo:hi, :] += lax.dot_general(piece, hv, (((0,), (0,)), ((), ())), preferred_element_type=F32)

        @pl.when(i == NT - 1)
        def _():
            for lo, hi in PIECES:
                win_send[lo:hi, :] = acc[lo:hi, :].astype(BF)
            for j in range(4):
                d2d_copy(j, "in").start()

        @pl.when(i == NT)
        def _():
            level2("in", acc)

        @pl.when(i >= NT)
        def _():
            pieces, _ = dproj_pieces()
            dproj = jnp.concatenate(pieces, axis=1)
            d_h = jnp.dot(dproj, wt_ref[...], preferred_element_type=F32)
            xv = x_ref[...]
            scale = mod_ref[:, D:2 * D]
            nw = nw_ref[...]
            r = lax.rsqrt(jnp.mean(xv * xv, axis=-1, keepdims=True) + EPS)
            xn = xv * r
            vec_acc[0:1, :] += jnp.sum(d_h, axis=0, keepdims=True)
            vec_acc[1:2, :] += jnp.sum(d_h * (xn * nw), axis=0, keepdims=True)
            d_u = d_h * (1.0 + scale)
            vec_acc[2:3, :] += jnp.sum(d_u * xn, axis=0, keepdims=True)
            d_xn = d_u * nw
            gx_ref[...] = dout_ref[...] + r * (d_xn - xn * jnp.mean(d_xn * xn, axis=-1, keepdims=True))

        @pl.when(i == 2 * NT - 1)
        def _():
            ch_cw = (_slab(cw_buf), cw_s, cw_r)
            z128 = jnp.zeros((1, 128), F32)
            row4 = jnp.concatenate([glw_ref[...], glb_ref[...]], axis=1)
            row5 = jnp.concatenate([gcb_ref[...], gqw_ref[...]], axis=1)
            row6 = jnp.concatenate([gkw_acc[...], gsink_ref[...], loss_ref[...]] + [z128] * 5, axis=1)
            sm_buf[me] = jnp.concatenate([vec_acc[0:2, :], dgate_ref[...], vec_acc[2:3, :], row4, row5, row6,
                                          jnp.zeros((1, D), F32)], axis=0)

            def sm_copy(k):
                peer = (1 - x if k & 4 else x, 1 - y if k & 2 else y, 1 - cc if k & 1 else cc)
                return pltpu.make_async_remote_copy(src_ref=sm_buf.at[me], dst_ref=sm_buf.at[me],
                                                    send_sem=sm_s.at[k - 1], recv_sem=sm_r.at[k - 1],
                                                    device_id=peer, device_id_type=MESH)

            for k in range(1, NDEV):
                sm_copy(k).start()
            _ag_finish(ch_cw, pos, [_ag_copy(ch_cw, 0, (x, y, cc), (x, y, 1 - cc))]
                       + [_ag_copy(ch_cw, 1 + j, (x, y, cc), (*chip(j + 1), cc)) for j in range(3)])
            for k in range(1, NDEV):
                sm_copy(k).wait_recv()
            tot = sm_buf[0]
            cw_tot = cw_buf[0]
            for d in range(1, NDEV):
                tot = tot + sm_buf[d]
                cw_tot = cw_tot + cw_buf[d]
            o_gbada[...] = jnp.concatenate([tot[0:1, :], tot[1:2, :], tot[2:3, :]], axis=1)
            o_gnw[...] = tot[3:4, :]
            o_glw[...] = tot[4:5, 0:CW]
            o_glb[...] = tot[4:5, CW:D]
            o_gcb[...] = tot[5:6, 0:CW]
            gq = tot[5:6, CW:CW + HD]
            for hh in range(1, NQ):
                gq = gq + tot[5:6, CW + HD * hh:CW + HD * (hh + 1)]
            o_gqw[...] = gq
            o_gkw[...] = tot[6:7, 0:HD] + tot[6:7, HD:2 * HD]
            o_gsink[...] = tot[6:7, 128:128 + NQ]
            o_loss[...] = tot[6:7, 256:384] * (0.5 / D)
            mine = jnp.zeros((CK, CONV_SHARD), F32)
            for d in range(NDEV):
                mine = mine + jnp.where(me == d, cw_tot[0:CK, CONV_SHARD * d:CONV_SHARD * (d + 1)], 0.0)
            for k in range(CK):
                o_gcw[k] = mine[k:k + 1, :]
            for d in range(NDEV):
                dmod_all[d:d + 1, :] = jnp.concatenate([sm_buf[d, 0:1, :], sm_buf[d, 1:2, :], sm_buf[d, 2:3, :]],
                                                       axis=1)
            col0 = pl.multiple_of(me * ADA_SHARD, 128)
            o_gwada[...] = lax.dot_general(cact_ref[...], dmod_all[:, pl.ds(col0, ADA_SHARD)], (((0,), (0,)), ((), ())),
                                           preferred_element_type=F32, precision=lax.Precision.HIGHEST)

            o_gwout[...] = finish("out", gwout_ref)
            o_gwin[...] = finish("in", acc)
            for k in range(1, NDEV):
                sm_copy(k).wait_send()

    half = lambda i: (i % NT, 0)
    late = lambda i: (jnp.maximum(i - NT, 0), 0)
    t512 = pl.BlockSpec((TS, 512), half)
    t128 = pl.BlockSpec((TS, 128), half)
    l1024 = pl.BlockSpec((TS, D), late)
    sem7 = pltpu.SemaphoreType.DMA((7,))
    sem4 = pltpu.SemaphoreType.DMA((4,))
    sem3 = pltpu.SemaphoreType.DMA((3,))
    sds = jax.ShapeDtypeStruct
    return pl.pallas_call(
        body, name="bwd_in", grid=(2 * NT,),
        out_shape=[sds((S, D), F32), sds((IN_SHARD, D), F32), sds((OUT_SHARD, D), F32), sds((D, ADA_SHARD), F32),
                   sds((1, 3 * D), F32), sds((1, D), F32), sds((1, HD), F32), sds((1, HD), F32), sds((1, NQ), F32),
                   sds((CK, 1, CONV_SHARD), F32), sds((1, CW), F32), sds((1, CW), F32), sds((1, CW), F32),
                   sds((1, 128), F32)],
        in_specs=[t512, t128, t128, t512, t512, t512, t512, t128, pl.BlockSpec((1, HD), _const), t128, t128,
                  pl.BlockSpec((KVW, KVW), _const), pl.BlockSpec((TS, D), half),
                  pl.BlockSpec((INW, D), _const, pipeline_mode=pl.Buffered(1)), l1024, l1024,
                  pl.BlockSpec((1, 3 * D), _const), pl.BlockSpec((1, D), _const)] + [_VMEM] * 10,
        out_specs=[l1024] + [_VMEM] * 13,
        scratch_shapes=[pltpu.VMEM((INW, D), F32), pltpu.VMEM((INW, D), BF), pltpu.VMEM((4, IN_SHARD, D), BF),
                        pltpu.VMEM((3, IN_SHARD, D), BF), pltpu.VMEM((D, D), BF), pltpu.VMEM((4, OUT_SHARD, D), BF),
                        pltpu.VMEM((3, OUT_SHARD, D), BF),
                        pltpu.VMEM((NDEV, SM_ROWS, D), F32), pltpu.VMEM((NDEV, CKP, CW), F32),
                        pltpu.VMEM((NDEV, 3 * D), F32), pltpu.VMEM((8, D), F32), pltpu.VMEM((1, KVW), F32)]
        + [sem4, sem4, sem3, sem3] * 2 + [sem7] * 4,
        compiler_params=pltpu.CompilerParams(dimension_semantics=("arbitrary",), vmem_limit_bytes=BIG_VMEM_LIMIT),
    )(dqraw, dk, dv, dga, da, dg, dgb, kraw, kw_t, cos_t, sin_t, bk, h, wt_full, x2, d_out, mod, norm_w,
      gw_out, gcw, glw, glb, gcb, gqw, gsink, dgate, loss_p, cact_all)


ADAM_STEPS = 4


def _adam_chunking(arr):
    if arr.ndim == 2 and arr.shape[0] % (8 * ADAM_STEPS) == 0:
        return "rows"
    if arr.ndim == 2 and arr.shape[1] % (128 * ADAM_STEPS) == 0:
        return "cols"
    return None


def _adam_call(ws, gs, ms, vs, grad_x, loss_v):
    n = len(ws)
    bc1 = 1.0 - ADAM_B1 ** ADAM_STEP
    bc2 = 1.0 - ADAM_B2 ** ADAM_STEP
    chunked = [_adam_chunking(w) for w in ws]

    def body(*refs):
        ins, outs = refs[:4 * n + 2], refs[4 * n + 2:]
        i = pl.program_id(0)

        def update(j):
            w, g, m, v = (ins[j][...], ins[n + j][...], ins[2 * n + j][...], ins[3 * n + j][...])
            m_new = ADAM_B1 * m + (1.0 - ADAM_B1) * g
            v_new = ADAM_B2 * v + (1.0 - ADAM_B2) * (g * g)
            m_hat = m_new / bc1
            v_hat = v_new / bc2
            outs[j][...] = g
            outs[n + j][...] = -ADAM_LR * (m_hat / (jnp.sqrt(v_hat) + ADAM_EPS) + ADAM_WD * w)
            outs[2 * n + j][...] = m_new
            outs[3 * n + j][...] = v_new

        for j in range(n):
            if chunked[j]:
                update(j)
        outs[4 * n][...] = ins[4 * n][...]

        @pl.when(i == 0)
        def _():
            for j in range(n):
                if not chunked[j]:
                    update(j)
            outs[4 * n + 1][...] = ins[4 * n + 1][...]

    def spec(arr, how):
        if how == "rows":
            return pl.BlockSpec((arr.shape[0] // ADAM_STEPS, arr.shape[1]), _row)
        if how == "cols":
            return pl.BlockSpec((arr.shape[0], arr.shape[1] // ADAM_STEPS), lambda i: (0, i))
        zeros = (0,) * arr.ndim
        return pl.BlockSpec(arr.shape, lambda i: zeros)

    par_specs = [spec(w, ch) for w, ch in zip(ws, chunked)]
    extra = [spec(grad_x, "rows"), spec(loss_v, None)]
    shapes = [jax.ShapeDtypeStruct(w.shape, F32) for w in ws]
    res = pl.pallas_call(
        body, name="adam", grid=(ADAM_STEPS,),
        out_shape=shapes * 4 + [jax.ShapeDtypeStruct(grad_x.shape, F32), jax.ShapeDtypeStruct(loss_v.shape, F32)],
        in_specs=par_specs * 4 + extra, out_specs=par_specs * 4 + extra,
        compiler_params=_params(True),
    )(*ws, *gs, *ms, *vs, grad_x, loss_v)
    return res[0:n], res[n:2 * n], res[2 * n:3 * n], res[3 * n:4 * n], res[4 * n], res[4 * n + 1]


def _rope_tables():
    inv = (np.float32(ROPE_THETA) ** (-np.arange(0, HD, 2, dtype=np.float32) / np.float32(HD))).astype(np.float32)
    ang = (np.arange(S, dtype=np.float32)[:, None] * inv[None, :]).astype(np.float32)
    cos, sin = np.cos(ang).astype(np.float32), np.sin(ang).astype(np.float32)
    cos64 = np.concatenate([cos, cos], axis=-1)
    sin64 = np.concatenate([-sin, sin], axis=-1)
    return jnp.asarray(np.tile(cos64, (1, 2))), jnp.asarray(np.tile(sin64, (1, 2)))


def _group_matrix(width):
    idx = np.arange(width) // HD
    return jnp.asarray(np.where(idx[:, None] == idx[None, :], 1.0 / HD, 0.0).astype(np.float32)).astype(BF)


def kernel(x, c, w_ada, b_ada, norm_w, w_in, q_norm_w, k_norm_w, sinks, conv_w, conv_b, ln_w, ln_b, w_out, loss_target, m_w_ada, m_b_ada, m_norm_w, m_w_in, m_q_norm_w, m_k_norm_w, m_sinks, m_conv_w, m_conv_b, m_ln_w, m_ln_b, m_w_out, v_w_ada, v_b_ada, v_norm_w, v_w_in, v_q_norm_w, v_k_norm_w, v_sinks, v_conv_w, v_conv_b, v_ln_w, v_ln_b, v_w_out):
    x2 = x[0]
    tgt = loss_target[0]
    cos_t, sin_t = _rope_tables()
    bq = _group_matrix(AW)
    bk = _group_matrix(KVW)
    qw_t, kw_t = q_norm_w, k_norm_w

    tr = lambda t: jnp.swapaxes(t[0], 0, 1)
    tc = lambda t: jnp.swapaxes(t, 0, 1)
    (wt_full, w_out_full, cwf, mod, cact_all, h, qraw, kraw, ga, a, g, gb, qr, kr, vb, z) = _gather_fwd_call(
        tr(w_in), w_out[0], tc(conv_w), c, w_ada[0], b_ada, x2, norm_w, qw_t, kw_t, cos_t, sin_t, bq, bk)
    o, zc, d_out, d_ya, d_yb, gw_out, loss_p, dgate = _fwd_tail_call(
        sinks, qr, kr, vb, z, gb, cwf, conv_b, ln_w, ln_b, ga, x2, tgt, mod, w_out_full)

    dqraw, dga, dk, dv, gqw, gsink, dgb, da, dg, gcw, glw, glb, gcb = _attn_conv_bwd_call(
        sinks, qr, kr, vb, d_ya, ga, o, qraw, qw_t, cos_t, sin_t, bq, d_yb, zc, gb, z, a, g, cwf, ln_w, ln_b)
    (grad_x, g_w_in_t, g_w_out, g_w_ada, g_b_ada, g_norm_w, g_qw, g_kw, g_sinks, g_conv_w, g_conv_b, g_ln_w, g_ln_b,
     loss_v) = _bwd_in_call(dqraw, dk, dv, dga, da, dg, dgb, kraw, kw_t, cos_t, sin_t, bk, h, wt_full, x2, d_out, mod,
                            norm_w, gw_out, gcw, glw, glb, gcb, gqw, gsink, dgate, loss_p, cact_all)

    ws = [w_ada[0], b_ada, norm_w, tr(w_in), q_norm_w, k_norm_w, sinks, tc(conv_w), conv_b, ln_w, ln_b, w_out[0]]
    gs = [g_w_ada, g_b_ada, g_norm_w, g_w_in_t, g_qw, g_kw, g_sinks, g_conv_w, g_conv_b, g_ln_w, g_ln_b, g_w_out]
    ms = [m_w_ada[0], m_b_ada, m_norm_w, tr(m_w_in), m_q_norm_w, m_k_norm_w, m_sinks, tc(m_conv_w), m_conv_b, m_ln_w,
          m_ln_b, m_w_out[0]]
    vs = [v_w_ada[0], v_b_ada, v_norm_w, tr(v_w_in), v_q_norm_w, v_k_norm_w, v_sinks, tc(v_conv_w), v_conv_b, v_ln_w,
          v_ln_b, v_w_out[0]]
    grads, deltas, new_m, new_v, grad_x, loss_v = _adam_call(ws, gs, ms, vs, grad_x, loss_v)
    shaped = [w_ada, b_ada, norm_w, w_in, q_norm_w, k_norm_w, sinks, conv_w, conv_b, ln_w, ln_b, w_out]
    W_IN_POS, CONV_W_POS = 3, 7

    def like(vals):
        vals = [jnp.swapaxes(v, 0, 1) if j in (W_IN_POS, CONV_W_POS) else v for j, v in enumerate(vals)]
        return [v.reshape(s.shape) for v, s in zip(vals, shaped)]

    return (loss_v[0, 0], grad_x[None], *like(grads), *like(deltas), *like(new_m), *like(new_v))
```

```python
import functools

import jax
import jax.numpy as jnp
import numpy as np
from jax import lax
from jax.experimental import pallas as pl
from jax.experimental.pallas import tpu as pltpu

S = 2048
D = 1024
NDEV = 8
HD = 64
NQ = 8
NKV = 2
AW = 512
KVW = 128
CW = 512
INW = 2816
IN_SHARD = INW // NDEV
ADA_SHARD = 3 * D // NDEV
OUT_SHARD = D // NDEV
CONV_SHARD = CW // NDEV
CK = 31
CKP = 32
BLK = 128
TS = 256
NT = S // TS
NB = S // BLK
EPS = 1e-6
ROPE_THETA = 10000.0
NEG = -1e30
BF = jnp.bfloat16
F32 = jnp.float32

ADAM_LR = 0.001
ADAM_B1 = 0.9
ADAM_B2 = 0.999
ADAM_EPS = 1e-08
ADAM_WD = 0.01
ADAM_STEP = 10

VMEM_LIMIT = 56 * 1024 * 1024
BIG_VMEM_LIMIT = 62 * 1024 * 1024
MESH = pl.DeviceIdType.MESH

_VMEM = pl.BlockSpec(memory_space=pltpu.VMEM)
_SMEM = pl.BlockSpec(memory_space=pltpu.SMEM)
_ANY = pl.BlockSpec(memory_space=pl.ANY)


def _params(grid=False):
    if grid:
        return pltpu.CompilerParams(dimension_semantics=("arbitrary",), vmem_limit_bytes=VMEM_LIMIT)
    return pltpu.CompilerParams(vmem_limit_bytes=VMEM_LIMIT)


def _row(i):
    return (i, 0)


def _const(i):
    return (0, 0)


def _sigmoid(t):
    return 1.0 / (1.0 + jnp.exp(-t))


def _silu_and_grad(t):
    sg = _sigmoid(t)
    return t * sg, sg * (1.0 + t * (1.0 - sg))


def _group_mean(t, b_ref):
    hi = t.astype(BF)
    lo = (t - hi.astype(F32)).astype(BF)
    b = b_ref[...]
    return jnp.dot(hi, b, preferred_element_type=F32) + jnp.dot(lo, b, preferred_element_type=F32)


def _partner(t):
    w = t.shape[-1]
    lane = lax.broadcasted_iota(jnp.int32, t.shape, 1)
    first = (lane & 32) == 0
    return jnp.where(first, pltpu.roll(t, w - 32, 1), pltpu.roll(t, 32, 1))


def _norm_rope_fwd(t, w_t, cos, sin, b_ref):
    r = lax.rsqrt(_group_mean(t * t, b_ref) + EPS)
    tn = t * r * w_t
    return tn * cos + _partner(tn) * sin


def _norm_rope_bwd(d_out, t, w_t, cos, sin, b_ref):
    d_tn = d_out * cos + _partner(d_out * sin)
    r = lax.rsqrt(_group_mean(t * t, b_ref) + EPS)
    th = t * r
    g_w = jnp.sum(d_tn * th, axis=0, keepdims=True)
    d_th = d_tn * w_t
    d_t = r * (d_th - th * _group_mean(d_th * th, b_ref))
    return d_t, g_w


def _mesh_pos():
    return lax.axis_index("x"), lax.axis_index("y"), lax.axis_index("c")


def _ag_copy(chan, k, block, to):
    blk, send_sems, recv_sems = chan
    ref = blk(*block)
    return pltpu.make_async_remote_copy(src_ref=ref, dst_ref=ref, send_sem=send_sems.at[k],
                                        recv_sem=recv_sems.at[k], device_id=to, device_id_type=MESH)


def _ag_start(chan, pos):
    x, y, c = pos
    me = (x, y, c)
    chips = [(1 - x, y), (x, 1 - y), (1 - x, 1 - y)]
    first = [_ag_copy(chan, 0, me, (x, y, 1 - c))]
    first += [_ag_copy(chan, 1 + j, me, (*chip, c)) for j, chip in enumerate(chips)]
    for cp in first:
        cp.start()
    return first


def _ag_finish(chan, pos, first):
    x, y, c = pos
    me = (x, y, c)
    sibling = (x, y, 1 - c)
    chips = [(1 - x, y), (x, 1 - y), (1 - x, 1 - y)]
    passed = [_ag_copy(chan, 4 + j, (*chip, c), sibling) for j, chip in enumerate(chips)]
    for j, chip in enumerate(chips):
        _ag_copy(chan, 1 + j, (*chip, c), me).wait_recv()
        passed[j].start()
    _ag_copy(chan, 0, sibling, me).wait_recv()
    for j, chip in enumerate(chips):
        _ag_copy(chan, 4 + j, (*chip, 1 - c), me).wait_recv()
    for cp in first + passed:
        cp.wait_send()


def _slab(buf):
    return lambda px, py, pc: buf.at[4 * px + 2 * py + pc]


def _row_block(buf, rows, align):
    return lambda px, py, pc: buf.at[pl.ds(pl.multiple_of((4 * px + 2 * py + pc) * rows, align), rows), :]


HALF = INW // 2


def _gather_fwd_call(w_in_t, w_out_s, conv_w_s, c, w_ada_s, b_ada, x2, norm_w, qw_t, kw_t, cos_t, sin_t, bq, bk):
    def body(win_ref, wout_ref, cw_ref, c_ref, wada_ref, bada_ref, x_ref, nw_ref, qw_ref, kw_ref, cos_ref, sin_ref,
             bq_ref, bk_ref,
             wtf_hbm, woutf_hbm, cwf_ref, mod_ref, cact_ref,
             h_ref, qraw_ref, kraw_ref, ga_ref, a_ref, g_ref, gb_ref, qr_ref, kr_ref, vb_ref, z_ref,
             cw_buf, ca_buf, mp_buf, h_s, raw0, pt, wtf_ref, woutf_ref,
             s0, r0, s1, r1, s2, r2, s3, r3, s4, r4, out_sems):
        s = pl.program_id(0)
        pos = _mesh_pos()
        x, y, cc = pos
        me3 = (x, y, cc)
        me = 4 * x + 2 * y + cc
        sibling = (x, y, 1 - cc)
        chips = [(1 - x, y), (x, 1 - y), (1 - x, 1 - y)]
        ch_win = (_row_block(wtf_ref, IN_SHARD, 16), s0, r0)
        ch_wout = (_row_block(woutf_ref, OUT_SHARD, 16), s1, r1)
        ch_cw = (_slab(cw_buf), s2, r2)
        ch_ca = (_slab(ca_buf), s3, r3)
        ch_mp = (_slab(mp_buf), s4, r4)

        def first(chan, j):
            return _ag_copy(chan, j, me3, sibling if j == 0 else (*chips[j - 1], cc))

        def passed(chan, j):
            return _ag_copy(chan, 4 + j, (*chips[j], cc), sibling)

        def landed(chan, j):
            return _ag_copy(chan, 1 + j, (*chips[j], cc), me3)

        def relayed(chan, j):
            return _ag_copy(chan, 4 + j, (*chips[j], 1 - cc), me3)

        def from_sibling(chan):
            return _ag_copy(chan, 0, sibling, me3)

        @pl.when(s == 0)
        def _():
            cv = c_ref[...]
            ca_buf[me] = jnp.broadcast_to(cv * _sigmoid(cv), (8, D))
            f_ca = _ag_start(ch_ca, pos)
            wtf_ref[pl.ds(pl.multiple_of(me * IN_SHARD, 16), IN_SHARD), :] = win_ref[...].astype(BF)
            for j in range(3):
                first(ch_win, j).start()
            cw_buf[me] = cw_ref[:, 0, :]
            f_cw = _ag_start(ch_cw, pos)

            _ag_finish(ch_ca, pos, f_ca)
            cact_all = jnp.concatenate([ca_buf[d, 0:1, :] for d in range(NDEV)], axis=0)
            cact_ref[...] = cact_all
            col0 = pl.multiple_of(me * ADA_SHARD, 128)
            mp_buf[me] = jnp.dot(cact_all, wada_ref[...], preferred_element_type=F32,
                                 precision=lax.Precision.HIGHEST) + bada_ref[:, pl.ds(col0, ADA_SHARD)]
            f_mp = _ag_start(ch_mp, pos)
            _ag_finish(ch_cw, pos, f_cw)
            _ag_finish(ch_mp, pos, f_mp)
            for d in range(NDEV):
                cwf_ref[0:CK, CONV_SHARD * d:CONV_SHARD * (d + 1)] = cw_buf[d]
            cwf_ref[CK:CKP, :] = jnp.zeros((CKP - CK, CW), F32)
            mod_ref[...] = jnp.concatenate([mp_buf[d, pl.ds(me, 1), :] for d in range(NDEV)], axis=1)

            for j in (1, 0):
                landed(ch_win, j).wait_recv()
                passed(ch_win, j).start()
            from_sibling(ch_win).wait_recv()
            relayed(ch_win, 1).wait_recv()
            first(ch_win, 1).wait_send()
            first(ch_win, 2).wait_send()
            first(ch_win, 3).start()

        row0 = pl.multiple_of((s % NT) * TS, TS)

        @pl.when(s < NT)
        def _():
            xv = x_ref[...]
            shift = mod_ref[:, 0:D]
            scale = mod_ref[:, D:2 * D]
            r = lax.rsqrt(jnp.mean(xv * xv, axis=-1, keepdims=True) + EPS)
            hb = ((xv * r * nw_ref[...]) * (1.0 + scale) + shift).astype(BF)
            h_s[pl.ds(row0, TS), :] = hb
            w_half = wtf_ref[pl.ds(pl.multiple_of(x * HALF, 16), HALF), :]
            raw0[pl.ds(row0, TS), :] = lax.dot_general(hb, w_half, (((1,), (1,)), ((), ())),
                                                       preferred_element_type=F32)

        @pl.when(s == NT)
        def _():
            relayed(ch_win, 0).wait_recv()
            landed(ch_win, 2).wait_recv()
            passed(ch_win, 2).start()
            relayed(ch_win, 2).wait_recv()
            pltpu.make_async_copy(wtf_ref, wtf_hbm, out_sems.at[0]).start()
            woutf_ref[pl.ds(pl.multiple_of(me * OUT_SHARD, 16), OUT_SHARD), :] = wout_ref[...].astype(BF)
            for j in range(4):
                first(ch_wout, j).start()

        @pl.when(s >= NT)
        def _():
            hb = h_s[pl.ds(row0, TS), :]
            h_ref[...] = hb
            w_half = wtf_ref[pl.ds(pl.multiple_of((1 - x) * HALF, 16), HALF), :]
            raw1 = lax.dot_general(hb, w_half, (((1,), (1,)), ((), ())), preferred_element_type=F32)
            pt[:, pl.ds(pl.multiple_of(x * HALF, 128), HALF)] = raw0[pl.ds(row0, TS), :]
            pt[:, pl.ds(pl.multiple_of((1 - x) * HALF, 128), HALF)] = raw1
            cos = cos_ref[...]
            sin = sin_ref[...]
            q = pt[:, 0:512]
            qraw_ref[...] = q
            qr_ref[...] = _norm_rope_fwd(q, jnp.tile(qw_ref[...], (1, NQ)), jnp.tile(cos, (1, 4)),
                                         jnp.tile(sin, (1, 4)), bq_ref).astype(BF)
            k = pt[:, 512:640]
            kraw_ref[...] = k
            kr_ref[...] = _norm_rope_fwd(k, jnp.tile(kw_ref[...], (1, NKV)), cos, sin, bk_ref).astype(BF)
            vb_ref[...] = pt[:, 640:768].astype(BF)
            ga_ref[...] = pt[:, 768:1280]
            a = pt[:, 1280:1792]
            g = pt[:, 1792:2304]
            a_ref[...] = a
            g_ref[...] = g
            z_ref[...] = a * _sigmoid(g)
            gb_ref[...] = pt[:, 2304:2816]

        @pl.when(s == 2 * NT - 1)
        def _():
            for j in range(3):
                landed(ch_wout, j).wait_recv()
                passed(ch_wout, j).start()
            from_sibling(ch_wout).wait_recv()
            for j in range(3):
                relayed(ch_wout, j).wait_recv()
            out_copy = pltpu.make_async_copy(woutf_ref, woutf_hbm, out_sems.at[1])
            out_copy.start()
            pltpu.make_async_copy(wtf_ref, wtf_hbm, out_sems.at[0]).wait()
            out_copy.wait()
            first(ch_win, 0).wait_send()
            first(ch_win, 3).wait_send()
            for j in range(3):
                passed(ch_win, j).wait_send()
                passed(ch_wout, j).wait_send()
            for j in range(4):
                first(ch_wout, j).wait_send()

    early = lambda i: (jnp.minimum(i, NT - 1), 0)
    late = lambda i: (jnp.maximum(i - NT, 0), 0)
    t512 = pl.BlockSpec((TS, 512), late)
    t128 = pl.BlockSpec((TS, 128), late)
    sem = pltpu.SemaphoreType.DMA((7,))
    sds = jax.ShapeDtypeStruct
    return pl.pallas_call(
        body, name="gather_fwd", grid=(2 * NT,),
        out_shape=[sds((INW, D), BF), sds((D, D), BF), sds((CKP, CW), F32), sds((1, 3 * D), F32), sds((NDEV, D), F32),
                   sds((S, D), BF), sds((S, AW), F32), sds((S, KVW), F32), sds((S, AW), F32), sds((S, CW), F32),
                   sds((S, CW), F32), sds((S, CW), F32), sds((S, AW), BF), sds((S, KVW), BF), sds((S, KVW), BF),
                   sds((S, CW), F32)],
        in_specs=[_VMEM] * 6 + [pl.BlockSpec((TS, D), early), pl.BlockSpec((1, D), _const),
                                pl.BlockSpec((1, HD), _const), pl.BlockSpec((1, HD), _const), t128, t128,
                                pl.BlockSpec((AW, AW), _const), pl.BlockSpec((KVW, KVW), _const)],
        out_specs=[_ANY, _ANY] + [_VMEM] * 3 + [pl.BlockSpec((TS, D), late), t512, t128, t512, t512, t512, t512, t512,
                                                t128, t128, t512],
        scratch_shapes=[pltpu.VMEM((NDEV, CK, CONV_SHARD), F32), pltpu.VMEM((NDEV, 8, D), F32),
                        pltpu.VMEM((NDEV, 8, ADA_SHARD), F32), pltpu.VMEM((S, D), BF), pltpu.VMEM((S, HALF), F32),
                        pltpu.VMEM((TS, INW), F32), pltpu.VMEM((INW, D), BF), pltpu.VMEM((D, D), BF)]
        + [sem] * 10 + [pltpu.SemaphoreType.DMA((2,))],
        compiler_params=_params(True),
    )(w_in_t, w_out_s, conv_w_s, c, w_ada_s, b_ada, x2, norm_w, qw_t, kw_t, cos_t, sin_t, bq, bk)


QB = 2
NQB = NB // QB


def _band_masks(has_prev):
    kj = lax.broadcasted_iota(jnp.int32, (2 * BLK, 4 * BLK), 0)
    qi = lax.broadcasted_iota(jnp.int32, (2 * BLK, 4 * BLK), 1) & (BLK - 1)
    dist = qi + BLK - kj
    local = (dist >= 0) & (dist < BLK)
    return local & ((kj >= BLK) | has_prev), local


def _key_blocks(sb, prev_ref, cur_ref):
    prev = prev_ref[...] if sb == 0 else cur_ref[BLK * (sb - 1):BLK * sb, :]
    return prev, cur_ref[BLK * sb:BLK * (sb + 1), :]


def _sink_lanes(sink_ref, g):
    lane = lax.broadcasted_iota(jnp.int32, (1, 4 * BLK), 1)
    return jnp.where(lane < BLK, sink_ref[0, 4 * g],
                     jnp.where(lane < 2 * BLK, sink_ref[0, 4 * g + 1],
                               jnp.where(lane < 3 * BLK, sink_ref[0, 4 * g + 2], sink_ref[0, 4 * g + 3])))


def _unstack_t(t):
    return [t[:, BLK * h:BLK * (h + 1)].T for h in range(4)]


def _stack_heads(t, g):
    return jnp.concatenate([t[:, HD * (4 * g + h):HD * (4 * g + h + 1)] for h in range(4)], axis=0)


def _band(prev, cur, g):
    return jnp.concatenate([prev[:, HD * g:HD * (g + 1)], cur[:, HD * g:HD * (g + 1)]], axis=0)


def _softmax_band(qs, kb, mask, sink):
    s = lax.dot_general(kb, qs, (((1,), (1,)), ((), ())), preferred_element_type=F32) * (HD ** -0.5)
    s = jnp.where(mask, s, NEG)
    m = jnp.maximum(jnp.max(s, axis=0, keepdims=True), sink)
    e = jnp.exp(s - m)
    es = jnp.exp(sink - m)
    inv = 1.0 / (jnp.sum(e, axis=0, keepdims=True) + es)
    return e * inv, es * inv


HALO = 32


RC = 64
LC = 128


def _windows(ext_ref, r0, l0, base):
    col = ext_ref[pl.ds(r0, RC + HALO), pl.ds(l0, LC)]
    for s in range(8):
        rolled = col if s == 0 else pltpu.roll(col, RC + HALO - s, 0)
        for t in range(CK):
            if (base + t) % 8 == s:
                a8 = base + t - s
                yield t, rolled[a8:a8 + RC]


def _taps(ext_ref, r0, l0, base, cw_ref, flip):
    acc = None
    for t, win in _windows(ext_ref, r0, l0, base):
        k = CK - 1 - t if flip else t
        term = win * cw_ref[k:k + 1, pl.ds(l0, LC)]
        acc = term if acc is None else acc + term
    return acc


NSUB = (BLK // RC) * (CW // LC)


def _sub_tile(t, row_base):
    r0 = pl.multiple_of(row_base + (t // (CW // LC)) * RC, RC)
    l0 = pl.multiple_of((t % (CW // LC)) * LC, LC)
    return r0, l0


FR = QB * BLK


def _fwd_tail_call(sinks, qr, kr, vb, z, gb, cwf, conv_b, ln_w, ln_b, ga, x2, tgt, mod, w_out_full):
    def body(sink_ref, q_ref, kp_ref, kc_ref, vp_ref, vc_ref, z_ref, zh_ref, gb_ref, cw_ref, cb_ref, lw_ref, lb_ref,
             ga_ref, x_ref, t_ref, mod_ref, w_ref,
             o_ref, zc_ref, dout_ref, dya_ref, dyb_ref, gw_ref, loss_ref, dgate_ref,
             zext, yb_ref):
        i = pl.program_id(0)

        @pl.when(i == 0)
        def _():
            gw_ref[...] = jnp.zeros((D, D), F32)
            loss_ref[...] = jnp.zeros((1, 128), F32)
            dgate_ref[...] = jnp.zeros((1, D), F32)

        def out_proj(rows):
            gav = ga_ref[rows, :]
            ya = o_ref[rows, :] * (gav * _sigmoid(gav))
            ycat = jnp.concatenate([ya.astype(BF), yb_ref[rows, :]], axis=1)
            w = w_ref[...]
            y = jnp.dot(ycat, w, preferred_element_type=F32)
            gate = mod_ref[:, 2 * D:3 * D]
            diff = x_ref[rows, :] + gate * y - t_ref[rows, :]
            sq = jnp.sum(jnp.sum(diff * diff, axis=1, keepdims=True), axis=0, keepdims=True)
            loss_ref[...] += jnp.broadcast_to(sq, (1, 128))
            d_out = diff * (1.0 / D)
            dout_ref[rows, :] = d_out
            dgate_ref[...] += jnp.sum(d_out * y, axis=0, keepdims=True)
            dy = (d_out * gate).astype(BF)
            d_ycat = lax.dot_general(dy, w, (((1,), (1,)), ((), ())), preferred_element_type=F32)
            dya_ref[rows, :] = d_ycat[:, 0:AW]
            dyb_ref[rows, :] = d_ycat[:, AW:D]
            gw_ref[...] += lax.dot_general(ycat, dy, (((0,), (0,)), ((), ())), preferred_element_type=F32)

        zext[0:HALO, :] = jnp.where(i > 0, zh_ref[...], 0.0)
        zext[HALO:HALO + FR, :] = z_ref[...]
        masks = _band_masks(i > 0)
        for sb in range(QB):
            rows = slice(BLK * sb, BLK * (sb + 1))
            mask = masks[min(sb, 1)]
            q = q_ref[rows, :]
            kp, kc = _key_blocks(sb, kp_ref, kc_ref)
            vp, vc = _key_blocks(sb, vp_ref, vc_ref)
            for g in range(NKV):
                p, _ = _softmax_band(_stack_heads(q, g), _band(kp, kc, g), mask, _sink_lanes(sink_ref, g))
                o_t = lax.dot_general(_band(vp, vc, g), p.astype(BF), (((0,), (0,)), ((), ())),
                                      preferred_element_type=F32)
                for h, o_h in enumerate(_unstack_t(o_t)):
                    o_ref[rows, HD * (4 * g + h):HD * (4 * g + h + 1)] = o_h
            for r0 in range(BLK * sb, BLK * (sb + 1), RC):
                for l0 in range(0, CW, LC):
                    acc = _taps(zext, r0, l0, HALO - (CK - 1), cw_ref, False)
                    zc_ref[r0:r0 + RC, l0:l0 + LC] = acc + cb_ref[:, l0:l0 + LC]
            zc = zc_ref[rows, :]
            mu = jnp.mean(zc, axis=-1, keepdims=True)
            dz = zc - mu
            rstd = lax.rsqrt(jnp.mean(dz * dz, axis=-1, keepdims=True) + EPS)
            zn = dz * rstd * lw_ref[...] + lb_ref[...]
            gbv = gb_ref[rows, :]
            yb_ref[rows, :] = ((zn * _sigmoid(zn)) * (gbv * _sigmoid(gbv))).astype(BF)
            if (sb + 1) * BLK % TS == 0:
                out_proj(slice((sb + 1) * BLK - TS, (sb + 1) * BLK))

    prev = lambda i: (jnp.maximum(QB * i - 1, 0), 0)
    halo = lambda i: (jnp.maximum(FR // HALO * i - 1, 0), 0)
    f512 = pl.BlockSpec((FR, 512), _row)
    f128 = pl.BlockSpec((FR, KVW), _row)
    f1024 = pl.BlockSpec((FR, D), _row)
    c512 = pl.BlockSpec((1, CW), _const)
    sds = jax.ShapeDtypeStruct
    return pl.pallas_call(
        body, name="fwd_tail", grid=(NQB,),
        out_shape=[sds((S, AW), F32), sds((S, CW), F32), sds((S, D), F32), sds((S, AW), F32), sds((S, CW), F32),
                   sds((D, D), F32), sds((1, 128), F32), sds((1, D), F32)],
        in_specs=[_SMEM, f512, pl.BlockSpec((BLK, KVW), prev), f128, pl.BlockSpec((BLK, KVW), prev), f128,
                  f512, pl.BlockSpec((HALO, CW), halo), f512, pl.BlockSpec((CKP, CW), _const), c512, c512, c512,
                  f512, f1024, f1024, pl.BlockSpec((1, 3 * D), _const),
                  pl.BlockSpec((D, D), _const, pipeline_mode=pl.Buffered(1))],
        out_specs=[f512, f512, f1024, f512, f512, _VMEM, _VMEM, _VMEM],
        scratch_shapes=[pltpu.VMEM((FR + HALO, CW), F32), pltpu.VMEM((FR, CW), BF)],
        compiler_params=_params(True),
    )(sinks, qr, kr, kr, vb, vb, z, z, gb, cwf, conv_b, ln_w, ln_b, ga, x2, tgt, mod, w_out_full)


def _ln_gate_bwd(d_yb, zc, gbv, lw, lb):
    mu = jnp.mean(zc, axis=-1, keepdims=True)
    dz = zc - mu
    rstd = lax.rsqrt(jnp.mean(dz * dz, axis=-1, keepdims=True) + EPS)
    zh = dz * rstd
    zn = zh * lw + lb
    act_n, dact_n = _silu_and_grad(zn)
    act_g, dact_g = _silu_and_grad(gbv)
    d_gb = d_yb * act_n * dact_g
    d_zn = d_yb * act_g * dact_n
    dzh = d_zn * lw
    d_zc = rstd * (dzh - jnp.mean(dzh, axis=-1, keepdims=True) - zh * jnp.mean(dzh * zh, axis=-1, keepdims=True))
    return d_zc, d_gb, d_zn, zh


def _attn_conv_bwd_call(sinks, qr, kr, vb, d_ya, ga, o, qraw, qw_t, cos_t, sin_t, bq, d_yb, zc, gb, z, a, g, cwf,
                        ln_w, ln_b):
    def body(sink_ref, q_ref, kp_ref, kc_ref, vp_ref, vc_ref, dya_ref, ga_ref, o_ref, qraw_ref, qw_ref,
             cos_ref, sin_ref, bq_ref,
             dyb_ref, dybn_ref, zc_ref, zcn_ref, gb_ref, gbn_ref, z_ref, zh_ref, a_ref, g_ref, cw_ref, lw_ref, lb_ref,
             dqraw_ref, dga_ref, dk_ref, dv_ref, gqw_ref, gsink_ref,
             dgb_ref, da_ref, dg_ref, gcw_ref, glw_ref, glb_ref, gcb_ref,
             dext, zext, gacc):
        i = pl.program_id(0)

        @pl.when(i == 0)
        def _():
            dk_ref[...] = jnp.zeros((S, KVW), F32)
            dv_ref[...] = jnp.zeros((S, KVW), F32)
            gqw_ref[...] = jnp.zeros((1, AW), F32)
            gsink_ref[...] = jnp.zeros((1, 128), F32)
            gacc[...] = jnp.zeros((CKP * 8, CW), F32)
            glw_ref[...] = jnp.zeros((1, CW), F32)
            glb_ref[...] = jnp.zeros((1, CW), F32)
            gcb_ref[...] = jnp.zeros((1, CW), F32)

        lw = lw_ref[...]
        lb = lb_ref[...]

        def ln_rows(rows):
            d_zc, d_gb, d_zn, zh = _ln_gate_bwd(dyb_ref[rows, :], zc_ref[rows, :], gb_ref[rows, :], lw, lb)
            dgb_ref[rows, :] = d_gb.astype(BF)
            glw_ref[...] += jnp.sum(d_zn * zh, axis=0, keepdims=True)
            glb_ref[...] += jnp.sum(d_zn, axis=0, keepdims=True)
            gcb_ref[...] += jnp.sum(d_zc, axis=0, keepdims=True)
            dext[rows, :] = d_zc

        ln_rows(slice(0, BLK))
        zext[0:HALO, :] = jnp.where(i > 0, zh_ref[...], 0.0)
        zext[HALO:HALO + FR, :] = z_ref[...]

        lane = lax.broadcasted_iota(jnp.int32, (1, 128), 1)
        gsink = jnp.zeros((1, 128), F32)
        dq_rows = []
        masks = _band_masks(i > 0)
        for sb in range(QB):
            rows = slice(BLK * sb, BLK * (sb + 1))
            if sb + 1 < QB:
                ln_rows(slice(BLK * (sb + 1), BLK * (sb + 2)))
            else:
                d_zc_next, _, _, _ = _ln_gate_bwd(dybn_ref[...], zcn_ref[...], gbn_ref[...], lw, lb)
                dext[FR:FR + HALO, :] = jnp.where(i < NQB - 1, d_zc_next, 0.0)
            mask = masks[min(sb, 1)]
            q = q_ref[rows, :]
            d_ya = dya_ref[rows, :]
            act, dact = _silu_and_grad(ga_ref[rows, :])
            dga_ref[rows, :] = (d_ya * o_ref[rows, :] * dact).astype(BF)
            d_o = (d_ya * act).astype(BF)
            kp, kc = _key_blocks(sb, kp_ref, kc_ref)
            vp, vc = _key_blocks(sb, vp_ref, vc_ref)
            dq_parts, dk_parts, dv_parts = [], [], []
            for gi in range(NKV):
                qs = _stack_heads(q, gi)
                kb = _band(kp, kc, gi)
                vbd = _band(vp, vc, gi)
                p, ps = _softmax_band(qs, kb, mask, _sink_lanes(sink_ref, gi))
                dos = _stack_heads(d_o, gi)
                dp = lax.dot_general(vbd, dos, (((1,), (1,)), ((), ())), preferred_element_type=F32)
                dr = jnp.sum(p * dp, axis=0, keepdims=True)
                ds = (p * (dp - dr) * (HD ** -0.5)).astype(BF)
                sink_term = ps * dr
                for h in range(4):
                    part = jnp.sum(sink_term[:, BLK * h:BLK * (h + 1)], axis=1, keepdims=True)
                    gsink = gsink - jnp.where(lane == 4 * gi + h, part, 0.0)
                dv_parts.append(jnp.dot(p.astype(BF), dos, preferred_element_type=F32))
                dk_parts.append(jnp.dot(ds, qs, preferred_element_type=F32))
                dq_t = lax.dot_general(kb, ds, (((0,), (0,)), ((), ())), preferred_element_type=F32)
                dq_parts.extend(_unstack_t(dq_t))
            dkb = jnp.concatenate(dk_parts, axis=1)
            dvb = jnp.concatenate(dv_parts, axis=1)
            blk = QB * i + sb
            r_prev = pl.multiple_of(jnp.maximum(blk - 1, 0) * BLK, BLK)
            r_cur = pl.multiple_of(blk * BLK, BLK)
            dk_ref[pl.ds(r_prev, BLK), :] += dkb[0:BLK]
            dv_ref[pl.ds(r_prev, BLK), :] += dvb[0:BLK]
            dk_ref[pl.ds(r_cur, BLK), :] += dkb[BLK:2 * BLK]
            dv_ref[pl.ds(r_cur, BLK), :] += dvb[BLK:2 * BLK]
            dq_rows.append(jnp.concatenate(dq_parts, axis=1))

            def taps_sub(t, carry, sb=sb):
                r0, l0 = _sub_tile(t, BLK * sb)
                here = (pl.ds(r0, RC), pl.ds(l0, LC))
                d_z = _taps(dext, r0, l0, 0, cw_ref, True)
                sg = _sigmoid(g_ref[here])
                da_ref[here] = (d_z * sg).astype(BF)
                dg_ref[here] = (d_z * a_ref[here] * sg * (1.0 - sg)).astype(BF)
                d_sub = dext[here]
                for k, win in _windows(zext, r0, l0, HALO - (CK - 1)):
                    prod = d_sub * win
                    part = prod[0:8]
                    for q8 in range(1, RC // 8):
                        part = part + prod[8 * q8:8 * q8 + 8]
                    gacc[8 * k:8 * k + 8, pl.ds(l0, LC)] += part
                return carry

            lax.fori_loop(0, NSUB, taps_sub, 0)
        gsink_ref[...] += gsink
        dq = jnp.concatenate(dq_rows, axis=0)
        dq_raw, g_qw = _norm_rope_bwd(dq, qraw_ref[...], jnp.tile(qw_ref[...], (1, NQ)),
                                      jnp.tile(cos_ref[...], (1, 4)), jnp.tile(sin_ref[...], (1, 4)), bq_ref)
        dqraw_ref[...] = dq_raw.astype(BF)
        gqw_ref[...] += g_qw

        @pl.when(i == NQB - 1)
        def _():
            for k in range(CK):
                gcw_ref[k:k + 1, :] = jnp.sum(gacc[8 * k:8 * k + 8, :], axis=0, keepdims=True)
            gcw_ref[CK:CKP, :] = jnp.zeros((CKP - CK, CW), F32)

    prev = lambda i: (jnp.maximum(QB * i - 1, 0), 0)
    halo_prev = lambda i: (jnp.maximum(FR // HALO * i - 1, 0), 0)
    halo_next = lambda i: (jnp.minimum(FR // HALO * (i + 1), S // HALO - 1), 0)
    f512 = pl.BlockSpec((FR, 512), _row)
    f128 = pl.BlockSpec((FR, 128), _row)
    hn = pl.BlockSpec((HALO, CW), halo_next)
    c512 = pl.BlockSpec((1, CW), _const)
    sds = jax.ShapeDtypeStruct
    vec = sds((1, CW), F32)
    return pl.pallas_call(
        body, name="attn_conv_bwd", grid=(NQB,),
        out_shape=[sds((S, AW), BF), sds((S, AW), BF), sds((S, KVW), F32), sds((S, KVW), F32), sds((1, AW), F32),
                   sds((1, 128), F32),
                   sds((S, CW), BF), sds((S, CW), BF), sds((S, CW), BF), sds((CKP, CW), F32), vec, vec, vec],
        in_specs=[_SMEM, f512, pl.BlockSpec((BLK, KVW), prev), f128, pl.BlockSpec((BLK, KVW), prev), f128,
                  f512, f512, f512, f512, pl.BlockSpec((1, HD), _const), f128, f128, pl.BlockSpec((AW, AW), _const),
                  f512, hn, f512, hn, f512, hn, f512, pl.BlockSpec((HALO, CW), halo_prev), f512, f512,
                  pl.BlockSpec((CKP, CW), _const), c512, c512],
        out_specs=[f512, f512, _VMEM, _VMEM, _VMEM, _VMEM, f512, f512, f512, _VMEM, _VMEM, _VMEM, _VMEM],
        scratch_shapes=[pltpu.VMEM((FR + HALO, CW), F32), pltpu.VMEM((FR + HALO, CW), F32),
                        pltpu.VMEM((CKP * 8, CW), F32)],
        compiler_params=_params(True),
    )(sinks, qr, kr, kr, vb, vb, d_ya, ga, o, qraw, qw_t, cos_t, sin_t, bq,
      d_yb, d_yb, zc, zc, gb, gb, z, z, a, g, cwf, ln_w, ln_b)


SM_ROWS = 8
PIECES = ((0, 512), (512, 640), (640, 768), (768, 1280), (1280, 1792), (1792, 2304), (2304, 2816))


def _bwd_in_call(dqraw, dk, dv, dga, da, dg, dgb, kraw, kw_t, cos_t, sin_t, bk, h, wt_full, x2, d_out, mod, norm_w,
                 gw_out, gcw, glw, glb, gcb, gqw, gsink, dgate, loss_p, cact_all):
    def body(dq_ref, dk_ref, dv_ref, dga_ref, da_ref, dg_ref, dgb_ref, kraw_ref, kw_ref, cos_ref, sin_ref, bk_ref,
             h_ref, wt_ref, x_ref, dout_ref, mod_ref, nw_ref, gwout_ref, gcw_ref, glw_ref, glb_ref, gcb_ref, gqw_ref,
             gsink_ref, dgate_ref, loss_ref, cact_ref,
             gx_ref, o_gwin, o_gwout, o_gwada, o_gbada, o_gnw, o_gqw, o_gkw, o_gsink, o_gcw, o_gcb, o_glw, o_glb,
             o_loss,
             acc, win_send, win_sib, win_ici, wout_send, wout_sib, wout_ici, sm_buf, cw_buf, dmod_all, vec_acc, gkw_acc,
             wi_ds, wi_dr, wi_is, wi_ir, wo_ds, wo_dr, wo_is, wo_ir, sm_s, sm_r, cw_s, cw_r):
        i = pl.program_id(0)
        pos = _mesh_pos()
        x, y, cc = pos
        me = 4 * x + 2 * y + cc

        def chip(j):
            return (1 - x if j & 1 else x, 1 - y if j & 2 else y)

        def rows_of(buf, px, py, pc, rows, align):
            return buf.at[pl.ds(pl.multiple_of((4 * px + 2 * py + pc) * rows, align), rows), :]

        bufs = {"in": (win_send, win_sib, win_ici, IN_SHARD, wi_ds, wi_dr, wi_is, wi_ir),
                "out": (wout_send, wout_sib, wout_ici, OUT_SHARD, wo_ds, wo_dr, wo_is, wo_ir)}

        def d2d_copy(j, which):
            send, sib, _, rows, ds_, dr_, _, _ = bufs[which]
            px, py = chip(j)
            return pltpu.make_async_remote_copy(src_ref=rows_of(send, px, py, 1 - cc, rows, 16), dst_ref=sib.at[j],
                                                send_sem=ds_.at[j], recv_sem=dr_.at[j], device_id=(x, y, 1 - cc),
                                                device_id_type=MESH)

        def ici_copy(j, which):
            send, _, ici, rows, _, _, is_, ir_ = bufs[which]
            px, py = chip(j)
            return pltpu.make_async_remote_copy(src_ref=rows_of(send, px, py, cc, rows, 16), dst_ref=ici.at[j - 1],
                                                send_sem=is_.at[j - 1], recv_sem=ir_.at[j - 1], device_id=(px, py, cc),
                                                device_id_type=MESH)

        def level2(which, partial_ref):
            send, sib, _, rows, _, _, _, _ = bufs[which]
            for j in range(1, 4):
                d2d_copy(j, which).wait_recv()
                px, py = chip(j)
                mine = rows_of(partial_ref, px, py, cc, rows, 8)[...]
                rows_of(send, px, py, cc, rows, 16)[...] = (mine + sib[j].astype(F32)).astype(BF)
                ici_copy(j, which).start()

        def finish(which, partial_ref):
            _, sib, ici, rows, _, _, _, _ = bufs[which]
            d2d_copy(0, which).wait_recv()
            total = rows_of(partial_ref, x, y, cc, rows, 8)[...] + sib[0].astype(F32)
            for j in range(1, 4):
                ici_copy(j, which).wait_recv()
                total = total + ici[j - 1].astype(F32)
            for j in range(4):
                d2d_copy(j, which).wait_send()
            for j in range(1, 4):
                ici_copy(j, which).wait_send()
            return total

        def dproj_pieces():
            dk_raw, g_kw = _norm_rope_bwd(dk_ref[...], kraw_ref[...], jnp.tile(kw_ref[...], (1, NKV)), cos_ref[...],
                                          sin_ref[...], bk_ref)
            return [dq_ref[...], dk_raw.astype(BF), dv_ref[...].astype(BF), dga_ref[...], da_ref[...], dg_ref[...],
                    dgb_ref[...]], g_kw

        @pl.when(i == 0)
        def _():
            acc[...] = jnp.zeros((INW, D), F32)
            vec_acc[...] = jnp.zeros((8, D), F32)
            gkw_acc[...] = jnp.zeros((1, KVW), F32)
            wout_send[...] = gwout_ref[...].astype(BF)
            for j in range(4):
                d2d_copy(j, "out").start()
            cw_buf[me] = gcw_ref[...]
            _ag_start((_slab(cw_buf), cw_s, cw_r), pos)

        @pl.when(i == 2)
        def _():
            level2("out", gwout_ref)

        @pl.when(i < NT)
        def _():
            pieces, g_kw = dproj_pieces()
            gkw_acc[...] += g_kw
            hv = h_ref[...]
            for (lo, hi), piece in zip(PIECES, pieces):
                acc[lo:hi, :] += lax.dot_general(piece, hv, (((0,), (0,)), ((), ())), preferred_element_type=F32)

        @pl.when(i == NT - 1)
        def _():
            for lo, hi in PIECES:
                win_send[lo:hi, :] = acc[lo:hi, :].astype(BF)
            for j in range(4):
                d2d_copy(j, "in").start()

        @pl.when(i == NT)
        def _():
            level2("in", acc)

        @pl.when(i >= NT)
        def _():
            pieces, _ = dproj_pieces()
            dproj = jnp.concatenate(pieces, axis=1)
            d_h = jnp.dot(dproj, wt_ref[...], preferred_element_type=F32)
            xv = x_ref[...]
            scale = mod_ref[:, D:2 * D]
            nw = nw_ref[...]
            r = lax.rsqrt(jnp.mean(xv * xv, axis=-1, keepdims=True) + EPS)
            xn = xv * r
            vec_acc[0:1, :] += jnp.sum(d_h, axis=0, keepdims=True)
            vec_acc[1:2, :] += jnp.sum(d_h * (xn * nw), axis=0, keepdims=True)
            d_u = d_h * (1.0 + scale)
            vec_acc[2:3, :] += jnp.sum(d_u * xn, axis=0, keepdims=True)
            d_xn = d_u * nw
            gx_ref[...] = dout_ref[...] + r * (d_xn - xn * jnp.mean(d_xn * xn, axis=-1, keepdims=True))

        @pl.when(i == 2 * NT - 1)
        def _():
            ch_cw = (_slab(cw_buf), cw_s, cw_r)
            z128 = jnp.zeros((1, 128), F32)
            row4 = jnp.concatenate([glw_ref[...], glb_ref[...]], axis=1)
            row5 = jnp.concatenate([gcb_ref[...], gqw_ref[...]], axis=1)
            row6 = jnp.concatenate([gkw_acc[...], gsink_ref[...], loss_ref[...]] + [z128] * 5, axis=1)
            sm_buf[me] = jnp.concatenate([vec_acc[0:2, :], dgate_ref[...], vec_acc[2:3, :], row4, row5, row6,
                                          jnp.zeros((1, D), F32)], axis=0)

            def sm_copy(k):
                peer = (1 - x if k & 4 else x, 1 - y if k & 2 else y, 1 - cc if k & 1 else cc)
                return pltpu.make_async_remote_copy(src_ref=sm_buf.at[me], dst_ref=sm_buf.at[me],
                                                    send_sem=sm_s.at[k - 1], recv_sem=sm_r.at[k - 1],
                                                    device_id=peer, device_id_type=MESH)

            for k in range(1, NDEV):
                sm_copy(k).start()
            _ag_finish(ch_cw, pos, [_ag_copy(ch_cw, 0, (x, y, cc), (x, y, 1 - cc))]
                       + [_ag_copy(ch_cw, 1 + j, (x, y, cc), (*chip(j + 1), cc)) for j in range(3)])
            for k in range(1, NDEV):
                sm_copy(k).wait_recv()
            tot = sm_buf[0]
            cw_tot = cw_buf[0]
            for d in range(1, NDEV):
                tot = tot + sm_buf[d]
                cw_tot = cw_tot + cw_buf[d]
            o_gbada[...] = jnp.concatenate([tot[0:1, :], tot[1:2, :], tot[2:3, :]], axis=1)
            o_gnw[...] = tot[3:4, :]
            o_glw[...] = tot[4:5, 0:CW]
            o_glb[...] = tot[4:5, CW:D]
            o_gcb[...] = tot[5:6, 0:CW]
            gq = tot[5:6, CW:CW + HD]
            for hh in range(1, NQ):
                gq = gq + tot[5:6, CW + HD * hh:CW + HD * (hh + 1)]
            o_gqw[...] = gq
            o_gkw[...] = tot[6:7, 0:HD] + tot[6:7, HD:2 * HD]
            o_gsink[...] = tot[6:7, 128:128 + NQ]
            o_loss[...] = tot[6:7, 256:384] * (0.5 / D)
            mine = jnp.zeros((CK, CONV_SHARD), F32)
            for d in range(NDEV):
                mine = mine + jnp.where(me == d, cw_tot[0:CK, CONV_SHARD * d:CONV_SHARD * (d + 1)], 0.0)
            for k in range(CK):
                o_gcw[k] = mine[k:k + 1, :]
            for d in range(NDEV):
                dmod_all[d:d + 1, :] = jnp.concatenate([sm_buf[d, 0:1, :], sm_buf[d, 1:2, :], sm_buf[d, 2:3, :]],
                                                       axis=1)
            col0 = pl.multiple_of(me * ADA_SHARD, 128)
            o_gwada[...] = lax.dot_general(cact_ref[...], dmod_all[:, pl.ds(col0, ADA_SHARD)], (((0,), (0,)), ((), ())),
                                           preferred_element_type=F32, precision=lax.Precision.HIGHEST)

            o_gwout[...] = finish("out", gwout_ref)
            o_gwin[...] = finish("in", acc)
            for k in range(1, NDEV):
                sm_copy(k).wait_send()

    half = lambda i: (i % NT, 0)
    late = lambda i: (jnp.maximum(i - NT, 0), 0)
    t512 = pl.BlockSpec((TS, 512), half)
    t128 = pl.BlockSpec((TS, 128), half)
    l1024 = pl.BlockSpec((TS, D), late)
    sem7 = pltpu.SemaphoreType.DMA((7,))
    sem4 = pltpu.SemaphoreType.DMA((4,))
    sem3 = pltpu.SemaphoreType.DMA((3,))
    sds = jax.ShapeDtypeStruct
    return pl.pallas_call(
        body, name="bwd_in", grid=(2 * NT,),
        out_shape=[sds((S, D), F32), sds((IN_SHARD, D), F32), sds((OUT_SHARD, D), F32), sds((D, ADA_SHARD), F32),
                   sds((1, 3 * D), F32), sds((1, D), F32), sds((1, HD), F32), sds((1, HD), F32), sds((1, NQ), F32),
                   sds((CK, 1, CONV_SHARD), F32), sds((1, CW), F32), sds((1, CW), F32), sds((1, CW), F32),
                   sds((1, 128), F32)],
        in_specs=[t512, t128, t128, t512, t512, t512, t512, t128, pl.BlockSpec((1, HD), _const), t128, t128,
                  pl.BlockSpec((KVW, KVW), _const), pl.BlockSpec((TS, D), half),
                  pl.BlockSpec((INW, D), _const, pipeline_mode=pl.Buffered(1)), l1024, l1024,
                  pl.BlockSpec((1, 3 * D), _const), pl.BlockSpec((1, D), _const)] + [_VMEM] * 10,
        out_specs=[l1024] + [_VMEM] * 13,
        scratch_shapes=[pltpu.VMEM((INW, D), F32), pltpu.VMEM((INW, D), BF), pltpu.VMEM((4, IN_SHARD, D), BF),
                        pltpu.VMEM((3, IN_SHARD, D), BF), pltpu.VMEM((D, D), BF), pltpu.VMEM((4, OUT_SHARD, D), BF),
                        pltpu.VMEM((3, OUT_SHARD, D), BF),
                        pltpu.VMEM((NDEV, SM_ROWS, D), F32), pltpu.VMEM((NDEV, CKP, CW), F32),
                        pltpu.VMEM((NDEV, 3 * D), F32), pltpu.VMEM((8, D), F32), pltpu.VMEM((1, KVW), F32)]
        + [sem4, sem4, sem3, sem3] * 2 + [sem7] * 4,
        compiler_params=pltpu.CompilerParams(dimension_semantics=("arbitrary",), vmem_limit_bytes=BIG_VMEM_LIMIT),
    )(dqraw, dk, dv, dga, da, dg, dgb, kraw, kw_t, cos_t, sin_t, bk, h, wt_full, x2, d_out, mod, norm_w,
      gw_out, gcw, glw, glb, gcb, gqw, gsink, dgate, loss_p, cact_all)


ADAM_STEPS = 4


def _adam_chunking(arr):
    if arr.ndim == 2 and arr.shape[0] % (8 * ADAM_STEPS) == 0:
        return "rows"
    if arr.ndim == 2 and arr.shape[1] % (128 * ADAM_STEPS) == 0:
        return "cols"
    return None


def _adam_call(ws, gs, ms, vs, grad_x, loss_v):
    n = len(ws)
    bc1 = 1.0 - ADAM_B1 ** ADAM_STEP
    bc2 = 1.0 - ADAM_B2 ** ADAM_STEP
    chunked = [_adam_chunking(w) for w in ws]

    def body(*refs):
        ins, outs = refs[:4 * n + 2], refs[4 * n + 2:]
        i = pl.program_id(0)

        def update(j):
            w, g, m, v = (ins[j][...], ins[n + j][...], ins[2 * n + j][...], ins[3 * n + j][...])
            m_new = ADAM_B1 * m + (1.0 - ADAM_B1) * g
            v_new = ADAM_B2 * v + (1.0 - ADAM_B2) * (g * g)
            m_hat = m_new / bc1
            v_hat = v_new / bc2
            outs[j][...] = g
            outs[n + j][...] = -ADAM_LR * (m_hat / (jnp.sqrt(v_hat) + ADAM_EPS) + ADAM_WD * w)
            outs[2 * n + j][...] = m_new
            outs[3 * n + j][...] = v_new

        for j in range(n):
            if chunked[j]:
                update(j)
        outs[4 * n][...] = ins[4 * n][...]

        @pl.when(i == 0)
        def _():
            for j in range(n):
                if not chunked[j]:
                    update(j)
            outs[4 * n + 1][...] = ins[4 * n + 1][...]

    def spec(arr, how):
        if how == "rows":
            return pl.BlockSpec((arr.shape[0] // ADAM_STEPS, arr.shape[1]), _row)
        if how == "cols":
            return pl.BlockSpec((arr.shape[0], arr.shape[1] // ADAM_STEPS), lambda i: (0, i))
        zeros = (0,) * arr.ndim
        return pl.BlockSpec(arr.shape, lambda i: zeros)

    par_specs = [spec(w, ch) for w, ch in zip(ws, chunked)]
    extra = [spec(grad_x, "rows"), spec(loss_v, None)]
    shapes = [jax.ShapeDtypeStruct(w.shape, F32) for w in ws]
    res = pl.pallas_call(
        body, name="adam", grid=(ADAM_STEPS,),
        out_shape=shapes * 4 + [jax.ShapeDtypeStruct(grad_x.shape, F32), jax.ShapeDtypeStruct(loss_v.shape, F32)],
        in_specs=par_specs * 4 + extra, out_specs=par_specs * 4 + extra,
        compiler_params=_params(True),
    )(*ws, *gs, *ms, *vs, grad_x, loss_v)
    return res[0:n], res[n:2 * n], res[2 * n:3 * n], res[3 * n:4 * n], res[4 * n], res[4 * n + 1]


def _rope_tables():
    inv = (np.float32(ROPE_THETA) ** (-np.arange(0, HD, 2, dtype=np.float32) / np.float32(HD))).astype(np.float32)
    ang = (np.arange(S, dtype=np.float32)[:, None] * inv[None, :]).astype(np.float32)
    cos, sin = np.cos(ang).astype(np.float32), np.sin(ang).astype(np.float32)
    cos64 = np.concatenate([cos, cos], axis=-1)
    sin64 = np.concatenate([-sin, sin], axis=-1)
    return jnp.asarray(np.tile(cos64, (1, 2))), jnp.asarray(np.tile(sin64, (1, 2)))


def _group_matrix(width):
    idx = np.arange(width) // HD
    return jnp.asarray(np.where(idx[:, None] == idx[None, :], 1.0 / HD, 0.0).astype(np.float32)).astype(BF)


def kernel(x, c, w_ada, b_ada, norm_w, w_in, q_norm_w, k_norm_w, sinks, conv_w, conv_b, ln_w, ln_b, w_out, loss_target, m_w_ada, m_b_ada, m_norm_w, m_w_in, m_q_norm_w, m_k_norm_w, m_sinks, m_conv_w, m_conv_b, m_ln_w, m_ln_b, m_w_out, v_w_ada, v_b_ada, v_norm_w, v_w_in, v_q_norm_w, v_k_norm_w, v_sinks, v_conv_w, v_conv_b, v_ln_w, v_ln_b, v_w_out):
    x2 = x[0]
    tgt = loss_target[0]
    cos_t, sin_t = _rope_tables()
    bq = _group_matrix(AW)
    bk = _group_matrix(KVW)
    qw_t, kw_t = q_norm_w, k_norm_w

    tr = lambda t: jnp.swapaxes(t[0], 0, 1)
    tc = lambda t: jnp.swapaxes(t, 0, 1)
    (wt_full, w_out_full, cwf, mod, cact_all, h, qraw, kraw, ga, a, g, gb, qr, kr, vb, z) = _gather_fwd_call(
        tr(w_in), w_out[0], tc(conv_w), c, w_ada[0], b_ada, x2, norm_w, qw_t, kw_t, cos_t, sin_t, bq, bk)
    o, zc, d_out, d_ya, d_yb, gw_out, loss_p, dgate = _fwd_tail_call(
        sinks, qr, kr, vb, z, gb, cwf, conv_b, ln_w, ln_b, ga, x2, tgt, mod, w_out_full)

    dqraw, dga, dk, dv, gqw, gsink, dgb, da, dg, gcw, glw, glb, gcb = _attn_conv_bwd_call(
        sinks, qr, kr, vb, d_ya, ga, o, qraw, qw_t, cos_t, sin_t, bq, d_yb, zc, gb, z, a, g, cwf, ln_w, ln_b)
    (grad_x, g_w_in_t, g_w_out, g_w_ada, g_b_ada, g_norm_w, g_qw, g_kw, g_sinks, g_conv_w, g_conv_b, g_ln_w, g_ln_b,
     loss_v) = _bwd_in_call(dqraw, dk, dv, dga, da, dg, dgb, kraw, kw_t, cos_t, sin_t, bk, h, wt_full, x2, d_out, mod,
                            norm_w, gw_out, gcw, glw, glb, gcb, gqw, gsink, dgate, loss_p, cact_all)

    ws = [w_ada[0], b_ada, norm_w, tr(w_in), q_norm_w, k_norm_w, sinks, tc(conv_w), conv_b, ln_w, ln_b, w_out[0]]
    gs = [g_w_ada, g_b_ada, g_norm_w, g_w_in_t, g_qw, g_kw, g_sinks, g_conv_w, g_conv_b, g_ln_w, g_ln_b, g_w_out]
    ms = [m_w_ada[0], m_b_ada, m_norm_w, tr(m_w_in), m_q_norm_w, m_k_norm_w, m_sinks, tc(m_conv_w), m_conv_b, m_ln_w,
          m_ln_b, m_w_out[0]]
    vs = [v_w_ada[0], v_b_ada, v_norm_w, tr(v_w_in), v_q_norm_w, v_k_norm_w, v_sinks, tc(v_conv_w), v_conv_b, v_ln_w,
          v_ln_b, v_w_out[0]]
    grads, deltas, new_m, new_v, grad_x, loss_v = _adam_call(ws, gs, ms, vs, grad_x, loss_v)
    shaped = [w_ada, b_ada, norm_w, w_in, q_norm_w, k_norm_w, sinks, conv_w, conv_b, ln_w, ln_b, w_out]
    W_IN_POS, CONV_W_POS = 3, 7

    def like(vals):
        vals = [jnp.swapaxes(v, 0, 1) if j in (W_IN_POS, CONV_W_POS) else v for j, v in enumerate(vals)]
        return [v.reshape(s.shape) for v, s in zip(vals, shaped)]

    return (loss_v[0, 0], grad_x[None], *like(grads), *like(deltas), *like(new_m), *like(new_v))
```
